```python
import jax, jax.numpy as jnp
from jax import lax
import numpy as np

D_MODEL = 1024
BATCH = 4
SEQ = 4096
DEPTH = 4

CTX_LEN = 256
GRID_W = 64

WIN_HEADS = 8
WIN_KV_HEADS = 2
WIN_GROUP = WIN_HEADS // WIN_KV_HEADS
WIN_HEAD_DIM = 64
WINDOW = 128
BAND_BLOCK = 128
MLA_HEADS = 8
MLA_Q_RANK = 384
MLA_KV_RANK = 256
MLA_NOPE_DIM = 64
MLA_ROPE_DIM = 32
MLA_V_DIM = 64
QUERY_BLOCK = 128
N_EXPERTS = 16
CAPACITY_FACTOR = 2
EXPERT_FF = 1024

ROPE_THETA = 10000.0
LN_EPS = 1e-5
RMS_EPS = 1e-6
NEG_INF = -1e30
ALPHA = (2 * DEPTH) ** 0.25
BETA = (8 * DEPTH) ** -0.25
N_BRANCH = 2
WIN_Q_W = WIN_HEADS * WIN_HEAD_DIM
WIN_KV_W = WIN_KV_HEADS * WIN_HEAD_DIM
MLA_OUT_W = MLA_HEADS * MLA_V_DIM
WIN_SCALE = WIN_HEAD_DIM ** -0.5
MLA_SCALE = (MLA_NOPE_DIM + MLA_ROPE_DIM) ** -0.5
KEY_SIZES = [WIN_KV_W, WIN_KV_W, MLA_KV_RANK, MLA_ROPE_DIM]
QUERY_SIZES = [WIN_Q_W, MLA_Q_RANK, N_BRANCH * D_MODEL]
KEY_COLS = sum(KEY_SIZES)
IN_COLS = KEY_COLS + sum(QUERY_SIZES)

kernel_name = "hybrid_dit_gqa_mla_ecmoe"


def split_cols(t, sizes):
    return jnp.split(t, np.cumsum(sizes)[:-1].tolist(), axis=-1)


def layer_norm(x, g, b):
    xf = x.astype(jnp.float32)
    mu = jnp.mean(xf, axis=-1, keepdims=True)
    var = jnp.mean(jnp.square(xf - mu), axis=-1, keepdims=True)
    return ((xf - mu) * lax.rsqrt(var + LN_EPS) * g.astype(jnp.float32) + b.astype(jnp.float32)).astype(x.dtype)


def rms_norm(x, g):
    xf = x.astype(jnp.float32)
    return (xf * lax.rsqrt(jnp.mean(jnp.square(xf), axis=-1, keepdims=True) + RMS_EPS) * g.astype(jnp.float32)).astype(x.dtype)


def adaln(cond, w, b):
    mod = jax.nn.silu(cond) @ w + b
    return jnp.split(mod[:, None, :], 6, axis=-1)


def modulate(x, shift, scale):
    return x * (1 + scale) + shift


def axial_angles(n, rot_dim):
    rows = n // GRID_W
    row = jnp.repeat(jnp.arange(rows, dtype=jnp.float32), GRID_W)
    col = jnp.tile(jnp.arange(GRID_W, dtype=jnp.float32), rows)
    n_freq = rot_dim // 4
    inv_freq = ROPE_THETA ** (-jnp.arange(n_freq, dtype=jnp.float32) / n_freq)
    ang = jnp.stack([row[:, None] * inv_freq, col[:, None] * inv_freq], axis=1)
    return jnp.cos(ang), jnp.sin(ang)


def apply_axial_rope(x, cos, sin):
    shp = x.shape
    xr = x.reshape(shp[:-1] + (2, 2, shp[-1] // 4))
    x1, x2 = xr[..., 0, :], xr[..., 1, :]
    bshape = (1, shp[1]) + (1,) * (x.ndim - 3) + cos.shape[1:]
    c = cos.reshape(bshape).astype(x.dtype)
    s = sin.reshape(bshape).astype(x.dtype)
    return jnp.stack([x1 * c - x2 * s, x1 * s + x2 * c], axis=-2).reshape(shp)


def sink_softmax(scores, sink):
    s = scores.astype(jnp.float32)
    sk = jnp.broadcast_to(sink.astype(jnp.float32), s.shape[:-1] + (1,))
    return jax.nn.softmax(jnp.concatenate([s, sk], axis=-1), axis=-1)[..., :-1]


def band_view(t, nb):
    B = t.shape[0]
    tp = jnp.pad(t, ((0, 0), (BAND_BLOCK, BAND_BLOCK), (0, 0), (0, 0)))
    tb = tp.reshape((B, nb + 2, BAND_BLOCK) + t.shape[2:])
    return jnp.concatenate([tb[:, :-2], tb[:, 1:-1], tb[:, 2:]], axis=2)


def window_attention_latent(q, k, v, k_c, v_c, sink):
    B, n = q.shape[:2]
    nb = n // BAND_BLOCK
    qb = q.reshape(B, nb, BAND_BLOCK, WIN_KV_HEADS, WIN_GROUP, WIN_HEAD_DIM)
    kw, vw = band_view(k, nb), band_view(v, nb)
    blk = jnp.arange(nb)
    key_pos = blk[:, None] * BAND_BLOCK - BAND_BLOCK + jnp.arange(3 * BAND_BLOCK)[None, :]
    q_pos = blk[:, None] * BAND_BLOCK + jnp.arange(BAND_BLOCK)[None, :]
    kp = key_pos[:, None, :]
    allowed = (jnp.abs(kp - q_pos[:, :, None]) <= WINDOW) & (kp >= 0) & (kp < n)
    s_loc = jnp.einsum('bnqkgd,bnskd->bnkgqs', qb, kw).astype(jnp.float32) * WIN_SCALE
    s_loc = jnp.where(allowed[None, :, None, None], s_loc, NEG_INF)
    s_ctx = jnp.einsum('bnqkgd,bckd->bnkgqc', qb, k_c).astype(jnp.float32) * WIN_SCALE
    p = sink_softmax(jnp.concatenate([s_loc, s_ctx], axis=-1), sink).astype(v.dtype)
    o = (jnp.einsum('bnkgqs,bnskd->bnqkgd', p[..., :3 * BAND_BLOCK], vw)
         + jnp.einsum('bnkgqc,bckd->bnqkgd', p[..., 3 * BAND_BLOCK:], v_c))
    return o.reshape(B, n, WIN_Q_W)


def window_attention_ctx(q_c, k_c, v_c, sink):
    B, Lc = q_c.shape[:2]
    s = jnp.einsum('bqkgd,bskd->bkgqs', q_c, k_c) * WIN_SCALE
    p = sink_softmax(s, sink).astype(v_c.dtype)
    return jnp.einsum('bkgqs,bskd->bqkgd', p, v_c).reshape(B, Lc, WIN_Q_W)


def mla_expand_q(cq, g_qn, w_uq):
    B, n = cq.shape[:2]
    q = (rms_norm(cq, g_qn) @ w_uq).reshape(B, n, MLA_HEADS, MLA_NOPE_DIM + MLA_ROPE_DIM)
    return q[..., :MLA_NOPE_DIM], q[..., MLA_NOPE_DIM:]


def mla_expand_kv(ckv, g_kvn, w_ukv):
    B, n = ckv.shape[:2]
    kv = (rms_norm(ckv, g_kvn) @ w_ukv).reshape(B, n, MLA_HEADS, MLA_NOPE_DIM + MLA_V_DIM)
    return kv[..., :MLA_NOPE_DIM], kv[..., MLA_NOPE_DIM:]


def mla_dense(qn, qr, kn, kr, v):
    s = jnp.einsum('bqhd,bkhd->bhqk', qn, kn) + jnp.einsum('bqhd,bkd->bhqk', qr, kr)
    p = jax.nn.softmax(s.astype(jnp.float32) * MLA_SCALE, axis=-1).astype(v.dtype)
    return jnp.einsum('bhqk,bkhd->bqhd', p, v)


def token_mixer(h_c, h_l, w_in, sink, g_qn, g_kvn, w_uq, w_ukv, w_oa, w_ob, w_out, rope_win, rope_mla, need_ctx):
    B, n, _ = h_l.shape
    Lc = h_c.shape[1]
    sink_g = sink.reshape(WIN_KV_HEADS, WIN_GROUP, 1, 1)
    p_l = h_l @ w_in
    ka_l, va_l, ckv_l, kr_l = split_cols(p_l[..., :KEY_COLS], KEY_SIZES)
    qa_l, cq_l, gt_l = split_cols(p_l[..., KEY_COLS:], QUERY_SIZES)
    p_c = h_c @ (w_in if need_ctx else w_in[:, :KEY_COLS])
    ka_c, va_c, ckv_c, kr_c = split_cols(p_c[..., :KEY_COLS], KEY_SIZES)

    ka_l = apply_axial_rope(ka_l.reshape(B, n, WIN_KV_HEADS, WIN_HEAD_DIM), *rope_win)
    va_l = va_l.reshape(B, n, WIN_KV_HEADS, WIN_HEAD_DIM)
    ka_c = ka_c.reshape(B, Lc, WIN_KV_HEADS, WIN_HEAD_DIM)
    va_c = va_c.reshape(B, Lc, WIN_KV_HEADS, WIN_HEAD_DIM)
    qa_l = apply_axial_rope(qa_l.reshape(B, n, WIN_HEADS, WIN_HEAD_DIM), *rope_win)
    qa_l = qa_l.reshape(B, n, WIN_KV_HEADS, WIN_GROUP, WIN_HEAD_DIM)
    oa_l = window_attention_latent(qa_l, ka_l, va_l, ka_c, va_c, sink_g)

    kn_l, v_l = mla_expand_kv(ckv_l, g_kvn, w_ukv)
    kn_c, v_c = mla_expand_kv(ckv_c, g_kvn, w_ukv)
    kr_l = apply_axial_rope(kr_l[:, :, None], *rope_mla)[:, :, 0]
    qn_l, qr_l = mla_expand_q(cq_l, g_qn, w_uq)
    qr_l = apply_axial_rope(qr_l, *rope_mla)
    kn_all = jnp.concatenate([kn_c, kn_l], axis=1)
    kr_all = jnp.concatenate([kr_c, kr_l], axis=1)
    v_all = jnp.concatenate([v_c, v_l], axis=1)
    nq = n // QUERY_BLOCK
    qn_b = jnp.moveaxis(qn_l.reshape(B, nq, QUERY_BLOCK, MLA_HEADS, MLA_NOPE_DIM), 1, 0)
    qr_b = jnp.moveaxis(qr_l.reshape(B, nq, QUERY_BLOCK, MLA_HEADS, MLA_ROPE_DIM), 1, 0)
    ob = lax.map(lambda blk: mla_dense(blk[0], blk[1], kn_all, kr_all, v_all), (qn_b, qr_b))
    ob_l = jnp.moveaxis(ob, 0, 1).reshape(B, n, MLA_OUT_W)

    def merge(o_a, o_b, gates):
        g_a, g_b = jnp.split(gates, N_BRANCH, axis=-1)
        return (jax.nn.sigmoid(g_a) * (o_a @ w_oa) + jax.nn.sigmoid(g_b) * (o_b @ w_ob)) @ w_out

    y_l = merge(oa_l, ob_l, gt_l)
    if not need_ctx:
        return None, y_l
    qa_c, cq_c, gt_c = split_cols(p_c[..., KEY_COLS:], QUERY_SIZES)
    qa_c = qa_c.reshape(B, Lc, WIN_KV_HEADS, WIN_GROUP, WIN_HEAD_DIM)
    oa_c = window_attention_ctx(qa_c, ka_c, va_c, sink_g)
    qn_c, qr_c = mla_expand_q(cq_c, g_qn, w_uq)
    ob_c = mla_dense(qn_c, qr_c, kn_c, kr_c, v_c).reshape(B, Lc, MLA_OUT_W)
    return merge(oa_c, ob_c, gt_c), y_l


def ec_moe(h, w_router, w_gate, w_up, w_down):
    B, N, D = h.shape
    cap = CAPACITY_FACTOR * N // N_EXPERTS
    aff = jax.nn.softmax((h @ w_router).astype(jnp.float32), axis=-1)
    top_aff, top_idx = lax.top_k(jnp.swapaxes(aff, 1, 2), cap)
    xg = jax.vmap(lambda hb, ib: hb[ib])(h, top_idx)
    a = jnp.einsum('becd,edf->becf', xg, w_gate)
    u = jnp.einsum('becd,edf->becf', xg, w_up)
    y = jnp.einsum('becf,efd->becd', jax.nn.silu(a) * u, w_down) * top_aff[..., None].astype(h.dtype)
    return jax.vmap(lambda yb, ib: jnp.zeros((N, D), h.dtype).at[ib.reshape(-1)].add(yb.reshape(-1, D)))(y, top_idx)


def setup_inputs(seed: int = 0) -> dict:
    key = jax.random.key(seed)
    ks = jax.random.split(key, 24)
    f32 = jnp.float32
    L, D = DEPTH, D_MODEL

    def nrm(k, shape, scale):
        return jax.random.normal(k, shape, f32) * scale

    def gain(k, shape):
        return 1.0 + 0.02 * jax.random.normal(k, shape, f32)

    return {
        "x": nrm(ks[0], (BATCH, SEQ, D), 1.0),
        "c": nrm(ks[1], (BATCH, D), 1.0),
        "ctx": nrm(ks[2], (BATCH, CTX_LEN, D), 1.0),
        "c_ctx": nrm(ks[3], (D,), 1.0),
        "w_ada": nrm(ks[4], (L, D, 6 * D), 0.5 * D ** -0.5),
        "b_ada": nrm(ks[5], (L, 6 * D), 0.02),
        "w_in": nrm(ks[6], (L, D, IN_COLS), D ** -0.5),
        "attn_sink": nrm(ks[7], (L, WIN_HEADS), 0.5),
        "mla_q_norm": gain(ks[8], (L, MLA_Q_RANK)),
        "mla_kv_norm": gain(ks[9], (L, MLA_KV_RANK)),
        "w_uq": nrm(ks[10], (L, MLA_Q_RANK, MLA_HEADS * (MLA_NOPE_DIM + MLA_ROPE_DIM)), MLA_Q_RANK ** -0.5),
        "w_ukv": nrm(ks[11], (L, MLA_KV_RANK, MLA_HEADS * (MLA_NOPE_DIM + MLA_V_DIM)), MLA_KV_RANK ** -0.5),
        "w_oa": nrm(ks[12], (L, WIN_Q_W, D), WIN_Q_W ** -0.5),
        "w_ob": nrm(ks[13], (L, MLA_OUT_W, D), MLA_OUT_W ** -0.5),
        "w_out": nrm(ks[14], (L, D, D), BETA * D ** -0.5),
        "ln1_g": gain(ks[15], (L, D)),
        "ln1_b": nrm(ks[16], (L, D), 0.02),
        "w_router": nrm(ks[17], (L, D, N_EXPERTS), D ** -0.5),
        "w_exp_gate": nrm(ks[18], (L, N_EXPERTS, D, EXPERT_FF), D ** -0.5),
        "w_exp_up": nrm(ks[19], (L, N_EXPERTS, D, EXPERT_FF), D ** -0.5),
        "w_exp_down": nrm(ks[20], (L, N_EXPERTS, EXPERT_FF, D), BETA * EXPERT_FF ** -0.5),
        "ln2_g": gain(ks[21], (L, D)),
        "ln2_b": nrm(ks[22], (L, D), 0.02),
    }


def reference(x, c, ctx, c_ctx, w_ada, b_ada, w_in, attn_sink, mla_q_norm, mla_kv_norm, w_uq, w_ukv,
              w_oa, w_ob, w_out, ln1_g, ln1_b, w_router, w_exp_gate, w_exp_up, w_exp_down, ln2_g, ln2_b):
    n = x.shape[1]
    rope_win = axial_angles(n, WIN_HEAD_DIM)
    rope_mla = axial_angles(n, MLA_ROPE_DIM)
    x_l, x_c = x, ctx
    for l in range(DEPTH):
        need_ctx = l < DEPTH - 1
        m_l = adaln(c, w_ada[l], b_ada[l])
        m_c = adaln(c_ctx[None], w_ada[l], b_ada[l])
        h_l = modulate(x_l, m_l[0], m_l[1])
        h_c = modulate(x_c, m_c[0], m_c[1])
        y_c, y_l = token_mixer(h_c, h_l, w_in[l], attn_sink[l], mla_q_norm[l], mla_kv_norm[l], w_uq[l], w_ukv[l],
                               w_oa[l], w_ob[l], w_out[l], rope_win, rope_mla, need_ctx)
        x_l = layer_norm(ALPHA * x_l + m_l[2] * y_l, ln1_g[l], ln1_b[l])
        h_l = modulate(x_l, m_l[3], m_l[4])
        x_l = layer_norm(ALPHA * x_l + m_l[5] * ec_moe(h_l, w_router[l], w_exp_gate[l], w_exp_up[l], w_exp_down[l]),
                         ln2_g[l], ln2_b[l])
        if need_ctx:
            x_c = layer_norm(ALPHA * x_c + m_c[2] * y_c, ln1_g[l], ln1_b[l])
            h_c = modulate(x_c, m_c[3], m_c[4])
            x_c = layer_norm(ALPHA * x_c + m_c[5] * ec_moe(h_c, w_router[l], w_exp_gate[l], w_exp_up[l], w_exp_down[l]),
                             ln2_g[l], ln2_b[l])
    return x_l
```

```python
import functools
import math

import jax
import jax.numpy as jnp
from jax import lax
from jax.experimental import pallas as pl
from jax.experimental.pallas import tpu as pltpu

D_MODEL = 1024
GRID_W = 64
WIN_HEADS = 8
WIN_KV_HEADS = 2
WIN_HEAD_DIM = 64
BAND = 128
MLA_HEADS = 8
MLA_Q_RANK = 384
MLA_KV_RANK = 256
MLA_NOPE = 64
MLA_ROPE = 32
MLA_V = 64
N_EXPERTS = 16
CAPACITY_FACTOR = 2
ROPE_THETA = 10000.0
LN_EPS = 1e-5
RMS_EPS = 1e-6
NEG_INF = -1e30
LOG2E = math.log2(math.e)
WIN_SCALE = WIN_HEAD_DIM ** -0.5
MLA_SCALE = (MLA_NOPE + MLA_ROPE) ** -0.5

LANES = 128
TOK_BLOCK = 256
VMEM_LIMIT = 56 * 1024 * 1024

BF16 = jnp.bfloat16
F32 = jnp.float32

SEG_KWIN = (0, 512)
SEG_VWIN = (512, 1024)
SEG_CKV = (1024, 1280)
SEG_KR = (1280, 1408)
SEG_QWIN = (1408, 1920)
SEG_CQ = (1920, 2304)
SEG_GATE = (2304, 4352)
IN_COLS_PAD = 4352


def _dot(a, b):
    return jnp.dot(a, b, preferred_element_type=F32)


def _dot_nt(a, b):
    return lax.dot_general(a, b, (((1,), (1,)), ((), ())), preferred_element_type=F32)


def _layer_norm(z, g, b):
    mu = jnp.mean(z, axis=-1, keepdims=True)
    zc = z - mu
    var = jnp.mean(zc * zc, axis=-1, keepdims=True)
    return zc * lax.rsqrt(var + LN_EPS) * g + b


def _rms_norm(x, g):
    return x * lax.rsqrt(jnp.mean(x * x, axis=-1, keepdims=True) + RMS_EPS) * g


def _rope_slab(x, cos, sin, half):
    lane = lax.broadcasted_iota(jnp.int32, x.shape, 1)
    partner = jnp.where((lane & half) == 0,
                        pltpu.roll(x, LANES - half, 1), pltpu.roll(x, half, 1))
    return x * cos + partner * sin


def _compiler_params(sem):
    return pltpu.CompilerParams(dimension_semantics=sem, vmem_limit_bytes=VMEM_LIMIT)


ADA_ROWS = 8
ADA_TN = 512


def _ada_kernel(n_rows, condt_ref, w_ref, b_ref, o_ref):
    ct = condt_ref[...]
    st = ct * jax.nn.sigmoid(ct)
    w = w_ref[0]
    rows = []
    for r in range(n_rows):
        rows.append(jnp.sum(w * st[:, r:r + 1], axis=0, keepdims=True) + b_ref[0])
    rows.append(jnp.zeros((ADA_ROWS - n_rows, w.shape[1]), F32))
    o_ref[0] = jnp.concatenate(rows, axis=0)


def _ada_call(cond_t, w_ada, b_ada, n_rows):
    L, D, N = w_ada.shape
    return pl.pallas_call(
        functools.partial(_ada_kernel, n_rows),
        grid=(L, N // ADA_TN),
        in_specs=[
            pl.BlockSpec((D, ADA_ROWS), lambda l, j: (0, 0)),
            pl.BlockSpec((1, D, ADA_TN), lambda l, j: (l, 0, j)),
            pl.BlockSpec((1, 1, ADA_TN), lambda l, j: (l, 0, j)),
        ],
        out_specs=pl.BlockSpec((1, ADA_ROWS, ADA_TN), lambda l, j: (l, 0, j)),
        out_shape=jax.ShapeDtypeStruct((L, ADA_ROWS, N), F32),
        compiler_params=_compiler_params(("parallel", "parallel")),
        name="adaln_mod",
    )(cond_t, w_ada, b_ada.reshape(L, 1, N))


def _proj_kernel(has_ln, alpha, *refs):
    if has_ln:
        (x_ref, moe_ref, g2_ref, b2_ref, m5_ref, sh_ref, sc_ref, tab_ref, win_ref, gkv_ref, gq_ref,
         wkv_ref, wq_ref, vones_ref,
         xres_ref, qwin_ref, kwin_ref, vwin_ref, qcat_ref, kcat_ref, vext_ref, sg_ref) = refs
        z = alpha * x_ref[0] + m5_ref[0] * moe_ref[0]
        x = _layer_norm(z, g2_ref[0], b2_ref[0])
        xres_ref[0] = x
    else:
        (x_ref, sh_ref, sc_ref, tab_ref, win_ref, gkv_ref, gq_ref, wkv_ref, wq_ref, vones_ref,
         qwin_ref, kwin_ref, vwin_ref, qcat_ref, kcat_ref, vext_ref, sg_ref) = refs
        x = x_ref[0]
    hb = (x * (1.0 + sc_ref[0]) + sh_ref[0]).astype(BF16)

    def seg(s):
        return _dot(hb, win_ref[0, :, s[0]:s[1]])

    def tab(i):
        return tab_ref[:, i * LANES:(i + 1) * LANES]

    cq_w, sq_w, ck_w, sk_w, cq_m, sq_m, ck_m, sk_m = (tab(i) for i in range(8))

    kw = seg(SEG_KWIN)
    kwin_ref[0] = jnp.concatenate(
        [_rope_slab(kw[:, i * LANES:(i + 1) * LANES], ck_w, sk_w, 16) for i in range(4)],
        axis=1).astype(BF16)
    vwin_ref[0] = (seg(SEG_VWIN) + vones_ref[:, 0:512]).astype(BF16)
    qw = seg(SEG_QWIN)
    qwin_ref[0] = jnp.concatenate(
        [_rope_slab(qw[:, i * LANES:(i + 1) * LANES], cq_w, sq_w, 16) for i in range(4)],
        axis=1).astype(BF16)

    ckv = _rms_norm(seg(SEG_CKV), gkv_ref[0]).astype(BF16)
    kv = _dot(ckv, wkv_ref[0])
    kr = _rope_slab(seg(SEG_KR), ck_m, sk_m, 8)
    cq = _rms_norm(seg(SEG_CQ), gq_ref[0]).astype(BF16)
    qm = _dot(cq, wq_ref[0])
    for h in range(MLA_HEADS):
        sl = slice(h * LANES, (h + 1) * LANES)
        kcat_ref[0, h] = (kv[:, sl] + kr).astype(BF16)
        vext_ref[0, h] = (kv[:, 1024 + h * LANES:1024 + (h + 1) * LANES]
                          + vones_ref[:, 512 + h * LANES:512 + (h + 1) * LANES]).astype(BF16)
        qcat_ref[0, h] = _rope_slab(qm[:, sl], cq_m, sq_m, 8).astype(BF16)

    sg_ref[0] = jax.nn.sigmoid(seg(SEG_GATE)).astype(BF16)


def _proj_call(layer, alpha, x_all, moe, ln2_g, ln2_b, mods, tab, w_in_p, g_kvn, g_qn, w_kv, w_q,
               vones, n_ctx_blocks):
    B, S, D = x_all.shape
    T = TOK_BLOCK
    has_ln = moe is not None
    L = w_in_p.shape[0]

    def mod_spec(k, lyr):
        def imap(b, i):
            row = jnp.where(i < n_ctx_blocks, B, b)
            return ((lyr * ADA_ROWS + row) * 6 + k, 0, 0)
        return pl.BlockSpec((1, 1, D), imap)

    tok = lambda w: pl.BlockSpec((1, T, w), lambda b, i: (b, i, 0))
    headed = pl.BlockSpec((1, MLA_HEADS, T, LANES), lambda b, i: (b, 0, i, 0))
    const = lambda shp: pl.BlockSpec(shp, lambda b, i: (layer,) + (0,) * (len(shp) - 1),
                                     pipeline_mode=pl.Buffered(1))

    in_specs, args = [tok(D)], [x_all]
    if has_ln:
        prev = pl.BlockSpec((1, 1, D), lambda b, i: (layer - 1, 0, 0))
        in_specs += [tok(D), prev, prev, mod_spec(5, layer - 1)]
        args += [moe, ln2_g.reshape(L, 1, D), ln2_b.reshape(L, 1, D), mods]
    in_specs += [
        mod_spec(0, layer), mod_spec(1, layer),
        pl.BlockSpec((T, 8 * LANES), lambda b, i: (i, 0)),
        const((1, D, IN_COLS_PAD)),
        const((1, 1, MLA_KV_RANK)), const((1, 1, MLA_Q_RANK)),
        const((1, MLA_KV_RANK, 2048)), const((1, MLA_Q_RANK, 1024)),
        pl.BlockSpec((1, 1536), lambda b, i: (0, 0)),
    ]
    args += [mods, mods, tab, w_in_p, g_kvn.reshape(L, 1, -1), g_qn.reshape(L, 1, -1), w_kv, w_q, vones]

    out_specs, out_shape = [], []
    if has_ln:
        out_specs.append(tok(D))
        out_shape.append(jax.ShapeDtypeStruct((B, S, D), F32))
    out_specs += [tok(512), tok(512), tok(512), headed, headed, headed, tok(2048)]
    out_shape += [jax.ShapeDtypeStruct((B, S, 512), BF16)] * 3
    out_shape += [jax.ShapeDtypeStruct((B, MLA_HEADS, S, LANES), BF16)] * 3
    out_shape += [jax.ShapeDtypeStruct((B, S, 2048), BF16)]

    outs = pl.pallas_call(
        functools.partial(_proj_kernel, has_ln, alpha),
        grid=(B, S // T),
        in_specs=in_specs, out_specs=out_specs, out_shape=out_shape,
        compiler_params=_compiler_params(("parallel", "parallel")),
        name="proj",
    )(*args)
    if has_ln:
        return outs[0], outs[1:]
    return x_all, outs


def _win_kernel(layer, n_ctx_blocks, n_blocks, sink_ref, q_ref, kc_ref, vc_ref,
                k0_ref, k1_ref, k2_ref, v0_ref, v1_ref, v2_ref, o_ref):
    i = pl.program_id(1)
    q = q_ref[0]
    R = 2 * BAND
    row = lax.broadcasted_iota(jnp.int32, (R, LANES), 0)
    col = lax.broadcasted_iota(jnp.int32, (R, LANES), 1)
    rq = jnp.where(row < BAND, row, row - BAND)
    lane_lo = col < 64

    def attend(sources):
        slabs = [None] * 4
        for g in range(WIN_KV_HEADS):
            qs = jnp.concatenate([q[:, (2 * g) * LANES:(2 * g + 1) * LANES],
                                  q[:, (2 * g + 1) * LANES:(2 * g + 2) * LANES]], axis=0)
            outs = []
            for par in range(2):
                c0 = (2 * g + par) * LANES
                sink_a = sink_ref[layer, 4 * g + par] * LOG2E
                sink_b = sink_ref[layer, 4 * g + 2 + par] * LOG2E
                sink_v = jnp.where(row[:, 0:1] < BAND, sink_a, sink_b)
                scores = []
                m = sink_v
                for (k_ref, _, mask) in sources:
                    s = _dot_nt(qs, k_ref[0, :, c0:c0 + LANES])
                    if mask is not None:
                        s = jnp.where(mask, s, NEG_INF)
                    scores.append(s)
                    m = jnp.maximum(m, jnp.max(s, axis=1, keepdims=True))
                acc = jnp.zeros((R, LANES), F32)
                for s, (_, v_ref, _) in zip(scores, sources):
                    p = jnp.exp2(s - m).astype(BF16)
                    acc = acc + _dot(p, v_ref[0, :, c0:c0 + LANES])
                den = (acc[:, 64:65] if par == 0 else acc[:, 0:1]) + jnp.exp2(sink_v - m)
                outs.append(acc / den)
            o = jnp.where(lane_lo, outs[0], outs[1])
            slabs[2 * g] = o[:BAND]
            slabs[2 * g + 1] = o[BAND:]
        o_ref[0] = jnp.concatenate(slabs, axis=1).astype(o_ref.dtype)

    @pl.when(i < n_ctx_blocks)
    def _():
        attend([(kc_ref, vc_ref, None)])

    @pl.when(i >= n_ctx_blocks)
    def _():
        off_first = jnp.where(i == n_ctx_blocks, 2 * LANES, 0)
        off_last = jnp.where(i == n_blocks - 1, 2 * LANES, 0)
        mask0 = col >= rq + off_first
        mask2 = col <= rq - off_last
        attend([(kc_ref, vc_ref, None), (k0_ref, v0_ref, mask0), (k1_ref, v1_ref, None),
                (k2_ref, v2_ref, mask2)])


def _win_call(layer, sink, qwin, kwin, vwin, n_ctx):
    B, S, _ = qwin.shape
    nb = S // BAND
    ncb = n_ctx // BAND
    blk = lambda off: pl.BlockSpec(
        (1, BAND, 512), lambda b, i: (b, jnp.clip(i + off, ncb, nb - 1), 0))
    ctx = pl.BlockSpec((1, n_ctx, 512), lambda b, i: (b, 0, 0))
    return pl.pallas_call(
        functools.partial(_win_kernel, layer, ncb, nb),
        grid=(B, nb),
        in_specs=[pl.BlockSpec(memory_space=pltpu.SMEM),
                  pl.BlockSpec((1, BAND, 512), lambda b, i: (b, i, 0)),
                  ctx, ctx, blk(-1), blk(0), blk(1), blk(-1), blk(0), blk(1)],
        out_specs=pl.BlockSpec((1, BAND, 512), lambda b, i: (b, i, 0)),
        out_shape=jax.ShapeDtypeStruct((B, S, 512), BF16),
        compiler_params=_compiler_params(("parallel", "parallel")),
        name="win_attn",
    )(sink, qwin, kwin, vwin, kwin, kwin, kwin, vwin, vwin, vwin)


MLA_KCHUNK = 256


def _mla_kernel(n_ctx, n_keys, q_ref, k_ref, v_ref, o_ref, s_ref):
    qi = pl.program_id(2)
    tq = q_ref.shape[2]
    lane = lax.broadcasted_iota(jnp.int32, (tq, LANES), 1)

    def run(nk):
        outs = []
        for hh in range(2):
            q = q_ref[0, hh]
            mrun = None
            for c in range(nk // MLA_KCHUNK):
                s = _dot_nt(q, k_ref[0, hh, c * MLA_KCHUNK:(c + 1) * MLA_KCHUNK, :])
                s_ref[hh, c] = s
                mc = jnp.maximum(s[:, :LANES], s[:, LANES:])
                mrun = mc if mrun is None else jnp.maximum(mrun, mc)
            m = jnp.max(mrun, axis=1, keepdims=True)
            acc = jnp.zeros((tq, LANES), F32)
            for c in range(nk // MLA_KCHUNK):
                p = jnp.exp2(s_ref[hh, c] - m).astype(BF16)
                acc = acc + _dot(p, v_ref[0, hh, c * MLA_KCHUNK:(c + 1) * MLA_KCHUNK, :])
            den = acc[:, 64:65] if hh == 0 else acc[:, 0:1]
            outs.append(acc / den)
        o_ref[0] = jnp.where(lane < 64, outs[0], outs[1]).astype(o_ref.dtype)

    @pl.when(qi == 0)
    def _():
        run(n_ctx)

    @pl.when(qi > 0)
    def _():
        run(n_keys)


def _mla_call(qcat, kcat, vext, n_ctx):
    B, H, S, _ = qcat.shape
    T = TOK_BLOCK
    assert n_ctx == T
    kv = pl.BlockSpec((1, 2, S, LANES), lambda b, hp, i: (b, hp, 0, 0))
    return pl.pallas_call(
        functools.partial(_mla_kernel, n_ctx, S),
        grid=(B, H // 2, S // T),
        in_specs=[pl.BlockSpec((1, 2, T, LANES), lambda b, hp, i: (b, hp, i, 0)), kv, kv],
        out_specs=pl.BlockSpec((1, T, LANES), lambda b, hp, i: (b, i, hp)),
        out_shape=jax.ShapeDtypeStruct((B, S, (H // 2) * LANES), BF16),
        scratch_shapes=[pltpu.VMEM((2, S // MLA_KCHUNK, T, MLA_KCHUNK), F32)],
        compiler_params=_compiler_params(("parallel", "parallel", "arbitrary")),
        name="mla_attn",
    )(qcat, kcat, vext)


def _merge_kernel(alpha, x_ref, oa_ref, ob_ref, sg_ref, woa_ref, wob_ref, wout_ref, g1_ref, b1_ref,
                  m2_ref, m3_ref, m4_ref, wr_ref, x1_ref, h2_ref, lg_ref):
    D = x_ref.shape[2]
    sg = sg_ref[0]
    t = (sg[:, :D].astype(F32) * _dot(oa_ref[0], woa_ref[0])
         + sg[:, D:].astype(F32) * _dot(ob_ref[0], wob_ref[0]))
    y = _dot(t.astype(BF16), wout_ref[0])
    x1 = _layer_norm(alpha * x_ref[0] + m2_ref[0] * y, g1_ref[0], b1_ref[0])
    x1_ref[0] = x1
    h2 = x1 * (1.0 + m4_ref[0]) + m3_ref[0]
    h2_ref[0] = h2
    lg = _dot(h2.astype(BF16), wr_ref[0])
    lg_ref[0] = lg.T[:N_EXPERTS]


def _merge_call(layer, alpha, x_res, oa, ob, sg, w_oa, w_ob, w_out, ln1_g, ln1_b, mods, w_r,
                n_ctx_blocks):
    B, S, D = x_res.shape
    T = TOK_BLOCK
    L = w_oa.shape[0]

    def mod_spec(k):
        def imap(b, i):
            row = jnp.where(i < n_ctx_blocks, B, b)
            return ((layer * ADA_ROWS + row) * 6 + k, 0, 0)
        return pl.BlockSpec((1, 1, D), imap)

    tok = lambda w: pl.BlockSpec((1, T, w), lambda b, i: (b, i, 0))
    const = lambda shp: pl.BlockSpec(shp, lambda b, i: (layer,) + (0,) * (len(shp) - 1),
                                     pipeline_mode=pl.Buffered(1))
    return pl.pallas_call(
        functools.partial(_merge_kernel, alpha),
        grid=(B, S // T),
        in_specs=[tok(D), tok(512), tok(512), tok(2048),
                  const((1, 512, D)), const((1, 512, D)), const((1, D, D)),
                  const((1, 1, D)), const((1, 1, D)),
                  mod_spec(2), mod_spec(3), mod_spec(4),
                  const((1, D, LANES))],
        out_specs=[tok(D), tok(D), pl.BlockSpec((1, N_EXPERTS, T), lambda b, i: (b, 0, i))],
        out_shape=[jax.ShapeDtypeStruct((B, S, D), F32), jax.ShapeDtypeStruct((B, S, D), F32),
                   jax.ShapeDtypeStruct((B, N_EXPERTS, S), F32)],
        compiler_params=_compiler_params(("parallel", "parallel")),
        name="merge",
    )(x_res, oa, ob, sg, w_oa, w_ob, w_out, ln1_g.reshape(L, 1, D), ln1_b.reshape(L, 1, D),
      mods, mods, mods, w_r)


def _cumsum_lanes(x):
    n = x.shape[1]
    xb = x.astype(BF16)
    r = lax.broadcasted_iota(jnp.int32, (LANES, LANES), 0)
    c = lax.broadcasted_iota(jnp.int32, (LANES, LANES), 1)
    tri = jnp.where(r <= c, 1.0, 0.0).astype(BF16)
    tb = lax.broadcasted_iota(jnp.int32, (n, LANES), 0) // LANES
    kb = lax.broadcasted_iota(jnp.int32, (n, LANES), 1)
    before = jnp.where(tb < kb, 1.0, 0.0).astype(BF16)
    off = _dot(xb, before)
    outs = []
    for k in range(n // LANES):
        outs.append(_dot(xb[:, k * LANES:(k + 1) * LANES], tri) + off[:, k:k + 1])
    return jnp.concatenate(outs, axis=1)


def _select_top(aff, cap):
    bits = pltpu.bitcast(aff, jnp.int32)
    thr = jnp.zeros((aff.shape[0], 1), jnp.int32)
    for bit in range(30, -1, -1):
        cand = thr | (1 << bit)
        cnt = jnp.sum(jnp.where(bits >= cand, 1.0, 0.0), axis=1, keepdims=True)
        thr = jnp.where(cnt >= cap, cand, thr)
    gt = bits > thr
    eq = jnp.where(bits == thr, 1.0, 0.0)
    need = cap - jnp.sum(jnp.where(gt, 1.0, 0.0), axis=1, keepdims=True)
    eq_rank = _cumsum_lanes(eq) - eq
    sel = jnp.where(jnp.logical_or(gt, jnp.logical_and(eq > 0.5, eq_rank < need)), 1.0, 0.0)
    return _cumsum_lanes(sel)


def _route_kernel(n_ctx, cap_c, cap_l, lg_ref, aff_ref, idx_ref, cc_ref, cl_ref):
    lg = lg_ref[0]
    m = jnp.max(lg, axis=0, keepdims=True)
    ex = jnp.exp(lg - m)
    aff = ex / jnp.sum(ex, axis=0, keepdims=True)
    aff_ref[0] = aff
    cnt_c = _select_top(aff[:, :n_ctx], cap_c)
    cnt_l = _select_top(aff[:, n_ctx:], cap_l)
    for ex_i in range(N_EXPERTS):
        cc_ref[ex_i] = cnt_c[ex_i:ex_i + 1, :]
        cl_ref[ex_i] = cnt_l[ex_i:ex_i + 1, :]

    def slots(c_ref, e, n, j0, rows, base, out_row):
        jio = (lax.broadcasted_iota(jnp.int32, (rows, LANES), 0) + j0).astype(F32)
        acc = jnp.zeros((rows, LANES), F32)
        for k in range(n // LANES):
            ck = c_ref[e, :, k * LANES:(k + 1) * LANES]
            acc = acc + jnp.where(ck <= jio, 1.0, 0.0)
        cnt = jnp.sum(acc, axis=1, keepdims=True)
        idx_ref[0, e, out_row:out_row + rows, :] = cnt.astype(jnp.int32) + base

    def per_expert(e, carry):
        slots(cc_ref, e, n_ctx, 0, cap_c, 0, 0)
        n_lat = cl_ref.shape[2]
        for h in range(cap_l // 256):
            slots(cl_ref, e, n_lat, h * 256, 256, n_ctx, cap_c + h * 256)
        return carry

    lax.fori_loop(0, N_EXPERTS, per_expert, 0)


def _route_call(lg_t, n_ctx, cap_c, cap_l):
    B, E, S = lg_t.shape
    cap = cap_c + cap_l
    return pl.pallas_call(
        functools.partial(_route_kernel, n_ctx, cap_c, cap_l),
        grid=(B,),
        in_specs=[pl.BlockSpec((1, E, S), lambda b: (b, 0, 0))],
        out_specs=[pl.BlockSpec((1, E, S), lambda b: (b, 0, 0)),
                   pl.BlockSpec((1, E, cap, 1), lambda b: (b, 0, 0, 0))],
        out_shape=[jax.ShapeDtypeStruct((B, E, S), F32),
                   jax.ShapeDtypeStruct((B, E, cap, 1), jnp.int32)],
        scratch_shapes=[pltpu.VMEM((E, 1, n_ctx), F32), pltpu.VMEM((E, 1, S - n_ctx), F32)],
        compiler_params=_compiler_params(("parallel",)),
        name="route",
    )(lg_t)


SUBLANES = 8
GATHER_ROWS = 16


def _gather_kernel(idx_ref, h_ref, xg_ref):
    cap, D = xg_ref.shape[2], xg_ref.shape[3]
    sub = lax.broadcasted_iota(jnp.int32, (SUBLANES, D), 0)

    def body(g, carry):
        halves = []
        for hf in range(GATHER_ROWS // SUBLANES):
            acc = jnp.zeros((SUBLANES, D), F32)
            for r in range(SUBLANES):
                t = idx_ref[0, 0, g * GATHER_ROWS + hf * SUBLANES + r]
                tile = h_ref[0, t >> 3]
                acc = jnp.where(sub == r, pltpu.roll(tile, (r - (t & 7)) & 7, 0), acc)
            halves.append(acc)
        row0 = pl.multiple_of(g * GATHER_ROWS, GATHER_ROWS)
        xg_ref[0, 0, pl.ds(row0, GATHER_ROWS), :] = jnp.concatenate(halves, axis=0).astype(xg_ref.dtype)
        return carry

    lax.fori_loop(0, cap // GATHER_ROWS, body, 0)


def _gather_call(idx, h2):
    B, S, D = h2.shape
    E, cap = idx.shape[1], idx.shape[2]
    assert cap % GATHER_ROWS == 0 and S % SUBLANES == 0
    return pl.pallas_call(
        _gather_kernel,
        grid=(B, E),
        in_specs=[pl.BlockSpec((1, 1, cap), lambda b, e: (b * E + e, 0, 0), memory_space=pltpu.SMEM),
                  pl.BlockSpec((1, S // SUBLANES, SUBLANES, D), lambda b, e: (b, 0, 0, 0),
                               pipeline_mode=pl.Buffered(1))],
        out_specs=pl.BlockSpec((1, 1, cap, D), lambda b, e: (b, e, 0, 0)),
        out_shape=jax.ShapeDtypeStruct((B, E, cap, D), BF16),
        compiler_params=_compiler_params(("parallel", "arbitrary")),
        name="moe_gather",
    )(idx.reshape(B * E, 1, cap), h2.reshape(B, S // SUBLANES, SUBLANES, D))


def _moe_kernel(idx_ref, aff_ref, xg_ref, wg_ref, wu_ref, wd_ref, out_ref, y_ref):
    e = pl.program_id(1)
    cap = y_ref.shape[0]

    @pl.when(e == 0)
    def _():
        out_ref[...] = jnp.zeros(out_ref.shape, out_ref.dtype)

    x = xg_ref[0, 0]
    a = _dot(x, wg_ref[0, 0])
    u = _dot(x, wu_ref[0, 0])
    hmid = (a * jax.nn.sigmoid(a) * u).astype(BF16)
    y_ref[...] = _dot(hmid, wd_ref[0, 0])
    sub = lax.broadcasted_iota(jnp.int32, (SUBLANES, y_ref.shape[1]), 0)

    def body(g, carry):
        ytile = y_ref[pl.ds(pl.multiple_of(g * SUBLANES, SUBLANES), SUBLANES), :]
        for r in range(SUBLANES):
            t = idx_ref[0, 0, g * SUBLANES + r]
            w = aff_ref[0, 0, t]
            ts = t & 7
            contrib = jnp.where(sub == ts, pltpu.roll(ytile, (ts - r) & 7, 0) * w, 0.0)
            out_ref[0, t >> 3] = out_ref[0, t >> 3] + contrib
        return carry

    lax.fori_loop(0, cap // SUBLANES, body, 0)


def _moe_call(layer, idx, aff, xg, w_gate, w_up, w_down, S):
    B, E, cap, D = xg.shape
    F = w_gate.shape[3]
    wspec = lambda shp: pl.BlockSpec(shp, lambda b, e: (layer, e, 0, 0))
    out = pl.pallas_call(
        _moe_kernel,
        grid=(B, E),
        in_specs=[pl.BlockSpec((1, 1, cap), lambda b, e: (b * E + e, 0, 0), memory_space=pltpu.SMEM),
                  pl.BlockSpec((1, 1, S), lambda b, e: (b * E + e, 0, 0), memory_space=pltpu.SMEM),
                  pl.BlockSpec((1, 1, cap, D), lambda b, e: (b, e, 0, 0)),
                  wspec((1, 1, D, F)), wspec((1, 1, D, F)), wspec((1, 1, F, D))],
        out_specs=pl.BlockSpec((1, S // SUBLANES, SUBLANES, D), lambda b, e: (b, 0, 0, 0),
                               pipeline_mode=pl.Buffered(1)),
        out_shape=jax.ShapeDtypeStruct((B, S // SUBLANES, SUBLANES, D), F32),
        scratch_shapes=[pltpu.VMEM((cap, D), F32)],
        compiler_params=_compiler_params(("parallel", "arbitrary")),
        name="moe_ffn",
    )(idx.reshape(B * E, 1, cap), aff.reshape(B * E, 1, S), xg, w_gate, w_up, w_down)
    return out.reshape(B, S, D)


def _final_kernel(alpha, x_ref, moe_ref, g_ref, b_ref, m5_ref, o_ref):
    o_ref[0] = _layer_norm(alpha * x_ref[0] + m5_ref[0] * moe_ref[0], g_ref[0], b_ref[0])


def _final_call(layer, alpha, x1, moe, ln2_g, ln2_b, mods, n_ctx):
    B, S, D = x1.shape
    T = TOK_BLOCK
    L = ln2_g.shape[0]
    ncb = n_ctx // T
    tok_in = pl.BlockSpec((1, T, D), lambda b, i: (b, i + ncb, 0))
    const = pl.BlockSpec((1, 1, D), lambda b, i: (layer, 0, 0))
    return pl.pallas_call(
        functools.partial(_final_kernel, alpha),
        grid=(B, (S - n_ctx) // T),
        in_specs=[tok_in, tok_in, const, const,
                  pl.BlockSpec((1, 1, D), lambda b, i: ((layer * ADA_ROWS + b) * 6 + 5, 0, 0))],
        out_specs=pl.BlockSpec((1, T, D), lambda b, i: (b, i, 0)),
        out_shape=jax.ShapeDtypeStruct((B, S - n_ctx, D), F32),
        compiler_params=_compiler_params(("parallel", "parallel")),
        name="final_norm",
    )(x1, moe, ln2_g.reshape(L, 1, D), ln2_b.reshape(L, 1, D), mods)


def _prep_weights(w_in, w_uq, w_ukv, w_router):
    L, D, _ = w_in.shape
    z64 = jnp.zeros((L, D, 64), w_in.dtype)
    ka, va = w_in[..., 0:128], w_in[..., 128:256]
    ckv, kr = w_in[..., 256:512], w_in[..., 512:544]
    qa, cq, gt = w_in[..., 544:1056], w_in[..., 1056:1440], w_in[..., 1440:3488]

    def quad(t):
        g0, g1 = t[..., :64], t[..., 64:]
        return jnp.concatenate([g0, z64, z64, g0, g1, z64, z64, g1], axis=-1)

    krs = jnp.concatenate([z64, kr, jnp.zeros((L, D, 32), w_in.dtype)], axis=-1)
    w_in_p = jnp.concatenate([quad(ka), quad(va), ckv, krs, qa, cq, gt], axis=-1).astype(BF16)

    kvr = w_ukv.reshape(L, MLA_KV_RANK, MLA_HEADS, MLA_NOPE + MLA_V)
    kn, vv = kvr[..., :MLA_NOPE], kvr[..., MLA_NOPE:]
    zk = jnp.zeros_like(kn)
    w_k = jnp.concatenate([kn, zk], axis=-1).reshape(L, MLA_KV_RANK, MLA_HEADS * LANES)
    v_even = jnp.concatenate([vv, zk], axis=-1)
    v_odd = jnp.concatenate([zk, vv], axis=-1)
    odd = (jnp.arange(MLA_HEADS) % 2 == 1)[None, None, :, None]
    w_v = jnp.where(odd, v_odd, v_even).reshape(L, MLA_KV_RANK, MLA_HEADS * LANES)
    w_kv = jnp.concatenate([w_k, w_v], axis=-1).astype(BF16)

    qr = w_uq.reshape(L, MLA_Q_RANK, MLA_HEADS, MLA_NOPE + MLA_ROPE)
    w_q = jnp.concatenate([qr, jnp.zeros((L, MLA_Q_RANK, MLA_HEADS, 32), w_uq.dtype)], axis=-1)
    w_q = w_q.reshape(L, MLA_Q_RANK, MLA_HEADS * LANES).astype(BF16)

    w_r = jnp.concatenate(
        [w_router, jnp.zeros((L, D, LANES - N_EXPERTS), w_router.dtype)], axis=-1).astype(BF16)
    return w_in_p, w_kv, w_q, w_r


def _ones_columns():
    lane = jnp.arange(LANES)
    even = (lane == 64).astype(F32)
    odd = (lane == 0).astype(F32)
    win = jnp.concatenate([even, odd, even, odd])
    mla = jnp.concatenate([even, odd] * (MLA_HEADS // 2))
    return jnp.concatenate([win, mla])[None, :]


def _rope_tables(n_ctx, n_lat):
    pos = jnp.arange(n_lat)
    rowp = (pos // GRID_W).astype(F32)
    colp = (pos % GRID_W).astype(F32)

    def pattern(rot_dim):
        nf = rot_dim // 4
        inv = ROPE_THETA ** (-jnp.arange(nf, dtype=F32) / nf)
        ar, ac = rowp[:, None] * inv, colp[:, None] * inv
        cos = jnp.concatenate([jnp.cos(ar), jnp.cos(ar), jnp.cos(ac), jnp.cos(ac)], axis=1)
        sin = jnp.concatenate([-jnp.sin(ar), jnp.sin(ar), -jnp.sin(ac), jnp.sin(ac)], axis=1)
        return cos, sin

    cw, sw = pattern(WIN_HEAD_DIM)
    cw, sw = jnp.tile(cw, (1, 2)), jnp.tile(sw, (1, 2))
    cm, sm = pattern(MLA_ROPE)
    one64, zero64 = jnp.ones((n_lat, 64), F32), jnp.zeros((n_lat, 64), F32)
    cm = jnp.concatenate([one64, cm, one64[:, :32]], axis=1)
    sm = jnp.concatenate([zero64, sm, zero64[:, :32]], axis=1)
    sq_w, sq_m = WIN_SCALE * LOG2E, MLA_SCALE * LOG2E
    lat = jnp.concatenate([cw * sq_w, sw * sq_w, cw, sw, cm * sq_m, sm * sq_m, cm, sm], axis=1)
    ones, zeros = jnp.ones((n_ctx, LANES), F32), jnp.zeros((n_ctx, LANES), F32)
    ctx = jnp.concatenate([ones * sq_w, zeros, ones, zeros, ones * sq_m, zeros, ones, zeros], axis=1)
    return jnp.concatenate([ctx, lat], axis=0)


def kernel(x, c, ctx, c_ctx, w_ada, b_ada, w_in, attn_sink, mla_q_norm, mla_kv_norm, w_uq, w_ukv,
           w_oa, w_ob, w_out, ln1_g, ln1_b, w_router, w_exp_gate, w_exp_up, w_exp_down, ln2_g, ln2_b):
    B, n_lat, D = x.shape
    n_ctx = ctx.shape[1]
    depth = w_in.shape[0]
    S = n_ctx + n_lat
    alpha = (2 * depth) ** 0.25
    assert D == D_MODEL and n_ctx == TOK_BLOCK and n_lat % TOK_BLOCK == 0 and B + 1 <= ADA_ROWS
    cap_c = CAPACITY_FACTOR * n_ctx // N_EXPERTS
    cap_l = CAPACITY_FACTOR * n_lat // N_EXPERTS
    assert cap_l % 256 == 0 and cap_c % 8 == 0
    ncb = n_ctx // TOK_BLOCK

    cond = jnp.concatenate([c, c_ctx[None], jnp.zeros((ADA_ROWS - B - 1, D), F32)], axis=0)
    mods = _ada_call(cond.T, w_ada, b_ada, B + 1).reshape(depth * ADA_ROWS * 6, 1, D)

    w_in_p, w_kv, w_q, w_r = _prep_weights(w_in, w_uq, w_ukv, w_router)
    w_oa_b, w_ob_b, w_out_b = w_oa.astype(BF16), w_ob.astype(BF16), w_out.astype(BF16)
    wg_b, wu_b, wd_b = w_exp_gate.astype(BF16), w_exp_up.astype(BF16), w_exp_down.astype(BF16)
    tab = _rope_tables(n_ctx, n_lat)
    vones = _ones_columns()

    x_all = jnp.concatenate([ctx, x], axis=1)
    moe = None
    for l in range(depth):
        x_res, (qwin, kwin, vwin, qcat, kcat, vext, sg) = _proj_call(
            l, alpha, x_all, moe, ln2_g, ln2_b, mods, tab, w_in_p, mla_kv_norm, mla_q_norm,
            w_kv, w_q, vones, ncb)
        oa = _win_call(l, attn_sink, qwin, kwin, vwin, n_ctx)
        ob = _mla_call(qcat, kcat, vext, n_ctx)
        x1, h2, lg_t = _merge_call(l, alpha, x_res, oa, ob, sg, w_oa_b, w_ob_b, w_out_b,
                                   ln1_g, ln1_b, mods, w_r, ncb)
        aff, idx4 = _route_call(lg_t, n_ctx, cap_c, cap_l)
        idx = idx4.reshape(B, N_EXPERTS, cap_c + cap_l)
        xg = _gather_call(idx, h2)
        moe = _moe_call(l, idx, aff, xg, wg_b, wu_b, wd_b, S)
        x_all = x1
    return _final_call(depth - 1, alpha, x_all, moe, ln2_g, ln2_b, mods, n_ctx)
```

```python
import functools
import math

import jax
import jax.numpy as jnp
from jax import lax
from jax.experimental import pallas as pl
from jax.experimental.pallas import tpu as pltpu

D_MODEL = 1024
GRID_W = 64
WIN_HEADS = 8
WIN_KV_HEADS = 2
WIN_HEAD_DIM = 64
BAND = 128
MLA_HEADS = 8
MLA_Q_RANK = 384
MLA_KV_RANK = 256
MLA_NOPE = 64
MLA_ROPE = 32
MLA_V = 64
N_EXPERTS = 16
CAPACITY_FACTOR = 2
ROPE_THETA = 10000.0
LN_EPS = 1e-5
RMS_EPS = 1e-6
NEG_INF = -1e30
LOG2E = math.log2(math.e)
WIN_SCALE = WIN_HEAD_DIM ** -0.5
MLA_SCALE = (MLA_NOPE + MLA_ROPE) ** -0.5

LANES = 128
TOK_BLOCK = 256
VMEM_LIMIT = 56 * 1024 * 1024

BF16 = jnp.bfloat16
F32 = jnp.float32

SEG_KWIN = (0, 128)
SEG_VWIN = (128, 640)
SEG_CKV = (640, 896)
SEG_KR = (896, 1024)
SEG_QWIN = (1024, 2048)
SEG_CQ = (2048, 2432)
SEG_GATE = (2432, 4480)
IN_COLS_PAD = 4480


def _dot(a, b):
    return jnp.dot(a, b, preferred_element_type=F32)


def _dot_nt(a, b):
    return lax.dot_general(a, b, (((1,), (1,)), ((), ())), preferred_element_type=F32)


def _layer_norm(z, g, b):
    mu = jnp.mean(z, axis=-1, keepdims=True)
    zc = z - mu
    var = jnp.mean(zc * zc, axis=-1, keepdims=True)
    return zc * lax.rsqrt(var + LN_EPS) * g + b


def _rms_norm(x, g):
    return x * lax.rsqrt(jnp.mean(x * x, axis=-1, keepdims=True) + RMS_EPS) * g


def _rope_slab(x, cos, sin, half):
    lane = lax.broadcasted_iota(jnp.int32, x.shape, 1)
    partner = jnp.where((lane & half) == 0,
                        pltpu.roll(x, LANES - half, 1), pltpu.roll(x, half, 1))
    return x * cos + partner * sin


SUBLANES = 8


def _rows_to_tiles(y):
    n = y.shape[0]
    y3 = pltpu.einshape("r(sl)->srl", y, s=SUBLANES)
    y4 = y3.reshape(SUBLANES, n // SUBLANES, SUBLANES, LANES)
    return jnp.transpose(y4, (1, 2, 0, 3)).reshape(n, SUBLANES, LANES)


def _tiles_to_rows(x3):
    n = x3.shape[0]
    x4 = x3.reshape(n // SUBLANES, SUBLANES, SUBLANES, LANES)
    xs = jnp.transpose(x4, (2, 0, 1, 3)).reshape(SUBLANES, n, LANES)
    return pltpu.einshape("srl->r(sl)", xs)


def _compiler_params(sem):
    return pltpu.CompilerParams(dimension_semantics=sem, vmem_limit_bytes=VMEM_LIMIT)


ADA_ROWS = 8
ADA_TN = 512


def _ada_kernel(n_rows, condt_ref, w_ref, b_ref, o_ref):
    ct = condt_ref[...]
    st = ct * jax.nn.sigmoid(ct)
    w = w_ref[0]
    rows = []
    for r in range(n_rows):
        rows.append(jnp.sum(w * st[:, r:r + 1], axis=0, keepdims=True) + b_ref[0])
    rows.append(jnp.zeros((ADA_ROWS - n_rows, w.shape[1]), F32))
    o_ref[0] = jnp.concatenate(rows, axis=0)


def _ada_call(cond_t, w_ada, b_ada, n_rows):
    L, D, N = w_ada.shape
    return pl.pallas_call(
        functools.partial(_ada_kernel, n_rows),
        grid=(L, N // ADA_TN),
        in_specs=[
            pl.BlockSpec((D, ADA_ROWS), lambda l, j: (0, 0)),
            pl.BlockSpec((1, D, ADA_TN), lambda l, j: (l, 0, j)),
            pl.BlockSpec((1, 1, ADA_TN), lambda l, j: (l, 0, j)),
        ],
        out_specs=pl.BlockSpec((1, ADA_ROWS, ADA_TN), lambda l, j: (l, 0, j)),
        out_shape=jax.ShapeDtypeStruct((L, ADA_ROWS, N), F32),
        compiler_params=_compiler_params(("parallel", "parallel")),
        name="adaln_mod",
    )(cond_t, w_ada, b_ada.reshape(L, 1, N))


def _proj_kernel(has_ln, alpha, *refs):
    if has_ln:
        (x_ref, moe_ref, g2_ref, b2_ref, m5_ref, sh_ref, sc_ref, tab_ref, win_ref, gkv_ref, gq_ref,
         wkv_ref, wq_ref, vones_ref,
         xres_ref, qwin_ref, kwin_ref, vwin_ref, qcat_ref, kcat_ref, vext_ref, sg_ref) = refs
        z = alpha * x_ref[0] + m5_ref[0] * _tiles_to_rows(moe_ref[0])
        x = _layer_norm(z, g2_ref[0], b2_ref[0])
        xres_ref[0] = x
    else:
        (x_ref, sh_ref, sc_ref, tab_ref, win_ref, gkv_ref, gq_ref, wkv_ref, wq_ref, vones_ref,
         qwin_ref, kwin_ref, vwin_ref, qcat_ref, kcat_ref, vext_ref, sg_ref) = refs
        x = x_ref[0]
    hb = (x * (1.0 + sc_ref[0]) + sh_ref[0]).astype(BF16)

    def seg(s):
        return _dot(hb, win_ref[0, :, s[0]:s[1]])

    def tab(i):
        return tab_ref[:, i * LANES:(i + 1) * LANES]

    cq_w, sq_w, ck_w, sk_w, cq_m, sq_m, ck_m, sk_m = (tab(i) for i in range(8))

    kwin_ref[0] = _rope_slab(seg(SEG_KWIN), ck_w, sk_w, 16).astype(BF16)
    vwin_ref[0] = (seg(SEG_VWIN) + vones_ref[:, 0:512]).astype(BF16)
    qw = seg(SEG_QWIN)
    qwin_ref[0] = jnp.concatenate(
        [_rope_slab(qw[:, i * LANES:(i + 1) * LANES], cq_w, sq_w, 16) for i in range(WIN_HEADS)],
        axis=1).astype(BF16)

    ckv = _rms_norm(seg(SEG_CKV), gkv_ref[0]).astype(BF16)
    kv = _dot(ckv, wkv_ref[0])
    kr = _rope_slab(seg(SEG_KR), ck_m, sk_m, 8)
    cq = _rms_norm(seg(SEG_CQ), gq_ref[0]).astype(BF16)
    qm = _dot(cq, wq_ref[0])
    for h in range(MLA_HEADS):
        sl = slice(h * LANES, (h + 1) * LANES)
        kcat_ref[0, h] = (kv[:, sl] + kr).astype(BF16)
        vext_ref[0, h] = (kv[:, 1024 + h * LANES:1024 + (h + 1) * LANES]
                          + vones_ref[:, 512 + h * LANES:512 + (h + 1) * LANES]).astype(BF16)
        qcat_ref[0, h] = _rope_slab(qm[:, sl], cq_m, sq_m, 8).astype(BF16)

    sg_ref[0] = jax.nn.sigmoid(seg(SEG_GATE)).astype(BF16)


def _proj_call(layer, alpha, x_all, moe, ln2_g, ln2_b, mods, tab, w_in_p, g_kvn, g_qn, w_kv, w_q,
               vones, n_ctx_blocks):
    B, S, D = x_all.shape
    T = TOK_BLOCK
    has_ln = moe is not None
    L = w_in_p.shape[0]

    def mod_spec(k, lyr):
        def imap(b, i):
            row = jnp.where(i < n_ctx_blocks, B, b)
            return ((lyr * ADA_ROWS + row) * 6 + k, 0, 0)
        return pl.BlockSpec((1, 1, D), imap)

    tok = lambda w: pl.BlockSpec((1, T, w), lambda b, i: (b, i, 0))
    headed = pl.BlockSpec((1, MLA_HEADS, T, LANES), lambda b, i: (b, 0, i, 0))
    const = lambda shp: pl.BlockSpec(shp, lambda b, i: (layer,) + (0,) * (len(shp) - 1),
                                     pipeline_mode=pl.Buffered(1))

    in_specs, args = [tok(D)], [x_all]
    if has_ln:
        prev = pl.BlockSpec((1, 1, D), lambda b, i: (layer - 1, 0, 0))
        tiled = pl.BlockSpec((1, T, SUBLANES, LANES), lambda b, i: (b, i, 0, 0))
        in_specs += [tiled, prev, prev, mod_spec(5, layer - 1)]
        args += [moe, ln2_g.reshape(L, 1, D), ln2_b.reshape(L, 1, D), mods]
    in_specs += [
        mod_spec(0, layer), mod_spec(1, layer),
        pl.BlockSpec((T, 8 * LANES), lambda b, i: (i, 0)),
        const((1, D, IN_COLS_PAD)),
        const((1, 1, MLA_KV_RANK)), const((1, 1, MLA_Q_RANK)),
        const((1, MLA_KV_RANK, 2048)), const((1, MLA_Q_RANK, 1024)),
        pl.BlockSpec((1, 1536), lambda b, i: (0, 0)),
    ]
    args += [mods, mods, tab, w_in_p, g_kvn.reshape(L, 1, -1), g_qn.reshape(L, 1, -1), w_kv, w_q, vones]

    out_specs, out_shape = [], []
    if has_ln:
        out_specs.append(tok(D))
        out_shape.append(jax.ShapeDtypeStruct((B, S, D), F32))
    out_specs += [tok(WIN_HEADS * LANES), tok(LANES), tok(512), headed, headed, headed, tok(2048)]
    out_shape += [jax.ShapeDtypeStruct((B, S, WIN_HEADS * LANES), BF16),
                  jax.ShapeDtypeStruct((B, S, LANES), BF16),
                  jax.ShapeDtypeStruct((B, S, 512), BF16)]
    out_shape += [jax.ShapeDtypeStruct((B, MLA_HEADS, S, LANES), BF16)] * 3
    out_shape += [jax.ShapeDtypeStruct((B, S, 2048), BF16)]

    outs = pl.pallas_call(
        functools.partial(_proj_kernel, has_ln, alpha),
        grid=(B, S // T),
        in_specs=in_specs, out_specs=out_specs, out_shape=out_shape,
        compiler_params=_compiler_params(("parallel", "parallel")),
        name="proj",
    )(*args)
    if has_ln:
        return outs[0], outs[1:]
    return x_all, outs


WIN_HEAD_ORDER = (0, 2, 1, 3, 4, 6, 5, 7)


def _win_kernel(layer, n_ctx, n_ctx_blocks, n_blocks, sink_ref, q_ref, kc_ref, vc_ref,
                k0_ref, k1_ref, k2_ref, v0_ref, v1_ref, v2_ref, o_ref, kall_ref, vall_ref, p_ref):
    i = pl.program_id(1)
    q = q_ref[0]
    q8 = jnp.concatenate([q[:, h * LANES:(h + 1) * LANES] for h in WIN_HEAD_ORDER], axis=0)
    row = lax.broadcasted_iota(jnp.int32, (BAND, LANES), 0)
    col = lax.broadcasted_iota(jnp.int32, (BAND, LANES), 1)
    lane_lo = col < 64
    kall_ref[0:n_ctx, :] = kc_ref[0]
    vall_ref[0:n_ctx, :] = vc_ref[0]

    def attend(nk, masks):
        s = _dot_nt(q8, kall_ref[0:nk, :])
        e_sink = []
        for c, h in enumerate(WIN_HEAD_ORDER):
            sc = s[c * BAND:(c + 1) * BAND, :]
            if masks is not None:
                b0 = n_ctx
                sc = jnp.concatenate(
                    [sc[:, :b0],
                     jnp.where(masks[0], sc[:, b0:b0 + BAND], NEG_INF),
                     sc[:, b0 + BAND:b0 + 2 * BAND],
                     jnp.where(masks[1], sc[:, b0 + 2 * BAND:], NEG_INF)], axis=1)
            sink = sink_ref[layer, h] * LOG2E
            m = jnp.maximum(jnp.max(sc, axis=1, keepdims=True), sink)
            p_ref[c * BAND:(c + 1) * BAND, 0:nk] = jnp.exp2(sc - m).astype(BF16)
            e_sink.append(jnp.exp2(sink - m))
        pairs = []
        for pi in range(4):
            acc = _dot(p_ref[pi * 2 * BAND:(pi + 1) * 2 * BAND, 0:nk],
                       vall_ref[0:nk, pi * LANES:(pi + 1) * LANES])
            es = jnp.concatenate([e_sink[2 * pi], e_sink[2 * pi + 1]], axis=0)
            den = (acc[:, 64:65] if pi % 2 == 0 else acc[:, 0:1]) + es
            pairs.append(acc / den)
        slabs = [jnp.where(lane_lo, pairs[0][:BAND], pairs[1][:BAND]),
                 jnp.where(lane_lo, pairs[0][BAND:], pairs[1][BAND:]),
                 jnp.where(lane_lo, pairs[2][:BAND], pairs[3][:BAND]),
                 jnp.where(lane_lo, pairs[2][BAND:], pairs[3][BAND:])]
        o_ref[0] = jnp.concatenate(slabs, axis=1).astype(o_ref.dtype)

    @pl.when(i < n_ctx_blocks)
    def _():
        attend(n_ctx, None)

    @pl.when(i >= n_ctx_blocks)
    def _():
        for t, (k_ref, v_ref) in enumerate(((k0_ref, v0_ref), (k1_ref, v1_ref), (k2_ref, v2_ref))):
            kall_ref[n_ctx + t * BAND:n_ctx + (t + 1) * BAND, :] = k_ref[0]
            vall_ref[n_ctx + t * BAND:n_ctx + (t + 1) * BAND, :] = v_ref[0]
        off_first = jnp.where(i == n_ctx_blocks, 2 * LANES, 0)
        off_last = jnp.where(i == n_blocks - 1, 2 * LANES, 0)
        attend(n_ctx + 3 * BAND, (col >= row + off_first, col <= row - off_last))


def _win_call(layer, sink, qwin, kwin, vwin, n_ctx):
    B, S, _ = qwin.shape
    nb = S // BAND
    ncb = n_ctx // BAND
    nk = n_ctx + 3 * BAND
    blk = lambda w, off: pl.BlockSpec(
        (1, BAND, w), lambda b, i: (b, jnp.clip(i + off, ncb, nb - 1), 0))
    ctx = lambda w: pl.BlockSpec((1, n_ctx, w), lambda b, i: (b, 0, 0))
    return pl.pallas_call(
        functools.partial(_win_kernel, layer, n_ctx, ncb, nb),
        grid=(B, nb),
        in_specs=[pl.BlockSpec(memory_space=pltpu.SMEM),
                  pl.BlockSpec((1, BAND, WIN_HEADS * LANES), lambda b, i: (b, i, 0)),
                  ctx(LANES), ctx(512), blk(LANES, -1), blk(LANES, 0), blk(LANES, 1),
                  blk(512, -1), blk(512, 0), blk(512, 1)],
        out_specs=pl.BlockSpec((1, BAND, 512), lambda b, i: (b, i, 0)),
        out_shape=jax.ShapeDtypeStruct((B, S, 512), BF16),
        scratch_shapes=[pltpu.VMEM((nk, LANES), BF16), pltpu.VMEM((nk, 512), BF16),
                        pltpu.VMEM((WIN_HEADS * BAND, nk), BF16)],
        compiler_params=_compiler_params(("parallel", "parallel")),
        name="win_attn",
    )(sink, qwin, kwin, vwin, kwin, kwin, kwin, vwin, vwin, vwin)


MLA_KCHUNK = 256


def _mla_kernel(n_ctx, n_keys, q_ref, k_ref, v_ref, o_ref, s_ref):
    qi = pl.program_id(2)
    tq = q_ref.shape[2]
    lane = lax.broadcasted_iota(jnp.int32, (tq, LANES), 1)

    def run(nk):
        outs = []
        for hh in range(2):
            q = q_ref[0, hh]
            mrun = None
            for c in range(nk // MLA_KCHUNK):
                s = _dot_nt(q, k_ref[0, hh, c * MLA_KCHUNK:(c + 1) * MLA_KCHUNK, :])
                s_ref[hh, c] = s
                mc = jnp.maximum(s[:, :LANES], s[:, LANES:])
                mrun = mc if mrun is None else jnp.maximum(mrun, mc)
            m = jnp.max(mrun, axis=1, keepdims=True)
            acc = jnp.zeros((tq, LANES), F32)
            for c in range(nk // MLA_KCHUNK):
                p = jnp.exp2(s_ref[hh, c] - m).astype(BF16)
                acc = acc + _dot(p, v_ref[0, hh, c * MLA_KCHUNK:(c + 1) * MLA_KCHUNK, :])
            den = acc[:, 64:65] if hh == 0 else acc[:, 0:1]
            outs.append(acc / den)
        o_ref[0] = jnp.where(lane < 64, outs[0], outs[1]).astype(o_ref.dtype)

    @pl.when(qi == 0)
    def _():
        run(n_ctx)

    @pl.when(qi > 0)
    def _():
        run(n_keys)


def _mla_call(qcat, kcat, vext, n_ctx):
    B, H, S, _ = qcat.shape
    T = TOK_BLOCK
    assert n_ctx == T
    kv = pl.BlockSpec((1, 2, S, LANES), lambda b, hp, i: (b, hp, 0, 0))
    return pl.pallas_call(
        functools.partial(_mla_kernel, n_ctx, S),
        grid=(B, H // 2, S // T),
        in_specs=[pl.BlockSpec((1, 2, T, LANES), lambda b, hp, i: (b, hp, i, 0)), kv, kv],
        out_specs=pl.BlockSpec((1, T, LANES), lambda b, hp, i: (b, i, hp)),
        out_shape=jax.ShapeDtypeStruct((B, S, (H // 2) * LANES), BF16),
        scratch_shapes=[pltpu.VMEM((2, S // MLA_KCHUNK, T, MLA_KCHUNK), F32)],
        compiler_params=_compiler_params(("parallel", "parallel", "arbitrary")),
        name="mla_attn",
    )(qcat, kcat, vext)


def _merge_kernel(alpha, x_ref, oa_ref, ob_ref, sg_ref, woa_ref, wob_ref, wout_ref, g1_ref, b1_ref,
                  m2_ref, m3_ref, m4_ref, wr_ref, x1_ref, h2_ref, lg_ref):
    D = x_ref.shape[2]
    sg = sg_ref[0]
    t = (sg[:, :D].astype(F32) * _dot(oa_ref[0], woa_ref[0])
         + sg[:, D:].astype(F32) * _dot(ob_ref[0], wob_ref[0]))
    y = _dot(t.astype(BF16), wout_ref[0])
    x1 = _layer_norm(alpha * x_ref[0] + m2_ref[0] * y, g1_ref[0], b1_ref[0])
    x1_ref[0] = x1
    h2 = x1 * (1.0 + m4_ref[0]) + m3_ref[0]
    h2_ref[0] = _rows_to_tiles(h2)
    lg = _dot(h2.astype(BF16), wr_ref[0])
    lg_ref[0] = lg.T[:N_EXPERTS]


def _merge_call(layer, alpha, x_res, oa, ob, sg, w_oa, w_ob, w_out, ln1_g, ln1_b, mods, w_r,
                n_ctx_blocks):
    B, S, D = x_res.shape
    T = TOK_BLOCK
    L = w_oa.shape[0]

    def mod_spec(k):
        def imap(b, i):
            row = jnp.where(i < n_ctx_blocks, B, b)
            return ((layer * ADA_ROWS + row) * 6 + k, 0, 0)
        return pl.BlockSpec((1, 1, D), imap)

    tok = lambda w: pl.BlockSpec((1, T, w), lambda b, i: (b, i, 0))
    const = lambda shp: pl.BlockSpec(shp, lambda b, i: (layer,) + (0,) * (len(shp) - 1),
                                     pipeline_mode=pl.Buffered(1))
    return pl.pallas_call(
        functools.partial(_merge_kernel, alpha),
        grid=(B, S // T),
        in_specs=[tok(D), tok(512), tok(512), tok(2048),
                  const((1, 512, D)), const((1, 512, D)), const((1, D, D)),
                  const((1, 1, D)), const((1, 1, D)),
                  mod_spec(2), mod_spec(3), mod_spec(4),
                  const((1, D, LANES))],
        out_specs=[tok(D), pl.BlockSpec((1, T, SUBLANES, LANES), lambda b, i: (b, i, 0, 0)),
                   pl.BlockSpec((1, N_EXPERTS, T), lambda b, i: (b, 0, i))],
        out_shape=[jax.ShapeDtypeStruct((B, S, D), F32),
                   jax.ShapeDtypeStruct((B, S, SUBLANES, LANES), F32),
                   jax.ShapeDtypeStruct((B, N_EXPERTS, S), F32)],
        compiler_params=_compiler_params(("parallel", "parallel")),
        name="merge",
    )(x_res, oa, ob, sg, w_oa, w_ob, w_out, ln1_g.reshape(L, 1, D), ln1_b.reshape(L, 1, D),
      mods, mods, mods, w_r)


def _cumsum_lanes(x):
    n = x.shape[1]
    xb = x.astype(BF16)
    r = lax.broadcasted_iota(jnp.int32, (LANES, LANES), 0)
    c = lax.broadcasted_iota(jnp.int32, (LANES, LANES), 1)
    tri = jnp.where(r <= c, 1.0, 0.0).astype(BF16)
    tb = lax.broadcasted_iota(jnp.int32, (n, LANES), 0) // LANES
    kb = lax.broadcasted_iota(jnp.int32, (n, LANES), 1)
    before = jnp.where(tb < kb, 1.0, 0.0).astype(BF16)
    off = _dot(xb, before)
    outs = []
    for k in range(n // LANES):
        outs.append(_dot(xb[:, k * LANES:(k + 1) * LANES], tri) + off[:, k:k + 1])
    return jnp.concatenate(outs, axis=1)


def _select_top(aff, cap):
    bits = pltpu.bitcast(aff, jnp.int32)
    thr = jnp.zeros((aff.shape[0], 1), jnp.int32)
    for bit in range(30, -1, -1):
        cand = thr | (1 << bit)
        cnt = jnp.sum(jnp.where(bits >= cand, 1.0, 0.0), axis=1, keepdims=True)
        thr = jnp.where(cnt >= cap, cand, thr)
    gt = bits > thr
    eq = jnp.where(bits == thr, 1.0, 0.0)
    need = cap - jnp.sum(jnp.where(gt, 1.0, 0.0), axis=1, keepdims=True)
    eq_rank = _cumsum_lanes(eq) - eq
    sel = jnp.where(jnp.logical_or(gt, jnp.logical_and(eq > 0.5, eq_rank < need)), 1.0, 0.0)
    return _cumsum_lanes(sel)


def _route_kernel(n_ctx, cap_c, cap_l, lg_ref, aff_ref, idx_ref, cc_ref, cl_ref):
    lg = lg_ref[0]
    m = jnp.max(lg, axis=0, keepdims=True)
    ex = jnp.exp(lg - m)
    aff = ex / jnp.sum(ex, axis=0, keepdims=True)
    aff_ref[0] = aff
    cnt_c = _select_top(aff[:, :n_ctx], cap_c)
    cnt_l = _select_top(aff[:, n_ctx:], cap_l)
    for ex_i in range(N_EXPERTS):
        cc_ref[ex_i] = cnt_c[ex_i:ex_i + 1, :]
        cl_ref[ex_i] = cnt_l[ex_i:ex_i + 1, :]

    def slots(c_ref, e, n, j0, rows, base, out_row):
        jio = (lax.broadcasted_iota(jnp.int32, (rows, LANES), 0) + j0).astype(F32)
        acc = jnp.zeros((rows, LANES), F32)
        for k in range(n // LANES):
            ck = c_ref[e, :, k * LANES:(k + 1) * LANES]
            acc = acc + jnp.where(ck <= jio, 1.0, 0.0)
        cnt = jnp.sum(acc, axis=1, keepdims=True)
        idx_ref[0, e, out_row:out_row + rows, :] = cnt.astype(jnp.int32) + base

    def per_expert(e, carry):
        slots(cc_ref, e, n_ctx, 0, cap_c, 0, 0)
        n_lat = cl_ref.shape[2]
        for h in range(cap_l // 256):
            slots(cl_ref, e, n_lat, h * 256, 256, n_ctx, cap_c + h * 256)
        return carry

    lax.fori_loop(0, N_EXPERTS, per_expert, 0)


def _route_call(lg_t, n_ctx, cap_c, cap_l):
    B, E, S = lg_t.shape
    cap = cap_c + cap_l
    return pl.pallas_call(
        functools.partial(_route_kernel, n_ctx, cap_c, cap_l),
        grid=(B,),
        in_specs=[pl.BlockSpec((1, E, S), lambda b: (b, 0, 0))],
        out_specs=[pl.BlockSpec((1, E, S), lambda b: (b, 0, 0)),
                   pl.BlockSpec((1, E, cap, 1), lambda b: (b, 0, 0, 0))],
        out_shape=[jax.ShapeDtypeStruct((B, E, S), F32),
                   jax.ShapeDtypeStruct((B, E, cap, 1), jnp.int32)],
        scratch_shapes=[pltpu.VMEM((E, 1, n_ctx), F32), pltpu.VMEM((E, 1, S - n_ctx), F32)],
        compiler_params=_compiler_params(("parallel",)),
        name="route",
    )(lg_t)


ROW_BATCH = 16


def _gather_kernel(idx_ref, h_ref, xg_ref, tmp_ref):
    cap = xg_ref.shape[2]

    def body(g, carry):
        for r in range(ROW_BATCH):
            j = g * ROW_BATCH + r
            tmp_ref[j] = h_ref[0, idx_ref[0, 0, j]]
        return carry

    lax.fori_loop(0, cap // ROW_BATCH, body, 0)
    xg_ref[0, 0] = _tiles_to_rows(tmp_ref[...]).astype(xg_ref.dtype)


def _gather_call(idx, h2t):
    B, S = h2t.shape[0], h2t.shape[1]
    D = SUBLANES * LANES
    E, cap = idx.shape[1], idx.shape[2]
    assert cap % ROW_BATCH == 0
    return pl.pallas_call(
        _gather_kernel,
        grid=(B, E),
        in_specs=[pl.BlockSpec((1, 1, cap), lambda b, e: (b * E + e, 0, 0), memory_space=pltpu.SMEM),
                  pl.BlockSpec((1, S, SUBLANES, LANES), lambda b, e: (b, 0, 0, 0),
                               pipeline_mode=pl.Buffered(1))],
        out_specs=pl.BlockSpec((1, 1, cap, D), lambda b, e: (b, e, 0, 0)),
        out_shape=jax.ShapeDtypeStruct((B, E, cap, D), BF16),
        scratch_shapes=[pltpu.VMEM((cap, SUBLANES, LANES), F32)],
        compiler_params=_compiler_params(("parallel", "arbitrary")),
        name="moe_gather",
    )(idx.reshape(B * E, 1, cap), h2t)


def _moe_kernel(idx_ref, aff_ref, xg_ref, wg_ref, wu_ref, wd_ref, out_ref, y_ref):
    e = pl.program_id(1)
    cap = y_ref.shape[0]

    @pl.when(e == 0)
    def _():
        out_ref[...] = jnp.zeros(out_ref.shape, out_ref.dtype)

    x = xg_ref[0, 0]
    a = _dot(x, wg_ref[0, 0])
    u = _dot(x, wu_ref[0, 0])
    hmid = (a * jax.nn.sigmoid(a) * u).astype(BF16)
    y_ref[...] = _rows_to_tiles(_dot(hmid, wd_ref[0, 0]))

    def body(g, carry):
        toks = [idx_ref[0, 0, g * ROW_BATCH + r] for r in range(ROW_BATCH)]
        old = [out_ref[0, t] for t in toks]
        for r, t in enumerate(toks):
            out_ref[0, t] = old[r] + y_ref[g * ROW_BATCH + r] * aff_ref[0, 0, t]
        return carry

    lax.fori_loop(0, cap // ROW_BATCH, body, 0)


def _moe_call(layer, idx, aff, xg, w_gate, w_up, w_down, S):
    B, E, cap, D = xg.shape
    F = w_gate.shape[3]
    wspec = lambda shp: pl.BlockSpec(shp, lambda b, e: (layer, e, 0, 0))
    return pl.pallas_call(
        _moe_kernel,
        grid=(B, E),
        in_specs=[pl.BlockSpec((1, 1, cap), lambda b, e: (b * E + e, 0, 0), memory_space=pltpu.SMEM),
                  pl.BlockSpec((1, 1, S), lambda b, e: (b * E + e, 0, 0), memory_space=pltpu.SMEM),
                  pl.BlockSpec((1, 1, cap, D), lambda b, e: (b, e, 0, 0)),
                  wspec((1, 1, D, F)), wspec((1, 1, D, F)), wspec((1, 1, F, D))],
        out_specs=pl.BlockSpec((1, S, SUBLANES, LANES), lambda b, e: (b, 0, 0, 0),
                               pipeline_mode=pl.Buffered(1)),
        out_shape=jax.ShapeDtypeStruct((B, S, SUBLANES, LANES), F32),
        scratch_shapes=[pltpu.VMEM((cap, SUBLANES, LANES), F32)],
        compiler_params=_compiler_params(("parallel", "arbitrary")),
        name="moe_ffn",
    )(idx.reshape(B * E, 1, cap), aff.reshape(B * E, 1, S), xg, w_gate, w_up, w_down)


def _final_kernel(alpha, x_ref, moe_ref, g_ref, b_ref, m5_ref, o_ref):
    o_ref[0] = _layer_norm(alpha * x_ref[0] + m5_ref[0] * _tiles_to_rows(moe_ref[0]),
                           g_ref[0], b_ref[0])


def _final_call(layer, alpha, x1, moe, ln2_g, ln2_b, mods, n_ctx):
    B, S, D = x1.shape
    T = TOK_BLOCK
    L = ln2_g.shape[0]
    ncb = n_ctx // T
    tok_in = pl.BlockSpec((1, T, D), lambda b, i: (b, i + ncb, 0))
    const = pl.BlockSpec((1, 1, D), lambda b, i: (layer, 0, 0))
    return pl.pallas_call(
        functools.partial(_final_kernel, alpha),
        grid=(B, (S - n_ctx) // T),
        in_specs=[tok_in, pl.BlockSpec((1, T, SUBLANES, LANES), lambda b, i: (b, i + ncb, 0, 0)),
                  const, const,
                  pl.BlockSpec((1, 1, D), lambda b, i: ((layer * ADA_ROWS + b) * 6 + 5, 0, 0))],
        out_specs=pl.BlockSpec((1, T, D), lambda b, i: (b, i, 0)),
        out_shape=jax.ShapeDtypeStruct((B, S - n_ctx, D), F32),
        compiler_params=_compiler_params(("parallel", "parallel")),
        name="final_norm",
    )(x1, moe, ln2_g.reshape(L, 1, D), ln2_b.reshape(L, 1, D), mods)


def _prep_weights(w_in, w_uq, w_ukv, w_router):
    L, D, _ = w_in.shape
    z64 = jnp.zeros((L, D, 64), w_in.dtype)
    ka, va = w_in[..., 0:128], w_in[..., 128:256]
    ckv, kr = w_in[..., 256:512], w_in[..., 512:544]
    qa, cq, gt = w_in[..., 544:1056], w_in[..., 1056:1440], w_in[..., 1440:3488]

    def quad(t):
        g0, g1 = t[..., :64], t[..., 64:]
        return jnp.concatenate([g0, z64, z64, g0, g1, z64, z64, g1], axis=-1)

    krs = jnp.concatenate([z64, kr, jnp.zeros((L, D, 32), w_in.dtype)], axis=-1)
    q_slabs = []
    for h in range(WIN_HEADS):
        q_h = qa[..., h * WIN_HEAD_DIM:(h + 1) * WIN_HEAD_DIM]
        q_slabs += [q_h, z64] if h < WIN_HEADS // WIN_KV_HEADS else [z64, q_h]
    w_in_p = jnp.concatenate([ka, quad(va), ckv, krs] + q_slabs + [cq, gt], axis=-1).astype(BF16)

    kvr = w_ukv.reshape(L, MLA_KV_RANK, MLA_HEADS, MLA_NOPE + MLA_V)
    kn, vv = kvr[..., :MLA_NOPE], kvr[..., MLA_NOPE:]
    zk = jnp.zeros_like(kn)
    w_k = jnp.concatenate([kn, zk], axis=-1).reshape(L, MLA_KV_RANK, MLA_HEADS * LANES)
    v_even = jnp.concatenate([vv, zk], axis=-1)
    v_odd = jnp.concatenate([zk, vv], axis=-1)
    odd = (jnp.arange(MLA_HEADS) % 2 == 1)[None, None, :, None]
    w_v = jnp.where(odd, v_odd, v_even).reshape(L, MLA_KV_RANK, MLA_HEADS * LANES)
    w_kv = jnp.concatenate([w_k, w_v], axis=-1).astype(BF16)

    qr = w_uq.reshape(L, MLA_Q_RANK, MLA_HEADS, MLA_NOPE + MLA_ROPE)
    w_q = jnp.concatenate([qr, jnp.zeros((L, MLA_Q_RANK, MLA_HEADS, 32), w_uq.dtype)], axis=-1)
    w_q = w_q.reshape(L, MLA_Q_RANK, MLA_HEADS * LANES).astype(BF16)

    w_r = jnp.concatenate(
        [w_router, jnp.zeros((L, D, LANES - N_EXPERTS), w_router.dtype)], axis=-1).astype(BF16)
    return w_in_p, w_kv, w_q, w_r


def _ones_columns():
    lane = jnp.arange(LANES)
    even = (lane == 64).astype(F32)
    odd = (lane == 0).astype(F32)
    win = jnp.concatenate([even, odd, even, odd])
    mla = jnp.concatenate([even, odd] * (MLA_HEADS // 2))
    return jnp.concatenate([win, mla])[None, :]


def _rope_tables(n_ctx, n_lat):
    pos = jnp.arange(n_lat)
    rowp = (pos // GRID_W).astype(F32)
    colp = (pos % GRID_W).astype(F32)

    def pattern(rot_dim):
        nf = rot_dim // 4
        inv = ROPE_THETA ** (-jnp.arange(nf, dtype=F32) / nf)
        ar, ac = rowp[:, None] * inv, colp[:, None] * inv
        cos = jnp.concatenate([jnp.cos(ar), jnp.cos(ar), jnp.cos(ac), jnp.cos(ac)], axis=1)
        sin = jnp.concatenate([-jnp.sin(ar), jnp.sin(ar), -jnp.sin(ac), jnp.sin(ac)], axis=1)
        return cos, sin

    cw, sw = pattern(WIN_HEAD_DIM)
    cw, sw = jnp.tile(cw, (1, 2)), jnp.tile(sw, (1, 2))
    cm, sm = pattern(MLA_ROPE)
    one64, zero64 = jnp.ones((n_lat, 64), F32), jnp.zeros((n_lat, 64), F32)
    cm = jnp.concatenate([one64, cm, one64[:, :32]], axis=1)
    sm = jnp.concatenate([zero64, sm, zero64[:, :32]], axis=1)
    sq_w, sq_m = WIN_SCALE * LOG2E, MLA_SCALE * LOG2E
    lat = jnp.concatenate([cw * sq_w, sw * sq_w, cw, sw, cm * sq_m, sm * sq_m, cm, sm], axis=1)
    ones, zeros = jnp.ones((n_ctx, LANES), F32), jnp.zeros((n_ctx, LANES), F32)
    ctx = jnp.concatenate([ones * sq_w, zeros, ones, zeros, ones * sq_m, zeros, ones, zeros], axis=1)
    return jnp.concatenate([ctx, lat], axis=0)


def kernel(x, c, ctx, c_ctx, w_ada, b_ada, w_in, attn_sink, mla_q_norm, mla_kv_norm, w_uq, w_ukv,
           w_oa, w_ob, w_out, ln1_g, ln1_b, w_router, w_exp_gate, w_exp_up, w_exp_down, ln2_g, ln2_b):
    B, n_lat, D = x.shape
    n_ctx = ctx.shape[1]
    depth = w_in.shape[0]
    S = n_ctx + n_lat
    alpha = (2 * depth) ** 0.25
    assert D == D_MODEL and n_ctx == TOK_BLOCK and n_lat % TOK_BLOCK == 0 and B + 1 <= ADA_ROWS
    cap_c = CAPACITY_FACTOR * n_ctx // N_EXPERTS
    cap_l = CAPACITY_FACTOR * n_lat // N_EXPERTS
    assert cap_l % 256 == 0 and cap_c % 8 == 0
    ncb = n_ctx // TOK_BLOCK

    cond = jnp.concatenate([c, c_ctx[None], jnp.zeros((ADA_ROWS - B - 1, D), F32)], axis=0)
    mods = _ada_call(cond.T, w_ada, b_ada, B + 1).reshape(depth * ADA_ROWS * 6, 1, D)

    w_in_p, w_kv, w_q, w_r = _prep_weights(w_in, w_uq, w_ukv, w_router)
    w_oa_b, w_ob_b, w_out_b = w_oa.astype(BF16), w_ob.astype(BF16), w_out.astype(BF16)
    wg_b, wu_b, wd_b = w_exp_gate.astype(BF16), w_exp_up.astype(BF16), w_exp_down.astype(BF16)
    tab = _rope_tables(n_ctx, n_lat)
    vones = _ones_columns()

    x_all = jnp.concatenate([ctx, x], axis=1)
    moe = None
    for l in range(depth):
        x_res, (qwin, kwin, vwin, qcat, kcat, vext, sg) = _proj_call(
            l, alpha, x_all, moe, ln2_g, ln2_b, mods, tab, w_in_p, mla_kv_norm, mla_q_norm,
            w_kv, w_q, vones, ncb)
        oa = _win_call(l, attn_sink, qwin, kwin, vwin, n_ctx)
        ob = _mla_call(qcat, kcat, vext, n_ctx)
        x1, h2, lg_t = _merge_call(l, alpha, x_res, oa, ob, sg, w_oa_b, w_ob_b, w_out_b,
                                   ln1_g, ln1_b, mods, w_r, ncb)
        aff, idx4 = _route_call(lg_t, n_ctx, cap_c, cap_l)
        idx = idx4.reshape(B, N_EXPERTS, cap_c + cap_l)
        xg = _gather_call(idx, h2)
        moe = _moe_call(l, idx, aff, xg, wg_b, wu_b, wd_b, S)
        x_all = x1
    return _final_call(depth - 1, alpha, x_all, moe, ln2_g, ln2_b, mods, n_ctx)
```

```python
import functools
import math

import jax
import jax.numpy as jnp
from jax import lax
from jax.experimental import pallas as pl
from jax.experimental.pallas import tpu as pltpu

D_MODEL = 1024
GRID_W = 64
WIN_HEADS = 8
WIN_KV_HEADS = 2
WIN_HEAD_DIM = 64
BAND = 128
MLA_HEADS = 8
MLA_Q_RANK = 384
MLA_KV_RANK = 256
MLA_NOPE = 64
MLA_ROPE = 32
MLA_V = 64
N_EXPERTS = 16
CAPACITY_FACTOR = 2
ROPE_THETA = 10000.0
LN_EPS = 1e-5
RMS_EPS = 1e-6
NEG_INF = -1e30
LOG2E = math.log2(math.e)
WIN_SCALE = WIN_HEAD_DIM ** -0.5
MLA_SCALE = (MLA_NOPE + MLA_ROPE) ** -0.5

LANES = 128
TOK_BLOCK = 256
VMEM_LIMIT = 56 * 1024 * 1024

BF16 = jnp.bfloat16
F32 = jnp.float32

SEG_KWIN = (0, 128)
SEG_VWIN = (128, 640)
SEG_CKV = (640, 896)
SEG_KR = (896, 1024)
SEG_QWIN = (1024, 2048)
SEG_CQ = (2048, 2432)
SEG_GATE = (2432, 4480)
IN_COLS_PAD = 4480


def _dot(a, b):
    return jnp.dot(a, b, preferred_element_type=F32)


def _dot_nt(a, b):
    return lax.dot_general(a, b, (((1,), (1,)), ((), ())), preferred_element_type=F32)


def _layer_norm(z, g, b):
    mu = jnp.mean(z, axis=-1, keepdims=True)
    zc = z - mu
    var = jnp.mean(zc * zc, axis=-1, keepdims=True)
    return zc * lax.rsqrt(var + LN_EPS) * g + b


def _rms_norm(x, g):
    return x * lax.rsqrt(jnp.mean(x * x, axis=-1, keepdims=True) + RMS_EPS) * g


def _rope_slab(x, cos, sin, half):
    lane = lax.broadcasted_iota(jnp.int32, x.shape, 1)
    partner = jnp.where((lane & half) == 0,
                        pltpu.roll(x, LANES - half, 1), pltpu.roll(x, half, 1))
    return x * cos + partner * sin


SUBLANES = 8


def _rows_to_tiles(y):
    n = y.shape[0]
    y3 = pltpu.einshape("r(sl)->srl", y, s=SUBLANES)
    y4 = y3.reshape(SUBLANES, n // SUBLANES, SUBLANES, LANES)
    return jnp.transpose(y4, (1, 2, 0, 3)).reshape(n, SUBLANES, LANES)


def _tiles_to_rows(x3):
    n = x3.shape[0]
    x4 = x3.reshape(n // SUBLANES, SUBLANES, SUBLANES, LANES)
    xs = jnp.transpose(x4, (2, 0, 1, 3)).reshape(SUBLANES, n, LANES)
    return pltpu.einshape("srl->r(sl)", xs)


def _compiler_params(sem):
    return pltpu.CompilerParams(dimension_semantics=sem, vmem_limit_bytes=VMEM_LIMIT)


ADA_ROWS = 8
ADA_TN = 512


def _ada_kernel(n_rows, condt_ref, w_ref, b_ref, o_ref):
    ct = condt_ref[...]
    st = ct * jax.nn.sigmoid(ct)
    w = w_ref[0]
    rows = []
    for r in range(n_rows):
        rows.append(jnp.sum(w * st[:, r:r + 1], axis=0, keepdims=True) + b_ref[0])
    rows.append(jnp.zeros((ADA_ROWS - n_rows, w.shape[1]), F32))
    o_ref[0] = jnp.concatenate(rows, axis=0)


def _ada_call(cond_t, w_ada, b_ada, n_rows):
    L, D, N = w_ada.shape
    return pl.pallas_call(
        functools.partial(_ada_kernel, n_rows),
        grid=(L, N // ADA_TN),
        in_specs=[
            pl.BlockSpec((D, ADA_ROWS), lambda l, j: (0, 0)),
            pl.BlockSpec((1, D, ADA_TN), lambda l, j: (l, 0, j)),
            pl.BlockSpec((1, 1, ADA_TN), lambda l, j: (l, 0, j)),
        ],
        out_specs=pl.BlockSpec((1, ADA_ROWS, ADA_TN), lambda l, j: (l, 0, j)),
        out_shape=jax.ShapeDtypeStruct((L, ADA_ROWS, N), F32),
        compiler_params=_compiler_params(("parallel", "parallel")),
        name="adaln_mod",
    )(cond_t, w_ada, b_ada.reshape(L, 1, N))


def _proj_kernel(has_ln, alpha, *refs):
    if has_ln:
        (x_ref, moe_ref, g2_ref, b2_ref, m5_ref, sh_ref, sc_ref, tab_ref, win_ref, gkv_ref, gq_ref,
         wkv_ref, wq_ref, vones_ref,
         xres_ref, qwin_ref, kwin_ref, vwin_ref, qcat_ref, kcat_ref, vext_ref, sg_ref) = refs
        z = alpha * x_ref[0] + m5_ref[0] * _tiles_to_rows(moe_ref[0])
        x = _layer_norm(z, g2_ref[0], b2_ref[0])
        xres_ref[0] = x
    else:
        (x_ref, sh_ref, sc_ref, tab_ref, win_ref, gkv_ref, gq_ref, wkv_ref, wq_ref, vones_ref,
         qwin_ref, kwin_ref, vwin_ref, qcat_ref, kcat_ref, vext_ref, sg_ref) = refs
        x = x_ref[0]
    hb = (x * (1.0 + sc_ref[0]) + sh_ref[0]).astype(BF16)

    def seg(s):
        return _dot(hb, win_ref[0, :, s[0]:s[1]])

    def tab(i):
        return tab_ref[:, i * LANES:(i + 1) * LANES]

    cq_w, sq_w, ck_w, sk_w, cq_m, sq_m, ck_m, sk_m = (tab(i) for i in range(8))

    kwin_ref[0] = _rope_slab(seg(SEG_KWIN), ck_w, sk_w, 16).astype(BF16)
    vwin_ref[0] = (seg(SEG_VWIN) + vones_ref[:, 0:512]).astype(BF16)
    qw = seg(SEG_QWIN)
    qwin_ref[0] = jnp.concatenate(
        [_rope_slab(qw[:, i * LANES:(i + 1) * LANES], cq_w, sq_w, 16) for i in range(WIN_HEADS)],
        axis=1).astype(BF16)

    ckv = _rms_norm(seg(SEG_CKV), gkv_ref[0]).astype(BF16)
    kv = _dot(ckv, wkv_ref[0])
    kr = _rope_slab(seg(SEG_KR), ck_m, sk_m, 8)
    cq = _rms_norm(seg(SEG_CQ), gq_ref[0]).astype(BF16)
    qm = _dot(cq, wq_ref[0])
    for h in range(MLA_HEADS):
        sl = slice(h * LANES, (h + 1) * LANES)
        kcat_ref[0, h] = (kv[:, sl] + kr).astype(BF16)
        vext_ref[0, h] = (kv[:, 1024 + h * LANES:1024 + (h + 1) * LANES]
                          + vones_ref[:, 512 + h * LANES:512 + (h + 1) * LANES]).astype(BF16)
        qcat_ref[0, h] = _rope_slab(qm[:, sl], cq_m, sq_m, 8).astype(BF16)

    sg_ref[0] = jax.nn.sigmoid(seg(SEG_GATE)).astype(BF16)


def _proj_call(layer, alpha, x_all, moe, ln2_g, ln2_b, mods, tab, w_in_p, g_kvn, g_qn, w_kv, w_q,
               vones, n_ctx_blocks):
    B, S, D = x_all.shape
    T = TOK_BLOCK
    has_ln = moe is not None
    L = w_in_p.shape[0]

    def mod_spec(k, lyr):
        def imap(b, i):
            row = jnp.where(i < n_ctx_blocks, B, b)
            return ((lyr * ADA_ROWS + row) * 6 + k, 0, 0)
        return pl.BlockSpec((1, 1, D), imap)

    tok = lambda w: pl.BlockSpec((1, T, w), lambda b, i: (b, i, 0))
    headed = pl.BlockSpec((1, MLA_HEADS, T, LANES), lambda b, i: (b, 0, i, 0))
    const = lambda shp: pl.BlockSpec(shp, lambda b, i: (layer,) + (0,) * (len(shp) - 1),
                                     pipeline_mode=pl.Buffered(1))

    in_specs, args = [tok(D)], [x_all]
    if has_ln:
        prev = pl.BlockSpec((1, 1, D), lambda b, i: (layer - 1, 0, 0))
        tiled = pl.BlockSpec((1, T, SUBLANES, LANES), lambda b, i: (b, i, 0, 0))
        in_specs += [tiled, prev, prev, mod_spec(5, layer - 1)]
        args += [moe, ln2_g.reshape(L, 1, D), ln2_b.reshape(L, 1, D), mods]
    in_specs += [
        mod_spec(0, layer), mod_spec(1, layer),
        pl.BlockSpec((T, 8 * LANES), lambda b, i: (i, 0)),
        const((1, D, IN_COLS_PAD)),
        const((1, 1, MLA_KV_RANK)), const((1, 1, MLA_Q_RANK)),
        const((1, MLA_KV_RANK, 2048)), const((1, MLA_Q_RANK, 1024)),
        pl.BlockSpec((1, 1536), lambda b, i: (0, 0)),
    ]
    args += [mods, mods, tab, w_in_p, g_kvn.reshape(L, 1, -1), g_qn.reshape(L, 1, -1), w_kv, w_q, vones]

    out_specs, out_shape = [], []
    if has_ln:
        out_specs.append(tok(D))
        out_shape.append(jax.ShapeDtypeStruct((B, S, D), F32))
    out_specs += [tok(WIN_HEADS * LANES), tok(LANES), tok(512), headed, headed, headed, tok(2048)]
    out_shape += [jax.ShapeDtypeStruct((B, S, WIN_HEADS * LANES), BF16),
                  jax.ShapeDtypeStruct((B, S, LANES), BF16),
                  jax.ShapeDtypeStruct((B, S, 512), BF16)]
    out_shape += [jax.ShapeDtypeStruct((B, MLA_HEADS, S, LANES), BF16)] * 3
    out_shape += [jax.ShapeDtypeStruct((B, S, 2048), BF16)]

    outs = pl.pallas_call(
        functools.partial(_proj_kernel, has_ln, alpha),
        grid=(B, S // T),
        in_specs=in_specs, out_specs=out_specs, out_shape=out_shape,
        compiler_params=_compiler_params(("parallel", "parallel")),
        name="proj",
    )(*args)
    if has_ln:
        return outs[0], outs[1:]
    return x_all, outs


WIN_HEAD_ORDER = (0, 2, 1, 3, 4, 6, 5, 7)


def _win_kernel(layer, n_ctx, n_ctx_blocks, n_blocks, sink_ref, q_ref, kc_ref, vc_ref,
                k0_ref, k1_ref, k2_ref, v0_ref, v1_ref, v2_ref, o_ref, kall_ref, vall_ref, p_ref):
    i = pl.program_id(1)
    q = q_ref[0]
    q8 = jnp.concatenate([q[:, h * LANES:(h + 1) * LANES] for h in WIN_HEAD_ORDER], axis=0)
    row = lax.broadcasted_iota(jnp.int32, (BAND, LANES), 0)
    col = lax.broadcasted_iota(jnp.int32, (BAND, LANES), 1)
    lane_lo = col < 64
    kall_ref[0:n_ctx, :] = kc_ref[0]
    vall_ref[0:n_ctx, :] = vc_ref[0]

    def attend(nk, masks):
        s = _dot_nt(q8, kall_ref[0:nk, :])
        e_sink = []
        for c, h in enumerate(WIN_HEAD_ORDER):
            sc = s[c * BAND:(c + 1) * BAND, :]
            if masks is not None:
                b0 = n_ctx
                sc = jnp.concatenate(
                    [sc[:, :b0],
                     jnp.where(masks[0], sc[:, b0:b0 + BAND], NEG_INF),
                     sc[:, b0 + BAND:b0 + 2 * BAND],
                     jnp.where(masks[1], sc[:, b0 + 2 * BAND:], NEG_INF)], axis=1)
            sink = sink_ref[layer, h] * LOG2E
            m = jnp.maximum(jnp.max(sc, axis=1, keepdims=True), sink)
            p_ref[c * BAND:(c + 1) * BAND, 0:nk] = jnp.exp2(sc - m).astype(BF16)
            e_sink.append(jnp.exp2(sink - m))
        pairs = []
        for pi in range(4):
            acc = _dot(p_ref[pi * 2 * BAND:(pi + 1) * 2 * BAND, 0:nk],
                       vall_ref[0:nk, pi * LANES:(pi + 1) * LANES])
            es = jnp.concatenate([e_sink[2 * pi], e_sink[2 * pi + 1]], axis=0)
            den = (acc[:, 64:65] if pi % 2 == 0 else acc[:, 0:1]) + es
            pairs.append(acc / den)
        slabs = [jnp.where(lane_lo, pairs[0][:BAND], pairs[1][:BAND]),
                 jnp.where(lane_lo, pairs[0][BAND:], pairs[1][BAND:]),
                 jnp.where(lane_lo, pairs[2][:BAND], pairs[3][:BAND]),
                 jnp.where(lane_lo, pairs[2][BAND:], pairs[3][BAND:])]
        o_ref[0] = jnp.concatenate(slabs, axis=1).astype(o_ref.dtype)

    @pl.when(i < n_ctx_blocks)
    def _():
        attend(n_ctx, None)

    @pl.when(i >= n_ctx_blocks)
    def _():
        for t, (k_ref, v_ref) in enumerate(((k0_ref, v0_ref), (k1_ref, v1_ref), (k2_ref, v2_ref))):
            kall_ref[n_ctx + t * BAND:n_ctx + (t + 1) * BAND, :] = k_ref[0]
            vall_ref[n_ctx + t * BAND:n_ctx + (t + 1) * BAND, :] = v_ref[0]
        off_first = jnp.where(i == n_ctx_blocks, 2 * LANES, 0)
        off_last = jnp.where(i == n_blocks - 1, 2 * LANES, 0)
        attend(n_ctx + 3 * BAND, (col >= row + off_first, col <= row - off_last))


def _win_call(layer, sink, qwin, kwin, vwin, n_ctx):
    B, S, _ = qwin.shape
    nb = S // BAND
    ncb = n_ctx // BAND
    nk = n_ctx + 3 * BAND
    blk = lambda w, off: pl.BlockSpec(
        (1, BAND, w), lambda b, i: (b, jnp.clip(i + off, ncb, nb - 1), 0))
    ctx = lambda w: pl.BlockSpec((1, n_ctx, w), lambda b, i: (b, 0, 0))
    return pl.pallas_call(
        functools.partial(_win_kernel, layer, n_ctx, ncb, nb),
        grid=(B, nb),
        in_specs=[pl.BlockSpec(memory_space=pltpu.SMEM),
                  pl.BlockSpec((1, BAND, WIN_HEADS * LANES), lambda b, i: (b, i, 0)),
                  ctx(LANES), ctx(512), blk(LANES, -1), blk(LANES, 0), blk(LANES, 1),
                  blk(512, -1), blk(512, 0), blk(512, 1)],
        out_specs=pl.BlockSpec((1, BAND, 512), lambda b, i: (b, i, 0)),
        out_shape=jax.ShapeDtypeStruct((B, S, 512), BF16),
        scratch_shapes=[pltpu.VMEM((nk, LANES), BF16), pltpu.VMEM((nk, 512), BF16),
                        pltpu.VMEM((WIN_HEADS * BAND, nk), BF16)],
        compiler_params=_compiler_params(("parallel", "parallel")),
        name="win_attn",
    )(sink, qwin, kwin, vwin, kwin, kwin, kwin, vwin, vwin, vwin)


MLA_KCHUNK = 256
MLA_HEADS_PER_STEP = 4


def _mla_kernel(n_ctx, n_keys, q_ref, k_ref, v_ref, o_ref, s_ref):
    qi = pl.program_id(2)
    tq = q_ref.shape[2]
    lane = lax.broadcasted_iota(jnp.int32, (tq, LANES), 1)

    def run(nk):
        outs = []
        for hh in range(MLA_HEADS_PER_STEP):
            q = q_ref[0, hh]
            mrun = None
            for c in range(nk // MLA_KCHUNK):
                s = _dot_nt(q, k_ref[0, hh, c * MLA_KCHUNK:(c + 1) * MLA_KCHUNK, :])
                s_ref[hh, c] = s
                mc = jnp.maximum(s[:, :LANES], s[:, LANES:])
                mrun = mc if mrun is None else jnp.maximum(mrun, mc)
            m = jnp.max(mrun, axis=1, keepdims=True)
            acc = jnp.zeros((tq, LANES), F32)
            for c in range(nk // MLA_KCHUNK):
                p = jnp.exp2(s_ref[hh, c] - m).astype(BF16)
                acc = acc + _dot(p, v_ref[0, hh, c * MLA_KCHUNK:(c + 1) * MLA_KCHUNK, :])
            den = acc[:, 64:65] if hh % 2 == 0 else acc[:, 0:1]
            outs.append(acc / den)
        o_ref[0] = jnp.concatenate(
            [jnp.where(lane < 64, outs[2 * j], outs[2 * j + 1]) for j in range(len(outs) // 2)],
            axis=1).astype(o_ref.dtype)

    @pl.when(qi == 0)
    def _():
        run(n_ctx)

    @pl.when(qi > 0)
    def _():
        run(n_keys)


def _mla_call(qcat, kcat, vext, n_ctx):
    B, H, S, _ = qcat.shape
    T = TOK_BLOCK
    assert n_ctx == T
    G = MLA_HEADS_PER_STEP
    kv = pl.BlockSpec((1, G, S, LANES), lambda b, hp, i: (b, hp, 0, 0))
    return pl.pallas_call(
        functools.partial(_mla_kernel, n_ctx, S),
        grid=(B, H // G, S // T),
        in_specs=[pl.BlockSpec((1, G, T, LANES), lambda b, hp, i: (b, hp, i, 0)), kv, kv],
        out_specs=pl.BlockSpec((1, T, (G // 2) * LANES), lambda b, hp, i: (b, i, hp)),
        out_shape=jax.ShapeDtypeStruct((B, S, (H // 2) * LANES), BF16),
        scratch_shapes=[pltpu.VMEM((G, S // MLA_KCHUNK, T, MLA_KCHUNK), F32)],
        compiler_params=_compiler_params(("parallel", "parallel", "arbitrary")),
        name="mla_attn",
    )(qcat, kcat, vext)


def _merge_kernel(alpha, x_ref, oa_ref, ob_ref, sg_ref, woa_ref, wob_ref, wout_ref, g1_ref, b1_ref,
                  m2_ref, m3_ref, m4_ref, wr_ref, x1_ref, h2_ref, lg_ref):
    D = x_ref.shape[2]
    sg = sg_ref[0]
    t = (sg[:, :D].astype(F32) * _dot(oa_ref[0], woa_ref[0])
         + sg[:, D:].astype(F32) * _dot(ob_ref[0], wob_ref[0]))
    y = _dot(t.astype(BF16), wout_ref[0])
    x1 = _layer_norm(alpha * x_ref[0] + m2_ref[0] * y, g1_ref[0], b1_ref[0])
    x1_ref[0] = x1
    h2 = x1 * (1.0 + m4_ref[0]) + m3_ref[0]
    h2_ref[0] = _rows_to_tiles(h2)
    lg = _dot(h2.astype(BF16), wr_ref[0])
    lg_ref[0] = lg.T[:N_EXPERTS]


def _merge_call(layer, alpha, x_res, oa, ob, sg, w_oa, w_ob, w_out, ln1_g, ln1_b, mods, w_r,
                n_ctx_blocks):
    B, S, D = x_res.shape
    T = TOK_BLOCK
    L = w_oa.shape[0]

    def mod_spec(k):
        def imap(b, i):
            row = jnp.where(i < n_ctx_blocks, B, b)
            return ((layer * ADA_ROWS + row) * 6 + k, 0, 0)
        return pl.BlockSpec((1, 1, D), imap)

    tok = lambda w: pl.BlockSpec((1, T, w), lambda b, i: (b, i, 0))
    const = lambda shp: pl.BlockSpec(shp, lambda b, i: (layer,) + (0,) * (len(shp) - 1),
                                     pipeline_mode=pl.Buffered(1))
    return pl.pallas_call(
        functools.partial(_merge_kernel, alpha),
        grid=(B, S // T),
        in_specs=[tok(D), tok(512), tok(512), tok(2048),
                  const((1, 512, D)), const((1, 512, D)), const((1, D, D)),
                  const((1, 1, D)), const((1, 1, D)),
                  mod_spec(2), mod_spec(3), mod_spec(4),
                  const((1, D, LANES))],
        out_specs=[tok(D), pl.BlockSpec((1, T, SUBLANES, LANES), lambda b, i: (b, i, 0, 0)),
                   pl.BlockSpec((1, N_EXPERTS, T), lambda b, i: (b, 0, i))],
        out_shape=[jax.ShapeDtypeStruct((B, S, D), F32),
                   jax.ShapeDtypeStruct((B, S, SUBLANES, LANES), F32),
                   jax.ShapeDtypeStruct((B, N_EXPERTS, S), F32)],
        compiler_params=_compiler_params(("parallel", "parallel")),
        name="merge",
    )(x_res, oa, ob, sg, w_oa, w_ob, w_out, ln1_g.reshape(L, 1, D), ln1_b.reshape(L, 1, D),
      mods, mods, mods, w_r)


def _cumsum_lanes(x):
    n = x.shape[1]
    xb = x.astype(BF16)
    r = lax.broadcasted_iota(jnp.int32, (LANES, LANES), 0)
    c = lax.broadcasted_iota(jnp.int32, (LANES, LANES), 1)
    tri = jnp.where(r <= c, 1.0, 0.0).astype(BF16)
    tb = lax.broadcasted_iota(jnp.int32, (n, LANES), 0) // LANES
    kb = lax.broadcasted_iota(jnp.int32, (n, LANES), 1)
    before = jnp.where(tb < kb, 1.0, 0.0).astype(BF16)
    off = _dot(xb, before)
    outs = []
    for k in range(n // LANES):
        outs.append(_dot(xb[:, k * LANES:(k + 1) * LANES], tri) + off[:, k:k + 1])
    return jnp.concatenate(outs, axis=1)


def _select_top(aff, cap):
    bits = pltpu.bitcast(aff, jnp.int32)
    thr = jnp.zeros((aff.shape[0], 1), jnp.int32)
    for bit in range(30, -1, -1):
        cand = thr | (1 << bit)
        cnt = jnp.sum(jnp.where(bits >= cand, 1.0, 0.0), axis=1, keepdims=True)
        thr = jnp.where(cnt >= cap, cand, thr)
    gt = bits > thr
    eq = jnp.where(bits == thr, 1.0, 0.0)
    need = cap - jnp.sum(jnp.where(gt, 1.0, 0.0), axis=1, keepdims=True)
    eq_rank = _cumsum_lanes(eq) - eq
    sel = jnp.where(jnp.logical_or(gt, jnp.logical_and(eq > 0.5, eq_rank < need)), 1.0, 0.0)
    return _cumsum_lanes(sel)


def _route_kernel(n_ctx, cap_c, cap_l, lg_ref, aff_ref, idx_ref, cc_ref, cl_ref):
    lg = lg_ref[0]
    m = jnp.max(lg, axis=0, keepdims=True)
    ex = jnp.exp(lg - m)
    aff = ex / jnp.sum(ex, axis=0, keepdims=True)
    aff_ref[0] = aff
    cnt_c = _select_top(aff[:, :n_ctx], cap_c)
    cnt_l = _select_top(aff[:, n_ctx:], cap_l)
    for ex_i in range(N_EXPERTS):
        cc_ref[ex_i] = cnt_c[ex_i:ex_i + 1, :]
        cl_ref[ex_i] = cnt_l[ex_i:ex_i + 1, :]

    def slots(c_ref, e, n, j0, rows, base, out_row):
        jio = (lax.broadcasted_iota(jnp.int32, (rows, LANES), 0) + j0).astype(F32)
        acc = jnp.zeros((rows, LANES), F32)
        for k in range(n // LANES):
            ck = c_ref[e, :, k * LANES:(k + 1) * LANES]
            acc = acc + jnp.where(ck <= jio, 1.0, 0.0)
        cnt = jnp.sum(acc, axis=1, keepdims=True)
        idx_ref[0, e, out_row:out_row + rows, :] = cnt.astype(jnp.int32) + base

    def per_expert(e, carry):
        slots(cc_ref, e, n_ctx, 0, cap_c, 0, 0)
        n_lat = cl_ref.shape[2]
        for h in range(cap_l // 256):
            slots(cl_ref, e, n_lat, h * 256, 256, n_ctx, cap_c + h * 256)
        return carry

    lax.fori_loop(0, N_EXPERTS, per_expert, 0)


def _route_call(lg_t, n_ctx, cap_c, cap_l):
    B, E, S = lg_t.shape
    cap = cap_c + cap_l
    return pl.pallas_call(
        functools.partial(_route_kernel, n_ctx, cap_c, cap_l),
        grid=(B,),
        in_specs=[pl.BlockSpec((1, E, S), lambda b: (b, 0, 0))],
        out_specs=[pl.BlockSpec((1, E, S), lambda b: (b, 0, 0)),
                   pl.BlockSpec((1, E, cap, 1), lambda b: (b, 0, 0, 0))],
        out_shape=[jax.ShapeDtypeStruct((B, E, S), F32),
                   jax.ShapeDtypeStruct((B, E, cap, 1), jnp.int32)],
        scratch_shapes=[pltpu.VMEM((E, 1, n_ctx), F32), pltpu.VMEM((E, 1, S - n_ctx), F32)],
        compiler_params=_compiler_params(("parallel",)),
        name="route",
    )(lg_t)


ROW_BATCH = 16
MOE_FF_CHUNK = 512


def _gather_kernel(idx_ref, h_ref, xg_ref, tmp_ref):
    cap = xg_ref.shape[2]

    def body(g, carry):
        for r in range(ROW_BATCH):
            j = g * ROW_BATCH + r
            tmp_ref[j] = h_ref[0, idx_ref[0, 0, j]]
        return carry

    lax.fori_loop(0, cap // ROW_BATCH, body, 0)
    xg_ref[0, 0] = _tiles_to_rows(tmp_ref[...]).astype(xg_ref.dtype)


def _gather_call(idx, h2t):
    B, S = h2t.shape[0], h2t.shape[1]
    D = SUBLANES * LANES
    E, cap = idx.shape[1], idx.shape[2]
    assert cap % ROW_BATCH == 0
    return pl.pallas_call(
        _gather_kernel,
        grid=(B, E),
        in_specs=[pl.BlockSpec((1, 1, cap), lambda b, e: (b * E + e, 0, 0), memory_space=pltpu.SMEM),
                  pl.BlockSpec((1, S, SUBLANES, LANES), lambda b, e: (b, 0, 0, 0),
                               pipeline_mode=pl.Buffered(1))],
        out_specs=pl.BlockSpec((1, 1, cap, D), lambda b, e: (b, e, 0, 0)),
        out_shape=jax.ShapeDtypeStruct((B, E, cap, D), BF16),
        scratch_shapes=[pltpu.VMEM((cap, SUBLANES, LANES), F32)],
        compiler_params=_compiler_params(("parallel", "arbitrary")),
        name="moe_gather",
    )(idx.reshape(B * E, 1, cap), h2t)


def _moe_kernel(idx_ref, aff_ref, xg_ref, wg_ref, wu_ref, wd_ref, out_ref, y_ref):
    e = pl.program_id(1)
    cap = y_ref.shape[0]

    @pl.when(e == 0)
    def _():
        out_ref[...] = jnp.zeros(out_ref.shape, out_ref.dtype)

    x = xg_ref[0, 0]
    ff = wg_ref.shape[3]
    y = None
    for f0 in range(0, ff, MOE_FF_CHUNK):
        f1 = f0 + MOE_FF_CHUNK
        a = _dot(x, wg_ref[0, 0, :, f0:f1].astype(BF16))
        u = _dot(x, wu_ref[0, 0, :, f0:f1].astype(BF16))
        hmid = (a * jax.nn.sigmoid(a) * u).astype(BF16)
        part = _dot(hmid, wd_ref[0, 0, f0:f1, :].astype(BF16))
        y = part if y is None else y + part
    y_ref[...] = _rows_to_tiles(y)

    def body(g, carry):
        toks = [idx_ref[0, 0, g * ROW_BATCH + r] for r in range(ROW_BATCH)]
        old = [out_ref[0, t] for t in toks]
        for r, t in enumerate(toks):
            out_ref[0, t] = old[r] + y_ref[g * ROW_BATCH + r] * aff_ref[0, 0, t]
        return carry

    lax.fori_loop(0, cap // ROW_BATCH, body, 0)


def _moe_call(layer, idx, aff, xg, w_gate, w_up, w_down, S):
    B, E, cap, D = xg.shape
    F = w_gate.shape[3]
    wspec = lambda shp: pl.BlockSpec(shp, lambda b, e: (layer, e, 0, 0))
    return pl.pallas_call(
        _moe_kernel,
        grid=(B, E),
        in_specs=[pl.BlockSpec((1, 1, cap), lambda b, e: (b * E + e, 0, 0), memory_space=pltpu.SMEM),
                  pl.BlockSpec((1, 1, S), lambda b, e: (b * E + e, 0, 0), memory_space=pltpu.SMEM),
                  pl.BlockSpec((1, 1, cap, D), lambda b, e: (b, e, 0, 0)),
                  wspec((1, 1, D, F)), wspec((1, 1, D, F)), wspec((1, 1, F, D))],
        out_specs=pl.BlockSpec((1, S, SUBLANES, LANES), lambda b, e: (b, 0, 0, 0),
                               pipeline_mode=pl.Buffered(1)),
        out_shape=jax.ShapeDtypeStruct((B, S, SUBLANES, LANES), F32),
        scratch_shapes=[pltpu.VMEM((cap, SUBLANES, LANES), F32)],
        compiler_params=_compiler_params(("parallel", "arbitrary")),
        name="moe_ffn",
    )(idx.reshape(B * E, 1, cap), aff.reshape(B * E, 1, S), xg, w_gate, w_up, w_down)


def _final_kernel(alpha, x_ref, moe_ref, g_ref, b_ref, m5_ref, o_ref):
    o_ref[0] = _layer_norm(alpha * x_ref[0] + m5_ref[0] * _tiles_to_rows(moe_ref[0]),
                           g_ref[0], b_ref[0])


def _final_call(layer, alpha, x1, moe, ln2_g, ln2_b, mods, n_ctx):
    B, S, D = x1.shape
    T = TOK_BLOCK
    L = ln2_g.shape[0]
    ncb = n_ctx // T
    tok_in = pl.BlockSpec((1, T, D), lambda b, i: (b, i + ncb, 0))
    const = pl.BlockSpec((1, 1, D), lambda b, i: (layer, 0, 0))
    return pl.pallas_call(
        functools.partial(_final_kernel, alpha),
        grid=(B, (S - n_ctx) // T),
        in_specs=[tok_in, pl.BlockSpec((1, T, SUBLANES, LANES), lambda b, i: (b, i + ncb, 0, 0)),
                  const, const,
                  pl.BlockSpec((1, 1, D), lambda b, i: ((layer * ADA_ROWS + b) * 6 + 5, 0, 0))],
        out_specs=pl.BlockSpec((1, T, D), lambda b, i: (b, i, 0)),
        out_shape=jax.ShapeDtypeStruct((B, S - n_ctx, D), F32),
        compiler_params=_compiler_params(("parallel", "parallel")),
        name="final_norm",
    )(x1, moe, ln2_g.reshape(L, 1, D), ln2_b.reshape(L, 1, D), mods)


def _prep_weights(w_in, w_uq, w_ukv, w_router):
    L, D, _ = w_in.shape
    z64 = jnp.zeros((L, D, 64), w_in.dtype)
    ka, va = w_in[..., 0:128], w_in[..., 128:256]
    ckv, kr = w_in[..., 256:512], w_in[..., 512:544]
    qa, cq, gt = w_in[..., 544:1056], w_in[..., 1056:1440], w_in[..., 1440:3488]

    def quad(t):
        g0, g1 = t[..., :64], t[..., 64:]
        return jnp.concatenate([g0, z64, z64, g0, g1, z64, z64, g1], axis=-1)

    krs = jnp.concatenate([z64, kr, jnp.zeros((L, D, 32), w_in.dtype)], axis=-1)
    q_slabs = []
    for h in range(WIN_HEADS):
        q_h = qa[..., h * WIN_HEAD_DIM:(h + 1) * WIN_HEAD_DIM]
        q_slabs += [q_h, z64] if h < WIN_HEADS // WIN_KV_HEADS else [z64, q_h]
    w_in_p = jnp.concatenate([ka, quad(va), ckv, krs] + q_slabs + [cq, gt], axis=-1).astype(BF16)

    kvr = w_ukv.reshape(L, MLA_KV_RANK, MLA_HEADS, MLA_NOPE + MLA_V)
    kn, vv = kvr[..., :MLA_NOPE], kvr[..., MLA_NOPE:]
    zk = jnp.zeros_like(kn)
    w_k = jnp.concatenate([kn, zk], axis=-1).reshape(L, MLA_KV_RANK, MLA_HEADS * LANES)
    v_even = jnp.concatenate([vv, zk], axis=-1)
    v_odd = jnp.concatenate([zk, vv], axis=-1)
    odd = (jnp.arange(MLA_HEADS) % 2 == 1)[None, None, :, None]
    w_v = jnp.where(odd, v_odd, v_even).reshape(L, MLA_KV_RANK, MLA_HEADS * LANES)
    w_kv = jnp.concatenate([w_k, w_v], axis=-1).astype(BF16)

    qr = w_uq.reshape(L, MLA_Q_RANK, MLA_HEADS, MLA_NOPE + MLA_ROPE)
    w_q = jnp.concatenate([qr, jnp.zeros((L, MLA_Q_RANK, MLA_HEADS, 32), w_uq.dtype)], axis=-1)
    w_q = w_q.reshape(L, MLA_Q_RANK, MLA_HEADS * LANES).astype(BF16)

    w_r = jnp.concatenate(
        [w_router, jnp.zeros((L, D, LANES - N_EXPERTS), w_router.dtype)], axis=-1).astype(BF16)
    return w_in_p, w_kv, w_q, w_r


def _ones_columns():
    lane = jnp.arange(LANES)
    even = (lane == 64).astype(F32)
    odd = (lane == 0).astype(F32)
    win = jnp.concatenate([even, odd, even, odd])
    mla = jnp.concatenate([even, odd] * (MLA_HEADS // 2))
    return jnp.concatenate([win, mla])[None, :]


def _rope_tables(n_ctx, n_lat):
    pos = jnp.arange(n_lat)
    rowp = (pos // GRID_W).astype(F32)
    colp = (pos % GRID_W).astype(F32)

    def pattern(rot_dim):
        nf = rot_dim // 4
        inv = ROPE_THETA ** (-jnp.arange(nf, dtype=F32) / nf)
        ar, ac = rowp[:, None] * inv, colp[:, None] * inv
        cos = jnp.concatenate([jnp.cos(ar), jnp.cos(ar), jnp.cos(ac), jnp.cos(ac)], axis=1)
        sin = jnp.concatenate([-jnp.sin(ar), jnp.sin(ar), -jnp.sin(ac), jnp.sin(ac)], axis=1)
        return cos, sin

    cw, sw = pattern(WIN_HEAD_DIM)
    cw, sw = jnp.tile(cw, (1, 2)), jnp.tile(sw, (1, 2))
    cm, sm = pattern(MLA_ROPE)
    one64, zero64 = jnp.ones((n_lat, 64), F32), jnp.zeros((n_lat, 64), F32)
    cm = jnp.concatenate([one64, cm, one64[:, :32]], axis=1)
    sm = jnp.concatenate([zero64, sm, zero64[:, :32]], axis=1)
    sq_w, sq_m = WIN_SCALE * LOG2E, MLA_SCALE * LOG2E
    lat = jnp.concatenate([cw * sq_w, sw * sq_w, cw, sw, cm * sq_m, sm * sq_m, cm, sm], axis=1)
    ones, zeros = jnp.ones((n_ctx, LANES), F32), jnp.zeros((n_ctx, LANES), F32)
    ctx = jnp.concatenate([ones * sq_w, zeros, ones, zeros, ones * sq_m, zeros, ones, zeros], axis=1)
    return jnp.concatenate([ctx, lat], axis=0)


def kernel(x, c, ctx, c_ctx, w_ada, b_ada, w_in, attn_sink, mla_q_norm, mla_kv_norm, w_uq, w_ukv,
           w_oa, w_ob, w_out, ln1_g, ln1_b, w_router, w_exp_gate, w_exp_up, w_exp_down, ln2_g, ln2_b):
    B, n_lat, D = x.shape
    n_ctx = ctx.shape[1]
    depth = w_in.shape[0]
    S = n_ctx + n_lat
    alpha = (2 * depth) ** 0.25
    assert D == D_MODEL and n_ctx == TOK_BLOCK and n_lat % TOK_BLOCK == 0 and B + 1 <= ADA_ROWS
    cap_c = CAPACITY_FACTOR * n_ctx // N_EXPERTS
    cap_l = CAPACITY_FACTOR * n_lat // N_EXPERTS
    assert cap_l % 256 == 0 and cap_c % 8 == 0
    ncb = n_ctx // TOK_BLOCK

    cond = jnp.concatenate([c, c_ctx[None], jnp.zeros((ADA_ROWS - B - 1, D), F32)], axis=0)
    mods = _ada_call(cond.T, w_ada, b_ada, B + 1).reshape(depth * ADA_ROWS * 6, 1, D)

    w_in_p, w_kv, w_q, w_r = _prep_weights(w_in, w_uq, w_ukv, w_router)
    w_oa_b, w_ob_b, w_out_b = w_oa.astype(BF16), w_ob.astype(BF16), w_out.astype(BF16)
    tab = _rope_tables(n_ctx, n_lat)
    vones = _ones_columns()

    x_all = jnp.concatenate([ctx, x], axis=1)
    moe = None
    for l in range(depth):
        x_res, (qwin, kwin, vwin, qcat, kcat, vext, sg) = _proj_call(
            l, alpha, x_all, moe, ln2_g, ln2_b, mods, tab, w_in_p, mla_kv_norm, mla_q_norm,
            w_kv, w_q, vones, ncb)
        oa = _win_call(l, attn_sink, qwin, kwin, vwin, n_ctx)
        ob = _mla_call(qcat, kcat, vext, n_ctx)
        x1, h2, lg_t = _merge_call(l, alpha, x_res, oa, ob, sg, w_oa_b, w_ob_b, w_out_b,
                                   ln1_g, ln1_b, mods, w_r, ncb)
        aff, idx4 = _route_call(lg_t, n_ctx, cap_c, cap_l)
        idx = idx4.reshape(B, N_EXPERTS, cap_c + cap_l)
        xg = _gather_call(idx, h2)
        moe = _moe_call(l, idx, aff, xg, w_exp_gate, w_exp_up, w_exp_down, S)
        x_all = x1
    return _final_call(depth - 1, alpha, x_all, moe, ln2_g, ln2_b, mods, n_ctx)
```

```python
import functools
import math

import jax
import jax.numpy as jnp
from jax import lax
from jax.experimental import pallas as pl
from jax.experimental.pallas import tpu as pltpu

D_MODEL = 1024
GRID_W = 64
WIN_HEADS = 8
WIN_KV_HEADS = 2
WIN_HEAD_DIM = 64
BAND = 128
MLA_HEADS = 8
MLA_Q_RANK = 384
MLA_KV_RANK = 256
MLA_NOPE = 64
MLA_ROPE = 32
MLA_V = 64
N_EXPERTS = 16
CAPACITY_FACTOR = 2
ROPE_THETA = 10000.0
LN_EPS = 1e-5
RMS_EPS = 1e-6
NEG_INF = -1e30
LOG2E = math.log2(math.e)
WIN_SCALE = WIN_HEAD_DIM ** -0.5
MLA_SCALE = (MLA_NOPE + MLA_ROPE) ** -0.5

LANES = 128
TOK_BLOCK = 256
VMEM_LIMIT = 56 * 1024 * 1024

BF16 = jnp.bfloat16
F32 = jnp.float32

SEG_KWIN = (0, 128)
SEG_VWIN = (128, 640)
SEG_CKV = (640, 896)
SEG_KR = (896, 1024)
SEG_QWIN = (1024, 2048)
SEG_CQ = (2048, 2432)
SEG_GATE = (2432, 4480)
IN_COLS_PAD = 4480


def _dot(a, b):
    return jnp.dot(a, b, preferred_element_type=F32)


def _dot_nt(a, b):
    return lax.dot_general(a, b, (((1,), (1,)), ((), ())), preferred_element_type=F32)


def _layer_norm(z, g, b):
    mu = jnp.mean(z, axis=-1, keepdims=True)
    zc = z - mu
    var = jnp.mean(zc * zc, axis=-1, keepdims=True)
    return zc * lax.rsqrt(var + LN_EPS) * g + b


def _rms_norm(x, g):
    return x * lax.rsqrt(jnp.mean(x * x, axis=-1, keepdims=True) + RMS_EPS) * g


def _rope_slab(x, cos, sin, half):
    lane = lax.broadcasted_iota(jnp.int32, x.shape, 1)
    partner = jnp.where((lane & half) == 0,
                        pltpu.roll(x, LANES - half, 1), pltpu.roll(x, half, 1))
    return x * cos + partner * sin


SUBLANES = 8


def _rows_to_tiles(y):
    n = y.shape[0]
    y3 = pltpu.einshape("r(sl)->srl", y, s=SUBLANES)
    y4 = y3.reshape(SUBLANES, n // SUBLANES, SUBLANES, LANES)
    return jnp.transpose(y4, (1, 2, 0, 3)).reshape(n, SUBLANES, LANES)


def _tiles_to_rows(x3):
    n = x3.shape[0]
    x4 = x3.reshape(n // SUBLANES, SUBLANES, SUBLANES, LANES)
    xs = jnp.transpose(x4, (2, 0, 1, 3)).reshape(SUBLANES, n, LANES)
    return pltpu.einshape("srl->r(sl)", xs)


def _compiler_params(sem):
    return pltpu.CompilerParams(dimension_semantics=sem, vmem_limit_bytes=VMEM_LIMIT)


ADA_ROWS = 8
ADA_TN = 512


def _ada_kernel(n_rows, condt_ref, w_ref, b_ref, o_ref):
    ct = condt_ref[...]
    st = ct * jax.nn.sigmoid(ct)
    w = w_ref[0]
    rows = []
    for r in range(n_rows):
        rows.append(jnp.sum(w * st[:, r:r + 1], axis=0, keepdims=True) + b_ref[0])
    rows.append(jnp.zeros((ADA_ROWS - n_rows, w.shape[1]), F32))
    o_ref[0] = jnp.concatenate(rows, axis=0)


def _ada_call(cond_t, w_ada, b_ada, n_rows):
    L, D, N = w_ada.shape
    return pl.pallas_call(
        functools.partial(_ada_kernel, n_rows),
        grid=(L, N // ADA_TN),
        in_specs=[
            pl.BlockSpec((D, ADA_ROWS), lambda l, j: (0, 0)),
            pl.BlockSpec((1, D, ADA_TN), lambda l, j: (l, 0, j)),
            pl.BlockSpec((1, 1, ADA_TN), lambda l, j: (l, 0, j)),
        ],
        out_specs=pl.BlockSpec((1, ADA_ROWS, ADA_TN), lambda l, j: (l, 0, j)),
        out_shape=jax.ShapeDtypeStruct((L, ADA_ROWS, N), F32),
        compiler_params=_compiler_params(("parallel", "parallel")),
        name="adaln_mod",
    )(cond_t, w_ada, b_ada.reshape(L, 1, N))


def _proj_kernel(has_ln, alpha, *refs):
    if has_ln:
        (x_ref, moe_ref, g2_ref, b2_ref, m5_ref, sh_ref, sc_ref, tab_ref, win_ref, gkv_ref, gq_ref,
         wkv_ref, wq_ref, vones_ref,
         xres_ref, qwin_ref, kwin_ref, vwin_ref, qcat_ref, kcat_ref, vext_ref, sg_ref) = refs
        z = alpha * x_ref[0] + m5_ref[0] * _tiles_to_rows(moe_ref[0])
        x = _layer_norm(z, g2_ref[0], b2_ref[0])
        xres_ref[0] = x
    else:
        (x_ref, sh_ref, sc_ref, tab_ref, win_ref, gkv_ref, gq_ref, wkv_ref, wq_ref, vones_ref,
         qwin_ref, kwin_ref, vwin_ref, qcat_ref, kcat_ref, vext_ref, sg_ref) = refs
        x = x_ref[0]
    hb = (x * (1.0 + sc_ref[0]) + sh_ref[0]).astype(BF16)

    def seg(s):
        return _dot(hb, win_ref[0, :, s[0]:s[1]])

    def tab(i):
        return tab_ref[:, i * LANES:(i + 1) * LANES]

    cq_w, sq_w, ck_w, sk_w, cq_m, sq_m, ck_m, sk_m = (tab(i) for i in range(8))

    kwin_ref[0] = _rope_slab(seg(SEG_KWIN), ck_w, sk_w, 16).astype(BF16)
    vwin_ref[0] = (seg(SEG_VWIN) + vones_ref[:, 0:512]).astype(BF16)
    qw = seg(SEG_QWIN)
    qwin_ref[0] = jnp.concatenate(
        [_rope_slab(qw[:, i * LANES:(i + 1) * LANES], cq_w, sq_w, 16) for i in range(WIN_HEADS)],
        axis=1).astype(BF16)

    ckv = _rms_norm(seg(SEG_CKV), gkv_ref[0]).astype(BF16)
    kv = _dot(ckv, wkv_ref[0])
    kr = _rope_slab(seg(SEG_KR), ck_m, sk_m, 8)
    cq = _rms_norm(seg(SEG_CQ), gq_ref[0]).astype(BF16)
    qm = _dot(cq, wq_ref[0])
    for h in range(MLA_HEADS):
        sl = slice(h * LANES, (h + 1) * LANES)
        kcat_ref[0, h] = (kv[:, sl] + kr).astype(BF16)
        vext_ref[0, h] = (kv[:, 1024 + h * LANES:1024 + (h + 1) * LANES]
                          + vones_ref[:, 512 + h * LANES:512 + (h + 1) * LANES]).astype(BF16)
        qcat_ref[0, h] = _rope_slab(qm[:, sl], cq_m, sq_m, 8).astype(BF16)

    sg_ref[0] = jax.nn.sigmoid(seg(SEG_GATE)).astype(BF16)


def _proj_call(layer, alpha, x_all, moe, ln2_g, ln2_b, mods, tab, w_in_p, g_kvn, g_qn, w_kv, w_q,
               vones, n_ctx_blocks):
    B, S, D = x_all.shape
    T = TOK_BLOCK
    has_ln = moe is not None
    L = w_in_p.shape[0]

    def mod_spec(k, lyr):
        def imap(b, i):
            row = jnp.where(i < n_ctx_blocks, B, b)
            return ((lyr * ADA_ROWS + row) * 6 + k, 0, 0)
        return pl.BlockSpec((1, 1, D), imap)

    tok = lambda w: pl.BlockSpec((1, T, w), lambda b, i: (b, i, 0))
    headed = pl.BlockSpec((1, MLA_HEADS, T, LANES), lambda b, i: (b, 0, i, 0))
    const = lambda shp: pl.BlockSpec(shp, lambda b, i: (layer,) + (0,) * (len(shp) - 1),
                                     pipeline_mode=pl.Buffered(1))

    in_specs, args = [tok(D)], [x_all]
    if has_ln:
        prev = pl.BlockSpec((1, 1, D), lambda b, i: (layer - 1, 0, 0))
        tiled = pl.BlockSpec((1, T, SUBLANES, LANES), lambda b, i: (b, i, 0, 0))
        in_specs += [tiled, prev, prev, mod_spec(5, layer - 1)]
        args += [moe, ln2_g.reshape(L, 1, D), ln2_b.reshape(L, 1, D), mods]
    in_specs += [
        mod_spec(0, layer), mod_spec(1, layer),
        pl.BlockSpec((T, 8 * LANES), lambda b, i: (i, 0)),
        const((1, D, IN_COLS_PAD)),
        const((1, 1, MLA_KV_RANK)), const((1, 1, MLA_Q_RANK)),
        const((1, MLA_KV_RANK, 2048)), const((1, MLA_Q_RANK, 1024)),
        pl.BlockSpec((1, 1536), lambda b, i: (0, 0)),
    ]
    args += [mods, mods, tab, w_in_p, g_kvn.reshape(L, 1, -1), g_qn.reshape(L, 1, -1), w_kv, w_q, vones]

    out_specs, out_shape = [], []
    if has_ln:
        out_specs.append(tok(D))
        out_shape.append(jax.ShapeDtypeStruct((B, S, D), F32))
    out_specs += [tok(WIN_HEADS * LANES), tok(LANES), tok(512), headed, headed, headed, tok(2048)]
    out_shape += [jax.ShapeDtypeStruct((B, S, WIN_HEADS * LANES), BF16),
                  jax.ShapeDtypeStruct((B, S, LANES), BF16),
                  jax.ShapeDtypeStruct((B, S, 512), BF16)]
    out_shape += [jax.ShapeDtypeStruct((B, MLA_HEADS, S, LANES), BF16)] * 3
    out_shape += [jax.ShapeDtypeStruct((B, S, 2048), BF16)]

    outs = pl.pallas_call(
        functools.partial(_proj_kernel, has_ln, alpha),
        grid=(B, S // T),
        in_specs=in_specs, out_specs=out_specs, out_shape=out_shape,
        compiler_params=_compiler_params(("parallel", "parallel")),
        name="proj",
    )(*args)
    if has_ln:
        return outs[0], outs[1:]
    return x_all, outs


WIN_HEAD_ORDER = (0, 2, 1, 3, 4, 6, 5, 7)


def _win_kernel(layer, n_ctx, n_ctx_blocks, n_blocks, sink_ref, q_ref, kc_ref, vc_ref,
                k0_ref, k1_ref, k2_ref, v0_ref, v1_ref, v2_ref, o_ref, kall_ref, vall_ref, p_ref):
    i = pl.program_id(1)
    q = q_ref[0]
    q8 = jnp.concatenate([q[:, h * LANES:(h + 1) * LANES] for h in WIN_HEAD_ORDER], axis=0)
    row = lax.broadcasted_iota(jnp.int32, (BAND, LANES), 0)
    col = lax.broadcasted_iota(jnp.int32, (BAND, LANES), 1)
    lane_lo = col < 64
    kall_ref[0:n_ctx, :] = kc_ref[0]
    vall_ref[0:n_ctx, :] = vc_ref[0]

    def attend(nk, masks):
        s = _dot_nt(q8, kall_ref[0:nk, :])
        e_sink = []
        for c, h in enumerate(WIN_HEAD_ORDER):
            sc = s[c * BAND:(c + 1) * BAND, :]
            if masks is not None:
                b0 = n_ctx
                sc = jnp.concatenate(
                    [sc[:, :b0],
                     jnp.where(masks[0], sc[:, b0:b0 + BAND], NEG_INF),
                     sc[:, b0 + BAND:b0 + 2 * BAND],
                     jnp.where(masks[1], sc[:, b0 + 2 * BAND:], NEG_INF)], axis=1)
            sink = sink_ref[layer, h] * LOG2E
            m = jnp.maximum(jnp.max(sc, axis=1, keepdims=True), sink)
            p_ref[c * BAND:(c + 1) * BAND, 0:nk] = jnp.exp2(sc - m).astype(BF16)
            e_sink.append(jnp.exp2(sink - m))
        pairs = []
        for pi in range(4):
            acc = _dot(p_ref[pi * 2 * BAND:(pi + 1) * 2 * BAND, 0:nk],
                       vall_ref[0:nk, pi * LANES:(pi + 1) * LANES])
            es = jnp.concatenate([e_sink[2 * pi], e_sink[2 * pi + 1]], axis=0)
            den = (acc[:, 64:65] if pi % 2 == 0 else acc[:, 0:1]) + es
            pairs.append(acc / den)
        slabs = [jnp.where(lane_lo, pairs[0][:BAND], pairs[1][:BAND]),
                 jnp.where(lane_lo, pairs[0][BAND:], pairs[1][BAND:]),
                 jnp.where(lane_lo, pairs[2][:BAND], pairs[3][:BAND]),
                 jnp.where(lane_lo, pairs[2][BAND:], pairs[3][BAND:])]
        o_ref[0] = jnp.concatenate(slabs, axis=1).astype(o_ref.dtype)

    @pl.when(i < n_ctx_blocks)
    def _():
        attend(n_ctx, None)

    @pl.when(i >= n_ctx_blocks)
    def _():
        for t, (k_ref, v_ref) in enumerate(((k0_ref, v0_ref), (k1_ref, v1_ref), (k2_ref, v2_ref))):
            kall_ref[n_ctx + t * BAND:n_ctx + (t + 1) * BAND, :] = k_ref[0]
            vall_ref[n_ctx + t * BAND:n_ctx + (t + 1) * BAND, :] = v_ref[0]
        off_first = jnp.where(i == n_ctx_blocks, 2 * LANES, 0)
        off_last = jnp.where(i == n_blocks - 1, 2 * LANES, 0)
        attend(n_ctx + 3 * BAND, (col >= row + off_first, col <= row - off_last))


def _win_call(layer, sink, qwin, kwin, vwin, n_ctx):
    B, S, _ = qwin.shape
    nb = S // BAND
    ncb = n_ctx // BAND
    nk = n_ctx + 3 * BAND
    blk = lambda w, off: pl.BlockSpec(
        (1, BAND, w), lambda b, i: (b, jnp.clip(i + off, ncb, nb - 1), 0))
    ctx = lambda w: pl.BlockSpec((1, n_ctx, w), lambda b, i: (b, 0, 0))
    return pl.pallas_call(
        functools.partial(_win_kernel, layer, n_ctx, ncb, nb),
        grid=(B, nb),
        in_specs=[pl.BlockSpec(memory_space=pltpu.SMEM),
                  pl.BlockSpec((1, BAND, WIN_HEADS * LANES), lambda b, i: (b, i, 0)),
                  ctx(LANES), ctx(512), blk(LANES, -1), blk(LANES, 0), blk(LANES, 1),
                  blk(512, -1), blk(512, 0), blk(512, 1)],
        out_specs=pl.BlockSpec((1, BAND, 512), lambda b, i: (b, i, 0)),
        out_shape=jax.ShapeDtypeStruct((B, S, 512), BF16),
        scratch_shapes=[pltpu.VMEM((nk, LANES), BF16), pltpu.VMEM((nk, 512), BF16),
                        pltpu.VMEM((WIN_HEADS * BAND, nk), BF16)],
        compiler_params=_compiler_params(("parallel", "parallel")),
        name="win_attn",
    )(sink, qwin, kwin, vwin, kwin, kwin, kwin, vwin, vwin, vwin)


MLA_KCHUNK = 256
MLA_HEADS_PER_STEP = 4


def _mla_kernel(n_ctx, n_keys, q_ref, k_ref, v_ref, o_ref, s_ref):
    qi = pl.program_id(2)
    tq = q_ref.shape[2]
    lane = lax.broadcasted_iota(jnp.int32, (tq, LANES), 1)

    def run(nk):
        outs = []
        for hh in range(MLA_HEADS_PER_STEP):
            q = q_ref[0, hh]
            mrun = None
            for c in range(nk // MLA_KCHUNK):
                s = _dot_nt(q, k_ref[0, hh, c * MLA_KCHUNK:(c + 1) * MLA_KCHUNK, :])
                s_ref[hh, c] = s
                mc = jnp.maximum(s[:, :LANES], s[:, LANES:])
                mrun = mc if mrun is None else jnp.maximum(mrun, mc)
            m = jnp.max(mrun, axis=1, keepdims=True)
            acc = jnp.zeros((tq, LANES), F32)
            for c in range(nk // MLA_KCHUNK):
                p = jnp.exp2(s_ref[hh, c] - m).astype(BF16)
                acc = acc + _dot(p, v_ref[0, hh, c * MLA_KCHUNK:(c + 1) * MLA_KCHUNK, :])
            den = acc[:, 64:65] if hh % 2 == 0 else acc[:, 0:1]
            outs.append(acc / den)
        o_ref[0] = jnp.concatenate(
            [jnp.where(lane < 64, outs[2 * j], outs[2 * j + 1]) for j in range(len(outs) // 2)],
            axis=1).astype(o_ref.dtype)

    @pl.when(qi == 0)
    def _():
        run(n_ctx)

    @pl.when(qi > 0)
    def _():
        run(n_keys)


def _mla_call(qcat, kcat, vext, n_ctx):
    B, H, S, _ = qcat.shape
    T = TOK_BLOCK
    assert n_ctx == T
    G = MLA_HEADS_PER_STEP
    kv = pl.BlockSpec((1, G, S, LANES), lambda b, hp, i: (b, hp, 0, 0))
    return pl.pallas_call(
        functools.partial(_mla_kernel, n_ctx, S),
        grid=(B, H // G, S // T),
        in_specs=[pl.BlockSpec((1, G, T, LANES), lambda b, hp, i: (b, hp, i, 0)), kv, kv],
        out_specs=pl.BlockSpec((1, T, (G // 2) * LANES), lambda b, hp, i: (b, i, hp)),
        out_shape=jax.ShapeDtypeStruct((B, S, (H // 2) * LANES), BF16),
        scratch_shapes=[pltpu.VMEM((G, S // MLA_KCHUNK, T, MLA_KCHUNK), F32)],
        compiler_params=_compiler_params(("parallel", "parallel", "arbitrary")),
        name="mla_attn",
    )(qcat, kcat, vext)


def _merge_kernel(alpha, x_ref, oa_ref, ob_ref, sg_ref, woa_ref, wob_ref, wout_ref, g1_ref, b1_ref,
                  m2_ref, m3_ref, m4_ref, wr_ref, x1_ref, h2_ref, lg_ref):
    D = x_ref.shape[2]
    sg = sg_ref[0]
    t = (sg[:, :D].astype(F32) * _dot(oa_ref[0], woa_ref[0])
         + sg[:, D:].astype(F32) * _dot(ob_ref[0], wob_ref[0]))
    y = _dot(t.astype(BF16), wout_ref[0])
    x1 = _layer_norm(alpha * x_ref[0] + m2_ref[0] * y, g1_ref[0], b1_ref[0])
    x1_ref[0] = x1
    h2 = x1 * (1.0 + m4_ref[0]) + m3_ref[0]
    h2_ref[0] = _rows_to_tiles(h2)
    lg = _dot(h2.astype(BF16), wr_ref[0])
    lg_ref[0] = lg.T[:N_EXPERTS]


def _merge_call(layer, alpha, x_res, oa, ob, sg, w_oa, w_ob, w_out, ln1_g, ln1_b, mods, w_r,
                n_ctx_blocks):
    B, S, D = x_res.shape
    T = TOK_BLOCK
    L = w_oa.shape[0]

    def mod_spec(k):
        def imap(b, i):
            row = jnp.where(i < n_ctx_blocks, B, b)
            return ((layer * ADA_ROWS + row) * 6 + k, 0, 0)
        return pl.BlockSpec((1, 1, D), imap)

    tok = lambda w: pl.BlockSpec((1, T, w), lambda b, i: (b, i, 0))
    const = lambda shp: pl.BlockSpec(shp, lambda b, i: (layer,) + (0,) * (len(shp) - 1),
                                     pipeline_mode=pl.Buffered(1))
    return pl.pallas_call(
        functools.partial(_merge_kernel, alpha),
        grid=(B, S // T),
        in_specs=[tok(D), tok(512), tok(512), tok(2048),
                  const((1, 512, D)), const((1, 512, D)), const((1, D, D)),
                  const((1, 1, D)), const((1, 1, D)),
                  mod_spec(2), mod_spec(3), mod_spec(4),
                  const((1, D, LANES))],
        out_specs=[tok(D), pl.BlockSpec((1, T, SUBLANES, LANES), lambda b, i: (b, i, 0, 0)),
                   pl.BlockSpec((1, N_EXPERTS, T), lambda b, i: (b, 0, i))],
        out_shape=[jax.ShapeDtypeStruct((B, S, D), F32),
                   jax.ShapeDtypeStruct((B, S, SUBLANES, LANES), F32),
                   jax.ShapeDtypeStruct((B, N_EXPERTS, S), F32)],
        compiler_params=_compiler_params(("parallel", "parallel")),
        name="merge",
    )(x_res, oa, ob, sg, w_oa, w_ob, w_out, ln1_g.reshape(L, 1, D), ln1_b.reshape(L, 1, D),
      mods, mods, mods, w_r)


def _cumsum_lanes(x):
    n = x.shape[1]
    xb = x.astype(BF16)
    r = lax.broadcasted_iota(jnp.int32, (LANES, LANES), 0)
    c = lax.broadcasted_iota(jnp.int32, (LANES, LANES), 1)
    tri = jnp.where(r <= c, 1.0, 0.0).astype(BF16)
    tb = lax.broadcasted_iota(jnp.int32, (n, LANES), 0) // LANES
    kb = lax.broadcasted_iota(jnp.int32, (n, LANES), 1)
    before = jnp.where(tb < kb, 1.0, 0.0).astype(BF16)
    off = _dot(xb, before)
    outs = []
    for k in range(n // LANES):
        outs.append(_dot(xb[:, k * LANES:(k + 1) * LANES], tri) + off[:, k:k + 1])
    return jnp.concatenate(outs, axis=1)


def _select_top(aff, cap):
    bits = pltpu.bitcast(aff, jnp.int32)
    thr = jnp.zeros((aff.shape[0], 1), jnp.int32)
    for bit in range(30, -1, -1):
        cand = thr | (1 << bit)
        cnt = jnp.sum(jnp.where(bits >= cand, 1.0, 0.0), axis=1, keepdims=True)
        thr = jnp.where(cnt >= cap, cand, thr)
    gt = bits > thr
    eq = jnp.where(bits == thr, 1.0, 0.0)
    need = cap - jnp.sum(jnp.where(gt, 1.0, 0.0), axis=1, keepdims=True)
    eq_rank = _cumsum_lanes(eq) - eq
    sel = jnp.where(jnp.logical_or(gt, jnp.logical_and(eq > 0.5, eq_rank < need)), 1.0, 0.0)
    return _cumsum_lanes(sel)


def _route_kernel(n_ctx, cap_c, cap_l, lg_ref, aff_ref, idx_ref, cc_ref, cl_ref):
    lg = lg_ref[0]
    m = jnp.max(lg, axis=0, keepdims=True)
    ex = jnp.exp(lg - m)
    aff = ex / jnp.sum(ex, axis=0, keepdims=True)
    aff_ref[0] = aff
    cnt_c = _select_top(aff[:, :n_ctx], cap_c)
    cnt_l = _select_top(aff[:, n_ctx:], cap_l)
    for ex_i in range(N_EXPERTS):
        cc_ref[ex_i] = cnt_c[ex_i:ex_i + 1, :]
        cl_ref[ex_i] = cnt_l[ex_i:ex_i + 1, :]

    def slots(c_ref, e, n, j0, rows, base, out_row):
        jio = (lax.broadcasted_iota(jnp.int32, (rows, LANES), 0) + j0).astype(F32)
        acc = jnp.zeros((rows, LANES), F32)
        for k in range(n // LANES):
            ck = c_ref[e, :, k * LANES:(k + 1) * LANES]
            acc = acc + jnp.where(ck <= jio, 1.0, 0.0)
        cnt = jnp.sum(acc, axis=1, keepdims=True)
        idx_ref[0, e, out_row:out_row + rows, :] = cnt.astype(jnp.int32) + base

    def per_expert(e, carry):
        slots(cc_ref, e, n_ctx, 0, cap_c, 0, 0)
        n_lat = cl_ref.shape[2]
        for h in range(cap_l // 256):
            slots(cl_ref, e, n_lat, h * 256, 256, n_ctx, cap_c + h * 256)
        return carry

    lax.fori_loop(0, N_EXPERTS, per_expert, 0)


def _route_call(lg_t, n_ctx, cap_c, cap_l):
    B, E, S = lg_t.shape
    cap = cap_c + cap_l
    return pl.pallas_call(
        functools.partial(_route_kernel, n_ctx, cap_c, cap_l),
        grid=(B,),
        in_specs=[pl.BlockSpec((1, E, S), lambda b: (b, 0, 0))],
        out_specs=[pl.BlockSpec((1, E, S), lambda b: (b, 0, 0)),
                   pl.BlockSpec((1, E, cap, 1), lambda b: (b, 0, 0, 0))],
        out_shape=[jax.ShapeDtypeStruct((B, E, S), F32),
                   jax.ShapeDtypeStruct((B, E, cap, 1), jnp.int32)],
        scratch_shapes=[pltpu.VMEM((E, 1, n_ctx), F32), pltpu.VMEM((E, 1, S - n_ctx), F32)],
        compiler_params=_compiler_params(("parallel",)),
        name="route",
    )(lg_t)


ROW_BATCH = 16
MOE_FF_CHUNK = 512


def _gather_kernel(idx_ref, h_ref, xg_ref, tmp_ref):
    cap = xg_ref.shape[2]

    def body(g, carry):
        for r in range(ROW_BATCH):
            j = g * ROW_BATCH + r
            tmp_ref[j] = h_ref[0, idx_ref[0, 0, j]]
        return carry

    lax.fori_loop(0, cap // ROW_BATCH, body, 0)
    xg_ref[0, 0] = _tiles_to_rows(tmp_ref[...]).astype(xg_ref.dtype)


def _gather_call(idx, h2t):
    B, S = h2t.shape[0], h2t.shape[1]
    D = SUBLANES * LANES
    E, cap = idx.shape[1], idx.shape[2]
    assert cap % ROW_BATCH == 0
    return pl.pallas_call(
        _gather_kernel,
        grid=(B, E),
        in_specs=[pl.BlockSpec((1, 1, cap), lambda b, e: (b * E + e, 0, 0), memory_space=pltpu.SMEM),
                  pl.BlockSpec((1, S, SUBLANES, LANES), lambda b, e: (b, 0, 0, 0),
                               pipeline_mode=pl.Buffered(1))],
        out_specs=pl.BlockSpec((1, 1, cap, D), lambda b, e: (b, e, 0, 0)),
        out_shape=jax.ShapeDtypeStruct((B, E, cap, D), BF16),
        scratch_shapes=[pltpu.VMEM((cap, SUBLANES, LANES), F32)],
        compiler_params=_compiler_params(("parallel", "arbitrary")),
        name="moe_gather",
    )(idx.reshape(B * E, 1, cap), h2t)


def _moe_kernel(idx_ref, aff_ref, xg_ref, wg_ref, wu_ref, wd_ref, out_ref, y_ref):
    s = pl.program_id(1)
    n_exp = pl.num_programs(1) - 1
    cap = y_ref.shape[1]

    def ffn(slot):
        x = xg_ref[0, 0]
        ff = wg_ref.shape[3]
        y = None
        for f0 in range(0, ff, MOE_FF_CHUNK):
            f1 = f0 + MOE_FF_CHUNK
            a = _dot(x, wg_ref[0, 0, :, f0:f1].astype(BF16))
            u = _dot(x, wu_ref[0, 0, :, f0:f1].astype(BF16))
            hmid = (a * jax.nn.sigmoid(a) * u).astype(BF16)
            part = _dot(hmid, wd_ref[0, 0, f0:f1, :].astype(BF16))
            y = part if y is None else y + part
        y_ref[slot] = _rows_to_tiles(y)

    def scatter(slot):
        for g in range(cap // ROW_BATCH):
            toks = [idx_ref[0, 0, g * ROW_BATCH + r] for r in range(ROW_BATCH)]
            old = [out_ref[0, t] for t in toks]
            for r, t in enumerate(toks):
                out_ref[0, t] = old[r] + y_ref[slot, g * ROW_BATCH + r] * aff_ref[0, 0, t]

    @pl.when(s == 0)
    def _():
        out_ref[...] = jnp.zeros(out_ref.shape, out_ref.dtype)
        ffn(0)

    @pl.when(jnp.logical_and(s > 0, s < n_exp))
    def _():
        slot = s % 2
        scatter(1 - slot)
        ffn(slot)

    @pl.when(s == n_exp)
    def _():
        scatter((n_exp - 1) % 2)


def _moe_call(layer, idx, aff, xg, w_gate, w_up, w_down, S):
    B, E, cap, D = xg.shape
    F = w_gate.shape[3]
    cur = lambda s: jnp.minimum(s, E - 1)
    prev = lambda s: jnp.maximum(s - 1, 0)
    wspec = lambda shp: pl.BlockSpec(shp, lambda b, s: (layer, cur(s), 0, 0))
    return pl.pallas_call(
        _moe_kernel,
        grid=(B, E + 1),
        in_specs=[pl.BlockSpec((1, 1, cap), lambda b, s: (b * E + prev(s), 0, 0), memory_space=pltpu.SMEM),
                  pl.BlockSpec((1, 1, S), lambda b, s: (b * E + prev(s), 0, 0), memory_space=pltpu.SMEM),
                  pl.BlockSpec((1, 1, cap, D), lambda b, s: (b, cur(s), 0, 0)),
                  wspec((1, 1, D, F)), wspec((1, 1, D, F)), wspec((1, 1, F, D))],
        out_specs=pl.BlockSpec((1, S, SUBLANES, LANES), lambda b, s: (b, 0, 0, 0),
                               pipeline_mode=pl.Buffered(1)),
        out_shape=jax.ShapeDtypeStruct((B, S, SUBLANES, LANES), F32),
        scratch_shapes=[pltpu.VMEM((2, cap, SUBLANES, LANES), F32)],
        compiler_params=_compiler_params(("parallel", "arbitrary")),
        name="moe_ffn",
    )(idx.reshape(B * E, 1, cap), aff.reshape(B * E, 1, S), xg, w_gate, w_up, w_down)


def _final_kernel(alpha, x_ref, moe_ref, g_ref, b_ref, m5_ref, o_ref):
    o_ref[0] = _layer_norm(alpha * x_ref[0] + m5_ref[0] * _tiles_to_rows(moe_ref[0]),
                           g_ref[0], b_ref[0])


def _final_call(layer, alpha, x1, moe, ln2_g, ln2_b, mods, n_ctx):
    B, S, D = x1.shape
    T = TOK_BLOCK
    L = ln2_g.shape[0]
    ncb = n_ctx // T
    tok_in = pl.BlockSpec((1, T, D), lambda b, i: (b, i + ncb, 0))
    const = pl.BlockSpec((1, 1, D), lambda b, i: (layer, 0, 0))
    return pl.pallas_call(
        functools.partial(_final_kernel, alpha),
        grid=(B, (S - n_ctx) // T),
        in_specs=[tok_in, pl.BlockSpec((1, T, SUBLANES, LANES), lambda b, i: (b, i + ncb, 0, 0)),
                  const, const,
                  pl.BlockSpec((1, 1, D), lambda b, i: ((layer * ADA_ROWS + b) * 6 + 5, 0, 0))],
        out_specs=pl.BlockSpec((1, T, D), lambda b, i: (b, i, 0)),
        out_shape=jax.ShapeDtypeStruct((B, S - n_ctx, D), F32),
        compiler_params=_compiler_params(("parallel", "parallel")),
        name="final_norm",
    )(x1, moe, ln2_g.reshape(L, 1, D), ln2_b.reshape(L, 1, D), mods)


def _prep_weights(w_in, w_uq, w_ukv, w_router):
    L, D, _ = w_in.shape
    z64 = jnp.zeros((L, D, 64), w_in.dtype)
    ka, va = w_in[..., 0:128], w_in[..., 128:256]
    ckv, kr = w_in[..., 256:512], w_in[..., 512:544]
    qa, cq, gt = w_in[..., 544:1056], w_in[..., 1056:1440], w_in[..., 1440:3488]

    def quad(t):
        g0, g1 = t[..., :64], t[..., 64:]
        return jnp.concatenate([g0, z64, z64, g0, g1, z64, z64, g1], axis=-1)

    krs = jnp.concatenate([z64, kr, jnp.zeros((L, D, 32), w_in.dtype)], axis=-1)
    q_slabs = []
    for h in range(WIN_HEADS):
        q_h = qa[..., h * WIN_HEAD_DIM:(h + 1) * WIN_HEAD_DIM]
        q_slabs += [q_h, z64] if h < WIN_HEADS // WIN_KV_HEADS else [z64, q_h]
    w_in_p = jnp.concatenate([ka, quad(va), ckv, krs] + q_slabs + [cq, gt], axis=-1).astype(BF16)

    kvr = w_ukv.reshape(L, MLA_KV_RANK, MLA_HEADS, MLA_NOPE + MLA_V)
    kn, vv = kvr[..., :MLA_NOPE], kvr[..., MLA_NOPE:]
    zk = jnp.zeros_like(kn)
    w_k = jnp.concatenate([kn, zk], axis=-1).reshape(L, MLA_KV_RANK, MLA_HEADS * LANES)
    v_even = jnp.concatenate([vv, zk], axis=-1)
    v_odd = jnp.concatenate([zk, vv], axis=-1)
    odd = (jnp.arange(MLA_HEADS) % 2 == 1)[None, None, :, None]
    w_v = jnp.where(odd, v_odd, v_even).reshape(L, MLA_KV_RANK, MLA_HEADS * LANES)
    w_kv = jnp.concatenate([w_k, w_v], axis=-1).astype(BF16)

    qr = w_uq.reshape(L, MLA_Q_RANK, MLA_HEADS, MLA_NOPE + MLA_ROPE)
    w_q = jnp.concatenate([qr, jnp.zeros((L, MLA_Q_RANK, MLA_HEADS, 32), w_uq.dtype)], axis=-1)
    w_q = w_q.reshape(L, MLA_Q_RANK, MLA_HEADS * LANES).astype(BF16)

    w_r = jnp.concatenate(
        [w_router, jnp.zeros((L, D, LANES - N_EXPERTS), w_router.dtype)], axis=-1).astype(BF16)
    return w_in_p, w_kv, w_q, w_r


def _ones_columns():
    lane = jnp.arange(LANES)
    even = (lane == 64).astype(F32)
    odd = (lane == 0).astype(F32)
    win = jnp.concatenate([even, odd, even, odd])
    mla = jnp.concatenate([even, odd] * (MLA_HEADS // 2))
    return jnp.concatenate([win, mla])[None, :]


def _rope_tables(n_ctx, n_lat):
    pos = jnp.arange(n_lat)
    rowp = (pos // GRID_W).astype(F32)
    colp = (pos % GRID_W).astype(F32)

    def pattern(rot_dim):
        nf = rot_dim // 4
        inv = ROPE_THETA ** (-jnp.arange(nf, dtype=F32) / nf)
        ar, ac = rowp[:, None] * inv, colp[:, None] * inv
        cos = jnp.concatenate([jnp.cos(ar), jnp.cos(ar), jnp.cos(ac), jnp.cos(ac)], axis=1)
        sin = jnp.concatenate([-jnp.sin(ar), jnp.sin(ar), -jnp.sin(ac), jnp.sin(ac)], axis=1)
        return cos, sin

    cw, sw = pattern(WIN_HEAD_DIM)
    cw, sw = jnp.tile(cw, (1, 2)), jnp.tile(sw, (1, 2))
    cm, sm = pattern(MLA_ROPE)
    one64, zero64 = jnp.ones((n_lat, 64), F32), jnp.zeros((n_lat, 64), F32)
    cm = jnp.concatenate([one64, cm, one64[:, :32]], axis=1)
    sm = jnp.concatenate([zero64, sm, zero64[:, :32]], axis=1)
    sq_w, sq_m = WIN_SCALE * LOG2E, MLA_SCALE * LOG2E
    lat = jnp.concatenate([cw * sq_w, sw * sq_w, cw, sw, cm * sq_m, sm * sq_m, cm, sm], axis=1)
    ones, zeros = jnp.ones((n_ctx, LANES), F32), jnp.zeros((n_ctx, LANES), F32)
    ctx = jnp.concatenate([ones * sq_w, zeros, ones, zeros, ones * sq_m, zeros, ones, zeros], axis=1)
    return jnp.concatenate([ctx, lat], axis=0)


def kernel(x, c, ctx, c_ctx, w_ada, b_ada, w_in, attn_sink, mla_q_norm, mla_kv_norm, w_uq, w_ukv,
           w_oa, w_ob, w_out, ln1_g, ln1_b, w_router, w_exp_gate, w_exp_up, w_exp_down, ln2_g, ln2_b):
    B, n_lat, D = x.shape
    n_ctx = ctx.shape[1]
    depth = w_in.shape[0]
    S = n_ctx + n_lat
    alpha = (2 * depth) ** 0.25
    assert D == D_MODEL and n_ctx == TOK_BLOCK and n_lat % TOK_BLOCK == 0 and B + 1 <= ADA_ROWS
    cap_c = CAPACITY_FACTOR * n_ctx // N_EXPERTS
    cap_l = CAPACITY_FACTOR * n_lat // N_EXPERTS
    assert cap_l % 256 == 0 and cap_c % 8 == 0
    ncb = n_ctx // TOK_BLOCK

    cond = jnp.concatenate([c, c_ctx[None], jnp.zeros((ADA_ROWS - B - 1, D), F32)], axis=0)
    mods = _ada_call(cond.T, w_ada, b_ada, B + 1).reshape(depth * ADA_ROWS * 6, 1, D)

    w_in_p, w_kv, w_q, w_r = _prep_weights(w_in, w_uq, w_ukv, w_router)
    w_oa_b, w_ob_b, w_out_b = w_oa.astype(BF16), w_ob.astype(BF16), w_out.astype(BF16)
    tab = _rope_tables(n_ctx, n_lat)
    vones = _ones_columns()

    x_all = jnp.concatenate([ctx, x], axis=1)
    moe = None
    for l in range(depth):
        x_res, (qwin, kwin, vwin, qcat, kcat, vext, sg) = _proj_call(
            l, alpha, x_all, moe, ln2_g, ln2_b, mods, tab, w_in_p, mla_kv_norm, mla_q_norm,
            w_kv, w_q, vones, ncb)
        oa = _win_call(l, attn_sink, qwin, kwin, vwin, n_ctx)
        ob = _mla_call(qcat, kcat, vext, n_ctx)
        x1, h2, lg_t = _merge_call(l, alpha, x_res, oa, ob, sg, w_oa_b, w_ob_b, w_out_b,
                                   ln1_g, ln1_b, mods, w_r, ncb)
        aff, idx4 = _route_call(lg_t, n_ctx, cap_c, cap_l)
        idx = idx4.reshape(B, N_EXPERTS, cap_c + cap_l)
        xg = _gather_call(idx, h2)
        moe = _moe_call(l, idx, aff, xg, w_exp_gate, w_exp_up, w_exp_down, S)
        x_all = x1
    return _final_call(depth - 1, alpha, x_all, moe, ln2_g, ln2_b, mods, n_ctx)
```

```python
import functools
import math

import jax
import jax.numpy as jnp
from jax import lax
from jax.experimental import pallas as pl
from jax.experimental.pallas import tpu as pltpu

D_MODEL = 1024
GRID_W = 64
WIN_HEADS = 8
WIN_KV_HEADS = 2
WIN_HEAD_DIM = 64
BAND = 128
MLA_HEADS = 8
MLA_Q_RANK = 384
MLA_KV_RANK = 256
MLA_NOPE = 64
MLA_ROPE = 32
MLA_V = 64
N_EXPERTS = 16
CAPACITY_FACTOR = 2
ROPE_THETA = 10000.0
LN_EPS = 1e-5
RMS_EPS = 1e-6
NEG_INF = -1e30
LOG2E = math.log2(math.e)
WIN_SCALE = WIN_HEAD_DIM ** -0.5
MLA_SCALE = (MLA_NOPE + MLA_ROPE) ** -0.5

LANES = 128
TOK_BLOCK = 256
VMEM_LIMIT = 56 * 1024 * 1024

BF16 = jnp.bfloat16
F32 = jnp.float32

SEG_KWIN = (0, 128)
SEG_VWIN = (128, 640)
SEG_CKV = (640, 896)
SEG_KR = (896, 1024)
SEG_QWIN = (1024, 2048)
SEG_CQ = (2048, 2432)
SEG_GATE = (2432, 4480)
IN_COLS_PAD = 4480


def _dot(a, b):
    return jnp.dot(a, b, preferred_element_type=F32)


def _dot_nt(a, b):
    return lax.dot_general(a, b, (((1,), (1,)), ((), ())), preferred_element_type=F32)


def _layer_norm(z, g, b):
    mu = jnp.mean(z, axis=-1, keepdims=True)
    zc = z - mu
    var = jnp.mean(zc * zc, axis=-1, keepdims=True)
    return zc * lax.rsqrt(var + LN_EPS) * g + b


def _rms_norm(x, g):
    return x * lax.rsqrt(jnp.mean(x * x, axis=-1, keepdims=True) + RMS_EPS) * g


def _rope_slab(x, cos, sin, half):
    lane = lax.broadcasted_iota(jnp.int32, x.shape, 1)
    partner = jnp.where((lane & half) == 0,
                        pltpu.roll(x, LANES - half, 1), pltpu.roll(x, half, 1))
    return x * cos + partner * sin


SUBLANES = 8


def _rows_to_tiles(y):
    n = y.shape[0]
    y3 = pltpu.einshape("r(sl)->srl", y, s=SUBLANES)
    y4 = y3.reshape(SUBLANES, n // SUBLANES, SUBLANES, LANES)
    return jnp.transpose(y4, (1, 2, 0, 3)).reshape(n, SUBLANES, LANES)


def _tiles_to_rows(x3):
    n = x3.shape[0]
    x4 = x3.reshape(n // SUBLANES, SUBLANES, SUBLANES, LANES)
    xs = jnp.transpose(x4, (2, 0, 1, 3)).reshape(SUBLANES, n, LANES)
    return pltpu.einshape("srl->r(sl)", xs)


def _compiler_params(sem):
    return pltpu.CompilerParams(dimension_semantics=sem, vmem_limit_bytes=VMEM_LIMIT)


ADA_ROWS = 8
ADA_TN = 512


def _ada_kernel(n_rows, condt_ref, w_ref, b_ref, o_ref):
    ct = condt_ref[...]
    st = ct * jax.nn.sigmoid(ct)
    w = w_ref[0]
    rows = []
    for r in range(n_rows):
        rows.append(jnp.sum(w * st[:, r:r + 1], axis=0, keepdims=True) + b_ref[0])
    rows.append(jnp.zeros((ADA_ROWS - n_rows, w.shape[1]), F32))
    o_ref[0] = jnp.concatenate(rows, axis=0)


def _ada_call(cond_t, w_ada, b_ada, n_rows):
    L, D, N = w_ada.shape
    return pl.pallas_call(
        functools.partial(_ada_kernel, n_rows),
        grid=(L, N // ADA_TN),
        in_specs=[
            pl.BlockSpec((D, ADA_ROWS), lambda l, j: (0, 0)),
            pl.BlockSpec((1, D, ADA_TN), lambda l, j: (l, 0, j)),
            pl.BlockSpec((1, 1, ADA_TN), lambda l, j: (l, 0, j)),
        ],
        out_specs=pl.BlockSpec((1, ADA_ROWS, ADA_TN), lambda l, j: (l, 0, j)),
        out_shape=jax.ShapeDtypeStruct((L, ADA_ROWS, N), F32),
        compiler_params=_compiler_params(("parallel", "parallel")),
        name="adaln_mod",
    )(cond_t, w_ada, b_ada.reshape(L, 1, N))


def _proj_kernel(has_ln, alpha, n_ctx_blocks, *refs):
    if has_ln:
        (x_ref, moe_ref, g2_ref, b2_ref, m5_ref, sh_ref, sc_ref, tab_ref, win_ref, gkv_ref, gq_ref,
         wkv_ref, wq_ref, vones_ref,
         xres_ref, qwin_ref, kwin_ref, vwin_ref, qcat_ref, kcat_ref, vext_ref, sg_ref) = refs
        z = alpha * x_ref[0] + m5_ref[0] * _tiles_to_rows(moe_ref[0])
        x = _layer_norm(z, g2_ref[0], b2_ref[0])
    else:
        (ctx_ref, x_ref, sh_ref, sc_ref, tab_ref, win_ref, gkv_ref, gq_ref, wkv_ref, wq_ref, vones_ref,
         xres_ref, qwin_ref, kwin_ref, vwin_ref, qcat_ref, kcat_ref, vext_ref, sg_ref) = refs
        x = jnp.where(pl.program_id(1) < n_ctx_blocks, ctx_ref[0], x_ref[0])
    xres_ref[0] = x
    hb = (x * (1.0 + sc_ref[0]) + sh_ref[0]).astype(BF16)

    def seg(s):
        return _dot(hb, win_ref[0, :, s[0]:s[1]])

    def tab(i):
        return tab_ref[:, i * LANES:(i + 1) * LANES]

    cq_w, sq_w, ck_w, sk_w, cq_m, sq_m, ck_m, sk_m = (tab(i) for i in range(8))

    kwin_ref[0] = _rope_slab(seg(SEG_KWIN), ck_w, sk_w, 16).astype(BF16)
    vwin_ref[0] = (seg(SEG_VWIN) + vones_ref[:, 0:512]).astype(BF16)
    qw = seg(SEG_QWIN)
    qwin_ref[0] = jnp.concatenate(
        [_rope_slab(qw[:, i * LANES:(i + 1) * LANES], cq_w, sq_w, 16) for i in range(WIN_HEADS)],
        axis=1).astype(BF16)

    ckv = _rms_norm(seg(SEG_CKV), gkv_ref[0]).astype(BF16)
    kv = _dot(ckv, wkv_ref[0])
    kr = _rope_slab(seg(SEG_KR), ck_m, sk_m, 8)
    cq = _rms_norm(seg(SEG_CQ), gq_ref[0]).astype(BF16)
    qm = _dot(cq, wq_ref[0])
    for h in range(MLA_HEADS):
        sl = slice(h * LANES, (h + 1) * LANES)
        kcat_ref[0, h] = (kv[:, sl] + kr).astype(BF16)
        vext_ref[0, h] = (kv[:, 1024 + h * LANES:1024 + (h + 1) * LANES]
                          + vones_ref[:, 512 + h * LANES:512 + (h + 1) * LANES]).astype(BF16)
        qcat_ref[0, h] = _rope_slab(qm[:, sl], cq_m, sq_m, 8).astype(BF16)

    sg_ref[0] = jax.nn.sigmoid(seg(SEG_GATE)).astype(BF16)


def _proj_call(layer, alpha, x_all, moe, ln2_g, ln2_b, mods, tab, w_in_p, g_kvn, g_qn, w_kv, w_q,
               vones, n_ctx_blocks):
    T = TOK_BLOCK
    has_ln = moe is not None
    if has_ln:
        B, S, D = x_all.shape
    else:
        ctx_in, x_in = x_all
        B, D = x_in.shape[0], x_in.shape[2]
        S = ctx_in.shape[1] + x_in.shape[1]
    L = w_in_p.shape[0]

    def mod_spec(k, lyr):
        def imap(b, i):
            row = jnp.where(i < n_ctx_blocks, B, b)
            return ((lyr * ADA_ROWS + row) * 6 + k, 0, 0)
        return pl.BlockSpec((1, 1, D), imap)

    tok = lambda w: pl.BlockSpec((1, T, w), lambda b, i: (b, i, 0))
    headed = pl.BlockSpec((1, MLA_HEADS, T, LANES), lambda b, i: (b, 0, i, 0))
    const = lambda shp: pl.BlockSpec(shp, lambda b, i: (layer,) + (0,) * (len(shp) - 1),
                                     pipeline_mode=pl.Buffered(1))

    if has_ln:
        prev = pl.BlockSpec((1, 1, D), lambda b, i: (layer - 1, 0, 0))
        tiled = pl.BlockSpec((1, T, SUBLANES, LANES), lambda b, i: (b, i, 0, 0))
        in_specs = [tok(D), tiled, prev, prev, mod_spec(5, layer - 1)]
        args = [x_all, moe, ln2_g.reshape(L, 1, D), ln2_b.reshape(L, 1, D), mods]
    else:
        assert ctx_in.shape[1] == n_ctx_blocks * T
        in_specs = [pl.BlockSpec((1, T, D), lambda b, i: (b, jnp.minimum(i, n_ctx_blocks - 1), 0)),
                    pl.BlockSpec((1, T, D), lambda b, i: (b, jnp.maximum(i - n_ctx_blocks, 0), 0))]
        args = [ctx_in, x_in]
    in_specs += [
        mod_spec(0, layer), mod_spec(1, layer),
        pl.BlockSpec((T, 8 * LANES), lambda b, i: (i, 0)),
        const((1, D, IN_COLS_PAD)),
        const((1, 1, MLA_KV_RANK)), const((1, 1, MLA_Q_RANK)),
        const((1, MLA_KV_RANK, 2048)), const((1, MLA_Q_RANK, 1024)),
        pl.BlockSpec((1, 1536), lambda b, i: (0, 0)),
    ]
    args += [mods, mods, tab, w_in_p, g_kvn.reshape(L, 1, -1), g_qn.reshape(L, 1, -1), w_kv, w_q, vones]

    out_specs, out_shape = [tok(D)], [jax.ShapeDtypeStruct((B, S, D), F32)]
    out_specs += [tok(WIN_HEADS * LANES), tok(LANES), tok(512), headed, headed, headed, tok(2048)]
    out_shape += [jax.ShapeDtypeStruct((B, S, WIN_HEADS * LANES), BF16),
                  jax.ShapeDtypeStruct((B, S, LANES), BF16),
                  jax.ShapeDtypeStruct((B, S, 512), BF16)]
    out_shape += [jax.ShapeDtypeStruct((B, MLA_HEADS, S, LANES), BF16)] * 3
    out_shape += [jax.ShapeDtypeStruct((B, S, 2048), BF16)]

    outs = pl.pallas_call(
        functools.partial(_proj_kernel, has_ln, alpha, n_ctx_blocks),
        grid=(B, S // T),
        in_specs=in_specs, out_specs=out_specs, out_shape=out_shape,
        compiler_params=_compiler_params(("parallel", "parallel")),
        name="proj",
    )(*args)
    return outs[0], outs[1:]


WIN_HEAD_ORDER = (0, 2, 1, 3, 4, 6, 5, 7)


def _win_kernel(layer, n_ctx, n_steps, sink_ref, q_ref, kc_ref, vc_ref, kp_ref, km_ref, kn_ref,
                vp_ref, vm_ref, vn_ref, o_ref, kall_ref, vall_ref, p_ref):
    i = pl.program_id(1)
    row = lax.broadcasted_iota(jnp.int32, (BAND, LANES), 0)
    col = lax.broadcasted_iota(jnp.int32, (BAND, LANES), 1)
    lane_lo = col < 64
    kall_ref[0:n_ctx, :] = kc_ref[0]
    vall_ref[0:n_ctx, :] = vc_ref[0]

    def attend(qb, r0, nk, masks):
        q = q_ref[0, qb * BAND:(qb + 1) * BAND, :]
        q8 = jnp.concatenate([q[:, h * LANES:(h + 1) * LANES] for h in WIN_HEAD_ORDER], axis=0)
        s = _dot_nt(q8, kall_ref[r0:r0 + nk, :])
        e_sink = []
        for c, h in enumerate(WIN_HEAD_ORDER):
            sc = s[c * BAND:(c + 1) * BAND, :]
            if masks:
                blocks = []
                for kb in range(nk // BAND):
                    blk = sc[:, kb * BAND:(kb + 1) * BAND]
                    blocks.append(jnp.where(masks[kb], blk, NEG_INF) if kb in masks else blk)
                sc = jnp.concatenate(blocks, axis=1)
            sink = sink_ref[layer, h] * LOG2E
            m = jnp.maximum(jnp.max(sc, axis=1, keepdims=True), sink)
            p_ref[qb, c * BAND:(c + 1) * BAND, 0:nk] = jnp.exp2(sc - m).astype(BF16)
            e_sink.append(jnp.exp2(sink - m))
        pairs = []
        for pi in range(4):
            acc = _dot(p_ref[qb, pi * 2 * BAND:(pi + 1) * 2 * BAND, 0:nk],
                       vall_ref[r0:r0 + nk, pi * LANES:(pi + 1) * LANES])
            es = jnp.concatenate([e_sink[2 * pi], e_sink[2 * pi + 1]], axis=0)
            den = (acc[:, 64:65] if pi % 2 == 0 else acc[:, 0:1]) + es
            pairs.append(acc / den)
        slabs = [jnp.where(lane_lo, pairs[0][:BAND], pairs[1][:BAND]),
                 jnp.where(lane_lo, pairs[0][BAND:], pairs[1][BAND:]),
                 jnp.where(lane_lo, pairs[2][:BAND], pairs[3][:BAND]),
                 jnp.where(lane_lo, pairs[2][BAND:], pairs[3][BAND:])]
        o_ref[0, qb * BAND:(qb + 1) * BAND, :] = jnp.concatenate(slabs, axis=1).astype(o_ref.dtype)

    @pl.when(i == 0)
    def _():
        attend(0, 0, n_ctx, None)
        attend(1, 0, n_ctx, None)

    @pl.when(i > 0)
    def _():
        r = n_ctx
        for k_ref, v_ref, n in ((kp_ref, vp_ref, BAND), (km_ref, vm_ref, 2 * BAND),
                                (kn_ref, vn_ref, BAND), (kc_ref, vc_ref, n_ctx)):
            kall_ref[r:r + n, :] = k_ref[0]
            vall_ref[r:r + n, :] = v_ref[0]
            r += n
        off_first = jnp.where(i == 1, 2 * LANES, 0)
        off_last = jnp.where(i == n_steps - 1, 2 * LANES, 0)
        cb = n_ctx // BAND
        attend(0, 0, n_ctx + 3 * BAND, {cb: col >= row + off_first, cb + 2: col <= row})
        attend(1, n_ctx + BAND, n_ctx + 3 * BAND, {0: col >= row, 2: col <= row - off_last})


def _win_call(layer, sink, qwin, kwin, vwin, n_ctx):
    B, S, _ = qwin.shape
    nb = S // BAND
    ncb = n_ctx // BAND
    assert n_ctx == 2 * BAND and nb % 2 == 0
    nk = n_ctx + 3 * BAND
    one = lambda w, f: pl.BlockSpec((1, BAND, w), lambda b, i: (b, jnp.clip(f(i), ncb, nb - 1), 0))
    two = lambda w: pl.BlockSpec((1, 2 * BAND, w), lambda b, i: (b, i, 0))
    ctx = lambda w: pl.BlockSpec((1, n_ctx, w), lambda b, i: (b, 0, 0))
    prev, nxt = (lambda i: 2 * i - 1), (lambda i: 2 * i + 2)
    return pl.pallas_call(
        functools.partial(_win_kernel, layer, n_ctx, nb // 2),
        grid=(B, nb // 2),
        in_specs=[pl.BlockSpec(memory_space=pltpu.SMEM), two(WIN_HEADS * LANES),
                  ctx(LANES), ctx(512), one(LANES, prev), two(LANES), one(LANES, nxt),
                  one(512, prev), two(512), one(512, nxt)],
        out_specs=two(512),
        out_shape=jax.ShapeDtypeStruct((B, S, 512), BF16),
        scratch_shapes=[pltpu.VMEM((2 * n_ctx + 4 * BAND, LANES), BF16),
                        pltpu.VMEM((2 * n_ctx + 4 * BAND, 512), BF16),
                        pltpu.VMEM((2, WIN_HEADS * BAND, nk), BF16)],
        compiler_params=_compiler_params(("parallel", "parallel")),
        name="win_attn",
    )(sink, qwin, kwin, vwin, kwin, kwin, kwin, vwin, vwin, vwin)


MLA_KCHUNK = 256
MLA_HEADS_PER_STEP = 4


def _mla_kernel(n_ctx, n_keys, q_ref, k_ref, v_ref, o_ref, s_ref):
    qi = pl.program_id(2)
    tq = q_ref.shape[2]
    lane = lax.broadcasted_iota(jnp.int32, (tq, LANES), 1)

    def run(nk):
        outs = []
        for hh in range(MLA_HEADS_PER_STEP):
            q = q_ref[0, hh]
            mrun = None
            for c in range(nk // MLA_KCHUNK):
                s = _dot_nt(q, k_ref[0, hh, c * MLA_KCHUNK:(c + 1) * MLA_KCHUNK, :])
                s_ref[hh, c] = s
                mc = jnp.maximum(s[:, :LANES], s[:, LANES:])
                mrun = mc if mrun is None else jnp.maximum(mrun, mc)
            m = jnp.max(mrun, axis=1, keepdims=True)
            acc = jnp.zeros((tq, LANES), F32)
            for c in range(nk // MLA_KCHUNK):
                p = jnp.exp2(s_ref[hh, c] - m).astype(BF16)
                acc = acc + _dot(p, v_ref[0, hh, c * MLA_KCHUNK:(c + 1) * MLA_KCHUNK, :])
            den = acc[:, 64:65] if hh % 2 == 0 else acc[:, 0:1]
            outs.append(acc / den)
        o_ref[0] = jnp.concatenate(
            [jnp.where(lane < 64, outs[2 * j], outs[2 * j + 1]) for j in range(len(outs) // 2)],
            axis=1).astype(o_ref.dtype)

    @pl.when(qi == 0)
    def _():
        run(n_ctx)

    @pl.when(qi > 0)
    def _():
        run(n_keys)


def _mla_call(qcat, kcat, vext, n_ctx):
    B, H, S, _ = qcat.shape
    T = TOK_BLOCK
    assert n_ctx == T
    G = MLA_HEADS_PER_STEP
    kv = pl.BlockSpec((1, G, S, LANES), lambda b, hp, i: (b, hp, 0, 0))
    return pl.pallas_call(
        functools.partial(_mla_kernel, n_ctx, S),
        grid=(B, H // G, S // T),
        in_specs=[pl.BlockSpec((1, G, T, LANES), lambda b, hp, i: (b, hp, i, 0)), kv, kv],
        out_specs=pl.BlockSpec((1, T, (G // 2) * LANES), lambda b, hp, i: (b, i, hp)),
        out_shape=jax.ShapeDtypeStruct((B, S, (H // 2) * LANES), BF16),
        scratch_shapes=[pltpu.VMEM((G, S // MLA_KCHUNK, T, MLA_KCHUNK), F32)],
        compiler_params=_compiler_params(("parallel", "parallel", "arbitrary")),
        name="mla_attn",
    )(qcat, kcat, vext)


def _merge_kernel(alpha, x_ref, oa_ref, ob_ref, sg_ref, woa_ref, wob_ref, wout_ref, g1_ref, b1_ref,
                  m2_ref, m3_ref, m4_ref, wr_ref, x1_ref, h2_ref, lg_ref):
    D = x_ref.shape[2]
    sg = sg_ref[0]
    t = (sg[:, :D].astype(F32) * _dot(oa_ref[0], woa_ref[0])
         + sg[:, D:].astype(F32) * _dot(ob_ref[0], wob_ref[0]))
    y = _dot(t.astype(BF16), wout_ref[0])
    x1 = _layer_norm(alpha * x_ref[0] + m2_ref[0] * y, g1_ref[0], b1_ref[0])
    x1_ref[0] = x1
    h2 = x1 * (1.0 + m4_ref[0]) + m3_ref[0]
    h2_ref[0] = _rows_to_tiles(h2)
    lg = _dot(h2.astype(BF16), wr_ref[0])
    lg_ref[0] = lg.T[:N_EXPERTS]


def _merge_call(layer, alpha, x_res, oa, ob, sg, w_oa, w_ob, w_out, ln1_g, ln1_b, mods, w_r,
                n_ctx_blocks):
    B, S, D = x_res.shape
    T = TOK_BLOCK
    L = w_oa.shape[0]

    def mod_spec(k):
        def imap(b, i):
            row = jnp.where(i < n_ctx_blocks, B, b)
            return ((layer * ADA_ROWS + row) * 6 + k, 0, 0)
        return pl.BlockSpec((1, 1, D), imap)

    tok = lambda w: pl.BlockSpec((1, T, w), lambda b, i: (b, i, 0))
    const = lambda shp: pl.BlockSpec(shp, lambda b, i: (layer,) + (0,) * (len(shp) - 1),
                                     pipeline_mode=pl.Buffered(1))
    return pl.pallas_call(
        functools.partial(_merge_kernel, alpha),
        grid=(B, S // T),
        in_specs=[tok(D), tok(512), tok(512), tok(2048),
                  const((1, 512, D)), const((1, 512, D)), const((1, D, D)),
                  const((1, 1, D)), const((1, 1, D)),
                  mod_spec(2), mod_spec(3), mod_spec(4),
                  const((1, D, LANES))],
        out_specs=[tok(D), pl.BlockSpec((1, T, SUBLANES, LANES), lambda b, i: (b, i, 0, 0)),
                   pl.BlockSpec((1, N_EXPERTS, T), lambda b, i: (b, 0, i))],
        out_shape=[jax.ShapeDtypeStruct((B, S, D), F32),
                   jax.ShapeDtypeStruct((B, S, SUBLANES, LANES), F32),
                   jax.ShapeDtypeStruct((B, N_EXPERTS, S), F32)],
        compiler_params=_compiler_params(("parallel", "parallel")),
        name="merge",
    )(x_res, oa, ob, sg, w_oa, w_ob, w_out, ln1_g.reshape(L, 1, D), ln1_b.reshape(L, 1, D),
      mods, mods, mods, w_r)


def _cumsum_lanes(x):
    n = x.shape[1]
    xb = x.astype(BF16)
    r = lax.broadcasted_iota(jnp.int32, (LANES, LANES), 0)
    c = lax.broadcasted_iota(jnp.int32, (LANES, LANES), 1)
    tri = jnp.where(r <= c, 1.0, 0.0).astype(BF16)
    tb = lax.broadcasted_iota(jnp.int32, (n, LANES), 0) // LANES
    kb = lax.broadcasted_iota(jnp.int32, (n, LANES), 1)
    before = jnp.where(tb < kb, 1.0, 0.0).astype(BF16)
    off = _dot(xb, before)
    outs = []
    for k in range(n // LANES):
        outs.append(_dot(xb[:, k * LANES:(k + 1) * LANES], tri) + off[:, k:k + 1])
    return outs


def _select_top(aff, cap):
    bits = pltpu.bitcast(aff, jnp.int32)
    thr = jnp.zeros((aff.shape[0], 1), jnp.int32)
    for bit in range(30, -1, -1):
        cand = thr | (1 << bit)
        cnt = jnp.sum(jnp.where(bits >= cand, 1.0, 0.0), axis=1, keepdims=True)
        thr = jnp.where(cnt >= cap, cand, thr)
    gt = bits > thr
    eq = jnp.where(bits == thr, 1.0, 0.0)
    need = cap - jnp.sum(jnp.where(gt, 1.0, 0.0), axis=1, keepdims=True)
    eq_rank = jnp.concatenate(_cumsum_lanes(eq), axis=1) - eq
    sel = jnp.where(jnp.logical_or(gt, jnp.logical_and(eq > 0.5, eq_rank < need)), 1.0, 0.0)
    return _cumsum_lanes(sel)


ROUTE_PAD = 1e6


def _route_kernel(n_ctx, cap_c, cap_l, lg_ref, aff_ref, idx_ref, cc_ref, cl_ref):
    lg = lg_ref[0]
    m = jnp.max(lg, axis=0, keepdims=True)
    ex = jnp.exp(lg - m)
    aff = ex / jnp.sum(ex, axis=0, keepdims=True)
    aff_ref[0] = aff
    for c_ref, blocks in ((cc_ref, _select_top(aff[:, :n_ctx], cap_c)),
                          (cl_ref, _select_top(aff[:, n_ctx:], cap_l))):
        c_ref[...] = jnp.full(c_ref.shape, ROUTE_PAD, F32)
        for k, blk in enumerate(blocks):
            for ex_i in range(N_EXPERTS):
                c_ref[ex_i, k:k + 1, :] = blk[ex_i:ex_i + 1, :]

    def slots(c_ref, e, n_blocks, width):
        cm = c_ref[e]
        jrow = lax.broadcasted_iota(jnp.int32, (1, width), 1).astype(F32)
        rows = max(n_blocks, SUBLANES)
        cend = cm[0:rows, LANES - 1:LANES]
        blk = jnp.sum(jnp.where(cend <= jrow, 1.0, 0.0), axis=0, keepdims=True)
        kio = lax.broadcasted_iota(jnp.int32, (LANES, width), 0).astype(F32)
        onehot = jnp.where(kio == blk, 1.0, 0.0).astype(BF16)
        cmt = cm.T
        hi = jnp.floor(cmt * (1.0 / 256.0))
        lo = cmt - 256.0 * hi
        cg = 256.0 * _dot(hi.astype(BF16), onehot) + _dot(lo.astype(BF16), onehot)
        inside = jnp.sum(jnp.where(cg <= jrow, 1.0, 0.0), axis=0, keepdims=True)
        return (blk * LANES + inside).astype(jnp.int32)

    def per_expert(e, carry):
        n_lat_blocks = (lg_ref.shape[2] - n_ctx) // LANES
        idx_ref[0, e, :, 0:cap_l] = slots(cl_ref, e, n_lat_blocks, cap_l) + n_ctx
        idx_ref[0, e, :, cap_l:cap_l + cap_c] = slots(cc_ref, e, n_ctx // LANES, LANES)[:, 0:cap_c]
        return carry

    lax.fori_loop(0, N_EXPERTS, per_expert, 0)


def _route_call(lg_t, n_ctx, cap_c, cap_l):
    B, E, S = lg_t.shape
    cap = cap_c + cap_l
    assert (S - n_ctx) // LANES <= LANES and cap_l % LANES == 0 and cap_c <= LANES
    return pl.pallas_call(
        functools.partial(_route_kernel, n_ctx, cap_c, cap_l),
        grid=(B,),
        in_specs=[pl.BlockSpec((1, E, S), lambda b: (b, 0, 0))],
        out_specs=[pl.BlockSpec((1, E, S), lambda b: (b, 0, 0)),
                   pl.BlockSpec((1, E, 1, cap), lambda b: (b, 0, 0, 0))],
        out_shape=[jax.ShapeDtypeStruct((B, E, S), F32),
                   jax.ShapeDtypeStruct((B, E, 1, cap), jnp.int32)],
        scratch_shapes=[pltpu.VMEM((E, LANES, LANES), F32), pltpu.VMEM((E, LANES, LANES), F32)],
        compiler_params=_compiler_params(("parallel",)),
        name="route",
    )(lg_t)


ROW_BATCH = 16
MOE_FF_CHUNK = 512


def _gather_kernel(idx_ref, h_ref, xg_ref, tmp_ref):
    cap = xg_ref.shape[2]

    def body(g, carry):
        for r in range(ROW_BATCH):
            j = g * ROW_BATCH + r
            tmp_ref[j] = h_ref[0, idx_ref[0, 0, j]]
        return carry

    lax.fori_loop(0, cap // ROW_BATCH, body, 0)
    xg_ref[0, 0] = _tiles_to_rows(tmp_ref[...]).astype(xg_ref.dtype)


def _gather_call(idx, h2t):
    B, S = h2t.shape[0], h2t.shape[1]
    D = SUBLANES * LANES
    E, cap = idx.shape[1], idx.shape[2]
    assert cap % ROW_BATCH == 0
    return pl.pallas_call(
        _gather_kernel,
        grid=(B, E),
        in_specs=[pl.BlockSpec((1, 1, cap), lambda b, e: (b * E + e, 0, 0), memory_space=pltpu.SMEM),
                  pl.BlockSpec((1, S, SUBLANES, LANES), lambda b, e: (b, 0, 0, 0),
                               pipeline_mode=pl.Buffered(1))],
        out_specs=pl.BlockSpec((1, 1, cap, D), lambda b, e: (b, e, 0, 0)),
        out_shape=jax.ShapeDtypeStruct((B, E, cap, D), BF16),
        scratch_shapes=[pltpu.VMEM((cap, SUBLANES, LANES), F32)],
        compiler_params=_compiler_params(("parallel", "arbitrary")),
        name="moe_gather",
    )(idx.reshape(B * E, 1, cap), h2t)


def _moe_kernel(idx_ref, aff_ref, xg_ref, wg_ref, wu_ref, wd_ref, out_ref, y_ref):
    s = pl.program_id(1)
    n_exp = pl.num_programs(1) - 1
    cap = y_ref.shape[1]

    def ffn(slot):
        x = xg_ref[0, 0]
        ff = wg_ref.shape[3]
        y = None
        for f0 in range(0, ff, MOE_FF_CHUNK):
            f1 = f0 + MOE_FF_CHUNK
            a = _dot(x, wg_ref[0, 0, :, f0:f1].astype(BF16))
            u = _dot(x, wu_ref[0, 0, :, f0:f1].astype(BF16))
            hmid = (a * jax.nn.sigmoid(a) * u).astype(BF16)
            part = _dot(hmid, wd_ref[0, 0, f0:f1, :].astype(BF16))
            y = part if y is None else y + part
        y_ref[slot] = _rows_to_tiles(y)

    def scatter(slot):
        for g in range(cap // ROW_BATCH):
            toks = [idx_ref[0, 0, g * ROW_BATCH + r] for r in range(ROW_BATCH)]
            old = [out_ref[0, t] for t in toks]
            for r, t in enumerate(toks):
                out_ref[0, t] = old[r] + y_ref[slot, g * ROW_BATCH + r] * aff_ref[0, 0, t]

    @pl.when(s == 0)
    def _():
        out_ref[...] = jnp.zeros(out_ref.shape, out_ref.dtype)
        ffn(0)

    @pl.when(jnp.logical_and(s > 0, s < n_exp))
    def _():
        slot = s % 2
        scatter(1 - slot)
        ffn(slot)

    @pl.when(s == n_exp)
    def _():
        scatter((n_exp - 1) % 2)


def _moe_call(layer, idx, aff, xg, w_gate, w_up, w_down, S):
    B, E, cap, D = xg.shape
    F = w_gate.shape[3]
    cur = lambda s: jnp.minimum(s, E - 1)
    prev = lambda s: jnp.maximum(s - 1, 0)
    wspec = lambda shp: pl.BlockSpec(shp, lambda b, s: (layer, cur(s), 0, 0))
    return pl.pallas_call(
        _moe_kernel,
        grid=(B, E + 1),
        in_specs=[pl.BlockSpec((1, 1, cap), lambda b, s: (b * E + prev(s), 0, 0), memory_space=pltpu.SMEM),
                  pl.BlockSpec((1, 1, S), lambda b, s: (b * E + prev(s), 0, 0), memory_space=pltpu.SMEM),
                  pl.BlockSpec((1, 1, cap, D), lambda b, s: (b, cur(s), 0, 0)),
                  wspec((1, 1, D, F)), wspec((1, 1, D, F)), wspec((1, 1, F, D))],
        out_specs=pl.BlockSpec((1, S, SUBLANES, LANES), lambda b, s: (b, 0, 0, 0),
                               pipeline_mode=pl.Buffered(1)),
        out_shape=jax.ShapeDtypeStruct((B, S, SUBLANES, LANES), F32),
        scratch_shapes=[pltpu.VMEM((2, cap, SUBLANES, LANES), F32)],
        compiler_params=_compiler_params(("parallel", "arbitrary")),
        name="moe_ffn",
    )(idx.reshape(B * E, 1, cap), aff.reshape(B * E, 1, S), xg, w_gate, w_up, w_down)


def _final_kernel(alpha, x_ref, moe_ref, g_ref, b_ref, m5_ref, o_ref):
    o_ref[0] = _layer_norm(alpha * x_ref[0] + m5_ref[0] * _tiles_to_rows(moe_ref[0]),
                           g_ref[0], b_ref[0])


def _final_call(layer, alpha, x1, moe, ln2_g, ln2_b, mods, n_ctx):
    B, S, D = x1.shape
    T = TOK_BLOCK
    L = ln2_g.shape[0]
    ncb = n_ctx // T
    tok_in = pl.BlockSpec((1, T, D), lambda b, i: (b, i + ncb, 0))
    const = pl.BlockSpec((1, 1, D), lambda b, i: (layer, 0, 0))
    return pl.pallas_call(
        functools.partial(_final_kernel, alpha),
        grid=(B, (S - n_ctx) // T),
        in_specs=[tok_in, pl.BlockSpec((1, T, SUBLANES, LANES), lambda b, i: (b, i + ncb, 0, 0)),
                  const, const,
                  pl.BlockSpec((1, 1, D), lambda b, i: ((layer * ADA_ROWS + b) * 6 + 5, 0, 0))],
        out_specs=pl.BlockSpec((1, T, D), lambda b, i: (b, i, 0)),
        out_shape=jax.ShapeDtypeStruct((B, S - n_ctx, D), F32),
        compiler_params=_compiler_params(("parallel", "parallel")),
        name="final_norm",
    )(x1, moe, ln2_g.reshape(L, 1, D), ln2_b.reshape(L, 1, D), mods)


def _prep_weights(w_in, w_uq, w_ukv, w_router):
    L, D, _ = w_in.shape
    z64 = jnp.zeros((L, D, 64), w_in.dtype)
    ka, va = w_in[..., 0:128], w_in[..., 128:256]
    ckv, kr = w_in[..., 256:512], w_in[..., 512:544]
    qa, cq, gt = w_in[..., 544:1056], w_in[..., 1056:1440], w_in[..., 1440:3488]

    def quad(t):
        g0, g1 = t[..., :64], t[..., 64:]
        return jnp.concatenate([g0, z64, z64, g0, g1, z64, z64, g1], axis=-1)

    krs = jnp.concatenate([z64, kr, jnp.zeros((L, D, 32), w_in.dtype)], axis=-1)
    q_slabs = []
    for h in range(WIN_HEADS):
        q_h = qa[..., h * WIN_HEAD_DIM:(h + 1) * WIN_HEAD_DIM]
        q_slabs += [q_h, z64] if h < WIN_HEADS // WIN_KV_HEADS else [z64, q_h]
    w_in_p = jnp.concatenate([ka, quad(va), ckv, krs] + q_slabs + [cq, gt], axis=-1).astype(BF16)

    kvr = w_ukv.reshape(L, MLA_KV_RANK, MLA_HEADS, MLA_NOPE + MLA_V)
    kn, vv = kvr[..., :MLA_NOPE], kvr[..., MLA_NOPE:]
    zk = jnp.zeros_like(kn)
    w_k = jnp.concatenate([kn, zk], axis=-1).reshape(L, MLA_KV_RANK, MLA_HEADS * LANES)
    v_even = jnp.concatenate([vv, zk], axis=-1)
    v_odd = jnp.concatenate([zk, vv], axis=-1)
    odd = (jnp.arange(MLA_HEADS) % 2 == 1)[None, None, :, None]
    w_v = jnp.where(odd, v_odd, v_even).reshape(L, MLA_KV_RANK, MLA_HEADS * LANES)
    w_kv = jnp.concatenate([w_k, w_v], axis=-1).astype(BF16)

    qr = w_uq.reshape(L, MLA_Q_RANK, MLA_HEADS, MLA_NOPE + MLA_ROPE)
    w_q = jnp.concatenate([qr, jnp.zeros((L, MLA_Q_RANK, MLA_HEADS, 32), w_uq.dtype)], axis=-1)
    w_q = w_q.reshape(L, MLA_Q_RANK, MLA_HEADS * LANES).astype(BF16)

    w_r = jnp.concatenate(
        [w_router, jnp.zeros((L, D, LANES - N_EXPERTS), w_router.dtype)], axis=-1).astype(BF16)
    return w_in_p, w_kv, w_q, w_r


def _ones_columns():
    lane = jnp.arange(LANES)
    even = (lane == 64).astype(F32)
    odd = (lane == 0).astype(F32)
    win = jnp.concatenate([even, odd, even, odd])
    mla = jnp.concatenate([even, odd] * (MLA_HEADS // 2))
    return jnp.concatenate([win, mla])[None, :]


def _rope_tables(n_ctx, n_lat):
    pos = jnp.arange(n_lat)
    rowp = (pos // GRID_W).astype(F32)
    colp = (pos % GRID_W).astype(F32)

    def pattern(rot_dim):
        nf = rot_dim // 4
        inv = ROPE_THETA ** (-jnp.arange(nf, dtype=F32) / nf)
        ar, ac = rowp[:, None] * inv, colp[:, None] * inv
        cos = jnp.concatenate([jnp.cos(ar), jnp.cos(ar), jnp.cos(ac), jnp.cos(ac)], axis=1)
        sin = jnp.concatenate([-jnp.sin(ar), jnp.sin(ar), -jnp.sin(ac), jnp.sin(ac)], axis=1)
        return cos, sin

    cw, sw = pattern(WIN_HEAD_DIM)
    cw, sw = jnp.tile(cw, (1, 2)), jnp.tile(sw, (1, 2))
    cm, sm = pattern(MLA_ROPE)
    one64, zero64 = jnp.ones((n_lat, 64), F32), jnp.zeros((n_lat, 64), F32)
    cm = jnp.concatenate([one64, cm, one64[:, :32]], axis=1)
    sm = jnp.concatenate([zero64, sm, zero64[:, :32]], axis=1)
    sq_w, sq_m = WIN_SCALE * LOG2E, MLA_SCALE * LOG2E
    lat = jnp.concatenate([cw * sq_w, sw * sq_w, cw, sw, cm * sq_m, sm * sq_m, cm, sm], axis=1)
    ones, zeros = jnp.ones((n_ctx, LANES), F32), jnp.zeros((n_ctx, LANES), F32)
    ctx = jnp.concatenate([ones * sq_w, zeros, ones, zeros, ones * sq_m, zeros, ones, zeros], axis=1)
    return jnp.concatenate([ctx, lat], axis=0)


def kernel(x, c, ctx, c_ctx, w_ada, b_ada, w_in, attn_sink, mla_q_norm, mla_kv_norm, w_uq, w_ukv,
           w_oa, w_ob, w_out, ln1_g, ln1_b, w_router, w_exp_gate, w_exp_up, w_exp_down, ln2_g, ln2_b):
    B, n_lat, D = x.shape
    n_ctx = ctx.shape[1]
    depth = w_in.shape[0]
    S = n_ctx + n_lat
    alpha = (2 * depth) ** 0.25
    assert D == D_MODEL and n_ctx == TOK_BLOCK and n_lat % TOK_BLOCK == 0 and B + 1 <= ADA_ROWS
    cap_c = CAPACITY_FACTOR * n_ctx // N_EXPERTS
    cap_l = CAPACITY_FACTOR * n_lat // N_EXPERTS
    assert cap_l % 256 == 0 and cap_c % 8 == 0
    ncb = n_ctx // TOK_BLOCK

    cond = jnp.concatenate([c, c_ctx[None], jnp.zeros((ADA_ROWS - B - 1, D), F32)], axis=0)
    mods = _ada_call(cond.T, w_ada, b_ada, B + 1).reshape(depth * ADA_ROWS * 6, 1, D)

    w_in_p, w_kv, w_q, w_r = _prep_weights(w_in, w_uq, w_ukv, w_router)
    w_oa_b, w_ob_b, w_out_b = w_oa.astype(BF16), w_ob.astype(BF16), w_out.astype(BF16)
    tab = _rope_tables(n_ctx, n_lat)
    vones = _ones_columns()

    x_all = (ctx, x)
    moe = None
    for l in range(depth):
        x_res, (qwin, kwin, vwin, qcat, kcat, vext, sg) = _proj_call(
            l, alpha, x_all, moe, ln2_g, ln2_b, mods, tab, w_in_p, mla_kv_norm, mla_q_norm,
            w_kv, w_q, vones, ncb)
        oa = _win_call(l, attn_sink, qwin, kwin, vwin, n_ctx)
        ob = _mla_call(qcat, kcat, vext, n_ctx)
        x1, h2, lg_t = _merge_call(l, alpha, x_res, oa, ob, sg, w_oa_b, w_ob_b, w_out_b,
                                   ln1_g, ln1_b, mods, w_r, ncb)
        aff, idx4 = _route_call(lg_t, n_ctx, cap_c, cap_l)
        idx = idx4.reshape(B, N_EXPERTS, cap_c + cap_l)
        xg = _gather_call(idx, h2)
        moe = _moe_call(l, idx, aff, xg, w_exp_gate, w_exp_up, w_exp_down, S)
        x_all = x1
    return _final_call(depth - 1, alpha, x_all, moe, ln2_g, ln2_b, mods, n_ctx)
```

```python
import functools
import math

import jax
import jax.numpy as jnp
from jax import lax
from jax.experimental import pallas as pl
from jax.experimental.pallas import tpu as pltpu

D_MODEL = 1024
GRID_W = 64
WIN_HEADS = 8
WIN_KV_HEADS = 2
WIN_HEAD_DIM = 64
BAND = 128
MLA_HEADS = 8
MLA_Q_RANK = 384
MLA_KV_RANK = 256
MLA_NOPE = 64
MLA_ROPE = 32
MLA_V = 64
N_EXPERTS = 16
CAPACITY_FACTOR = 2
ROPE_THETA = 10000.0
LN_EPS = 1e-5
RMS_EPS = 1e-6
NEG_INF = -1e30
LOG2E = math.log2(math.e)
WIN_SCALE = WIN_HEAD_DIM ** -0.5
MLA_SCALE = (MLA_NOPE + MLA_ROPE) ** -0.5

LANES = 128
TOK_BLOCK = 256
VMEM_LIMIT = 56 * 1024 * 1024

BF16 = jnp.bfloat16
F32 = jnp.float32

SEG_KWIN = (0, 128)
SEG_VWIN = (128, 256)
SEG_CKV = (256, 512)
SEG_KR = (512, 640)
SEG_QWIN = (640, 1152)
SEG_CQ = (1152, 1536)
SEG_GATE = (1536, 3584)
IN_COLS_PAD = 3584


def _dot(a, b):
    return jnp.dot(a, b, preferred_element_type=F32)


def _dot_nt(a, b):
    return lax.dot_general(a, b, (((1,), (1,)), ((), ())), preferred_element_type=F32)


def _layer_norm(z, g, b):
    mu = jnp.mean(z, axis=-1, keepdims=True)
    zc = z - mu
    var = jnp.mean(zc * zc, axis=-1, keepdims=True)
    return zc * lax.rsqrt(var + LN_EPS) * g + b


def _rms_norm(x, g):
    return x * lax.rsqrt(jnp.mean(x * x, axis=-1, keepdims=True) + RMS_EPS) * g


def _rope_slab(x, cos, sin, half):
    lane = lax.broadcasted_iota(jnp.int32, x.shape, 1)
    partner = jnp.where((lane & half) == 0,
                        pltpu.roll(x, LANES - half, 1), pltpu.roll(x, half, 1))
    return x * cos + partner * sin


SUBLANES = 8


def _rows_to_tiles(y):
    n = y.shape[0]
    y3 = pltpu.einshape("r(sl)->srl", y, s=SUBLANES)
    y4 = y3.reshape(SUBLANES, n // SUBLANES, SUBLANES, LANES)
    return jnp.transpose(y4, (1, 2, 0, 3)).reshape(n, SUBLANES, LANES)


def _tiles_to_rows(x3):
    n = x3.shape[0]
    x4 = x3.reshape(n // SUBLANES, SUBLANES, SUBLANES, LANES)
    xs = jnp.transpose(x4, (2, 0, 1, 3)).reshape(SUBLANES, n, LANES)
    return pltpu.einshape("srl->r(sl)", xs)


def _compiler_params(sem):
    return pltpu.CompilerParams(dimension_semantics=sem, vmem_limit_bytes=VMEM_LIMIT)


ADA_ROWS = 8
ADA_TN = 512


def _ada_kernel(n_rows, condt_ref, w_ref, b_ref, o_ref):
    ct = condt_ref[...]
    st = ct * jax.nn.sigmoid(ct)
    w = w_ref[0]
    rows = []
    for r in range(n_rows):
        rows.append(jnp.sum(w * st[:, r:r + 1], axis=0, keepdims=True) + b_ref[0])
    rows.append(jnp.zeros((ADA_ROWS - n_rows, w.shape[1]), F32))
    o_ref[0] = jnp.concatenate(rows, axis=0)


def _ada_call(cond_t, w_ada, b_ada, n_rows):
    L, D, N = w_ada.shape
    return pl.pallas_call(
        functools.partial(_ada_kernel, n_rows),
        grid=(L, N // ADA_TN),
        in_specs=[
            pl.BlockSpec((D, ADA_ROWS), lambda l, j: (0, 0)),
            pl.BlockSpec((1, D, ADA_TN), lambda l, j: (l, 0, j)),
            pl.BlockSpec((1, 1, ADA_TN), lambda l, j: (l, 0, j)),
        ],
        out_specs=pl.BlockSpec((1, ADA_ROWS, ADA_TN), lambda l, j: (l, 0, j)),
        out_shape=jax.ShapeDtypeStruct((L, ADA_ROWS, N), F32),
        compiler_params=_compiler_params(("parallel", "parallel")),
        name="adaln_mod",
    )(cond_t, w_ada, b_ada.reshape(L, 1, N))


def _proj_kernel(has_ln, alpha, n_ctx_blocks, *refs):
    if has_ln:
        (x_ref, moe_ref, g2_ref, b2_ref, m5_ref, sh_ref, sc_ref, tab_ref, win_ref, gkv_ref, gq_ref,
         wkv_ref, wq_ref, vones_ref,
         xres_ref, qwin_ref, kwin_ref, vwin_ref, qcat_ref, kcat_ref, vext_ref, sg_ref) = refs
        z = alpha * x_ref[0] + m5_ref[0] * _tiles_to_rows(moe_ref[0])
        x = _layer_norm(z, g2_ref[0], b2_ref[0])
    else:
        (ctx_ref, x_ref, sh_ref, sc_ref, tab_ref, win_ref, gkv_ref, gq_ref, wkv_ref, wq_ref, vones_ref,
         xres_ref, qwin_ref, kwin_ref, vwin_ref, qcat_ref, kcat_ref, vext_ref, sg_ref) = refs
        x = jnp.where(pl.program_id(1) < n_ctx_blocks, ctx_ref[0], x_ref[0])
    xres_ref[0] = x
    hb = (x * (1.0 + sc_ref[0]) + sh_ref[0]).astype(BF16)

    def seg(s):
        return _dot(hb, win_ref[0, :, s[0]:s[1]])

    def tab(i):
        return tab_ref[:, i * LANES:(i + 1) * LANES]

    cq_w, sq_w, ck_w, sk_w, cq_m, sq_m, ck_m, sk_m = (tab(i) for i in range(8))

    lane_lo = lax.broadcasted_iota(jnp.int32, (x.shape[0], LANES), 1) < WIN_HEAD_DIM
    half_swap = lambda t: pltpu.roll(t, WIN_HEAD_DIM, 1)

    ckv = _rms_norm(seg(SEG_CKV), gkv_ref[0]).astype(BF16)
    cq = _rms_norm(seg(SEG_CQ), gq_ref[0]).astype(BF16)

    qw = seg(SEG_QWIN)
    q_slabs = []
    for s2 in range(WIN_HEADS // 2):
        pair = _rope_slab(qw[:, s2 * LANES:(s2 + 1) * LANES], cq_w, sq_w, 16)
        pair_sw = half_swap(pair)
        if 2 * s2 < WIN_HEADS // WIN_KV_HEADS:
            q_slabs += [jnp.where(lane_lo, pair, 0.0), jnp.where(lane_lo, pair_sw, 0.0)]
        else:
            q_slabs += [jnp.where(lane_lo, 0.0, pair_sw), jnp.where(lane_lo, 0.0, pair)]
    qwin_ref[0] = jnp.concatenate(q_slabs, axis=1).astype(BF16)

    kv = _dot(ckv, wkv_ref[0])
    kr = _rope_slab(seg(SEG_KR), ck_m, sk_m, 8)
    qm = _dot(cq, wq_ref[0])
    for h in range(MLA_HEADS):
        sl = slice(h * LANES, (h + 1) * LANES)
        kcat_ref[0, h] = (kv[:, sl] + kr).astype(BF16)
        vext_ref[0, h] = (kv[:, 1024 + h * LANES:1024 + (h + 1) * LANES]
                          + vones_ref[:, 512 + h * LANES:512 + (h + 1) * LANES]).astype(BF16)
        qcat_ref[0, h] = _rope_slab(qm[:, sl], cq_m, sq_m, 8).astype(BF16)

    sg_ref[0] = jax.nn.sigmoid(seg(SEG_GATE)).astype(BF16)
    va = seg(SEG_VWIN)
    va_sw = half_swap(va)
    vwin_ref[0] = (jnp.concatenate(
        [jnp.where(lane_lo, va, 0.0), jnp.where(lane_lo, 0.0, va_sw),
         jnp.where(lane_lo, va_sw, 0.0), jnp.where(lane_lo, 0.0, va)], axis=1)
        + vones_ref[:, 0:512]).astype(BF16)
    kwin_ref[0] = _rope_slab(seg(SEG_KWIN), ck_w, sk_w, 16).astype(BF16)


def _proj_call(layer, alpha, x_all, moe, ln2_g, ln2_b, mods, tab, w_in_p, g_kvn, g_qn, w_kv, w_q,
               vones, n_ctx_blocks):
    T = TOK_BLOCK
    has_ln = moe is not None
    if has_ln:
        B, S, D = x_all.shape
    else:
        ctx_in, x_in = x_all
        B, D = x_in.shape[0], x_in.shape[2]
        S = ctx_in.shape[1] + x_in.shape[1]
    L = w_in_p.shape[0]

    def mod_spec(k, lyr):
        def imap(b, i):
            row = jnp.where(i < n_ctx_blocks, B, b)
            return ((lyr * ADA_ROWS + row) * 6 + k, 0, 0)
        return pl.BlockSpec((1, 1, D), imap)

    tok = lambda w: pl.BlockSpec((1, T, w), lambda b, i: (b, i, 0))
    headed = pl.BlockSpec((1, MLA_HEADS, T, LANES), lambda b, i: (b, 0, i, 0))
    const = lambda shp: pl.BlockSpec(shp, lambda b, i: (layer,) + (0,) * (len(shp) - 1),
                                     pipeline_mode=pl.Buffered(1))

    if has_ln:
        prev = pl.BlockSpec((1, 1, D), lambda b, i: (layer - 1, 0, 0))
        tiled = pl.BlockSpec((1, T, SUBLANES, LANES), lambda b, i: (b, i, 0, 0))
        in_specs = [tok(D), tiled, prev, prev, mod_spec(5, layer - 1)]
        args = [x_all, moe, ln2_g.reshape(L, 1, D), ln2_b.reshape(L, 1, D), mods]
    else:
        assert ctx_in.shape[1] == n_ctx_blocks * T
        in_specs = [pl.BlockSpec((1, T, D), lambda b, i: (b, jnp.minimum(i, n_ctx_blocks - 1), 0)),
                    pl.BlockSpec((1, T, D), lambda b, i: (b, jnp.maximum(i - n_ctx_blocks, 0), 0))]
        args = [ctx_in, x_in]
    in_specs += [
        mod_spec(0, layer), mod_spec(1, layer),
        pl.BlockSpec((T, 8 * LANES), lambda b, i: (i, 0)),
        const((1, D, IN_COLS_PAD)),
        const((1, 1, MLA_KV_RANK)), const((1, 1, MLA_Q_RANK)),
        const((1, MLA_KV_RANK, 2048)), const((1, MLA_Q_RANK, 1024)),
        pl.BlockSpec((1, 1536), lambda b, i: (0, 0)),
    ]
    args += [mods, mods, tab, w_in_p, g_kvn.reshape(L, 1, -1), g_qn.reshape(L, 1, -1), w_kv, w_q, vones]

    out_specs, out_shape = [tok(D)], [jax.ShapeDtypeStruct((B, S, D), F32)]
    out_specs += [tok(WIN_HEADS * LANES), tok(LANES), tok(512), headed, headed, headed, tok(2048)]
    out_shape += [jax.ShapeDtypeStruct((B, S, WIN_HEADS * LANES), BF16),
                  jax.ShapeDtypeStruct((B, S, LANES), BF16),
                  jax.ShapeDtypeStruct((B, S, 512), BF16)]
    out_shape += [jax.ShapeDtypeStruct((B, MLA_HEADS, S, LANES), BF16)] * 3
    out_shape += [jax.ShapeDtypeStruct((B, S, 2048), BF16)]

    outs = pl.pallas_call(
        functools.partial(_proj_kernel, has_ln, alpha, n_ctx_blocks),
        grid=(B, S // T),
        in_specs=in_specs, out_specs=out_specs, out_shape=out_shape,
        compiler_params=_compiler_params(("parallel", "parallel")),
        name="proj",
    )(*args)
    return outs[0], outs[1:]


WIN_HEAD_ORDER = (0, 2, 1, 3, 4, 6, 5, 7)


def _win_kernel(layer, n_ctx, n_steps, sink_ref, q_ref, kc_ref, vc_ref, kp_ref, km_ref, kn_ref,
                vp_ref, vm_ref, vn_ref, o_ref, kall_ref, vall_ref, p_ref):
    i = pl.program_id(1)
    row = lax.broadcasted_iota(jnp.int32, (BAND, LANES), 0)
    col = lax.broadcasted_iota(jnp.int32, (BAND, LANES), 1)
    lane_lo = col < 64
    kall_ref[0:n_ctx, :] = kc_ref[0]
    vall_ref[0:n_ctx, :] = vc_ref[0]

    def attend(qb, r0, nk, masks):
        q = q_ref[0, qb * BAND:(qb + 1) * BAND, :]
        q8 = jnp.concatenate([q[:, h * LANES:(h + 1) * LANES] for h in WIN_HEAD_ORDER], axis=0)
        s = _dot_nt(q8, kall_ref[r0:r0 + nk, :])
        e_sink = []
        for c, h in enumerate(WIN_HEAD_ORDER):
            sc = s[c * BAND:(c + 1) * BAND, :]
            if masks:
                blocks = []
                for kb in range(nk // BAND):
                    blk = sc[:, kb * BAND:(kb + 1) * BAND]
                    blocks.append(jnp.where(masks[kb], blk, NEG_INF) if kb in masks else blk)
                sc = jnp.concatenate(blocks, axis=1)
            sink = sink_ref[layer, h] * LOG2E
            m = jnp.maximum(jnp.max(sc, axis=1, keepdims=True), sink)
            p_ref[qb, c * BAND:(c + 1) * BAND, 0:nk] = jnp.exp2(sc - m).astype(BF16)
            e_sink.append(jnp.exp2(sink - m))
        pairs = []
        for pi in range(4):
            acc = _dot(p_ref[qb, pi * 2 * BAND:(pi + 1) * 2 * BAND, 0:nk],
                       vall_ref[r0:r0 + nk, pi * LANES:(pi + 1) * LANES])
            es = jnp.concatenate([e_sink[2 * pi], e_sink[2 * pi + 1]], axis=0)
            den = (acc[:, 64:65] if pi % 2 == 0 else acc[:, 0:1]) + es
            pairs.append(acc / den)
        slabs = [jnp.where(lane_lo, pairs[0][:BAND], pairs[1][:BAND]),
                 jnp.where(lane_lo, pairs[0][BAND:], pairs[1][BAND:]),
                 jnp.where(lane_lo, pairs[2][:BAND], pairs[3][:BAND]),
                 jnp.where(lane_lo, pairs[2][BAND:], pairs[3][BAND:])]
        o_ref[0, qb * BAND:(qb + 1) * BAND, :] = jnp.concatenate(slabs, axis=1).astype(o_ref.dtype)

    @pl.when(i == 0)
    def _():
        attend(0, 0, n_ctx, None)
        attend(1, 0, n_ctx, None)

    @pl.when(i > 0)
    def _():
        r = n_ctx
        for k_ref, v_ref, n in ((kp_ref, vp_ref, BAND), (km_ref, vm_ref, 2 * BAND),
                                (kn_ref, vn_ref, BAND), (kc_ref, vc_ref, n_ctx)):
            kall_ref[r:r + n, :] = k_ref[0]
            vall_ref[r:r + n, :] = v_ref[0]
            r += n
        off_first = jnp.where(i == 1, 2 * LANES, 0)
        off_last = jnp.where(i == n_steps - 1, 2 * LANES, 0)
        cb = n_ctx // BAND
        attend(0, 0, n_ctx + 3 * BAND, {cb: col >= row + off_first, cb + 2: col <= row})
        attend(1, n_ctx + BAND, n_ctx + 3 * BAND, {0: col >= row, 2: col <= row - off_last})


def _win_call(layer, sink, qwin, kwin, vwin, n_ctx):
    B, S, _ = qwin.shape
    nb = S // BAND
    ncb = n_ctx // BAND
    assert n_ctx == 2 * BAND and nb % 2 == 0
    nk = n_ctx + 3 * BAND
    one = lambda w, f: pl.BlockSpec((1, BAND, w), lambda b, i: (b, jnp.clip(f(i), ncb, nb - 1), 0))
    two = lambda w: pl.BlockSpec((1, 2 * BAND, w), lambda b, i: (b, i, 0))
    ctx = lambda w: pl.BlockSpec((1, n_ctx, w), lambda b, i: (b, 0, 0))
    prev, nxt = (lambda i: 2 * i - 1), (lambda i: 2 * i + 2)
    return pl.pallas_call(
        functools.partial(_win_kernel, layer, n_ctx, nb // 2),
        grid=(B, nb // 2),
        in_specs=[pl.BlockSpec(memory_space=pltpu.SMEM), two(WIN_HEADS * LANES),
                  ctx(LANES), ctx(512), one(LANES, prev), two(LANES), one(LANES, nxt),
                  one(512, prev), two(512), one(512, nxt)],
        out_specs=two(512),
        out_shape=jax.ShapeDtypeStruct((B, S, 512), BF16),
        scratch_shapes=[pltpu.VMEM((2 * n_ctx + 4 * BAND, LANES), BF16),
                        pltpu.VMEM((2 * n_ctx + 4 * BAND, 512), BF16),
                        pltpu.VMEM((2, WIN_HEADS * BAND, nk), BF16)],
        compiler_params=_compiler_params(("parallel", "parallel")),
        name="win_attn",
    )(sink, qwin, kwin, vwin, kwin, kwin, kwin, vwin, vwin, vwin)


MLA_KCHUNK = 256
MLA_HEADS_PER_STEP = 4


def _mla_kernel(n_ctx, n_keys, q_ref, k_ref, v_ref, o_ref, s_ref):
    qi = pl.program_id(2)
    tq = q_ref.shape[2]
    lane = lax.broadcasted_iota(jnp.int32, (tq, LANES), 1)

    def run(nk):
        outs = []
        for hh in range(MLA_HEADS_PER_STEP):
            q = q_ref[0, hh]
            mrun = None
            for c in range(nk // MLA_KCHUNK):
                s = _dot_nt(q, k_ref[0, hh, c * MLA_KCHUNK:(c + 1) * MLA_KCHUNK, :])
                s_ref[hh, c] = s
                mc = jnp.maximum(s[:, :LANES], s[:, LANES:])
                mrun = mc if mrun is None else jnp.maximum(mrun, mc)
            m = jnp.max(mrun, axis=1, keepdims=True)
            acc = jnp.zeros((tq, LANES), F32)
            for c in range(nk // MLA_KCHUNK):
                p = jnp.exp2(s_ref[hh, c] - m).astype(BF16)
                acc = acc + _dot(p, v_ref[0, hh, c * MLA_KCHUNK:(c + 1) * MLA_KCHUNK, :])
            den = acc[:, 64:65] if hh % 2 == 0 else acc[:, 0:1]
            outs.append(acc / den)
        o_ref[0] = jnp.concatenate(
            [jnp.where(lane < 64, outs[2 * j], outs[2 * j + 1]) for j in range(len(outs) // 2)],
            axis=1).astype(o_ref.dtype)

    @pl.when(qi == 0)
    def _():
        run(n_ctx)

    @pl.when(qi > 0)
    def _():
        run(n_keys)


def _mla_call(qcat, kcat, vext, n_ctx):
    B, H, S, _ = qcat.shape
    T = TOK_BLOCK
    assert n_ctx == T
    G = MLA_HEADS_PER_STEP
    kv = pl.BlockSpec((1, G, S, LANES), lambda b, hp, i: (b, hp, 0, 0))
    return pl.pallas_call(
        functools.partial(_mla_kernel, n_ctx, S),
        grid=(B, H // G, S // T),
        in_specs=[pl.BlockSpec((1, G, T, LANES), lambda b, hp, i: (b, hp, i, 0)), kv, kv],
        out_specs=pl.BlockSpec((1, T, (G // 2) * LANES), lambda b, hp, i: (b, i, hp)),
        out_shape=jax.ShapeDtypeStruct((B, S, (H // 2) * LANES), BF16),
        scratch_shapes=[pltpu.VMEM((G, S // MLA_KCHUNK, T, MLA_KCHUNK), F32)],
        compiler_params=_compiler_params(("parallel", "parallel", "arbitrary")),
        name="mla_attn",
    )(qcat, kcat, vext)


def _merge_kernel(alpha, x_ref, oa_ref, ob_ref, sg_ref, woa_ref, wob_ref, wout_ref, g1_ref, b1_ref,
                  m2_ref, m3_ref, m4_ref, wr_ref, x1_ref, h2_ref, lg_ref):
    D = x_ref.shape[2]
    sg = sg_ref[0]
    t = (sg[:, :D].astype(F32) * _dot(oa_ref[0], woa_ref[0])
         + sg[:, D:].astype(F32) * _dot(ob_ref[0], wob_ref[0]))
    y = _dot(t.astype(BF16), wout_ref[0])
    x1 = _layer_norm(alpha * x_ref[0] + m2_ref[0] * y, g1_ref[0], b1_ref[0])
    x1_ref[0] = x1
    h2 = x1 * (1.0 + m4_ref[0]) + m3_ref[0]
    h2_ref[0] = _rows_to_tiles(h2)
    lg = _dot(h2.astype(BF16), wr_ref[0])
    lg_ref[0] = lg.T[:N_EXPERTS]


def _merge_call(layer, alpha, x_res, oa, ob, sg, w_oa, w_ob, w_out, ln1_g, ln1_b, mods, w_r,
                n_ctx_blocks):
    B, S, D = x_res.shape
    T = TOK_BLOCK
    L = w_oa.shape[0]

    def mod_spec(k):
        def imap(b, i):
            row = jnp.where(i < n_ctx_blocks, B, b)
            return ((layer * ADA_ROWS + row) * 6 + k, 0, 0)
        return pl.BlockSpec((1, 1, D), imap)

    tok = lambda w: pl.BlockSpec((1, T, w), lambda b, i: (b, i, 0))
    const = lambda shp: pl.BlockSpec(shp, lambda b, i: (layer,) + (0,) * (len(shp) - 1),
                                     pipeline_mode=pl.Buffered(1))
    return pl.pallas_call(
        functools.partial(_merge_kernel, alpha),
        grid=(B, S // T),
        in_specs=[tok(D), tok(512), tok(512), tok(2048),
                  const((1, 512, D)), const((1, 512, D)), const((1, D, D)),
                  const((1, 1, D)), const((1, 1, D)),
                  mod_spec(2), mod_spec(3), mod_spec(4),
                  const((1, D, LANES))],
        out_specs=[tok(D), pl.BlockSpec((1, T, SUBLANES, LANES), lambda b, i: (b, i, 0, 0)),
                   pl.BlockSpec((1, N_EXPERTS, T), lambda b, i: (b, 0, i))],
        out_shape=[jax.ShapeDtypeStruct((B, S, D), F32),
                   jax.ShapeDtypeStruct((B, S, SUBLANES, LANES), F32),
                   jax.ShapeDtypeStruct((B, N_EXPERTS, S), F32)],
        compiler_params=_compiler_params(("parallel", "parallel")),
        name="merge",
    )(x_res, oa, ob, sg, w_oa, w_ob, w_out, ln1_g.reshape(L, 1, D), ln1_b.reshape(L, 1, D),
      mods, mods, mods, w_r)


def _cumsum_lanes(x):
    n = x.shape[1]
    xb = x.astype(BF16)
    r = lax.broadcasted_iota(jnp.int32, (LANES, LANES), 0)
    c = lax.broadcasted_iota(jnp.int32, (LANES, LANES), 1)
    tri = jnp.where(r <= c, 1.0, 0.0).astype(BF16)
    tb = lax.broadcasted_iota(jnp.int32, (n, LANES), 0) // LANES
    kb = lax.broadcasted_iota(jnp.int32, (n, LANES), 1)
    before = jnp.where(tb < kb, 1.0, 0.0).astype(BF16)
    off = _dot(xb, before)
    outs = []
    for k in range(n // LANES):
        outs.append(_dot(xb[:, k * LANES:(k + 1) * LANES], tri) + off[:, k:k + 1])
    return outs


def _select_top(aff, cap):
    bits = pltpu.bitcast(aff, jnp.int32)
    thr = jnp.zeros((aff.shape[0], 1), jnp.int32)
    for bit in range(30, -1, -1):
        cand = thr | (1 << bit)
        cnt = jnp.sum(jnp.where(bits >= cand, 1.0, 0.0), axis=1, keepdims=True)
        thr = jnp.where(cnt >= cap, cand, thr)
    gt = bits > thr
    eq = jnp.where(bits == thr, 1.0, 0.0)
    need = cap - jnp.sum(jnp.where(gt, 1.0, 0.0), axis=1, keepdims=True)
    eq_rank = jnp.concatenate(_cumsum_lanes(eq), axis=1) - eq
    sel = jnp.where(jnp.logical_or(gt, jnp.logical_and(eq > 0.5, eq_rank < need)), 1.0, 0.0)
    return _cumsum_lanes(sel)


ROUTE_PAD = 1e6


def _route_kernel(n_ctx, cap_c, cap_l, lg_ref, aff_ref, idx_ref, cc_ref, cl_ref):
    lg = lg_ref[0]
    m = jnp.max(lg, axis=0, keepdims=True)
    ex = jnp.exp(lg - m)
    aff = ex / jnp.sum(ex, axis=0, keepdims=True)
    aff_ref[0] = aff
    for c_ref, blocks in ((cc_ref, _select_top(aff[:, :n_ctx], cap_c)),
                          (cl_ref, _select_top(aff[:, n_ctx:], cap_l))):
        c_ref[...] = jnp.full(c_ref.shape, ROUTE_PAD, F32)
        for k, blk in enumerate(blocks):
            for ex_i in range(N_EXPERTS):
                c_ref[ex_i, k:k + 1, :] = blk[ex_i:ex_i + 1, :]

    def slots(c_ref, e, n_blocks, width):
        cm = c_ref[e]
        jrow = lax.broadcasted_iota(jnp.int32, (1, width), 1).astype(F32)
        rows = max(n_blocks, SUBLANES)
        cend = cm[0:rows, LANES - 1:LANES]
        blk = jnp.sum(jnp.where(cend <= jrow, 1.0, 0.0), axis=0, keepdims=True)
        kio = lax.broadcasted_iota(jnp.int32, (LANES, width), 0).astype(F32)
        onehot = jnp.where(kio == blk, 1.0, 0.0).astype(BF16)
        cmt = cm.T
        hi = jnp.floor(cmt * (1.0 / 256.0))
        lo = cmt - 256.0 * hi
        cg = 256.0 * _dot(hi.astype(BF16), onehot) + _dot(lo.astype(BF16), onehot)
        inside = jnp.sum(jnp.where(cg <= jrow, 1.0, 0.0), axis=0, keepdims=True)
        return (blk * LANES + inside).astype(jnp.int32)

    def per_expert(e, carry):
        n_lat_blocks = (lg_ref.shape[2] - n_ctx) // LANES
        idx_ref[0, e, :, 0:cap_l] = slots(cl_ref, e, n_lat_blocks, cap_l) + n_ctx
        idx_ref[0, e, :, cap_l:cap_l + cap_c] = slots(cc_ref, e, n_ctx // LANES, LANES)[:, 0:cap_c]
        return carry

    lax.fori_loop(0, N_EXPERTS, per_expert, 0)


def _route_call(lg_t, n_ctx, cap_c, cap_l):
    B, E, S = lg_t.shape
    cap = cap_c + cap_l
    assert (S - n_ctx) // LANES <= LANES and cap_l % LANES == 0 and cap_c <= LANES
    return pl.pallas_call(
        functools.partial(_route_kernel, n_ctx, cap_c, cap_l),
        grid=(B,),
        in_specs=[pl.BlockSpec((1, E, S), lambda b: (b, 0, 0))],
        out_specs=[pl.BlockSpec((1, E, S), lambda b: (b, 0, 0)),
                   pl.BlockSpec((1, E, 1, cap), lambda b: (b, 0, 0, 0))],
        out_shape=[jax.ShapeDtypeStruct((B, E, S), F32),
                   jax.ShapeDtypeStruct((B, E, 1, cap), jnp.int32)],
        scratch_shapes=[pltpu.VMEM((E, LANES, LANES), F32), pltpu.VMEM((E, LANES, LANES), F32)],
        compiler_params=_compiler_params(("parallel",)),
        name="route",
    )(lg_t)


ROW_BATCH = 16
MOE_FF_CHUNK = 512


def _gather_kernel(idx_ref, h_ref, xg_ref, tmp_ref):
    cap = xg_ref.shape[2]

    def body(g, carry):
        for r in range(ROW_BATCH):
            j = g * ROW_BATCH + r
            tmp_ref[j] = h_ref[0, idx_ref[0, 0, j]]
        return carry

    lax.fori_loop(0, cap // ROW_BATCH, body, 0)
    xg_ref[0, 0] = _tiles_to_rows(tmp_ref[...]).astype(xg_ref.dtype)


def _gather_call(idx, h2t):
    B, S = h2t.shape[0], h2t.shape[1]
    D = SUBLANES * LANES
    E, cap = idx.shape[1], idx.shape[2]
    assert cap % ROW_BATCH == 0
    return pl.pallas_call(
        _gather_kernel,
        grid=(B, E),
        in_specs=[pl.BlockSpec((1, 1, cap), lambda b, e: (b * E + e, 0, 0), memory_space=pltpu.SMEM),
                  pl.BlockSpec((1, S, SUBLANES, LANES), lambda b, e: (b, 0, 0, 0),
                               pipeline_mode=pl.Buffered(1))],
        out_specs=pl.BlockSpec((1, 1, cap, D), lambda b, e: (b, e, 0, 0)),
        out_shape=jax.ShapeDtypeStruct((B, E, cap, D), BF16),
        scratch_shapes=[pltpu.VMEM((cap, SUBLANES, LANES), F32)],
        compiler_params=_compiler_params(("parallel", "arbitrary")),
        name="moe_gather",
    )(idx.reshape(B * E, 1, cap), h2t)


def _moe_kernel(idx_ref, aff_ref, xg_ref, wg_ref, wu_ref, wd_ref, out_ref, y_ref):
    s = pl.program_id(1)
    n_exp = pl.num_programs(1) - 1
    cap = y_ref.shape[1]

    def ffn(slot):
        x = xg_ref[0, 0]
        ff = wg_ref.shape[3]
        y = None
        for f0 in range(0, ff, MOE_FF_CHUNK):
            f1 = f0 + MOE_FF_CHUNK
            a = _dot(x, wg_ref[0, 0, :, f0:f1].astype(BF16))
            u = _dot(x, wu_ref[0, 0, :, f0:f1].astype(BF16))
            hmid = (a * jax.nn.sigmoid(a) * u).astype(BF16)
            part = _dot(hmid, wd_ref[0, 0, f0:f1, :].astype(BF16))
            y = part if y is None else y + part
        y_ref[slot] = _rows_to_tiles(y)

    def scatter(slot):
        for g in range(cap // ROW_BATCH):
            toks = [idx_ref[0, 0, g * ROW_BATCH + r] for r in range(ROW_BATCH)]
            old = [out_ref[0, t] for t in toks]
            for r, t in enumerate(toks):
                out_ref[0, t] = old[r] + y_ref[slot, g * ROW_BATCH + r] * aff_ref[0, 0, t]

    @pl.when(s == 0)
    def _():
        out_ref[...] = jnp.zeros(out_ref.shape, out_ref.dtype)
        ffn(0)

    @pl.when(jnp.logical_and(s > 0, s < n_exp))
    def _():
        slot = s % 2
        scatter(1 - slot)
        ffn(slot)

    @pl.when(s == n_exp)
    def _():
        scatter((n_exp - 1) % 2)


def _moe_call(layer, idx, aff, xg, w_gate, w_up, w_down, S):
    B, E, cap, D = xg.shape
    F = w_gate.shape[3]
    cur = lambda s: jnp.minimum(s, E - 1)
    prev = lambda s: jnp.maximum(s - 1, 0)
    wspec = lambda shp: pl.BlockSpec(shp, lambda b, s: (layer, cur(s), 0, 0))
    return pl.pallas_call(
        _moe_kernel,
        grid=(B, E + 1),
        in_specs=[pl.BlockSpec((1, 1, cap), lambda b, s: (b * E + prev(s), 0, 0), memory_space=pltpu.SMEM),
                  pl.BlockSpec((1, 1, S), lambda b, s: (b * E + prev(s), 0, 0), memory_space=pltpu.SMEM),
                  pl.BlockSpec((1, 1, cap, D), lambda b, s: (b, cur(s), 0, 0)),
                  wspec((1, 1, D, F)), wspec((1, 1, D, F)), wspec((1, 1, F, D))],
        out_specs=pl.BlockSpec((1, S, SUBLANES, LANES), lambda b, s: (b, 0, 0, 0),
                               pipeline_mode=pl.Buffered(1)),
        out_shape=jax.ShapeDtypeStruct((B, S, SUBLANES, LANES), F32),
        scratch_shapes=[pltpu.VMEM((2, cap, SUBLANES, LANES), F32)],
        compiler_params=_compiler_params(("parallel", "arbitrary")),
        name="moe_ffn",
    )(idx.reshape(B * E, 1, cap), aff.reshape(B * E, 1, S), xg, w_gate, w_up, w_down)


def _final_kernel(alpha, x_ref, moe_ref, g_ref, b_ref, m5_ref, o_ref):
    o_ref[0] = _layer_norm(alpha * x_ref[0] + m5_ref[0] * _tiles_to_rows(moe_ref[0]),
                           g_ref[0], b_ref[0])


def _final_call(layer, alpha, x1, moe, ln2_g, ln2_b, mods, n_ctx):
    B, S, D = x1.shape
    T = TOK_BLOCK
    L = ln2_g.shape[0]
    ncb = n_ctx // T
    tok_in = pl.BlockSpec((1, T, D), lambda b, i: (b, i + ncb, 0))
    const = pl.BlockSpec((1, 1, D), lambda b, i: (layer, 0, 0))
    return pl.pallas_call(
        functools.partial(_final_kernel, alpha),
        grid=(B, (S - n_ctx) // T),
        in_specs=[tok_in, pl.BlockSpec((1, T, SUBLANES, LANES), lambda b, i: (b, i + ncb, 0, 0)),
                  const, const,
                  pl.BlockSpec((1, 1, D), lambda b, i: ((layer * ADA_ROWS + b) * 6 + 5, 0, 0))],
        out_specs=pl.BlockSpec((1, T, D), lambda b, i: (b, i, 0)),
        out_shape=jax.ShapeDtypeStruct((B, S - n_ctx, D), F32),
        compiler_params=_compiler_params(("parallel", "parallel")),
        name="final_norm",
    )(x1, moe, ln2_g.reshape(L, 1, D), ln2_b.reshape(L, 1, D), mods)


def _prep_weights(w_in, w_uq, w_ukv, w_router):
    L, D, _ = w_in.shape
    krs = jnp.concatenate([jnp.zeros((L, D, 64), w_in.dtype), w_in[..., 512:544],
                           jnp.zeros((L, D, 32), w_in.dtype)], axis=-1)
    w_in_p = jnp.concatenate([w_in[..., 0:512], krs, w_in[..., 544:]], axis=-1).astype(BF16)

    kvr = w_ukv.reshape(L, MLA_KV_RANK, MLA_HEADS, MLA_NOPE + MLA_V)
    kn, vv = kvr[..., :MLA_NOPE], kvr[..., MLA_NOPE:]
    zk = jnp.zeros_like(kn)
    w_k = jnp.concatenate([kn, zk], axis=-1).reshape(L, MLA_KV_RANK, MLA_HEADS * LANES)
    v_even = jnp.concatenate([vv, zk], axis=-1)
    v_odd = jnp.concatenate([zk, vv], axis=-1)
    odd = (jnp.arange(MLA_HEADS) % 2 == 1)[None, None, :, None]
    w_v = jnp.where(odd, v_odd, v_even).reshape(L, MLA_KV_RANK, MLA_HEADS * LANES)
    w_kv = jnp.concatenate([w_k, w_v], axis=-1).astype(BF16)

    qr = w_uq.reshape(L, MLA_Q_RANK, MLA_HEADS, MLA_NOPE + MLA_ROPE)
    w_q = jnp.concatenate([qr, jnp.zeros((L, MLA_Q_RANK, MLA_HEADS, 32), w_uq.dtype)], axis=-1)
    w_q = w_q.reshape(L, MLA_Q_RANK, MLA_HEADS * LANES).astype(BF16)

    w_r = jnp.concatenate(
        [w_router, jnp.zeros((L, D, LANES - N_EXPERTS), w_router.dtype)], axis=-1).astype(BF16)
    return w_in_p, w_kv, w_q, w_r


def _ones_columns():
    lane = jnp.arange(LANES)
    even = (lane == 64).astype(F32)
    odd = (lane == 0).astype(F32)
    win = jnp.concatenate([even, odd, even, odd])
    mla = jnp.concatenate([even, odd] * (MLA_HEADS // 2))
    return jnp.concatenate([win, mla])[None, :]


def _rope_tables(n_ctx, n_lat):
    pos = jnp.arange(n_lat)
    rowp = (pos // GRID_W).astype(F32)
    colp = (pos % GRID_W).astype(F32)

    def pattern(rot_dim):
        nf = rot_dim // 4
        inv = ROPE_THETA ** (-jnp.arange(nf, dtype=F32) / nf)
        ar, ac = rowp[:, None] * inv, colp[:, None] * inv
        cos = jnp.concatenate([jnp.cos(ar), jnp.cos(ar), jnp.cos(ac), jnp.cos(ac)], axis=1)
        sin = jnp.concatenate([-jnp.sin(ar), jnp.sin(ar), -jnp.sin(ac), jnp.sin(ac)], axis=1)
        return cos, sin

    cw, sw = pattern(WIN_HEAD_DIM)
    cw, sw = jnp.tile(cw, (1, 2)), jnp.tile(sw, (1, 2))
    cm, sm = pattern(MLA_ROPE)
    one64, zero64 = jnp.ones((n_lat, 64), F32), jnp.zeros((n_lat, 64), F32)
    cm = jnp.concatenate([one64, cm, one64[:, :32]], axis=1)
    sm = jnp.concatenate([zero64, sm, zero64[:, :32]], axis=1)
    sq_w, sq_m = WIN_SCALE * LOG2E, MLA_SCALE * LOG2E
    lat = jnp.concatenate([cw * sq_w, sw * sq_w, cw, sw, cm * sq_m, sm * sq_m, cm, sm], axis=1)
    ones, zeros = jnp.ones((n_ctx, LANES), F32), jnp.zeros((n_ctx, LANES), F32)
    ctx = jnp.concatenate([ones * sq_w, zeros, ones, zeros, ones * sq_m, zeros, ones, zeros], axis=1)
    return jnp.concatenate([ctx, lat], axis=0)


def kernel(x, c, ctx, c_ctx, w_ada, b_ada, w_in, attn_sink, mla_q_norm, mla_kv_norm, w_uq, w_ukv,
           w_oa, w_ob, w_out, ln1_g, ln1_b, w_router, w_exp_gate, w_exp_up, w_exp_down, ln2_g, ln2_b):
    B, n_lat, D = x.shape
    n_ctx = ctx.shape[1]
    depth = w_in.shape[0]
    S = n_ctx + n_lat
    alpha = (2 * depth) ** 0.25
    assert D == D_MODEL and n_ctx == TOK_BLOCK and n_lat % TOK_BLOCK == 0 and B + 1 <= ADA_ROWS
    cap_c = CAPACITY_FACTOR * n_ctx // N_EXPERTS
    cap_l = CAPACITY_FACTOR * n_lat // N_EXPERTS
    assert cap_l % 256 == 0 and cap_c % 8 == 0
    ncb = n_ctx // TOK_BLOCK

    cond = jnp.concatenate([c, c_ctx[None], jnp.zeros((ADA_ROWS - B - 1, D), F32)], axis=0)
    mods = _ada_call(cond.T, w_ada, b_ada, B + 1).reshape(depth * ADA_ROWS * 6, 1, D)

    w_in_p, w_kv, w_q, w_r = _prep_weights(w_in, w_uq, w_ukv, w_router)
    w_oa_b, w_ob_b, w_out_b = w_oa.astype(BF16), w_ob.astype(BF16), w_out.astype(BF16)
    tab = _rope_tables(n_ctx, n_lat)
    vones = _ones_columns()

    x_all = (ctx, x)
    moe = None
    for l in range(depth):
        x_res, (qwin, kwin, vwin, qcat, kcat, vext, sg) = _proj_call(
            l, alpha, x_all, moe, ln2_g, ln2_b, mods, tab, w_in_p, mla_kv_norm, mla_q_norm,
            w_kv, w_q, vones, ncb)
        oa = _win_call(l, attn_sink, qwin, kwin, vwin, n_ctx)
        ob = _mla_call(qcat, kcat, vext, n_ctx)
        x1, h2, lg_t = _merge_call(l, alpha, x_res, oa, ob, sg, w_oa_b, w_ob_b, w_out_b,
                                   ln1_g, ln1_b, mods, w_r, ncb)
        aff, idx4 = _route_call(lg_t, n_ctx, cap_c, cap_l)
        idx = idx4.reshape(B, N_EXPERTS, cap_c + cap_l)
        xg = _gather_call(idx, h2)
        moe = _moe_call(l, idx, aff, xg, w_exp_gate, w_exp_up, w_exp_down, S)
        x_all = x1
    return _final_call(depth - 1, alpha, x_all, moe, ln2_g, ln2_b, mods, n_ctx)
```

```python
import functools
import math

import jax
import jax.numpy as jnp
from jax import lax
from jax.experimental import pallas as pl
from jax.experimental.pallas import tpu as pltpu

D_MODEL = 1024
GRID_W = 64
WIN_HEADS = 8
WIN_KV_HEADS = 2
WIN_HEAD_DIM = 64
BAND = 128
MLA_HEADS = 8
MLA_Q_RANK = 384
MLA_KV_RANK = 256
MLA_NOPE = 64
MLA_ROPE = 32
MLA_V = 64
N_EXPERTS = 16
CAPACITY_FACTOR = 2
ROPE_THETA = 10000.0
LN_EPS = 1e-5
RMS_EPS = 1e-6
NEG_INF = -1e30
LOG2E = math.log2(math.e)
WIN_SCALE = WIN_HEAD_DIM ** -0.5
MLA_SCALE = (MLA_NOPE + MLA_ROPE) ** -0.5

LANES = 128
TOK_BLOCK = 256
VMEM_LIMIT = 56 * 1024 * 1024

BF16 = jnp.bfloat16
F32 = jnp.float32

SEG_KWIN = (0, 128)
SEG_VWIN = (128, 256)
SEG_CKV = (256, 512)
SEG_KR = (512, 640)
SEG_QWIN = (640, 1152)
SEG_CQ = (1152, 1536)
SEG_GATE = (1536, 3584)
IN_COLS_PAD = 3584


def _dot(a, b):
    return jnp.dot(a, b, preferred_element_type=F32)


def _dot_nt(a, b):
    return lax.dot_general(a, b, (((1,), (1,)), ((), ())), preferred_element_type=F32)


def _layer_norm(z, g, b):
    mu = jnp.mean(z, axis=-1, keepdims=True)
    zc = z - mu
    var = jnp.mean(zc * zc, axis=-1, keepdims=True)
    return zc * lax.rsqrt(var + LN_EPS) * g + b


def _rms_norm(x, g):
    return x * lax.rsqrt(jnp.mean(x * x, axis=-1, keepdims=True) + RMS_EPS) * g


def _rope_slab(x, cos, sin, half):
    lane = lax.broadcasted_iota(jnp.int32, x.shape, 1)
    partner = jnp.where((lane & half) == 0,
                        pltpu.roll(x, LANES - half, 1), pltpu.roll(x, half, 1))
    return x * cos + partner * sin


SUBLANES = 8


def _rows_to_tiles(y):
    n = y.shape[0]
    y3 = pltpu.einshape("r(sl)->srl", y, s=SUBLANES)
    y4 = y3.reshape(SUBLANES, n // SUBLANES, SUBLANES, LANES)
    return jnp.transpose(y4, (1, 2, 0, 3)).reshape(n, SUBLANES, LANES)


def _tiles_to_rows(x3):
    n = x3.shape[0]
    x4 = x3.reshape(n // SUBLANES, SUBLANES, SUBLANES, LANES)
    xs = jnp.transpose(x4, (2, 0, 1, 3)).reshape(SUBLANES, n, LANES)
    return pltpu.einshape("srl->r(sl)", xs)


def _compiler_params(sem):
    return pltpu.CompilerParams(dimension_semantics=sem, vmem_limit_bytes=VMEM_LIMIT)


ADA_ROWS = 8
ADA_TN = 512


def _ada_kernel(n_rows, condt_ref, w_ref, b_ref, o_ref):
    ct = condt_ref[...]
    st = ct * jax.nn.sigmoid(ct)
    w = w_ref[0]
    rows = []
    for r in range(n_rows):
        rows.append(jnp.sum(w * st[:, r:r + 1], axis=0, keepdims=True) + b_ref[0])
    rows.append(jnp.zeros((ADA_ROWS - n_rows, w.shape[1]), F32))
    o_ref[0] = jnp.concatenate(rows, axis=0)


def _ada_call(cond_t, w_ada, b_ada, n_rows):
    L, D, N = w_ada.shape
    return pl.pallas_call(
        functools.partial(_ada_kernel, n_rows),
        grid=(L, N // ADA_TN),
        in_specs=[
            pl.BlockSpec((D, ADA_ROWS), lambda l, j: (0, 0)),
            pl.BlockSpec((1, D, ADA_TN), lambda l, j: (l, 0, j)),
            pl.BlockSpec((1, 1, ADA_TN), lambda l, j: (l, 0, j)),
        ],
        out_specs=pl.BlockSpec((1, ADA_ROWS, ADA_TN), lambda l, j: (l, 0, j)),
        out_shape=jax.ShapeDtypeStruct((L, ADA_ROWS, N), F32),
        compiler_params=_compiler_params(("parallel", "parallel")),
        name="adaln_mod",
    )(cond_t, w_ada, b_ada.reshape(L, 1, N))


def _proj_kernel(has_ln, alpha, n_ctx_blocks, *refs):
    if has_ln:
        (x_ref, moe_ref, g2_ref, b2_ref, m5_ref, sh_ref, sc_ref, tab_ref, win_ref, gkv_ref, gq_ref,
         wkv_ref, wq_ref, vones_ref,
         xres_ref, qwin_ref, kwin_ref, vwin_ref, qcat_ref, kcat_ref, vext_ref, sg_ref) = refs
        z = alpha * x_ref[0] + m5_ref[0] * _tiles_to_rows(moe_ref[0])
        x = _layer_norm(z, g2_ref[0], b2_ref[0])
    else:
        (ctx_ref, x_ref, sh_ref, sc_ref, tab_ref, win_ref, gkv_ref, gq_ref, wkv_ref, wq_ref, vones_ref,
         xres_ref, qwin_ref, kwin_ref, vwin_ref, qcat_ref, kcat_ref, vext_ref, sg_ref) = refs
        x = jnp.where(pl.program_id(1) < n_ctx_blocks, ctx_ref[0], x_ref[0])
    xres_ref[0] = x
    hb = (x * (1.0 + sc_ref[0]) + sh_ref[0]).astype(BF16)

    def seg(s):
        return _dot(hb, win_ref[0, :, s[0]:s[1]])

    def tab(i):
        return tab_ref[:, i * LANES:(i + 1) * LANES]

    cq_w, sq_w, ck_w, sk_w, cq_m, sq_m, ck_m, sk_m = (tab(i) for i in range(8))

    lane_lo = lax.broadcasted_iota(jnp.int32, (x.shape[0], LANES), 1) < WIN_HEAD_DIM
    half_swap = lambda t: pltpu.roll(t, WIN_HEAD_DIM, 1)

    ckv = _rms_norm(seg(SEG_CKV), gkv_ref[0]).astype(BF16)
    cq = _rms_norm(seg(SEG_CQ), gq_ref[0]).astype(BF16)

    qw = seg(SEG_QWIN)
    q_slabs = []
    for s2 in range(WIN_HEADS // 2):
        pair = _rope_slab(qw[:, s2 * LANES:(s2 + 1) * LANES], cq_w, sq_w, 16)
        pair_sw = half_swap(pair)
        if 2 * s2 < WIN_HEADS // WIN_KV_HEADS:
            q_slabs += [jnp.where(lane_lo, pair, 0.0), jnp.where(lane_lo, pair_sw, 0.0)]
        else:
            q_slabs += [jnp.where(lane_lo, 0.0, pair_sw), jnp.where(lane_lo, 0.0, pair)]
    qwin_ref[0] = jnp.concatenate(q_slabs, axis=1).astype(BF16)

    kv = _dot(ckv, wkv_ref[0])
    kr = _rope_slab(seg(SEG_KR), ck_m, sk_m, 8)
    qm = _dot(cq, wq_ref[0])
    for h in range(MLA_HEADS):
        sl = slice(h * LANES, (h + 1) * LANES)
        kcat_ref[0, h] = (kv[:, sl] + kr).astype(BF16)
        vext_ref[0, h] = (kv[:, 1024 + h * LANES:1024 + (h + 1) * LANES]
                          + vones_ref[:, 512 + h * LANES:512 + (h + 1) * LANES]).astype(BF16)
        qcat_ref[0, h] = _rope_slab(qm[:, sl], cq_m, sq_m, 8).astype(BF16)

    sg_ref[0] = jax.nn.sigmoid(seg(SEG_GATE)).astype(BF16)
    va = seg(SEG_VWIN)
    va_sw = half_swap(va)
    vwin_ref[0] = (jnp.concatenate(
        [jnp.where(lane_lo, va, 0.0), jnp.where(lane_lo, 0.0, va_sw),
         jnp.where(lane_lo, va_sw, 0.0), jnp.where(lane_lo, 0.0, va)], axis=1)
        + vones_ref[:, 0:512]).astype(BF16)
    kwin_ref[0] = _rope_slab(seg(SEG_KWIN), ck_w, sk_w, 16).astype(BF16)


def _proj_call(layer, alpha, x_all, moe, ln2_g, ln2_b, mods, tab, w_in_p, g_kvn, g_qn, w_kv, w_q,
               vones, n_ctx_blocks):
    T = TOK_BLOCK
    has_ln = moe is not None
    if has_ln:
        B, S, D = x_all.shape
    else:
        ctx_in, x_in = x_all
        B, D = x_in.shape[0], x_in.shape[2]
        S = ctx_in.shape[1] + x_in.shape[1]
    L = w_in_p.shape[0]

    def mod_spec(k, lyr):
        def imap(b, i):
            row = jnp.where(i < n_ctx_blocks, B, b)
            return ((lyr * ADA_ROWS + row) * 6 + k, 0, 0)
        return pl.BlockSpec((1, 1, D), imap)

    tok = lambda w: pl.BlockSpec((1, T, w), lambda b, i: (b, i, 0))
    headed = pl.BlockSpec((1, MLA_HEADS, T, LANES), lambda b, i: (b, 0, i, 0))
    const = lambda shp: pl.BlockSpec(shp, lambda b, i: (layer,) + (0,) * (len(shp) - 1),
                                     pipeline_mode=pl.Buffered(1))

    if has_ln:
        prev = pl.BlockSpec((1, 1, D), lambda b, i: (layer - 1, 0, 0))
        tiled = pl.BlockSpec((1, T, SUBLANES, LANES), lambda b, i: (b, i, 0, 0))
        in_specs = [tok(D), tiled, prev, prev, mod_spec(5, layer - 1)]
        args = [x_all, moe, ln2_g.reshape(L, 1, D), ln2_b.reshape(L, 1, D), mods]
    else:
        assert ctx_in.shape[1] == n_ctx_blocks * T
        in_specs = [pl.BlockSpec((1, T, D), lambda b, i: (b, jnp.minimum(i, n_ctx_blocks - 1), 0)),
                    pl.BlockSpec((1, T, D), lambda b, i: (b, jnp.maximum(i - n_ctx_blocks, 0), 0))]
        args = [ctx_in, x_in]
    in_specs += [
        mod_spec(0, layer), mod_spec(1, layer),
        pl.BlockSpec((T, 8 * LANES), lambda b, i: (i, 0)),
        const((1, D, IN_COLS_PAD)),
        const((1, 1, MLA_KV_RANK)), const((1, 1, MLA_Q_RANK)),
        const((1, MLA_KV_RANK, 2048)), const((1, MLA_Q_RANK, 1024)),
        pl.BlockSpec((1, 1536), lambda b, i: (0, 0)),
    ]
    args += [mods, mods, tab, w_in_p, g_kvn.reshape(L, 1, -1), g_qn.reshape(L, 1, -1), w_kv, w_q, vones]

    out_specs, out_shape = [tok(D)], [jax.ShapeDtypeStruct((B, S, D), F32)]
    out_specs += [tok(WIN_HEADS * LANES), tok(LANES), tok(512), headed, headed, headed, tok(2048)]
    out_shape += [jax.ShapeDtypeStruct((B, S, WIN_HEADS * LANES), BF16),
                  jax.ShapeDtypeStruct((B, S, LANES), BF16),
                  jax.ShapeDtypeStruct((B, S, 512), BF16)]
    out_shape += [jax.ShapeDtypeStruct((B, MLA_HEADS, S, LANES), BF16)] * 3
    out_shape += [jax.ShapeDtypeStruct((B, S, 2048), BF16)]

    outs = pl.pallas_call(
        functools.partial(_proj_kernel, has_ln, alpha, n_ctx_blocks),
        grid=(B, S // T),
        in_specs=in_specs, out_specs=out_specs, out_shape=out_shape,
        compiler_params=_compiler_params(("parallel", "parallel")),
        name="proj",
    )(*args)
    return outs[0], outs[1:]


WIN_HEAD_ORDER = (0, 2, 1, 3, 4, 6, 5, 7)


def _win_kernel(layer, n_ctx, n_steps, sink_ref, q_ref, kc_ref, vc_ref, kp_ref, km_ref, kn_ref,
                vp_ref, vm_ref, vn_ref, o_ref, kall_ref, vall_ref, p_ref):
    i = pl.program_id(1)
    row = lax.broadcasted_iota(jnp.int32, (BAND, LANES), 0)
    col = lax.broadcasted_iota(jnp.int32, (BAND, LANES), 1)
    lane_lo = col < 64
    kall_ref[0:n_ctx, :] = kc_ref[0]
    vall_ref[0:n_ctx, :] = vc_ref[0]

    def scores(qb, r0, nk):
        q = q_ref[0, qb * BAND:(qb + 1) * BAND, :]
        q8 = jnp.concatenate([q[:, h * LANES:(h + 1) * LANES] for h in WIN_HEAD_ORDER], axis=0)
        return _dot_nt(q8, kall_ref[r0:r0 + nk, :])

    def finish(qb, r0, nk, masks, s):
        e_sink = []
        for c, h in enumerate(WIN_HEAD_ORDER):
            sc = s[c * BAND:(c + 1) * BAND, :]
            if masks:
                blocks = []
                for kb in range(nk // BAND):
                    blk = sc[:, kb * BAND:(kb + 1) * BAND]
                    blocks.append(jnp.where(masks[kb], blk, NEG_INF) if kb in masks else blk)
                sc = jnp.concatenate(blocks, axis=1)
            sink = sink_ref[layer, h] * LOG2E
            m = jnp.maximum(jnp.max(sc, axis=1, keepdims=True), sink)
            p_ref[qb, c * BAND:(c + 1) * BAND, 0:nk] = jnp.exp2(sc - m).astype(BF16)
            e_sink.append(jnp.exp2(sink - m))
        pairs = []
        for pi in range(4):
            acc = _dot(p_ref[qb, pi * 2 * BAND:(pi + 1) * 2 * BAND, 0:nk],
                       vall_ref[r0:r0 + nk, pi * LANES:(pi + 1) * LANES])
            es = jnp.concatenate([e_sink[2 * pi], e_sink[2 * pi + 1]], axis=0)
            den = (acc[:, 64:65] if pi % 2 == 0 else acc[:, 0:1]) + es
            pairs.append(acc / den)
        slabs = [jnp.where(lane_lo, pairs[0][:BAND], pairs[1][:BAND]),
                 jnp.where(lane_lo, pairs[0][BAND:], pairs[1][BAND:]),
                 jnp.where(lane_lo, pairs[2][:BAND], pairs[3][:BAND]),
                 jnp.where(lane_lo, pairs[2][BAND:], pairs[3][BAND:])]
        o_ref[0, qb * BAND:(qb + 1) * BAND, :] = jnp.concatenate(slabs, axis=1).astype(o_ref.dtype)

    def attend_both(args0, args1):
        s0 = scores(0, args0[0], args0[1])
        s1 = scores(1, args1[0], args1[1])
        finish(0, *args0, s0)
        finish(1, *args1, s1)

    @pl.when(i == 0)
    def _():
        attend_both((0, n_ctx, None), (0, n_ctx, None))

    @pl.when(i > 0)
    def _():
        r = n_ctx
        for k_ref, v_ref, n in ((kp_ref, vp_ref, BAND), (km_ref, vm_ref, 2 * BAND),
                                (kn_ref, vn_ref, BAND), (kc_ref, vc_ref, n_ctx)):
            kall_ref[r:r + n, :] = k_ref[0]
            vall_ref[r:r + n, :] = v_ref[0]
            r += n
        off_first = jnp.where(i == 1, 2 * LANES, 0)
        off_last = jnp.where(i == n_steps - 1, 2 * LANES, 0)
        cb = n_ctx // BAND
        attend_both((0, n_ctx + 3 * BAND, {cb: col >= row + off_first, cb + 2: col <= row}),
                    (n_ctx + BAND, n_ctx + 3 * BAND, {0: col >= row, 2: col <= row - off_last}))


def _win_call(layer, sink, qwin, kwin, vwin, n_ctx):
    B, S, _ = qwin.shape
    nb = S // BAND
    ncb = n_ctx // BAND
    assert n_ctx == 2 * BAND and nb % 2 == 0
    nk = n_ctx + 3 * BAND
    one = lambda w, f: pl.BlockSpec((1, BAND, w), lambda b, i: (b, jnp.clip(f(i), ncb, nb - 1), 0))
    two = lambda w: pl.BlockSpec((1, 2 * BAND, w), lambda b, i: (b, i, 0))
    ctx = lambda w: pl.BlockSpec((1, n_ctx, w), lambda b, i: (b, 0, 0))
    prev, nxt = (lambda i: 2 * i - 1), (lambda i: 2 * i + 2)
    return pl.pallas_call(
        functools.partial(_win_kernel, layer, n_ctx, nb // 2),
        grid=(B, nb // 2),
        in_specs=[pl.BlockSpec(memory_space=pltpu.SMEM), two(WIN_HEADS * LANES),
                  ctx(LANES), ctx(512), one(LANES, prev), two(LANES), one(LANES, nxt),
                  one(512, prev), two(512), one(512, nxt)],
        out_specs=two(512),
        out_shape=jax.ShapeDtypeStruct((B, S, 512), BF16),
        scratch_shapes=[pltpu.VMEM((2 * n_ctx + 4 * BAND, LANES), BF16),
                        pltpu.VMEM((2 * n_ctx + 4 * BAND, 512), BF16),
                        pltpu.VMEM((2, WIN_HEADS * BAND, nk), BF16)],
        compiler_params=_compiler_params(("parallel", "parallel")),
        name="win_attn",
    )(sink, qwin, kwin, vwin, kwin, kwin, kwin, vwin, vwin, vwin)


MLA_KCHUNK = 256
MLA_HEADS_PER_STEP = 4


def _mla_kernel(n_ctx, n_keys, q_ref, k_ref, v_ref, o_ref, s_ref):
    qi = pl.program_id(2)
    tq = q_ref.shape[2]
    lane = lax.broadcasted_iota(jnp.int32, (tq, LANES), 1)

    def run(nk):
        row_max = []
        for hh in range(MLA_HEADS_PER_STEP):
            q = q_ref[0, hh]
            mrun = None
            for c in range(nk // MLA_KCHUNK):
                s = _dot_nt(q, k_ref[0, hh, c * MLA_KCHUNK:(c + 1) * MLA_KCHUNK, :])
                s_ref[hh, c] = s
                mc = jnp.maximum(s[:, :LANES], s[:, LANES:])
                mrun = mc if mrun is None else jnp.maximum(mrun, mc)
            row_max.append(jnp.max(mrun, axis=1, keepdims=True))
        outs = []
        for hh in range(MLA_HEADS_PER_STEP):
            acc = jnp.zeros((tq, LANES), F32)
            for c in range(nk // MLA_KCHUNK):
                p = jnp.exp2(s_ref[hh, c] - row_max[hh]).astype(BF16)
                acc = acc + _dot(p, v_ref[0, hh, c * MLA_KCHUNK:(c + 1) * MLA_KCHUNK, :])
            den = acc[:, 64:65] if hh % 2 == 0 else acc[:, 0:1]
            outs.append(acc / den)
        o_ref[0] = jnp.concatenate(
            [jnp.where(lane < 64, outs[2 * j], outs[2 * j + 1]) for j in range(len(outs) // 2)],
            axis=1).astype(o_ref.dtype)

    @pl.when(qi == 0)
    def _():
        run(n_ctx)

    @pl.when(qi > 0)
    def _():
        run(n_keys)


def _mla_call(qcat, kcat, vext, n_ctx):
    B, H, S, _ = qcat.shape
    T = TOK_BLOCK
    assert n_ctx == T
    G = MLA_HEADS_PER_STEP
    kv = pl.BlockSpec((1, G, S, LANES), lambda b, hp, i: (b, hp, 0, 0))
    return pl.pallas_call(
        functools.partial(_mla_kernel, n_ctx, S),
        grid=(B, H // G, S // T),
        in_specs=[pl.BlockSpec((1, G, T, LANES), lambda b, hp, i: (b, hp, i, 0)), kv, kv],
        out_specs=pl.BlockSpec((1, T, (G // 2) * LANES), lambda b, hp, i: (b, i, hp)),
        out_shape=jax.ShapeDtypeStruct((B, S, (H // 2) * LANES), BF16),
        scratch_shapes=[pltpu.VMEM((G, S // MLA_KCHUNK, T, MLA_KCHUNK), F32)],
        compiler_params=_compiler_params(("parallel", "parallel", "arbitrary")),
        name="mla_attn",
    )(qcat, kcat, vext)


def _merge_kernel(alpha, x_ref, oa_ref, ob_ref, sg_ref, woa_ref, wob_ref, wout_ref, g1_ref, b1_ref,
                  m2_ref, m3_ref, m4_ref, wr_ref, x1_ref, h2_ref, lg_ref):
    D = x_ref.shape[2]
    sg = sg_ref[0]
    t = (sg[:, :D].astype(F32) * _dot(oa_ref[0], woa_ref[0])
         + sg[:, D:].astype(F32) * _dot(ob_ref[0], wob_ref[0]))
    y = _dot(t.astype(BF16), wout_ref[0])
    x1 = _layer_norm(alpha * x_ref[0] + m2_ref[0] * y, g1_ref[0], b1_ref[0])
    x1_ref[0] = x1
    h2 = x1 * (1.0 + m4_ref[0]) + m3_ref[0]
    h2_ref[0] = _rows_to_tiles(h2)
    lg = _dot(h2.astype(BF16), wr_ref[0])
    lg_ref[0] = lg.T[:N_EXPERTS]


def _merge_call(layer, alpha, x_res, oa, ob, sg, w_oa, w_ob, w_out, ln1_g, ln1_b, mods, w_r,
                n_ctx_blocks):
    B, S, D = x_res.shape
    T = TOK_BLOCK
    L = w_oa.shape[0]

    def mod_spec(k):
        def imap(b, i):
            row = jnp.where(i < n_ctx_blocks, B, b)
            return ((layer * ADA_ROWS + row) * 6 + k, 0, 0)
        return pl.BlockSpec((1, 1, D), imap)

    tok = lambda w: pl.BlockSpec((1, T, w), lambda b, i: (b, i, 0))
    const = lambda shp: pl.BlockSpec(shp, lambda b, i: (layer,) + (0,) * (len(shp) - 1),
                                     pipeline_mode=pl.Buffered(1))
    return pl.pallas_call(
        functools.partial(_merge_kernel, alpha),
        grid=(B, S // T),
        in_specs=[tok(D), tok(512), tok(512), tok(2048),
                  const((1, 512, D)), const((1, 512, D)), const((1, D, D)),
                  const((1, 1, D)), const((1, 1, D)),
                  mod_spec(2), mod_spec(3), mod_spec(4),
                  const((1, D, LANES))],
        out_specs=[tok(D), pl.BlockSpec((1, T, SUBLANES, LANES), lambda b, i: (b, i, 0, 0)),
                   pl.BlockSpec((1, N_EXPERTS, T), lambda b, i: (b, 0, i))],
        out_shape=[jax.ShapeDtypeStruct((B, S, D), F32),
                   jax.ShapeDtypeStruct((B, S, SUBLANES, LANES), F32),
                   jax.ShapeDtypeStruct((B, N_EXPERTS, S), F32)],
        compiler_params=_compiler_params(("parallel", "parallel")),
        name="merge",
    )(x_res, oa, ob, sg, w_oa, w_ob, w_out, ln1_g.reshape(L, 1, D), ln1_b.reshape(L, 1, D),
      mods, mods, mods, w_r)


def _cumsum_lanes(x):
    n = x.shape[1]
    xb = x.astype(BF16)
    r = lax.broadcasted_iota(jnp.int32, (LANES, LANES), 0)
    c = lax.broadcasted_iota(jnp.int32, (LANES, LANES), 1)
    tri = jnp.where(r <= c, 1.0, 0.0).astype(BF16)
    tb = lax.broadcasted_iota(jnp.int32, (n, LANES), 0) // LANES
    kb = lax.broadcasted_iota(jnp.int32, (n, LANES), 1)
    before = jnp.where(tb < kb, 1.0, 0.0).astype(BF16)
    off = _dot(xb, before)
    outs = []
    for k in range(n // LANES):
        outs.append(_dot(xb[:, k * LANES:(k + 1) * LANES], tri) + off[:, k:k + 1])
    return outs


def _select_top(aff, cap):
    bits = pltpu.bitcast(aff, jnp.int32)
    thr = jnp.zeros((aff.shape[0], 1), jnp.int32)
    for bit in range(30, -1, -1):
        cand = thr | (1 << bit)
        cnt = jnp.sum(jnp.where(bits >= cand, 1.0, 0.0), axis=1, keepdims=True)
        thr = jnp.where(cnt >= cap, cand, thr)
    gt = bits > thr
    eq = jnp.where(bits == thr, 1.0, 0.0)
    need = cap - jnp.sum(jnp.where(gt, 1.0, 0.0), axis=1, keepdims=True)
    eq_rank = jnp.concatenate(_cumsum_lanes(eq), axis=1) - eq
    sel = jnp.where(jnp.logical_or(gt, jnp.logical_and(eq > 0.5, eq_rank < need)), 1.0, 0.0)
    return _cumsum_lanes(sel)


ROUTE_PAD = 1e6


def _route_kernel(n_ctx, cap_c, cap_l, lg_ref, aff_ref, idx_ref, cc_ref, cl_ref):
    lg = lg_ref[0]
    m = jnp.max(lg, axis=0, keepdims=True)
    ex = jnp.exp(lg - m)
    aff = ex / jnp.sum(ex, axis=0, keepdims=True)
    aff_ref[0] = aff
    for c_ref, blocks in ((cc_ref, _select_top(aff[:, :n_ctx], cap_c)),
                          (cl_ref, _select_top(aff[:, n_ctx:], cap_l))):
        c_ref[...] = jnp.full(c_ref.shape, ROUTE_PAD, F32)
        for k, blk in enumerate(blocks):
            for ex_i in range(N_EXPERTS):
                c_ref[ex_i, k:k + 1, :] = blk[ex_i:ex_i + 1, :]

    def slots(c_ref, e, n_blocks, width):
        cm = c_ref[e]
        jrow = lax.broadcasted_iota(jnp.int32, (1, width), 1).astype(F32)
        rows = max(n_blocks, SUBLANES)
        cend = cm[0:rows, LANES - 1:LANES]
        blk = jnp.sum(jnp.where(cend <= jrow, 1.0, 0.0), axis=0, keepdims=True)
        kio = lax.broadcasted_iota(jnp.int32, (LANES, width), 0).astype(F32)
        onehot = jnp.where(kio == blk, 1.0, 0.0).astype(BF16)
        cmt = cm.T
        hi = jnp.floor(cmt * (1.0 / 256.0))
        lo = cmt - 256.0 * hi
        cg = 256.0 * _dot(hi.astype(BF16), onehot) + _dot(lo.astype(BF16), onehot)
        inside = jnp.sum(jnp.where(cg <= jrow, 1.0, 0.0), axis=0, keepdims=True)
        return (blk * LANES + inside).astype(jnp.int32)

    def per_expert(e, carry):
        n_lat_blocks = (lg_ref.shape[2] - n_ctx) // LANES
        idx_ref[0, e, :, 0:cap_l] = slots(cl_ref, e, n_lat_blocks, cap_l) + n_ctx
        idx_ref[0, e, :, cap_l:cap_l + cap_c] = slots(cc_ref, e, n_ctx // LANES, LANES)[:, 0:cap_c]
        return carry

    lax.fori_loop(0, N_EXPERTS, per_expert, 0)


def _route_call(lg_t, n_ctx, cap_c, cap_l):
    B, E, S = lg_t.shape
    cap = cap_c + cap_l
    assert (S - n_ctx) // LANES <= LANES and cap_l % LANES == 0 and cap_c <= LANES
    return pl.pallas_call(
        functools.partial(_route_kernel, n_ctx, cap_c, cap_l),
        grid=(B,),
        in_specs=[pl.BlockSpec((1, E, S), lambda b: (b, 0, 0))],
        out_specs=[pl.BlockSpec((1, E, S), lambda b: (b, 0, 0)),
                   pl.BlockSpec((1, E, 1, cap), lambda b: (b, 0, 0, 0))],
        out_shape=[jax.ShapeDtypeStruct((B, E, S), F32),
                   jax.ShapeDtypeStruct((B, E, 1, cap), jnp.int32)],
        scratch_shapes=[pltpu.VMEM((E, LANES, LANES), F32), pltpu.VMEM((E, LANES, LANES), F32)],
        compiler_params=_compiler_params(("parallel",)),
        name="route",
    )(lg_t)


ROW_BATCH = 16
MOE_FF_CHUNK = 512


def _gather_kernel(idx_ref, h_ref, xg_ref, tmp_ref):
    cap = xg_ref.shape[2]

    def body(g, carry):
        for r in range(ROW_BATCH):
            j = g * ROW_BATCH + r
            tmp_ref[j] = h_ref[0, idx_ref[0, 0, j]]
        return carry

    lax.fori_loop(0, cap // ROW_BATCH, body, 0)
    xg_ref[0, 0] = _tiles_to_rows(tmp_ref[...]).astype(xg_ref.dtype)


def _gather_call(idx, h2t):
    B, S = h2t.shape[0], h2t.shape[1]
    D = SUBLANES * LANES
    E, cap = idx.shape[1], idx.shape[2]
    assert cap % ROW_BATCH == 0
    return pl.pallas_call(
        _gather_kernel,
        grid=(B, E),
        in_specs=[pl.BlockSpec((1, 1, cap), lambda b, e: (b * E + e, 0, 0), memory_space=pltpu.SMEM),
                  pl.BlockSpec((1, S, SUBLANES, LANES), lambda b, e: (b, 0, 0, 0),
                               pipeline_mode=pl.Buffered(1))],
        out_specs=pl.BlockSpec((1, 1, cap, D), lambda b, e: (b, e, 0, 0)),
        out_shape=jax.ShapeDtypeStruct((B, E, cap, D), BF16),
        scratch_shapes=[pltpu.VMEM((cap, SUBLANES, LANES), F32)],
        compiler_params=_compiler_params(("parallel", "arbitrary")),
        name="moe_gather",
    )(idx.reshape(B * E, 1, cap), h2t)


def _moe_kernel(idx_ref, aff_ref, xg_ref, wg_ref, wu_ref, wd_ref, out_ref, y_ref):
    s = pl.program_id(1)
    n_exp = pl.num_programs(1) - 1
    cap = y_ref.shape[1]

    def ffn(slot):
        x = xg_ref[0, 0]
        ff = wg_ref.shape[3]
        y = None
        for f0 in range(0, ff, MOE_FF_CHUNK):
            f1 = f0 + MOE_FF_CHUNK
            a = _dot(x, wg_ref[0, 0, :, f0:f1].astype(BF16))
            u = _dot(x, wu_ref[0, 0, :, f0:f1].astype(BF16))
            hmid = (a * jax.nn.sigmoid(a) * u).astype(BF16)
            part = _dot(hmid, wd_ref[0, 0, f0:f1, :].astype(BF16))
            y = part if y is None else y + part
        y_ref[slot] = _rows_to_tiles(y)

    def scatter(slot):
        for g in range(cap // ROW_BATCH):
            toks = [idx_ref[0, 0, g * ROW_BATCH + r] for r in range(ROW_BATCH)]
            old = [out_ref[0, t] for t in toks]
            for r, t in enumerate(toks):
                out_ref[0, t] = old[r] + y_ref[slot, g * ROW_BATCH + r] * aff_ref[0, 0, t]

    @pl.when(s == 0)
    def _():
        out_ref[...] = jnp.zeros(out_ref.shape, out_ref.dtype)
        ffn(0)

    @pl.when(jnp.logical_and(s > 0, s < n_exp))
    def _():
        slot = s % 2
        scatter(1 - slot)
        ffn(slot)

    @pl.when(s == n_exp)
    def _():
        scatter((n_exp - 1) % 2)


def _moe_call(layer, idx, aff, xg, w_gate, w_up, w_down, S):
    B, E, cap, D = xg.shape
    F = w_gate.shape[3]
    cur = lambda s: jnp.minimum(s, E - 1)
    prev = lambda s: jnp.maximum(s - 1, 0)
    wspec = lambda shp: pl.BlockSpec(shp, lambda b, s: (layer, cur(s), 0, 0))
    return pl.pallas_call(
        _moe_kernel,
        grid=(B, E + 1),
        in_specs=[pl.BlockSpec((1, 1, cap), lambda b, s: (b * E + prev(s), 0, 0), memory_space=pltpu.SMEM),
                  pl.BlockSpec((1, 1, S), lambda b, s: (b * E + prev(s), 0, 0), memory_space=pltpu.SMEM),
                  pl.BlockSpec((1, 1, cap, D), lambda b, s: (b, cur(s), 0, 0)),
                  wspec((1, 1, D, F)), wspec((1, 1, D, F)), wspec((1, 1, F, D))],
        out_specs=pl.BlockSpec((1, S, SUBLANES, LANES), lambda b, s: (b, 0, 0, 0),
                               pipeline_mode=pl.Buffered(1)),
        out_shape=jax.ShapeDtypeStruct((B, S, SUBLANES, LANES), F32),
        scratch_shapes=[pltpu.VMEM((2, cap, SUBLANES, LANES), F32)],
        compiler_params=_compiler_params(("parallel", "arbitrary")),
        name="moe_ffn",
    )(idx.reshape(B * E, 1, cap), aff.reshape(B * E, 1, S), xg, w_gate, w_up, w_down)


def _final_kernel(alpha, x_ref, moe_ref, g_ref, b_ref, m5_ref, o_ref):
    o_ref[0] = _layer_norm(alpha * x_ref[0] + m5_ref[0] * _tiles_to_rows(moe_ref[0]),
                           g_ref[0], b_ref[0])


def _final_call(layer, alpha, x1, moe, ln2_g, ln2_b, mods, n_ctx):
    B, S, D = x1.shape
    T = TOK_BLOCK
    L = ln2_g.shape[0]
    ncb = n_ctx // T
    tok_in = pl.BlockSpec((1, T, D), lambda b, i: (b, i + ncb, 0))
    const = pl.BlockSpec((1, 1, D), lambda b, i: (layer, 0, 0))
    return pl.pallas_call(
        functools.partial(_final_kernel, alpha),
        grid=(B, (S - n_ctx) // T),
        in_specs=[tok_in, pl.BlockSpec((1, T, SUBLANES, LANES), lambda b, i: (b, i + ncb, 0, 0)),
                  const, const,
                  pl.BlockSpec((1, 1, D), lambda b, i: ((layer * ADA_ROWS + b) * 6 + 5, 0, 0))],
        out_specs=pl.BlockSpec((1, T, D), lambda b, i: (b, i, 0)),
        out_shape=jax.ShapeDtypeStruct((B, S - n_ctx, D), F32),
        compiler_params=_compiler_params(("parallel", "parallel")),
        name="final_norm",
    )(x1, moe, ln2_g.reshape(L, 1, D), ln2_b.reshape(L, 1, D), mods)


def _prep_weights(w_in, w_uq, w_ukv, w_router):
    L, D, _ = w_in.shape
    krs = jnp.concatenate([jnp.zeros((L, D, 64), w_in.dtype), w_in[..., 512:544],
                           jnp.zeros((L, D, 32), w_in.dtype)], axis=-1)
    w_in_p = jnp.concatenate([w_in[..., 0:512], krs, w_in[..., 544:]], axis=-1).astype(BF16)

    kvr = w_ukv.reshape(L, MLA_KV_RANK, MLA_HEADS, MLA_NOPE + MLA_V)
    kn, vv = kvr[..., :MLA_NOPE], kvr[..., MLA_NOPE:]
    zk = jnp.zeros_like(kn)
    w_k = jnp.concatenate([kn, zk], axis=-1).reshape(L, MLA_KV_RANK, MLA_HEADS * LANES)
    v_even = jnp.concatenate([vv, zk], axis=-1)
    v_odd = jnp.concatenate([zk, vv], axis=-1)
    odd = (jnp.arange(MLA_HEADS) % 2 == 1)[None, None, :, None]
    w_v = jnp.where(odd, v_odd, v_even).reshape(L, MLA_KV_RANK, MLA_HEADS * LANES)
    w_kv = jnp.concatenate([w_k, w_v], axis=-1).astype(BF16)

    qr = w_uq.reshape(L, MLA_Q_RANK, MLA_HEADS, MLA_NOPE + MLA_ROPE)
    w_q = jnp.concatenate([qr, jnp.zeros((L, MLA_Q_RANK, MLA_HEADS, 32), w_uq.dtype)], axis=-1)
    w_q = w_q.reshape(L, MLA_Q_RANK, MLA_HEADS * LANES).astype(BF16)

    w_r = jnp.concatenate(
        [w_router, jnp.zeros((L, D, LANES - N_EXPERTS), w_router.dtype)], axis=-1).astype(BF16)
    return w_in_p, w_kv, w_q, w_r


def _ones_columns():
    lane = jnp.arange(LANES)
    even = (lane == 64).astype(F32)
    odd = (lane == 0).astype(F32)
    win = jnp.concatenate([even, odd, even, odd])
    mla = jnp.concatenate([even, odd] * (MLA_HEADS // 2))
    return jnp.concatenate([win, mla])[None, :]


def _rope_tables(n_ctx, n_lat):
    pos = jnp.arange(n_lat)
    rowp = (pos // GRID_W).astype(F32)
    colp = (pos % GRID_W).astype(F32)

    def pattern(rot_dim):
        nf = rot_dim // 4
        inv = ROPE_THETA ** (-jnp.arange(nf, dtype=F32) / nf)
        ar, ac = rowp[:, None] * inv, colp[:, None] * inv
        cos = jnp.concatenate([jnp.cos(ar), jnp.cos(ar), jnp.cos(ac), jnp.cos(ac)], axis=1)
        sin = jnp.concatenate([-jnp.sin(ar), jnp.sin(ar), -jnp.sin(ac), jnp.sin(ac)], axis=1)
        return cos, sin

    cw, sw = pattern(WIN_HEAD_DIM)
    cw, sw = jnp.tile(cw, (1, 2)), jnp.tile(sw, (1, 2))
    cm, sm = pattern(MLA_ROPE)
    one64, zero64 = jnp.ones((n_lat, 64), F32), jnp.zeros((n_lat, 64), F32)
    cm = jnp.concatenate([one64, cm, one64[:, :32]], axis=1)
    sm = jnp.concatenate([zero64, sm, zero64[:, :32]], axis=1)
    sq_w, sq_m = WIN_SCALE * LOG2E, MLA_SCALE * LOG2E
    lat = jnp.concatenate([cw * sq_w, sw * sq_w, cw, sw, cm * sq_m, sm * sq_m, cm, sm], axis=1)
    ones, zeros = jnp.ones((n_ctx, LANES), F32), jnp.zeros((n_ctx, LANES), F32)
    ctx = jnp.concatenate([ones * sq_w, zeros, ones, zeros, ones * sq_m, zeros, ones, zeros], axis=1)
    return jnp.concatenate([ctx, lat], axis=0)


def kernel(x, c, ctx, c_ctx, w_ada, b_ada, w_in, attn_sink, mla_q_norm, mla_kv_norm, w_uq, w_ukv,
           w_oa, w_ob, w_out, ln1_g, ln1_b, w_router, w_exp_gate, w_exp_up, w_exp_down, ln2_g, ln2_b):
    B, n_lat, D = x.shape
    n_ctx = ctx.shape[1]
    depth = w_in.shape[0]
    S = n_ctx + n_lat
    alpha = (2 * depth) ** 0.25
    assert D == D_MODEL and n_ctx == TOK_BLOCK and n_lat % TOK_BLOCK == 0 and B + 1 <= ADA_ROWS
    cap_c = CAPACITY_FACTOR * n_ctx // N_EXPERTS
    cap_l = CAPACITY_FACTOR * n_lat // N_EXPERTS
    assert cap_l % 256 == 0 and cap_c % 8 == 0
    ncb = n_ctx // TOK_BLOCK

    cond = jnp.concatenate([c, c_ctx[None], jnp.zeros((ADA_ROWS - B - 1, D), F32)], axis=0)
    mods = _ada_call(cond.T, w_ada, b_ada, B + 1).reshape(depth * ADA_ROWS * 6, 1, D)

    w_in_p, w_kv, w_q, w_r = _prep_weights(w_in, w_uq, w_ukv, w_router)
    w_oa_b, w_ob_b, w_out_b = w_oa.astype(BF16), w_ob.astype(BF16), w_out.astype(BF16)
    tab = _rope_tables(n_ctx, n_lat)
    vones = _ones_columns()

    x_all = (ctx, x)
    moe = None
    for l in range(depth):
        x_res, (qwin, kwin, vwin, qcat, kcat, vext, sg) = _proj_call(
            l, alpha, x_all, moe, ln2_g, ln2_b, mods, tab, w_in_p, mla_kv_norm, mla_q_norm,
            w_kv, w_q, vones, ncb)
        oa = _win_call(l, attn_sink, qwin, kwin, vwin, n_ctx)
        ob = _mla_call(qcat, kcat, vext, n_ctx)
        x1, h2, lg_t = _merge_call(l, alpha, x_res, oa, ob, sg, w_oa_b, w_ob_b, w_out_b,
                                   ln1_g, ln1_b, mods, w_r, ncb)
        aff, idx4 = _route_call(lg_t, n_ctx, cap_c, cap_l)
        idx = idx4.reshape(B, N_EXPERTS, cap_c + cap_l)
        xg = _gather_call(idx, h2)
        moe = _moe_call(l, idx, aff, xg, w_exp_gate, w_exp_up, w_exp_down, S)
        x_all = x1
    return _final_call(depth - 1, alpha, x_all, moe, ln2_g, ln2_b, mods, n_ctx)
```

```python
import functools
import math

import jax
import jax.numpy as jnp
from jax import lax
from jax.experimental import pallas as pl
from jax.experimental.pallas import tpu as pltpu

D_MODEL = 1024
GRID_W = 64
WIN_HEADS = 8
WIN_KV_HEADS = 2
WIN_HEAD_DIM = 64
BAND = 128
MLA_HEADS = 8
MLA_Q_RANK = 384
MLA_KV_RANK = 256
MLA_NOPE = 64
MLA_ROPE = 32
MLA_V = 64
N_EXPERTS = 16
CAPACITY_FACTOR = 2
ROPE_THETA = 10000.0
LN_EPS = 1e-5
RMS_EPS = 1e-6
NEG_INF = -1e30
LOG2E = math.log2(math.e)
WIN_SCALE = WIN_HEAD_DIM ** -0.5
MLA_SCALE = (MLA_NOPE + MLA_ROPE) ** -0.5

LANES = 128
TOK_BLOCK = 256
VMEM_LIMIT = 56 * 1024 * 1024

BF16 = jnp.bfloat16
F32 = jnp.float32

SEG_KWIN = (0, 128)
SEG_VWIN = (128, 256)
SEG_CKV = (256, 512)
SEG_KR = (512, 640)
SEG_QWIN = (640, 1152)
SEG_CQ = (1152, 1536)
SEG_GATE = (1536, 3584)
IN_COLS_PAD = 3584


def _dot(a, b):
    return jnp.dot(a, b, preferred_element_type=F32)


def _dot_nt(a, b):
    return lax.dot_general(a, b, (((1,), (1,)), ((), ())), preferred_element_type=F32)


def _layer_norm(z, g, b):
    mu = jnp.mean(z, axis=-1, keepdims=True)
    zc = z - mu
    var = jnp.mean(zc * zc, axis=-1, keepdims=True)
    return zc * lax.rsqrt(var + LN_EPS) * g + b


def _rms_norm(x, g):
    return x * lax.rsqrt(jnp.mean(x * x, axis=-1, keepdims=True) + RMS_EPS) * g


def _rope_slab(x, cos, sin, half):
    lane = lax.broadcasted_iota(jnp.int32, x.shape, 1)
    partner = jnp.where((lane & half) == 0,
                        pltpu.roll(x, LANES - half, 1), pltpu.roll(x, half, 1))
    return x * cos + partner * sin


SUBLANES = 8


def _rows_to_tiles(y):
    n = y.shape[0]
    y3 = pltpu.einshape("r(sl)->srl", y, s=SUBLANES)
    y4 = y3.reshape(SUBLANES, n // SUBLANES, SUBLANES, LANES)
    return jnp.transpose(y4, (1, 2, 0, 3)).reshape(n, SUBLANES, LANES)


def _tiles_to_rows(x3):
    n = x3.shape[0]
    x4 = x3.reshape(n // SUBLANES, SUBLANES, SUBLANES, LANES)
    xs = jnp.transpose(x4, (2, 0, 1, 3)).reshape(SUBLANES, n, LANES)
    return pltpu.einshape("srl->r(sl)", xs)


def _compiler_params(sem):
    return pltpu.CompilerParams(dimension_semantics=sem, vmem_limit_bytes=VMEM_LIMIT)


ADA_ROWS = 8
ADA_TN = 512


def _ada_kernel(n_rows, condt_ref, w_ref, b_ref, o_ref):
    ct = condt_ref[...]
    st = ct * jax.nn.sigmoid(ct)
    w = w_ref[0]
    rows = []
    for r in range(n_rows):
        rows.append(jnp.sum(w * st[:, r:r + 1], axis=0, keepdims=True) + b_ref[0])
    rows.append(jnp.zeros((ADA_ROWS - n_rows, w.shape[1]), F32))
    o_ref[0] = jnp.concatenate(rows, axis=0)


def _ada_call(cond_t, w_ada, b_ada, n_rows):
    L, D, N = w_ada.shape
    return pl.pallas_call(
        functools.partial(_ada_kernel, n_rows),
        grid=(L, N // ADA_TN),
        in_specs=[
            pl.BlockSpec((D, ADA_ROWS), lambda l, j: (0, 0)),
            pl.BlockSpec((1, D, ADA_TN), lambda l, j: (l, 0, j)),
            pl.BlockSpec((1, 1, ADA_TN), lambda l, j: (l, 0, j)),
        ],
        out_specs=pl.BlockSpec((1, ADA_ROWS, ADA_TN), lambda l, j: (l, 0, j)),
        out_shape=jax.ShapeDtypeStruct((L, ADA_ROWS, N), F32),
        compiler_params=_compiler_params(("parallel", "parallel")),
        name="adaln_mod",
    )(cond_t, w_ada, b_ada.reshape(L, 1, N))


def _proj_kernel(has_ln, alpha, n_ctx_blocks, *refs):
    if has_ln:
        (x_ref, moe_ref, g2_ref, b2_ref, m5_ref, sh_ref, sc_ref, tab_ref, win_ref, gkv_ref, gq_ref,
         wkv_ref, wq_ref, vones_ref,
         xres_ref, qwin_ref, kwin_ref, vwin_ref, qcat_ref, kcat_ref, vext_ref, sg_ref, hb_ref) = refs
    else:
        (ctx_ref, x_ref, sh_ref, sc_ref, tab_ref, win_ref, gkv_ref, gq_ref, wkv_ref, wq_ref, vones_ref,
         xres_ref, qwin_ref, kwin_ref, vwin_ref, qcat_ref, kcat_ref, vext_ref, sg_ref, hb_ref) = refs
    step = pl.program_id(1)
    n_blocks = pl.num_programs(1) - 1

    def modulated():
        if has_ln:
            z = alpha * x_ref[0] + m5_ref[0] * _tiles_to_rows(moe_ref[0])
            x = _layer_norm(z, g2_ref[0], b2_ref[0])
        else:
            x = jnp.where(step < n_ctx_blocks, ctx_ref[0], x_ref[0])
        xres_ref[0] = x
        return (x * (1.0 + sc_ref[0]) + sh_ref[0]).astype(BF16)

    @pl.when(step == 0)
    def _():
        hb_ref[...] = modulated()

    @pl.when(jnp.logical_and(step > 0, step < n_blocks))
    def _():
        _project(hb_ref[...], tab_ref, win_ref, gkv_ref, gq_ref, wkv_ref, wq_ref, vones_ref,
                 qwin_ref, kwin_ref, vwin_ref, qcat_ref, kcat_ref, vext_ref, sg_ref)
        hb_ref[...] = modulated()

    @pl.when(step == n_blocks)
    def _():
        _project(hb_ref[...], tab_ref, win_ref, gkv_ref, gq_ref, wkv_ref, wq_ref, vones_ref,
                 qwin_ref, kwin_ref, vwin_ref, qcat_ref, kcat_ref, vext_ref, sg_ref)


def _project(hb, tab_ref, win_ref, gkv_ref, gq_ref, wkv_ref, wq_ref, vones_ref,
             qwin_ref, kwin_ref, vwin_ref, qcat_ref, kcat_ref, vext_ref, sg_ref):
    def seg(s):
        return _dot(hb, win_ref[0, :, s[0]:s[1]])

    def tab(i):
        return tab_ref[:, i * LANES:(i + 1) * LANES]

    cq_w, sq_w, ck_w, sk_w, cq_m, sq_m, ck_m, sk_m = (tab(i) for i in range(8))

    lane_lo = lax.broadcasted_iota(jnp.int32, (hb.shape[0], LANES), 1) < WIN_HEAD_DIM
    half_swap = lambda t: pltpu.roll(t, WIN_HEAD_DIM, 1)

    ckv = _rms_norm(seg(SEG_CKV), gkv_ref[0]).astype(BF16)
    cq = _rms_norm(seg(SEG_CQ), gq_ref[0]).astype(BF16)

    qw = seg(SEG_QWIN)
    q_slabs = []
    for s2 in range(WIN_HEADS // 2):
        pair = _rope_slab(qw[:, s2 * LANES:(s2 + 1) * LANES], cq_w, sq_w, 16)
        pair_sw = half_swap(pair)
        if 2 * s2 < WIN_HEADS // WIN_KV_HEADS:
            q_slabs += [jnp.where(lane_lo, pair, 0.0), jnp.where(lane_lo, pair_sw, 0.0)]
        else:
            q_slabs += [jnp.where(lane_lo, 0.0, pair_sw), jnp.where(lane_lo, 0.0, pair)]
    qwin_ref[0] = jnp.concatenate(q_slabs, axis=1).astype(BF16)

    kv = _dot(ckv, wkv_ref[0])
    kr = _rope_slab(seg(SEG_KR), ck_m, sk_m, 8)
    qm = _dot(cq, wq_ref[0])
    for h in range(MLA_HEADS):
        sl = slice(h * LANES, (h + 1) * LANES)
        kcat_ref[0, h] = (kv[:, sl] + kr).astype(BF16)
        vext_ref[0, h] = (kv[:, 1024 + h * LANES:1024 + (h + 1) * LANES]
                          + vones_ref[:, 512 + h * LANES:512 + (h + 1) * LANES]).astype(BF16)
        qcat_ref[0, h] = _rope_slab(qm[:, sl], cq_m, sq_m, 8).astype(BF16)

    sg_ref[0] = jax.nn.sigmoid(seg(SEG_GATE)).astype(BF16)
    va = seg(SEG_VWIN)
    va_sw = half_swap(va)
    vwin_ref[0] = (jnp.concatenate(
        [jnp.where(lane_lo, va, 0.0), jnp.where(lane_lo, 0.0, va_sw),
         jnp.where(lane_lo, va_sw, 0.0), jnp.where(lane_lo, 0.0, va)], axis=1)
        + vones_ref[:, 0:512]).astype(BF16)
    kwin_ref[0] = _rope_slab(seg(SEG_KWIN), ck_w, sk_w, 16).astype(BF16)


def _proj_call(layer, alpha, x_all, moe, ln2_g, ln2_b, mods, tab, w_in_p, g_kvn, g_qn, w_kv, w_q,
               vones, n_ctx_blocks):
    T = TOK_BLOCK
    has_ln = moe is not None
    if has_ln:
        B, S, D = x_all.shape
    else:
        ctx_in, x_in = x_all
        B, D = x_in.shape[0], x_in.shape[2]
        S = ctx_in.shape[1] + x_in.shape[1]
    L = w_in_p.shape[0]

    nb = S // T
    cur = lambda s: jnp.minimum(s, nb - 1)
    prv = lambda s: jnp.maximum(s - 1, 0)

    def mod_spec(k, lyr):
        def imap(b, s):
            row = jnp.where(cur(s) < n_ctx_blocks, B, b)
            return ((lyr * ADA_ROWS + row) * 6 + k, 0, 0)
        return pl.BlockSpec((1, 1, D), imap)

    tok = lambda w: pl.BlockSpec((1, T, w), lambda b, s: (b, prv(s), 0))
    tok_cur = lambda w: pl.BlockSpec((1, T, w), lambda b, s: (b, cur(s), 0))
    headed = pl.BlockSpec((1, MLA_HEADS, T, LANES), lambda b, s: (b, 0, prv(s), 0))
    const = lambda shp: pl.BlockSpec(shp, lambda b, s: (layer,) + (0,) * (len(shp) - 1),
                                     pipeline_mode=pl.Buffered(1))

    if has_ln:
        prev = pl.BlockSpec((1, 1, D), lambda b, s: (layer - 1, 0, 0))
        tiled = pl.BlockSpec((1, T, SUBLANES, LANES), lambda b, s: (b, cur(s), 0, 0))
        in_specs = [tok_cur(D), tiled, prev, prev, mod_spec(5, layer - 1)]
        args = [x_all, moe, ln2_g.reshape(L, 1, D), ln2_b.reshape(L, 1, D), mods]
    else:
        assert ctx_in.shape[1] == n_ctx_blocks * T
        in_specs = [pl.BlockSpec((1, T, D), lambda b, s: (b, jnp.minimum(s, n_ctx_blocks - 1), 0)),
                    pl.BlockSpec((1, T, D),
                                 lambda b, s: (b, jnp.clip(s - n_ctx_blocks, 0, nb - n_ctx_blocks - 1), 0))]
        args = [ctx_in, x_in]
    in_specs += [
        mod_spec(0, layer), mod_spec(1, layer),
        pl.BlockSpec((T, 8 * LANES), lambda b, s: (prv(s), 0)),
        const((1, D, IN_COLS_PAD)),
        const((1, 1, MLA_KV_RANK)), const((1, 1, MLA_Q_RANK)),
        const((1, MLA_KV_RANK, 2048)), const((1, MLA_Q_RANK, 1024)),
        pl.BlockSpec((1, 1536), lambda b, s: (0, 0)),
    ]
    args += [mods, mods, tab, w_in_p, g_kvn.reshape(L, 1, -1), g_qn.reshape(L, 1, -1), w_kv, w_q, vones]

    out_specs, out_shape = [tok_cur(D)], [jax.ShapeDtypeStruct((B, S, D), F32)]
    out_specs += [tok(WIN_HEADS * LANES), tok(LANES), tok(512), headed, headed, headed, tok(2048)]
    out_shape += [jax.ShapeDtypeStruct((B, S, WIN_HEADS * LANES), BF16),
                  jax.ShapeDtypeStruct((B, S, LANES), BF16),
                  jax.ShapeDtypeStruct((B, S, 512), BF16)]
    out_shape += [jax.ShapeDtypeStruct((B, MLA_HEADS, S, LANES), BF16)] * 3
    out_shape += [jax.ShapeDtypeStruct((B, S, 2048), BF16)]

    outs = pl.pallas_call(
        functools.partial(_proj_kernel, has_ln, alpha, n_ctx_blocks),
        grid=(B, nb + 1),
        in_specs=in_specs, out_specs=out_specs, out_shape=out_shape,
        scratch_shapes=[pltpu.VMEM((T, D), BF16)],
        compiler_params=_compiler_params(("parallel", "arbitrary")),
        name="proj",
    )(*args)
    return outs[0], outs[1:]


WIN_HEAD_ORDER = (0, 2, 1, 3, 4, 6, 5, 7)


def _win_kernel(layer, n_ctx, n_steps, sink_ref, q_ref, kc_ref, vc_ref, kp_ref, km_ref, kn_ref,
                vp_ref, vm_ref, vn_ref, o_ref, kall_ref, vall_ref, p_ref):
    i = pl.program_id(1)
    row = lax.broadcasted_iota(jnp.int32, (BAND, LANES), 0)
    col = lax.broadcasted_iota(jnp.int32, (BAND, LANES), 1)
    lane_lo = col < 64
    kall_ref[0:n_ctx, :] = kc_ref[0]
    vall_ref[0:n_ctx, :] = vc_ref[0]

    def scores(qb, r0, nk):
        q = q_ref[0, qb * BAND:(qb + 1) * BAND, :]
        q8 = jnp.concatenate([q[:, h * LANES:(h + 1) * LANES] for h in WIN_HEAD_ORDER], axis=0)
        return _dot_nt(q8, kall_ref[r0:r0 + nk, :])

    def finish(qb, r0, nk, masks, s):
        e_sink = []
        for c, h in enumerate(WIN_HEAD_ORDER):
            sc = s[c * BAND:(c + 1) * BAND, :]
            if masks:
                blocks = []
                for kb in range(nk // BAND):
                    blk = sc[:, kb * BAND:(kb + 1) * BAND]
                    blocks.append(jnp.where(masks[kb], blk, NEG_INF) if kb in masks else blk)
                sc = jnp.concatenate(blocks, axis=1)
            sink = sink_ref[layer, h] * LOG2E
            m = jnp.maximum(jnp.max(sc, axis=1, keepdims=True), sink)
            p_ref[qb, c * BAND:(c + 1) * BAND, 0:nk] = jnp.exp2(sc - m).astype(BF16)
            e_sink.append(jnp.exp2(sink - m))
        pairs = []
        for pi in range(4):
            acc = _dot(p_ref[qb, pi * 2 * BAND:(pi + 1) * 2 * BAND, 0:nk],
                       vall_ref[r0:r0 + nk, pi * LANES:(pi + 1) * LANES])
            es = jnp.concatenate([e_sink[2 * pi], e_sink[2 * pi + 1]], axis=0)
            den = (acc[:, 64:65] if pi % 2 == 0 else acc[:, 0:1]) + es
            pairs.append(acc / den)
        slabs = [jnp.where(lane_lo, pairs[0][:BAND], pairs[1][:BAND]),
                 jnp.where(lane_lo, pairs[0][BAND:], pairs[1][BAND:]),
                 jnp.where(lane_lo, pairs[2][:BAND], pairs[3][:BAND]),
                 jnp.where(lane_lo, pairs[2][BAND:], pairs[3][BAND:])]
        o_ref[0, qb * BAND:(qb + 1) * BAND, :] = jnp.concatenate(slabs, axis=1).astype(o_ref.dtype)

    def attend_both(args0, args1):
        s0 = scores(0, args0[0], args0[1])
        s1 = scores(1, args1[0], args1[1])
        finish(0, *args0, s0)
        finish(1, *args1, s1)

    @pl.when(i == 0)
    def _():
        attend_both((0, n_ctx, None), (0, n_ctx, None))

    @pl.when(i > 0)
    def _():
        r = n_ctx
        for k_ref, v_ref, n in ((kp_ref, vp_ref, BAND), (km_ref, vm_ref, 2 * BAND),
                                (kn_ref, vn_ref, BAND), (kc_ref, vc_ref, n_ctx)):
            kall_ref[r:r + n, :] = k_ref[0]
            vall_ref[r:r + n, :] = v_ref[0]
            r += n
        off_first = jnp.where(i == 1, 2 * LANES, 0)
        off_last = jnp.where(i == n_steps - 1, 2 * LANES, 0)
        cb = n_ctx // BAND
        attend_both((0, n_ctx + 3 * BAND, {cb: col >= row + off_first, cb + 2: col <= row}),
                    (n_ctx + BAND, n_ctx + 3 * BAND, {0: col >= row, 2: col <= row - off_last}))


def _win_call(layer, sink, qwin, kwin, vwin, n_ctx):
    B, S, _ = qwin.shape
    nb = S // BAND
    ncb = n_ctx // BAND
    assert n_ctx == 2 * BAND and nb % 2 == 0
    nk = n_ctx + 3 * BAND
    one = lambda w, f: pl.BlockSpec((1, BAND, w), lambda b, i: (b, jnp.clip(f(i), ncb, nb - 1), 0))
    two = lambda w: pl.BlockSpec((1, 2 * BAND, w), lambda b, i: (b, i, 0))
    ctx = lambda w: pl.BlockSpec((1, n_ctx, w), lambda b, i: (b, 0, 0))
    prev, nxt = (lambda i: 2 * i - 1), (lambda i: 2 * i + 2)
    return pl.pallas_call(
        functools.partial(_win_kernel, layer, n_ctx, nb // 2),
        grid=(B, nb // 2),
        in_specs=[pl.BlockSpec(memory_space=pltpu.SMEM), two(WIN_HEADS * LANES),
                  ctx(LANES), ctx(512), one(LANES, prev), two(LANES), one(LANES, nxt),
                  one(512, prev), two(512), one(512, nxt)],
        out_specs=two(512),
        out_shape=jax.ShapeDtypeStruct((B, S, 512), BF16),
        scratch_shapes=[pltpu.VMEM((2 * n_ctx + 4 * BAND, LANES), BF16),
                        pltpu.VMEM((2 * n_ctx + 4 * BAND, 512), BF16),
                        pltpu.VMEM((2, WIN_HEADS * BAND, nk), BF16)],
        compiler_params=_compiler_params(("parallel", "parallel")),
        name="win_attn",
    )(sink, qwin, kwin, vwin, kwin, kwin, kwin, vwin, vwin, vwin)


MLA_KCHUNK = 256
MLA_HEADS_PER_STEP = 4


def _mla_kernel(n_ctx, n_keys, q_ref, k_ref, v_ref, o_ref, s_ref):
    qi = pl.program_id(2)
    tq = q_ref.shape[2]
    lane = lax.broadcasted_iota(jnp.int32, (tq, LANES), 1)

    def run(nk):
        row_max = []
        for hh in range(MLA_HEADS_PER_STEP):
            q = q_ref[0, hh]
            mrun = None
            for c in range(nk // MLA_KCHUNK):
                s = _dot_nt(q, k_ref[0, hh, c * MLA_KCHUNK:(c + 1) * MLA_KCHUNK, :])
                s_ref[hh, c] = s
                mc = jnp.maximum(s[:, :LANES], s[:, LANES:])
                mrun = mc if mrun is None else jnp.maximum(mrun, mc)
            row_max.append(jnp.max(mrun, axis=1, keepdims=True))
        outs = []
        for hh in range(MLA_HEADS_PER_STEP):
            acc = jnp.zeros((tq, LANES), F32)
            for c in range(nk // MLA_KCHUNK):
                p = jnp.exp2(s_ref[hh, c] - row_max[hh]).astype(BF16)
                acc = acc + _dot(p, v_ref[0, hh, c * MLA_KCHUNK:(c + 1) * MLA_KCHUNK, :])
            den = acc[:, 64:65] if hh % 2 == 0 else acc[:, 0:1]
            outs.append(acc / den)
        o_ref[0] = jnp.concatenate(
            [jnp.where(lane < 64, outs[2 * j], outs[2 * j + 1]) for j in range(len(outs) // 2)],
            axis=1).astype(o_ref.dtype)

    @pl.when(qi == 0)
    def _():
        run(n_ctx)

    @pl.when(qi > 0)
    def _():
        run(n_keys)


def _mla_call(qcat, kcat, vext, n_ctx):
    B, H, S, _ = qcat.shape
    T = TOK_BLOCK
    assert n_ctx == T
    G = MLA_HEADS_PER_STEP
    kv = pl.BlockSpec((1, G, S, LANES), lambda b, hp, i: (b, hp, 0, 0))
    return pl.pallas_call(
        functools.partial(_mla_kernel, n_ctx, S),
        grid=(B, H // G, S // T),
        in_specs=[pl.BlockSpec((1, G, T, LANES), lambda b, hp, i: (b, hp, i, 0)), kv, kv],
        out_specs=pl.BlockSpec((1, T, (G // 2) * LANES), lambda b, hp, i: (b, i, hp)),
        out_shape=jax.ShapeDtypeStruct((B, S, (H // 2) * LANES), BF16),
        scratch_shapes=[pltpu.VMEM((G, S // MLA_KCHUNK, T, MLA_KCHUNK), F32)],
        compiler_params=_compiler_params(("parallel", "parallel", "arbitrary")),
        name="mla_attn",
    )(qcat, kcat, vext)


def _merge_kernel(alpha, x_ref, oa_ref, ob_ref, sg_ref, woa_ref, wob_ref, wout_ref, g1_ref, b1_ref,
                  m2_ref, m3_ref, m4_ref, wr_ref, x1_ref, h2_ref, lg_ref, t_ref):
    s = pl.program_id(1)
    n_blocks = pl.num_programs(1) - 1
    D = x_ref.shape[2]

    def out_proj():
        return _dot(t_ref[...], wout_ref[0])

    def gated_sum():
        sg = sg_ref[0]
        t = (sg[:, :D].astype(F32) * _dot(oa_ref[0], woa_ref[0])
             + sg[:, D:].astype(F32) * _dot(ob_ref[0], wob_ref[0]))
        return t.astype(BF16)

    def finish(y):
        x1 = _layer_norm(alpha * x_ref[0] + m2_ref[0] * y, g1_ref[0], b1_ref[0])
        x1_ref[0] = x1
        h2 = x1 * (1.0 + m4_ref[0]) + m3_ref[0]
        h2_ref[0] = _rows_to_tiles(h2)
        lg = _dot(h2.astype(BF16), wr_ref[0])
        lg_ref[0] = lg.T[:N_EXPERTS]

    @pl.when(s == 0)
    def _():
        t_ref[...] = gated_sum()

    @pl.when(jnp.logical_and(s > 0, s < n_blocks))
    def _():
        y = out_proj()
        t_next = gated_sum()
        finish(y)
        t_ref[...] = t_next

    @pl.when(s == n_blocks)
    def _():
        finish(out_proj())


def _merge_call(layer, alpha, x_res, oa, ob, sg, w_oa, w_ob, w_out, ln1_g, ln1_b, mods, w_r,
                n_ctx_blocks):
    B, S, D = x_res.shape
    T = TOK_BLOCK
    L = w_oa.shape[0]
    nb = S // T
    cur = lambda s: jnp.minimum(s, nb - 1)
    prev = lambda s: jnp.maximum(s - 1, 0)

    def mod_spec(k):
        def imap(b, s):
            row = jnp.where(prev(s) < n_ctx_blocks, B, b)
            return ((layer * ADA_ROWS + row) * 6 + k, 0, 0)
        return pl.BlockSpec((1, 1, D), imap)

    tok_cur = lambda w: pl.BlockSpec((1, T, w), lambda b, s: (b, cur(s), 0))
    tok_prev = lambda w: pl.BlockSpec((1, T, w), lambda b, s: (b, prev(s), 0))
    const = lambda shp: pl.BlockSpec(shp, lambda b, s: (layer,) + (0,) * (len(shp) - 1),
                                     pipeline_mode=pl.Buffered(1))
    return pl.pallas_call(
        functools.partial(_merge_kernel, alpha),
        grid=(B, nb + 1),
        in_specs=[tok_prev(D), tok_cur(512), tok_cur(512), tok_cur(2048),
                  const((1, 512, D)), const((1, 512, D)), const((1, D, D)),
                  const((1, 1, D)), const((1, 1, D)),
                  mod_spec(2), mod_spec(3), mod_spec(4),
                  const((1, D, LANES))],
        out_specs=[tok_prev(D),
                   pl.BlockSpec((1, T, SUBLANES, LANES), lambda b, s: (b, prev(s), 0, 0)),
                   pl.BlockSpec((1, N_EXPERTS, T), lambda b, s: (b, 0, prev(s)))],
        out_shape=[jax.ShapeDtypeStruct((B, S, D), F32),
                   jax.ShapeDtypeStruct((B, S, SUBLANES, LANES), F32),
                   jax.ShapeDtypeStruct((B, N_EXPERTS, S), F32)],
        scratch_shapes=[pltpu.VMEM((T, D), BF16)],
        compiler_params=_compiler_params(("parallel", "arbitrary")),
        name="merge",
    )(x_res, oa, ob, sg, w_oa, w_ob, w_out, ln1_g.reshape(L, 1, D), ln1_b.reshape(L, 1, D),
      mods, mods, mods, w_r)


def _cumsum_lanes(x):
    n = x.shape[1]
    xb = x.astype(BF16)
    r = lax.broadcasted_iota(jnp.int32, (LANES, LANES), 0)
    c = lax.broadcasted_iota(jnp.int32, (LANES, LANES), 1)
    tri = jnp.where(r <= c, 1.0, 0.0).astype(BF16)
    tb = lax.broadcasted_iota(jnp.int32, (n, LANES), 0) // LANES
    kb = lax.broadcasted_iota(jnp.int32, (n, LANES), 1)
    before = jnp.where(tb < kb, 1.0, 0.0).astype(BF16)
    off = _dot(xb, before)
    outs = []
    for k in range(n // LANES):
        outs.append(_dot(xb[:, k * LANES:(k + 1) * LANES], tri) + off[:, k:k + 1])
    return outs


def _select_top(aff, cap):
    bits = pltpu.bitcast(aff, jnp.int32)
    thr = jnp.zeros((aff.shape[0], 1), jnp.int32)
    for bit in range(30, -1, -1):
        cand = thr | (1 << bit)
        cnt = jnp.sum(jnp.where(bits >= cand, 1.0, 0.0), axis=1, keepdims=True)
        thr = jnp.where(cnt >= cap, cand, thr)
    gt = bits > thr
    eq = jnp.where(bits == thr, 1.0, 0.0)
    need = cap - jnp.sum(jnp.where(gt, 1.0, 0.0), axis=1, keepdims=True)
    eq_rank = jnp.concatenate(_cumsum_lanes(eq), axis=1) - eq
    sel = jnp.where(jnp.logical_or(gt, jnp.logical_and(eq > 0.5, eq_rank < need)), 1.0, 0.0)
    return _cumsum_lanes(sel)


ROUTE_PAD = 1e6


def _route_kernel(n_ctx, cap_c, cap_l, lg_ref, aff_ref, idx_ref, cc_ref, cl_ref):
    lg = lg_ref[0]
    m = jnp.max(lg, axis=0, keepdims=True)
    ex = jnp.exp(lg - m)
    aff = ex / jnp.sum(ex, axis=0, keepdims=True)
    aff_ref[0] = aff
    for c_ref, blocks in ((cc_ref, _select_top(aff[:, :n_ctx], cap_c)),
                          (cl_ref, _select_top(aff[:, n_ctx:], cap_l))):
        c_ref[...] = jnp.full(c_ref.shape, ROUTE_PAD, F32)
        for k, blk in enumerate(blocks):
            for ex_i in range(N_EXPERTS):
                c_ref[ex_i, k:k + 1, :] = blk[ex_i:ex_i + 1, :]

    def slots(c_ref, e, n_blocks, width):
        cm = c_ref[e]
        jrow = lax.broadcasted_iota(jnp.int32, (1, width), 1).astype(F32)
        rows = max(n_blocks, SUBLANES)
        cend = cm[0:rows, LANES - 1:LANES]
        blk = jnp.sum(jnp.where(cend <= jrow, 1.0, 0.0), axis=0, keepdims=True)
        kio = lax.broadcasted_iota(jnp.int32, (LANES, width), 0).astype(F32)
        onehot = jnp.where(kio == blk, 1.0, 0.0).astype(BF16)
        cmt = cm.T
        hi = jnp.floor(cmt * (1.0 / 256.0))
        lo = cmt - 256.0 * hi
        cg = 256.0 * _dot(hi.astype(BF16), onehot) + _dot(lo.astype(BF16), onehot)
        inside = jnp.sum(jnp.where(cg <= jrow, 1.0, 0.0), axis=0, keepdims=True)
        return (blk * LANES + inside).astype(jnp.int32)

    def per_expert(e, carry):
        n_lat_blocks = (lg_ref.shape[2] - n_ctx) // LANES
        idx_ref[0, e, :, 0:cap_l] = slots(cl_ref, e, n_lat_blocks, cap_l) + n_ctx
        idx_ref[0, e, :, cap_l:cap_l + cap_c] = slots(cc_ref, e, n_ctx // LANES, LANES)[:, 0:cap_c]
        return carry

    lax.fori_loop(0, N_EXPERTS, per_expert, 0)


def _route_call(lg_t, n_ctx, cap_c, cap_l):
    B, E, S = lg_t.shape
    cap = cap_c + cap_l
    assert (S - n_ctx) // LANES <= LANES and cap_l % LANES == 0 and cap_c <= LANES
    return pl.pallas_call(
        functools.partial(_route_kernel, n_ctx, cap_c, cap_l),
        grid=(B,),
        in_specs=[pl.BlockSpec((1, E, S), lambda b: (b, 0, 0))],
        out_specs=[pl.BlockSpec((1, E, S), lambda b: (b, 0, 0)),
                   pl.BlockSpec((1, E, 1, cap), lambda b: (b, 0, 0, 0))],
        out_shape=[jax.ShapeDtypeStruct((B, E, S), F32),
                   jax.ShapeDtypeStruct((B, E, 1, cap), jnp.int32)],
        scratch_shapes=[pltpu.VMEM((E, LANES, LANES), F32), pltpu.VMEM((E, LANES, LANES), F32)],
        compiler_params=_compiler_params(("parallel",)),
        name="route",
    )(lg_t)


ROW_BATCH = 16
MOE_FF_CHUNK = 512


def _gather_kernel(idx_ref, h_ref, xg_ref, tmp_ref):
    cap = xg_ref.shape[2]

    def body(g, carry):
        for r in range(ROW_BATCH):
            j = g * ROW_BATCH + r
            tmp_ref[j] = h_ref[0, idx_ref[0, 0, j]]
        return carry

    lax.fori_loop(0, cap // ROW_BATCH, body, 0)
    xg_ref[0, 0] = _tiles_to_rows(tmp_ref[...]).astype(xg_ref.dtype)


def _gather_call(idx, h2t):
    B, S = h2t.shape[0], h2t.shape[1]
    D = SUBLANES * LANES
    E, cap = idx.shape[1], idx.shape[2]
    assert cap % ROW_BATCH == 0
    return pl.pallas_call(
        _gather_kernel,
        grid=(B, E),
        in_specs=[pl.BlockSpec((1, 1, cap), lambda b, e: (b * E + e, 0, 0), memory_space=pltpu.SMEM),
                  pl.BlockSpec((1, S, SUBLANES, LANES), lambda b, e: (b, 0, 0, 0),
                               pipeline_mode=pl.Buffered(1))],
        out_specs=pl.BlockSpec((1, 1, cap, D), lambda b, e: (b, e, 0, 0)),
        out_shape=jax.ShapeDtypeStruct((B, E, cap, D), BF16),
        scratch_shapes=[pltpu.VMEM((cap, SUBLANES, LANES), F32)],
        compiler_params=_compiler_params(("parallel", "arbitrary")),
        name="moe_gather",
    )(idx.reshape(B * E, 1, cap), h2t)


def _moe_kernel(idx_ref, aff_ref, xg_ref, wg_ref, wu_ref, wd_ref, out_ref, y_ref):
    s = pl.program_id(1)
    n_exp = pl.num_programs(1) - 1
    cap = y_ref.shape[1]
    n_batches = cap // ROW_BATCH

    def scatter(slot, batches):
        for g in batches:
            rows = _rows_to_tiles(y_ref[slot, g * ROW_BATCH:(g + 1) * ROW_BATCH, :])
            toks = [idx_ref[0, 0, g * ROW_BATCH + r] for r in range(ROW_BATCH)]
            old = [out_ref[0, t] for t in toks]
            for r, t in enumerate(toks):
                out_ref[0, t] = old[r] + rows[r] * aff_ref[0, 0, t]

    def ffn(slot, prev_slot):
        ff = wg_ref.shape[3]
        n_dots = 3 * (ff // MOE_FF_CHUNK)
        per = -(-n_batches // n_dots)
        todo = list(range(n_batches)) if prev_slot is not None else []

        def some_scatter():
            take, todo[:] = todo[:per], todo[per:]
            if take:
                scatter(prev_slot, take)

        x = xg_ref[0, 0]
        y = None
        for f0 in range(0, ff, MOE_FF_CHUNK):
            f1 = f0 + MOE_FF_CHUNK
            some_scatter()
            a = _dot(x, wg_ref[0, 0, :, f0:f1].astype(BF16))
            some_scatter()
            u = _dot(x, wu_ref[0, 0, :, f0:f1].astype(BF16))
            hmid = (a * jax.nn.sigmoid(a) * u).astype(BF16)
            some_scatter()
            part = _dot(hmid, wd_ref[0, 0, f0:f1, :].astype(BF16))
            y = part if y is None else y + part
        y_ref[slot] = y

    @pl.when(s == 0)
    def _():
        out_ref[...] = jnp.zeros(out_ref.shape, out_ref.dtype)
        ffn(0, None)

    @pl.when(jnp.logical_and(s > 0, s < n_exp))
    def _():
        slot = s % 2
        ffn(slot, 1 - slot)

    @pl.when(s == n_exp)
    def _():
        scatter((n_exp - 1) % 2, range(n_batches))


def _moe_call(layer, idx, aff, xg, w_gate, w_up, w_down, S):
    B, E, cap, D = xg.shape
    F = w_gate.shape[3]
    cur = lambda s: jnp.minimum(s, E - 1)
    prev = lambda s: jnp.maximum(s - 1, 0)
    wspec = lambda shp: pl.BlockSpec(shp, lambda b, s: (layer, cur(s), 0, 0))
    return pl.pallas_call(
        _moe_kernel,
        grid=(B, E + 1),
        in_specs=[pl.BlockSpec((1, 1, cap), lambda b, s: (b * E + prev(s), 0, 0), memory_space=pltpu.SMEM),
                  pl.BlockSpec((1, 1, S), lambda b, s: (b * E + prev(s), 0, 0), memory_space=pltpu.SMEM),
                  pl.BlockSpec((1, 1, cap, D), lambda b, s: (b, cur(s), 0, 0)),
                  wspec((1, 1, D, F)), wspec((1, 1, D, F)), wspec((1, 1, F, D))],
        out_specs=pl.BlockSpec((1, S, SUBLANES, LANES), lambda b, s: (b, 0, 0, 0),
                               pipeline_mode=pl.Buffered(1)),
        out_shape=jax.ShapeDtypeStruct((B, S, SUBLANES, LANES), F32),
        scratch_shapes=[pltpu.VMEM((2, cap, D), F32)],
        compiler_params=_compiler_params(("parallel", "arbitrary")),
        name="moe_ffn",
    )(idx.reshape(B * E, 1, cap), aff.reshape(B * E, 1, S), xg, w_gate, w_up, w_down)


def _final_kernel(alpha, x_ref, moe_ref, g_ref, b_ref, m5_ref, o_ref):
    o_ref[0] = _layer_norm(alpha * x_ref[0] + m5_ref[0] * _tiles_to_rows(moe_ref[0]),
                           g_ref[0], b_ref[0])


def _final_call(layer, alpha, x1, moe, ln2_g, ln2_b, mods, n_ctx):
    B, S, D = x1.shape
    T = TOK_BLOCK
    L = ln2_g.shape[0]
    ncb = n_ctx // T
    tok_in = pl.BlockSpec((1, T, D), lambda b, i: (b, i + ncb, 0))
    const = pl.BlockSpec((1, 1, D), lambda b, i: (layer, 0, 0))
    return pl.pallas_call(
        functools.partial(_final_kernel, alpha),
        grid=(B, (S - n_ctx) // T),
        in_specs=[tok_in, pl.BlockSpec((1, T, SUBLANES, LANES), lambda b, i: (b, i + ncb, 0, 0)),
                  const, const,
                  pl.BlockSpec((1, 1, D), lambda b, i: ((layer * ADA_ROWS + b) * 6 + 5, 0, 0))],
        out_specs=pl.BlockSpec((1, T, D), lambda b, i: (b, i, 0)),
        out_shape=jax.ShapeDtypeStruct((B, S - n_ctx, D), F32),
        compiler_params=_compiler_params(("parallel", "parallel")),
        name="final_norm",
    )(x1, moe, ln2_g.reshape(L, 1, D), ln2_b.reshape(L, 1, D), mods)


def _prep_weights(w_in, w_uq, w_ukv, w_router):
    L, D, _ = w_in.shape
    krs = jnp.concatenate([jnp.zeros((L, D, 64), w_in.dtype), w_in[..., 512:544],
                           jnp.zeros((L, D, 32), w_in.dtype)], axis=-1)
    w_in_p = jnp.concatenate([w_in[..., 0:512], krs, w_in[..., 544:]], axis=-1).astype(BF16)

    kvr = w_ukv.reshape(L, MLA_KV_RANK, MLA_HEADS, MLA_NOPE + MLA_V)
    kn, vv = kvr[..., :MLA_NOPE], kvr[..., MLA_NOPE:]
    zk = jnp.zeros_like(kn)
    w_k = jnp.concatenate([kn, zk], axis=-1).reshape(L, MLA_KV_RANK, MLA_HEADS * LANES)
    v_even = jnp.concatenate([vv, zk], axis=-1)
    v_odd = jnp.concatenate([zk, vv], axis=-1)
    odd = (jnp.arange(MLA_HEADS) % 2 == 1)[None, None, :, None]
    w_v = jnp.where(odd, v_odd, v_even).reshape(L, MLA_KV_RANK, MLA_HEADS * LANES)
    w_kv = jnp.concatenate([w_k, w_v], axis=-1).astype(BF16)

    qr = w_uq.reshape(L, MLA_Q_RANK, MLA_HEADS, MLA_NOPE + MLA_ROPE)
    w_q = jnp.concatenate([qr, jnp.zeros((L, MLA_Q_RANK, MLA_HEADS, 32), w_uq.dtype)], axis=-1)
    w_q = w_q.reshape(L, MLA_Q_RANK, MLA_HEADS * LANES).astype(BF16)

    w_r = jnp.concatenate(
        [w_router, jnp.zeros((L, D, LANES - N_EXPERTS), w_router.dtype)], axis=-1).astype(BF16)
    return w_in_p, w_kv, w_q, w_r


def _ones_columns():
    lane = jnp.arange(LANES)
    even = (lane == 64).astype(F32)
    odd = (lane == 0).astype(F32)
    win = jnp.concatenate([even, odd, even, odd])
    mla = jnp.concatenate([even, odd] * (MLA_HEADS // 2))
    return jnp.concatenate([win, mla])[None, :]


def _rope_tables(n_ctx, n_lat):
    pos = jnp.arange(n_lat)
    rowp = (pos // GRID_W).astype(F32)
    colp = (pos % GRID_W).astype(F32)

    def pattern(rot_dim):
        nf = rot_dim // 4
        inv = ROPE_THETA ** (-jnp.arange(nf, dtype=F32) / nf)
        ar, ac = rowp[:, None] * inv, colp[:, None] * inv
        cos = jnp.concatenate([jnp.cos(ar), jnp.cos(ar), jnp.cos(ac), jnp.cos(ac)], axis=1)
        sin = jnp.concatenate([-jnp.sin(ar), jnp.sin(ar), -jnp.sin(ac), jnp.sin(ac)], axis=1)
        return cos, sin

    cw, sw = pattern(WIN_HEAD_DIM)
    cw, sw = jnp.tile(cw, (1, 2)), jnp.tile(sw, (1, 2))
    cm, sm = pattern(MLA_ROPE)
    one64, zero64 = jnp.ones((n_lat, 64), F32), jnp.zeros((n_lat, 64), F32)
    cm = jnp.concatenate([one64, cm, one64[:, :32]], axis=1)
    sm = jnp.concatenate([zero64, sm, zero64[:, :32]], axis=1)
    sq_w, sq_m = WIN_SCALE * LOG2E, MLA_SCALE * LOG2E
    lat = jnp.concatenate([cw * sq_w, sw * sq_w, cw, sw, cm * sq_m, sm * sq_m, cm, sm], axis=1)
    ones, zeros = jnp.ones((n_ctx, LANES), F32), jnp.zeros((n_ctx, LANES), F32)
    ctx = jnp.concatenate([ones * sq_w, zeros, ones, zeros, ones * sq_m, zeros, ones, zeros], axis=1)
    return jnp.concatenate([ctx, lat], axis=0)


def kernel(x, c, ctx, c_ctx, w_ada, b_ada, w_in, attn_sink, mla_q_norm, mla_kv_norm, w_uq, w_ukv,
           w_oa, w_ob, w_out, ln1_g, ln1_b, w_router, w_exp_gate, w_exp_up, w_exp_down, ln2_g, ln2_b):
    B, n_lat, D = x.shape
    n_ctx = ctx.shape[1]
    depth = w_in.shape[0]
    S = n_ctx + n_lat
    alpha = (2 * depth) ** 0.25
    assert D == D_MODEL and n_ctx == TOK_BLOCK and n_lat % TOK_BLOCK == 0 and B + 1 <= ADA_ROWS
    cap_c = CAPACITY_FACTOR * n_ctx // N_EXPERTS
    cap_l = CAPACITY_FACTOR * n_lat // N_EXPERTS
    assert cap_l % 256 == 0 and cap_c % 8 == 0
    ncb = n_ctx // TOK_BLOCK

    cond = jnp.concatenate([c, c_ctx[None], jnp.zeros((ADA_ROWS - B - 1, D), F32)], axis=0)
    mods = _ada_call(cond.T, w_ada, b_ada, B + 1).reshape(depth * ADA_ROWS * 6, 1, D)

    w_in_p, w_kv, w_q, w_r = _prep_weights(w_in, w_uq, w_ukv, w_router)
    w_oa_b, w_ob_b, w_out_b = w_oa.astype(BF16), w_ob.astype(BF16), w_out.astype(BF16)
    tab = _rope_tables(n_ctx, n_lat)
    vones = _ones_columns()

    x_all = (ctx, x)
    moe = None
    for l in range(depth):
        x_res, (qwin, kwin, vwin, qcat, kcat, vext, sg) = _proj_call(
            l, alpha, x_all, moe, ln2_g, ln2_b, mods, tab, w_in_p, mla_kv_norm, mla_q_norm,
            w_kv, w_q, vones, ncb)
        oa = _win_call(l, attn_sink, qwin, kwin, vwin, n_ctx)
        ob = _mla_call(qcat, kcat, vext, n_ctx)
        x1, h2, lg_t = _merge_call(l, alpha, x_res, oa, ob, sg, w_oa_b, w_ob_b, w_out_b,
                                   ln1_g, ln1_b, mods, w_r, ncb)
        aff, idx4 = _route_call(lg_t, n_ctx, cap_c, cap_l)
        idx = idx4.reshape(B, N_EXPERTS, cap_c + cap_l)
        xg = _gather_call(idx, h2)
        moe = _moe_call(l, idx, aff, xg, w_exp_gate, w_exp_up, w_exp_down, S)
        x_all = x1
    return _final_call(depth - 1, alpha, x_all, moe, ln2_g, ln2_b, mods, n_ctx)
```

```python
import functools
import math

import jax
import jax.numpy as jnp
from jax import lax
from jax.experimental import pallas as pl
from jax.experimental.pallas import tpu as pltpu

D_MODEL = 1024
GRID_W = 64
WIN_HEADS = 8
WIN_KV_HEADS = 2
WIN_HEAD_DIM = 64
BAND = 128
MLA_HEADS = 8
MLA_Q_RANK = 384
MLA_KV_RANK = 256
MLA_NOPE = 64
MLA_ROPE = 32
MLA_V = 64
N_EXPERTS = 16
CAPACITY_FACTOR = 2
ROPE_THETA = 10000.0
LN_EPS = 1e-5
RMS_EPS = 1e-6
NEG_INF = -1e30
LOG2E = math.log2(math.e)
WIN_SCALE = WIN_HEAD_DIM ** -0.5
MLA_SCALE = (MLA_NOPE + MLA_ROPE) ** -0.5

LANES = 128
TOK_BLOCK = 256
VMEM_LIMIT = 56 * 1024 * 1024

BF16 = jnp.bfloat16
F32 = jnp.float32

SEG_KWIN = (0, 128)
SEG_VWIN = (128, 256)
SEG_CKV = (256, 512)
SEG_KR = (512, 640)
SEG_QWIN = (640, 1152)
SEG_CQ = (1152, 1536)
SEG_GATE = (1536, 3584)
IN_COLS_PAD = 3584


def _dot(a, b):
    return jnp.dot(a, b, preferred_element_type=F32)


def _dot_nt(a, b):
    return lax.dot_general(a, b, (((1,), (1,)), ((), ())), preferred_element_type=F32)


def _layer_norm(z, g, b):
    mu = jnp.mean(z, axis=-1, keepdims=True)
    zc = z - mu
    var = jnp.mean(zc * zc, axis=-1, keepdims=True)
    return zc * lax.rsqrt(var + LN_EPS) * g + b


def _rms_norm(x, g):
    return x * lax.rsqrt(jnp.mean(x * x, axis=-1, keepdims=True) + RMS_EPS) * g


def _rope_slab(x, cos, sin, half):
    lane = lax.broadcasted_iota(jnp.int32, x.shape, 1)
    partner = jnp.where((lane & half) == 0,
                        pltpu.roll(x, LANES - half, 1), pltpu.roll(x, half, 1))
    return x * cos + partner * sin


SUBLANES = 8


def _rows_to_tiles(y):
    n = y.shape[0]
    y3 = pltpu.einshape("r(sl)->srl", y, s=SUBLANES)
    y4 = y3.reshape(SUBLANES, n // SUBLANES, SUBLANES, LANES)
    return jnp.transpose(y4, (1, 2, 0, 3)).reshape(n, SUBLANES, LANES)


def _tiles_to_rows(x3):
    n = x3.shape[0]
    x4 = x3.reshape(n // SUBLANES, SUBLANES, SUBLANES, LANES)
    xs = jnp.transpose(x4, (2, 0, 1, 3)).reshape(SUBLANES, n, LANES)
    return pltpu.einshape("srl->r(sl)", xs)


def _compiler_params(sem):
    return pltpu.CompilerParams(dimension_semantics=sem, vmem_limit_bytes=VMEM_LIMIT)


ADA_ROWS = 8
ADA_TN = 512


def _ada_kernel(n_rows, condt_ref, w_ref, b_ref, o_ref):
    ct = condt_ref[...]
    st = ct * jax.nn.sigmoid(ct)
    w = w_ref[0]
    rows = []
    for r in range(n_rows):
        rows.append(jnp.sum(w * st[:, r:r + 1], axis=0, keepdims=True) + b_ref[0])
    rows.append(jnp.zeros((ADA_ROWS - n_rows, w.shape[1]), F32))
    o_ref[0] = jnp.concatenate(rows, axis=0)


def _ada_call(cond_t, w_ada, b_ada, n_rows):
    L, D, N = w_ada.shape
    return pl.pallas_call(
        functools.partial(_ada_kernel, n_rows),
        grid=(L, N // ADA_TN),
        in_specs=[
            pl.BlockSpec((D, ADA_ROWS), lambda l, j: (0, 0)),
            pl.BlockSpec((1, D, ADA_TN), lambda l, j: (l, 0, j)),
            pl.BlockSpec((1, 1, ADA_TN), lambda l, j: (l, 0, j)),
        ],
        out_specs=pl.BlockSpec((1, ADA_ROWS, ADA_TN), lambda l, j: (l, 0, j)),
        out_shape=jax.ShapeDtypeStruct((L, ADA_ROWS, N), F32),
        compiler_params=_compiler_params(("parallel", "parallel")),
        name="adaln_mod",
    )(cond_t, w_ada, b_ada.reshape(L, 1, N))


def _proj_kernel(has_ln, alpha, n_ctx_blocks, *refs):
    if has_ln:
        (x_ref, moe_ref, g2_ref, b2_ref, m5_ref, sh_ref, sc_ref, tab_ref, win_ref, gkv_ref, gq_ref,
         wkv_ref, wq_ref, vones_ref,
         xres_ref, qwin_ref, kwin_ref, vwin_ref, qcat_ref, kcat_ref, vext_ref, sg_ref) = refs
        z = alpha * x_ref[0] + m5_ref[0] * _tiles_to_rows(moe_ref[0])
        x = _layer_norm(z, g2_ref[0], b2_ref[0])
    else:
        (ctx_ref, x_ref, sh_ref, sc_ref, tab_ref, win_ref, gkv_ref, gq_ref, wkv_ref, wq_ref, vones_ref,
         xres_ref, qwin_ref, kwin_ref, vwin_ref, qcat_ref, kcat_ref, vext_ref, sg_ref) = refs
        x = jnp.where(pl.program_id(0) < n_ctx_blocks, ctx_ref[0], x_ref[0])
    xres_ref[0] = x
    hb = (x * (1.0 + sc_ref[0]) + sh_ref[0]).astype(BF16)

    def seg(s):
        return _dot(hb, win_ref[0, :, s[0]:s[1]])

    def tab(i):
        return tab_ref[:, i * LANES:(i + 1) * LANES]

    cq_w, sq_w, ck_w, sk_w, cq_m, sq_m, ck_m, sk_m = (tab(i) for i in range(8))

    lane_lo = lax.broadcasted_iota(jnp.int32, (hb.shape[0], LANES), 1) < WIN_HEAD_DIM
    half_swap = lambda t: pltpu.roll(t, WIN_HEAD_DIM, 1)

    ckv = _rms_norm(seg(SEG_CKV), gkv_ref[0]).astype(BF16)
    cq = _rms_norm(seg(SEG_CQ), gq_ref[0]).astype(BF16)

    qw = seg(SEG_QWIN)
    q_slabs = []
    for s2 in range(WIN_HEADS // 2):
        pair = _rope_slab(qw[:, s2 * LANES:(s2 + 1) * LANES], cq_w, sq_w, 16)
        pair_sw = half_swap(pair)
        if 2 * s2 < WIN_HEADS // WIN_KV_HEADS:
            q_slabs += [jnp.where(lane_lo, pair, 0.0), jnp.where(lane_lo, pair_sw, 0.0)]
        else:
            q_slabs += [jnp.where(lane_lo, 0.0, pair_sw), jnp.where(lane_lo, 0.0, pair)]
    qwin_ref[0] = jnp.concatenate(q_slabs, axis=1).astype(BF16)

    kv = _dot(ckv, wkv_ref[0])
    kr = _rope_slab(seg(SEG_KR), ck_m, sk_m, 8)
    qm = _dot(cq, wq_ref[0])
    for h in range(MLA_HEADS):
        sl = slice(h * LANES, (h + 1) * LANES)
        kcat_ref[0, h] = (kv[:, sl] + kr).astype(BF16)
        vext_ref[0, h] = (kv[:, 1024 + h * LANES:1024 + (h + 1) * LANES]
                          + vones_ref[:, 512 + h * LANES:512 + (h + 1) * LANES]).astype(BF16)
        qcat_ref[0, h] = _rope_slab(qm[:, sl], cq_m, sq_m, 8).astype(BF16)

    sg_ref[0] = jax.nn.sigmoid(seg(SEG_GATE)).astype(BF16)
    va = seg(SEG_VWIN)
    va_sw = half_swap(va)
    vwin_ref[0] = (jnp.concatenate(
        [jnp.where(lane_lo, va, 0.0), jnp.where(lane_lo, 0.0, va_sw),
         jnp.where(lane_lo, va_sw, 0.0), jnp.where(lane_lo, 0.0, va)], axis=1)
        + vones_ref[:, 0:512]).astype(BF16)
    kwin_ref[0] = _rope_slab(seg(SEG_KWIN), ck_w, sk_w, 16).astype(BF16)


def _proj_call(layer, alpha, x_all, moe, ln2_g, ln2_b, mods, tab, w_in_p, g_kvn, g_qn, w_kv, w_q,
               vones, n_ctx_blocks):
    T = TOK_BLOCK
    has_ln = moe is not None
    if has_ln:
        B, S, D = x_all.shape
    else:
        ctx_in, x_in = x_all
        B, D = x_in.shape[0], x_in.shape[2]
        S = ctx_in.shape[1] + x_in.shape[1]
    L = w_in_p.shape[0]

    def mod_spec(k, lyr):
        def imap(i, b):
            row = jnp.where(i < n_ctx_blocks, B, b)
            return ((lyr * ADA_ROWS + row) * 6 + k, 0, 0)
        return pl.BlockSpec((1, 1, D), imap)

    tok = lambda w: pl.BlockSpec((1, T, w), lambda i, b: (b, i, 0))
    headed = pl.BlockSpec((1, MLA_HEADS, T, LANES), lambda i, b: (b, 0, i, 0))
    const = lambda shp: pl.BlockSpec(shp, lambda i, b: (layer,) + (0,) * (len(shp) - 1),
                                     pipeline_mode=pl.Buffered(1))

    if has_ln:
        prev = pl.BlockSpec((1, 1, D), lambda i, b: (layer - 1, 0, 0))
        tiled = pl.BlockSpec((1, T, SUBLANES, LANES), lambda i, b: (b, i, 0, 0))
        in_specs = [tok(D), tiled, prev, prev, mod_spec(5, layer - 1)]
        args = [x_all, moe, ln2_g.reshape(L, 1, D), ln2_b.reshape(L, 1, D), mods]
    else:
        assert ctx_in.shape[1] == n_ctx_blocks * T
        in_specs = [pl.BlockSpec((1, T, D), lambda i, b: (b, jnp.minimum(i, n_ctx_blocks - 1), 0)),
                    pl.BlockSpec((1, T, D), lambda i, b: (b, jnp.maximum(i - n_ctx_blocks, 0), 0))]
        args = [ctx_in, x_in]
    in_specs += [
        mod_spec(0, layer), mod_spec(1, layer),
        pl.BlockSpec((T, 8 * LANES), lambda i, b: (i, 0)),
        const((1, D, IN_COLS_PAD)),
        const((1, 1, MLA_KV_RANK)), const((1, 1, MLA_Q_RANK)),
        const((1, MLA_KV_RANK, 2048)), const((1, MLA_Q_RANK, 1024)),
        pl.BlockSpec((1, 1536), lambda i, b: (0, 0)),
    ]
    args += [mods, mods, tab, w_in_p, g_kvn.reshape(L, 1, -1), g_qn.reshape(L, 1, -1), w_kv, w_q, vones]

    out_specs, out_shape = [tok(D)], [jax.ShapeDtypeStruct((B, S, D), F32)]
    out_specs += [tok(WIN_HEADS * LANES), tok(LANES), tok(512), headed, headed, headed, tok(2048)]
    out_shape += [jax.ShapeDtypeStruct((B, S, WIN_HEADS * LANES), BF16),
                  jax.ShapeDtypeStruct((B, S, LANES), BF16),
                  jax.ShapeDtypeStruct((B, S, 512), BF16)]
    out_shape += [jax.ShapeDtypeStruct((B, MLA_HEADS, S, LANES), BF16)] * 3
    out_shape += [jax.ShapeDtypeStruct((B, S, 2048), BF16)]

    outs = pl.pallas_call(
        functools.partial(_proj_kernel, has_ln, alpha, n_ctx_blocks),
        grid=(S // T, B),
        in_specs=in_specs, out_specs=out_specs, out_shape=out_shape,
        compiler_params=_compiler_params(("parallel", "parallel")),
        name="proj",
    )(*args)
    return outs[0], outs[1:]


WIN_HEAD_ORDER = (0, 2, 1, 3, 4, 6, 5, 7)


def _win_kernel(layer, n_ctx, n_steps, sink_ref, q_ref, kc_ref, vc_ref, kp_ref, km_ref, kn_ref,
                vp_ref, vm_ref, vn_ref, o_ref, kall_ref, vall_ref, p_ref):
    i = pl.program_id(1)
    row = lax.broadcasted_iota(jnp.int32, (BAND, LANES), 0)
    col = lax.broadcasted_iota(jnp.int32, (BAND, LANES), 1)
    lane_lo = col < 64
    kall_ref[0:n_ctx, :] = kc_ref[0]
    vall_ref[0:n_ctx, :] = vc_ref[0]

    def scores(qb, r0, nk):
        q = q_ref[0, qb * BAND:(qb + 1) * BAND, :]
        q8 = jnp.concatenate([q[:, h * LANES:(h + 1) * LANES] for h in WIN_HEAD_ORDER], axis=0)
        return _dot_nt(q8, kall_ref[r0:r0 + nk, :])

    def finish(qb, r0, nk, masks, s):
        e_sink = []
        for c, h in enumerate(WIN_HEAD_ORDER):
            sc = s[c * BAND:(c + 1) * BAND, :]
            if masks:
                blocks = []
                for kb in range(nk // BAND):
                    blk = sc[:, kb * BAND:(kb + 1) * BAND]
                    blocks.append(jnp.where(masks[kb], blk, NEG_INF) if kb in masks else blk)
                sc = jnp.concatenate(blocks, axis=1)
            sink = sink_ref[layer, h] * LOG2E
            m = jnp.maximum(jnp.max(sc, axis=1, keepdims=True), sink)
            p_ref[qb, c * BAND:(c + 1) * BAND, 0:nk] = jnp.exp2(sc - m).astype(BF16)
            e_sink.append(jnp.exp2(sink - m))
        pairs = []
        for pi in range(4):
            acc = _dot(p_ref[qb, pi * 2 * BAND:(pi + 1) * 2 * BAND, 0:nk],
                       vall_ref[r0:r0 + nk, pi * LANES:(pi + 1) * LANES])
            es = jnp.concatenate([e_sink[2 * pi], e_sink[2 * pi + 1]], axis=0)
            den = (acc[:, 64:65] if pi % 2 == 0 else acc[:, 0:1]) + es
            pairs.append(acc / den)
        slabs = [jnp.where(lane_lo, pairs[0][:BAND], pairs[1][:BAND]),
                 jnp.where(lane_lo, pairs[0][BAND:], pairs[1][BAND:]),
                 jnp.where(lane_lo, pairs[2][:BAND], pairs[3][:BAND]),
                 jnp.where(lane_lo, pairs[2][BAND:], pairs[3][BAND:])]
        o_ref[0, qb * BAND:(qb + 1) * BAND, :] = jnp.concatenate(slabs, axis=1).astype(o_ref.dtype)

    def attend_both(args0, args1):
        s0 = scores(0, args0[0], args0[1])
        s1 = scores(1, args1[0], args1[1])
        finish(0, *args0, s0)
        finish(1, *args1, s1)

    @pl.when(i == 0)
    def _():
        attend_both((0, n_ctx, None), (0, n_ctx, None))

    @pl.when(i > 0)
    def _():
        r = n_ctx
        for k_ref, v_ref, n in ((kp_ref, vp_ref, BAND), (km_ref, vm_ref, 2 * BAND),
                                (kn_ref, vn_ref, BAND), (kc_ref, vc_ref, n_ctx)):
            kall_ref[r:r + n, :] = k_ref[0]
            vall_ref[r:r + n, :] = v_ref[0]
            r += n
        off_first = jnp.where(i == 1, 2 * LANES, 0)
        off_last = jnp.where(i == n_steps - 1, 2 * LANES, 0)
        cb = n_ctx // BAND
        attend_both((0, n_ctx + 3 * BAND, {cb: col >= row + off_first, cb + 2: col <= row}),
                    (n_ctx + BAND, n_ctx + 3 * BAND, {0: col >= row, 2: col <= row - off_last}))


def _win_call(layer, sink, qwin, kwin, vwin, n_ctx):
    B, S, _ = qwin.shape
    nb = S // BAND
    ncb = n_ctx // BAND
    assert n_ctx == 2 * BAND and nb % 2 == 0
    nk = n_ctx + 3 * BAND
    one = lambda w, f: pl.BlockSpec((1, BAND, w), lambda b, i: (b, jnp.clip(f(i), ncb, nb - 1), 0))
    two = lambda w: pl.BlockSpec((1, 2 * BAND, w), lambda b, i: (b, i, 0))
    ctx = lambda w: pl.BlockSpec((1, n_ctx, w), lambda b, i: (b, 0, 0))
    prev, nxt = (lambda i: 2 * i - 1), (lambda i: 2 * i + 2)
    return pl.pallas_call(
        functools.partial(_win_kernel, layer, n_ctx, nb // 2),
        grid=(B, nb // 2),
        in_specs=[pl.BlockSpec(memory_space=pltpu.SMEM), two(WIN_HEADS * LANES),
                  ctx(LANES), ctx(512), one(LANES, prev), two(LANES), one(LANES, nxt),
                  one(512, prev), two(512), one(512, nxt)],
        out_specs=two(512),
        out_shape=jax.ShapeDtypeStruct((B, S, 512), BF16),
        scratch_shapes=[pltpu.VMEM((2 * n_ctx + 4 * BAND, LANES), BF16),
                        pltpu.VMEM((2 * n_ctx + 4 * BAND, 512), BF16),
                        pltpu.VMEM((2, WIN_HEADS * BAND, nk), BF16)],
        compiler_params=_compiler_params(("parallel", "parallel")),
        name="win_attn",
    )(sink, qwin, kwin, vwin, kwin, kwin, kwin, vwin, vwin, vwin)


MLA_KCHUNK = 256
MLA_HEADS_PER_STEP = 4


def _mla_kernel(n_ctx, n_keys, q_ref, k_ref, v_ref, o_ref, s_ref):
    qi = pl.program_id(2)
    tq = q_ref.shape[2]
    lane = lax.broadcasted_iota(jnp.int32, (tq, LANES), 1)

    def run(nk):
        row_max = []
        for hh in range(MLA_HEADS_PER_STEP):
            q = q_ref[0, hh]
            mrun = None
            for c in range(nk // MLA_KCHUNK):
                s = _dot_nt(q, k_ref[0, hh, c * MLA_KCHUNK:(c + 1) * MLA_KCHUNK, :])
                s_ref[hh, c] = s
                mc = jnp.maximum(s[:, :LANES], s[:, LANES:])
                mrun = mc if mrun is None else jnp.maximum(mrun, mc)
            row_max.append(jnp.max(mrun, axis=1, keepdims=True))
        outs = []
        for hh in range(MLA_HEADS_PER_STEP):
            acc = jnp.zeros((tq, LANES), F32)
            for c in range(nk // MLA_KCHUNK):
                p = jnp.exp2(s_ref[hh, c] - row_max[hh]).astype(BF16)
                acc = acc + _dot(p, v_ref[0, hh, c * MLA_KCHUNK:(c + 1) * MLA_KCHUNK, :])
            den = acc[:, 64:65] if hh % 2 == 0 else acc[:, 0:1]
            outs.append(acc / den)
        o_ref[0] = jnp.concatenate(
            [jnp.where(lane < 64, outs[2 * j], outs[2 * j + 1]) for j in range(len(outs) // 2)],
            axis=1).astype(o_ref.dtype)

    @pl.when(qi == 0)
    def _():
        run(n_ctx)

    @pl.when(qi > 0)
    def _():
        run(n_keys)


def _mla_call(qcat, kcat, vext, n_ctx):
    B, H, S, _ = qcat.shape
    T = TOK_BLOCK
    assert n_ctx == T
    G = MLA_HEADS_PER_STEP
    kv = pl.BlockSpec((1, G, S, LANES), lambda b, hp, i: (b, hp, 0, 0))
    return pl.pallas_call(
        functools.partial(_mla_kernel, n_ctx, S),
        grid=(B, H // G, S // T),
        in_specs=[pl.BlockSpec((1, G, T, LANES), lambda b, hp, i: (b, hp, i, 0)), kv, kv],
        out_specs=pl.BlockSpec((1, T, (G // 2) * LANES), lambda b, hp, i: (b, i, hp)),
        out_shape=jax.ShapeDtypeStruct((B, S, (H // 2) * LANES), BF16),
        scratch_shapes=[pltpu.VMEM((G, S // MLA_KCHUNK, T, MLA_KCHUNK), F32)],
        compiler_params=_compiler_params(("parallel", "parallel", "arbitrary")),
        name="mla_attn",
    )(qcat, kcat, vext)


def _merge_kernel(alpha, x_ref, oa_ref, ob_ref, sg_ref, woa_ref, wob_ref, wout_ref, g1_ref, b1_ref,
                  m2_ref, m3_ref, m4_ref, wr_ref, x1_ref, h2_ref, lg_ref):
    D = x_ref.shape[2]
    sg = sg_ref[0]
    t = (sg[:, :D].astype(F32) * _dot(oa_ref[0], woa_ref[0])
         + sg[:, D:].astype(F32) * _dot(ob_ref[0], wob_ref[0]))
    y = _dot(t.astype(BF16), wout_ref[0])
    x1 = _layer_norm(alpha * x_ref[0] + m2_ref[0] * y, g1_ref[0], b1_ref[0])
    x1_ref[0] = x1
    h2 = x1 * (1.0 + m4_ref[0]) + m3_ref[0]
    h2_ref[0] = _rows_to_tiles(h2)
    lg = _dot(h2.astype(BF16), wr_ref[0])
    lg_ref[0] = lg.T[:N_EXPERTS]


def _merge_call(layer, alpha, x_res, oa, ob, sg, w_oa, w_ob, w_out, ln1_g, ln1_b, mods, w_r,
                n_ctx_blocks):
    B, S, D = x_res.shape
    T = TOK_BLOCK
    L = w_oa.shape[0]

    def mod_spec(k):
        def imap(b, i):
            row = jnp.where(i < n_ctx_blocks, B, b)
            return ((layer * ADA_ROWS + row) * 6 + k, 0, 0)
        return pl.BlockSpec((1, 1, D), imap)

    tok = lambda w: pl.BlockSpec((1, T, w), lambda b, i: (b, i, 0))
    const = lambda shp: pl.BlockSpec(shp, lambda b, i: (layer,) + (0,) * (len(shp) - 1),
                                     pipeline_mode=pl.Buffered(1))
    return pl.pallas_call(
        functools.partial(_merge_kernel, alpha),
        grid=(B, S // T),
        in_specs=[tok(D), tok(512), tok(512), tok(2048),
                  const((1, 512, D)), const((1, 512, D)), const((1, D, D)),
                  const((1, 1, D)), const((1, 1, D)),
                  mod_spec(2), mod_spec(3), mod_spec(4),
                  const((1, D, LANES))],
        out_specs=[tok(D), pl.BlockSpec((1, T, SUBLANES, LANES), lambda b, i: (b, i, 0, 0)),
                   pl.BlockSpec((1, N_EXPERTS, T), lambda b, i: (b, 0, i))],
        out_shape=[jax.ShapeDtypeStruct((B, S, D), F32),
                   jax.ShapeDtypeStruct((B, S, SUBLANES, LANES), F32),
                   jax.ShapeDtypeStruct((B, N_EXPERTS, S), F32)],
        compiler_params=_compiler_params(("parallel", "parallel")),
        name="merge",
    )(x_res, oa, ob, sg, w_oa, w_ob, w_out, ln1_g.reshape(L, 1, D), ln1_b.reshape(L, 1, D),
      mods, mods, mods, w_r)


def _cumsum_lanes(x):
    n = x.shape[1]
    xb = x.astype(BF16)
    r = lax.broadcasted_iota(jnp.int32, (LANES, LANES), 0)
    c = lax.broadcasted_iota(jnp.int32, (LANES, LANES), 1)
    tri = jnp.where(r <= c, 1.0, 0.0).astype(BF16)
    tb = lax.broadcasted_iota(jnp.int32, (n, LANES), 0) // LANES
    kb = lax.broadcasted_iota(jnp.int32, (n, LANES), 1)
    before = jnp.where(tb < kb, 1.0, 0.0).astype(BF16)
    off = _dot(xb, before)
    outs = []
    for k in range(n // LANES):
        outs.append(_dot(xb[:, k * LANES:(k + 1) * LANES], tri) + off[:, k:k + 1])
    return outs


def _select_top(aff, cap):
    bits = pltpu.bitcast(aff, jnp.int32)
    thr = jnp.zeros((aff.shape[0], 1), jnp.int32)
    for bit in range(30, -1, -1):
        cand = thr | (1 << bit)
        cnt = jnp.sum(jnp.where(bits >= cand, 1.0, 0.0), axis=1, keepdims=True)
        thr = jnp.where(cnt >= cap, cand, thr)
    gt = bits > thr
    eq = jnp.where(bits == thr, 1.0, 0.0)
    need = cap - jnp.sum(jnp.where(gt, 1.0, 0.0), axis=1, keepdims=True)
    eq_rank = jnp.concatenate(_cumsum_lanes(eq), axis=1) - eq
    sel = jnp.where(jnp.logical_or(gt, jnp.logical_and(eq > 0.5, eq_rank < need)), 1.0, 0.0)
    return _cumsum_lanes(sel)


ROUTE_PAD = 1e6


def _route_kernel(n_ctx, cap_c, cap_l, lg_ref, aff_ref, idx_ref, cc_ref, cl_ref):
    lg = lg_ref[0]
    m = jnp.max(lg, axis=0, keepdims=True)
    ex = jnp.exp(lg - m)
    aff = ex / jnp.sum(ex, axis=0, keepdims=True)
    aff_ref[0] = aff
    for c_ref, blocks in ((cc_ref, _select_top(aff[:, :n_ctx], cap_c)),
                          (cl_ref, _select_top(aff[:, n_ctx:], cap_l))):
        c_ref[...] = jnp.full(c_ref.shape, ROUTE_PAD, F32)
        for k, blk in enumerate(blocks):
            for ex_i in range(N_EXPERTS):
                c_ref[ex_i, k:k + 1, :] = blk[ex_i:ex_i + 1, :]

    def slots(c_ref, e, n_blocks, width):
        cm = c_ref[e]
        jrow = lax.broadcasted_iota(jnp.int32, (1, width), 1).astype(F32)
        rows = max(n_blocks, SUBLANES)
        cend = cm[0:rows, LANES - 1:LANES]
        blk = jnp.sum(jnp.where(cend <= jrow, 1.0, 0.0), axis=0, keepdims=True)
        kio = lax.broadcasted_iota(jnp.int32, (LANES, width), 0).astype(F32)
        onehot = jnp.where(kio == blk, 1.0, 0.0).astype(BF16)
        cmt = cm.T
        hi = jnp.floor(cmt * (1.0 / 256.0))
        lo = cmt - 256.0 * hi
        cg = 256.0 * _dot(hi.astype(BF16), onehot) + _dot(lo.astype(BF16), onehot)
        inside = jnp.sum(jnp.where(cg <= jrow, 1.0, 0.0), axis=0, keepdims=True)
        return (blk * LANES + inside).astype(jnp.int32)

    def per_expert(e, carry):
        n_lat_blocks = (lg_ref.shape[2] - n_ctx) // LANES
        idx_ref[0, e, :, 0:cap_l] = slots(cl_ref, e, n_lat_blocks, cap_l) + n_ctx
        idx_ref[0, e, :, cap_l:cap_l + cap_c] = slots(cc_ref, e, n_ctx // LANES, LANES)[:, 0:cap_c]
        return carry

    lax.fori_loop(0, N_EXPERTS, per_expert, 0)


def _route_call(lg_t, n_ctx, cap_c, cap_l):
    B, E, S = lg_t.shape
    cap = cap_c + cap_l
    assert (S - n_ctx) // LANES <= LANES and cap_l % LANES == 0 and cap_c <= LANES
    return pl.pallas_call(
        functools.partial(_route_kernel, n_ctx, cap_c, cap_l),
        grid=(B,),
        in_specs=[pl.BlockSpec((1, E, S), lambda b: (b, 0, 0))],
        out_specs=[pl.BlockSpec((1, E, S), lambda b: (b, 0, 0)),
                   pl.BlockSpec((1, E, 1, cap), lambda b: (b, 0, 0, 0))],
        out_shape=[jax.ShapeDtypeStruct((B, E, S), F32),
                   jax.ShapeDtypeStruct((B, E, 1, cap), jnp.int32)],
        scratch_shapes=[pltpu.VMEM((E, LANES, LANES), F32), pltpu.VMEM((E, LANES, LANES), F32)],
        compiler_params=_compiler_params(("parallel",)),
        name="route",
    )(lg_t)


ROW_BATCH = 16
MOE_FF_CHUNK = 512


def _gather_kernel(idx_ref, h_ref, xg_ref, tmp_ref):
    cap = xg_ref.shape[2]

    def body(g, carry):
        for r in range(ROW_BATCH):
            j = g * ROW_BATCH + r
            tmp_ref[j] = h_ref[0, idx_ref[0, 0, j]]
        return carry

    lax.fori_loop(0, cap // ROW_BATCH, body, 0)
    xg_ref[0, 0] = _tiles_to_rows(tmp_ref[...]).astype(xg_ref.dtype)


def _gather_call(idx, h2t):
    B, S = h2t.shape[0], h2t.shape[1]
    D = SUBLANES * LANES
    E, cap = idx.shape[1], idx.shape[2]
    assert cap % ROW_BATCH == 0
    return pl.pallas_call(
        _gather_kernel,
        grid=(B, E),
        in_specs=[pl.BlockSpec((1, 1, cap), lambda b, e: (b * E + e, 0, 0), memory_space=pltpu.SMEM),
                  pl.BlockSpec((1, S, SUBLANES, LANES), lambda b, e: (b, 0, 0, 0),
                               pipeline_mode=pl.Buffered(1))],
        out_specs=pl.BlockSpec((1, 1, cap, D), lambda b, e: (b, e, 0, 0)),
        out_shape=jax.ShapeDtypeStruct((B, E, cap, D), BF16),
        scratch_shapes=[pltpu.VMEM((cap, SUBLANES, LANES), F32)],
        compiler_params=_compiler_params(("parallel", "arbitrary")),
        name="moe_gather",
    )(idx.reshape(B * E, 1, cap), h2t)


def _moe_kernel(idx_ref, aff_ref, xg_ref, wg_ref, wu_ref, wd_ref, out_ref, y_ref):
    s = pl.program_id(1)
    n_exp = pl.num_programs(1) - 1
    cap = y_ref.shape[1]
    n_batches = cap // ROW_BATCH

    def scatter(slot, batches):
        new = None
        for g in batches:
            rows = _rows_to_tiles(y_ref[slot, g * ROW_BATCH:(g + 1) * ROW_BATCH, :])
            toks = [idx_ref[0, 0, g * ROW_BATCH + r] for r in range(ROW_BATCH)]
            old = [out_ref[0, t] for t in toks]
            for r, t in enumerate(toks):
                new = old[r] + rows[r] * aff_ref[0, 0, t]
                out_ref[0, t] = new
        return new

    def ffn(slot, prev_slot):
        ff = wg_ref.shape[3]
        n_dots = 3 * (ff // MOE_FF_CHUNK)
        per = -(-n_batches // n_dots)
        todo = list(range(n_batches)) if prev_slot is not None else []

        def weights(w):
            take, todo[:] = todo[:per], todo[per:]
            w = w.astype(BF16)
            if not take:
                return w
            last = scatter(prev_slot, take)
            bits = pltpu.bitcast(last, jnp.uint32)
            zero = pltpu.bitcast((bits >> 16) >> 16, F32)
            zero = jnp.concatenate([zero, zero], axis=0).astype(BF16)
            return w + jnp.tile(zero, (w.shape[0] // zero.shape[0], w.shape[1] // LANES))

        x = xg_ref[0, 0]
        y = None
        for f0 in range(0, ff, MOE_FF_CHUNK):
            f1 = f0 + MOE_FF_CHUNK
            a = _dot(x, weights(wg_ref[0, 0, :, f0:f1]))
            u = _dot(x, weights(wu_ref[0, 0, :, f0:f1]))
            hmid = (a * jax.nn.sigmoid(a) * u).astype(BF16)
            part = _dot(hmid, weights(wd_ref[0, 0, f0:f1, :]))
            y = part if y is None else y + part
        y_ref[slot] = y

    @pl.when(s == 0)
    def _():
        out_ref[...] = jnp.zeros(out_ref.shape, out_ref.dtype)
        ffn(0, None)

    @pl.when(jnp.logical_and(s > 0, s < n_exp))
    def _():
        slot = s % 2
        ffn(slot, 1 - slot)

    @pl.when(s == n_exp)
    def _():
        scatter((n_exp - 1) % 2, range(n_batches))


def _moe_call(layer, idx, aff, xg, w_gate, w_up, w_down, S):
    B, E, cap, D = xg.shape
    F = w_gate.shape[3]
    cur = lambda s: jnp.minimum(s, E - 1)
    prev = lambda s: jnp.maximum(s - 1, 0)
    wspec = lambda shp: pl.BlockSpec(shp, lambda b, s: (layer, cur(s), 0, 0))
    return pl.pallas_call(
        _moe_kernel,
        grid=(B, E + 1),
        in_specs=[pl.BlockSpec((1, 1, cap), lambda b, s: (b * E + prev(s), 0, 0), memory_space=pltpu.SMEM),
                  pl.BlockSpec((1, 1, S), lambda b, s: (b * E + prev(s), 0, 0), memory_space=pltpu.SMEM),
                  pl.BlockSpec((1, 1, cap, D), lambda b, s: (b, cur(s), 0, 0)),
                  wspec((1, 1, D, F)), wspec((1, 1, D, F)), wspec((1, 1, F, D))],
        out_specs=pl.BlockSpec((1, S, SUBLANES, LANES), lambda b, s: (b, 0, 0, 0),
                               pipeline_mode=pl.Buffered(1)),
        out_shape=jax.ShapeDtypeStruct((B, S, SUBLANES, LANES), F32),
        scratch_shapes=[pltpu.VMEM((2, cap, D), F32)],
        compiler_params=_compiler_params(("parallel", "arbitrary")),
        name="moe_ffn",
    )(idx.reshape(B * E, 1, cap), aff.reshape(B * E, 1, S), xg, w_gate, w_up, w_down)


def _final_kernel(alpha, x_ref, moe_ref, g_ref, b_ref, m5_ref, o_ref):
    o_ref[0] = _layer_norm(alpha * x_ref[0] + m5_ref[0] * _tiles_to_rows(moe_ref[0]),
                           g_ref[0], b_ref[0])


def _final_call(layer, alpha, x1, moe, ln2_g, ln2_b, mods, n_ctx):
    B, S, D = x1.shape
    T = TOK_BLOCK
    L = ln2_g.shape[0]
    ncb = n_ctx // T
    tok_in = pl.BlockSpec((1, T, D), lambda b, i: (b, i + ncb, 0))
    const = pl.BlockSpec((1, 1, D), lambda b, i: (layer, 0, 0))
    return pl.pallas_call(
        functools.partial(_final_kernel, alpha),
        grid=(B, (S - n_ctx) // T),
        in_specs=[tok_in, pl.BlockSpec((1, T, SUBLANES, LANES), lambda b, i: (b, i + ncb, 0, 0)),
                  const, const,
                  pl.BlockSpec((1, 1, D), lambda b, i: ((layer * ADA_ROWS + b) * 6 + 5, 0, 0))],
        out_specs=pl.BlockSpec((1, T, D), lambda b, i: (b, i, 0)),
        out_shape=jax.ShapeDtypeStruct((B, S - n_ctx, D), F32),
        compiler_params=_compiler_params(("parallel", "parallel")),
        name="final_norm",
    )(x1, moe, ln2_g.reshape(L, 1, D), ln2_b.reshape(L, 1, D), mods)


def _prep_weights(w_in, w_uq, w_ukv, w_router):
    L, D, _ = w_in.shape
    krs = jnp.concatenate([jnp.zeros((L, D, 64), w_in.dtype), w_in[..., 512:544],
                           jnp.zeros((L, D, 32), w_in.dtype)], axis=-1)
    w_in_p = jnp.concatenate([w_in[..., 0:512], krs, w_in[..., 544:]], axis=-1).astype(BF16)

    kvr = w_ukv.reshape(L, MLA_KV_RANK, MLA_HEADS, MLA_NOPE + MLA_V)
    kn, vv = kvr[..., :MLA_NOPE], kvr[..., MLA_NOPE:]
    zk = jnp.zeros_like(kn)
    w_k = jnp.concatenate([kn, zk], axis=-1).reshape(L, MLA_KV_RANK, MLA_HEADS * LANES)
    v_even = jnp.concatenate([vv, zk], axis=-1)
    v_odd = jnp.concatenate([zk, vv], axis=-1)
    odd = (jnp.arange(MLA_HEADS) % 2 == 1)[None, None, :, None]
    w_v = jnp.where(odd, v_odd, v_even).reshape(L, MLA_KV_RANK, MLA_HEADS * LANES)
    w_kv = jnp.concatenate([w_k, w_v], axis=-1).astype(BF16)

    qr = w_uq.reshape(L, MLA_Q_RANK, MLA_HEADS, MLA_NOPE + MLA_ROPE)
    w_q = jnp.concatenate([qr, jnp.zeros((L, MLA_Q_RANK, MLA_HEADS, 32), w_uq.dtype)], axis=-1)
    w_q = w_q.reshape(L, MLA_Q_RANK, MLA_HEADS * LANES).astype(BF16)

    w_r = jnp.concatenate(
        [w_router, jnp.zeros((L, D, LANES - N_EXPERTS), w_router.dtype)], axis=-1).astype(BF16)
    return w_in_p, w_kv, w_q, w_r


def _ones_columns():
    lane = jnp.arange(LANES)
    even = (lane == 64).astype(F32)
    odd = (lane == 0).astype(F32)
    win = jnp.concatenate([even, odd, even, odd])
    mla = jnp.concatenate([even, odd] * (MLA_HEADS // 2))
    return jnp.concatenate([win, mla])[None, :]


def _rope_tables(n_ctx, n_lat):
    pos = jnp.arange(n_lat)
    rowp = (pos // GRID_W).astype(F32)
    colp = (pos % GRID_W).astype(F32)

    def pattern(rot_dim):
        nf = rot_dim // 4
        inv = ROPE_THETA ** (-jnp.arange(nf, dtype=F32) / nf)
        ar, ac = rowp[:, None] * inv, colp[:, None] * inv
        cos = jnp.concatenate([jnp.cos(ar), jnp.cos(ar), jnp.cos(ac), jnp.cos(ac)], axis=1)
        sin = jnp.concatenate([-jnp.sin(ar), jnp.sin(ar), -jnp.sin(ac), jnp.sin(ac)], axis=1)
        return cos, sin

    cw, sw = pattern(WIN_HEAD_DIM)
    cw, sw = jnp.tile(cw, (1, 2)), jnp.tile(sw, (1, 2))
    cm, sm = pattern(MLA_ROPE)
    one64, zero64 = jnp.ones((n_lat, 64), F32), jnp.zeros((n_lat, 64), F32)
    cm = jnp.concatenate([one64, cm, one64[:, :32]], axis=1)
    sm = jnp.concatenate([zero64, sm, zero64[:, :32]], axis=1)
    sq_w, sq_m = WIN_SCALE * LOG2E, MLA_SCALE * LOG2E
    lat = jnp.concatenate([cw * sq_w, sw * sq_w, cw, sw, cm * sq_m, sm * sq_m, cm, sm], axis=1)
    ones, zeros = jnp.ones((n_ctx, LANES), F32), jnp.zeros((n_ctx, LANES), F32)
    ctx = jnp.concatenate([ones * sq_w, zeros, ones, zeros, ones * sq_m, zeros, ones, zeros], axis=1)
    return jnp.concatenate([ctx, lat], axis=0)


def kernel(x, c, ctx, c_ctx, w_ada, b_ada, w_in, attn_sink, mla_q_norm, mla_kv_norm, w_uq, w_ukv,
           w_oa, w_ob, w_out, ln1_g, ln1_b, w_router, w_exp_gate, w_exp_up, w_exp_down, ln2_g, ln2_b):
    B, n_lat, D = x.shape
    n_ctx = ctx.shape[1]
    depth = w_in.shape[0]
    S = n_ctx + n_lat
    alpha = (2 * depth) ** 0.25
    assert D == D_MODEL and n_ctx == TOK_BLOCK and n_lat % TOK_BLOCK == 0 and B + 1 <= ADA_ROWS
    cap_c = CAPACITY_FACTOR * n_ctx // N_EXPERTS
    cap_l = CAPACITY_FACTOR * n_lat // N_EXPERTS
    assert cap_l % 256 == 0 and cap_c % 8 == 0
    ncb = n_ctx // TOK_BLOCK

    cond = jnp.concatenate([c, c_ctx[None], jnp.zeros((ADA_ROWS - B - 1, D), F32)], axis=0)
    mods = _ada_call(cond.T, w_ada, b_ada, B + 1).reshape(depth * ADA_ROWS * 6, 1, D)

    w_in_p, w_kv, w_q, w_r = _prep_weights(w_in, w_uq, w_ukv, w_router)
    w_oa_b, w_ob_b, w_out_b = w_oa.astype(BF16), w_ob.astype(BF16), w_out.astype(BF16)
    tab = _rope_tables(n_ctx, n_lat)
    vones = _ones_columns()

    x_all = (ctx, x)
    moe = None
    for l in range(depth):
        x_res, (qwin, kwin, vwin, qcat, kcat, vext, sg) = _proj_call(
            l, alpha, x_all, moe, ln2_g, ln2_b, mods, tab, w_in_p, mla_kv_norm, mla_q_norm,
            w_kv, w_q, vones, ncb)
        oa = _win_call(l, attn_sink, qwin, kwin, vwin, n_ctx)
        ob = _mla_call(qcat, kcat, vext, n_ctx)
        x1, h2, lg_t = _merge_call(l, alpha, x_res, oa, ob, sg, w_oa_b, w_ob_b, w_out_b,
                                   ln1_g, ln1_b, mods, w_r, ncb)
        aff, idx4 = _route_call(lg_t, n_ctx, cap_c, cap_l)
        idx = idx4.reshape(B, N_EXPERTS, cap_c + cap_l)
        xg = _gather_call(idx, h2)
        moe = _moe_call(l, idx, aff, xg, w_exp_gate, w_exp_up, w_exp_down, S)
        x_all = x1
    return _final_call(depth - 1, alpha, x_all, moe, ln2_g, ln2_b, mods, n_ctx)
```

```python
import functools
import math

import jax
import jax.numpy as jnp
from jax import lax
from jax.experimental import pallas as pl
from jax.experimental.pallas import tpu as pltpu

D_MODEL = 1024
GRID_W = 64
WIN_HEADS = 8
WIN_KV_HEADS = 2
WIN_HEAD_DIM = 64
BAND = 128
MLA_HEADS = 8
MLA_Q_RANK = 384
MLA_KV_RANK = 256
MLA_NOPE = 64
MLA_ROPE = 32
MLA_V = 64
N_EXPERTS = 16
CAPACITY_FACTOR = 2
ROPE_THETA = 10000.0
LN_EPS = 1e-5
RMS_EPS = 1e-6
NEG_INF = -1e30
LOG2E = math.log2(math.e)
WIN_SCALE = WIN_HEAD_DIM ** -0.5
MLA_SCALE = (MLA_NOPE + MLA_ROPE) ** -0.5

LANES = 128
TOK_BLOCK = 256
VMEM_LIMIT = 56 * 1024 * 1024

BF16 = jnp.bfloat16
F32 = jnp.float32

SEG_KWIN = (0, 128)
SEG_VWIN = (128, 256)
SEG_CKV = (256, 512)
SEG_KR = (512, 640)
SEG_QWIN = (640, 1152)
SEG_CQ = (1152, 1536)
SEG_GATE = (1536, 3584)
IN_COLS_PAD = 3584


def _dot(a, b):
    return jnp.dot(a, b, preferred_element_type=F32)


def _dot_nt(a, b):
    return lax.dot_general(a, b, (((1,), (1,)), ((), ())), preferred_element_type=F32)


def _layer_norm(z, g, b):
    mu = jnp.mean(z, axis=-1, keepdims=True)
    zc = z - mu
    var = jnp.mean(zc * zc, axis=-1, keepdims=True)
    return zc * lax.rsqrt(var + LN_EPS) * g + b


def _rms_norm(x, g):
    return x * lax.rsqrt(jnp.mean(x * x, axis=-1, keepdims=True) + RMS_EPS) * g


def _rope_slab(x, cos, sin, half):
    lane = lax.broadcasted_iota(jnp.int32, x.shape, 1)
    partner = jnp.where((lane & half) == 0,
                        pltpu.roll(x, LANES - half, 1), pltpu.roll(x, half, 1))
    return x * cos + partner * sin


SUBLANES = 8


def _rows_to_tiles(y):
    n = y.shape[0]
    y3 = pltpu.einshape("r(sl)->srl", y, s=SUBLANES)
    y4 = y3.reshape(SUBLANES, n // SUBLANES, SUBLANES, LANES)
    return jnp.transpose(y4, (1, 2, 0, 3)).reshape(n, SUBLANES, LANES)


def _tiles_to_rows(x3):
    n = x3.shape[0]
    x4 = x3.reshape(n // SUBLANES, SUBLANES, SUBLANES, LANES)
    xs = jnp.transpose(x4, (2, 0, 1, 3)).reshape(SUBLANES, n, LANES)
    return pltpu.einshape("srl->r(sl)", xs)


def _compiler_params(sem):
    return pltpu.CompilerParams(dimension_semantics=sem, vmem_limit_bytes=VMEM_LIMIT)


ADA_ROWS = 8
ADA_TN = 512


def _ada_kernel(n_rows, condt_ref, w_ref, b_ref, o_ref):
    ct = condt_ref[...]
    st = ct * jax.nn.sigmoid(ct)
    w = w_ref[0]
    rows = []
    for r in range(n_rows):
        rows.append(jnp.sum(w * st[:, r:r + 1], axis=0, keepdims=True) + b_ref[0])
    rows.append(jnp.zeros((ADA_ROWS - n_rows, w.shape[1]), F32))
    o_ref[0] = jnp.concatenate(rows, axis=0)


def _ada_call(cond_t, w_ada, b_ada, n_rows):
    L, D, N = w_ada.shape
    return pl.pallas_call(
        functools.partial(_ada_kernel, n_rows),
        grid=(L, N // ADA_TN),
        in_specs=[
            pl.BlockSpec((D, ADA_ROWS), lambda l, j: (0, 0)),
            pl.BlockSpec((1, D, ADA_TN), lambda l, j: (l, 0, j)),
            pl.BlockSpec((1, 1, ADA_TN), lambda l, j: (l, 0, j)),
        ],
        out_specs=pl.BlockSpec((1, ADA_ROWS, ADA_TN), lambda l, j: (l, 0, j)),
        out_shape=jax.ShapeDtypeStruct((L, ADA_ROWS, N), F32),
        compiler_params=_compiler_params(("parallel", "parallel")),
        name="adaln_mod",
    )(cond_t, w_ada, b_ada.reshape(L, 1, N))


def _proj_kernel(has_ln, alpha, n_ctx_blocks, *refs):
    if has_ln:
        (x_ref, moe_ref, g2_ref, b2_ref, m5_ref, sh_ref, sc_ref, tab_ref, win_ref, gkv_ref, gq_ref,
         wkv_ref, wq_ref, vones_ref,
         xres_ref, qwin_ref, kwin_ref, vwin_ref, qcat_ref, kcat_ref, vext_ref, sg_ref) = refs
        z = alpha * x_ref[0] + m5_ref[0] * _tiles_to_rows(moe_ref[0])
        x = _layer_norm(z, g2_ref[0], b2_ref[0])
    else:
        (ctx_ref, x_ref, sh_ref, sc_ref, tab_ref, win_ref, gkv_ref, gq_ref, wkv_ref, wq_ref, vones_ref,
         xres_ref, qwin_ref, kwin_ref, vwin_ref, qcat_ref, kcat_ref, vext_ref, sg_ref) = refs
        x = jnp.where(pl.program_id(0) < n_ctx_blocks, ctx_ref[0], x_ref[0])
    xres_ref[0] = x
    hb = (x * (1.0 + sc_ref[0]) + sh_ref[0]).astype(BF16)

    def seg(s):
        return _dot(hb, win_ref[0, :, s[0]:s[1]])

    def tab(i):
        return tab_ref[:, i * LANES:(i + 1) * LANES]

    cq_w, sq_w, ck_w, sk_w, cq_m, sq_m, ck_m, sk_m = (tab(i) for i in range(8))

    lane_lo = lax.broadcasted_iota(jnp.int32, (hb.shape[0], LANES), 1) < WIN_HEAD_DIM
    half_swap = lambda t: pltpu.roll(t, WIN_HEAD_DIM, 1)

    ckv = _rms_norm(seg(SEG_CKV), gkv_ref[0]).astype(BF16)
    cq = _rms_norm(seg(SEG_CQ), gq_ref[0]).astype(BF16)

    qw = seg(SEG_QWIN)
    q_slabs = []
    for s2 in range(WIN_HEADS // 2):
        pair = _rope_slab(qw[:, s2 * LANES:(s2 + 1) * LANES], cq_w, sq_w, 16)
        pair_sw = half_swap(pair)
        if 2 * s2 < WIN_HEADS // WIN_KV_HEADS:
            q_slabs += [jnp.where(lane_lo, pair, 0.0), jnp.where(lane_lo, pair_sw, 0.0)]
        else:
            q_slabs += [jnp.where(lane_lo, 0.0, pair_sw), jnp.where(lane_lo, 0.0, pair)]
    qwin_ref[0] = jnp.concatenate(q_slabs, axis=1).astype(BF16)

    kv = _dot(ckv, wkv_ref[0])
    kr = _rope_slab(seg(SEG_KR), ck_m, sk_m, 8)
    qm = _dot(cq, wq_ref[0])
    for h in range(MLA_HEADS):
        sl = slice(h * LANES, (h + 1) * LANES)
        kcat_ref[0, :, sl] = (kv[:, sl] + kr).astype(BF16)
        vext_ref[0, :, sl] = (kv[:, 1024 + h * LANES:1024 + (h + 1) * LANES]
                              + vones_ref[:, 512 + h * LANES:512 + (h + 1) * LANES]).astype(BF16)
        qcat_ref[0, :, sl] = _rope_slab(qm[:, sl], cq_m, sq_m, 8).astype(BF16)

    sg_ref[0] = jax.nn.sigmoid(seg(SEG_GATE)).astype(BF16)
    va = seg(SEG_VWIN)
    va_sw = half_swap(va)
    vwin_ref[0] = (jnp.concatenate(
        [jnp.where(lane_lo, va, 0.0), jnp.where(lane_lo, 0.0, va_sw),
         jnp.where(lane_lo, va_sw, 0.0), jnp.where(lane_lo, 0.0, va)], axis=1)
        + vones_ref[:, 0:512]).astype(BF16)
    kwin_ref[0] = _rope_slab(seg(SEG_KWIN), ck_w, sk_w, 16).astype(BF16)


def _proj_call(layer, alpha, x_all, moe, ln2_g, ln2_b, mods, tab, w_in_p, g_kvn, g_qn, w_kv, w_q,
               vones, n_ctx_blocks):
    T = TOK_BLOCK
    has_ln = moe is not None
    if has_ln:
        B, S, D = x_all.shape
    else:
        ctx_in, x_in = x_all
        B, D = x_in.shape[0], x_in.shape[2]
        S = ctx_in.shape[1] + x_in.shape[1]
    L = w_in_p.shape[0]

    def mod_spec(k, lyr):
        def imap(i, b):
            row = jnp.where(i < n_ctx_blocks, B, b)
            return ((lyr * ADA_ROWS + row) * 6 + k, 0, 0)
        return pl.BlockSpec((1, 1, D), imap)

    tok = lambda w: pl.BlockSpec((1, T, w), lambda i, b: (b, i, 0))
    headed = tok(MLA_HEADS * LANES)
    const = lambda shp: pl.BlockSpec(shp, lambda i, b: (layer,) + (0,) * (len(shp) - 1),
                                     pipeline_mode=pl.Buffered(1))

    if has_ln:
        prev = pl.BlockSpec((1, 1, D), lambda i, b: (layer - 1, 0, 0))
        tiled = pl.BlockSpec((1, T, SUBLANES, LANES), lambda i, b: (b, i, 0, 0))
        in_specs = [tok(D), tiled, prev, prev, mod_spec(5, layer - 1)]
        args = [x_all, moe, ln2_g.reshape(L, 1, D), ln2_b.reshape(L, 1, D), mods]
    else:
        assert ctx_in.shape[1] == n_ctx_blocks * T
        in_specs = [pl.BlockSpec((1, T, D), lambda i, b: (b, jnp.minimum(i, n_ctx_blocks - 1), 0)),
                    pl.BlockSpec((1, T, D), lambda i, b: (b, jnp.maximum(i - n_ctx_blocks, 0), 0))]
        args = [ctx_in, x_in]
    in_specs += [
        mod_spec(0, layer), mod_spec(1, layer),
        pl.BlockSpec((T, 8 * LANES), lambda i, b: (i, 0)),
        const((1, D, IN_COLS_PAD)),
        const((1, 1, MLA_KV_RANK)), const((1, 1, MLA_Q_RANK)),
        const((1, MLA_KV_RANK, 2048)), const((1, MLA_Q_RANK, 1024)),
        pl.BlockSpec((1, 1536), lambda i, b: (0, 0)),
    ]
    args += [mods, mods, tab, w_in_p, g_kvn.reshape(L, 1, -1), g_qn.reshape(L, 1, -1), w_kv, w_q, vones]

    out_specs, out_shape = [tok(D)], [jax.ShapeDtypeStruct((B, S, D), F32)]
    out_specs += [tok(WIN_HEADS * LANES), tok(LANES), tok(512), headed, headed, headed, tok(2048)]
    out_shape += [jax.ShapeDtypeStruct((B, S, WIN_HEADS * LANES), BF16),
                  jax.ShapeDtypeStruct((B, S, LANES), BF16),
                  jax.ShapeDtypeStruct((B, S, 512), BF16)]
    out_shape += [jax.ShapeDtypeStruct((B, S, MLA_HEADS * LANES), BF16)] * 3
    out_shape += [jax.ShapeDtypeStruct((B, S, 2048), BF16)]

    outs = pl.pallas_call(
        functools.partial(_proj_kernel, has_ln, alpha, n_ctx_blocks),
        grid=(S // T, B),
        in_specs=in_specs, out_specs=out_specs, out_shape=out_shape,
        compiler_params=_compiler_params(("parallel", "parallel")),
        name="proj",
    )(*args)
    return outs[0], outs[1:]


WIN_HEAD_ORDER = (0, 2, 1, 3, 4, 6, 5, 7)


def _win_kernel(layer, n_ctx, n_steps, sink_ref, q_ref, kc_ref, vc_ref, kp_ref, km_ref, kn_ref,
                vp_ref, vm_ref, vn_ref, o_ref, kall_ref, vall_ref, p_ref):
    i = pl.program_id(1)
    row = lax.broadcasted_iota(jnp.int32, (BAND, LANES), 0)
    col = lax.broadcasted_iota(jnp.int32, (BAND, LANES), 1)
    lane_lo = col < 64
    kall_ref[0:n_ctx, :] = kc_ref[0]
    vall_ref[0:n_ctx, :] = vc_ref[0]

    def scores(qb, r0, nk):
        q = q_ref[0, qb * BAND:(qb + 1) * BAND, :]
        q8 = jnp.concatenate([q[:, h * LANES:(h + 1) * LANES] for h in WIN_HEAD_ORDER], axis=0)
        return _dot_nt(q8, kall_ref[r0:r0 + nk, :])

    def softmax(qb, nk, masks, s):
        e_sink = []
        for c, h in enumerate(WIN_HEAD_ORDER):
            sc = s[c * BAND:(c + 1) * BAND, :]
            if masks:
                blocks = []
                for kb in range(nk // BAND):
                    blk = sc[:, kb * BAND:(kb + 1) * BAND]
                    blocks.append(jnp.where(masks[kb], blk, NEG_INF) if kb in masks else blk)
                sc = jnp.concatenate(blocks, axis=1)
            sink = sink_ref[layer, h] * LOG2E
            m = jnp.maximum(jnp.max(sc, axis=1, keepdims=True), sink)
            p_ref[qb, c * BAND:(c + 1) * BAND, 0:nk] = jnp.exp2(sc - m).astype(BF16)
            e_sink.append(jnp.exp2(sink - m))
        return e_sink

    def values(qb, r0, nk, e_sink):
        pairs = []
        for pi in range(4):
            acc = _dot(p_ref[qb, pi * 2 * BAND:(pi + 1) * 2 * BAND, 0:nk],
                       vall_ref[r0:r0 + nk, pi * LANES:(pi + 1) * LANES])
            es = jnp.concatenate([e_sink[2 * pi], e_sink[2 * pi + 1]], axis=0)
            den = (acc[:, 64:65] if pi % 2 == 0 else acc[:, 0:1]) + es
            pairs.append(acc / den)
        slabs = [jnp.where(lane_lo, pairs[0][:BAND], pairs[1][:BAND]),
                 jnp.where(lane_lo, pairs[0][BAND:], pairs[1][BAND:]),
                 jnp.where(lane_lo, pairs[2][:BAND], pairs[3][:BAND]),
                 jnp.where(lane_lo, pairs[2][BAND:], pairs[3][BAND:])]
        o_ref[0, qb * BAND:(qb + 1) * BAND, :] = jnp.concatenate(slabs, axis=1).astype(o_ref.dtype)

    def attend_both(args0, args1):
        (r0, nk0, m0), (r1, nk1, m1) = args0, args1
        s0 = scores(0, r0, nk0)
        s1 = scores(1, r1, nk1)
        e0 = softmax(0, nk0, m0, s0)
        e1 = softmax(1, nk1, m1, s1)
        values(0, r0, nk0, e0)
        values(1, r1, nk1, e1)

    @pl.when(i == 0)
    def _():
        attend_both((0, n_ctx, None), (0, n_ctx, None))

    @pl.when(i > 0)
    def _():
        r = n_ctx
        for k_ref, v_ref, n in ((kp_ref, vp_ref, BAND), (km_ref, vm_ref, 2 * BAND),
                                (kn_ref, vn_ref, BAND), (kc_ref, vc_ref, n_ctx)):
            kall_ref[r:r + n, :] = k_ref[0]
            vall_ref[r:r + n, :] = v_ref[0]
            r += n
        off_first = jnp.where(i == 1, 2 * LANES, 0)
        off_last = jnp.where(i == n_steps - 1, 2 * LANES, 0)
        cb = n_ctx // BAND
        attend_both((0, n_ctx + 3 * BAND, {cb: col >= row + off_first, cb + 2: col <= row}),
                    (n_ctx + BAND, n_ctx + 3 * BAND, {0: col >= row, 2: col <= row - off_last}))


def _win_call(layer, sink, qwin, kwin, vwin, n_ctx):
    B, S, _ = qwin.shape
    nb = S // BAND
    ncb = n_ctx // BAND
    assert n_ctx == 2 * BAND and nb % 2 == 0
    nk = n_ctx + 3 * BAND
    one = lambda w, f: pl.BlockSpec((1, BAND, w), lambda b, i: (b, jnp.clip(f(i), ncb, nb - 1), 0))
    two = lambda w: pl.BlockSpec((1, 2 * BAND, w), lambda b, i: (b, i, 0))
    ctx = lambda w: pl.BlockSpec((1, n_ctx, w), lambda b, i: (b, 0, 0))
    prev, nxt = (lambda i: 2 * i - 1), (lambda i: 2 * i + 2)
    return pl.pallas_call(
        functools.partial(_win_kernel, layer, n_ctx, nb // 2),
        grid=(B, nb // 2),
        in_specs=[pl.BlockSpec(memory_space=pltpu.SMEM), two(WIN_HEADS * LANES),
                  ctx(LANES), ctx(512), one(LANES, prev), two(LANES), one(LANES, nxt),
                  one(512, prev), two(512), one(512, nxt)],
        out_specs=two(512),
        out_shape=jax.ShapeDtypeStruct((B, S, 512), BF16),
        scratch_shapes=[pltpu.VMEM((2 * n_ctx + 4 * BAND, LANES), BF16),
                        pltpu.VMEM((2 * n_ctx + 4 * BAND, 512), BF16),
                        pltpu.VMEM((2, WIN_HEADS * BAND, nk), BF16)],
        compiler_params=_compiler_params(("parallel", "parallel")),
        name="win_attn",
    )(sink, qwin, kwin, vwin, kwin, kwin, kwin, vwin, vwin, vwin)


MLA_KCHUNK = 256
MLA_HEADS_PER_STEP = 4


def _mla_kernel(n_ctx, n_keys, q_ref, k_ref, v_ref, o_ref, s_ref):
    qi = pl.program_id(2)
    tq = q_ref.shape[1]
    lane = lax.broadcasted_iota(jnp.int32, (tq, LANES), 1)

    def run(nk):
        row_max = []
        for hh in range(MLA_HEADS_PER_STEP):
            hl = slice(hh * LANES, (hh + 1) * LANES)
            q = q_ref[0, :, hl]
            mrun = None
            for c in range(nk // MLA_KCHUNK):
                s = _dot_nt(q, k_ref[0, c * MLA_KCHUNK:(c + 1) * MLA_KCHUNK, hl])
                s_ref[hh, c] = s
                mc = jnp.maximum(s[:, :LANES], s[:, LANES:])
                mrun = mc if mrun is None else jnp.maximum(mrun, mc)
            row_max.append(jnp.max(mrun, axis=1, keepdims=True))
        outs = []
        for hh in range(MLA_HEADS_PER_STEP):
            acc = jnp.zeros((tq, LANES), F32)
            for c in range(nk // MLA_KCHUNK):
                p = jnp.exp2(s_ref[hh, c] - row_max[hh]).astype(BF16)
                acc = acc + _dot(p, v_ref[0, c * MLA_KCHUNK:(c + 1) * MLA_KCHUNK,
                                          hh * LANES:(hh + 1) * LANES])
            den = acc[:, 64:65] if hh % 2 == 0 else acc[:, 0:1]
            outs.append(acc / den)
        o_ref[0] = jnp.concatenate(
            [jnp.where(lane < 64, outs[2 * j], outs[2 * j + 1]) for j in range(len(outs) // 2)],
            axis=1).astype(o_ref.dtype)

    @pl.when(qi == 0)
    def _():
        run(n_ctx)

    @pl.when(qi > 0)
    def _():
        run(n_keys)


def _mla_call(qcat, kcat, vext, n_ctx):
    B, S, _ = qcat.shape
    H = MLA_HEADS
    T = TOK_BLOCK
    assert n_ctx == T
    G = MLA_HEADS_PER_STEP
    kv = pl.BlockSpec((1, S, G * LANES), lambda b, hp, i: (b, 0, hp))
    return pl.pallas_call(
        functools.partial(_mla_kernel, n_ctx, S),
        grid=(B, H // G, S // T),
        in_specs=[pl.BlockSpec((1, T, G * LANES), lambda b, hp, i: (b, i, hp)), kv, kv],
        out_specs=pl.BlockSpec((1, T, (G // 2) * LANES), lambda b, hp, i: (b, i, hp)),
        out_shape=jax.ShapeDtypeStruct((B, S, (H // 2) * LANES), BF16),
        scratch_shapes=[pltpu.VMEM((G, S // MLA_KCHUNK, T, MLA_KCHUNK), F32)],
        compiler_params=_compiler_params(("parallel", "parallel", "arbitrary")),
        name="mla_attn",
    )(qcat, kcat, vext)


def _merge_kernel(alpha, x_ref, oa_ref, ob_ref, sg_ref, woa_ref, wob_ref, wout_ref, g1_ref, b1_ref,
                  m2_ref, m3_ref, m4_ref, wr_ref, x1_ref, h2_ref, lg_ref):
    D = x_ref.shape[2]
    sg = sg_ref[0]
    t = (sg[:, :D].astype(F32) * _dot(oa_ref[0], woa_ref[0])
         + sg[:, D:].astype(F32) * _dot(ob_ref[0], wob_ref[0]))
    y = _dot(t.astype(BF16), wout_ref[0])
    x1 = _layer_norm(alpha * x_ref[0] + m2_ref[0] * y, g1_ref[0], b1_ref[0])
    x1_ref[0] = x1
    h2 = x1 * (1.0 + m4_ref[0]) + m3_ref[0]
    h2_ref[0] = _rows_to_tiles(h2)
    lg = _dot(h2.astype(BF16), wr_ref[0])
    lg_ref[0] = lg.T[:N_EXPERTS]


def _merge_call(layer, alpha, x_res, oa, ob, sg, w_oa, w_ob, w_out, ln1_g, ln1_b, mods, w_r,
                n_ctx_blocks):
    B, S, D = x_res.shape
    T = TOK_BLOCK
    L = w_oa.shape[0]

    def mod_spec(k):
        def imap(b, i):
            row = jnp.where(i < n_ctx_blocks, B, b)
            return ((layer * ADA_ROWS + row) * 6 + k, 0, 0)
        return pl.BlockSpec((1, 1, D), imap)

    tok = lambda w: pl.BlockSpec((1, T, w), lambda b, i: (b, i, 0))
    const = lambda shp: pl.BlockSpec(shp, lambda b, i: (layer,) + (0,) * (len(shp) - 1),
                                     pipeline_mode=pl.Buffered(1))
    return pl.pallas_call(
        functools.partial(_merge_kernel, alpha),
        grid=(B, S // T),
        in_specs=[tok(D), tok(512), tok(512), tok(2048),
                  const((1, 512, D)), const((1, 512, D)), const((1, D, D)),
                  const((1, 1, D)), const((1, 1, D)),
                  mod_spec(2), mod_spec(3), mod_spec(4),
                  const((1, D, LANES))],
        out_specs=[tok(D), pl.BlockSpec((1, T, SUBLANES, LANES), lambda b, i: (b, i, 0, 0)),
                   pl.BlockSpec((1, N_EXPERTS, T), lambda b, i: (b, 0, i))],
        out_shape=[jax.ShapeDtypeStruct((B, S, D), F32),
                   jax.ShapeDtypeStruct((B, S, SUBLANES, LANES), F32),
                   jax.ShapeDtypeStruct((B, N_EXPERTS, S), F32)],
        compiler_params=_compiler_params(("parallel", "parallel")),
        name="merge",
    )(x_res, oa, ob, sg, w_oa, w_ob, w_out, ln1_g.reshape(L, 1, D), ln1_b.reshape(L, 1, D),
      mods, mods, mods, w_r)


def _cumsum_lanes(x):
    n = x.shape[1]
    xb = x.astype(BF16)
    r = lax.broadcasted_iota(jnp.int32, (LANES, LANES), 0)
    c = lax.broadcasted_iota(jnp.int32, (LANES, LANES), 1)
    tri = jnp.where(r <= c, 1.0, 0.0).astype(BF16)
    tb = lax.broadcasted_iota(jnp.int32, (n, LANES), 0) // LANES
    kb = lax.broadcasted_iota(jnp.int32, (n, LANES), 1)
    before = jnp.where(tb < kb, 1.0, 0.0).astype(BF16)
    off = _dot(xb, before)
    outs = []
    for k in range(n // LANES):
        outs.append(_dot(xb[:, k * LANES:(k + 1) * LANES], tri) + off[:, k:k + 1])
    return outs


def _select_top(aff, cap):
    bits = pltpu.bitcast(aff, jnp.int32)
    thr = jnp.zeros((aff.shape[0], 1), jnp.int32)
    for bit in range(30, -1, -1):
        cand = thr | (1 << bit)
        cnt = jnp.sum(jnp.where(bits >= cand, 1.0, 0.0), axis=1, keepdims=True)
        thr = jnp.where(cnt >= cap, cand, thr)
    gt = bits > thr
    eq = jnp.where(bits == thr, 1.0, 0.0)
    need = cap - jnp.sum(jnp.where(gt, 1.0, 0.0), axis=1, keepdims=True)
    eq_rank = jnp.concatenate(_cumsum_lanes(eq), axis=1) - eq
    sel = jnp.where(jnp.logical_or(gt, jnp.logical_and(eq > 0.5, eq_rank < need)), 1.0, 0.0)
    return _cumsum_lanes(sel)


ROUTE_PAD = 1e6


def _route_kernel(n_ctx, cap_c, cap_l, lg_ref, aff_ref, idx_ref, cc_ref, cl_ref):
    lg = lg_ref[0]
    m = jnp.max(lg, axis=0, keepdims=True)
    ex = jnp.exp(lg - m)
    aff = ex / jnp.sum(ex, axis=0, keepdims=True)
    aff_ref[0] = aff
    for c_ref, blocks in ((cc_ref, _select_top(aff[:, :n_ctx], cap_c)),
                          (cl_ref, _select_top(aff[:, n_ctx:], cap_l))):
        c_ref[...] = jnp.full(c_ref.shape, ROUTE_PAD, F32)
        for k, blk in enumerate(blocks):
            for ex_i in range(N_EXPERTS):
                c_ref[ex_i, k:k + 1, :] = blk[ex_i:ex_i + 1, :]

    def slots(c_ref, e, n_blocks, width):
        cm = c_ref[e]
        jrow = lax.broadcasted_iota(jnp.int32, (1, width), 1).astype(F32)
        rows = max(n_blocks, SUBLANES)
        cend = cm[0:rows, LANES - 1:LANES]
        blk = jnp.sum(jnp.where(cend <= jrow, 1.0, 0.0), axis=0, keepdims=True)
        kio = lax.broadcasted_iota(jnp.int32, (LANES, width), 0).astype(F32)
        onehot = jnp.where(kio == blk, 1.0, 0.0).astype(BF16)
        cmt = cm.T
        hi = jnp.floor(cmt * (1.0 / 256.0))
        lo = cmt - 256.0 * hi
        cg = 256.0 * _dot(hi.astype(BF16), onehot) + _dot(lo.astype(BF16), onehot)
        inside = jnp.sum(jnp.where(cg <= jrow, 1.0, 0.0), axis=0, keepdims=True)
        return (blk * LANES + inside).astype(jnp.int32)

    def per_expert(e, carry):
        n_lat_blocks = (lg_ref.shape[2] - n_ctx) // LANES
        idx_ref[0, e, :, 0:cap_l] = slots(cl_ref, e, n_lat_blocks, cap_l) + n_ctx
        idx_ref[0, e, :, cap_l:cap_l + cap_c] = slots(cc_ref, e, n_ctx // LANES, LANES)[:, 0:cap_c]
        return carry

    lax.fori_loop(0, N_EXPERTS, per_expert, 0)


def _route_call(lg_t, n_ctx, cap_c, cap_l):
    B, E, S = lg_t.shape
    cap = cap_c + cap_l
    assert (S - n_ctx) // LANES <= LANES and cap_l % LANES == 0 and cap_c <= LANES
    return pl.pallas_call(
        functools.partial(_route_kernel, n_ctx, cap_c, cap_l),
        grid=(B,),
        in_specs=[pl.BlockSpec((1, E, S), lambda b: (b, 0, 0))],
        out_specs=[pl.BlockSpec((1, E, S), lambda b: (b, 0, 0)),
                   pl.BlockSpec((1, E, 1, cap), lambda b: (b, 0, 0, 0))],
        out_shape=[jax.ShapeDtypeStruct((B, E, S), F32),
                   jax.ShapeDtypeStruct((B, E, 1, cap), jnp.int32)],
        scratch_shapes=[pltpu.VMEM((E, LANES, LANES), F32), pltpu.VMEM((E, LANES, LANES), F32)],
        compiler_params=_compiler_params(("parallel",)),
        name="route",
    )(lg_t)


ROW_BATCH = 16
MOE_FF_CHUNK = 512


def _gather_kernel(idx_ref, h_ref, xg_ref, tmp_ref):
    cap = xg_ref.shape[2]

    def body(g, carry):
        for r in range(ROW_BATCH):
            j = g * ROW_BATCH + r
            tmp_ref[j] = h_ref[0, idx_ref[0, 0, j]]
        return carry

    lax.fori_loop(0, cap // ROW_BATCH, body, 0)
    xg_ref[0, 0] = _tiles_to_rows(tmp_ref[...]).astype(xg_ref.dtype)


def _gather_call(idx, h2t):
    B, S = h2t.shape[0], h2t.shape[1]
    D = SUBLANES * LANES
    E, cap = idx.shape[1], idx.shape[2]
    assert cap % ROW_BATCH == 0
    return pl.pallas_call(
        _gather_kernel,
        grid=(B, E),
        in_specs=[pl.BlockSpec((1, 1, cap), lambda b, e: (b * E + e, 0, 0), memory_space=pltpu.SMEM),
                  pl.BlockSpec((1, S, SUBLANES, LANES), lambda b, e: (b, 0, 0, 0),
                               pipeline_mode=pl.Buffered(1))],
        out_specs=pl.BlockSpec((1, 1, cap, D), lambda b, e: (b, e, 0, 0)),
        out_shape=jax.ShapeDtypeStruct((B, E, cap, D), BF16),
        scratch_shapes=[pltpu.VMEM((cap, SUBLANES, LANES), F32)],
        compiler_params=_compiler_params(("parallel", "arbitrary")),
        name="moe_gather",
    )(idx.reshape(B * E, 1, cap), h2t)


def _moe_kernel(idx_ref, aff_ref, xg_ref, wg_ref, wu_ref, wd_ref, out_ref, y_ref):
    s = pl.program_id(1)
    n_exp = pl.num_programs(1) - 1
    cap = y_ref.shape[1]
    n_batches = cap // ROW_BATCH

    def scatter(slot, batches):
        new = None
        for g in batches:
            rows = _rows_to_tiles(y_ref[slot, g * ROW_BATCH:(g + 1) * ROW_BATCH, :])
            toks = [idx_ref[0, 0, g * ROW_BATCH + r] for r in range(ROW_BATCH)]
            old = [out_ref[0, t] for t in toks]
            for r, t in enumerate(toks):
                new = old[r] + rows[r] * aff_ref[0, 0, t]
                out_ref[0, t] = new
        return new

    def ffn(slot, prev_slot):
        ff = wg_ref.shape[3]
        n_dots = 3 * (ff // MOE_FF_CHUNK)
        per = -(-n_batches // n_dots)
        todo = list(range(n_batches)) if prev_slot is not None else []

        def weights(w):
            take, todo[:] = todo[:per], todo[per:]
            w = w.astype(BF16)
            if not take:
                return w
            last = scatter(prev_slot, take)
            bits = pltpu.bitcast(last, jnp.uint32)
            zero = pltpu.bitcast((bits >> 16) >> 16, F32)
            zero = jnp.concatenate([zero, zero], axis=0).astype(BF16)
            return w + jnp.tile(zero, (w.shape[0] // zero.shape[0], w.shape[1] // LANES))

        x = xg_ref[0, 0]
        y = None
        for f0 in range(0, ff, MOE_FF_CHUNK):
            f1 = f0 + MOE_FF_CHUNK
            a = _dot(x, weights(wg_ref[0, 0, :, f0:f1]))
            u = _dot(x, weights(wu_ref[0, 0, :, f0:f1]))
            hmid = (a * jax.nn.sigmoid(a) * u).astype(BF16)
            part = _dot(hmid, weights(wd_ref[0, 0, f0:f1, :]))
            y = part if y is None else y + part
        y_ref[slot] = y

    @pl.when(s == 0)
    def _():
        out_ref[...] = jnp.zeros(out_ref.shape, out_ref.dtype)
        ffn(0, None)

    @pl.when(jnp.logical_and(s > 0, s < n_exp))
    def _():
        slot = s % 2
        ffn(slot, 1 - slot)

    @pl.when(s == n_exp)
    def _():
        scatter((n_exp - 1) % 2, range(n_batches))


def _moe_call(layer, idx, aff, xg, w_gate, w_up, w_down, S):
    B, E, cap, D = xg.shape
    F = w_gate.shape[3]
    cur = lambda s: jnp.minimum(s, E - 1)
    prev = lambda s: jnp.maximum(s - 1, 0)
    wspec = lambda shp: pl.BlockSpec(shp, lambda b, s: (layer, cur(s), 0, 0))
    return pl.pallas_call(
        _moe_kernel,
        grid=(B, E + 1),
        in_specs=[pl.BlockSpec((1, 1, cap), lambda b, s: (b * E + prev(s), 0, 0), memory_space=pltpu.SMEM),
                  pl.BlockSpec((1, 1, S), lambda b, s: (b * E + prev(s), 0, 0), memory_space=pltpu.SMEM),
                  pl.BlockSpec((1, 1, cap, D), lambda b, s: (b, cur(s), 0, 0)),
                  wspec((1, 1, D, F)), wspec((1, 1, D, F)), wspec((1, 1, F, D))],
        out_specs=pl.BlockSpec((1, S, SUBLANES, LANES), lambda b, s: (b, 0, 0, 0),
                               pipeline_mode=pl.Buffered(1)),
        out_shape=jax.ShapeDtypeStruct((B, S, SUBLANES, LANES), F32),
        scratch_shapes=[pltpu.VMEM((2, cap, D), F32)],
        compiler_params=_compiler_params(("parallel", "arbitrary")),
        name="moe_ffn",
    )(idx.reshape(B * E, 1, cap), aff.reshape(B * E, 1, S), xg, w_gate, w_up, w_down)


def _final_kernel(alpha, x_ref, moe_ref, g_ref, b_ref, m5_ref, o_ref):
    o_ref[0] = _layer_norm(alpha * x_ref[0] + m5_ref[0] * _tiles_to_rows(moe_ref[0]),
                           g_ref[0], b_ref[0])


def _final_call(layer, alpha, x1, moe, ln2_g, ln2_b, mods, n_ctx):
    B, S, D = x1.shape
    T = TOK_BLOCK
    L = ln2_g.shape[0]
    ncb = n_ctx // T
    tok_in = pl.BlockSpec((1, T, D), lambda b, i: (b, i + ncb, 0))
    const = pl.BlockSpec((1, 1, D), lambda b, i: (layer, 0, 0))
    return pl.pallas_call(
        functools.partial(_final_kernel, alpha),
        grid=(B, (S - n_ctx) // T),
        in_specs=[tok_in, pl.BlockSpec((1, T, SUBLANES, LANES), lambda b, i: (b, i + ncb, 0, 0)),
                  const, const,
                  pl.BlockSpec((1, 1, D), lambda b, i: ((layer * ADA_ROWS + b) * 6 + 5, 0, 0))],
        out_specs=pl.BlockSpec((1, T, D), lambda b, i: (b, i, 0)),
        out_shape=jax.ShapeDtypeStruct((B, S - n_ctx, D), F32),
        compiler_params=_compiler_params(("parallel", "parallel")),
        name="final_norm",
    )(x1, moe, ln2_g.reshape(L, 1, D), ln2_b.reshape(L, 1, D), mods)


def _prep_weights(w_in, w_uq, w_ukv, w_router):
    L, D, _ = w_in.shape
    krs = jnp.concatenate([jnp.zeros((L, D, 64), w_in.dtype), w_in[..., 512:544],
                           jnp.zeros((L, D, 32), w_in.dtype)], axis=-1)
    w_in_p = jnp.concatenate([w_in[..., 0:512], krs, w_in[..., 544:]], axis=-1).astype(BF16)

    kvr = w_ukv.reshape(L, MLA_KV_RANK, MLA_HEADS, MLA_NOPE + MLA_V)
    kn, vv = kvr[..., :MLA_NOPE], kvr[..., MLA_NOPE:]
    zk = jnp.zeros_like(kn)
    w_k = jnp.concatenate([kn, zk], axis=-1).reshape(L, MLA_KV_RANK, MLA_HEADS * LANES)
    v_even = jnp.concatenate([vv, zk], axis=-1)
    v_odd = jnp.concatenate([zk, vv], axis=-1)
    odd = (jnp.arange(MLA_HEADS) % 2 == 1)[None, None, :, None]
    w_v = jnp.where(odd, v_odd, v_even).reshape(L, MLA_KV_RANK, MLA_HEADS * LANES)
    w_kv = jnp.concatenate([w_k, w_v], axis=-1).astype(BF16)

    qr = w_uq.reshape(L, MLA_Q_RANK, MLA_HEADS, MLA_NOPE + MLA_ROPE)
    w_q = jnp.concatenate([qr, jnp.zeros((L, MLA_Q_RANK, MLA_HEADS, 32), w_uq.dtype)], axis=-1)
    w_q = w_q.reshape(L, MLA_Q_RANK, MLA_HEADS * LANES).astype(BF16)

    w_r = jnp.concatenate(
        [w_router, jnp.zeros((L, D, LANES - N_EXPERTS), w_router.dtype)], axis=-1).astype(BF16)
    return w_in_p, w_kv, w_q, w_r


def _ones_columns():
    lane = jnp.arange(LANES)
    even = (lane == 64).astype(F32)
    odd = (lane == 0).astype(F32)
    win = jnp.concatenate([even, odd, even, odd])
    mla = jnp.concatenate([even, odd] * (MLA_HEADS // 2))
    return jnp.concatenate([win, mla])[None, :]


def _rope_tables(n_ctx, n_lat):
    pos = jnp.arange(n_lat)
    rowp = (pos // GRID_W).astype(F32)
    colp = (pos % GRID_W).astype(F32)

    def pattern(rot_dim):
        nf = rot_dim // 4
        inv = ROPE_THETA ** (-jnp.arange(nf, dtype=F32) / nf)
        ar, ac = rowp[:, None] * inv, colp[:, None] * inv
        cos = jnp.concatenate([jnp.cos(ar), jnp.cos(ar), jnp.cos(ac), jnp.cos(ac)], axis=1)
        sin = jnp.concatenate([-jnp.sin(ar), jnp.sin(ar), -jnp.sin(ac), jnp.sin(ac)], axis=1)
        return cos, sin

    cw, sw = pattern(WIN_HEAD_DIM)
    cw, sw = jnp.tile(cw, (1, 2)), jnp.tile(sw, (1, 2))
    cm, sm = pattern(MLA_ROPE)
    one64, zero64 = jnp.ones((n_lat, 64), F32), jnp.zeros((n_lat, 64), F32)
    cm = jnp.concatenate([one64, cm, one64[:, :32]], axis=1)
    sm = jnp.concatenate([zero64, sm, zero64[:, :32]], axis=1)
    sq_w, sq_m = WIN_SCALE * LOG2E, MLA_SCALE * LOG2E
    lat = jnp.concatenate([cw * sq_w, sw * sq_w, cw, sw, cm * sq_m, sm * sq_m, cm, sm], axis=1)
    ones, zeros = jnp.ones((n_ctx, LANES), F32), jnp.zeros((n_ctx, LANES), F32)
    ctx = jnp.concatenate([ones * sq_w, zeros, ones, zeros, ones * sq_m, zeros, ones, zeros], axis=1)
    return jnp.concatenate([ctx, lat], axis=0)


def kernel(x, c, ctx, c_ctx, w_ada, b_ada, w_in, attn_sink, mla_q_norm, mla_kv_norm, w_uq, w_ukv,
           w_oa, w_ob, w_out, ln1_g, ln1_b, w_router, w_exp_gate, w_exp_up, w_exp_down, ln2_g, ln2_b):
    B, n_lat, D = x.shape
    n_ctx = ctx.shape[1]
    depth = w_in.shape[0]
    S = n_ctx + n_lat
    alpha = (2 * depth) ** 0.25
    assert D == D_MODEL and n_ctx == TOK_BLOCK and n_lat % TOK_BLOCK == 0 and B + 1 <= ADA_ROWS
    cap_c = CAPACITY_FACTOR * n_ctx // N_EXPERTS
    cap_l = CAPACITY_FACTOR * n_lat // N_EXPERTS
    assert cap_l % 256 == 0 and cap_c % 8 == 0
    ncb = n_ctx // TOK_BLOCK

    cond = jnp.concatenate([c, c_ctx[None], jnp.zeros((ADA_ROWS - B - 1, D), F32)], axis=0)
    mods = _ada_call(cond.T, w_ada, b_ada, B + 1).reshape(depth * ADA_ROWS * 6, 1, D)

    w_in_p, w_kv, w_q, w_r = _prep_weights(w_in, w_uq, w_ukv, w_router)
    w_oa_b, w_ob_b, w_out_b = w_oa.astype(BF16), w_ob.astype(BF16), w_out.astype(BF16)
    tab = _rope_tables(n_ctx, n_lat)
    vones = _ones_columns()

    x_all = (ctx, x)
    moe = None
    for l in range(depth):
        x_res, (qwin, kwin, vwin, qcat, kcat, vext, sg) = _proj_call(
            l, alpha, x_all, moe, ln2_g, ln2_b, mods, tab, w_in_p, mla_kv_norm, mla_q_norm,
            w_kv, w_q, vones, ncb)
        oa = _win_call(l, attn_sink, qwin, kwin, vwin, n_ctx)
        ob = _mla_call(qcat, kcat, vext, n_ctx)
        x1, h2, lg_t = _merge_call(l, alpha, x_res, oa, ob, sg, w_oa_b, w_ob_b, w_out_b,
                                   ln1_g, ln1_b, mods, w_r, ncb)
        aff, idx4 = _route_call(lg_t, n_ctx, cap_c, cap_l)
        idx = idx4.reshape(B, N_EXPERTS, cap_c + cap_l)
        xg = _gather_call(idx, h2)
        moe = _moe_call(l, idx, aff, xg, w_exp_gate, w_exp_up, w_exp_down, S)
        x_all = x1
    return _final_call(depth - 1, alpha, x_all, moe, ln2_g, ln2_b, mods, n_ctx)
```

```python
import functools
import math

import jax
import jax.numpy as jnp
from jax import lax
from jax.experimental import pallas as pl
from jax.experimental.pallas import tpu as pltpu

D_MODEL = 1024
GRID_W = 64
WIN_HEADS = 8
WIN_KV_HEADS = 2
WIN_HEAD_DIM = 64
BAND = 128
MLA_HEADS = 8
MLA_Q_RANK = 384
MLA_KV_RANK = 256
MLA_NOPE = 64
MLA_ROPE = 32
MLA_V = 64
N_EXPERTS = 16
CAPACITY_FACTOR = 2
ROPE_THETA = 10000.0
LN_EPS = 1e-5
RMS_EPS = 1e-6
NEG_INF = -1e30
LOG2E = math.log2(math.e)
WIN_SCALE = WIN_HEAD_DIM ** -0.5
MLA_SCALE = (MLA_NOPE + MLA_ROPE) ** -0.5

LANES = 128
TOK_BLOCK = 256
VMEM_LIMIT = 56 * 1024 * 1024

BF16 = jnp.bfloat16
F32 = jnp.float32

SEG_KWIN = (0, 128)
SEG_VWIN = (128, 256)
SEG_CKV = (256, 512)
SEG_KR = (512, 640)
SEG_QWIN = (640, 1152)
SEG_CQ = (1152, 1536)
SEG_GATE = (1536, 3584)
IN_COLS_PAD = 3584


def _dot(a, b):
    return jnp.dot(a, b, preferred_element_type=F32)


def _dot_nt(a, b):
    return lax.dot_general(a, b, (((1,), (1,)), ((), ())), preferred_element_type=F32)


def _layer_norm(z, g, b):
    mu = jnp.mean(z, axis=-1, keepdims=True)
    zc = z - mu
    var = jnp.mean(zc * zc, axis=-1, keepdims=True)
    return zc * lax.rsqrt(var + LN_EPS) * g + b


def _rms_norm(x, g):
    return x * lax.rsqrt(jnp.mean(x * x, axis=-1, keepdims=True) + RMS_EPS) * g


def _rope_slab(x, cos, sin, half):
    lane = lax.broadcasted_iota(jnp.int32, x.shape, 1)
    partner = jnp.where((lane & half) == 0,
                        pltpu.roll(x, LANES - half, 1), pltpu.roll(x, half, 1))
    return x * cos + partner * sin


SUBLANES = 8


def _rows_to_tiles(y):
    n = y.shape[0]
    y3 = pltpu.einshape("r(sl)->srl", y, s=SUBLANES)
    y4 = y3.reshape(SUBLANES, n // SUBLANES, SUBLANES, LANES)
    return jnp.transpose(y4, (1, 2, 0, 3)).reshape(n, SUBLANES, LANES)


def _tiles_to_rows(x3):
    n = x3.shape[0]
    x4 = x3.reshape(n // SUBLANES, SUBLANES, SUBLANES, LANES)
    xs = jnp.transpose(x4, (2, 0, 1, 3)).reshape(SUBLANES, n, LANES)
    return pltpu.einshape("srl->r(sl)", xs)


def _compiler_params(sem):
    return pltpu.CompilerParams(dimension_semantics=sem, vmem_limit_bytes=VMEM_LIMIT)


ADA_ROWS = 8
ADA_TN = 512


def _ada_kernel(n_rows, condt_ref, w_ref, b_ref, o_ref):
    ct = condt_ref[...]
    st = ct * jax.nn.sigmoid(ct)
    w = w_ref[0]
    rows = []
    for r in range(n_rows):
        rows.append(jnp.sum(w * st[:, r:r + 1], axis=0, keepdims=True) + b_ref[0])
    rows.append(jnp.zeros((ADA_ROWS - n_rows, w.shape[1]), F32))
    o_ref[0] = jnp.concatenate(rows, axis=0)


def _ada_call(cond_t, w_ada, b_ada, n_rows):
    L, D, N = w_ada.shape
    return pl.pallas_call(
        functools.partial(_ada_kernel, n_rows),
        grid=(L, N // ADA_TN),
        in_specs=[
            pl.BlockSpec((D, ADA_ROWS), lambda l, j: (0, 0)),
            pl.BlockSpec((1, D, ADA_TN), lambda l, j: (l, 0, j)),
            pl.BlockSpec((1, 1, ADA_TN), lambda l, j: (l, 0, j)),
        ],
        out_specs=pl.BlockSpec((1, ADA_ROWS, ADA_TN), lambda l, j: (l, 0, j)),
        out_shape=jax.ShapeDtypeStruct((L, ADA_ROWS, N), F32),
        compiler_params=_compiler_params(("parallel", "parallel")),
        name="adaln_mod",
    )(cond_t, w_ada, b_ada.reshape(L, 1, N))


def _proj_kernel(has_ln, alpha, n_ctx_blocks, *refs):
    if has_ln:
        (x_ref, moe_ref, g2_ref, b2_ref, m5_ref, sh_ref, sc_ref, tab_ref, win_ref, gkv_ref, gq_ref,
         wkv_ref, wq_ref, vones_ref,
         xres_ref, qwin_ref, kwin_ref, vwin_ref, qcat_ref, kcat_ref, vext_ref, sg_ref) = refs
        z = alpha * x_ref[0] + m5_ref[0] * _tiles_to_rows(moe_ref[0])
        x = _layer_norm(z, g2_ref[0], b2_ref[0])
    else:
        (ctx_ref, x_ref, sh_ref, sc_ref, tab_ref, win_ref, gkv_ref, gq_ref, wkv_ref, wq_ref, vones_ref,
         xres_ref, qwin_ref, kwin_ref, vwin_ref, qcat_ref, kcat_ref, vext_ref, sg_ref) = refs
        x = jnp.where(pl.program_id(0) < n_ctx_blocks, ctx_ref[0], x_ref[0])
    xres_ref[0] = x
    hb = (x * (1.0 + sc_ref[0]) + sh_ref[0]).astype(BF16)

    def seg(s):
        return _dot(hb, win_ref[0, :, s[0]:s[1]])

    def tab(i):
        return tab_ref[:, i * LANES:(i + 1) * LANES]

    cq_w, sq_w, ck_w, sk_w, cq_m, sq_m, ck_m, sk_m = (tab(i) for i in range(8))

    lane_lo = lax.broadcasted_iota(jnp.int32, (hb.shape[0], LANES), 1) < WIN_HEAD_DIM
    half_swap = lambda t: pltpu.roll(t, WIN_HEAD_DIM, 1)

    ckv = _rms_norm(seg(SEG_CKV), gkv_ref[0]).astype(BF16)
    cq = _rms_norm(seg(SEG_CQ), gq_ref[0]).astype(BF16)

    qw = seg(SEG_QWIN)
    q_slabs = []
    for s2 in range(WIN_HEADS // 2):
        pair = _rope_slab(qw[:, s2 * LANES:(s2 + 1) * LANES], cq_w, sq_w, 16)
        pair_sw = half_swap(pair)
        if 2 * s2 < WIN_HEADS // WIN_KV_HEADS:
            q_slabs += [jnp.where(lane_lo, pair, 0.0), jnp.where(lane_lo, pair_sw, 0.0)]
        else:
            q_slabs += [jnp.where(lane_lo, 0.0, pair_sw), jnp.where(lane_lo, 0.0, pair)]
    qwin_ref[0] = jnp.concatenate(q_slabs, axis=1).astype(BF16)

    kv = _dot(ckv, wkv_ref[0])
    kr = _rope_slab(seg(SEG_KR), ck_m, sk_m, 8)
    qm = _dot(cq, wq_ref[0])
    for h in range(MLA_HEADS):
        sl = slice(h * LANES, (h + 1) * LANES)
        kcat_ref[0, :, sl] = (kv[:, sl] + kr).astype(BF16)
        vext_ref[0, :, sl] = (kv[:, 1024 + h * LANES:1024 + (h + 1) * LANES]
                              + vones_ref[:, 512 + h * LANES:512 + (h + 1) * LANES]).astype(BF16)
        qcat_ref[0, :, sl] = _rope_slab(qm[:, sl], cq_m, sq_m, 8).astype(BF16)

    sg_ref[0] = jax.nn.sigmoid(seg(SEG_GATE)).astype(BF16)
    va = seg(SEG_VWIN)
    va_sw = half_swap(va)
    vwin_ref[0] = (jnp.concatenate(
        [jnp.where(lane_lo, va, 0.0), jnp.where(lane_lo, 0.0, va_sw),
         jnp.where(lane_lo, va_sw, 0.0), jnp.where(lane_lo, 0.0, va)], axis=1)
        + vones_ref[:, 0:512]).astype(BF16)
    kwin_ref[0] = _rope_slab(seg(SEG_KWIN), ck_w, sk_w, 16).astype(BF16)


def _proj_call(layer, alpha, x_all, moe, ln2_g, ln2_b, mods, tab, w_in_p, g_kvn, g_qn, w_kv, w_q,
               vones, n_ctx_blocks):
    T = TOK_BLOCK
    has_ln = moe is not None
    if has_ln:
        B, S, D = x_all.shape
    else:
        ctx_in, x_in = x_all
        B, D = x_in.shape[0], x_in.shape[2]
        S = ctx_in.shape[1] + x_in.shape[1]
    L = w_in_p.shape[0]

    def mod_spec(k, lyr):
        def imap(i, b):
            row = jnp.where(i < n_ctx_blocks, B, b)
            return ((lyr * ADA_ROWS + row) * 6 + k, 0, 0)
        return pl.BlockSpec((1, 1, D), imap)

    tok = lambda w: pl.BlockSpec((1, T, w), lambda i, b: (b, i, 0))
    headed = tok(MLA_HEADS * LANES)
    const = lambda shp: pl.BlockSpec(shp, lambda i, b: (layer,) + (0,) * (len(shp) - 1),
                                     pipeline_mode=pl.Buffered(1))

    if has_ln:
        prev = pl.BlockSpec((1, 1, D), lambda i, b: (layer - 1, 0, 0))
        tiled = pl.BlockSpec((1, T, SUBLANES, LANES), lambda i, b: (b, i, 0, 0))
        in_specs = [tok(D), tiled, prev, prev, mod_spec(5, layer - 1)]
        args = [x_all, moe, ln2_g.reshape(L, 1, D), ln2_b.reshape(L, 1, D), mods]
    else:
        assert ctx_in.shape[1] == n_ctx_blocks * T
        in_specs = [pl.BlockSpec((1, T, D), lambda i, b: (b, jnp.minimum(i, n_ctx_blocks - 1), 0)),
                    pl.BlockSpec((1, T, D), lambda i, b: (b, jnp.maximum(i - n_ctx_blocks, 0), 0))]
        args = [ctx_in, x_in]
    in_specs += [
        mod_spec(0, layer), mod_spec(1, layer),
        pl.BlockSpec((T, 8 * LANES), lambda i, b: (i, 0)),
        const((1, D, IN_COLS_PAD)),
        const((1, 1, MLA_KV_RANK)), const((1, 1, MLA_Q_RANK)),
        const((1, MLA_KV_RANK, 2048)), const((1, MLA_Q_RANK, 1024)),
        pl.BlockSpec((1, 1536), lambda i, b: (0, 0)),
    ]
    args += [mods, mods, tab, w_in_p, g_kvn.reshape(L, 1, -1), g_qn.reshape(L, 1, -1), w_kv, w_q, vones]

    out_specs, out_shape = [tok(D)], [jax.ShapeDtypeStruct((B, S, D), F32)]
    out_specs += [tok(WIN_HEADS * LANES), tok(LANES), tok(512), headed, headed, headed, tok(2048)]
    out_shape += [jax.ShapeDtypeStruct((B, S, WIN_HEADS * LANES), BF16),
                  jax.ShapeDtypeStruct((B, S, LANES), BF16),
                  jax.ShapeDtypeStruct((B, S, 512), BF16)]
    out_shape += [jax.ShapeDtypeStruct((B, S, MLA_HEADS * LANES), BF16)] * 3
    out_shape += [jax.ShapeDtypeStruct((B, S, 2048), BF16)]

    outs = pl.pallas_call(
        functools.partial(_proj_kernel, has_ln, alpha, n_ctx_blocks),
        grid=(S // T, B),
        in_specs=in_specs, out_specs=out_specs, out_shape=out_shape,
        compiler_params=_compiler_params(("parallel", "parallel")),
        name="proj",
    )(*args)
    return outs[0], outs[1:]


WIN_HEAD_ORDER = (0, 2, 1, 3, 4, 6, 5, 7)


def _win_kernel(layer, n_ctx, n_steps, sink_ref, q_ref, kc_ref, vc_ref, kp_ref, km_ref, kn_ref,
                vp_ref, vm_ref, vn_ref, o_ref, kall_ref, vall_ref, p_ref):
    i = pl.program_id(1)
    row = lax.broadcasted_iota(jnp.int32, (BAND, LANES), 0)
    col = lax.broadcasted_iota(jnp.int32, (BAND, LANES), 1)
    lane_lo = col < 64
    kall_ref[0:n_ctx, :] = kc_ref[0]
    vall_ref[0:n_ctx, :] = vc_ref[0]

    def scores(qb, r0, nk):
        q = q_ref[0, qb * BAND:(qb + 1) * BAND, :]
        q8 = jnp.concatenate([q[:, h * LANES:(h + 1) * LANES] for h in WIN_HEAD_ORDER], axis=0)
        return _dot_nt(q8, kall_ref[r0:r0 + nk, :])

    def softmax(qb, nk, masks, s):
        e_sink = []
        for c, h in enumerate(WIN_HEAD_ORDER):
            sc = s[c * BAND:(c + 1) * BAND, :]
            if masks:
                blocks = []
                for kb in range(nk // BAND):
                    blk = sc[:, kb * BAND:(kb + 1) * BAND]
                    blocks.append(jnp.where(masks[kb], blk, NEG_INF) if kb in masks else blk)
                sc = jnp.concatenate(blocks, axis=1)
            sink = sink_ref[layer, h] * LOG2E
            m = jnp.maximum(jnp.max(sc, axis=1, keepdims=True), sink)
            p_ref[qb, c * BAND:(c + 1) * BAND, 0:nk] = jnp.exp2(sc - m).astype(BF16)
            e_sink.append(jnp.exp2(sink - m))
        return e_sink

    def values(qb, r0, nk, e_sink):
        pairs = []
        for pi in range(4):
            acc = _dot(p_ref[qb, pi * 2 * BAND:(pi + 1) * 2 * BAND, 0:nk],
                       vall_ref[r0:r0 + nk, pi * LANES:(pi + 1) * LANES])
            es = jnp.concatenate([e_sink[2 * pi], e_sink[2 * pi + 1]], axis=0)
            den = (acc[:, 64:65] if pi % 2 == 0 else acc[:, 0:1]) + es
            pairs.append(acc / den)
        slabs = [jnp.where(lane_lo, pairs[0][:BAND], pairs[1][:BAND]),
                 jnp.where(lane_lo, pairs[0][BAND:], pairs[1][BAND:]),
                 jnp.where(lane_lo, pairs[2][:BAND], pairs[3][:BAND]),
                 jnp.where(lane_lo, pairs[2][BAND:], pairs[3][BAND:])]
        o_ref[0, qb * BAND:(qb + 1) * BAND, :] = jnp.concatenate(slabs, axis=1).astype(o_ref.dtype)

    def attend_both(args0, args1):
        (r0, nk0, m0), (r1, nk1, m1) = args0, args1
        s0 = scores(0, r0, nk0)
        s1 = scores(1, r1, nk1)
        e0 = softmax(0, nk0, m0, s0)
        e1 = softmax(1, nk1, m1, s1)
        values(0, r0, nk0, e0)
        values(1, r1, nk1, e1)

    @pl.when(i == 0)
    def _():
        attend_both((0, n_ctx, None), (0, n_ctx, None))

    @pl.when(i > 0)
    def _():
        r = n_ctx
        for k_ref, v_ref, n in ((kp_ref, vp_ref, BAND), (km_ref, vm_ref, 2 * BAND),
                                (kn_ref, vn_ref, BAND), (kc_ref, vc_ref, n_ctx)):
            kall_ref[r:r + n, :] = k_ref[0]
            vall_ref[r:r + n, :] = v_ref[0]
            r += n
        off_first = jnp.where(i == 1, 2 * LANES, 0)
        off_last = jnp.where(i == n_steps - 1, 2 * LANES, 0)
        cb = n_ctx // BAND
        attend_both((0, n_ctx + 3 * BAND, {cb: col >= row + off_first, cb + 2: col <= row}),
                    (n_ctx + BAND, n_ctx + 3 * BAND, {0: col >= row, 2: col <= row - off_last}))


def _win_call(layer, sink, qwin, kwin, vwin, n_ctx):
    B, S, _ = qwin.shape
    nb = S // BAND
    ncb = n_ctx // BAND
    assert n_ctx == 2 * BAND and nb % 2 == 0
    nk = n_ctx + 3 * BAND
    one = lambda w, f: pl.BlockSpec((1, BAND, w), lambda b, i: (b, jnp.clip(f(i), ncb, nb - 1), 0))
    two = lambda w: pl.BlockSpec((1, 2 * BAND, w), lambda b, i: (b, i, 0))
    ctx = lambda w: pl.BlockSpec((1, n_ctx, w), lambda b, i: (b, 0, 0))
    prev, nxt = (lambda i: 2 * i - 1), (lambda i: 2 * i + 2)
    return pl.pallas_call(
        functools.partial(_win_kernel, layer, n_ctx, nb // 2),
        grid=(B, nb // 2),
        in_specs=[pl.BlockSpec(memory_space=pltpu.SMEM), two(WIN_HEADS * LANES),
                  ctx(LANES), ctx(512), one(LANES, prev), two(LANES), one(LANES, nxt),
                  one(512, prev), two(512), one(512, nxt)],
        out_specs=two(512),
        out_shape=jax.ShapeDtypeStruct((B, S, 512), BF16),
        scratch_shapes=[pltpu.VMEM((2 * n_ctx + 4 * BAND, LANES), BF16),
                        pltpu.VMEM((2 * n_ctx + 4 * BAND, 512), BF16),
                        pltpu.VMEM((2, WIN_HEADS * BAND, nk), BF16)],
        compiler_params=_compiler_params(("parallel", "parallel")),
        name="win_attn",
    )(sink, qwin, kwin, vwin, kwin, kwin, kwin, vwin, vwin, vwin)


MLA_KCHUNK = 256
MLA_HEADS_PER_STEP = 4


def _mla_kernel(n_ctx, n_keys, q_ref, k_ref, v_ref, o_ref, s_ref):
    qi = pl.program_id(2)
    tq = q_ref.shape[1]
    lane = lax.broadcasted_iota(jnp.int32, (tq, LANES), 1)

    def run(nk):
        row_max = []
        for hh in range(MLA_HEADS_PER_STEP):
            hl = slice(hh * LANES, (hh + 1) * LANES)
            q = q_ref[0, :, hl]
            mrun = None
            for c in range(nk // MLA_KCHUNK):
                s = _dot_nt(q, k_ref[0, c * MLA_KCHUNK:(c + 1) * MLA_KCHUNK, hl])
                s_ref[hh, c] = s
                mc = jnp.maximum(s[:, :LANES], s[:, LANES:])
                mrun = mc if mrun is None else jnp.maximum(mrun, mc)
            row_max.append(jnp.max(mrun, axis=1, keepdims=True))
        outs = []
        for hh in range(MLA_HEADS_PER_STEP):
            acc = jnp.zeros((tq, LANES), F32)
            for c in range(nk // MLA_KCHUNK):
                p = jnp.exp2(s_ref[hh, c] - row_max[hh]).astype(BF16)
                acc = acc + _dot(p, v_ref[0, c * MLA_KCHUNK:(c + 1) * MLA_KCHUNK,
                                          hh * LANES:(hh + 1) * LANES])
            den = acc[:, 64:65] if hh % 2 == 0 else acc[:, 0:1]
            outs.append(acc / den)
        o_ref[0] = jnp.concatenate(
            [jnp.where(lane < 64, outs[2 * j], outs[2 * j + 1]) for j in range(len(outs) // 2)],
            axis=1).astype(o_ref.dtype)

    @pl.when(qi == 0)
    def _():
        run(n_ctx)

    @pl.when(qi > 0)
    def _():
        run(n_keys)


def _mla_call(qcat, kcat, vext, n_ctx):
    B, S, _ = qcat.shape
    H = MLA_HEADS
    T = TOK_BLOCK
    assert n_ctx == T
    G = MLA_HEADS_PER_STEP
    kv = pl.BlockSpec((1, S, G * LANES), lambda b, hp, i: (b, 0, hp))
    return pl.pallas_call(
        functools.partial(_mla_kernel, n_ctx, S),
        grid=(B, H // G, S // T),
        in_specs=[pl.BlockSpec((1, T, G * LANES), lambda b, hp, i: (b, i, hp)), kv, kv],
        out_specs=pl.BlockSpec((1, T, (G // 2) * LANES), lambda b, hp, i: (b, i, hp)),
        out_shape=jax.ShapeDtypeStruct((B, S, (H // 2) * LANES), BF16),
        scratch_shapes=[pltpu.VMEM((G, S // MLA_KCHUNK, T, MLA_KCHUNK), F32)],
        compiler_params=_compiler_params(("parallel", "parallel", "arbitrary")),
        name="mla_attn",
    )(qcat, kcat, vext)


def _merge_kernel(alpha, x_ref, oa_ref, ob_ref, sg_ref, woa_ref, wob_ref, wout_ref, g1_ref, b1_ref,
                  m2_ref, m3_ref, m4_ref, wr_ref, x1_ref, h2_ref, lg_ref):
    D = x_ref.shape[2]
    sg = sg_ref[0]
    t = (sg[:, :D].astype(F32) * _dot(oa_ref[0], woa_ref[0])
         + sg[:, D:].astype(F32) * _dot(ob_ref[0], wob_ref[0]))
    y = _dot(t.astype(BF16), wout_ref[0])
    x1 = _layer_norm(alpha * x_ref[0] + m2_ref[0] * y, g1_ref[0], b1_ref[0])
    x1_ref[0] = x1
    h2 = x1 * (1.0 + m4_ref[0]) + m3_ref[0]
    h2_ref[0] = _rows_to_tiles(h2)
    lg = _dot(h2.astype(BF16), wr_ref[0])
    lg_ref[0] = lg.T[:N_EXPERTS]


def _merge_call(layer, alpha, x_res, oa, ob, sg, w_oa, w_ob, w_out, ln1_g, ln1_b, mods, w_r,
                n_ctx_blocks):
    B, S, D = x_res.shape
    T = TOK_BLOCK
    L = w_oa.shape[0]

    def mod_spec(k):
        def imap(b, i):
            row = jnp.where(i < n_ctx_blocks, B, b)
            return ((layer * ADA_ROWS + row) * 6 + k, 0, 0)
        return pl.BlockSpec((1, 1, D), imap)

    tok = lambda w: pl.BlockSpec((1, T, w), lambda b, i: (b, i, 0))
    const = lambda shp: pl.BlockSpec(shp, lambda b, i: (layer,) + (0,) * (len(shp) - 1),
                                     pipeline_mode=pl.Buffered(1))
    return pl.pallas_call(
        functools.partial(_merge_kernel, alpha),
        grid=(B, S // T),
        in_specs=[tok(D), tok(512), tok(512), tok(2048),
                  const((1, 512, D)), const((1, 512, D)), const((1, D, D)),
                  const((1, 1, D)), const((1, 1, D)),
                  mod_spec(2), mod_spec(3), mod_spec(4),
                  const((1, D, LANES))],
        out_specs=[tok(D), pl.BlockSpec((1, T, SUBLANES, LANES), lambda b, i: (b, i, 0, 0)),
                   pl.BlockSpec((1, N_EXPERTS, T), lambda b, i: (b, 0, i))],
        out_shape=[jax.ShapeDtypeStruct((B, S, D), F32),
                   jax.ShapeDtypeStruct((B, S, SUBLANES, LANES), F32),
                   jax.ShapeDtypeStruct((B, N_EXPERTS, S), F32)],
        compiler_params=_compiler_params(("parallel", "parallel")),
        name="merge",
    )(x_res, oa, ob, sg, w_oa, w_ob, w_out, ln1_g.reshape(L, 1, D), ln1_b.reshape(L, 1, D),
      mods, mods, mods, w_r)


def _cumsum_lanes(x):
    n = x.shape[1]
    xb = x.astype(BF16)
    r = lax.broadcasted_iota(jnp.int32, (LANES, LANES), 0)
    c = lax.broadcasted_iota(jnp.int32, (LANES, LANES), 1)
    tri = jnp.where(r <= c, 1.0, 0.0).astype(BF16)
    tb = lax.broadcasted_iota(jnp.int32, (n, LANES), 0) // LANES
    kb = lax.broadcasted_iota(jnp.int32, (n, LANES), 1)
    before = jnp.where(tb < kb, 1.0, 0.0).astype(BF16)
    off = _dot(xb, before)
    outs = []
    for k in range(n // LANES):
        outs.append(_dot(xb[:, k * LANES:(k + 1) * LANES], tri) + off[:, k:k + 1])
    return outs


def _select_top(aff, cap):
    bits = pltpu.bitcast(aff, jnp.int32)
    thr = jnp.zeros((aff.shape[0], 1), jnp.int32)
    for bit in range(30, -1, -1):
        cand = thr | (1 << bit)
        cnt = jnp.sum(jnp.where(bits >= cand, 1.0, 0.0), axis=1, keepdims=True)
        thr = jnp.where(cnt >= cap, cand, thr)
    gt = bits > thr
    eq = jnp.where(bits == thr, 1.0, 0.0)
    need = cap - jnp.sum(jnp.where(gt, 1.0, 0.0), axis=1, keepdims=True)
    eq_rank = jnp.concatenate(_cumsum_lanes(eq), axis=1) - eq
    sel = jnp.where(jnp.logical_or(gt, jnp.logical_and(eq > 0.5, eq_rank < need)), 1.0, 0.0)
    return _cumsum_lanes(sel)


ROUTE_PAD = 1e6


def _route_kernel(n_ctx, cap_c, cap_l, lg_ref, aff_ref, idx_ref, cc_ref, cl_ref):
    lg = lg_ref[0]
    m = jnp.max(lg, axis=0, keepdims=True)
    ex = jnp.exp(lg - m)
    aff = ex / jnp.sum(ex, axis=0, keepdims=True)
    aff_ref[0] = aff
    for c_ref, blocks in ((cc_ref, _select_top(aff[:, :n_ctx], cap_c)),
                          (cl_ref, _select_top(aff[:, n_ctx:], cap_l))):
        c_ref[...] = jnp.full(c_ref.shape, ROUTE_PAD, F32)
        for k, blk in enumerate(blocks):
            for ex_i in range(N_EXPERTS):
                c_ref[ex_i, k:k + 1, :] = blk[ex_i:ex_i + 1, :]

    def slots(c_ref, e, n_blocks, width):
        cm = c_ref[e]
        jrow = lax.broadcasted_iota(jnp.int32, (1, width), 1).astype(F32)
        rows = max(n_blocks, SUBLANES)
        cend = cm[0:rows, LANES - 1:LANES]
        blk = jnp.sum(jnp.where(cend <= jrow, 1.0, 0.0), axis=0, keepdims=True)
        kio = lax.broadcasted_iota(jnp.int32, (LANES, width), 0).astype(F32)
        onehot = jnp.where(kio == blk, 1.0, 0.0).astype(BF16)
        cmt = cm.T
        hi = jnp.floor(cmt * (1.0 / 256.0))
        lo = cmt - 256.0 * hi
        cg = 256.0 * _dot(hi.astype(BF16), onehot) + _dot(lo.astype(BF16), onehot)
        inside = jnp.sum(jnp.where(cg <= jrow, 1.0, 0.0), axis=0, keepdims=True)
        return (blk * LANES + inside).astype(jnp.int32)

    def per_expert(e, carry):
        n_lat_blocks = (lg_ref.shape[2] - n_ctx) // LANES
        idx_ref[0, e, :, 0:cap_l] = slots(cl_ref, e, n_lat_blocks, cap_l) + n_ctx
        idx_ref[0, e, :, cap_l:cap_l + cap_c] = slots(cc_ref, e, n_ctx // LANES, LANES)[:, 0:cap_c]
        return carry

    lax.fori_loop(0, N_EXPERTS, per_expert, 0)


def _route_call(lg_t, n_ctx, cap_c, cap_l):
    B, E, S = lg_t.shape
    cap = cap_c + cap_l
    assert (S - n_ctx) // LANES <= LANES and cap_l % LANES == 0 and cap_c <= LANES
    return pl.pallas_call(
        functools.partial(_route_kernel, n_ctx, cap_c, cap_l),
        grid=(B,),
        in_specs=[pl.BlockSpec((1, E, S), lambda b: (b, 0, 0))],
        out_specs=[pl.BlockSpec((1, E, S), lambda b: (b, 0, 0)),
                   pl.BlockSpec((1, E, 1, cap), lambda b: (b, 0, 0, 0))],
        out_shape=[jax.ShapeDtypeStruct((B, E, S), F32),
                   jax.ShapeDtypeStruct((B, E, 1, cap), jnp.int32)],
        scratch_shapes=[pltpu.VMEM((E, LANES, LANES), F32), pltpu.VMEM((E, LANES, LANES), F32)],
        compiler_params=_compiler_params(("parallel",)),
        name="route",
    )(lg_t)


ROW_BATCH = 16
MOE_FF_CHUNK = 512


GATHER_EXPERTS = 2


def _gather_kernel(idx_ref, h_ref, xg_ref, tmp_ref):
    cap = xg_ref.shape[2]
    for ge in range(GATHER_EXPERTS):
        def body(g, carry):
            for r in range(ROW_BATCH):
                j = g * ROW_BATCH + r
                tmp_ref[j] = h_ref[0, idx_ref[0, 0, ge * cap + j]]
            return carry

        lax.fori_loop(0, cap // ROW_BATCH, body, 0)
        xg_ref[0, ge] = _tiles_to_rows(tmp_ref[...]).astype(xg_ref.dtype)


def _gather_call(idx, h2t):
    B, S = h2t.shape[0], h2t.shape[1]
    D = SUBLANES * LANES
    E, cap = idx.shape[1], idx.shape[2]
    G = GATHER_EXPERTS
    assert cap % ROW_BATCH == 0 and E % G == 0
    return pl.pallas_call(
        _gather_kernel,
        grid=(B, E // G),
        in_specs=[pl.BlockSpec((1, 1, G * cap), lambda b, e: (b * (E // G) + e, 0, 0),
                               memory_space=pltpu.SMEM),
                  pl.BlockSpec((1, S, SUBLANES, LANES), lambda b, e: (b, 0, 0, 0))],
        out_specs=pl.BlockSpec((1, G, cap, D), lambda b, e: (b, e, 0, 0)),
        out_shape=jax.ShapeDtypeStruct((B, E, cap, D), BF16),
        scratch_shapes=[pltpu.VMEM((cap, SUBLANES, LANES), F32)],
        compiler_params=_compiler_params(("parallel", "arbitrary")),
        name="moe_gather",
    )(idx.reshape(B * E // G, 1, G * cap), h2t)


def _moe_kernel(idx_ref, aff_ref, xg_ref, wg_ref, wu_ref, wd_ref, out_ref, y_ref):
    s = pl.program_id(1)
    n_exp = pl.num_programs(1) - 1
    cap = y_ref.shape[1]
    n_batches = cap // ROW_BATCH

    def scatter(slot, batches):
        new = None
        for g in batches:
            rows = _rows_to_tiles(y_ref[slot, g * ROW_BATCH:(g + 1) * ROW_BATCH, :])
            toks = [idx_ref[0, 0, g * ROW_BATCH + r] for r in range(ROW_BATCH)]
            old = [out_ref[0, t] for t in toks]
            for r, t in enumerate(toks):
                new = old[r] + rows[r] * aff_ref[0, 0, t]
                out_ref[0, t] = new
        return new

    def ffn(slot, prev_slot):
        ff = wg_ref.shape[3]
        n_dots = 3 * (ff // MOE_FF_CHUNK)
        per = -(-n_batches // n_dots)
        todo = list(range(n_batches)) if prev_slot is not None else []

        def weights(w):
            take, todo[:] = todo[:per], todo[per:]
            w = w.astype(BF16)
            if not take:
                return w
            last = scatter(prev_slot, take)
            bits = pltpu.bitcast(last, jnp.uint32)
            zero = pltpu.bitcast((bits >> 16) >> 16, F32)
            zero = jnp.concatenate([zero, zero], axis=0).astype(BF16)
            return w + jnp.tile(zero, (w.shape[0] // zero.shape[0], w.shape[1] // LANES))

        x = xg_ref[0, 0]
        y = None
        for f0 in range(0, ff, MOE_FF_CHUNK):
            f1 = f0 + MOE_FF_CHUNK
            a = _dot(x, weights(wg_ref[0, 0, :, f0:f1]))
            u = _dot(x, weights(wu_ref[0, 0, :, f0:f1]))
            hmid = (a * jax.nn.sigmoid(a) * u).astype(BF16)
            part = _dot(hmid, weights(wd_ref[0, 0, f0:f1, :]))
            y = part if y is None else y + part
        y_ref[slot] = y

    @pl.when(s == 0)
    def _():
        out_ref[...] = jnp.zeros(out_ref.shape, out_ref.dtype)
        ffn(0, None)

    @pl.when(jnp.logical_and(s > 0, s < n_exp))
    def _():
        slot = s % 2
        ffn(slot, 1 - slot)

    @pl.when(s == n_exp)
    def _():
        scatter((n_exp - 1) % 2, range(n_batches))


def _moe_call(layer, idx, aff, xg, w_gate, w_up, w_down, S):
    B, E, cap, D = xg.shape
    F = w_gate.shape[3]
    cur = lambda s: jnp.minimum(s, E - 1)
    prev = lambda s: jnp.maximum(s - 1, 0)
    wspec = lambda shp: pl.BlockSpec(shp, lambda b, s: (layer, cur(s), 0, 0))
    return pl.pallas_call(
        _moe_kernel,
        grid=(B, E + 1),
        in_specs=[pl.BlockSpec((1, 1, cap), lambda b, s: (b * E + prev(s), 0, 0), memory_space=pltpu.SMEM),
                  pl.BlockSpec((1, 1, S), lambda b, s: (b * E + prev(s), 0, 0), memory_space=pltpu.SMEM),
                  pl.BlockSpec((1, 1, cap, D), lambda b, s: (b, cur(s), 0, 0)),
                  wspec((1, 1, D, F)), wspec((1, 1, D, F)), wspec((1, 1, F, D))],
        out_specs=pl.BlockSpec((1, S, SUBLANES, LANES), lambda b, s: (b, 0, 0, 0),
                               pipeline_mode=pl.Buffered(1)),
        out_shape=jax.ShapeDtypeStruct((B, S, SUBLANES, LANES), F32),
        scratch_shapes=[pltpu.VMEM((2, cap, D), F32)],
        compiler_params=_compiler_params(("parallel", "arbitrary")),
        name="moe_ffn",
    )(idx.reshape(B * E, 1, cap), aff.reshape(B * E, 1, S), xg, w_gate, w_up, w_down)


def _final_kernel(alpha, x_ref, moe_ref, g_ref, b_ref, m5_ref, o_ref):
    o_ref[0] = _layer_norm(alpha * x_ref[0] + m5_ref[0] * _tiles_to_rows(moe_ref[0]),
                           g_ref[0], b_ref[0])


def _final_call(layer, alpha, x1, moe, ln2_g, ln2_b, mods, n_ctx):
    B, S, D = x1.shape
    T = TOK_BLOCK
    L = ln2_g.shape[0]
    ncb = n_ctx // T
    tok_in = pl.BlockSpec((1, T, D), lambda b, i: (b, i + ncb, 0))
    const = pl.BlockSpec((1, 1, D), lambda b, i: (layer, 0, 0))
    return pl.pallas_call(
        functools.partial(_final_kernel, alpha),
        grid=(B, (S - n_ctx) // T),
        in_specs=[tok_in, pl.BlockSpec((1, T, SUBLANES, LANES), lambda b, i: (b, i + ncb, 0, 0)),
                  const, const,
                  pl.BlockSpec((1, 1, D), lambda b, i: ((layer * ADA_ROWS + b) * 6 + 5, 0, 0))],
        out_specs=pl.BlockSpec((1, T, D), lambda b, i: (b, i, 0)),
        out_shape=jax.ShapeDtypeStruct((B, S - n_ctx, D), F32),
        compiler_params=_compiler_params(("parallel", "parallel")),
        name="final_norm",
    )(x1, moe, ln2_g.reshape(L, 1, D), ln2_b.reshape(L, 1, D), mods)


def _prep_weights(w_in, w_uq, w_ukv, w_router):
    L, D, _ = w_in.shape
    krs = jnp.concatenate([jnp.zeros((L, D, 64), w_in.dtype), w_in[..., 512:544],
                           jnp.zeros((L, D, 32), w_in.dtype)], axis=-1)
    w_in_p = jnp.concatenate([w_in[..., 0:512], krs, w_in[..., 544:]], axis=-1).astype(BF16)

    kvr = w_ukv.reshape(L, MLA_KV_RANK, MLA_HEADS, MLA_NOPE + MLA_V)
    kn, vv = kvr[..., :MLA_NOPE], kvr[..., MLA_NOPE:]
    zk = jnp.zeros_like(kn)
    w_k = jnp.concatenate([kn, zk], axis=-1).reshape(L, MLA_KV_RANK, MLA_HEADS * LANES)
    v_even = jnp.concatenate([vv, zk], axis=-1)
    v_odd = jnp.concatenate([zk, vv], axis=-1)
    odd = (jnp.arange(MLA_HEADS) % 2 == 1)[None, None, :, None]
    w_v = jnp.where(odd, v_odd, v_even).reshape(L, MLA_KV_RANK, MLA_HEADS * LANES)
    w_kv = jnp.concatenate([w_k, w_v], axis=-1).astype(BF16)

    qr = w_uq.reshape(L, MLA_Q_RANK, MLA_HEADS, MLA_NOPE + MLA_ROPE)
    w_q = jnp.concatenate([qr, jnp.zeros((L, MLA_Q_RANK, MLA_HEADS, 32), w_uq.dtype)], axis=-1)
    w_q = w_q.reshape(L, MLA_Q_RANK, MLA_HEADS * LANES).astype(BF16)

    w_r = jnp.concatenate(
        [w_router, jnp.zeros((L, D, LANES - N_EXPERTS), w_router.dtype)], axis=-1).astype(BF16)
    return w_in_p, w_kv, w_q, w_r


def _ones_columns():
    lane = jnp.arange(LANES)
    even = (lane == 64).astype(F32)
    odd = (lane == 0).astype(F32)
    win = jnp.concatenate([even, odd, even, odd])
    mla = jnp.concatenate([even, odd] * (MLA_HEADS // 2))
    return jnp.concatenate([win, mla])[None, :]


def _rope_tables(n_ctx, n_lat):
    pos = jnp.arange(n_lat)
    rowp = (pos // GRID_W).astype(F32)
    colp = (pos % GRID_W).astype(F32)

    def pattern(rot_dim):
        nf = rot_dim // 4
        inv = ROPE_THETA ** (-jnp.arange(nf, dtype=F32) / nf)
        ar, ac = rowp[:, None] * inv, colp[:, None] * inv
        cos = jnp.concatenate([jnp.cos(ar), jnp.cos(ar), jnp.cos(ac), jnp.cos(ac)], axis=1)
        sin = jnp.concatenate([-jnp.sin(ar), jnp.sin(ar), -jnp.sin(ac), jnp.sin(ac)], axis=1)
        return cos, sin

    cw, sw = pattern(WIN_HEAD_DIM)
    cw, sw = jnp.tile(cw, (1, 2)), jnp.tile(sw, (1, 2))
    cm, sm = pattern(MLA_ROPE)
    one64, zero64 = jnp.ones((n_lat, 64), F32), jnp.zeros((n_lat, 64), F32)
    cm = jnp.concatenate([one64, cm, one64[:, :32]], axis=1)
    sm = jnp.concatenate([zero64, sm, zero64[:, :32]], axis=1)
    sq_w, sq_m = WIN_SCALE * LOG2E, MLA_SCALE * LOG2E
    lat = jnp.concatenate([cw * sq_w, sw * sq_w, cw, sw, cm * sq_m, sm * sq_m, cm, sm], axis=1)
    ones, zeros = jnp.ones((n_ctx, LANES), F32), jnp.zeros((n_ctx, LANES), F32)
    ctx = jnp.concatenate([ones * sq_w, zeros, ones, zeros, ones * sq_m, zeros, ones, zeros], axis=1)
    return jnp.concatenate([ctx, lat], axis=0)


def kernel(x, c, ctx, c_ctx, w_ada, b_ada, w_in, attn_sink, mla_q_norm, mla_kv_norm, w_uq, w_ukv,
           w_oa, w_ob, w_out, ln1_g, ln1_b, w_router, w_exp_gate, w_exp_up, w_exp_down, ln2_g, ln2_b):
    B, n_lat, D = x.shape
    n_ctx = ctx.shape[1]
    depth = w_in.shape[0]
    S = n_ctx + n_lat
    alpha = (2 * depth) ** 0.25
    assert D == D_MODEL and n_ctx == TOK_BLOCK and n_lat % TOK_BLOCK == 0 and B + 1 <= ADA_ROWS
    cap_c = CAPACITY_FACTOR * n_ctx // N_EXPERTS
    cap_l = CAPACITY_FACTOR * n_lat // N_EXPERTS
    assert cap_l % 256 == 0 and cap_c % 8 == 0
    ncb = n_ctx // TOK_BLOCK

    cond = jnp.concatenate([c, c_ctx[None], jnp.zeros((ADA_ROWS - B - 1, D), F32)], axis=0)
    mods = _ada_call(cond.T, w_ada, b_ada, B + 1).reshape(depth * ADA_ROWS * 6, 1, D)

    w_in_p, w_kv, w_q, w_r = _prep_weights(w_in, w_uq, w_ukv, w_router)
    w_oa_b, w_ob_b, w_out_b = w_oa.astype(BF16), w_ob.astype(BF16), w_out.astype(BF16)
    tab = _rope_tables(n_ctx, n_lat)
    vones = _ones_columns()

    x_all = (ctx, x)
    moe = None
    for l in range(depth):
        x_res, (qwin, kwin, vwin, qcat, kcat, vext, sg) = _proj_call(
            l, alpha, x_all, moe, ln2_g, ln2_b, mods, tab, w_in_p, mla_kv_norm, mla_q_norm,
            w_kv, w_q, vones, ncb)
        oa = _win_call(l, attn_sink, qwin, kwin, vwin, n_ctx)
        ob = _mla_call(qcat, kcat, vext, n_ctx)
        x1, h2, lg_t = _merge_call(l, alpha, x_res, oa, ob, sg, w_oa_b, w_ob_b, w_out_b,
                                   ln1_g, ln1_b, mods, w_r, ncb)
        aff, idx4 = _route_call(lg_t, n_ctx, cap_c, cap_l)
        idx = idx4.reshape(B, N_EXPERTS, cap_c + cap_l)
        xg = _gather_call(idx, h2)
        moe = _moe_call(l, idx, aff, xg, w_exp_gate, w_exp_up, w_exp_down, S)
        x_all = x1
    return _final_call(depth - 1, alpha, x_all, moe, ln2_g, ln2_b, mods, n_ctx)
```

```python
import functools
import math

import jax
import jax.numpy as jnp
from jax import lax
from jax.experimental import pallas as pl
from jax.experimental.pallas import tpu as pltpu

D_MODEL = 1024
GRID_W = 64
WIN_HEADS = 8
WIN_KV_HEADS = 2
WIN_HEAD_DIM = 64
BAND = 128
MLA_HEADS = 8
MLA_Q_RANK = 384
MLA_KV_RANK = 256
MLA_NOPE = 64
MLA_ROPE = 32
MLA_V = 64
N_EXPERTS = 16
CAPACITY_FACTOR = 2
ROPE_THETA = 10000.0
LN_EPS = 1e-5
RMS_EPS = 1e-6
NEG_INF = -1e30
LOG2E = math.log2(math.e)
WIN_SCALE = WIN_HEAD_DIM ** -0.5
MLA_SCALE = (MLA_NOPE + MLA_ROPE) ** -0.5

LANES = 128
TOK_BLOCK = 256
VMEM_LIMIT = 56 * 1024 * 1024

BF16 = jnp.bfloat16
F32 = jnp.float32

SEG_KWIN = (0, 128)
SEG_VWIN = (128, 256)
SEG_CKV = (256, 512)
SEG_KR = (512, 640)
SEG_QWIN = (640, 1152)
SEG_CQ = (1152, 1536)
SEG_GATE = (1536, 3584)
IN_COLS_PAD = 3584


def _dot(a, b):
    return jnp.dot(a, b, preferred_element_type=F32)


def _dot_nt(a, b):
    return lax.dot_general(a, b, (((1,), (1,)), ((), ())), preferred_element_type=F32)


def _layer_norm(z, g, b):
    mu = jnp.mean(z, axis=-1, keepdims=True)
    zc = z - mu
    var = jnp.mean(zc * zc, axis=-1, keepdims=True)
    return zc * lax.rsqrt(var + LN_EPS) * g + b


def _rms_norm(x, g):
    return x * lax.rsqrt(jnp.mean(x * x, axis=-1, keepdims=True) + RMS_EPS) * g


def _rope_slab(x, cos, sin, half):
    lane = lax.broadcasted_iota(jnp.int32, x.shape, 1)
    partner = jnp.where((lane & half) == 0,
                        pltpu.roll(x, LANES - half, 1), pltpu.roll(x, half, 1))
    return x * cos + partner * sin


SUBLANES = 8


def _rows_to_tiles(y):
    n = y.shape[0]
    y3 = pltpu.einshape("r(sl)->srl", y, s=SUBLANES)
    y4 = y3.reshape(SUBLANES, n // SUBLANES, SUBLANES, LANES)
    return jnp.transpose(y4, (1, 2, 0, 3)).reshape(n, SUBLANES, LANES)


def _tiles_to_rows(x3):
    n = x3.shape[0]
    x4 = x3.reshape(n // SUBLANES, SUBLANES, SUBLANES, LANES)
    xs = jnp.transpose(x4, (2, 0, 1, 3)).reshape(SUBLANES, n, LANES)
    return pltpu.einshape("srl->r(sl)", xs)


def _compiler_params(sem):
    return pltpu.CompilerParams(dimension_semantics=sem, vmem_limit_bytes=VMEM_LIMIT)


ADA_ROWS = 8
ADA_TN = 512


def _ada_kernel(n_rows, condt_ref, w_ref, b_ref, o_ref):
    ct = condt_ref[...]
    st = ct * jax.nn.sigmoid(ct)
    w = w_ref[0]
    rows = []
    for r in range(n_rows):
        rows.append(jnp.sum(w * st[:, r:r + 1], axis=0, keepdims=True) + b_ref[0])
    rows.append(jnp.zeros((ADA_ROWS - n_rows, w.shape[1]), F32))
    o_ref[0] = jnp.concatenate(rows, axis=0)


def _ada_call(cond_t, w_ada, b_ada, n_rows):
    L, D, N = w_ada.shape
    return pl.pallas_call(
        functools.partial(_ada_kernel, n_rows),
        grid=(L, N // ADA_TN),
        in_specs=[
            pl.BlockSpec((D, ADA_ROWS), lambda l, j: (0, 0)),
            pl.BlockSpec((1, D, ADA_TN), lambda l, j: (l, 0, j)),
            pl.BlockSpec((1, 1, ADA_TN), lambda l, j: (l, 0, j)),
        ],
        out_specs=pl.BlockSpec((1, ADA_ROWS, ADA_TN), lambda l, j: (l, 0, j)),
        out_shape=jax.ShapeDtypeStruct((L, ADA_ROWS, N), F32),
        compiler_params=_compiler_params(("parallel", "parallel")),
        name="adaln_mod",
    )(cond_t, w_ada, b_ada.reshape(L, 1, N))


def _proj_kernel(has_ln, alpha, n_ctx_blocks, *refs):
    if has_ln:
        (x_ref, moe_ref, g2_ref, b2_ref, m5_ref, sh_ref, sc_ref, tab_ref, win_ref, gkv_ref, gq_ref,
         wkv_ref, wq_ref, vones_ref,
         xres_ref, qwin_ref, kwin_ref, vwin_ref, qcat_ref, kcat_ref, vext_ref, sg_ref) = refs
        z = _tiles_to_rows(alpha * x_ref[0] + m5_ref[0] * moe_ref[0])
        x = _layer_norm(z, g2_ref[0], b2_ref[0])
    else:
        (ctx_ref, x_ref, sh_ref, sc_ref, tab_ref, win_ref, gkv_ref, gq_ref, wkv_ref, wq_ref, vones_ref,
         xres_ref, qwin_ref, kwin_ref, vwin_ref, qcat_ref, kcat_ref, vext_ref, sg_ref) = refs
        x = jnp.where(pl.program_id(0) < n_ctx_blocks, ctx_ref[0], x_ref[0])
    xres_ref[0] = x
    hb = (x * (1.0 + sc_ref[0]) + sh_ref[0]).astype(BF16)

    def seg(s):
        return _dot(hb, win_ref[0, :, s[0]:s[1]])

    def tab(i):
        return tab_ref[:, i * LANES:(i + 1) * LANES]

    cq_w, sq_w, ck_w, sk_w, cq_m, sq_m, ck_m, sk_m = (tab(i) for i in range(8))

    lane_lo = lax.broadcasted_iota(jnp.int32, (hb.shape[0], LANES), 1) < WIN_HEAD_DIM
    half_swap = lambda t: pltpu.roll(t, WIN_HEAD_DIM, 1)

    ckv = _rms_norm(seg(SEG_CKV), gkv_ref[0]).astype(BF16)
    cq = _rms_norm(seg(SEG_CQ), gq_ref[0]).astype(BF16)

    qw = seg(SEG_QWIN)
    q_slabs = []
    for s2 in range(WIN_HEADS // 2):
        pair = _rope_slab(qw[:, s2 * LANES:(s2 + 1) * LANES], cq_w, sq_w, 16)
        pair_sw = half_swap(pair)
        if 2 * s2 < WIN_HEADS // WIN_KV_HEADS:
            q_slabs += [jnp.where(lane_lo, pair, 0.0), jnp.where(lane_lo, pair_sw, 0.0)]
        else:
            q_slabs += [jnp.where(lane_lo, 0.0, pair_sw), jnp.where(lane_lo, 0.0, pair)]
    qwin_ref[0] = jnp.concatenate(q_slabs, axis=1).astype(BF16)

    kv = _dot(ckv, wkv_ref[0])
    kr = _rope_slab(seg(SEG_KR), ck_m, sk_m, 8)
    qm = _dot(cq, wq_ref[0])
    for h in range(MLA_HEADS):
        sl = slice(h * LANES, (h + 1) * LANES)
        kcat_ref[0, :, sl] = (kv[:, sl] + kr).astype(BF16)
        vext_ref[0, :, sl] = (kv[:, 1024 + h * LANES:1024 + (h + 1) * LANES]
                              + vones_ref[:, 512 + h * LANES:512 + (h + 1) * LANES]).astype(BF16)
        qcat_ref[0, :, sl] = _rope_slab(qm[:, sl], cq_m, sq_m, 8).astype(BF16)

    sg_ref[0] = jax.nn.sigmoid(seg(SEG_GATE)).astype(BF16)
    va = seg(SEG_VWIN)
    va_sw = half_swap(va)
    vwin_ref[0] = (jnp.concatenate(
        [jnp.where(lane_lo, va, 0.0), jnp.where(lane_lo, 0.0, va_sw),
         jnp.where(lane_lo, va_sw, 0.0), jnp.where(lane_lo, 0.0, va)], axis=1)
        + vones_ref[:, 0:512]).astype(BF16)
    kwin_ref[0] = _rope_slab(seg(SEG_KWIN), ck_w, sk_w, 16).astype(BF16)


def _proj_call(layer, alpha, x_all, moe, ln2_g, ln2_b, mods, tab, w_in_p, g_kvn, g_qn, w_kv, w_q,
               vones, n_ctx_blocks):
    T = TOK_BLOCK
    has_ln = moe is not None
    if has_ln:
        B, S = x_all.shape[0], x_all.shape[1]
        D = SUBLANES * LANES
    else:
        ctx_in, x_in = x_all
        B, D = x_in.shape[0], x_in.shape[2]
        S = ctx_in.shape[1] + x_in.shape[1]
    L = w_in_p.shape[0]

    def mod_spec(k, lyr):
        def imap(i, b):
            row = jnp.where(i < n_ctx_blocks, B, b)
            return ((lyr * ADA_ROWS + row) * 6 + k, 0, 0)
        return pl.BlockSpec((1, 1, D), imap)

    tok = lambda w: pl.BlockSpec((1, T, w), lambda i, b: (b, i, 0))
    headed = tok(MLA_HEADS * LANES)
    const = lambda shp: pl.BlockSpec(shp, lambda i, b: (layer,) + (0,) * (len(shp) - 1),
                                     pipeline_mode=pl.Buffered(1))

    if has_ln:
        prev = pl.BlockSpec((1, 1, D), lambda i, b: (layer - 1, 0, 0))
        tiled = pl.BlockSpec((1, T, SUBLANES, LANES), lambda i, b: (b, i, 0, 0))
        def gate_imap(i, b):
            row = jnp.where(i < n_ctx_blocks, B, b)
            return (((layer - 1) * ADA_ROWS + row) * 6 + 5, 0, 0)
        gate_tile = pl.BlockSpec((1, SUBLANES, LANES), gate_imap)
        in_specs = [tiled, tiled, prev, prev, gate_tile]
        args = [x_all, moe, ln2_g.reshape(L, 1, D), ln2_b.reshape(L, 1, D),
                mods.reshape(-1, SUBLANES, LANES)]
    else:
        assert ctx_in.shape[1] == n_ctx_blocks * T
        in_specs = [pl.BlockSpec((1, T, D), lambda i, b: (b, jnp.minimum(i, n_ctx_blocks - 1), 0)),
                    pl.BlockSpec((1, T, D), lambda i, b: (b, jnp.maximum(i - n_ctx_blocks, 0), 0))]
        args = [ctx_in, x_in]
    in_specs += [
        mod_spec(0, layer), mod_spec(1, layer),
        pl.BlockSpec((T, 8 * LANES), lambda i, b: (i, 0)),
        const((1, D, IN_COLS_PAD)),
        const((1, 1, MLA_KV_RANK)), const((1, 1, MLA_Q_RANK)),
        const((1, MLA_KV_RANK, 2048)), const((1, MLA_Q_RANK, 1024)),
        pl.BlockSpec((1, 1536), lambda i, b: (0, 0)),
    ]
    args += [mods, mods, tab, w_in_p, g_kvn.reshape(L, 1, -1), g_qn.reshape(L, 1, -1), w_kv, w_q, vones]

    out_specs, out_shape = [tok(D)], [jax.ShapeDtypeStruct((B, S, D), F32)]
    out_specs += [tok(WIN_HEADS * LANES), tok(LANES), tok(512), headed, headed, headed, tok(2048)]
    out_shape += [jax.ShapeDtypeStruct((B, S, WIN_HEADS * LANES), BF16),
                  jax.ShapeDtypeStruct((B, S, LANES), BF16),
                  jax.ShapeDtypeStruct((B, S, 512), BF16)]
    out_shape += [jax.ShapeDtypeStruct((B, S, MLA_HEADS * LANES), BF16)] * 3
    out_shape += [jax.ShapeDtypeStruct((B, S, 2048), BF16)]

    outs = pl.pallas_call(
        functools.partial(_proj_kernel, has_ln, alpha, n_ctx_blocks),
        grid=(S // T, B),
        in_specs=in_specs, out_specs=out_specs, out_shape=out_shape,
        compiler_params=_compiler_params(("parallel", "parallel")),
        name="proj",
    )(*args)
    return outs[0], outs[1:]


WIN_HEAD_ORDER = (0, 2, 1, 3, 4, 6, 5, 7)


def _win_kernel(layer, n_ctx, n_steps, sink_ref, q_ref, kc_ref, vc_ref, kp_ref, km_ref, kn_ref,
                vp_ref, vm_ref, vn_ref, o_ref, kall_ref, vall_ref, p_ref):
    i = pl.program_id(1)
    row = lax.broadcasted_iota(jnp.int32, (BAND, LANES), 0)
    col = lax.broadcasted_iota(jnp.int32, (BAND, LANES), 1)
    lane_lo = col < 64
    kall_ref[0:n_ctx, :] = kc_ref[0]
    vall_ref[0:n_ctx, :] = vc_ref[0]

    def scores(qb, r0, nk):
        q = q_ref[0, qb * BAND:(qb + 1) * BAND, :]
        q8 = jnp.concatenate([q[:, h * LANES:(h + 1) * LANES] for h in WIN_HEAD_ORDER], axis=0)
        return _dot_nt(q8, kall_ref[r0:r0 + nk, :])

    def softmax(qb, nk, masks, s):
        e_sink = []
        for c, h in enumerate(WIN_HEAD_ORDER):
            sc = s[c * BAND:(c + 1) * BAND, :]
            if masks:
                blocks = []
                for kb in range(nk // BAND):
                    blk = sc[:, kb * BAND:(kb + 1) * BAND]
                    blocks.append(jnp.where(masks[kb], blk, NEG_INF) if kb in masks else blk)
                sc = jnp.concatenate(blocks, axis=1)
            sink = sink_ref[layer, h] * LOG2E
            m = jnp.maximum(jnp.max(sc, axis=1, keepdims=True), sink)
            p_ref[qb, c * BAND:(c + 1) * BAND, 0:nk] = jnp.exp2(sc - m).astype(BF16)
            e_sink.append(jnp.exp2(sink - m))
        return e_sink

    def values(qb, r0, nk, e_sink):
        pairs = []
        for pi in range(4):
            acc = _dot(p_ref[qb, pi * 2 * BAND:(pi + 1) * 2 * BAND, 0:nk],
                       vall_ref[r0:r0 + nk, pi * LANES:(pi + 1) * LANES])
            es = jnp.concatenate([e_sink[2 * pi], e_sink[2 * pi + 1]], axis=0)
            den = (acc[:, 64:65] if pi % 2 == 0 else acc[:, 0:1]) + es
            pairs.append(acc / den)
        slabs = [jnp.where(lane_lo, pairs[0][:BAND], pairs[1][:BAND]),
                 jnp.where(lane_lo, pairs[0][BAND:], pairs[1][BAND:]),
                 jnp.where(lane_lo, pairs[2][:BAND], pairs[3][:BAND]),
                 jnp.where(lane_lo, pairs[2][BAND:], pairs[3][BAND:])]
        o_ref[0, qb * BAND:(qb + 1) * BAND, :] = jnp.concatenate(slabs, axis=1).astype(o_ref.dtype)

    def attend_both(args0, args1):
        (r0, nk0, m0), (r1, nk1, m1) = args0, args1
        s0 = scores(0, r0, nk0)
        s1 = scores(1, r1, nk1)
        e0 = softmax(0, nk0, m0, s0)
        e1 = softmax(1, nk1, m1, s1)
        values(0, r0, nk0, e0)
        values(1, r1, nk1, e1)

    @pl.when(i == 0)
    def _():
        attend_both((0, n_ctx, None), (0, n_ctx, None))

    @pl.when(i > 0)
    def _():
        r = n_ctx
        for k_ref, v_ref, n in ((kp_ref, vp_ref, BAND), (km_ref, vm_ref, 2 * BAND),
                                (kn_ref, vn_ref, BAND), (kc_ref, vc_ref, n_ctx)):
            kall_ref[r:r + n, :] = k_ref[0]
            vall_ref[r:r + n, :] = v_ref[0]
            r += n
        off_first = jnp.where(i == 1, 2 * LANES, 0)
        off_last = jnp.where(i == n_steps - 1, 2 * LANES, 0)
        cb = n_ctx // BAND
        attend_both((0, n_ctx + 3 * BAND, {cb: col >= row + off_first, cb + 2: col <= row}),
                    (n_ctx + BAND, n_ctx + 3 * BAND, {0: col >= row, 2: col <= row - off_last}))


def _win_call(layer, sink, qwin, kwin, vwin, n_ctx):
    B, S, _ = qwin.shape
    nb = S // BAND
    ncb = n_ctx // BAND
    assert n_ctx == 2 * BAND and nb % 2 == 0
    nk = n_ctx + 3 * BAND
    one = lambda w, f: pl.BlockSpec((1, BAND, w), lambda b, i: (b, jnp.clip(f(i), ncb, nb - 1), 0))
    two = lambda w: pl.BlockSpec((1, 2 * BAND, w), lambda b, i: (b, i, 0))
    ctx = lambda w: pl.BlockSpec((1, n_ctx, w), lambda b, i: (b, 0, 0))
    prev, nxt = (lambda i: 2 * i - 1), (lambda i: 2 * i + 2)
    return pl.pallas_call(
        functools.partial(_win_kernel, layer, n_ctx, nb // 2),
        grid=(B, nb // 2),
        in_specs=[pl.BlockSpec(memory_space=pltpu.SMEM), two(WIN_HEADS * LANES),
                  ctx(LANES), ctx(512), one(LANES, prev), two(LANES), one(LANES, nxt),
                  one(512, prev), two(512), one(512, nxt)],
        out_specs=two(512),
        out_shape=jax.ShapeDtypeStruct((B, S, 512), BF16),
        scratch_shapes=[pltpu.VMEM((2 * n_ctx + 4 * BAND, LANES), BF16),
                        pltpu.VMEM((2 * n_ctx + 4 * BAND, 512), BF16),
                        pltpu.VMEM((2, WIN_HEADS * BAND, nk), BF16)],
        compiler_params=_compiler_params(("parallel", "parallel")),
        name="win_attn",
    )(sink, qwin, kwin, vwin, kwin, kwin, kwin, vwin, vwin, vwin)


MLA_KCHUNK = 256
MLA_HEADS_PER_STEP = 4


def _mla_kernel(n_ctx, n_keys, q_ref, k_ref, v_ref, o_ref, s_ref):
    qi = pl.program_id(2)
    tq = q_ref.shape[1]
    lane = lax.broadcasted_iota(jnp.int32, (tq, LANES), 1)

    def run(nk):
        row_max = []
        for hh in range(MLA_HEADS_PER_STEP):
            hl = slice(hh * LANES, (hh + 1) * LANES)
            q = q_ref[0, :, hl]
            mrun = None
            for c in range(nk // MLA_KCHUNK):
                s = _dot_nt(q, k_ref[0, c * MLA_KCHUNK:(c + 1) * MLA_KCHUNK, hl])
                s_ref[hh, c] = s
                mc = jnp.maximum(s[:, :LANES], s[:, LANES:])
                mrun = mc if mrun is None else jnp.maximum(mrun, mc)
            row_max.append(jnp.max(mrun, axis=1, keepdims=True))
        outs = []
        for hh in range(MLA_HEADS_PER_STEP):
            acc = jnp.zeros((tq, LANES), F32)
            for c in range(nk // MLA_KCHUNK):
                p = jnp.exp2(s_ref[hh, c] - row_max[hh]).astype(BF16)
                acc = acc + _dot(p, v_ref[0, c * MLA_KCHUNK:(c + 1) * MLA_KCHUNK,
                                          hh * LANES:(hh + 1) * LANES])
            den = acc[:, 64:65] if hh % 2 == 0 else acc[:, 0:1]
            outs.append(acc / den)
        o_ref[0] = jnp.concatenate(
            [jnp.where(lane < 64, outs[2 * j], outs[2 * j + 1]) for j in range(len(outs) // 2)],
            axis=1).astype(o_ref.dtype)

    @pl.when(qi == 0)
    def _():
        run(n_ctx)

    @pl.when(qi > 0)
    def _():
        run(n_keys)


def _mla_call(qcat, kcat, vext, n_ctx):
    B, S, _ = qcat.shape
    H = MLA_HEADS
    T = TOK_BLOCK
    assert n_ctx == T
    G = MLA_HEADS_PER_STEP
    kv = pl.BlockSpec((1, S, G * LANES), lambda b, hp, i: (b, 0, hp))
    return pl.pallas_call(
        functools.partial(_mla_kernel, n_ctx, S),
        grid=(B, H // G, S // T),
        in_specs=[pl.BlockSpec((1, T, G * LANES), lambda b, hp, i: (b, i, hp)), kv, kv],
        out_specs=pl.BlockSpec((1, T, (G // 2) * LANES), lambda b, hp, i: (b, i, hp)),
        out_shape=jax.ShapeDtypeStruct((B, S, (H // 2) * LANES), BF16),
        scratch_shapes=[pltpu.VMEM((G, S // MLA_KCHUNK, T, MLA_KCHUNK), F32)],
        compiler_params=_compiler_params(("parallel", "parallel", "arbitrary")),
        name="mla_attn",
    )(qcat, kcat, vext)


def _merge_kernel(alpha, x_ref, oa_ref, ob_ref, sg_ref, woa_ref, wob_ref, wout_ref, g1_ref, b1_ref,
                  m2_ref, m3_ref, m4_ref, wr_ref, x1_ref, lg_ref):
    D = x_ref.shape[2]
    sg = sg_ref[0]
    t = (sg[:, :D].astype(F32) * _dot(oa_ref[0], woa_ref[0])
         + sg[:, D:].astype(F32) * _dot(ob_ref[0], wob_ref[0]))
    y = _dot(t.astype(BF16), wout_ref[0])
    x1 = _layer_norm(alpha * x_ref[0] + m2_ref[0] * y, g1_ref[0], b1_ref[0])
    x1_ref[0] = _rows_to_tiles(x1)
    h2 = x1 * (1.0 + m4_ref[0]) + m3_ref[0]
    lg = _dot(h2.astype(BF16), wr_ref[0])
    lg_ref[0] = lg.T[:N_EXPERTS]


def _merge_call(layer, alpha, x_res, oa, ob, sg, w_oa, w_ob, w_out, ln1_g, ln1_b, mods, w_r,
                n_ctx_blocks):
    B, S, D = x_res.shape
    T = TOK_BLOCK
    L = w_oa.shape[0]

    def mod_spec(k):
        def imap(b, i):
            row = jnp.where(i < n_ctx_blocks, B, b)
            return ((layer * ADA_ROWS + row) * 6 + k, 0, 0)
        return pl.BlockSpec((1, 1, D), imap)

    tok = lambda w: pl.BlockSpec((1, T, w), lambda b, i: (b, i, 0))
    const = lambda shp: pl.BlockSpec(shp, lambda b, i: (layer,) + (0,) * (len(shp) - 1),
                                     pipeline_mode=pl.Buffered(1))
    return pl.pallas_call(
        functools.partial(_merge_kernel, alpha),
        grid=(B, S // T),
        in_specs=[tok(D), tok(512), tok(512), tok(2048),
                  const((1, 512, D)), const((1, 512, D)), const((1, D, D)),
                  const((1, 1, D)), const((1, 1, D)),
                  mod_spec(2), mod_spec(3), mod_spec(4),
                  const((1, D, LANES))],
        out_specs=[pl.BlockSpec((1, T, SUBLANES, LANES), lambda b, i: (b, i, 0, 0)),
                   pl.BlockSpec((1, N_EXPERTS, T), lambda b, i: (b, 0, i))],
        out_shape=[jax.ShapeDtypeStruct((B, S, SUBLANES, LANES), F32),
                   jax.ShapeDtypeStruct((B, N_EXPERTS, S), F32)],
        compiler_params=_compiler_params(("parallel", "parallel")),
        name="merge",
    )(x_res, oa, ob, sg, w_oa, w_ob, w_out, ln1_g.reshape(L, 1, D), ln1_b.reshape(L, 1, D),
      mods, mods, mods, w_r)


def _cumsum_lanes(x):
    n = x.shape[1]
    xb = x.astype(BF16)
    r = lax.broadcasted_iota(jnp.int32, (LANES, LANES), 0)
    c = lax.broadcasted_iota(jnp.int32, (LANES, LANES), 1)
    tri = jnp.where(r <= c, 1.0, 0.0).astype(BF16)
    tb = lax.broadcasted_iota(jnp.int32, (n, LANES), 0) // LANES
    kb = lax.broadcasted_iota(jnp.int32, (n, LANES), 1)
    before = jnp.where(tb < kb, 1.0, 0.0).astype(BF16)
    off = _dot(xb, before)
    outs = []
    for k in range(n // LANES):
        outs.append(_dot(xb[:, k * LANES:(k + 1) * LANES], tri) + off[:, k:k + 1])
    return outs


def _select_top(aff, cap):
    bits = pltpu.bitcast(aff, jnp.int32)
    thr = jnp.zeros((aff.shape[0], 1), jnp.int32)
    for bit in range(30, -1, -1):
        cand = thr | (1 << bit)
        cnt = jnp.sum(jnp.where(bits >= cand, 1.0, 0.0), axis=1, keepdims=True)
        thr = jnp.where(cnt >= cap, cand, thr)
    gt = bits > thr
    eq = jnp.where(bits == thr, 1.0, 0.0)
    need = cap - jnp.sum(jnp.where(gt, 1.0, 0.0), axis=1, keepdims=True)
    eq_rank = jnp.concatenate(_cumsum_lanes(eq), axis=1) - eq
    sel = jnp.where(jnp.logical_or(gt, jnp.logical_and(eq > 0.5, eq_rank < need)), 1.0, 0.0)
    return _cumsum_lanes(sel)


ROUTE_PAD = 1e6


def _route_kernel(n_ctx, cap_c, cap_l, lg_ref, aff_ref, idx_ref, cc_ref, cl_ref):
    lg = lg_ref[0]
    m = jnp.max(lg, axis=0, keepdims=True)
    ex = jnp.exp(lg - m)
    aff = ex / jnp.sum(ex, axis=0, keepdims=True)
    aff_ref[0] = aff
    for c_ref, blocks in ((cc_ref, _select_top(aff[:, :n_ctx], cap_c)),
                          (cl_ref, _select_top(aff[:, n_ctx:], cap_l))):
        c_ref[...] = jnp.full(c_ref.shape, ROUTE_PAD, F32)
        for k, blk in enumerate(blocks):
            for ex_i in range(N_EXPERTS):
                c_ref[ex_i, k:k + 1, :] = blk[ex_i:ex_i + 1, :]

    def slots(c_ref, e, n_blocks, width):
        cm = c_ref[e]
        jrow = lax.broadcasted_iota(jnp.int32, (1, width), 1).astype(F32)
        rows = max(n_blocks, SUBLANES)
        cend = cm[0:rows, LANES - 1:LANES]
        blk = jnp.sum(jnp.where(cend <= jrow, 1.0, 0.0), axis=0, keepdims=True)
        kio = lax.broadcasted_iota(jnp.int32, (LANES, width), 0).astype(F32)
        onehot = jnp.where(kio == blk, 1.0, 0.0).astype(BF16)
        cmt = cm.T
        hi = jnp.floor(cmt * (1.0 / 256.0))
        lo = cmt - 256.0 * hi
        cg = 256.0 * _dot(hi.astype(BF16), onehot) + _dot(lo.astype(BF16), onehot)
        inside = jnp.sum(jnp.where(cg <= jrow, 1.0, 0.0), axis=0, keepdims=True)
        return (blk * LANES + inside).astype(jnp.int32)

    def per_expert(e, carry):
        n_lat_blocks = (lg_ref.shape[2] - n_ctx) // LANES
        idx_ref[0, e, :, 0:cap_l] = slots(cl_ref, e, n_lat_blocks, cap_l) + n_ctx
        idx_ref[0, e, :, cap_l:cap_l + cap_c] = slots(cc_ref, e, n_ctx // LANES, LANES)[:, 0:cap_c]
        return carry

    lax.fori_loop(0, N_EXPERTS, per_expert, 0)


def _route_call(lg_t, n_ctx, cap_c, cap_l):
    B, E, S = lg_t.shape
    cap = cap_c + cap_l
    assert (S - n_ctx) // LANES <= LANES and cap_l % LANES == 0 and cap_c <= LANES
    return pl.pallas_call(
        functools.partial(_route_kernel, n_ctx, cap_c, cap_l),
        grid=(B,),
        in_specs=[pl.BlockSpec((1, E, S), lambda b: (b, 0, 0))],
        out_specs=[pl.BlockSpec((1, E, S), lambda b: (b, 0, 0)),
                   pl.BlockSpec((1, E, 1, cap), lambda b: (b, 0, 0, 0))],
        out_shape=[jax.ShapeDtypeStruct((B, E, S), F32),
                   jax.ShapeDtypeStruct((B, E, 1, cap), jnp.int32)],
        scratch_shapes=[pltpu.VMEM((E, LANES, LANES), F32), pltpu.VMEM((E, LANES, LANES), F32)],
        compiler_params=_compiler_params(("parallel",)),
        name="route",
    )(lg_t)


ROW_BATCH = 16
MOE_FF_CHUNK = 512


GATHER_EXPERTS = 2


def _gather_kernel(cap_l, idx_ref, x_ref, shl_ref, scl_ref, shc_ref, scc_ref, xg_ref, tmp_ref):
    cap = xg_ref.shape[2]
    for ge in range(GATHER_EXPERTS):
        def body(g, carry):
            for r in range(ROW_BATCH):
                j = g * ROW_BATCH + r
                tmp_ref[j] = x_ref[0, idx_ref[0, 0, ge * cap + j]]
            return carry

        lax.fori_loop(0, cap // ROW_BATCH, body, 0)
        lat = tmp_ref[0:cap_l] * (1.0 + scl_ref[0]) + shl_ref[0]
        ctx = tmp_ref[cap_l:cap] * (1.0 + scc_ref[0]) + shc_ref[0]
        xg_ref[0, ge] = _tiles_to_rows(jnp.concatenate([lat, ctx], axis=0)).astype(xg_ref.dtype)


def _gather_call(layer, idx, x1t, mods, cap_l):
    B, S = x1t.shape[0], x1t.shape[1]
    D = SUBLANES * LANES
    E, cap = idx.shape[1], idx.shape[2]
    G = GATHER_EXPERTS
    assert cap % ROW_BATCH == 0 and E % G == 0 and cap_l % SUBLANES == 0

    def mod_tile(k, ctx):
        def imap(b, e):
            return ((layer * ADA_ROWS + (B if ctx else b)) * 6 + k, 0, 0)
        return pl.BlockSpec((1, SUBLANES, LANES), imap)

    mods_t = mods.reshape(-1, SUBLANES, LANES)
    return pl.pallas_call(
        functools.partial(_gather_kernel, cap_l),
        grid=(B, E // G),
        in_specs=[pl.BlockSpec((1, 1, G * cap), lambda b, e: (b * (E // G) + e, 0, 0),
                               memory_space=pltpu.SMEM),
                  pl.BlockSpec((1, S, SUBLANES, LANES), lambda b, e: (b, 0, 0, 0)),
                  mod_tile(3, False), mod_tile(4, False), mod_tile(3, True), mod_tile(4, True)],
        out_specs=pl.BlockSpec((1, G, cap, D), lambda b, e: (b, e, 0, 0)),
        out_shape=jax.ShapeDtypeStruct((B, E, cap, D), BF16),
        scratch_shapes=[pltpu.VMEM((cap, SUBLANES, LANES), F32)],
        compiler_params=_compiler_params(("parallel", "arbitrary")),
        name="moe_gather",
    )(idx.reshape(B * E // G, 1, G * cap), x1t, mods_t, mods_t, mods_t, mods_t)


def _moe_kernel(idx_ref, aff_ref, xg_ref, wg_ref, wu_ref, wd_ref, out_ref, y_ref):
    s = pl.program_id(1)
    n_exp = pl.num_programs(1) - 1
    cap = y_ref.shape[1]
    n_batches = cap // ROW_BATCH

    def scatter(slot, batches):
        new = None
        for g in batches:
            rows = _rows_to_tiles(y_ref[slot, g * ROW_BATCH:(g + 1) * ROW_BATCH, :])
            toks = [idx_ref[0, 0, g * ROW_BATCH + r] for r in range(ROW_BATCH)]
            old = [out_ref[0, t] for t in toks]
            for r, t in enumerate(toks):
                new = old[r] + rows[r] * aff_ref[0, 0, t]
                out_ref[0, t] = new
        return new

    def ffn(slot, prev_slot):
        ff = wg_ref.shape[3]
        n_dots = 3 * (ff // MOE_FF_CHUNK)
        per = -(-n_batches // n_dots)
        todo = list(range(n_batches)) if prev_slot is not None else []

        def weights(w):
            take, todo[:] = todo[:per], todo[per:]
            w = w.astype(BF16)
            if not take:
                return w
            last = scatter(prev_slot, take)
            bits = pltpu.bitcast(last, jnp.uint32)
            zero = pltpu.bitcast((bits >> 16) >> 16, F32)
            zero = jnp.concatenate([zero, zero], axis=0).astype(BF16)
            return w + jnp.tile(zero, (w.shape[0] // zero.shape[0], w.shape[1] // LANES))

        x = xg_ref[0, 0]
        y = None
        for f0 in range(0, ff, MOE_FF_CHUNK):
            f1 = f0 + MOE_FF_CHUNK
            a = _dot(x, weights(wg_ref[0, 0, :, f0:f1]))
            u = _dot(x, weights(wu_ref[0, 0, :, f0:f1]))
            hmid = (a * jax.nn.sigmoid(a) * u).astype(BF16)
            part = _dot(hmid, weights(wd_ref[0, 0, f0:f1, :]))
            y = part if y is None else y + part
        y_ref[slot] = y

    @pl.when(s == 0)
    def _():
        out_ref[...] = jnp.zeros(out_ref.shape, out_ref.dtype)
        ffn(0, None)

    @pl.when(jnp.logical_and(s > 0, s < n_exp))
    def _():
        slot = s % 2
        ffn(slot, 1 - slot)

    @pl.when(s == n_exp)
    def _():
        scatter((n_exp - 1) % 2, range(n_batches))


def _moe_call(layer, idx, aff, xg, w_gate, w_up, w_down, S):
    B, E, cap, D = xg.shape
    F = w_gate.shape[3]
    cur = lambda s: jnp.minimum(s, E - 1)
    prev = lambda s: jnp.maximum(s - 1, 0)
    wspec = lambda shp: pl.BlockSpec(shp, lambda b, s: (layer, cur(s), 0, 0))
    return pl.pallas_call(
        _moe_kernel,
        grid=(B, E + 1),
        in_specs=[pl.BlockSpec((1, 1, cap), lambda b, s: (b * E + prev(s), 0, 0), memory_space=pltpu.SMEM),
                  pl.BlockSpec((1, 1, S), lambda b, s: (b * E + prev(s), 0, 0), memory_space=pltpu.SMEM),
                  pl.BlockSpec((1, 1, cap, D), lambda b, s: (b, cur(s), 0, 0)),
                  wspec((1, 1, D, F)), wspec((1, 1, D, F)), wspec((1, 1, F, D))],
        out_specs=pl.BlockSpec((1, S, SUBLANES, LANES), lambda b, s: (b, 0, 0, 0),
                               pipeline_mode=pl.Buffered(1)),
        out_shape=jax.ShapeDtypeStruct((B, S, SUBLANES, LANES), F32),
        scratch_shapes=[pltpu.VMEM((2, cap, D), F32)],
        compiler_params=_compiler_params(("parallel", "arbitrary")),
        name="moe_ffn",
    )(idx.reshape(B * E, 1, cap), aff.reshape(B * E, 1, S), xg, w_gate, w_up, w_down)


def _final_kernel(alpha, x_ref, moe_ref, g_ref, b_ref, m5_ref, o_ref):
    z = _tiles_to_rows(alpha * x_ref[0] + m5_ref[0] * moe_ref[0])
    o_ref[0] = _layer_norm(z, g_ref[0], b_ref[0])


def _final_call(layer, alpha, x1t, moe, ln2_g, ln2_b, mods, n_ctx):
    B, S = x1t.shape[0], x1t.shape[1]
    D = SUBLANES * LANES
    T = TOK_BLOCK
    L = ln2_g.shape[0]
    ncb = n_ctx // T
    tiled = pl.BlockSpec((1, T, SUBLANES, LANES), lambda b, i: (b, i + ncb, 0, 0))
    const = pl.BlockSpec((1, 1, D), lambda b, i: (layer, 0, 0))
    return pl.pallas_call(
        functools.partial(_final_kernel, alpha),
        grid=(B, (S - n_ctx) // T),
        in_specs=[tiled, tiled, const, const,
                  pl.BlockSpec((1, SUBLANES, LANES),
                               lambda b, i: ((layer * ADA_ROWS + b) * 6 + 5, 0, 0))],
        out_specs=pl.BlockSpec((1, T, D), lambda b, i: (b, i, 0)),
        out_shape=jax.ShapeDtypeStruct((B, S - n_ctx, D), F32),
        compiler_params=_compiler_params(("parallel", "parallel")),
        name="final_norm",
    )(x1t, moe, ln2_g.reshape(L, 1, D), ln2_b.reshape(L, 1, D), mods.reshape(-1, SUBLANES, LANES))


def _prep_weights(w_in, w_uq, w_ukv, w_router):
    L, D, _ = w_in.shape
    krs = jnp.concatenate([jnp.zeros((L, D, 64), w_in.dtype), w_in[..., 512:544],
                           jnp.zeros((L, D, 32), w_in.dtype)], axis=-1)
    w_in_p = jnp.concatenate([w_in[..., 0:512], krs, w_in[..., 544:]], axis=-1).astype(BF16)

    kvr = w_ukv.reshape(L, MLA_KV_RANK, MLA_HEADS, MLA_NOPE + MLA_V)
    kn, vv = kvr[..., :MLA_NOPE], kvr[..., MLA_NOPE:]
    zk = jnp.zeros_like(kn)
    w_k = jnp.concatenate([kn, zk], axis=-1).reshape(L, MLA_KV_RANK, MLA_HEADS * LANES)
    v_even = jnp.concatenate([vv, zk], axis=-1)
    v_odd = jnp.concatenate([zk, vv], axis=-1)
    odd = (jnp.arange(MLA_HEADS) % 2 == 1)[None, None, :, None]
    w_v = jnp.where(odd, v_odd, v_even).reshape(L, MLA_KV_RANK, MLA_HEADS * LANES)
    w_kv = jnp.concatenate([w_k, w_v], axis=-1).astype(BF16)

    qr = w_uq.reshape(L, MLA_Q_RANK, MLA_HEADS, MLA_NOPE + MLA_ROPE)
    w_q = jnp.concatenate([qr, jnp.zeros((L, MLA_Q_RANK, MLA_HEADS, 32), w_uq.dtype)], axis=-1)
    w_q = w_q.reshape(L, MLA_Q_RANK, MLA_HEADS * LANES).astype(BF16)

    w_r = jnp.concatenate(
        [w_router, jnp.zeros((L, D, LANES - N_EXPERTS), w_router.dtype)], axis=-1).astype(BF16)
    return w_in_p, w_kv, w_q, w_r


def _ones_columns():
    lane = jnp.arange(LANES)
    even = (lane == 64).astype(F32)
    odd = (lane == 0).astype(F32)
    win = jnp.concatenate([even, odd, even, odd])
    mla = jnp.concatenate([even, odd] * (MLA_HEADS // 2))
    return jnp.concatenate([win, mla])[None, :]


def _rope_tables(n_ctx, n_lat):
    pos = jnp.arange(n_lat)
    rowp = (pos // GRID_W).astype(F32)
    colp = (pos % GRID_W).astype(F32)

    def pattern(rot_dim):
        nf = rot_dim // 4
        inv = ROPE_THETA ** (-jnp.arange(nf, dtype=F32) / nf)
        ar, ac = rowp[:, None] * inv, colp[:, None] * inv
        cos = jnp.concatenate([jnp.cos(ar), jnp.cos(ar), jnp.cos(ac), jnp.cos(ac)], axis=1)
        sin = jnp.concatenate([-jnp.sin(ar), jnp.sin(ar), -jnp.sin(ac), jnp.sin(ac)], axis=1)
        return cos, sin

    cw, sw = pattern(WIN_HEAD_DIM)
    cw, sw = jnp.tile(cw, (1, 2)), jnp.tile(sw, (1, 2))
    cm, sm = pattern(MLA_ROPE)
    one64, zero64 = jnp.ones((n_lat, 64), F32), jnp.zeros((n_lat, 64), F32)
    cm = jnp.concatenate([one64, cm, one64[:, :32]], axis=1)
    sm = jnp.concatenate([zero64, sm, zero64[:, :32]], axis=1)
    sq_w, sq_m = WIN_SCALE * LOG2E, MLA_SCALE * LOG2E
    lat = jnp.concatenate([cw * sq_w, sw * sq_w, cw, sw, cm * sq_m, sm * sq_m, cm, sm], axis=1)
    ones, zeros = jnp.ones((n_ctx, LANES), F32), jnp.zeros((n_ctx, LANES), F32)
    ctx = jnp.concatenate([ones * sq_w, zeros, ones, zeros, ones * sq_m, zeros, ones, zeros], axis=1)
    return jnp.concatenate([ctx, lat], axis=0)


def kernel(x, c, ctx, c_ctx, w_ada, b_ada, w_in, attn_sink, mla_q_norm, mla_kv_norm, w_uq, w_ukv,
           w_oa, w_ob, w_out, ln1_g, ln1_b, w_router, w_exp_gate, w_exp_up, w_exp_down, ln2_g, ln2_b):
    B, n_lat, D = x.shape
    n_ctx = ctx.shape[1]
    depth = w_in.shape[0]
    S = n_ctx + n_lat
    alpha = (2 * depth) ** 0.25
    assert D == D_MODEL and n_ctx == TOK_BLOCK and n_lat % TOK_BLOCK == 0 and B + 1 <= ADA_ROWS
    cap_c = CAPACITY_FACTOR * n_ctx // N_EXPERTS
    cap_l = CAPACITY_FACTOR * n_lat // N_EXPERTS
    assert cap_l % 256 == 0 and cap_c % 8 == 0
    ncb = n_ctx // TOK_BLOCK

    cond = jnp.concatenate([c, c_ctx[None], jnp.zeros((ADA_ROWS - B - 1, D), F32)], axis=0)
    mods = _ada_call(cond.T, w_ada, b_ada, B + 1).reshape(depth * ADA_ROWS * 6, 1, D)

    w_in_p, w_kv, w_q, w_r = _prep_weights(w_in, w_uq, w_ukv, w_router)
    w_oa_b, w_ob_b, w_out_b = w_oa.astype(BF16), w_ob.astype(BF16), w_out.astype(BF16)
    tab = _rope_tables(n_ctx, n_lat)
    vones = _ones_columns()

    x_all = (ctx, x)
    moe = None
    for l in range(depth):
        x_res, (qwin, kwin, vwin, qcat, kcat, vext, sg) = _proj_call(
            l, alpha, x_all, moe, ln2_g, ln2_b, mods, tab, w_in_p, mla_kv_norm, mla_q_norm,
            w_kv, w_q, vones, ncb)
        oa = _win_call(l, attn_sink, qwin, kwin, vwin, n_ctx)
        ob = _mla_call(qcat, kcat, vext, n_ctx)
        x1, lg_t = _merge_call(l, alpha, x_res, oa, ob, sg, w_oa_b, w_ob_b, w_out_b,
                               ln1_g, ln1_b, mods, w_r, ncb)
        aff, idx4 = _route_call(lg_t, n_ctx, cap_c, cap_l)
        idx = idx4.reshape(B, N_EXPERTS, cap_c + cap_l)
        xg = _gather_call(l, idx, x1, mods, cap_l)
        moe = _moe_call(l, idx, aff, xg, w_exp_gate, w_exp_up, w_exp_down, S)
        x_all = x1
    return _final_call(depth - 1, alpha, x_all, moe, ln2_g, ln2_b, mods, n_ctx)
```

```python
import functools
import math

import jax
import jax.numpy as jnp
from jax import lax
from jax.experimental import pallas as pl
from jax.experimental.pallas import tpu as pltpu

D_MODEL = 1024
GRID_W = 64
WIN_HEADS = 8
WIN_KV_HEADS = 2
WIN_HEAD_DIM = 64
BAND = 128
MLA_HEADS = 8
MLA_Q_RANK = 384
MLA_KV_RANK = 256
MLA_NOPE = 64
MLA_ROPE = 32
MLA_V = 64
N_EXPERTS = 16
CAPACITY_FACTOR = 2
ROPE_THETA = 10000.0
LN_EPS = 1e-5
RMS_EPS = 1e-6
NEG_INF = -1e30
LOG2E = math.log2(math.e)
WIN_SCALE = WIN_HEAD_DIM ** -0.5
MLA_SCALE = (MLA_NOPE + MLA_ROPE) ** -0.5

LANES = 128
TOK_BLOCK = 256
VMEM_LIMIT = 56 * 1024 * 1024

BF16 = jnp.bfloat16
F32 = jnp.float32

SEG_KWIN = (0, 128)
SEG_VWIN = (128, 256)
SEG_CKV = (256, 512)
SEG_KR = (512, 640)
SEG_QWIN = (640, 1152)
SEG_CQ = (1152, 1536)
SEG_GATE = (1536, 3584)
IN_COLS_PAD = 3584


def _dot(a, b):
    return jnp.dot(a, b, preferred_element_type=F32)


def _dot_nt(a, b):
    return lax.dot_general(a, b, (((1,), (1,)), ((), ())), preferred_element_type=F32)


def _layer_norm(z, g, b):
    mu = jnp.mean(z, axis=-1, keepdims=True)
    zc = z - mu
    var = jnp.mean(zc * zc, axis=-1, keepdims=True)
    return zc * lax.rsqrt(var + LN_EPS) * g + b


def _rms_norm(x, g):
    return x * lax.rsqrt(jnp.mean(x * x, axis=-1, keepdims=True) + RMS_EPS) * g


def _rope_slab(x, cos, sin, half):
    lane = lax.broadcasted_iota(jnp.int32, x.shape, 1)
    partner = jnp.where((lane & half) == 0,
                        pltpu.roll(x, LANES - half, 1), pltpu.roll(x, half, 1))
    return x * cos + partner * sin


SUBLANES = 8


def _rows_to_tiles(y):
    n = y.shape[0]
    y3 = pltpu.einshape("r(sl)->srl", y, s=SUBLANES)
    y4 = y3.reshape(SUBLANES, n // SUBLANES, SUBLANES, LANES)
    return jnp.transpose(y4, (1, 2, 0, 3)).reshape(n, SUBLANES, LANES)


def _tiles_to_rows(x3):
    n = x3.shape[0]
    x4 = x3.reshape(n // SUBLANES, SUBLANES, SUBLANES, LANES)
    xs = jnp.transpose(x4, (2, 0, 1, 3)).reshape(SUBLANES, n, LANES)
    return pltpu.einshape("srl->r(sl)", xs)


def _compiler_params(sem):
    return pltpu.CompilerParams(dimension_semantics=sem, vmem_limit_bytes=VMEM_LIMIT)


ADA_ROWS = 8
ADA_TN = 512


def _ada_kernel(n_rows, condt_ref, w_ref, b_ref, o_ref):
    ct = condt_ref[...]
    st = ct * jax.nn.sigmoid(ct)
    w = w_ref[0]
    rows = []
    for r in range(n_rows):
        rows.append(jnp.sum(w * st[:, r:r + 1], axis=0, keepdims=True) + b_ref[0])
    rows.append(jnp.zeros((ADA_ROWS - n_rows, w.shape[1]), F32))
    o_ref[0] = jnp.concatenate(rows, axis=0)


def _ada_call(cond_t, w_ada, b_ada, n_rows):
    L, D, N = w_ada.shape
    return pl.pallas_call(
        functools.partial(_ada_kernel, n_rows),
        grid=(L, N // ADA_TN),
        in_specs=[
            pl.BlockSpec((D, ADA_ROWS), lambda l, j: (0, 0)),
            pl.BlockSpec((1, D, ADA_TN), lambda l, j: (l, 0, j)),
            pl.BlockSpec((1, 1, ADA_TN), lambda l, j: (l, 0, j)),
        ],
        out_specs=pl.BlockSpec((1, ADA_ROWS, ADA_TN), lambda l, j: (l, 0, j)),
        out_shape=jax.ShapeDtypeStruct((L, ADA_ROWS, N), F32),
        compiler_params=_compiler_params(("parallel", "parallel")),
        name="adaln_mod",
    )(cond_t, w_ada, b_ada.reshape(L, 1, N))


def _proj_kernel(has_ln, alpha, n_ctx_blocks, *refs):
    if has_ln:
        (x_ref, moe_ref, g2_ref, b2_ref, m5_ref, sh_ref, sc_ref, tab_ref, win_ref, gkv_ref, gq_ref,
         wkv_ref, wq_ref, vones_ref,
         xres_ref, qwin_ref, kwin_ref, vwin_ref, qcat_ref, kcat_ref, vext_ref, sg_ref) = refs
        z = alpha * x_ref[0] + m5_ref[0] * _tiles_to_rows(moe_ref[0])
        x = _layer_norm(z, g2_ref[0], b2_ref[0])
    else:
        (ctx_ref, x_ref, sh_ref, sc_ref, tab_ref, win_ref, gkv_ref, gq_ref, wkv_ref, wq_ref, vones_ref,
         xres_ref, qwin_ref, kwin_ref, vwin_ref, qcat_ref, kcat_ref, vext_ref, sg_ref) = refs
        x = jnp.where(pl.program_id(0) < n_ctx_blocks, ctx_ref[0], x_ref[0])
    xres_ref[0] = x
    hb = (x * (1.0 + sc_ref[0]) + sh_ref[0]).astype(BF16)

    def seg(s):
        return _dot(hb, win_ref[0, :, s[0]:s[1]])

    def tab(i):
        return tab_ref[:, i * LANES:(i + 1) * LANES]

    cq_w, sq_w, ck_w, sk_w, cq_m, sq_m, ck_m, sk_m = (tab(i) for i in range(8))

    lane_lo = lax.broadcasted_iota(jnp.int32, (hb.shape[0], LANES), 1) < WIN_HEAD_DIM
    half_swap = lambda t: pltpu.roll(t, WIN_HEAD_DIM, 1)

    ckv = _rms_norm(seg(SEG_CKV), gkv_ref[0]).astype(BF16)
    cq = _rms_norm(seg(SEG_CQ), gq_ref[0]).astype(BF16)

    qw = seg(SEG_QWIN)
    q_slabs = []
    for s2 in range(WIN_HEADS // 2):
        pair = _rope_slab(qw[:, s2 * LANES:(s2 + 1) * LANES], cq_w, sq_w, 16)
        pair_sw = half_swap(pair)
        if 2 * s2 < WIN_HEADS // WIN_KV_HEADS:
            q_slabs += [jnp.where(lane_lo, pair, 0.0), jnp.where(lane_lo, pair_sw, 0.0)]
        else:
            q_slabs += [jnp.where(lane_lo, 0.0, pair_sw), jnp.where(lane_lo, 0.0, pair)]
    qwin_ref[0] = jnp.concatenate(q_slabs, axis=1).astype(BF16)

    kv = _dot(ckv, wkv_ref[0])
    kr = _rope_slab(seg(SEG_KR), ck_m, sk_m, 8)
    qm = _dot(cq, wq_ref[0])
    for h in range(MLA_HEADS):
        sl = slice(h * LANES, (h + 1) * LANES)
        kcat_ref[0, :, sl] = (kv[:, sl] + kr).astype(BF16)
        vext_ref[0, :, sl] = (kv[:, 1024 + h * LANES:1024 + (h + 1) * LANES]
                              + vones_ref[:, 512 + h * LANES:512 + (h + 1) * LANES]).astype(BF16)
        qcat_ref[0, :, sl] = _rope_slab(qm[:, sl], cq_m, sq_m, 8).astype(BF16)

    sg_ref[0] = jax.nn.sigmoid(seg(SEG_GATE)).astype(BF16)
    va = seg(SEG_VWIN)
    va_sw = half_swap(va)
    vwin_ref[0] = (jnp.concatenate(
        [jnp.where(lane_lo, va, 0.0), jnp.where(lane_lo, 0.0, va_sw),
         jnp.where(lane_lo, va_sw, 0.0), jnp.where(lane_lo, 0.0, va)], axis=1)
        + vones_ref[:, 0:512]).astype(BF16)
    kwin_ref[0] = _rope_slab(seg(SEG_KWIN), ck_w, sk_w, 16).astype(BF16)


def _proj_call(layer, alpha, x_all, moe, ln2_g, ln2_b, mods, tab, w_in_p, g_kvn, g_qn, w_kv, w_q,
               vones, n_ctx_blocks):
    T = TOK_BLOCK
    has_ln = moe is not None
    if has_ln:
        B, S, D = x_all.shape
    else:
        ctx_in, x_in = x_all
        B, D = x_in.shape[0], x_in.shape[2]
        S = ctx_in.shape[1] + x_in.shape[1]
    L = w_in_p.shape[0]

    def mod_spec(k, lyr):
        def imap(i, b):
            row = jnp.where(i < n_ctx_blocks, B, b)
            return ((lyr * ADA_ROWS + row) * 6 + k, 0, 0)
        return pl.BlockSpec((1, 1, D), imap)

    tok = lambda w: pl.BlockSpec((1, T, w), lambda i, b: (b, i, 0))
    headed = tok(MLA_HEADS * LANES)
    const = lambda shp: pl.BlockSpec(shp, lambda i, b: (layer,) + (0,) * (len(shp) - 1),
                                     pipeline_mode=pl.Buffered(1))

    if has_ln:
        prev = pl.BlockSpec((1, 1, D), lambda i, b: (layer - 1, 0, 0))
        tiled = pl.BlockSpec((1, T, SUBLANES, LANES), lambda i, b: (b, i, 0, 0))
        in_specs = [tok(D), tiled, prev, prev, mod_spec(5, layer - 1)]
        args = [x_all, moe, ln2_g.reshape(L, 1, D), ln2_b.reshape(L, 1, D), mods]
    else:
        assert ctx_in.shape[1] == n_ctx_blocks * T
        in_specs = [pl.BlockSpec((1, T, D), lambda i, b: (b, jnp.minimum(i, n_ctx_blocks - 1), 0)),
                    pl.BlockSpec((1, T, D), lambda i, b: (b, jnp.maximum(i - n_ctx_blocks, 0), 0))]
        args = [ctx_in, x_in]
    in_specs += [
        mod_spec(0, layer), mod_spec(1, layer),
        pl.BlockSpec((T, 8 * LANES), lambda i, b: (i, 0)),
        const((1, D, IN_COLS_PAD)),
        const((1, 1, MLA_KV_RANK)), const((1, 1, MLA_Q_RANK)),
        const((1, MLA_KV_RANK, 2048)), const((1, MLA_Q_RANK, 1024)),
        pl.BlockSpec((1, 1536), lambda i, b: (0, 0)),
    ]
    args += [mods, mods, tab, w_in_p, g_kvn.reshape(L, 1, -1), g_qn.reshape(L, 1, -1), w_kv, w_q, vones]

    out_specs, out_shape = [tok(D)], [jax.ShapeDtypeStruct((B, S, D), F32)]
    out_specs += [tok(WIN_HEADS * LANES), tok(LANES), tok(512), headed, headed, headed, tok(2048)]
    out_shape += [jax.ShapeDtypeStruct((B, S, WIN_HEADS * LANES), BF16),
                  jax.ShapeDtypeStruct((B, S, LANES), BF16),
                  jax.ShapeDtypeStruct((B, S, 512), BF16)]
    out_shape += [jax.ShapeDtypeStruct((B, S, MLA_HEADS * LANES), BF16)] * 3
    out_shape += [jax.ShapeDtypeStruct((B, S, 2048), BF16)]

    outs = pl.pallas_call(
        functools.partial(_proj_kernel, has_ln, alpha, n_ctx_blocks),
        grid=(S // T, B),
        in_specs=in_specs, out_specs=out_specs, out_shape=out_shape,
        compiler_params=_compiler_params(("parallel", "parallel")),
        name="proj",
    )(*args)
    return outs[0], outs[1:]


WIN_HEAD_ORDER = (0, 2, 1, 3, 4, 6, 5, 7)


def _win_kernel(layer, n_ctx, n_steps, sink_ref, q_ref, kc_ref, vc_ref, kp_ref, km_ref, kn_ref,
                vp_ref, vm_ref, vn_ref, o_ref, kall_ref, vall_ref, p_ref):
    i = pl.program_id(1)
    row = lax.broadcasted_iota(jnp.int32, (BAND, LANES), 0)
    col = lax.broadcasted_iota(jnp.int32, (BAND, LANES), 1)
    lane_lo = col < 64
    kall_ref[0:n_ctx, :] = kc_ref[0]
    vall_ref[0:n_ctx, :] = vc_ref[0]

    def scores(qb, r0, nk):
        q = q_ref[0, qb * BAND:(qb + 1) * BAND, :]
        q8 = jnp.concatenate([q[:, h * LANES:(h + 1) * LANES] for h in WIN_HEAD_ORDER], axis=0)
        return _dot_nt(q8, kall_ref[r0:r0 + nk, :])

    def softmax(qb, nk, masks, s):
        e_sink = []
        for c, h in enumerate(WIN_HEAD_ORDER):
            sc = s[c * BAND:(c + 1) * BAND, :]
            if masks:
                blocks = []
                for kb in range(nk // BAND):
                    blk = sc[:, kb * BAND:(kb + 1) * BAND]
                    blocks.append(jnp.where(masks[kb], blk, NEG_INF) if kb in masks else blk)
                sc = jnp.concatenate(blocks, axis=1)
            sink = sink_ref[layer, h] * LOG2E
            m = jnp.maximum(jnp.max(sc, axis=1, keepdims=True), sink)
            p_ref[qb, c * BAND:(c + 1) * BAND, 0:nk] = jnp.exp2(sc - m).astype(BF16)
            e_sink.append(jnp.exp2(sink - m))
        return e_sink

    def values(qb, r0, nk, e_sink):
        pairs = []
        for pi in range(4):
            acc = _dot(p_ref[qb, pi * 2 * BAND:(pi + 1) * 2 * BAND, 0:nk],
                       vall_ref[r0:r0 + nk, pi * LANES:(pi + 1) * LANES])
            es = jnp.concatenate([e_sink[2 * pi], e_sink[2 * pi + 1]], axis=0)
            den = (acc[:, 64:65] if pi % 2 == 0 else acc[:, 0:1]) + es
            pairs.append(acc / den)
        slabs = [jnp.where(lane_lo, pairs[0][:BAND], pairs[1][:BAND]),
                 jnp.where(lane_lo, pairs[0][BAND:], pairs[1][BAND:]),
                 jnp.where(lane_lo, pairs[2][:BAND], pairs[3][:BAND]),
                 jnp.where(lane_lo, pairs[2][BAND:], pairs[3][BAND:])]
        o_ref[0, qb * BAND:(qb + 1) * BAND, :] = jnp.concatenate(slabs, axis=1).astype(o_ref.dtype)

    def attend_both(args0, args1):
        (r0, nk0, m0), (r1, nk1, m1) = args0, args1
        s0 = scores(0, r0, nk0)
        s1 = scores(1, r1, nk1)
        e0 = softmax(0, nk0, m0, s0)
        e1 = softmax(1, nk1, m1, s1)
        values(0, r0, nk0, e0)
        values(1, r1, nk1, e1)

    @pl.when(i == 0)
    def _():
        attend_both((0, n_ctx, None), (0, n_ctx, None))

    @pl.when(i > 0)
    def _():
        r = n_ctx
        for k_ref, v_ref, n in ((kp_ref, vp_ref, BAND), (km_ref, vm_ref, 2 * BAND),
                                (kn_ref, vn_ref, BAND), (kc_ref, vc_ref, n_ctx)):
            kall_ref[r:r + n, :] = k_ref[0]
            vall_ref[r:r + n, :] = v_ref[0]
            r += n
        off_first = jnp.where(i == 1, 2 * LANES, 0)
        off_last = jnp.where(i == n_steps - 1, 2 * LANES, 0)
        cb = n_ctx // BAND
        attend_both((0, n_ctx + 3 * BAND, {cb: col >= row + off_first, cb + 2: col <= row}),
                    (n_ctx + BAND, n_ctx + 3 * BAND, {0: col >= row, 2: col <= row - off_last}))


def _win_call(layer, sink, qwin, kwin, vwin, n_ctx):
    B, S, _ = qwin.shape
    nb = S // BAND
    ncb = n_ctx // BAND
    assert n_ctx == 2 * BAND and nb % 2 == 0
    nk = n_ctx + 3 * BAND
    one = lambda w, f: pl.BlockSpec((1, BAND, w), lambda b, i: (b, jnp.clip(f(i), ncb, nb - 1), 0))
    two = lambda w: pl.BlockSpec((1, 2 * BAND, w), lambda b, i: (b, i, 0))
    ctx = lambda w: pl.BlockSpec((1, n_ctx, w), lambda b, i: (b, 0, 0))
    prev, nxt = (lambda i: 2 * i - 1), (lambda i: 2 * i + 2)
    return pl.pallas_call(
        functools.partial(_win_kernel, layer, n_ctx, nb // 2),
        grid=(B, nb // 2),
        in_specs=[pl.BlockSpec(memory_space=pltpu.SMEM), two(WIN_HEADS * LANES),
                  ctx(LANES), ctx(512), one(LANES, prev), two(LANES), one(LANES, nxt),
                  one(512, prev), two(512), one(512, nxt)],
        out_specs=two(512),
        out_shape=jax.ShapeDtypeStruct((B, S, 512), BF16),
        scratch_shapes=[pltpu.VMEM((2 * n_ctx + 4 * BAND, LANES), BF16),
                        pltpu.VMEM((2 * n_ctx + 4 * BAND, 512), BF16),
                        pltpu.VMEM((2, WIN_HEADS * BAND, nk), BF16)],
        compiler_params=_compiler_params(("parallel", "parallel")),
        name="win_attn",
    )(sink, qwin, kwin, vwin, kwin, kwin, kwin, vwin, vwin, vwin)


MLA_KCHUNK = 256
MLA_HEADS_PER_STEP = 4


def _mla_kernel(n_ctx, n_keys, q_ref, k_ref, v_ref, o_ref, s_ref):
    qi = pl.program_id(2)
    tq = q_ref.shape[1]
    lane = lax.broadcasted_iota(jnp.int32, (tq, LANES), 1)

    def run(nk):
        row_max = []
        for hh in range(MLA_HEADS_PER_STEP):
            hl = slice(hh * LANES, (hh + 1) * LANES)
            q = q_ref[0, :, hl]
            mrun = None
            for c in range(nk // MLA_KCHUNK):
                s = _dot_nt(q, k_ref[0, c * MLA_KCHUNK:(c + 1) * MLA_KCHUNK, hl])
                s_ref[hh, c] = s
                mc = jnp.maximum(s[:, :LANES], s[:, LANES:])
                mrun = mc if mrun is None else jnp.maximum(mrun, mc)
            row_max.append(jnp.max(mrun, axis=1, keepdims=True))
        outs = []
        for hh in range(MLA_HEADS_PER_STEP):
            acc = jnp.zeros((tq, LANES), F32)
            for c in range(nk // MLA_KCHUNK):
                p = jnp.exp2(s_ref[hh, c] - row_max[hh]).astype(BF16)
                acc = acc + _dot(p, v_ref[0, c * MLA_KCHUNK:(c + 1) * MLA_KCHUNK,
                                          hh * LANES:(hh + 1) * LANES])
            den = acc[:, 64:65] if hh % 2 == 0 else acc[:, 0:1]
            outs.append(acc / den)
        o_ref[0] = jnp.concatenate(
            [jnp.where(lane < 64, outs[2 * j], outs[2 * j + 1]) for j in range(len(outs) // 2)],
            axis=1).astype(o_ref.dtype)

    @pl.when(qi == 0)
    def _():
        run(n_ctx)

    @pl.when(qi > 0)
    def _():
        run(n_keys)


def _mla_call(qcat, kcat, vext, n_ctx):
    B, S, _ = qcat.shape
    H = MLA_HEADS
    T = TOK_BLOCK
    assert n_ctx == T
    G = MLA_HEADS_PER_STEP
    kv = pl.BlockSpec((1, S, G * LANES), lambda b, hp, i: (b, 0, hp))
    return pl.pallas_call(
        functools.partial(_mla_kernel, n_ctx, S),
        grid=(B, H // G, S // T),
        in_specs=[pl.BlockSpec((1, T, G * LANES), lambda b, hp, i: (b, i, hp)), kv, kv],
        out_specs=pl.BlockSpec((1, T, (G // 2) * LANES), lambda b, hp, i: (b, i, hp)),
        out_shape=jax.ShapeDtypeStruct((B, S, (H // 2) * LANES), BF16),
        scratch_shapes=[pltpu.VMEM((G, S // MLA_KCHUNK, T, MLA_KCHUNK), F32)],
        compiler_params=_compiler_params(("parallel", "parallel", "arbitrary")),
        name="mla_attn",
    )(qcat, kcat, vext)


def _merge_kernel(alpha, x_ref, oa_ref, ob_ref, sg_ref, woa_ref, wob_ref, wout_ref, g1_ref, b1_ref,
                  m2a_ref, m3a_ref, m4a_ref, m2b_ref, m3b_ref, m4b_ref, wr_ref,
                  x1_ref, h2_ref, lg_ref):
    T, D = x_ref.shape[1], x_ref.shape[2]
    rows2 = lambda ref: ref[...].reshape(2 * T, ref.shape[2])
    sg = rows2(sg_ref)
    t = (sg[:, :D].astype(F32) * _dot(rows2(oa_ref), woa_ref[0])
         + sg[:, D:].astype(F32) * _dot(rows2(ob_ref), wob_ref[0]))
    y = _dot(t.astype(BF16), wout_ref[0])
    h2s = []
    for n, (m2_ref, m3_ref, m4_ref) in enumerate(((m2a_ref, m3a_ref, m4a_ref),
                                                  (m2b_ref, m3b_ref, m4b_ref))):
        x1 = _layer_norm(alpha * x_ref[n] + m2_ref[0] * y[n * T:(n + 1) * T],
                         g1_ref[0], b1_ref[0])
        x1_ref[n] = x1
        h2 = x1 * (1.0 + m4_ref[0]) + m3_ref[0]
        h2_ref[n] = _rows_to_tiles(h2)
        h2s.append(h2.astype(BF16))
    lg = _dot(jnp.concatenate(h2s, axis=0), wr_ref[0])
    for n in range(2):
        lg_ref[n] = lg[n * T:(n + 1) * T].T[:N_EXPERTS]


def _merge_call(layer, alpha, x_res, oa, ob, sg, w_oa, w_ob, w_out, ln1_g, ln1_b, mods, w_r,
                n_ctx_blocks):
    B, S, D = x_res.shape
    T = TOK_BLOCK
    L = w_oa.shape[0]
    assert B % 2 == 0

    def mod_spec(k, n):
        def imap(bp, i):
            row = jnp.where(i < n_ctx_blocks, B, 2 * bp + n)
            return ((layer * ADA_ROWS + row) * 6 + k, 0, 0)
        return pl.BlockSpec((1, 1, D), imap)

    tok = lambda w: pl.BlockSpec((2, T, w), lambda bp, i: (bp, i, 0))
    const = lambda shp: pl.BlockSpec(shp, lambda bp, i: (layer,) + (0,) * (len(shp) - 1),
                                     pipeline_mode=pl.Buffered(1))
    return pl.pallas_call(
        functools.partial(_merge_kernel, alpha),
        grid=(B // 2, S // T),
        in_specs=[tok(D), tok(512), tok(512), tok(2048),
                  const((1, 512, D)), const((1, 512, D)), const((1, D, D)),
                  const((1, 1, D)), const((1, 1, D)),
                  mod_spec(2, 0), mod_spec(3, 0), mod_spec(4, 0),
                  mod_spec(2, 1), mod_spec(3, 1), mod_spec(4, 1),
                  const((1, D, LANES))],
        out_specs=[tok(D), pl.BlockSpec((2, T, SUBLANES, LANES), lambda bp, i: (bp, i, 0, 0)),
                   pl.BlockSpec((2, N_EXPERTS, T), lambda bp, i: (bp, 0, i))],
        out_shape=[jax.ShapeDtypeStruct((B, S, D), F32),
                   jax.ShapeDtypeStruct((B, S, SUBLANES, LANES), F32),
                   jax.ShapeDtypeStruct((B, N_EXPERTS, S), F32)],
        compiler_params=_compiler_params(("parallel", "parallel")),
        name="merge",
    )(x_res, oa, ob, sg, w_oa, w_ob, w_out, ln1_g.reshape(L, 1, D), ln1_b.reshape(L, 1, D),
      mods, mods, mods, mods, mods, mods, w_r)


def _cumsum_lanes(x):
    n = x.shape[1]
    xb = x.astype(BF16)
    r = lax.broadcasted_iota(jnp.int32, (LANES, LANES), 0)
    c = lax.broadcasted_iota(jnp.int32, (LANES, LANES), 1)
    tri = jnp.where(r <= c, 1.0, 0.0).astype(BF16)
    tb = lax.broadcasted_iota(jnp.int32, (n, LANES), 0) // LANES
    kb = lax.broadcasted_iota(jnp.int32, (n, LANES), 1)
    before = jnp.where(tb < kb, 1.0, 0.0).astype(BF16)
    off = _dot(xb, before)
    outs = []
    for k in range(n // LANES):
        outs.append(_dot(xb[:, k * LANES:(k + 1) * LANES], tri) + off[:, k:k + 1])
    return outs


def _select_top(aff, cap):
    bits = pltpu.bitcast(aff, jnp.int32)
    thr = jnp.zeros((aff.shape[0], 1), jnp.int32)
    for bit in range(30, -1, -1):
        cand = thr | (1 << bit)
        cnt = jnp.sum(jnp.where(bits >= cand, 1.0, 0.0), axis=1, keepdims=True)
        thr = jnp.where(cnt >= cap, cand, thr)
    gt = bits > thr
    eq = jnp.where(bits == thr, 1.0, 0.0)
    need = cap - jnp.sum(jnp.where(gt, 1.0, 0.0), axis=1, keepdims=True)
    eq_rank = jnp.concatenate(_cumsum_lanes(eq), axis=1) - eq
    sel = jnp.where(jnp.logical_or(gt, jnp.logical_and(eq > 0.5, eq_rank < need)), 1.0, 0.0)
    return _cumsum_lanes(sel)


ROUTE_PAD = 1e6


def _route_kernel(n_ctx, cap_c, cap_l, lg_ref, aff_ref, idx_ref, cc_ref, cl_ref):
    lg = lg_ref[0]
    m = jnp.max(lg, axis=0, keepdims=True)
    ex = jnp.exp(lg - m)
    aff = ex / jnp.sum(ex, axis=0, keepdims=True)
    aff_ref[0] = aff
    for c_ref, blocks in ((cc_ref, _select_top(aff[:, :n_ctx], cap_c)),
                          (cl_ref, _select_top(aff[:, n_ctx:], cap_l))):
        c_ref[...] = jnp.full(c_ref.shape, ROUTE_PAD, F32)
        for k, blk in enumerate(blocks):
            for ex_i in range(N_EXPERTS):
                c_ref[ex_i, k:k + 1, :] = blk[ex_i:ex_i + 1, :]

    def slots(c_ref, e, n_blocks, width):
        cm = c_ref[e]
        jrow = lax.broadcasted_iota(jnp.int32, (1, width), 1).astype(F32)
        rows = max(n_blocks, SUBLANES)
        cend = cm[0:rows, LANES - 1:LANES]
        blk = jnp.sum(jnp.where(cend <= jrow, 1.0, 0.0), axis=0, keepdims=True)
        kio = lax.broadcasted_iota(jnp.int32, (LANES, width), 0).astype(F32)
        onehot = jnp.where(kio == blk, 1.0, 0.0).astype(BF16)
        cmt = cm.T
        hi = jnp.floor(cmt * (1.0 / 256.0))
        lo = cmt - 256.0 * hi
        cg = 256.0 * _dot(hi.astype(BF16), onehot) + _dot(lo.astype(BF16), onehot)
        inside = jnp.sum(jnp.where(cg <= jrow, 1.0, 0.0), axis=0, keepdims=True)
        return (blk * LANES + inside).astype(jnp.int32)

    def per_expert(e, carry):
        n_lat_blocks = (lg_ref.shape[2] - n_ctx) // LANES
        idx_ref[0, e, :, 0:cap_l] = slots(cl_ref, e, n_lat_blocks, cap_l) + n_ctx
        idx_ref[0, e, :, cap_l:cap_l + cap_c] = slots(cc_ref, e, n_ctx // LANES, LANES)[:, 0:cap_c]
        return carry

    lax.fori_loop(0, N_EXPERTS, per_expert, 0)


def _route_call(lg_t, n_ctx, cap_c, cap_l):
    B, E, S = lg_t.shape
    cap = cap_c + cap_l
    assert (S - n_ctx) // LANES <= LANES and cap_l % LANES == 0 and cap_c <= LANES
    return pl.pallas_call(
        functools.partial(_route_kernel, n_ctx, cap_c, cap_l),
        grid=(B,),
        in_specs=[pl.BlockSpec((1, E, S), lambda b: (b, 0, 0))],
        out_specs=[pl.BlockSpec((1, E, S), lambda b: (b, 0, 0)),
                   pl.BlockSpec((1, E, 1, cap), lambda b: (b, 0, 0, 0))],
        out_shape=[jax.ShapeDtypeStruct((B, E, S), F32),
                   jax.ShapeDtypeStruct((B, E, 1, cap), jnp.int32)],
        scratch_shapes=[pltpu.VMEM((E, LANES, LANES), F32), pltpu.VMEM((E, LANES, LANES), F32)],
        compiler_params=_compiler_params(("parallel",)),
        name="route",
    )(lg_t)


ROW_BATCH = 16
MOE_FF_CHUNK = 512


GATHER_EXPERTS = 2


def _gather_kernel(idx_ref, h_ref, xg_ref, tmp_ref):
    cap = xg_ref.shape[2]
    for ge in range(GATHER_EXPERTS):
        def body(g, carry):
            for r in range(ROW_BATCH):
                j = g * ROW_BATCH + r
                tmp_ref[j] = h_ref[0, idx_ref[0, 0, ge * cap + j]]
            return carry

        lax.fori_loop(0, cap // ROW_BATCH, body, 0)
        xg_ref[0, ge] = _tiles_to_rows(tmp_ref[...]).astype(xg_ref.dtype)


def _gather_call(idx, h2t):
    B, S = h2t.shape[0], h2t.shape[1]
    D = SUBLANES * LANES
    E, cap = idx.shape[1], idx.shape[2]
    G = GATHER_EXPERTS
    assert cap % ROW_BATCH == 0 and E % G == 0
    return pl.pallas_call(
        _gather_kernel,
        grid=(B, E // G),
        in_specs=[pl.BlockSpec((1, 1, G * cap), lambda b, e: (b * (E // G) + e, 0, 0),
                               memory_space=pltpu.SMEM),
                  pl.BlockSpec((1, S, SUBLANES, LANES), lambda b, e: (b, 0, 0, 0))],
        out_specs=pl.BlockSpec((1, G, cap, D), lambda b, e: (b, e, 0, 0)),
        out_shape=jax.ShapeDtypeStruct((B, E, cap, D), BF16),
        scratch_shapes=[pltpu.VMEM((cap, SUBLANES, LANES), F32)],
        compiler_params=_compiler_params(("parallel", "arbitrary")),
        name="moe_gather",
    )(idx.reshape(B * E // G, 1, G * cap), h2t)


def _moe_kernel(idx_ref, aff_ref, xg_ref, wg_ref, wu_ref, wd_ref, out_ref, y_ref):
    s = pl.program_id(1)
    n_exp = pl.num_programs(1) - 1
    cap = y_ref.shape[1]
    n_batches = cap // ROW_BATCH

    def scatter(slot, batches):
        new = None
        for g in batches:
            rows = _rows_to_tiles(y_ref[slot, g * ROW_BATCH:(g + 1) * ROW_BATCH, :])
            toks = [idx_ref[0, 0, g * ROW_BATCH + r] for r in range(ROW_BATCH)]
            old = [out_ref[0, t] for t in toks]
            for r, t in enumerate(toks):
                new = old[r] + rows[r] * aff_ref[0, 0, t]
                out_ref[0, t] = new
        return new

    def ffn(slot, prev_slot):
        ff = wg_ref.shape[3]
        n_dots = 3 * (ff // MOE_FF_CHUNK)
        per = -(-n_batches // n_dots)
        todo = list(range(n_batches)) if prev_slot is not None else []

        def weights(w):
            take, todo[:] = todo[:per], todo[per:]
            w = w.astype(BF16)
            if not take:
                return w
            last = scatter(prev_slot, take)
            bits = pltpu.bitcast(last, jnp.uint32)
            zero = pltpu.bitcast((bits >> 16) >> 16, F32)
            zero = jnp.concatenate([zero, zero], axis=0).astype(BF16)
            return w + jnp.tile(zero, (w.shape[0] // zero.shape[0], w.shape[1] // LANES))

        x = xg_ref[0, 0]
        y = None
        for f0 in range(0, ff, MOE_FF_CHUNK):
            f1 = f0 + MOE_FF_CHUNK
            a = _dot(x, weights(wg_ref[0, 0, :, f0:f1]))
            u = _dot(x, weights(wu_ref[0, 0, :, f0:f1]))
            hmid = (a * jax.nn.sigmoid(a) * u).astype(BF16)
            part = _dot(hmid, weights(wd_ref[0, 0, f0:f1, :]))
            y = part if y is None else y + part
        y_ref[slot] = y

    @pl.when(s == 0)
    def _():
        out_ref[...] = jnp.zeros(out_ref.shape, out_ref.dtype)
        ffn(0, None)

    @pl.when(jnp.logical_and(s > 0, s < n_exp))
    def _():
        slot = s % 2
        ffn(slot, 1 - slot)

    @pl.when(s == n_exp)
    def _():
        scatter((n_exp - 1) % 2, range(n_batches))


def _moe_call(layer, idx, aff, xg, w_gate, w_up, w_down, S):
    B, E, cap, D = xg.shape
    F = w_gate.shape[3]
    cur = lambda s: jnp.minimum(s, E - 1)
    prev = lambda s: jnp.maximum(s - 1, 0)
    wspec = lambda shp: pl.BlockSpec(shp, lambda b, s: (layer, cur(s), 0, 0))
    return pl.pallas_call(
        _moe_kernel,
        grid=(B, E + 1),
        in_specs=[pl.BlockSpec((1, 1, cap), lambda b, s: (b * E + prev(s), 0, 0), memory_space=pltpu.SMEM),
                  pl.BlockSpec((1, 1, S), lambda b, s: (b * E + prev(s), 0, 0), memory_space=pltpu.SMEM),
                  pl.BlockSpec((1, 1, cap, D), lambda b, s: (b, cur(s), 0, 0)),
                  wspec((1, 1, D, F)), wspec((1, 1, D, F)), wspec((1, 1, F, D))],
        out_specs=pl.BlockSpec((1, S, SUBLANES, LANES), lambda b, s: (b, 0, 0, 0),
                               pipeline_mode=pl.Buffered(1)),
        out_shape=jax.ShapeDtypeStruct((B, S, SUBLANES, LANES), F32),
        scratch_shapes=[pltpu.VMEM((2, cap, D), F32)],
        compiler_params=_compiler_params(("parallel", "arbitrary")),
        name="moe_ffn",
    )(idx.reshape(B * E, 1, cap), aff.reshape(B * E, 1, S), xg, w_gate, w_up, w_down)


def _final_kernel(alpha, x_ref, moe_ref, g_ref, b_ref, m5_ref, o_ref):
    o_ref[0] = _layer_norm(alpha * x_ref[0] + m5_ref[0] * _tiles_to_rows(moe_ref[0]),
                           g_ref[0], b_ref[0])


def _final_call(layer, alpha, x1, moe, ln2_g, ln2_b, mods, n_ctx):
    B, S, D = x1.shape
    T = TOK_BLOCK
    L = ln2_g.shape[0]
    ncb = n_ctx // T
    tok_in = pl.BlockSpec((1, T, D), lambda b, i: (b, i + ncb, 0))
    const = pl.BlockSpec((1, 1, D), lambda b, i: (layer, 0, 0))
    return pl.pallas_call(
        functools.partial(_final_kernel, alpha),
        grid=(B, (S - n_ctx) // T),
        in_specs=[tok_in, pl.BlockSpec((1, T, SUBLANES, LANES), lambda b, i: (b, i + ncb, 0, 0)),
                  const, const,
                  pl.BlockSpec((1, 1, D), lambda b, i: ((layer * ADA_ROWS + b) * 6 + 5, 0, 0))],
        out_specs=pl.BlockSpec((1, T, D), lambda b, i: (b, i, 0)),
        out_shape=jax.ShapeDtypeStruct((B, S - n_ctx, D), F32),
        compiler_params=_compiler_params(("parallel", "parallel")),
        name="final_norm",
    )(x1, moe, ln2_g.reshape(L, 1, D), ln2_b.reshape(L, 1, D), mods)


def _prep_weights(w_in, w_uq, w_ukv, w_router):
    L, D, _ = w_in.shape
    krs = jnp.concatenate([jnp.zeros((L, D, 64), w_in.dtype), w_in[..., 512:544],
                           jnp.zeros((L, D, 32), w_in.dtype)], axis=-1)
    w_in_p = jnp.concatenate([w_in[..., 0:512], krs, w_in[..., 544:]], axis=-1).astype(BF16)

    kvr = w_ukv.reshape(L, MLA_KV_RANK, MLA_HEADS, MLA_NOPE + MLA_V)
    kn, vv = kvr[..., :MLA_NOPE], kvr[..., MLA_NOPE:]
    zk = jnp.zeros_like(kn)
    w_k = jnp.concatenate([kn, zk], axis=-1).reshape(L, MLA_KV_RANK, MLA_HEADS * LANES)
    v_even = jnp.concatenate([vv, zk], axis=-1)
    v_odd = jnp.concatenate([zk, vv], axis=-1)
    odd = (jnp.arange(MLA_HEADS) % 2 == 1)[None, None, :, None]
    w_v = jnp.where(odd, v_odd, v_even).reshape(L, MLA_KV_RANK, MLA_HEADS * LANES)
    w_kv = jnp.concatenate([w_k, w_v], axis=-1).astype(BF16)

    qr = w_uq.reshape(L, MLA_Q_RANK, MLA_HEADS, MLA_NOPE + MLA_ROPE)
    w_q = jnp.concatenate([qr, jnp.zeros((L, MLA_Q_RANK, MLA_HEADS, 32), w_uq.dtype)], axis=-1)
    w_q = w_q.reshape(L, MLA_Q_RANK, MLA_HEADS * LANES).astype(BF16)

    w_r = jnp.concatenate(
        [w_router, jnp.zeros((L, D, LANES - N_EXPERTS), w_router.dtype)], axis=-1).astype(BF16)
    return w_in_p, w_kv, w_q, w_r


def _ones_columns():
    lane = jnp.arange(LANES)
    even = (lane == 64).astype(F32)
    odd = (lane == 0).astype(F32)
    win = jnp.concatenate([even, odd, even, odd])
    mla = jnp.concatenate([even, odd] * (MLA_HEADS // 2))
    return jnp.concatenate([win, mla])[None, :]


def _rope_tables(n_ctx, n_lat):
    pos = jnp.arange(n_lat)
    rowp = (pos // GRID_W).astype(F32)
    colp = (pos % GRID_W).astype(F32)

    def pattern(rot_dim):
        nf = rot_dim // 4
        inv = ROPE_THETA ** (-jnp.arange(nf, dtype=F32) / nf)
        ar, ac = rowp[:, None] * inv, colp[:, None] * inv
        cos = jnp.concatenate([jnp.cos(ar), jnp.cos(ar), jnp.cos(ac), jnp.cos(ac)], axis=1)
        sin = jnp.concatenate([-jnp.sin(ar), jnp.sin(ar), -jnp.sin(ac), jnp.sin(ac)], axis=1)
        return cos, sin

    cw, sw = pattern(WIN_HEAD_DIM)
    cw, sw = jnp.tile(cw, (1, 2)), jnp.tile(sw, (1, 2))
    cm, sm = pattern(MLA_ROPE)
    one64, zero64 = jnp.ones((n_lat, 64), F32), jnp.zeros((n_lat, 64), F32)
    cm = jnp.concatenate([one64, cm, one64[:, :32]], axis=1)
    sm = jnp.concatenate([zero64, sm, zero64[:, :32]], axis=1)
    sq_w, sq_m = WIN_SCALE * LOG2E, MLA_SCALE * LOG2E
    lat = jnp.concatenate([cw * sq_w, sw * sq_w, cw, sw, cm * sq_m, sm * sq_m, cm, sm], axis=1)
    ones, zeros = jnp.ones((n_ctx, LANES), F32), jnp.zeros((n_ctx, LANES), F32)
    ctx = jnp.concatenate([ones * sq_w, zeros, ones, zeros, ones * sq_m, zeros, ones, zeros], axis=1)
    return jnp.concatenate([ctx, lat], axis=0)


def kernel(x, c, ctx, c_ctx, w_ada, b_ada, w_in, attn_sink, mla_q_norm, mla_kv_norm, w_uq, w_ukv,
           w_oa, w_ob, w_out, ln1_g, ln1_b, w_router, w_exp_gate, w_exp_up, w_exp_down, ln2_g, ln2_b):
    B, n_lat, D = x.shape
    n_ctx = ctx.shape[1]
    depth = w_in.shape[0]
    S = n_ctx + n_lat
    alpha = (2 * depth) ** 0.25
    assert D == D_MODEL and n_ctx == TOK_BLOCK and n_lat % TOK_BLOCK == 0 and B + 1 <= ADA_ROWS
    cap_c = CAPACITY_FACTOR * n_ctx // N_EXPERTS
    cap_l = CAPACITY_FACTOR * n_lat // N_EXPERTS
    assert cap_l % 256 == 0 and cap_c % 8 == 0
    ncb = n_ctx // TOK_BLOCK

    cond = jnp.concatenate([c, c_ctx[None], jnp.zeros((ADA_ROWS - B - 1, D), F32)], axis=0)
    mods = _ada_call(cond.T, w_ada, b_ada, B + 1).reshape(depth * ADA_ROWS * 6, 1, D)

    w_in_p, w_kv, w_q, w_r = _prep_weights(w_in, w_uq, w_ukv, w_router)
    w_oa_b, w_ob_b, w_out_b = w_oa.astype(BF16), w_ob.astype(BF16), w_out.astype(BF16)
    tab = _rope_tables(n_ctx, n_lat)
    vones = _ones_columns()

    x_all = (ctx, x)
    moe = None
    for l in range(depth):
        x_res, (qwin, kwin, vwin, qcat, kcat, vext, sg) = _proj_call(
            l, alpha, x_all, moe, ln2_g, ln2_b, mods, tab, w_in_p, mla_kv_norm, mla_q_norm,
            w_kv, w_q, vones, ncb)
        oa = _win_call(l, attn_sink, qwin, kwin, vwin, n_ctx)
        ob = _mla_call(qcat, kcat, vext, n_ctx)
        x1, h2, lg_t = _merge_call(l, alpha, x_res, oa, ob, sg, w_oa_b, w_ob_b, w_out_b,
                                   ln1_g, ln1_b, mods, w_r, ncb)
        aff, idx4 = _route_call(lg_t, n_ctx, cap_c, cap_l)
        idx = idx4.reshape(B, N_EXPERTS, cap_c + cap_l)
        xg = _gather_call(idx, h2)
        moe = _moe_call(l, idx, aff, xg, w_exp_gate, w_exp_up, w_exp_down, S)
        x_all = x1
    return _final_call(depth - 1, alpha, x_all, moe, ln2_g, ln2_b, mods, n_ctx)
```

```python
import functools
import math

import jax
import jax.numpy as jnp
from jax import lax
from jax.experimental import pallas as pl
from jax.experimental.pallas import tpu as pltpu

D_MODEL = 1024
GRID_W = 64
WIN_HEADS = 8
WIN_KV_HEADS = 2
WIN_HEAD_DIM = 64
BAND = 128
MLA_HEADS = 8
MLA_Q_RANK = 384
MLA_KV_RANK = 256
MLA_NOPE = 64
MLA_ROPE = 32
MLA_V = 64
N_EXPERTS = 16
CAPACITY_FACTOR = 2
ROPE_THETA = 10000.0
LN_EPS = 1e-5
RMS_EPS = 1e-6
NEG_INF = -1e30
LOG2E = math.log2(math.e)
WIN_SCALE = WIN_HEAD_DIM ** -0.5
MLA_SCALE = (MLA_NOPE + MLA_ROPE) ** -0.5

LANES = 128
TOK_BLOCK = 256
VMEM_LIMIT = 56 * 1024 * 1024

BF16 = jnp.bfloat16
F32 = jnp.float32

SEG_KWIN = (0, 128)
SEG_VWIN = (128, 256)
SEG_CKV = (256, 512)
SEG_KR = (512, 640)
SEG_QWIN = (640, 1152)
SEG_CQ = (1152, 1536)
SEG_GATE = (1536, 3584)
IN_COLS_PAD = 3584


def _dot(a, b):
    return jnp.dot(a, b, preferred_element_type=F32)


def _dot_nt(a, b):
    return lax.dot_general(a, b, (((1,), (1,)), ((), ())), preferred_element_type=F32)


def _layer_norm(z, g, b):
    mu = jnp.mean(z, axis=-1, keepdims=True)
    zc = z - mu
    var = jnp.mean(zc * zc, axis=-1, keepdims=True)
    return zc * lax.rsqrt(var + LN_EPS) * g + b


def _rms_norm(x, g):
    return x * lax.rsqrt(jnp.mean(x * x, axis=-1, keepdims=True) + RMS_EPS) * g


def _rope_slab(x, cos, sin, half):
    lane = lax.broadcasted_iota(jnp.int32, x.shape, 1)
    partner = jnp.where((lane & half) == 0,
                        pltpu.roll(x, LANES - half, 1), pltpu.roll(x, half, 1))
    return x * cos + partner * sin


SUBLANES = 8


def _rows_to_tiles(y):
    n = y.shape[0]
    y3 = pltpu.einshape("r(sl)->srl", y, s=SUBLANES)
    y4 = y3.reshape(SUBLANES, n // SUBLANES, SUBLANES, LANES)
    return jnp.transpose(y4, (1, 2, 0, 3)).reshape(n, SUBLANES, LANES)


def _tiles_to_rows(x3):
    n = x3.shape[0]
    x4 = x3.reshape(n // SUBLANES, SUBLANES, SUBLANES, LANES)
    xs = jnp.transpose(x4, (2, 0, 1, 3)).reshape(SUBLANES, n, LANES)
    return pltpu.einshape("srl->r(sl)", xs)


def _compiler_params(sem):
    return pltpu.CompilerParams(dimension_semantics=sem, vmem_limit_bytes=VMEM_LIMIT)


ADA_ROWS = 8
ADA_TN = 512


def _ada_kernel(n_rows, condt_ref, w_ref, b_ref, o_ref):
    ct = condt_ref[...]
    st = ct * jax.nn.sigmoid(ct)
    w = w_ref[0]
    rows = []
    for r in range(n_rows):
        rows.append(jnp.sum(w * st[:, r:r + 1], axis=0, keepdims=True) + b_ref[0])
    rows.append(jnp.zeros((ADA_ROWS - n_rows, w.shape[1]), F32))
    o_ref[0] = jnp.concatenate(rows, axis=0)


def _ada_call(cond_t, w_ada, b_ada, n_rows):
    L, D, N = w_ada.shape
    return pl.pallas_call(
        functools.partial(_ada_kernel, n_rows),
        grid=(L, N // ADA_TN),
        in_specs=[
            pl.BlockSpec((D, ADA_ROWS), lambda l, j: (0, 0)),
            pl.BlockSpec((1, D, ADA_TN), lambda l, j: (l, 0, j)),
            pl.BlockSpec((1, 1, ADA_TN), lambda l, j: (l, 0, j)),
        ],
        out_specs=pl.BlockSpec((1, ADA_ROWS, ADA_TN), lambda l, j: (l, 0, j)),
        out_shape=jax.ShapeDtypeStruct((L, ADA_ROWS, N), F32),
        compiler_params=_compiler_params(("parallel", "parallel")),
        name="adaln_mod",
    )(cond_t, w_ada, b_ada.reshape(L, 1, N))


def _proj_kernel(has_ln, alpha, n_ctx_blocks, ns, *refs):
    refs = list(refs)
    if has_ln:
        x_ref, moe_ref, g2_ref, b2_ref = refs[:4]
        m5_refs, refs = refs[4:4 + ns], refs[4 + ns:]
    else:
        (ctx_ref, x_ref), refs = refs[:2], refs[2:]
    mod_refs, refs = refs[:2 * ns], refs[2 * ns:]
    (tab_ref, win_ref, gkv_ref, gq_ref, wkv_ref, wq_ref, vones_ref,
     xres_ref, qwin_ref, kwin_ref, vwin_ref, qcat_ref, kcat_ref, vext_ref, sg_ref) = refs
    T = x_ref.shape[1]
    hbs = []
    for n in range(ns):
        if has_ln:
            z = alpha * x_ref[n] + m5_refs[n][0] * _tiles_to_rows(moe_ref[n])
            x = _layer_norm(z, g2_ref[0], b2_ref[0])
        else:
            x = jnp.where(pl.program_id(0) < n_ctx_blocks, ctx_ref[n], x_ref[n])
        xres_ref[n] = x
        sh_ref, sc_ref = mod_refs[2 * n:2 * n + 2]
        hbs.append((x * (1.0 + sc_ref[0]) + sh_ref[0]).astype(BF16))
    hb = jnp.concatenate(hbs, axis=0)
    split = lambda v: v.reshape(ns, T, v.shape[1])

    def seg(s):
        return _dot(hb, win_ref[0, :, s[0]:s[1]])

    def tab(i):
        t = tab_ref[:, i * LANES:(i + 1) * LANES]
        return jnp.concatenate([t] * ns, axis=0)

    cq_w, sq_w, ck_w, sk_w, cq_m, sq_m, ck_m, sk_m = (tab(i) for i in range(8))

    lane_lo = lax.broadcasted_iota(jnp.int32, (hb.shape[0], LANES), 1) < WIN_HEAD_DIM
    half_swap = lambda t: pltpu.roll(t, WIN_HEAD_DIM, 1)

    ckv = _rms_norm(seg(SEG_CKV), gkv_ref[0]).astype(BF16)
    cq = _rms_norm(seg(SEG_CQ), gq_ref[0]).astype(BF16)

    qw = seg(SEG_QWIN)
    q_slabs = []
    for s2 in range(WIN_HEADS // 2):
        pair = _rope_slab(qw[:, s2 * LANES:(s2 + 1) * LANES], cq_w, sq_w, 16)
        pair_sw = half_swap(pair)
        if 2 * s2 < WIN_HEADS // WIN_KV_HEADS:
            q_slabs += [jnp.where(lane_lo, pair, 0.0), jnp.where(lane_lo, pair_sw, 0.0)]
        else:
            q_slabs += [jnp.where(lane_lo, 0.0, pair_sw), jnp.where(lane_lo, 0.0, pair)]
    qwin_ref[...] = split(jnp.concatenate(q_slabs, axis=1).astype(BF16))

    kv = _dot(ckv, wkv_ref[0])
    kr = _rope_slab(seg(SEG_KR), ck_m, sk_m, 8)
    qm = _dot(cq, wq_ref[0])
    for h in range(MLA_HEADS):
        sl = slice(h * LANES, (h + 1) * LANES)
        kcat_ref[:, :, sl] = split((kv[:, sl] + kr).astype(BF16))
        vext_ref[:, :, sl] = split(
            (kv[:, 1024 + h * LANES:1024 + (h + 1) * LANES]
             + vones_ref[:, 512 + h * LANES:512 + (h + 1) * LANES]).astype(BF16))
        qcat_ref[:, :, sl] = split(_rope_slab(qm[:, sl], cq_m, sq_m, 8).astype(BF16))

    sg_ref[...] = split(jax.nn.sigmoid(seg(SEG_GATE)).astype(BF16))
    va = seg(SEG_VWIN)
    va_sw = half_swap(va)
    vwin_ref[...] = split((jnp.concatenate(
        [jnp.where(lane_lo, va, 0.0), jnp.where(lane_lo, 0.0, va_sw),
         jnp.where(lane_lo, va_sw, 0.0), jnp.where(lane_lo, 0.0, va)], axis=1)
        + vones_ref[:, 0:512]).astype(BF16))
    kwin_ref[...] = split(_rope_slab(seg(SEG_KWIN), ck_w, sk_w, 16).astype(BF16))


def _proj_call(layer, alpha, x_all, moe, ln2_g, ln2_b, mods, tab, w_in_p, g_kvn, g_qn, w_kv, w_q,
               vones, n_ctx_blocks):
    T = TOK_BLOCK
    has_ln = moe is not None
    if has_ln:
        B, S, D = x_all.shape
    else:
        ctx_in, x_in = x_all
        B, D = x_in.shape[0], x_in.shape[2]
        S = ctx_in.shape[1] + x_in.shape[1]
    L = w_in_p.shape[0]
    ns = PROJ_SAMPLES if B % PROJ_SAMPLES == 0 else 1

    def mod_spec(k, lyr, n):
        def imap(i, b):
            row = jnp.where(i < n_ctx_blocks, B, ns * b + n)
            return ((lyr * ADA_ROWS + row) * 6 + k, 0, 0)
        return pl.BlockSpec((1, 1, D), imap)

    tok = lambda w: pl.BlockSpec((ns, T, w), lambda i, b: (b, i, 0))
    headed = tok(MLA_HEADS * LANES)
    const = lambda shp: pl.BlockSpec(shp, lambda i, b: (layer,) + (0,) * (len(shp) - 1),
                                     pipeline_mode=pl.Buffered(1))

    if has_ln:
        prev = pl.BlockSpec((1, 1, D), lambda i, b: (layer - 1, 0, 0))
        tiled = pl.BlockSpec((ns, T, SUBLANES, LANES), lambda i, b: (b, i, 0, 0))
        in_specs = [tok(D), tiled, prev, prev] + [mod_spec(5, layer - 1, n) for n in range(ns)]
        args = [x_all, moe, ln2_g.reshape(L, 1, D), ln2_b.reshape(L, 1, D)] + [mods] * ns
    else:
        assert ctx_in.shape[1] == n_ctx_blocks * T
        in_specs = [pl.BlockSpec((ns, T, D), lambda i, b: (b, jnp.minimum(i, n_ctx_blocks - 1), 0)),
                    pl.BlockSpec((ns, T, D), lambda i, b: (b, jnp.maximum(i - n_ctx_blocks, 0), 0))]
        args = [ctx_in, x_in]
    in_specs += [mod_spec(k, layer, n) for n in range(ns) for k in (0, 1)]
    args += [mods] * (2 * ns)
    in_specs += [
        pl.BlockSpec((T, 8 * LANES), lambda i, b: (i, 0)),
        const((1, D, IN_COLS_PAD)),
        const((1, 1, MLA_KV_RANK)), const((1, 1, MLA_Q_RANK)),
        const((1, MLA_KV_RANK, 2048)), const((1, MLA_Q_RANK, 1024)),
        pl.BlockSpec((1, 1536), lambda i, b: (0, 0)),
    ]
    args += [tab, w_in_p, g_kvn.reshape(L, 1, -1), g_qn.reshape(L, 1, -1), w_kv, w_q, vones]

    out_specs, out_shape = [tok(D)], [jax.ShapeDtypeStruct((B, S, D), F32)]
    out_specs += [tok(WIN_HEADS * LANES), tok(LANES), tok(512), headed, headed, headed, tok(2048)]
    out_shape += [jax.ShapeDtypeStruct((B, S, WIN_HEADS * LANES), BF16),
                  jax.ShapeDtypeStruct((B, S, LANES), BF16),
                  jax.ShapeDtypeStruct((B, S, 512), BF16)]
    out_shape += [jax.ShapeDtypeStruct((B, S, MLA_HEADS * LANES), BF16)] * 3
    out_shape += [jax.ShapeDtypeStruct((B, S, 2048), BF16)]

    outs = pl.pallas_call(
        functools.partial(_proj_kernel, has_ln, alpha, n_ctx_blocks, ns),
        grid=(S // T, B // ns),
        in_specs=in_specs, out_specs=out_specs, out_shape=out_shape,
        compiler_params=_compiler_params(("parallel", "parallel")),
        name="proj",
    )(*args)
    return outs[0], outs[1:]


WIN_HEAD_ORDER = (0, 2, 1, 3, 4, 6, 5, 7)


def _win_kernel(layer, n_ctx, n_steps, sink_ref, q_ref, kc_ref, vc_ref, kp_ref, km_ref, kn_ref,
                vp_ref, vm_ref, vn_ref, o_ref, kall_ref, vall_ref, p_ref):
    i = pl.program_id(1)
    row = lax.broadcasted_iota(jnp.int32, (BAND, LANES), 0)
    col = lax.broadcasted_iota(jnp.int32, (BAND, LANES), 1)
    lane_lo = col < 64
    kall_ref[0:n_ctx, :] = kc_ref[0]
    vall_ref[0:n_ctx, :] = vc_ref[0]

    def scores(qb, r0, nk):
        q = q_ref[0, qb * BAND:(qb + 1) * BAND, :]
        q8 = jnp.concatenate([q[:, h * LANES:(h + 1) * LANES] for h in WIN_HEAD_ORDER], axis=0)
        return _dot_nt(q8, kall_ref[r0:r0 + nk, :])

    def softmax(qb, nk, masks, s):
        e_sink = []
        for c, h in enumerate(WIN_HEAD_ORDER):
            sc = s[c * BAND:(c + 1) * BAND, :]
            if masks:
                blocks = []
                for kb in range(nk // BAND):
                    blk = sc[:, kb * BAND:(kb + 1) * BAND]
                    blocks.append(jnp.where(masks[kb], blk, NEG_INF) if kb in masks else blk)
                sc = jnp.concatenate(blocks, axis=1)
            sink = sink_ref[layer, h] * LOG2E
            m = jnp.maximum(jnp.max(sc, axis=1, keepdims=True), sink)
            p_ref[qb, c * BAND:(c + 1) * BAND, 0:nk] = jnp.exp2(sc - m).astype(BF16)
            e_sink.append(jnp.exp2(sink - m))
        return e_sink

    def values(qb, r0, nk, e_sink):
        pairs = []
        for pi in range(4):
            acc = _dot(p_ref[qb, pi * 2 * BAND:(pi + 1) * 2 * BAND, 0:nk],
                       vall_ref[r0:r0 + nk, pi * LANES:(pi + 1) * LANES])
            es = jnp.concatenate([e_sink[2 * pi], e_sink[2 * pi + 1]], axis=0)
            den = (acc[:, 64:65] if pi % 2 == 0 else acc[:, 0:1]) + es
            pairs.append(acc / den)
        slabs = [jnp.where(lane_lo, pairs[0][:BAND], pairs[1][:BAND]),
                 jnp.where(lane_lo, pairs[0][BAND:], pairs[1][BAND:]),
                 jnp.where(lane_lo, pairs[2][:BAND], pairs[3][:BAND]),
                 jnp.where(lane_lo, pairs[2][BAND:], pairs[3][BAND:])]
        o_ref[0, qb * BAND:(qb + 1) * BAND, :] = jnp.concatenate(slabs, axis=1).astype(o_ref.dtype)

    def attend_both(args0, args1):
        (r0, nk0, m0), (r1, nk1, m1) = args0, args1
        s0 = scores(0, r0, nk0)
        s1 = scores(1, r1, nk1)
        e0 = softmax(0, nk0, m0, s0)
        e1 = softmax(1, nk1, m1, s1)
        values(0, r0, nk0, e0)
        values(1, r1, nk1, e1)

    @pl.when(i == 0)
    def _():
        attend_both((0, n_ctx, None), (0, n_ctx, None))

    @pl.when(i > 0)
    def _():
        r = n_ctx
        for k_ref, v_ref, n in ((kp_ref, vp_ref, BAND), (km_ref, vm_ref, 2 * BAND),
                                (kn_ref, vn_ref, BAND), (kc_ref, vc_ref, n_ctx)):
            kall_ref[r:r + n, :] = k_ref[0]
            vall_ref[r:r + n, :] = v_ref[0]
            r += n
        off_first = jnp.where(i == 1, 2 * LANES, 0)
        off_last = jnp.where(i == n_steps - 1, 2 * LANES, 0)
        cb = n_ctx // BAND
        attend_both((0, n_ctx + 3 * BAND, {cb: col >= row + off_first, cb + 2: col <= row}),
                    (n_ctx + BAND, n_ctx + 3 * BAND, {0: col >= row, 2: col <= row - off_last}))


def _win_call(layer, sink, qwin, kwin, vwin, n_ctx):
    B, S, _ = qwin.shape
    nb = S // BAND
    ncb = n_ctx // BAND
    assert n_ctx == 2 * BAND and nb % 2 == 0
    nk = n_ctx + 3 * BAND
    one = lambda w, f: pl.BlockSpec((1, BAND, w), lambda b, i: (b, jnp.clip(f(i), ncb, nb - 1), 0))
    two = lambda w: pl.BlockSpec((1, 2 * BAND, w), lambda b, i: (b, i, 0))
    ctx = lambda w: pl.BlockSpec((1, n_ctx, w), lambda b, i: (b, 0, 0))
    prev, nxt = (lambda i: 2 * i - 1), (lambda i: 2 * i + 2)
    return pl.pallas_call(
        functools.partial(_win_kernel, layer, n_ctx, nb // 2),
        grid=(B, nb // 2),
        in_specs=[pl.BlockSpec(memory_space=pltpu.SMEM), two(WIN_HEADS * LANES),
                  ctx(LANES), ctx(512), one(LANES, prev), two(LANES), one(LANES, nxt),
                  one(512, prev), two(512), one(512, nxt)],
        out_specs=two(512),
        out_shape=jax.ShapeDtypeStruct((B, S, 512), BF16),
        scratch_shapes=[pltpu.VMEM((2 * n_ctx + 4 * BAND, LANES), BF16),
                        pltpu.VMEM((2 * n_ctx + 4 * BAND, 512), BF16),
                        pltpu.VMEM((2, WIN_HEADS * BAND, nk), BF16)],
        compiler_params=_compiler_params(("parallel", "parallel")),
        name="win_attn",
    )(sink, qwin, kwin, vwin, kwin, kwin, kwin, vwin, vwin, vwin)


MLA_KCHUNK = 256
MLA_HEADS_PER_STEP = 4


def _mla_kernel(n_ctx, n_keys, q_ref, k_ref, v_ref, o_ref, s_ref):
    qi = pl.program_id(2)
    tq = q_ref.shape[1]
    lane = lax.broadcasted_iota(jnp.int32, (tq, LANES), 1)

    def run(nk):
        row_max = []
        for hh in range(MLA_HEADS_PER_STEP):
            hl = slice(hh * LANES, (hh + 1) * LANES)
            q = q_ref[0, :, hl]
            mrun = None
            for c in range(nk // MLA_KCHUNK):
                s = _dot_nt(q, k_ref[0, c * MLA_KCHUNK:(c + 1) * MLA_KCHUNK, hl])
                s_ref[hh, c] = s
                mc = jnp.maximum(s[:, :LANES], s[:, LANES:])
                mrun = mc if mrun is None else jnp.maximum(mrun, mc)
            row_max.append(jnp.max(mrun, axis=1, keepdims=True))
        outs = []
        for hh in range(MLA_HEADS_PER_STEP):
            acc = jnp.zeros((tq, LANES), F32)
            for c in range(nk // MLA_KCHUNK):
                p = jnp.exp2(s_ref[hh, c] - row_max[hh]).astype(BF16)
                acc = acc + _dot(p, v_ref[0, c * MLA_KCHUNK:(c + 1) * MLA_KCHUNK,
                                          hh * LANES:(hh + 1) * LANES])
            den = acc[:, 64:65] if hh % 2 == 0 else acc[:, 0:1]
            outs.append(acc / den)
        o_ref[0] = jnp.concatenate(
            [jnp.where(lane < 64, outs[2 * j], outs[2 * j + 1]) for j in range(len(outs) // 2)],
            axis=1).astype(o_ref.dtype)

    @pl.when(qi == 0)
    def _():
        run(n_ctx)

    @pl.when(qi > 0)
    def _():
        run(n_keys)


def _mla_call(qcat, kcat, vext, n_ctx):
    B, S, _ = qcat.shape
    H = MLA_HEADS
    T = TOK_BLOCK
    assert n_ctx == T
    G = MLA_HEADS_PER_STEP
    kv = pl.BlockSpec((1, S, G * LANES), lambda b, hp, i: (b, 0, hp))
    return pl.pallas_call(
        functools.partial(_mla_kernel, n_ctx, S),
        grid=(B, H // G, S // T),
        in_specs=[pl.BlockSpec((1, T, G * LANES), lambda b, hp, i: (b, i, hp)), kv, kv],
        out_specs=pl.BlockSpec((1, T, (G // 2) * LANES), lambda b, hp, i: (b, i, hp)),
        out_shape=jax.ShapeDtypeStruct((B, S, (H // 2) * LANES), BF16),
        scratch_shapes=[pltpu.VMEM((G, S // MLA_KCHUNK, T, MLA_KCHUNK), F32)],
        compiler_params=_compiler_params(("parallel", "parallel", "arbitrary")),
        name="mla_attn",
    )(qcat, kcat, vext)


MERGE_SAMPLES = 4
PROJ_SAMPLES = 2


def _merge_kernel(alpha, ns, x_ref, oa_ref, ob_ref, sg_ref, woa_ref, wob_ref, wout_ref, g1_ref,
                  b1_ref, *refs):
    mod_refs, (wr_ref, x1_ref, h2_ref, lg_ref) = refs[:3 * ns], refs[3 * ns:]
    T, D = x_ref.shape[1], x_ref.shape[2]
    rows = lambda ref: ref[...].reshape(ns * T, ref.shape[2])
    sg = rows(sg_ref)
    t = (sg[:, :D].astype(F32) * _dot(rows(oa_ref), woa_ref[0])
         + sg[:, D:].astype(F32) * _dot(rows(ob_ref), wob_ref[0]))
    y = _dot(t.astype(BF16), wout_ref[0])
    h2s = []
    for n in range(ns):
        m2_ref, m3_ref, m4_ref = mod_refs[3 * n:3 * n + 3]
        x1 = _layer_norm(alpha * x_ref[n] + m2_ref[0] * y[n * T:(n + 1) * T],
                         g1_ref[0], b1_ref[0])
        x1_ref[n] = x1
        h2 = x1 * (1.0 + m4_ref[0]) + m3_ref[0]
        h2_ref[n] = _rows_to_tiles(h2)
        h2s.append(h2.astype(BF16))
    lg = _dot(jnp.concatenate(h2s, axis=0), wr_ref[0])
    for n in range(ns):
        lg_ref[n] = lg[n * T:(n + 1) * T].T[:N_EXPERTS]


def _merge_call(layer, alpha, x_res, oa, ob, sg, w_oa, w_ob, w_out, ln1_g, ln1_b, mods, w_r,
                n_ctx_blocks):
    B, S, D = x_res.shape
    T = TOK_BLOCK
    L = w_oa.shape[0]
    ns = MERGE_SAMPLES if B % MERGE_SAMPLES == 0 else 1

    def mod_spec(k, n):
        def imap(bp, i):
            row = jnp.where(i < n_ctx_blocks, B, ns * bp + n)
            return ((layer * ADA_ROWS + row) * 6 + k, 0, 0)
        return pl.BlockSpec((1, 1, D), imap)

    tok = lambda w: pl.BlockSpec((ns, T, w), lambda bp, i: (bp, i, 0))
    const = lambda shp: pl.BlockSpec(shp, lambda bp, i: (layer,) + (0,) * (len(shp) - 1),
                                     pipeline_mode=pl.Buffered(1))
    mod_specs = [mod_spec(k, n) for n in range(ns) for k in (2, 3, 4)]
    return pl.pallas_call(
        functools.partial(_merge_kernel, alpha, ns),
        grid=(B // ns, S // T),
        in_specs=[tok(D), tok(512), tok(512), tok(2048),
                  const((1, 512, D)), const((1, 512, D)), const((1, D, D)),
                  const((1, 1, D)), const((1, 1, D))] + mod_specs + [const((1, D, LANES))],
        out_specs=[tok(D), pl.BlockSpec((ns, T, SUBLANES, LANES), lambda bp, i: (bp, i, 0, 0)),
                   pl.BlockSpec((ns, N_EXPERTS, T), lambda bp, i: (bp, 0, i))],
        out_shape=[jax.ShapeDtypeStruct((B, S, D), F32),
                   jax.ShapeDtypeStruct((B, S, SUBLANES, LANES), F32),
                   jax.ShapeDtypeStruct((B, N_EXPERTS, S), F32)],
        compiler_params=_compiler_params(("parallel", "parallel")),
        name="merge",
    )(x_res, oa, ob, sg, w_oa, w_ob, w_out, ln1_g.reshape(L, 1, D), ln1_b.reshape(L, 1, D),
      *([mods] * (3 * ns)), w_r)


def _cumsum_lanes(x):
    n = x.shape[1]
    xb = x.astype(BF16)
    r = lax.broadcasted_iota(jnp.int32, (LANES, LANES), 0)
    c = lax.broadcasted_iota(jnp.int32, (LANES, LANES), 1)
    tri = jnp.where(r <= c, 1.0, 0.0).astype(BF16)
    tb = lax.broadcasted_iota(jnp.int32, (n, LANES), 0) // LANES
    kb = lax.broadcasted_iota(jnp.int32, (n, LANES), 1)
    before = jnp.where(tb < kb, 1.0, 0.0).astype(BF16)
    off = _dot(xb, before)
    outs = []
    for k in range(n // LANES):
        outs.append(_dot(xb[:, k * LANES:(k + 1) * LANES], tri) + off[:, k:k + 1])
    return outs


def _select_top(aff, cap):
    bits = pltpu.bitcast(aff, jnp.int32)
    thr = jnp.zeros((aff.shape[0], 1), jnp.int32)
    for bit in range(30, -1, -1):
        cand = thr | (1 << bit)
        cnt = jnp.sum(jnp.where(bits >= cand, 1.0, 0.0), axis=1, keepdims=True)
        thr = jnp.where(cnt >= cap, cand, thr)
    gt = bits > thr
    eq = jnp.where(bits == thr, 1.0, 0.0)
    need = cap - jnp.sum(jnp.where(gt, 1.0, 0.0), axis=1, keepdims=True)
    eq_rank = jnp.concatenate(_cumsum_lanes(eq), axis=1) - eq
    sel = jnp.where(jnp.logical_or(gt, jnp.logical_and(eq > 0.5, eq_rank < need)), 1.0, 0.0)
    return _cumsum_lanes(sel)


ROUTE_PAD = 1e6


def _route_kernel(n_ctx, cap_c, cap_l, lg_ref, aff_ref, idx_ref, cc_ref, cl_ref):
    lg = lg_ref[0]
    m = jnp.max(lg, axis=0, keepdims=True)
    ex = jnp.exp(lg - m)
    aff = ex / jnp.sum(ex, axis=0, keepdims=True)
    aff_ref[0] = aff
    for c_ref, blocks in ((cc_ref, _select_top(aff[:, :n_ctx], cap_c)),
                          (cl_ref, _select_top(aff[:, n_ctx:], cap_l))):
        c_ref[...] = jnp.full(c_ref.shape, ROUTE_PAD, F32)
        for k, blk in enumerate(blocks):
            for ex_i in range(N_EXPERTS):
                c_ref[ex_i, k:k + 1, :] = blk[ex_i:ex_i + 1, :]

    def slots(c_ref, e, n_blocks, width):
        cm = c_ref[e]
        jrow = lax.broadcasted_iota(jnp.int32, (1, width), 1).astype(F32)
        rows = max(n_blocks, SUBLANES)
        cend = cm[0:rows, LANES - 1:LANES]
        blk = jnp.sum(jnp.where(cend <= jrow, 1.0, 0.0), axis=0, keepdims=True)
        kio = lax.broadcasted_iota(jnp.int32, (LANES, width), 0).astype(F32)
        onehot = jnp.where(kio == blk, 1.0, 0.0).astype(BF16)
        cmt = cm.T
        hi = jnp.floor(cmt * (1.0 / 256.0))
        lo = cmt - 256.0 * hi
        cg = 256.0 * _dot(hi.astype(BF16), onehot) + _dot(lo.astype(BF16), onehot)
        inside = jnp.sum(jnp.where(cg <= jrow, 1.0, 0.0), axis=0, keepdims=True)
        return (blk * LANES + inside).astype(jnp.int32)

    def per_expert(e, carry):
        n_lat_blocks = (lg_ref.shape[2] - n_ctx) // LANES
        idx_ref[0, e, :, 0:cap_l] = slots(cl_ref, e, n_lat_blocks, cap_l) + n_ctx
        idx_ref[0, e, :, cap_l:cap_l + cap_c] = slots(cc_ref, e, n_ctx // LANES, LANES)[:, 0:cap_c]
        return carry

    lax.fori_loop(0, N_EXPERTS, per_expert, 0)


def _route_call(lg_t, n_ctx, cap_c, cap_l):
    B, E, S = lg_t.shape
    cap = cap_c + cap_l
    assert (S - n_ctx) // LANES <= LANES and cap_l % LANES == 0 and cap_c <= LANES
    return pl.pallas_call(
        functools.partial(_route_kernel, n_ctx, cap_c, cap_l),
        grid=(B,),
        in_specs=[pl.BlockSpec((1, E, S), lambda b: (b, 0, 0))],
        out_specs=[pl.BlockSpec((1, E, S), lambda b: (b, 0, 0)),
                   pl.BlockSpec((1, E, 1, cap), lambda b: (b, 0, 0, 0))],
        out_shape=[jax.ShapeDtypeStruct((B, E, S), F32),
                   jax.ShapeDtypeStruct((B, E, 1, cap), jnp.int32)],
        scratch_shapes=[pltpu.VMEM((E, LANES, LANES), F32), pltpu.VMEM((E, LANES, LANES), F32)],
        compiler_params=_compiler_params(("parallel",)),
        name="route",
    )(lg_t)


ROW_BATCH = 16
MOE_FF_CHUNK = 512


GATHER_EXPERTS = 2


def _gather_kernel(idx_ref, h_ref, xg_ref, tmp_ref):
    cap = xg_ref.shape[2]
    for ge in range(GATHER_EXPERTS):
        def body(g, carry):
            for r in range(ROW_BATCH):
                j = g * ROW_BATCH + r
                tmp_ref[j] = h_ref[0, idx_ref[0, 0, ge * cap + j]]
            return carry

        lax.fori_loop(0, cap // ROW_BATCH, body, 0)
        xg_ref[0, ge] = _tiles_to_rows(tmp_ref[...]).astype(xg_ref.dtype)


def _gather_call(idx, h2t):
    B, S = h2t.shape[0], h2t.shape[1]
    D = SUBLANES * LANES
    E, cap = idx.shape[1], idx.shape[2]
    G = GATHER_EXPERTS
    assert cap % ROW_BATCH == 0 and E % G == 0
    return pl.pallas_call(
        _gather_kernel,
        grid=(B, E // G),
        in_specs=[pl.BlockSpec((1, 1, G * cap), lambda b, e: (b * (E // G) + e, 0, 0),
                               memory_space=pltpu.SMEM),
                  pl.BlockSpec((1, S, SUBLANES, LANES), lambda b, e: (b, 0, 0, 0))],
        out_specs=pl.BlockSpec((1, G, cap, D), lambda b, e: (b, e, 0, 0)),
        out_shape=jax.ShapeDtypeStruct((B, E, cap, D), BF16),
        scratch_shapes=[pltpu.VMEM((cap, SUBLANES, LANES), F32)],
        compiler_params=_compiler_params(("parallel", "arbitrary")),
        name="moe_gather",
    )(idx.reshape(B * E // G, 1, G * cap), h2t)


def _moe_kernel(idx_ref, aff_ref, xg_ref, wg_ref, wu_ref, wd_ref, out_ref, y_ref):
    s = pl.program_id(1)
    n_exp = pl.num_programs(1) - 1
    cap = y_ref.shape[1]
    n_batches = cap // ROW_BATCH

    def scatter(slot, batches):
        new = None
        for g in batches:
            rows = _rows_to_tiles(y_ref[slot, g * ROW_BATCH:(g + 1) * ROW_BATCH, :])
            toks = [idx_ref[0, 0, g * ROW_BATCH + r] for r in range(ROW_BATCH)]
            old = [out_ref[0, t] for t in toks]
            for r, t in enumerate(toks):
                new = old[r] + rows[r] * aff_ref[0, 0, t]
                out_ref[0, t] = new
        return new

    def ffn(slot, prev_slot):
        ff = wg_ref.shape[3]
        n_dots = 3 * (ff // MOE_FF_CHUNK)
        per = -(-n_batches // n_dots)
        todo = list(range(n_batches)) if prev_slot is not None else []

        def weights(w):
            take, todo[:] = todo[:per], todo[per:]
            w = w.astype(BF16)
            if not take:
                return w
            last = scatter(prev_slot, take)
            bits = pltpu.bitcast(last, jnp.uint32)
            zero = pltpu.bitcast((bits >> 16) >> 16, F32)
            zero = jnp.concatenate([zero, zero], axis=0).astype(BF16)
            return w + jnp.tile(zero, (w.shape[0] // zero.shape[0], w.shape[1] // LANES))

        x = xg_ref[0, 0]
        y = None
        for f0 in range(0, ff, MOE_FF_CHUNK):
            f1 = f0 + MOE_FF_CHUNK
            a = _dot(x, weights(wg_ref[0, 0, :, f0:f1]))
            u = _dot(x, weights(wu_ref[0, 0, :, f0:f1]))
            hmid = (a * jax.nn.sigmoid(a) * u).astype(BF16)
            part = _dot(hmid, weights(wd_ref[0, 0, f0:f1, :]))
            y = part if y is None else y + part
        y_ref[slot] = y

    @pl.when(s == 0)
    def _():
        out_ref[...] = jnp.zeros(out_ref.shape, out_ref.dtype)
        ffn(0, None)

    @pl.when(jnp.logical_and(s > 0, s < n_exp))
    def _():
        slot = s % 2
        ffn(slot, 1 - slot)

    @pl.when(s == n_exp)
    def _():
        scatter((n_exp - 1) % 2, range(n_batches))


def _moe_call(layer, idx, aff, xg, w_gate, w_up, w_down, S):
    B, E, cap, D = xg.shape
    F = w_gate.shape[3]
    cur = lambda s: jnp.minimum(s, E - 1)
    prev = lambda s: jnp.maximum(s - 1, 0)
    wspec = lambda shp: pl.BlockSpec(shp, lambda b, s: (layer, cur(s), 0, 0))
    return pl.pallas_call(
        _moe_kernel,
        grid=(B, E + 1),
        in_specs=[pl.BlockSpec((1, 1, cap), lambda b, s: (b * E + prev(s), 0, 0), memory_space=pltpu.SMEM),
                  pl.BlockSpec((1, 1, S), lambda b, s: (b * E + prev(s), 0, 0), memory_space=pltpu.SMEM),
                  pl.BlockSpec((1, 1, cap, D), lambda b, s: (b, cur(s), 0, 0)),
                  wspec((1, 1, D, F)), wspec((1, 1, D, F)), wspec((1, 1, F, D))],
        out_specs=pl.BlockSpec((1, S, SUBLANES, LANES), lambda b, s: (b, 0, 0, 0),
                               pipeline_mode=pl.Buffered(1)),
        out_shape=jax.ShapeDtypeStruct((B, S, SUBLANES, LANES), F32),
        scratch_shapes=[pltpu.VMEM((2, cap, D), F32)],
        compiler_params=_compiler_params(("parallel", "arbitrary")),
        name="moe_ffn",
    )(idx.reshape(B * E, 1, cap), aff.reshape(B * E, 1, S), xg, w_gate, w_up, w_down)


def _final_kernel(alpha, x_ref, moe_ref, g_ref, b_ref, m5_ref, o_ref):
    o_ref[0] = _layer_norm(alpha * x_ref[0] + m5_ref[0] * _tiles_to_rows(moe_ref[0]),
                           g_ref[0], b_ref[0])


def _final_call(layer, alpha, x1, moe, ln2_g, ln2_b, mods, n_ctx):
    B, S, D = x1.shape
    T = TOK_BLOCK
    L = ln2_g.shape[0]
    ncb = n_ctx // T
    tok_in = pl.BlockSpec((1, T, D), lambda b, i: (b, i + ncb, 0))
    const = pl.BlockSpec((1, 1, D), lambda b, i: (layer, 0, 0))
    return pl.pallas_call(
        functools.partial(_final_kernel, alpha),
        grid=(B, (S - n_ctx) // T),
        in_specs=[tok_in, pl.BlockSpec((1, T, SUBLANES, LANES), lambda b, i: (b, i + ncb, 0, 0)),
                  const, const,
                  pl.BlockSpec((1, 1, D), lambda b, i: ((layer * ADA_ROWS + b) * 6 + 5, 0, 0))],
        out_specs=pl.BlockSpec((1, T, D), lambda b, i: (b, i, 0)),
        out_shape=jax.ShapeDtypeStruct((B, S - n_ctx, D), F32),
        compiler_params=_compiler_params(("parallel", "parallel")),
        name="final_norm",
    )(x1, moe, ln2_g.reshape(L, 1, D), ln2_b.reshape(L, 1, D), mods)


def _prep_weights(w_in, w_uq, w_ukv, w_router):
    L, D, _ = w_in.shape
    krs = jnp.concatenate([jnp.zeros((L, D, 64), w_in.dtype), w_in[..., 512:544],
                           jnp.zeros((L, D, 32), w_in.dtype)], axis=-1)
    w_in_p = jnp.concatenate([w_in[..., 0:512], krs, w_in[..., 544:]], axis=-1).astype(BF16)

    kvr = w_ukv.reshape(L, MLA_KV_RANK, MLA_HEADS, MLA_NOPE + MLA_V)
    kn, vv = kvr[..., :MLA_NOPE], kvr[..., MLA_NOPE:]
    zk = jnp.zeros_like(kn)
    w_k = jnp.concatenate([kn, zk], axis=-1).reshape(L, MLA_KV_RANK, MLA_HEADS * LANES)
    v_even = jnp.concatenate([vv, zk], axis=-1)
    v_odd = jnp.concatenate([zk, vv], axis=-1)
    odd = (jnp.arange(MLA_HEADS) % 2 == 1)[None, None, :, None]
    w_v = jnp.where(odd, v_odd, v_even).reshape(L, MLA_KV_RANK, MLA_HEADS * LANES)
    w_kv = jnp.concatenate([w_k, w_v], axis=-1).astype(BF16)

    qr = w_uq.reshape(L, MLA_Q_RANK, MLA_HEADS, MLA_NOPE + MLA_ROPE)
    w_q = jnp.concatenate([qr, jnp.zeros((L, MLA_Q_RANK, MLA_HEADS, 32), w_uq.dtype)], axis=-1)
    w_q = w_q.reshape(L, MLA_Q_RANK, MLA_HEADS * LANES).astype(BF16)

    w_r = jnp.concatenate(
        [w_router, jnp.zeros((L, D, LANES - N_EXPERTS), w_router.dtype)], axis=-1).astype(BF16)
    return w_in_p, w_kv, w_q, w_r


def _ones_columns():
    lane = jnp.arange(LANES)
    even = (lane == 64).astype(F32)
    odd = (lane == 0).astype(F32)
    win = jnp.concatenate([even, odd, even, odd])
    mla = jnp.concatenate([even, odd] * (MLA_HEADS // 2))
    return jnp.concatenate([win, mla])[None, :]


def _rope_tables(n_ctx, n_lat):
    pos = jnp.arange(n_lat)
    rowp = (pos // GRID_W).astype(F32)
    colp = (pos % GRID_W).astype(F32)

    def pattern(rot_dim):
        nf = rot_dim // 4
        inv = ROPE_THETA ** (-jnp.arange(nf, dtype=F32) / nf)
        ar, ac = rowp[:, None] * inv, colp[:, None] * inv
        cos = jnp.concatenate([jnp.cos(ar), jnp.cos(ar), jnp.cos(ac), jnp.cos(ac)], axis=1)
        sin = jnp.concatenate([-jnp.sin(ar), jnp.sin(ar), -jnp.sin(ac), jnp.sin(ac)], axis=1)
        return cos, sin

    cw, sw = pattern(WIN_HEAD_DIM)
    cw, sw = jnp.tile(cw, (1, 2)), jnp.tile(sw, (1, 2))
    cm, sm = pattern(MLA_ROPE)
    one64, zero64 = jnp.ones((n_lat, 64), F32), jnp.zeros((n_lat, 64), F32)
    cm = jnp.concatenate([one64, cm, one64[:, :32]], axis=1)
    sm = jnp.concatenate([zero64, sm, zero64[:, :32]], axis=1)
    sq_w, sq_m = WIN_SCALE * LOG2E, MLA_SCALE * LOG2E
    lat = jnp.concatenate([cw * sq_w, sw * sq_w, cw, sw, cm * sq_m, sm * sq_m, cm, sm], axis=1)
    ones, zeros = jnp.ones((n_ctx, LANES), F32), jnp.zeros((n_ctx, LANES), F32)
    ctx = jnp.concatenate([ones * sq_w, zeros, ones, zeros, ones * sq_m, zeros, ones, zeros], axis=1)
    return jnp.concatenate([ctx, lat], axis=0)


def kernel(x, c, ctx, c_ctx, w_ada, b_ada, w_in, attn_sink, mla_q_norm, mla_kv_norm, w_uq, w_ukv,
           w_oa, w_ob, w_out, ln1_g, ln1_b, w_router, w_exp_gate, w_exp_up, w_exp_down, ln2_g, ln2_b):
    B, n_lat, D = x.shape
    n_ctx = ctx.shape[1]
    depth = w_in.shape[0]
    S = n_ctx + n_lat
    alpha = (2 * depth) ** 0.25
    assert D == D_MODEL and n_ctx == TOK_BLOCK and n_lat % TOK_BLOCK == 0 and B + 1 <= ADA_ROWS
    cap_c = CAPACITY_FACTOR * n_ctx // N_EXPERTS
    cap_l = CAPACITY_FACTOR * n_lat // N_EXPERTS
    assert cap_l % 256 == 0 and cap_c % 8 == 0
    ncb = n_ctx // TOK_BLOCK

    cond = jnp.concatenate([c, c_ctx[None], jnp.zeros((ADA_ROWS - B - 1, D), F32)], axis=0)
    mods = _ada_call(cond.T, w_ada, b_ada, B + 1).reshape(depth * ADA_ROWS * 6, 1, D)

    w_in_p, w_kv, w_q, w_r = _prep_weights(w_in, w_uq, w_ukv, w_router)
    w_oa_b, w_ob_b, w_out_b = w_oa.astype(BF16), w_ob.astype(BF16), w_out.astype(BF16)
    tab = _rope_tables(n_ctx, n_lat)
    vones = _ones_columns()

    x_all = (ctx, x)
    moe = None
    for l in range(depth):
        x_res, (qwin, kwin, vwin, qcat, kcat, vext, sg) = _proj_call(
            l, alpha, x_all, moe, ln2_g, ln2_b, mods, tab, w_in_p, mla_kv_norm, mla_q_norm,
            w_kv, w_q, vones, ncb)
        oa = _win_call(l, attn_sink, qwin, kwin, vwin, n_ctx)
        ob = _mla_call(qcat, kcat, vext, n_ctx)
        x1, h2, lg_t = _merge_call(l, alpha, x_res, oa, ob, sg, w_oa_b, w_ob_b, w_out_b,
                                   ln1_g, ln1_b, mods, w_r, ncb)
        aff, idx4 = _route_call(lg_t, n_ctx, cap_c, cap_l)
        idx = idx4.reshape(B, N_EXPERTS, cap_c + cap_l)
        xg = _gather_call(idx, h2)
        moe = _moe_call(l, idx, aff, xg, w_exp_gate, w_exp_up, w_exp_down, S)
        x_all = x1
    return _final_call(depth - 1, alpha, x_all, moe, ln2_g, ln2_b, mods, n_ctx)
```

```python
import functools
import math

import jax
import jax.numpy as jnp
from jax import lax
from jax.experimental import pallas as pl
from jax.experimental.pallas import tpu as pltpu

D_MODEL = 1024
GRID_W = 64
WIN_HEADS = 8
WIN_KV_HEADS = 2
WIN_HEAD_DIM = 64
BAND = 128
MLA_HEADS = 8
MLA_Q_RANK = 384
MLA_KV_RANK = 256
MLA_NOPE = 64
MLA_ROPE = 32
MLA_V = 64
N_EXPERTS = 16
CAPACITY_FACTOR = 2
ROPE_THETA = 10000.0
LN_EPS = 1e-5
RMS_EPS = 1e-6
NEG_INF = -1e30
LOG2E = math.log2(math.e)
WIN_SCALE = WIN_HEAD_DIM ** -0.5
MLA_SCALE = (MLA_NOPE + MLA_ROPE) ** -0.5

LANES = 128
TOK_BLOCK = 256
VMEM_LIMIT = 56 * 1024 * 1024

BF16 = jnp.bfloat16
F32 = jnp.float32

SEG_KWIN = (0, 128)
SEG_VWIN = (128, 256)
SEG_CKV = (256, 512)
SEG_KR = (512, 640)
SEG_QWIN = (640, 1152)
SEG_CQ = (1152, 1536)
SEG_GATE = (1536, 3584)
IN_COLS_PAD = 3584


def _dot(a, b):
    return jnp.dot(a, b, preferred_element_type=F32)


def _dot_nt(a, b):
    return lax.dot_general(a, b, (((1,), (1,)), ((), ())), preferred_element_type=F32)


def _layer_norm(z, g, b):
    mu = jnp.mean(z, axis=-1, keepdims=True)
    zc = z - mu
    var = jnp.mean(zc * zc, axis=-1, keepdims=True)
    return zc * lax.rsqrt(var + LN_EPS) * g + b


def _rms_norm(x, g):
    return x * lax.rsqrt(jnp.mean(x * x, axis=-1, keepdims=True) + RMS_EPS) * g


def _rope_slab(x, cos, sin, half):
    lane = lax.broadcasted_iota(jnp.int32, x.shape, 1)
    partner = jnp.where((lane & half) == 0,
                        pltpu.roll(x, LANES - half, 1), pltpu.roll(x, half, 1))
    return x * cos + partner * sin


SUBLANES = 8


def _rows_to_tiles(y):
    n = y.shape[0]
    y3 = pltpu.einshape("r(sl)->srl", y, s=SUBLANES)
    y4 = y3.reshape(SUBLANES, n // SUBLANES, SUBLANES, LANES)
    return jnp.transpose(y4, (1, 2, 0, 3)).reshape(n, SUBLANES, LANES)


def _tiles_to_rows(x3):
    n = x3.shape[0]
    x4 = x3.reshape(n // SUBLANES, SUBLANES, SUBLANES, LANES)
    xs = jnp.transpose(x4, (2, 0, 1, 3)).reshape(SUBLANES, n, LANES)
    return pltpu.einshape("srl->r(sl)", xs)


def _compiler_params(sem):
    return pltpu.CompilerParams(dimension_semantics=sem, vmem_limit_bytes=VMEM_LIMIT)


ADA_ROWS = 8
ADA_TN = 512


def _ada_kernel(n_rows, condt_ref, w_ref, b_ref, o_ref):
    ct = condt_ref[...]
    st = ct * jax.nn.sigmoid(ct)
    w = w_ref[0]
    rows = []
    for r in range(n_rows):
        rows.append(jnp.sum(w * st[:, r:r + 1], axis=0, keepdims=True) + b_ref[0])
    rows.append(jnp.zeros((ADA_ROWS - n_rows, w.shape[1]), F32))
    o_ref[0] = jnp.concatenate(rows, axis=0)


def _ada_call(cond_t, w_ada, b_ada, n_rows):
    L, D, N = w_ada.shape
    return pl.pallas_call(
        functools.partial(_ada_kernel, n_rows),
        grid=(L, N // ADA_TN),
        in_specs=[
            pl.BlockSpec((D, ADA_ROWS), lambda l, j: (0, 0)),
            pl.BlockSpec((1, D, ADA_TN), lambda l, j: (l, 0, j)),
            pl.BlockSpec((1, 1, ADA_TN), lambda l, j: (l, 0, j)),
        ],
        out_specs=pl.BlockSpec((1, ADA_ROWS, ADA_TN), lambda l, j: (l, 0, j)),
        out_shape=jax.ShapeDtypeStruct((L, ADA_ROWS, N), F32),
        compiler_params=_compiler_params(("parallel", "parallel")),
        name="adaln_mod",
    )(cond_t, w_ada, b_ada.reshape(L, 1, N))


def _proj_kernel(has_ln, alpha, n_ctx_blocks, ns, *refs):
    refs = list(refs)
    if has_ln:
        x_ref, moe_ref, g2_ref, b2_ref = refs[:4]
        m5_refs, refs = refs[4:4 + ns], refs[4 + ns:]
    else:
        (ctx_ref, x_ref), refs = refs[:2], refs[2:]
    mod_refs, refs = refs[:2 * ns], refs[2 * ns:]
    (tab_ref, win_ref, gkv_ref, gq_ref, wkv_ref, wq_ref, vones_ref,
     xres_ref, qwin_ref, kwin_ref, vwin_ref, qcat_ref, kcat_ref, vext_ref, sg_ref) = refs
    T = x_ref.shape[1]
    hbs = []
    for n in range(ns):
        if has_ln:
            z = alpha * x_ref[n] + m5_refs[n][0] * _tiles_to_rows(moe_ref[n])
            x = _layer_norm(z, g2_ref[0], b2_ref[0])
        else:
            x = jnp.where(pl.program_id(0) < n_ctx_blocks, ctx_ref[n], x_ref[n])
        xres_ref[n] = x
        sh_ref, sc_ref = mod_refs[2 * n:2 * n + 2]
        hbs.append((x * (1.0 + sc_ref[0]) + sh_ref[0]).astype(BF16))
    hb = jnp.concatenate(hbs, axis=0)
    split = lambda v: v.reshape(ns, T, v.shape[1])

    def seg(s):
        return _dot(hb, win_ref[0, :, s[0]:s[1]])

    def tab(i):
        t = tab_ref[:, i * LANES:(i + 1) * LANES]
        return jnp.concatenate([t] * ns, axis=0)

    cq_w, sq_w, ck_w, sk_w, cq_m, sq_m, ck_m, sk_m = (tab(i) for i in range(8))

    lane_lo = lax.broadcasted_iota(jnp.int32, (hb.shape[0], LANES), 1) < WIN_HEAD_DIM
    half_swap = lambda t: pltpu.roll(t, WIN_HEAD_DIM, 1)

    ckv = _rms_norm(seg(SEG_CKV), gkv_ref[0]).astype(BF16)
    cq = _rms_norm(seg(SEG_CQ), gq_ref[0]).astype(BF16)

    qw = seg(SEG_QWIN)
    q_slabs = []
    for s2 in range(WIN_HEADS // 2):
        pair = _rope_slab(qw[:, s2 * LANES:(s2 + 1) * LANES], cq_w, sq_w, 16)
        pair_sw = half_swap(pair)
        if 2 * s2 < WIN_HEADS // WIN_KV_HEADS:
            q_slabs += [jnp.where(lane_lo, pair, 0.0), jnp.where(lane_lo, pair_sw, 0.0)]
        else:
            q_slabs += [jnp.where(lane_lo, 0.0, pair_sw), jnp.where(lane_lo, 0.0, pair)]
    qwin_ref[...] = split(jnp.concatenate(q_slabs, axis=1).astype(BF16))

    kv = _dot(ckv, wkv_ref[0])
    kr = _rope_slab(seg(SEG_KR), ck_m, sk_m, 8)
    qm = _dot(cq, wq_ref[0])
    for h in range(MLA_HEADS):
        sl = slice(h * LANES, (h + 1) * LANES)
        kcat_ref[:, :, sl] = split((kv[:, sl] + kr).astype(BF16))
        vext_ref[:, :, sl] = split(
            (kv[:, 1024 + h * LANES:1024 + (h + 1) * LANES]
             + vones_ref[:, 512 + h * LANES:512 + (h + 1) * LANES]).astype(BF16))
        qcat_ref[:, :, sl] = split(_rope_slab(qm[:, sl], cq_m, sq_m, 8).astype(BF16))

    sg_ref[...] = split(jax.nn.sigmoid(seg(SEG_GATE)).astype(BF16))
    va = seg(SEG_VWIN)
    va_sw = half_swap(va)
    vwin_ref[...] = split((jnp.concatenate(
        [jnp.where(lane_lo, va, 0.0), jnp.where(lane_lo, 0.0, va_sw),
         jnp.where(lane_lo, va_sw, 0.0), jnp.where(lane_lo, 0.0, va)], axis=1)
        + vones_ref[:, 0:512]).astype(BF16))
    kwin_ref[...] = split(_rope_slab(seg(SEG_KWIN), ck_w, sk_w, 16).astype(BF16))


def _proj_call(layer, alpha, x_all, moe, ln2_g, ln2_b, mods, tab, w_in_p, g_kvn, g_qn, w_kv, w_q,
               vones, n_ctx_blocks):
    T = TOK_BLOCK
    has_ln = moe is not None
    if has_ln:
        B, S, D = x_all.shape
    else:
        ctx_in, x_in = x_all
        B, D = x_in.shape[0], x_in.shape[2]
        S = ctx_in.shape[1] + x_in.shape[1]
    L = w_in_p.shape[0]
    ns = PROJ_SAMPLES if B % PROJ_SAMPLES == 0 else 1

    def mod_spec(k, lyr, n):
        def imap(i, b):
            row = jnp.where(i < n_ctx_blocks, B, ns * b + n)
            return ((lyr * ADA_ROWS + row) * 6 + k, 0, 0)
        return pl.BlockSpec((1, 1, D), imap)

    tok = lambda w: pl.BlockSpec((ns, T, w), lambda i, b: (b, i, 0))
    headed = tok(MLA_HEADS * LANES)
    const = lambda shp: pl.BlockSpec(shp, lambda i, b: (layer,) + (0,) * (len(shp) - 1),
                                     pipeline_mode=pl.Buffered(1))

    if has_ln:
        prev = pl.BlockSpec((1, 1, D), lambda i, b: (layer - 1, 0, 0))
        tiled = pl.BlockSpec((ns, T, SUBLANES, LANES), lambda i, b: (b, i, 0, 0))
        in_specs = [tok(D), tiled, prev, prev] + [mod_spec(5, layer - 1, n) for n in range(ns)]
        args = [x_all, moe, ln2_g.reshape(L, 1, D), ln2_b.reshape(L, 1, D)] + [mods] * ns
    else:
        assert ctx_in.shape[1] == n_ctx_blocks * T
        in_specs = [pl.BlockSpec((ns, T, D), lambda i, b: (b, jnp.minimum(i, n_ctx_blocks - 1), 0)),
                    pl.BlockSpec((ns, T, D), lambda i, b: (b, jnp.maximum(i - n_ctx_blocks, 0), 0))]
        args = [ctx_in, x_in]
    in_specs += [mod_spec(k, layer, n) for n in range(ns) for k in (0, 1)]
    args += [mods] * (2 * ns)
    in_specs += [
        pl.BlockSpec((T, 8 * LANES), lambda i, b: (i, 0)),
        const((1, D, IN_COLS_PAD)),
        const((1, 1, MLA_KV_RANK)), const((1, 1, MLA_Q_RANK)),
        const((1, MLA_KV_RANK, 2048)), const((1, MLA_Q_RANK, 1024)),
        pl.BlockSpec((1, 1536), lambda i, b: (0, 0)),
    ]
    args += [tab, w_in_p, g_kvn.reshape(L, 1, -1), g_qn.reshape(L, 1, -1), w_kv, w_q, vones]

    out_specs, out_shape = [tok(D)], [jax.ShapeDtypeStruct((B, S, D), F32)]
    out_specs += [tok(WIN_HEADS * LANES), tok(LANES), tok(512), headed, headed, headed, tok(2048)]
    out_shape += [jax.ShapeDtypeStruct((B, S, WIN_HEADS * LANES), BF16),
                  jax.ShapeDtypeStruct((B, S, LANES), BF16),
                  jax.ShapeDtypeStruct((B, S, 512), BF16)]
    out_shape += [jax.ShapeDtypeStruct((B, S, MLA_HEADS * LANES), BF16)] * 3
    out_shape += [jax.ShapeDtypeStruct((B, S, 2048), BF16)]

    outs = pl.pallas_call(
        functools.partial(_proj_kernel, has_ln, alpha, n_ctx_blocks, ns),
        grid=(S // T, B // ns),
        in_specs=in_specs, out_specs=out_specs, out_shape=out_shape,
        compiler_params=_compiler_params(("parallel", "parallel")),
        name="proj",
    )(*args)
    return outs[0], outs[1:]


WIN_HEAD_ORDER = (0, 2, 1, 3, 4, 6, 5, 7)
WIN_SAMPLES = 2


def _win_kernel(layer, n_ctx, n_steps, sink_ref, q_ref, kc_ref, vc_ref, kp_ref, km_ref, kn_ref,
                vp_ref, vm_ref, vn_ref, o_ref, kall_ref, vall_ref, p_ref):
    i = pl.program_id(1)
    row = lax.broadcasted_iota(jnp.int32, (BAND, LANES), 0)
    col = lax.broadcasted_iota(jnp.int32, (BAND, LANES), 1)
    lane_lo = col < 64
    ns = q_ref.shape[0]
    for n in range(ns):
        kall_ref[n, 0:n_ctx, :] = kc_ref[n]
        vall_ref[n, 0:n_ctx, :] = vc_ref[n]

    def scores(n, qb, r0, nk):
        q = q_ref[n, qb * BAND:(qb + 1) * BAND, :]
        q8 = jnp.concatenate([q[:, h * LANES:(h + 1) * LANES] for h in WIN_HEAD_ORDER], axis=0)
        return _dot_nt(q8, kall_ref[n, r0:r0 + nk, :])

    def softmax(n, qb, nk, masks, s):
        e_sink = []
        for c, h in enumerate(WIN_HEAD_ORDER):
            sc = s[c * BAND:(c + 1) * BAND, :]
            if masks:
                blocks = []
                for kb in range(nk // BAND):
                    blk = sc[:, kb * BAND:(kb + 1) * BAND]
                    blocks.append(jnp.where(masks[kb], blk, NEG_INF) if kb in masks else blk)
                sc = jnp.concatenate(blocks, axis=1)
            sink = sink_ref[layer, h] * LOG2E
            m = jnp.maximum(jnp.max(sc, axis=1, keepdims=True), sink)
            p_ref[n, qb, c * BAND:(c + 1) * BAND, 0:nk] = jnp.exp2(sc - m).astype(BF16)
            e_sink.append(jnp.exp2(sink - m))
        return e_sink

    def values(n, qb, r0, nk, e_sink):
        pairs = []
        for pi in range(4):
            acc = _dot(p_ref[n, qb, pi * 2 * BAND:(pi + 1) * 2 * BAND, 0:nk],
                       vall_ref[n, r0:r0 + nk, pi * LANES:(pi + 1) * LANES])
            es = jnp.concatenate([e_sink[2 * pi], e_sink[2 * pi + 1]], axis=0)
            den = (acc[:, 64:65] if pi % 2 == 0 else acc[:, 0:1]) + es
            pairs.append(acc / den)
        slabs = [jnp.where(lane_lo, pairs[0][:BAND], pairs[1][:BAND]),
                 jnp.where(lane_lo, pairs[0][BAND:], pairs[1][BAND:]),
                 jnp.where(lane_lo, pairs[2][:BAND], pairs[3][:BAND]),
                 jnp.where(lane_lo, pairs[2][BAND:], pairs[3][BAND:])]
        o_ref[n, qb * BAND:(qb + 1) * BAND, :] = jnp.concatenate(slabs, axis=1).astype(o_ref.dtype)

    def attend_both(args0, args1):
        (r0, nk0, m0), (r1, nk1, m1) = args0, args1
        s = [(scores(n, 0, r0, nk0), scores(n, 1, r1, nk1)) for n in range(ns)]
        e = [(softmax(n, 0, nk0, m0, s[n][0]), softmax(n, 1, nk1, m1, s[n][1])) for n in range(ns)]
        for n in range(ns):
            values(n, 0, r0, nk0, e[n][0])
            values(n, 1, r1, nk1, e[n][1])

    @pl.when(i == 0)
    def _():
        attend_both((0, n_ctx, None), (0, n_ctx, None))

    @pl.when(i > 0)
    def _():
        for n in range(ns):
            r = n_ctx
            for k_ref, v_ref, rows in ((kp_ref, vp_ref, BAND), (km_ref, vm_ref, 2 * BAND),
                                       (kn_ref, vn_ref, BAND), (kc_ref, vc_ref, n_ctx)):
                kall_ref[n, r:r + rows, :] = k_ref[n]
                vall_ref[n, r:r + rows, :] = v_ref[n]
                r += rows
        off_first = jnp.where(i == 1, 2 * LANES, 0)
        off_last = jnp.where(i == n_steps - 1, 2 * LANES, 0)
        cb = n_ctx // BAND
        attend_both((0, n_ctx + 3 * BAND, {cb: col >= row + off_first, cb + 2: col <= row}),
                    (n_ctx + BAND, n_ctx + 3 * BAND, {0: col >= row, 2: col <= row - off_last}))


def _win_call(layer, sink, qwin, kwin, vwin, n_ctx):
    B, S, _ = qwin.shape
    nb = S // BAND
    ncb = n_ctx // BAND
    assert n_ctx == 2 * BAND and nb % 2 == 0
    nk = n_ctx + 3 * BAND
    ns = WIN_SAMPLES if B % WIN_SAMPLES == 0 else 1
    one = lambda w, f: pl.BlockSpec((ns, BAND, w), lambda b, i: (b, jnp.clip(f(i), ncb, nb - 1), 0))
    two = lambda w: pl.BlockSpec((ns, 2 * BAND, w), lambda b, i: (b, i, 0))
    ctx = lambda w: pl.BlockSpec((ns, n_ctx, w), lambda b, i: (b, 0, 0))
    prev, nxt = (lambda i: 2 * i - 1), (lambda i: 2 * i + 2)
    return pl.pallas_call(
        functools.partial(_win_kernel, layer, n_ctx, nb // 2),
        grid=(B // ns, nb // 2),
        in_specs=[pl.BlockSpec(memory_space=pltpu.SMEM), two(WIN_HEADS * LANES),
                  ctx(LANES), ctx(512), one(LANES, prev), two(LANES), one(LANES, nxt),
                  one(512, prev), two(512), one(512, nxt)],
        out_specs=two(512),
        out_shape=jax.ShapeDtypeStruct((B, S, 512), BF16),
        scratch_shapes=[pltpu.VMEM((ns, 2 * n_ctx + 4 * BAND, LANES), BF16),
                        pltpu.VMEM((ns, 2 * n_ctx + 4 * BAND, 512), BF16),
                        pltpu.VMEM((ns, 2, WIN_HEADS * BAND, nk), BF16)],
        compiler_params=_compiler_params(("parallel", "parallel")),
        name="win_attn",
    )(sink, qwin, kwin, vwin, kwin, kwin, kwin, vwin, vwin, vwin)


MLA_KCHUNK = 256
MLA_HEADS_PER_STEP = 4


def _mla_kernel(n_ctx, n_keys, q_ref, k_ref, v_ref, o_ref, s_ref):
    qi = pl.program_id(2)
    tq = q_ref.shape[1]
    lane = lax.broadcasted_iota(jnp.int32, (tq, LANES), 1)

    def run(nk):
        row_max = []
        for hh in range(MLA_HEADS_PER_STEP):
            hl = slice(hh * LANES, (hh + 1) * LANES)
            q = q_ref[0, :, hl]
            mrun = None
            for c in range(nk // MLA_KCHUNK):
                s = _dot_nt(q, k_ref[0, c * MLA_KCHUNK:(c + 1) * MLA_KCHUNK, hl])
                s_ref[hh, c] = s
                mc = jnp.maximum(s[:, :LANES], s[:, LANES:])
                mrun = mc if mrun is None else jnp.maximum(mrun, mc)
            row_max.append(jnp.max(mrun, axis=1, keepdims=True))
        outs = []
        for hh in range(MLA_HEADS_PER_STEP):
            acc = jnp.zeros((tq, LANES), F32)
            for c in range(nk // MLA_KCHUNK):
                p = jnp.exp2(s_ref[hh, c] - row_max[hh]).astype(BF16)
                acc = acc + _dot(p, v_ref[0, c * MLA_KCHUNK:(c + 1) * MLA_KCHUNK,
                                          hh * LANES:(hh + 1) * LANES])
            den = acc[:, 64:65] if hh % 2 == 0 else acc[:, 0:1]
            outs.append(acc / den)
        o_ref[0] = jnp.concatenate(
            [jnp.where(lane < 64, outs[2 * j], outs[2 * j + 1]) for j in range(len(outs) // 2)],
            axis=1).astype(o_ref.dtype)

    @pl.when(qi == 0)
    def _():
        run(n_ctx)

    @pl.when(qi > 0)
    def _():
        run(n_keys)


def _mla_call(qcat, kcat, vext, n_ctx):
    B, S, _ = qcat.shape
    H = MLA_HEADS
    T = TOK_BLOCK
    assert n_ctx == T
    G = MLA_HEADS_PER_STEP
    kv = pl.BlockSpec((1, S, G * LANES), lambda b, hp, i: (b, 0, hp))
    return pl.pallas_call(
        functools.partial(_mla_kernel, n_ctx, S),
        grid=(B, H // G, S // T),
        in_specs=[pl.BlockSpec((1, T, G * LANES), lambda b, hp, i: (b, i, hp)), kv, kv],
        out_specs=pl.BlockSpec((1, T, (G // 2) * LANES), lambda b, hp, i: (b, i, hp)),
        out_shape=jax.ShapeDtypeStruct((B, S, (H // 2) * LANES), BF16),
        scratch_shapes=[pltpu.VMEM((G, S // MLA_KCHUNK, T, MLA_KCHUNK), F32)],
        compiler_params=_compiler_params(("parallel", "parallel", "arbitrary")),
        name="mla_attn",
    )(qcat, kcat, vext)


MERGE_SAMPLES = 4
PROJ_SAMPLES = 1


def _merge_kernel(alpha, ns, x_ref, oa_ref, ob_ref, sg_ref, woa_ref, wob_ref, wout_ref, g1_ref,
                  b1_ref, *refs):
    mod_refs, (wr_ref, x1_ref, h2_ref, lg_ref) = refs[:3 * ns], refs[3 * ns:]
    T, D = x_ref.shape[1], x_ref.shape[2]
    rows = lambda ref: ref[...].reshape(ns * T, ref.shape[2])
    sg = rows(sg_ref)
    t = (sg[:, :D].astype(F32) * _dot(rows(oa_ref), woa_ref[0])
         + sg[:, D:].astype(F32) * _dot(rows(ob_ref), wob_ref[0]))
    y = _dot(t.astype(BF16), wout_ref[0])
    h2s = []
    for n in range(ns):
        m2_ref, m3_ref, m4_ref = mod_refs[3 * n:3 * n + 3]
        x1 = _layer_norm(alpha * x_ref[n] + m2_ref[0] * y[n * T:(n + 1) * T],
                         g1_ref[0], b1_ref[0])
        x1_ref[n] = x1
        h2 = x1 * (1.0 + m4_ref[0]) + m3_ref[0]
        h2_ref[n] = _rows_to_tiles(h2)
        h2s.append(h2.astype(BF16))
    lg = _dot(jnp.concatenate(h2s, axis=0), wr_ref[0])
    for n in range(ns):
        lg_ref[n] = lg[n * T:(n + 1) * T].T[:N_EXPERTS]


def _merge_call(layer, alpha, x_res, oa, ob, sg, w_oa, w_ob, w_out, ln1_g, ln1_b, mods, w_r,
                n_ctx_blocks):
    B, S, D = x_res.shape
    T = TOK_BLOCK
    L = w_oa.shape[0]
    ns = MERGE_SAMPLES if B % MERGE_SAMPLES == 0 else 1

    def mod_spec(k, n):
        def imap(bp, i):
            row = jnp.where(i < n_ctx_blocks, B, ns * bp + n)
            return ((layer * ADA_ROWS + row) * 6 + k, 0, 0)
        return pl.BlockSpec((1, 1, D), imap)

    tok = lambda w: pl.BlockSpec((ns, T, w), lambda bp, i: (bp, i, 0))
    const = lambda shp: pl.BlockSpec(shp, lambda bp, i: (layer,) + (0,) * (len(shp) - 1),
                                     pipeline_mode=pl.Buffered(1))
    mod_specs = [mod_spec(k, n) for n in range(ns) for k in (2, 3, 4)]
    return pl.pallas_call(
        functools.partial(_merge_kernel, alpha, ns),
        grid=(B // ns, S // T),
        in_specs=[tok(D), tok(512), tok(512), tok(2048),
                  const((1, 512, D)), const((1, 512, D)), const((1, D, D)),
                  const((1, 1, D)), const((1, 1, D))] + mod_specs + [const((1, D, LANES))],
        out_specs=[tok(D), pl.BlockSpec((ns, T, SUBLANES, LANES), lambda bp, i: (bp, i, 0, 0)),
                   pl.BlockSpec((ns, N_EXPERTS, T), lambda bp, i: (bp, 0, i))],
        out_shape=[jax.ShapeDtypeStruct((B, S, D), F32),
                   jax.ShapeDtypeStruct((B, S, SUBLANES, LANES), F32),
                   jax.ShapeDtypeStruct((B, N_EXPERTS, S), F32)],
        compiler_params=_compiler_params(("parallel", "parallel")),
        name="merge",
    )(x_res, oa, ob, sg, w_oa, w_ob, w_out, ln1_g.reshape(L, 1, D), ln1_b.reshape(L, 1, D),
      *([mods] * (3 * ns)), w_r)


def _cumsum_lanes(x):
    n = x.shape[1]
    xb = x.astype(BF16)
    r = lax.broadcasted_iota(jnp.int32, (LANES, LANES), 0)
    c = lax.broadcasted_iota(jnp.int32, (LANES, LANES), 1)
    tri = jnp.where(r <= c, 1.0, 0.0).astype(BF16)
    tb = lax.broadcasted_iota(jnp.int32, (n, LANES), 0) // LANES
    kb = lax.broadcasted_iota(jnp.int32, (n, LANES), 1)
    before = jnp.where(tb < kb, 1.0, 0.0).astype(BF16)
    off = _dot(xb, before)
    outs = []
    for k in range(n // LANES):
        outs.append(_dot(xb[:, k * LANES:(k + 1) * LANES], tri) + off[:, k:k + 1])
    return outs


def _select_top(aff, cap):
    bits = pltpu.bitcast(aff, jnp.int32)
    thr = jnp.zeros((aff.shape[0], 1), jnp.int32)
    for bit in range(30, -1, -1):
        cand = thr | (1 << bit)
        cnt = jnp.sum(jnp.where(bits >= cand, 1.0, 0.0), axis=1, keepdims=True)
        thr = jnp.where(cnt >= cap, cand, thr)
    gt = bits > thr
    eq = jnp.where(bits == thr, 1.0, 0.0)
    need = cap - jnp.sum(jnp.where(gt, 1.0, 0.0), axis=1, keepdims=True)
    eq_rank = jnp.concatenate(_cumsum_lanes(eq), axis=1) - eq
    sel = jnp.where(jnp.logical_or(gt, jnp.logical_and(eq > 0.5, eq_rank < need)), 1.0, 0.0)
    return _cumsum_lanes(sel)


ROUTE_PAD = 1e6


def _route_kernel(n_ctx, cap_c, cap_l, lg_ref, aff_ref, idx_ref, cc_ref, cl_ref):
    lg = lg_ref[0]
    m = jnp.max(lg, axis=0, keepdims=True)
    ex = jnp.exp(lg - m)
    aff = ex / jnp.sum(ex, axis=0, keepdims=True)
    aff_ref[0] = aff
    for c_ref, blocks in ((cc_ref, _select_top(aff[:, :n_ctx], cap_c)),
                          (cl_ref, _select_top(aff[:, n_ctx:], cap_l))):
        c_ref[...] = jnp.full(c_ref.shape, ROUTE_PAD, F32)
        for k, blk in enumerate(blocks):
            for ex_i in range(N_EXPERTS):
                c_ref[ex_i, k:k + 1, :] = blk[ex_i:ex_i + 1, :]

    def slots(c_ref, e, n_blocks, width):
        cm = c_ref[e]
        jrow = lax.broadcasted_iota(jnp.int32, (1, width), 1).astype(F32)
        rows = max(n_blocks, SUBLANES)
        cend = cm[0:rows, LANES - 1:LANES]
        blk = jnp.sum(jnp.where(cend <= jrow, 1.0, 0.0), axis=0, keepdims=True)
        kio = lax.broadcasted_iota(jnp.int32, (LANES, width), 0).astype(F32)
        onehot = jnp.where(kio == blk, 1.0, 0.0).astype(BF16)
        cmt = cm.T
        hi = jnp.floor(cmt * (1.0 / 256.0))
        lo = cmt - 256.0 * hi
        cg = 256.0 * _dot(hi.astype(BF16), onehot) + _dot(lo.astype(BF16), onehot)
        inside = jnp.sum(jnp.where(cg <= jrow, 1.0, 0.0), axis=0, keepdims=True)
        return (blk * LANES + inside).astype(jnp.int32)

    def per_expert(e, carry):
        n_lat_blocks = (lg_ref.shape[2] - n_ctx) // LANES
        idx_ref[0, e, :, 0:cap_l] = slots(cl_ref, e, n_lat_blocks, cap_l) + n_ctx
        idx_ref[0, e, :, cap_l:cap_l + cap_c] = slots(cc_ref, e, n_ctx // LANES, LANES)[:, 0:cap_c]
        return carry

    lax.fori_loop(0, N_EXPERTS, per_expert, 0)


def _route_call(lg_t, n_ctx, cap_c, cap_l):
    B, E, S = lg_t.shape
    cap = cap_c + cap_l
    assert (S - n_ctx) // LANES <= LANES and cap_l % LANES == 0 and cap_c <= LANES
    return pl.pallas_call(
        functools.partial(_route_kernel, n_ctx, cap_c, cap_l),
        grid=(B,),
        in_specs=[pl.BlockSpec((1, E, S), lambda b: (b, 0, 0))],
        out_specs=[pl.BlockSpec((1, E, S), lambda b: (b, 0, 0)),
                   pl.BlockSpec((1, E, 1, cap), lambda b: (b, 0, 0, 0))],
        out_shape=[jax.ShapeDtypeStruct((B, E, S), F32),
                   jax.ShapeDtypeStruct((B, E, 1, cap), jnp.int32)],
        scratch_shapes=[pltpu.VMEM((E, LANES, LANES), F32), pltpu.VMEM((E, LANES, LANES), F32)],
        compiler_params=_compiler_params(("parallel",)),
        name="route",
    )(lg_t)


ROW_BATCH = 16
MOE_FF_CHUNK = 512


GATHER_EXPERTS = 2


def _gather_kernel(idx_ref, h_ref, xg_ref, tmp_ref):
    cap = xg_ref.shape[2]
    for ge in range(GATHER_EXPERTS):
        def body(g, carry):
            for r in range(ROW_BATCH):
                j = g * ROW_BATCH + r
                tmp_ref[j] = h_ref[0, idx_ref[0, 0, ge * cap + j]]
            return carry

        lax.fori_loop(0, cap // ROW_BATCH, body, 0)
        xg_ref[0, ge] = _tiles_to_rows(tmp_ref[...]).astype(xg_ref.dtype)


def _gather_call(idx, h2t):
    B, S = h2t.shape[0], h2t.shape[1]
    D = SUBLANES * LANES
    E, cap = idx.shape[1], idx.shape[2]
    G = GATHER_EXPERTS
    assert cap % ROW_BATCH == 0 and E % G == 0
    return pl.pallas_call(
        _gather_kernel,
        grid=(B, E // G),
        in_specs=[pl.BlockSpec((1, 1, G * cap), lambda b, e: (b * (E // G) + e, 0, 0),
                               memory_space=pltpu.SMEM),
                  pl.BlockSpec((1, S, SUBLANES, LANES), lambda b, e: (b, 0, 0, 0))],
        out_specs=pl.BlockSpec((1, G, cap, D), lambda b, e: (b, e, 0, 0)),
        out_shape=jax.ShapeDtypeStruct((B, E, cap, D), BF16),
        scratch_shapes=[pltpu.VMEM((cap, SUBLANES, LANES), F32)],
        compiler_params=_compiler_params(("parallel", "arbitrary")),
        name="moe_gather",
    )(idx.reshape(B * E // G, 1, G * cap), h2t)


def _moe_kernel(idx_ref, aff_ref, xg_ref, wg_ref, wu_ref, wd_ref, out_ref, y_ref):
    s = pl.program_id(1)
    n_exp = pl.num_programs(1) - 1
    cap = y_ref.shape[1]
    n_batches = cap // ROW_BATCH

    def scatter(slot, batches):
        new = None
        for g in batches:
            rows = _rows_to_tiles(y_ref[slot, g * ROW_BATCH:(g + 1) * ROW_BATCH, :])
            toks = [idx_ref[0, 0, g * ROW_BATCH + r] for r in range(ROW_BATCH)]
            old = [out_ref[0, t] for t in toks]
            for r, t in enumerate(toks):
                new = old[r] + rows[r] * aff_ref[0, 0, t]
                out_ref[0, t] = new
        return new

    def ffn(slot, prev_slot):
        ff = wg_ref.shape[3]
        n_dots = 3 * (ff // MOE_FF_CHUNK)
        per = -(-n_batches // n_dots)
        todo = list(range(n_batches)) if prev_slot is not None else []

        def weights(w):
            take, todo[:] = todo[:per], todo[per:]
            w = w.astype(BF16)
            if not take:
                return w
            last = scatter(prev_slot, take)
            bits = pltpu.bitcast(last, jnp.uint32)
            zero = pltpu.bitcast((bits >> 16) >> 16, F32)
            zero = jnp.concatenate([zero, zero], axis=0).astype(BF16)
            return w + jnp.tile(zero, (w.shape[0] // zero.shape[0], w.shape[1] // LANES))

        x = xg_ref[0, 0]
        y = None
        for f0 in range(0, ff, MOE_FF_CHUNK):
            f1 = f0 + MOE_FF_CHUNK
            a = _dot(x, weights(wg_ref[0, 0, :, f0:f1]))
            u = _dot(x, weights(wu_ref[0, 0, :, f0:f1]))
            hmid = (a * jax.nn.sigmoid(a) * u).astype(BF16)
            part = _dot(hmid, weights(wd_ref[0, 0, f0:f1, :]))
            y = part if y is None else y + part
        y_ref[slot] = y

    @pl.when(s == 0)
    def _():
        out_ref[...] = jnp.zeros(out_ref.shape, out_ref.dtype)
        ffn(0, None)

    @pl.when(jnp.logical_and(s > 0, s < n_exp))
    def _():
        slot = s % 2
        ffn(slot, 1 - slot)

    @pl.when(s == n_exp)
    def _():
        scatter((n_exp - 1) % 2, range(n_batches))


def _moe_call(layer, idx, aff, xg, w_gate, w_up, w_down, S):
    B, E, cap, D = xg.shape
    F = w_gate.shape[3]
    cur = lambda s: jnp.minimum(s, E - 1)
    prev = lambda s: jnp.maximum(s - 1, 0)
    wspec = lambda shp: pl.BlockSpec(shp, lambda b, s: (layer, cur(s), 0, 0))
    return pl.pallas_call(
        _moe_kernel,
        grid=(B, E + 1),
        in_specs=[pl.BlockSpec((1, 1, cap), lambda b, s: (b * E + prev(s), 0, 0), memory_space=pltpu.SMEM),
                  pl.BlockSpec((1, 1, S), lambda b, s: (b * E + prev(s), 0, 0), memory_space=pltpu.SMEM),
                  pl.BlockSpec((1, 1, cap, D), lambda b, s: (b, cur(s), 0, 0)),
                  wspec((1, 1, D, F)), wspec((1, 1, D, F)), wspec((1, 1, F, D))],
        out_specs=pl.BlockSpec((1, S, SUBLANES, LANES), lambda b, s: (b, 0, 0, 0),
                               pipeline_mode=pl.Buffered(1)),
        out_shape=jax.ShapeDtypeStruct((B, S, SUBLANES, LANES), F32),
        scratch_shapes=[pltpu.VMEM((2, cap, D), F32)],
        compiler_params=_compiler_params(("parallel", "arbitrary")),
        name="moe_ffn",
    )(idx.reshape(B * E, 1, cap), aff.reshape(B * E, 1, S), xg, w_gate, w_up, w_down)


def _final_kernel(alpha, ns, x_ref, moe_ref, g_ref, b_ref, *refs):
    o_ref = refs[ns]
    for n in range(ns):
        o_ref[n] = _layer_norm(alpha * x_ref[n] + refs[n][0] * _tiles_to_rows(moe_ref[n]),
                               g_ref[0], b_ref[0])


def _final_call(layer, alpha, x1, moe, ln2_g, ln2_b, mods, n_ctx):
    B, S, D = x1.shape
    T = TOK_BLOCK
    L = ln2_g.shape[0]
    ncb = n_ctx // T
    ns = MERGE_SAMPLES if B % MERGE_SAMPLES == 0 else 1
    tok_in = pl.BlockSpec((ns, T, D), lambda b, i: (b, i + ncb, 0))
    const = pl.BlockSpec((1, 1, D), lambda b, i: (layer, 0, 0))
    gate = lambda n: pl.BlockSpec(
        (1, 1, D), lambda b, i: ((layer * ADA_ROWS + ns * b + n) * 6 + 5, 0, 0))
    return pl.pallas_call(
        functools.partial(_final_kernel, alpha, ns),
        grid=(B // ns, (S - n_ctx) // T),
        in_specs=[tok_in, pl.BlockSpec((ns, T, SUBLANES, LANES), lambda b, i: (b, i + ncb, 0, 0)),
                  const, const] + [gate(n) for n in range(ns)],
        out_specs=pl.BlockSpec((ns, T, D), lambda b, i: (b, i, 0)),
        out_shape=jax.ShapeDtypeStruct((B, S - n_ctx, D), F32),
        compiler_params=_compiler_params(("parallel", "parallel")),
        name="final_norm",
    )(x1, moe, ln2_g.reshape(L, 1, D), ln2_b.reshape(L, 1, D), *([mods] * ns))


def _prep_weights(w_in, w_uq, w_ukv, w_router):
    L, D, _ = w_in.shape
    krs = jnp.concatenate([jnp.zeros((L, D, 64), w_in.dtype), w_in[..., 512:544],
                           jnp.zeros((L, D, 32), w_in.dtype)], axis=-1)
    w_in_p = jnp.concatenate([w_in[..., 0:512], krs, w_in[..., 544:]], axis=-1).astype(BF16)

    kvr = w_ukv.reshape(L, MLA_KV_RANK, MLA_HEADS, MLA_NOPE + MLA_V)
    kn, vv = kvr[..., :MLA_NOPE], kvr[..., MLA_NOPE:]
    zk = jnp.zeros_like(kn)
    w_k = jnp.concatenate([kn, zk], axis=-1).reshape(L, MLA_KV_RANK, MLA_HEADS * LANES)
    v_even = jnp.concatenate([vv, zk], axis=-1)
    v_odd = jnp.concatenate([zk, vv], axis=-1)
    odd = (jnp.arange(MLA_HEADS) % 2 == 1)[None, None, :, None]
    w_v = jnp.where(odd, v_odd, v_even).reshape(L, MLA_KV_RANK, MLA_HEADS * LANES)
    w_kv = jnp.concatenate([w_k, w_v], axis=-1).astype(BF16)

    qr = w_uq.reshape(L, MLA_Q_RANK, MLA_HEADS, MLA_NOPE + MLA_ROPE)
    w_q = jnp.concatenate([qr, jnp.zeros((L, MLA_Q_RANK, MLA_HEADS, 32), w_uq.dtype)], axis=-1)
    w_q = w_q.reshape(L, MLA_Q_RANK, MLA_HEADS * LANES).astype(BF16)

    w_r = jnp.concatenate(
        [w_router, jnp.zeros((L, D, LANES - N_EXPERTS), w_router.dtype)], axis=-1).astype(BF16)
    return w_in_p, w_kv, w_q, w_r


def _ones_columns():
    lane = jnp.arange(LANES)
    even = (lane == 64).astype(F32)
    odd = (lane == 0).astype(F32)
    win = jnp.concatenate([even, odd, even, odd])
    mla = jnp.concatenate([even, odd] * (MLA_HEADS // 2))
    return jnp.concatenate([win, mla])[None, :]


def _rope_tables(n_ctx, n_lat):
    pos = jnp.arange(n_lat)
    rowp = (pos // GRID_W).astype(F32)
    colp = (pos % GRID_W).astype(F32)

    def pattern(rot_dim):
        nf = rot_dim // 4
        inv = ROPE_THETA ** (-jnp.arange(nf, dtype=F32) / nf)
        ar, ac = rowp[:, None] * inv, colp[:, None] * inv
        cos = jnp.concatenate([jnp.cos(ar), jnp.cos(ar), jnp.cos(ac), jnp.cos(ac)], axis=1)
        sin = jnp.concatenate([-jnp.sin(ar), jnp.sin(ar), -jnp.sin(ac), jnp.sin(ac)], axis=1)
        return cos, sin

    cw, sw = pattern(WIN_HEAD_DIM)
    cw, sw = jnp.tile(cw, (1, 2)), jnp.tile(sw, (1, 2))
    cm, sm = pattern(MLA_ROPE)
    one64, zero64 = jnp.ones((n_lat, 64), F32), jnp.zeros((n_lat, 64), F32)
    cm = jnp.concatenate([one64, cm, one64[:, :32]], axis=1)
    sm = jnp.concatenate([zero64, sm, zero64[:, :32]], axis=1)
    sq_w, sq_m = WIN_SCALE * LOG2E, MLA_SCALE * LOG2E
    lat = jnp.concatenate([cw * sq_w, sw * sq_w, cw, sw, cm * sq_m, sm * sq_m, cm, sm], axis=1)
    ones, zeros = jnp.ones((n_ctx, LANES), F32), jnp.zeros((n_ctx, LANES), F32)
    ctx = jnp.concatenate([ones * sq_w, zeros, ones, zeros, ones * sq_m, zeros, ones, zeros], axis=1)
    return jnp.concatenate([ctx, lat], axis=0)


def kernel(x, c, ctx, c_ctx, w_ada, b_ada, w_in, attn_sink, mla_q_norm, mla_kv_norm, w_uq, w_ukv,
           w_oa, w_ob, w_out, ln1_g, ln1_b, w_router, w_exp_gate, w_exp_up, w_exp_down, ln2_g, ln2_b):
    B, n_lat, D = x.shape
    n_ctx = ctx.shape[1]
    depth = w_in.shape[0]
    S = n_ctx + n_lat
    alpha = (2 * depth) ** 0.25
    assert D == D_MODEL and n_ctx == TOK_BLOCK and n_lat % TOK_BLOCK == 0 and B + 1 <= ADA_ROWS
    cap_c = CAPACITY_FACTOR * n_ctx // N_EXPERTS
    cap_l = CAPACITY_FACTOR * n_lat // N_EXPERTS
    assert cap_l % 256 == 0 and cap_c % 8 == 0
    ncb = n_ctx // TOK_BLOCK

    cond = jnp.concatenate([c, c_ctx[None], jnp.zeros((ADA_ROWS - B - 1, D), F32)], axis=0)
    mods = _ada_call(cond.T, w_ada, b_ada, B + 1).reshape(depth * ADA_ROWS * 6, 1, D)

    w_in_p, w_kv, w_q, w_r = _prep_weights(w_in, w_uq, w_ukv, w_router)
    w_oa_b, w_ob_b, w_out_b = w_oa.astype(BF16), w_ob.astype(BF16), w_out.astype(BF16)
    tab = _rope_tables(n_ctx, n_lat)
    vones = _ones_columns()

    x_all = (ctx, x)
    moe = None
    for l in range(depth):
        x_res, (qwin, kwin, vwin, qcat, kcat, vext, sg) = _proj_call(
            l, alpha, x_all, moe, ln2_g, ln2_b, mods, tab, w_in_p, mla_kv_norm, mla_q_norm,
            w_kv, w_q, vones, ncb)
        oa = _win_call(l, attn_sink, qwin, kwin, vwin, n_ctx)
        ob = _mla_call(qcat, kcat, vext, n_ctx)
        x1, h2, lg_t = _merge_call(l, alpha, x_res, oa, ob, sg, w_oa_b, w_ob_b, w_out_b,
                                   ln1_g, ln1_b, mods, w_r, ncb)
        aff, idx4 = _route_call(lg_t, n_ctx, cap_c, cap_l)
        idx = idx4.reshape(B, N_EXPERTS, cap_c + cap_l)
        xg = _gather_call(idx, h2)
        moe = _moe_call(l, idx, aff, xg, w_exp_gate, w_exp_up, w_exp_down, S)
        x_all = x1
    return _final_call(depth - 1, alpha, x_all, moe, ln2_g, ln2_b, mods, n_ctx)
```

```python
import functools
import math

import jax
import jax.numpy as jnp
from jax import lax
from jax.experimental import pallas as pl
from jax.experimental.pallas import tpu as pltpu

D_MODEL = 1024
GRID_W = 64
WIN_HEADS = 8
WIN_KV_HEADS = 2
WIN_HEAD_DIM = 64
BAND = 128
MLA_HEADS = 8
MLA_Q_RANK = 384
MLA_KV_RANK = 256
MLA_NOPE = 64
MLA_ROPE = 32
MLA_V = 64
N_EXPERTS = 16
CAPACITY_FACTOR = 2
ROPE_THETA = 10000.0
LN_EPS = 1e-5
RMS_EPS = 1e-6
NEG_INF = -1e30
LOG2E = math.log2(math.e)
WIN_SCALE = WIN_HEAD_DIM ** -0.5
MLA_SCALE = (MLA_NOPE + MLA_ROPE) ** -0.5

LANES = 128
TOK_BLOCK = 256
VMEM_LIMIT = 56 * 1024 * 1024

BF16 = jnp.bfloat16
F32 = jnp.float32

SEG_KWIN = (0, 128)
SEG_VWIN = (128, 256)
SEG_CKV = (256, 512)
SEG_KR = (512, 640)
SEG_QWIN = (640, 1152)
SEG_CQ = (1152, 1536)
SEG_GATE = (1536, 3584)
IN_COLS_PAD = 3584


def _dot(a, b):
    return jnp.dot(a, b, preferred_element_type=F32)


def _dot_nt(a, b):
    return lax.dot_general(a, b, (((1,), (1,)), ((), ())), preferred_element_type=F32)


def _layer_norm(z, g, b):
    mu = jnp.mean(z, axis=-1, keepdims=True)
    zc = z - mu
    var = jnp.mean(zc * zc, axis=-1, keepdims=True)
    return zc * lax.rsqrt(var + LN_EPS) * g + b


def _rms_norm(x, g):
    return x * lax.rsqrt(jnp.mean(x * x, axis=-1, keepdims=True) + RMS_EPS) * g


def _rope_slab(x, cos, sin, half):
    lane = lax.broadcasted_iota(jnp.int32, x.shape, 1)
    partner = jnp.where((lane & half) == 0,
                        pltpu.roll(x, LANES - half, 1), pltpu.roll(x, half, 1))
    return x * cos + partner * sin


SUBLANES = 8


def _rows_to_tiles(y):
    n = y.shape[0]
    y3 = pltpu.einshape("r(sl)->srl", y, s=SUBLANES)
    y4 = y3.reshape(SUBLANES, n // SUBLANES, SUBLANES, LANES)
    return jnp.transpose(y4, (1, 2, 0, 3)).reshape(n, SUBLANES, LANES)


def _tiles_to_rows(x3):
    n = x3.shape[0]
    x4 = x3.reshape(n // SUBLANES, SUBLANES, SUBLANES, LANES)
    xs = jnp.transpose(x4, (2, 0, 1, 3)).reshape(SUBLANES, n, LANES)
    return pltpu.einshape("srl->r(sl)", xs)


def _compiler_params(sem):
    return pltpu.CompilerParams(dimension_semantics=sem, vmem_limit_bytes=VMEM_LIMIT)


ADA_ROWS = 8
ADA_TN = 512


def _ada_kernel(n_rows, condt_ref, w_ref, b_ref, o_ref):
    ct = condt_ref[...]
    st = ct * jax.nn.sigmoid(ct)
    w = w_ref[0]
    rows = []
    for r in range(n_rows):
        rows.append(jnp.sum(w * st[:, r:r + 1], axis=0, keepdims=True) + b_ref[0])
    rows.append(jnp.zeros((ADA_ROWS - n_rows, w.shape[1]), F32))
    o_ref[0] = jnp.concatenate(rows, axis=0)


def _ada_call(cond_t, w_ada, b_ada, n_rows):
    L, D, N = w_ada.shape
    return pl.pallas_call(
        functools.partial(_ada_kernel, n_rows),
        grid=(L, N // ADA_TN),
        in_specs=[
            pl.BlockSpec((D, ADA_ROWS), lambda l, j: (0, 0)),
            pl.BlockSpec((1, D, ADA_TN), lambda l, j: (l, 0, j)),
            pl.BlockSpec((1, 1, ADA_TN), lambda l, j: (l, 0, j)),
        ],
        out_specs=pl.BlockSpec((1, ADA_ROWS, ADA_TN), lambda l, j: (l, 0, j)),
        out_shape=jax.ShapeDtypeStruct((L, ADA_ROWS, N), F32),
        compiler_params=_compiler_params(("parallel", "parallel")),
        name="adaln_mod",
    )(cond_t, w_ada, b_ada.reshape(L, 1, N))


def _proj_kernel(has_ln, alpha, n_ctx_blocks, ns, *refs):
    refs = list(refs)
    if has_ln:
        x_ref, moe_ref, g2_ref, b2_ref = refs[:4]
        m5_refs, refs = refs[4:4 + ns], refs[4 + ns:]
    else:
        (ctx_ref, x_ref), refs = refs[:2], refs[2:]
    mod_refs, refs = refs[:2 * ns], refs[2 * ns:]
    (tab_ref, win_ref, gkv_ref, gq_ref, wkv_ref, wq_ref, vones_ref,
     xres_ref, qwin_ref, kwin_ref, vwin_ref, qcat_ref, kcat_ref, vext_ref, sg_ref) = refs
    T = x_ref.shape[1]
    hbs = []
    for n in range(ns):
        if has_ln:
            z = alpha * x_ref[n] + m5_refs[n][0] * _tiles_to_rows(moe_ref[n])
            x = _layer_norm(z, g2_ref[0], b2_ref[0])
        else:
            x = jnp.where(pl.program_id(0) < n_ctx_blocks, ctx_ref[n], x_ref[n])
        xres_ref[n] = x
        sh_ref, sc_ref = mod_refs[2 * n:2 * n + 2]
        hbs.append((x * (1.0 + sc_ref[0]) + sh_ref[0]).astype(BF16))
    hb = jnp.concatenate(hbs, axis=0)
    split = lambda v: v.reshape(ns, T, v.shape[1])

    def seg(s):
        return _dot(hb, win_ref[0, :, s[0]:s[1]])

    def tab(i):
        t = tab_ref[:, i * LANES:(i + 1) * LANES]
        return jnp.concatenate([t] * ns, axis=0)

    cq_w, sq_w, ck_w, sk_w, cq_m, sq_m, ck_m, sk_m = (tab(i) for i in range(8))

    lane_lo = lax.broadcasted_iota(jnp.int32, (hb.shape[0], LANES), 1) < WIN_HEAD_DIM
    half_swap = lambda t: pltpu.roll(t, WIN_HEAD_DIM, 1)

    ckv = _rms_norm(seg(SEG_CKV), gkv_ref[0]).astype(BF16)
    cq = _rms_norm(seg(SEG_CQ), gq_ref[0]).astype(BF16)

    qw = seg(SEG_QWIN)
    q_slabs = []
    for s2 in range(WIN_HEADS // 2):
        pair = _rope_slab(qw[:, s2 * LANES:(s2 + 1) * LANES], cq_w, sq_w, 16)
        pair_sw = half_swap(pair)
        if 2 * s2 < WIN_HEADS // WIN_KV_HEADS:
            q_slabs += [jnp.where(lane_lo, pair, 0.0), jnp.where(lane_lo, pair_sw, 0.0)]
        else:
            q_slabs += [jnp.where(lane_lo, 0.0, pair_sw), jnp.where(lane_lo, 0.0, pair)]
    qwin_ref[...] = split(jnp.concatenate(q_slabs, axis=1).astype(BF16))

    kv = _dot(ckv, wkv_ref[0])
    kr = _rope_slab(seg(SEG_KR), ck_m, sk_m, 8)
    qm = _dot(cq, wq_ref[0])
    for h in range(MLA_HEADS):
        sl = slice(h * LANES, (h + 1) * LANES)
        kcat_ref[:, :, sl] = split((kv[:, sl] + kr).astype(BF16))
        vext_ref[:, :, sl] = split(
            (kv[:, 1024 + h * LANES:1024 + (h + 1) * LANES]
             + vones_ref[:, 512 + h * LANES:512 + (h + 1) * LANES]).astype(BF16))
        qcat_ref[:, :, sl] = split(_rope_slab(qm[:, sl], cq_m, sq_m, 8).astype(BF16))

    sg_ref[...] = split(jax.nn.sigmoid(seg(SEG_GATE)).astype(BF16))
    va = seg(SEG_VWIN)
    va_sw = half_swap(va)
    vwin_ref[...] = split((jnp.concatenate(
        [jnp.where(lane_lo, va, 0.0), jnp.where(lane_lo, 0.0, va_sw),
         jnp.where(lane_lo, va_sw, 0.0), jnp.where(lane_lo, 0.0, va)], axis=1)
        + vones_ref[:, 0:512]).astype(BF16))
    kwin_ref[...] = split(_rope_slab(seg(SEG_KWIN), ck_w, sk_w, 16).astype(BF16))


def _proj_call(layer, alpha, x_all, moe, ln2_g, ln2_b, mods, tab, w_in_p, g_kvn, g_qn, w_kv, w_q,
               vones, n_ctx_blocks):
    T = TOK_BLOCK
    has_ln = moe is not None
    if has_ln:
        B, S, D = x_all.shape
    else:
        ctx_in, x_in = x_all
        B, D = x_in.shape[0], x_in.shape[2]
        S = ctx_in.shape[1] + x_in.shape[1]
    L = w_in_p.shape[0]
    ns = PROJ_SAMPLES if B % PROJ_SAMPLES == 0 else 1

    def mod_spec(k, lyr, n):
        def imap(i, b):
            row = jnp.where(i < n_ctx_blocks, B, ns * b + n)
            return ((lyr * ADA_ROWS + row) * 6 + k, 0, 0)
        return pl.BlockSpec((1, 1, D), imap)

    tok = lambda w: pl.BlockSpec((ns, T, w), lambda i, b: (b, i, 0))
    headed = tok(MLA_HEADS * LANES)
    const = lambda shp: pl.BlockSpec(shp, lambda i, b: (layer,) + (0,) * (len(shp) - 1),
                                     pipeline_mode=pl.Buffered(1))

    if has_ln:
        prev = pl.BlockSpec((1, 1, D), lambda i, b: (layer - 1, 0, 0))
        tiled = pl.BlockSpec((ns, T, SUBLANES, LANES), lambda i, b: (b, i, 0, 0))
        in_specs = [tok(D), tiled, prev, prev] + [mod_spec(5, layer - 1, n) for n in range(ns)]
        args = [x_all, moe, ln2_g.reshape(L, 1, D), ln2_b.reshape(L, 1, D)] + [mods] * ns
    else:
        assert ctx_in.shape[1] == n_ctx_blocks * T
        in_specs = [pl.BlockSpec((ns, T, D), lambda i, b: (b, jnp.minimum(i, n_ctx_blocks - 1), 0)),
                    pl.BlockSpec((ns, T, D), lambda i, b: (b, jnp.maximum(i - n_ctx_blocks, 0), 0))]
        args = [ctx_in, x_in]
    in_specs += [mod_spec(k, layer, n) for n in range(ns) for k in (0, 1)]
    args += [mods] * (2 * ns)
    in_specs += [
        pl.BlockSpec((T, 8 * LANES), lambda i, b: (i, 0)),
        const((1, D, IN_COLS_PAD)),
        const((1, 1, MLA_KV_RANK)), const((1, 1, MLA_Q_RANK)),
        const((1, MLA_KV_RANK, 2048)), const((1, MLA_Q_RANK, 1024)),
        pl.BlockSpec((1, 1536), lambda i, b: (0, 0)),
    ]
    args += [tab, w_in_p, g_kvn.reshape(L, 1, -1), g_qn.reshape(L, 1, -1), w_kv, w_q, vones]

    out_specs, out_shape = [tok(D)], [jax.ShapeDtypeStruct((B, S, D), F32)]
    out_specs += [tok(WIN_HEADS * LANES), tok(LANES), tok(512), headed, headed, headed, tok(2048)]
    out_shape += [jax.ShapeDtypeStruct((B, S, WIN_HEADS * LANES), BF16),
                  jax.ShapeDtypeStruct((B, S, LANES), BF16),
                  jax.ShapeDtypeStruct((B, S, 512), BF16)]
    out_shape += [jax.ShapeDtypeStruct((B, S, MLA_HEADS * LANES), BF16)] * 3
    out_shape += [jax.ShapeDtypeStruct((B, S, 2048), BF16)]

    outs = pl.pallas_call(
        functools.partial(_proj_kernel, has_ln, alpha, n_ctx_blocks, ns),
        grid=(S // T, B // ns),
        in_specs=in_specs, out_specs=out_specs, out_shape=out_shape,
        compiler_params=_compiler_params(("parallel", "parallel")),
        name="proj",
    )(*args)
    return outs[0], outs[1:]


WIN_HEAD_ORDER = (0, 2, 1, 3, 4, 6, 5, 7)
WIN_SAMPLES = 4


def _win_kernel(layer, n_ctx, n_steps, sink_ref, q_ref, kc_ref, vc_ref, kp_ref, km_ref, kn_ref,
                vp_ref, vm_ref, vn_ref, o_ref, kall_ref, vall_ref, p_ref):
    i = pl.program_id(1)
    row = lax.broadcasted_iota(jnp.int32, (BAND, LANES), 0)
    col = lax.broadcasted_iota(jnp.int32, (BAND, LANES), 1)
    lane_lo = col < 64
    ns = q_ref.shape[0]
    for n in range(ns):
        kall_ref[n, 0:n_ctx, :] = kc_ref[n]
        vall_ref[n, 0:n_ctx, :] = vc_ref[n]

    def scores(n, qb, r0, nk):
        q = q_ref[n, qb * BAND:(qb + 1) * BAND, :]
        q8 = jnp.concatenate([q[:, h * LANES:(h + 1) * LANES] for h in WIN_HEAD_ORDER], axis=0)
        return _dot_nt(q8, kall_ref[n, r0:r0 + nk, :])

    def softmax(n, qb, nk, masks, s):
        e_sink = []
        for c, h in enumerate(WIN_HEAD_ORDER):
            sc = s[c * BAND:(c + 1) * BAND, :]
            if masks:
                blocks = []
                for kb in range(nk // BAND):
                    blk = sc[:, kb * BAND:(kb + 1) * BAND]
                    blocks.append(jnp.where(masks[kb], blk, NEG_INF) if kb in masks else blk)
                sc = jnp.concatenate(blocks, axis=1)
            sink = sink_ref[layer, h] * LOG2E
            m = jnp.maximum(jnp.max(sc, axis=1, keepdims=True), sink)
            p_ref[n, qb, c * BAND:(c + 1) * BAND, 0:nk] = jnp.exp2(sc - m).astype(BF16)
            e_sink.append(jnp.exp2(sink - m))
        return e_sink

    def values(n, qb, r0, nk, e_sink):
        pairs = []
        for pi in range(4):
            acc = _dot(p_ref[n, qb, pi * 2 * BAND:(pi + 1) * 2 * BAND, 0:nk],
                       vall_ref[n, r0:r0 + nk, pi * LANES:(pi + 1) * LANES])
            es = jnp.concatenate([e_sink[2 * pi], e_sink[2 * pi + 1]], axis=0)
            den = (acc[:, 64:65] if pi % 2 == 0 else acc[:, 0:1]) + es
            pairs.append(acc / den)
        slabs = [jnp.where(lane_lo, pairs[0][:BAND], pairs[1][:BAND]),
                 jnp.where(lane_lo, pairs[0][BAND:], pairs[1][BAND:]),
                 jnp.where(lane_lo, pairs[2][:BAND], pairs[3][:BAND]),
                 jnp.where(lane_lo, pairs[2][BAND:], pairs[3][BAND:])]
        o_ref[n, qb * BAND:(qb + 1) * BAND, :] = jnp.concatenate(slabs, axis=1).astype(o_ref.dtype)

    def attend_both(args0, args1):
        (r0, nk0, m0), (r1, nk1, m1) = args0, args1
        s = [(scores(n, 0, r0, nk0), scores(n, 1, r1, nk1)) for n in range(ns)]
        e = [(softmax(n, 0, nk0, m0, s[n][0]), softmax(n, 1, nk1, m1, s[n][1])) for n in range(ns)]
        for n in range(ns):
            values(n, 0, r0, nk0, e[n][0])
            values(n, 1, r1, nk1, e[n][1])

    @pl.when(i == 0)
    def _():
        attend_both((0, n_ctx, None), (0, n_ctx, None))

    @pl.when(i > 0)
    def _():
        for n in range(ns):
            r = n_ctx
            for k_ref, v_ref, rows in ((kp_ref, vp_ref, BAND), (km_ref, vm_ref, 2 * BAND),
                                       (kn_ref, vn_ref, BAND), (kc_ref, vc_ref, n_ctx)):
                kall_ref[n, r:r + rows, :] = k_ref[n]
                vall_ref[n, r:r + rows, :] = v_ref[n]
                r += rows
        off_first = jnp.where(i == 1, 2 * LANES, 0)
        off_last = jnp.where(i == n_steps - 1, 2 * LANES, 0)
        cb = n_ctx // BAND
        attend_both((0, n_ctx + 3 * BAND, {cb: col >= row + off_first, cb + 2: col <= row}),
                    (n_ctx + BAND, n_ctx + 3 * BAND, {0: col >= row, 2: col <= row - off_last}))


def _win_call(layer, sink, qwin, kwin, vwin, n_ctx):
    B, S, _ = qwin.shape
    nb = S // BAND
    ncb = n_ctx // BAND
    assert n_ctx == 2 * BAND and nb % 2 == 0
    nk = n_ctx + 3 * BAND
    ns = WIN_SAMPLES if B % WIN_SAMPLES == 0 else 1
    one = lambda w, f: pl.BlockSpec((ns, BAND, w), lambda b, i: (b, jnp.clip(f(i), ncb, nb - 1), 0))
    two = lambda w: pl.BlockSpec((ns, 2 * BAND, w), lambda b, i: (b, i, 0))
    ctx = lambda w: pl.BlockSpec((ns, n_ctx, w), lambda b, i: (b, 0, 0))
    prev, nxt = (lambda i: 2 * i - 1), (lambda i: 2 * i + 2)
    return pl.pallas_call(
        functools.partial(_win_kernel, layer, n_ctx, nb // 2),
        grid=(B // ns, nb // 2),
        in_specs=[pl.BlockSpec(memory_space=pltpu.SMEM), two(WIN_HEADS * LANES),
                  ctx(LANES), ctx(512), one(LANES, prev), two(LANES), one(LANES, nxt),
                  one(512, prev), two(512), one(512, nxt)],
        out_specs=two(512),
        out_shape=jax.ShapeDtypeStruct((B, S, 512), BF16),
        scratch_shapes=[pltpu.VMEM((ns, 2 * n_ctx + 4 * BAND, LANES), BF16),
                        pltpu.VMEM((ns, 2 * n_ctx + 4 * BAND, 512), BF16),
                        pltpu.VMEM((ns, 2, WIN_HEADS * BAND, nk), BF16)],
        compiler_params=_compiler_params(("parallel", "parallel")),
        name="win_attn",
    )(sink, qwin, kwin, vwin, kwin, kwin, kwin, vwin, vwin, vwin)


MLA_KCHUNK = 256
MLA_HEADS_PER_STEP = 4


def _mla_kernel(n_ctx, n_keys, q_ref, k_ref, v_ref, o_ref, s_ref):
    qi = pl.program_id(2)
    tq = q_ref.shape[1]
    lane = lax.broadcasted_iota(jnp.int32, (tq, LANES), 1)

    def run(nk):
        row_max = []
        for hh in range(MLA_HEADS_PER_STEP):
            hl = slice(hh * LANES, (hh + 1) * LANES)
            q = q_ref[0, :, hl]
            mrun = None
            for c in range(nk // MLA_KCHUNK):
                s = _dot_nt(q, k_ref[0, c * MLA_KCHUNK:(c + 1) * MLA_KCHUNK, hl])
                s_ref[hh, c] = s
                mc = jnp.maximum(s[:, :LANES], s[:, LANES:])
                mrun = mc if mrun is None else jnp.maximum(mrun, mc)
            row_max.append(jnp.max(mrun, axis=1, keepdims=True))
        outs = []
        for hh in range(MLA_HEADS_PER_STEP):
            acc = jnp.zeros((tq, LANES), F32)
            for c in range(nk // MLA_KCHUNK):
                p = jnp.exp2(s_ref[hh, c] - row_max[hh]).astype(BF16)
                acc = acc + _dot(p, v_ref[0, c * MLA_KCHUNK:(c + 1) * MLA_KCHUNK,
                                          hh * LANES:(hh + 1) * LANES])
            den = acc[:, 64:65] if hh % 2 == 0 else acc[:, 0:1]
            outs.append(acc / den)
        o_ref[0] = jnp.concatenate(
            [jnp.where(lane < 64, outs[2 * j], outs[2 * j + 1]) for j in range(len(outs) // 2)],
            axis=1).astype(o_ref.dtype)

    @pl.when(qi == 0)
    def _():
        run(n_ctx)

    @pl.when(qi > 0)
    def _():
        run(n_keys)


def _mla_call(qcat, kcat, vext, n_ctx):
    B, S, _ = qcat.shape
    H = MLA_HEADS
    T = TOK_BLOCK
    assert n_ctx == T
    G = MLA_HEADS_PER_STEP
    kv = pl.BlockSpec((1, S, G * LANES), lambda b, hp, i: (b, 0, hp))
    return pl.pallas_call(
        functools.partial(_mla_kernel, n_ctx, S),
        grid=(B, H // G, S // T),
        in_specs=[pl.BlockSpec((1, T, G * LANES), lambda b, hp, i: (b, i, hp)), kv, kv],
        out_specs=pl.BlockSpec((1, T, (G // 2) * LANES), lambda b, hp, i: (b, i, hp)),
        out_shape=jax.ShapeDtypeStruct((B, S, (H // 2) * LANES), BF16),
        scratch_shapes=[pltpu.VMEM((G, S // MLA_KCHUNK, T, MLA_KCHUNK), F32)],
        compiler_params=_compiler_params(("parallel", "parallel", "arbitrary")),
        name="mla_attn",
    )(qcat, kcat, vext)


MERGE_SAMPLES = 4
PROJ_SAMPLES = 1


def _merge_kernel(alpha, ns, x_ref, oa_ref, ob_ref, sg_ref, woa_ref, wob_ref, wout_ref, g1_ref,
                  b1_ref, *refs):
    mod_refs, (wr_ref, x1_ref, h2_ref, lg_ref) = refs[:3 * ns], refs[3 * ns:]
    T, D = x_ref.shape[1], x_ref.shape[2]
    rows = lambda ref: ref[...].reshape(ns * T, ref.shape[2])
    sg = rows(sg_ref)
    t = (sg[:, :D].astype(F32) * _dot(rows(oa_ref), woa_ref[0])
         + sg[:, D:].astype(F32) * _dot(rows(ob_ref), wob_ref[0]))
    y = _dot(t.astype(BF16), wout_ref[0])
    h2s = []
    for n in range(ns):
        m2_ref, m3_ref, m4_ref = mod_refs[3 * n:3 * n + 3]
        x1 = _layer_norm(alpha * x_ref[n] + m2_ref[0] * y[n * T:(n + 1) * T],
                         g1_ref[0], b1_ref[0])
        x1_ref[n] = x1
        h2 = x1 * (1.0 + m4_ref[0]) + m3_ref[0]
        h2_ref[n] = _rows_to_tiles(h2)
        h2s.append(h2.astype(BF16))
    lg = _dot(jnp.concatenate(h2s, axis=0), wr_ref[0])
    for n in range(ns):
        lg_ref[n] = lg[n * T:(n + 1) * T].T[:N_EXPERTS]


def _merge_call(layer, alpha, x_res, oa, ob, sg, w_oa, w_ob, w_out, ln1_g, ln1_b, mods, w_r,
                n_ctx_blocks):
    B, S, D = x_res.shape
    T = TOK_BLOCK
    L = w_oa.shape[0]
    ns = MERGE_SAMPLES if B % MERGE_SAMPLES == 0 else 1

    def mod_spec(k, n):
        def imap(bp, i):
            row = jnp.where(i < n_ctx_blocks, B, ns * bp + n)
            return ((layer * ADA_ROWS + row) * 6 + k, 0, 0)
        return pl.BlockSpec((1, 1, D), imap)

    tok = lambda w: pl.BlockSpec((ns, T, w), lambda bp, i: (bp, i, 0))
    const = lambda shp: pl.BlockSpec(shp, lambda bp, i: (layer,) + (0,) * (len(shp) - 1),
                                     pipeline_mode=pl.Buffered(1))
    mod_specs = [mod_spec(k, n) for n in range(ns) for k in (2, 3, 4)]
    return pl.pallas_call(
        functools.partial(_merge_kernel, alpha, ns),
        grid=(B // ns, S // T),
        in_specs=[tok(D), tok(512), tok(512), tok(2048),
                  const((1, 512, D)), const((1, 512, D)), const((1, D, D)),
                  const((1, 1, D)), const((1, 1, D))] + mod_specs + [const((1, D, LANES))],
        out_specs=[tok(D), pl.BlockSpec((ns, T, SUBLANES, LANES), lambda bp, i: (bp, i, 0, 0)),
                   pl.BlockSpec((ns, N_EXPERTS, T), lambda bp, i: (bp, 0, i))],
        out_shape=[jax.ShapeDtypeStruct((B, S, D), F32),
                   jax.ShapeDtypeStruct((B, S, SUBLANES, LANES), F32),
                   jax.ShapeDtypeStruct((B, N_EXPERTS, S), F32)],
        compiler_params=_compiler_params(("parallel", "parallel")),
        name="merge",
    )(x_res, oa, ob, sg, w_oa, w_ob, w_out, ln1_g.reshape(L, 1, D), ln1_b.reshape(L, 1, D),
      *([mods] * (3 * ns)), w_r)


def _cumsum_lanes(x):
    n = x.shape[1]
    xb = x.astype(BF16)
    r = lax.broadcasted_iota(jnp.int32, (LANES, LANES), 0)
    c = lax.broadcasted_iota(jnp.int32, (LANES, LANES), 1)
    tri = jnp.where(r <= c, 1.0, 0.0).astype(BF16)
    tb = lax.broadcasted_iota(jnp.int32, (n, LANES), 0) // LANES
    kb = lax.broadcasted_iota(jnp.int32, (n, LANES), 1)
    before = jnp.where(tb < kb, 1.0, 0.0).astype(BF16)
    off = _dot(xb, before)
    outs = []
    for k in range(n // LANES):
        outs.append(_dot(xb[:, k * LANES:(k + 1) * LANES], tri) + off[:, k:k + 1])
    return outs


def _select_top(aff, cap):
    bits = pltpu.bitcast(aff, jnp.int32)
    thr = jnp.zeros((aff.shape[0], 1), jnp.int32)
    for bit in range(30, -1, -1):
        cand = thr | (1 << bit)
        cnt = jnp.sum(jnp.where(bits >= cand, 1.0, 0.0), axis=1, keepdims=True)
        thr = jnp.where(cnt >= cap, cand, thr)
    gt = bits > thr
    eq = jnp.where(bits == thr, 1.0, 0.0)
    need = cap - jnp.sum(jnp.where(gt, 1.0, 0.0), axis=1, keepdims=True)
    eq_rank = jnp.concatenate(_cumsum_lanes(eq), axis=1) - eq
    sel = jnp.where(jnp.logical_or(gt, jnp.logical_and(eq > 0.5, eq_rank < need)), 1.0, 0.0)
    return _cumsum_lanes(sel)


ROUTE_PAD = 1e6


def _route_kernel(n_ctx, cap_c, cap_l, lg_ref, aff_ref, idx_ref, cc_ref, cl_ref):
    lg = lg_ref[0]
    m = jnp.max(lg, axis=0, keepdims=True)
    ex = jnp.exp(lg - m)
    aff = ex / jnp.sum(ex, axis=0, keepdims=True)
    aff_ref[0] = aff
    for c_ref, blocks in ((cc_ref, _select_top(aff[:, :n_ctx], cap_c)),
                          (cl_ref, _select_top(aff[:, n_ctx:], cap_l))):
        c_ref[...] = jnp.full(c_ref.shape, ROUTE_PAD, F32)
        for k, blk in enumerate(blocks):
            for ex_i in range(N_EXPERTS):
                c_ref[ex_i, k:k + 1, :] = blk[ex_i:ex_i + 1, :]

    def slots(c_ref, e, n_blocks, width):
        cm = c_ref[e]
        jrow = lax.broadcasted_iota(jnp.int32, (1, width), 1).astype(F32)
        rows = max(n_blocks, SUBLANES)
        cend = cm[0:rows, LANES - 1:LANES]
        blk = jnp.sum(jnp.where(cend <= jrow, 1.0, 0.0), axis=0, keepdims=True)
        kio = lax.broadcasted_iota(jnp.int32, (LANES, width), 0).astype(F32)
        onehot = jnp.where(kio == blk, 1.0, 0.0).astype(BF16)
        cmt = cm.T
        hi = jnp.floor(cmt * (1.0 / 256.0))
        lo = cmt - 256.0 * hi
        cg = 256.0 * _dot(hi.astype(BF16), onehot) + _dot(lo.astype(BF16), onehot)
        inside = jnp.sum(jnp.where(cg <= jrow, 1.0, 0.0), axis=0, keepdims=True)
        return (blk * LANES + inside).astype(jnp.int32)

    def per_expert(e, carry):
        n_lat_blocks = (lg_ref.shape[2] - n_ctx) // LANES
        idx_ref[0, e, :, 0:cap_l] = slots(cl_ref, e, n_lat_blocks, cap_l) + n_ctx
        idx_ref[0, e, :, cap_l:cap_l + cap_c] = slots(cc_ref, e, n_ctx // LANES, LANES)[:, 0:cap_c]
        return carry

    lax.fori_loop(0, N_EXPERTS, per_expert, 0)


def _route_call(lg_t, n_ctx, cap_c, cap_l):
    B, E, S = lg_t.shape
    cap = cap_c + cap_l
    assert (S - n_ctx) // LANES <= LANES and cap_l % LANES == 0 and cap_c <= LANES
    return pl.pallas_call(
        functools.partial(_route_kernel, n_ctx, cap_c, cap_l),
        grid=(B,),
        in_specs=[pl.BlockSpec((1, E, S), lambda b: (b, 0, 0))],
        out_specs=[pl.BlockSpec((1, E, S), lambda b: (b, 0, 0)),
                   pl.BlockSpec((1, E, 1, cap), lambda b: (b, 0, 0, 0))],
        out_shape=[jax.ShapeDtypeStruct((B, E, S), F32),
                   jax.ShapeDtypeStruct((B, E, 1, cap), jnp.int32)],
        scratch_shapes=[pltpu.VMEM((E, LANES, LANES), F32), pltpu.VMEM((E, LANES, LANES), F32)],
        compiler_params=_compiler_params(("parallel",)),
        name="route",
    )(lg_t)


ROW_BATCH = 16
MOE_FF_CHUNK = 512


GATHER_EXPERTS = 4


def _gather_kernel(idx_ref, h_ref, xg_ref, tmp_ref):
    cap = xg_ref.shape[2]
    for ge in range(GATHER_EXPERTS):
        def body(g, carry):
            for r in range(ROW_BATCH):
                j = g * ROW_BATCH + r
                tmp_ref[j] = h_ref[0, idx_ref[0, 0, ge * cap + j]]
            return carry

        lax.fori_loop(0, cap // ROW_BATCH, body, 0)
        xg_ref[0, ge] = _tiles_to_rows(tmp_ref[...]).astype(xg_ref.dtype)


def _gather_call(idx, h2t):
    B, S = h2t.shape[0], h2t.shape[1]
    D = SUBLANES * LANES
    E, cap = idx.shape[1], idx.shape[2]
    G = GATHER_EXPERTS
    assert cap % ROW_BATCH == 0 and E % G == 0
    return pl.pallas_call(
        _gather_kernel,
        grid=(B, E // G),
        in_specs=[pl.BlockSpec((1, 1, G * cap), lambda b, e: (b * (E // G) + e, 0, 0),
                               memory_space=pltpu.SMEM),
                  pl.BlockSpec((1, S, SUBLANES, LANES), lambda b, e: (b, 0, 0, 0))],
        out_specs=pl.BlockSpec((1, G, cap, D), lambda b, e: (b, e, 0, 0)),
        out_shape=jax.ShapeDtypeStruct((B, E, cap, D), BF16),
        scratch_shapes=[pltpu.VMEM((cap, SUBLANES, LANES), F32)],
        compiler_params=_compiler_params(("parallel", "arbitrary")),
        name="moe_gather",
    )(idx.reshape(B * E // G, 1, G * cap), h2t)


def _moe_kernel(idx_ref, aff_ref, xg_ref, wg_ref, wu_ref, wd_ref, out_ref, y_ref):
    s = pl.program_id(1)
    n_exp = pl.num_programs(1) - 1
    cap = y_ref.shape[1]
    n_batches = cap // ROW_BATCH

    def scatter(slot, batches):
        new = None
        for g in batches:
            rows = _rows_to_tiles(y_ref[slot, g * ROW_BATCH:(g + 1) * ROW_BATCH, :])
            toks = [idx_ref[0, 0, g * ROW_BATCH + r] for r in range(ROW_BATCH)]
            old = [out_ref[0, t] for t in toks]
            for r, t in enumerate(toks):
                new = old[r] + rows[r] * aff_ref[0, 0, t]
                out_ref[0, t] = new
        return new

    def ffn(slot, prev_slot):
        ff = wg_ref.shape[3]
        n_dots = 3 * (ff // MOE_FF_CHUNK)
        per = -(-n_batches // n_dots)
        todo = list(range(n_batches)) if prev_slot is not None else []

        def weights(w):
            take, todo[:] = todo[:per], todo[per:]
            w = w.astype(BF16)
            if not take:
                return w
            last = scatter(prev_slot, take)
            bits = pltpu.bitcast(last, jnp.uint32)
            zero = pltpu.bitcast((bits >> 16) >> 16, F32)
            zero = jnp.concatenate([zero, zero], axis=0).astype(BF16)
            return w + jnp.tile(zero, (w.shape[0] // zero.shape[0], w.shape[1] // LANES))

        x = xg_ref[0, 0]
        y = None
        for f0 in range(0, ff, MOE_FF_CHUNK):
            f1 = f0 + MOE_FF_CHUNK
            a = _dot(x, weights(wg_ref[0, 0, :, f0:f1]))
            u = _dot(x, weights(wu_ref[0, 0, :, f0:f1]))
            hmid = (a * jax.nn.sigmoid(a) * u).astype(BF16)
            part = _dot(hmid, weights(wd_ref[0, 0, f0:f1, :]))
            y = part if y is None else y + part
        y_ref[slot] = y

    @pl.when(s == 0)
    def _():
        out_ref[...] = jnp.zeros(out_ref.shape, out_ref.dtype)
        ffn(0, None)

    @pl.when(jnp.logical_and(s > 0, s < n_exp))
    def _():
        slot = s % 2
        ffn(slot, 1 - slot)

    @pl.when(s == n_exp)
    def _():
        scatter((n_exp - 1) % 2, range(n_batches))


def _moe_call(layer, idx, aff, xg, w_gate, w_up, w_down, S):
    B, E, cap, D = xg.shape
    F = w_gate.shape[3]
    cur = lambda s: jnp.minimum(s, E - 1)
    prev = lambda s: jnp.maximum(s - 1, 0)
    wspec = lambda shp: pl.BlockSpec(shp, lambda b, s: (layer, cur(s), 0, 0))
    return pl.pallas_call(
        _moe_kernel,
        grid=(B, E + 1),
        in_specs=[pl.BlockSpec((1, 1, cap), lambda b, s: (b * E + prev(s), 0, 0), memory_space=pltpu.SMEM),
                  pl.BlockSpec((1, 1, S), lambda b, s: (b * E + prev(s), 0, 0), memory_space=pltpu.SMEM),
                  pl.BlockSpec((1, 1, cap, D), lambda b, s: (b, cur(s), 0, 0)),
                  wspec((1, 1, D, F)), wspec((1, 1, D, F)), wspec((1, 1, F, D))],
        out_specs=pl.BlockSpec((1, S, SUBLANES, LANES), lambda b, s: (b, 0, 0, 0),
                               pipeline_mode=pl.Buffered(1)),
        out_shape=jax.ShapeDtypeStruct((B, S, SUBLANES, LANES), F32),
        scratch_shapes=[pltpu.VMEM((2, cap, D), F32)],
        compiler_params=_compiler_params(("parallel", "arbitrary")),
        name="moe_ffn",
    )(idx.reshape(B * E, 1, cap), aff.reshape(B * E, 1, S), xg, w_gate, w_up, w_down)


def _final_kernel(alpha, ns, x_ref, moe_ref, g_ref, b_ref, *refs):
    o_ref = refs[ns]
    for n in range(ns):
        o_ref[n] = _layer_norm(alpha * x_ref[n] + refs[n][0] * _tiles_to_rows(moe_ref[n]),
                               g_ref[0], b_ref[0])


def _final_call(layer, alpha, x1, moe, ln2_g, ln2_b, mods, n_ctx):
    B, S, D = x1.shape
    T = TOK_BLOCK
    L = ln2_g.shape[0]
    ncb = n_ctx // T
    ns = MERGE_SAMPLES if B % MERGE_SAMPLES == 0 else 1
    tok_in = pl.BlockSpec((ns, T, D), lambda b, i: (b, i + ncb, 0))
    const = pl.BlockSpec((1, 1, D), lambda b, i: (layer, 0, 0))
    gate = lambda n: pl.BlockSpec(
        (1, 1, D), lambda b, i: ((layer * ADA_ROWS + ns * b + n) * 6 + 5, 0, 0))
    return pl.pallas_call(
        functools.partial(_final_kernel, alpha, ns),
        grid=(B // ns, (S - n_ctx) // T),
        in_specs=[tok_in, pl.BlockSpec((ns, T, SUBLANES, LANES), lambda b, i: (b, i + ncb, 0, 0)),
                  const, const] + [gate(n) for n in range(ns)],
        out_specs=pl.BlockSpec((ns, T, D), lambda b, i: (b, i, 0)),
        out_shape=jax.ShapeDtypeStruct((B, S - n_ctx, D), F32),
        compiler_params=_compiler_params(("parallel", "parallel")),
        name="final_norm",
    )(x1, moe, ln2_g.reshape(L, 1, D), ln2_b.reshape(L, 1, D), *([mods] * ns))


def _prep_weights(w_in, w_uq, w_ukv, w_router):
    L, D, _ = w_in.shape
    krs = jnp.concatenate([jnp.zeros((L, D, 64), w_in.dtype), w_in[..., 512:544],
                           jnp.zeros((L, D, 32), w_in.dtype)], axis=-1)
    w_in_p = jnp.concatenate([w_in[..., 0:512], krs, w_in[..., 544:]], axis=-1).astype(BF16)

    kvr = w_ukv.reshape(L, MLA_KV_RANK, MLA_HEADS, MLA_NOPE + MLA_V)
    kn, vv = kvr[..., :MLA_NOPE], kvr[..., MLA_NOPE:]
    zk = jnp.zeros_like(kn)
    w_k = jnp.concatenate([kn, zk], axis=-1).reshape(L, MLA_KV_RANK, MLA_HEADS * LANES)
    v_even = jnp.concatenate([vv, zk], axis=-1)
    v_odd = jnp.concatenate([zk, vv], axis=-1)
    odd = (jnp.arange(MLA_HEADS) % 2 == 1)[None, None, :, None]
    w_v = jnp.where(odd, v_odd, v_even).reshape(L, MLA_KV_RANK, MLA_HEADS * LANES)
    w_kv = jnp.concatenate([w_k, w_v], axis=-1).astype(BF16)

    qr = w_uq.reshape(L, MLA_Q_RANK, MLA_HEADS, MLA_NOPE + MLA_ROPE)
    w_q = jnp.concatenate([qr, jnp.zeros((L, MLA_Q_RANK, MLA_HEADS, 32), w_uq.dtype)], axis=-1)
    w_q = w_q.reshape(L, MLA_Q_RANK, MLA_HEADS * LANES).astype(BF16)

    w_r = jnp.concatenate(
        [w_router, jnp.zeros((L, D, LANES - N_EXPERTS), w_router.dtype)], axis=-1).astype(BF16)
    return w_in_p, w_kv, w_q, w_r


def _ones_columns():
    lane = jnp.arange(LANES)
    even = (lane == 64).astype(F32)
    odd = (lane == 0).astype(F32)
    win = jnp.concatenate([even, odd, even, odd])
    mla = jnp.concatenate([even, odd] * (MLA_HEADS // 2))
    return jnp.concatenate([win, mla])[None, :]


def _rope_tables(n_ctx, n_lat):
    pos = jnp.arange(n_lat)
    rowp = (pos // GRID_W).astype(F32)
    colp = (pos % GRID_W).astype(F32)

    def pattern(rot_dim):
        nf = rot_dim // 4
        inv = ROPE_THETA ** (-jnp.arange(nf, dtype=F32) / nf)
        ar, ac = rowp[:, None] * inv, colp[:, None] * inv
        cos = jnp.concatenate([jnp.cos(ar), jnp.cos(ar), jnp.cos(ac), jnp.cos(ac)], axis=1)
        sin = jnp.concatenate([-jnp.sin(ar), jnp.sin(ar), -jnp.sin(ac), jnp.sin(ac)], axis=1)
        return cos, sin

    cw, sw = pattern(WIN_HEAD_DIM)
    cw, sw = jnp.tile(cw, (1, 2)), jnp.tile(sw, (1, 2))
    cm, sm = pattern(MLA_ROPE)
    one64, zero64 = jnp.ones((n_lat, 64), F32), jnp.zeros((n_lat, 64), F32)
    cm = jnp.concatenate([one64, cm, one64[:, :32]], axis=1)
    sm = jnp.concatenate([zero64, sm, zero64[:, :32]], axis=1)
    sq_w, sq_m = WIN_SCALE * LOG2E, MLA_SCALE * LOG2E
    lat = jnp.concatenate([cw * sq_w, sw * sq_w, cw, sw, cm * sq_m, sm * sq_m, cm, sm], axis=1)
    ones, zeros = jnp.ones((n_ctx, LANES), F32), jnp.zeros((n_ctx, LANES), F32)
    ctx = jnp.concatenate([ones * sq_w, zeros, ones, zeros, ones * sq_m, zeros, ones, zeros], axis=1)
    return jnp.concatenate([ctx, lat], axis=0)


def kernel(x, c, ctx, c_ctx, w_ada, b_ada, w_in, attn_sink, mla_q_norm, mla_kv_norm, w_uq, w_ukv,
           w_oa, w_ob, w_out, ln1_g, ln1_b, w_router, w_exp_gate, w_exp_up, w_exp_down, ln2_g, ln2_b):
    B, n_lat, D = x.shape
    n_ctx = ctx.shape[1]
    depth = w_in.shape[0]
    S = n_ctx + n_lat
    alpha = (2 * depth) ** 0.25
    assert D == D_MODEL and n_ctx == TOK_BLOCK and n_lat % TOK_BLOCK == 0 and B + 1 <= ADA_ROWS
    cap_c = CAPACITY_FACTOR * n_ctx // N_EXPERTS
    cap_l = CAPACITY_FACTOR * n_lat // N_EXPERTS
    assert cap_l % 256 == 0 and cap_c % 8 == 0
    ncb = n_ctx // TOK_BLOCK

    cond = jnp.concatenate([c, c_ctx[None], jnp.zeros((ADA_ROWS - B - 1, D), F32)], axis=0)
    mods = _ada_call(cond.T, w_ada, b_ada, B + 1).reshape(depth * ADA_ROWS * 6, 1, D)

    w_in_p, w_kv, w_q, w_r = _prep_weights(w_in, w_uq, w_ukv, w_router)
    w_oa_b, w_ob_b, w_out_b = w_oa.astype(BF16), w_ob.astype(BF16), w_out.astype(BF16)
    tab = _rope_tables(n_ctx, n_lat)
    vones = _ones_columns()

    x_all = (ctx, x)
    moe = None
    for l in range(depth):
        x_res, (qwin, kwin, vwin, qcat, kcat, vext, sg) = _proj_call(
            l, alpha, x_all, moe, ln2_g, ln2_b, mods, tab, w_in_p, mla_kv_norm, mla_q_norm,
            w_kv, w_q, vones, ncb)
        oa = _win_call(l, attn_sink, qwin, kwin, vwin, n_ctx)
        ob = _mla_call(qcat, kcat, vext, n_ctx)
        x1, h2, lg_t = _merge_call(l, alpha, x_res, oa, ob, sg, w_oa_b, w_ob_b, w_out_b,
                                   ln1_g, ln1_b, mods, w_r, ncb)
        aff, idx4 = _route_call(lg_t, n_ctx, cap_c, cap_l)
        idx = idx4.reshape(B, N_EXPERTS, cap_c + cap_l)
        xg = _gather_call(idx, h2)
        moe = _moe_call(l, idx, aff, xg, w_exp_gate, w_exp_up, w_exp_down, S)
        x_all = x1
    return _final_call(depth - 1, alpha, x_all, moe, ln2_g, ln2_b, mods, n_ctx)
```

```python
import functools
import math

import jax
import jax.numpy as jnp
from jax import lax
from jax.experimental import pallas as pl
from jax.experimental.pallas import tpu as pltpu

D_MODEL = 1024
GRID_W = 64
WIN_HEADS = 8
WIN_KV_HEADS = 2
WIN_HEAD_DIM = 64
BAND = 128
MLA_HEADS = 8
MLA_Q_RANK = 384
MLA_KV_RANK = 256
MLA_NOPE = 64
MLA_ROPE = 32
MLA_V = 64
N_EXPERTS = 16
CAPACITY_FACTOR = 2
ROPE_THETA = 10000.0
LN_EPS = 1e-5
RMS_EPS = 1e-6
NEG_INF = -1e30
LOG2E = math.log2(math.e)
WIN_SCALE = WIN_HEAD_DIM ** -0.5
MLA_SCALE = (MLA_NOPE + MLA_ROPE) ** -0.5

LANES = 128
TOK_BLOCK = 256
VMEM_LIMIT = 56 * 1024 * 1024

BF16 = jnp.bfloat16
F32 = jnp.float32

SEG_KWIN = (0, 0, 128)
SEG_VWIN = (0, 128, 256)
SEG_CKV = (0, 256, 512)
SEG_KR = (1, 0, 128)
SEG_QWIN = (2, 0, 512)
SEG_CQ = (2, 512, 896)
SEG_GATE = (2, 896, 2944)


def _dot(a, b):
    return jnp.dot(a, b, preferred_element_type=F32)


def _dot_nt(a, b):
    return lax.dot_general(a, b, (((1,), (1,)), ((), ())), preferred_element_type=F32)


def _layer_norm(z, g, b):
    mu = jnp.mean(z, axis=-1, keepdims=True)
    zc = z - mu
    var = jnp.mean(zc * zc, axis=-1, keepdims=True)
    return zc * lax.rsqrt(var + LN_EPS) * g + b


def _rms_norm(x, g):
    return x * lax.rsqrt(jnp.mean(x * x, axis=-1, keepdims=True) + RMS_EPS) * g


def _rope_slab(x, cos, sin, half):
    lane = lax.broadcasted_iota(jnp.int32, x.shape, 1)
    partner = jnp.where((lane & half) == 0,
                        pltpu.roll(x, LANES - half, 1), pltpu.roll(x, half, 1))
    return x * cos + partner * sin


SUBLANES = 8


def _rows_to_tiles(y):
    n = y.shape[0]
    y3 = pltpu.einshape("r(sl)->srl", y, s=SUBLANES)
    y4 = y3.reshape(SUBLANES, n // SUBLANES, SUBLANES, LANES)
    return jnp.transpose(y4, (1, 2, 0, 3)).reshape(n, SUBLANES, LANES)


def _tiles_to_rows(x3):
    n = x3.shape[0]
    x4 = x3.reshape(n // SUBLANES, SUBLANES, SUBLANES, LANES)
    xs = jnp.transpose(x4, (2, 0, 1, 3)).reshape(SUBLANES, n, LANES)
    return pltpu.einshape("srl->r(sl)", xs)


def _compiler_params(sem):
    return pltpu.CompilerParams(dimension_semantics=sem, vmem_limit_bytes=VMEM_LIMIT)


ADA_ROWS = 8
ADA_TN = 512


def _ada_kernel(n_rows, condt_ref, w_ref, b_ref, o_ref):
    ct = condt_ref[...]
    st = ct * jax.nn.sigmoid(ct)
    w = w_ref[0]
    rows = []
    for r in range(n_rows):
        rows.append(jnp.sum(w * st[:, r:r + 1], axis=0, keepdims=True) + b_ref[0])
    rows.append(jnp.zeros((ADA_ROWS - n_rows, w.shape[1]), F32))
    o_ref[0] = jnp.concatenate(rows, axis=0)


def _ada_call(cond_t, w_ada, b_ada, n_rows):
    L, D, N = w_ada.shape
    return pl.pallas_call(
        functools.partial(_ada_kernel, n_rows),
        grid=(L, N // ADA_TN),
        in_specs=[
            pl.BlockSpec((D, ADA_ROWS), lambda l, j: (0, 0)),
            pl.BlockSpec((1, D, ADA_TN), lambda l, j: (l, 0, j)),
            pl.BlockSpec((1, 1, ADA_TN), lambda l, j: (l, 0, j)),
        ],
        out_specs=pl.BlockSpec((1, ADA_ROWS, ADA_TN), lambda l, j: (l, 0, j)),
        out_shape=jax.ShapeDtypeStruct((L, ADA_ROWS, N), F32),
        compiler_params=_compiler_params(("parallel", "parallel")),
        name="adaln_mod",
    )(cond_t, w_ada, b_ada.reshape(L, 1, N))


def _proj_kernel(has_ln, alpha, n_ctx_blocks, ns, *refs):
    refs = list(refs)
    if has_ln:
        x_ref, moe_ref, g2_ref, b2_ref = refs[:4]
        m5_refs, refs = refs[4:4 + ns], refs[4 + ns:]
    else:
        (ctx_ref, x_ref), refs = refs[:2], refs[2:]
    mod_refs, refs = refs[:2 * ns], refs[2 * ns:]
    (tab_ref, wa_ref, wb_ref, wc_ref, gkv_ref, gq_ref, wkv_ref, wq_ref, vones_ref,
     xres_ref, qwin_ref, kwin_ref, vwin_ref, qcat_ref, kcat_ref, vext_ref, sg_ref) = refs
    win_refs = (wa_ref, wb_ref, wc_ref)
    T = x_ref.shape[1]
    hbs = []
    for n in range(ns):
        if has_ln:
            z = alpha * x_ref[n] + m5_refs[n][0] * _tiles_to_rows(moe_ref[n])
            x = _layer_norm(z, g2_ref[0], b2_ref[0])
        else:
            x = jnp.where(pl.program_id(0) < n_ctx_blocks, ctx_ref[n], x_ref[n])
        xres_ref[n] = x
        sh_ref, sc_ref = mod_refs[2 * n:2 * n + 2]
        hbs.append((x * (1.0 + sc_ref[0]) + sh_ref[0]).astype(BF16))
    hb = jnp.concatenate(hbs, axis=0)
    split = lambda v: v.reshape(ns, T, v.shape[1])

    def seg(s):
        return _dot(hb, win_refs[s[0]][0, :, s[1]:s[2]])

    def tab(i):
        t = tab_ref[:, i * LANES:(i + 1) * LANES]
        return jnp.concatenate([t] * ns, axis=0)

    cq_w, sq_w, ck_w, sk_w, cq_m, sq_m, ck_m, sk_m = (tab(i) for i in range(8))

    lane_lo = lax.broadcasted_iota(jnp.int32, (hb.shape[0], LANES), 1) < WIN_HEAD_DIM
    half_swap = lambda t: pltpu.roll(t, WIN_HEAD_DIM, 1)

    ckv = _rms_norm(seg(SEG_CKV), gkv_ref[0]).astype(BF16)
    cq = _rms_norm(seg(SEG_CQ), gq_ref[0]).astype(BF16)

    qw = seg(SEG_QWIN)
    q_slabs = []
    for s2 in range(WIN_HEADS // 2):
        pair = _rope_slab(qw[:, s2 * LANES:(s2 + 1) * LANES], cq_w, sq_w, 16)
        pair_sw = half_swap(pair)
        if 2 * s2 < WIN_HEADS // WIN_KV_HEADS:
            q_slabs += [jnp.where(lane_lo, pair, 0.0), jnp.where(lane_lo, pair_sw, 0.0)]
        else:
            q_slabs += [jnp.where(lane_lo, 0.0, pair_sw), jnp.where(lane_lo, 0.0, pair)]
    qwin_ref[...] = split(jnp.concatenate(q_slabs, axis=1).astype(BF16))

    kv = _dot(ckv, wkv_ref[0])
    kr = _rope_slab(seg(SEG_KR), ck_m, sk_m, 8)
    qm = _dot(cq, wq_ref[0])
    for h in range(MLA_HEADS):
        sl = slice(h * LANES, (h + 1) * LANES)
        kcat_ref[:, :, sl] = split((kv[:, sl] + kr).astype(BF16))
        vext_ref[:, :, sl] = split(
            (kv[:, 1024 + h * LANES:1024 + (h + 1) * LANES]
             + vones_ref[:, 512 + h * LANES:512 + (h + 1) * LANES]).astype(BF16))
        qcat_ref[:, :, sl] = split(_rope_slab(qm[:, sl], cq_m, sq_m, 8).astype(BF16))

    sg_ref[...] = split(jax.nn.sigmoid(seg(SEG_GATE)).astype(BF16))
    va = seg(SEG_VWIN)
    va_sw = half_swap(va)
    vwin_ref[...] = split((jnp.concatenate(
        [jnp.where(lane_lo, va, 0.0), jnp.where(lane_lo, 0.0, va_sw),
         jnp.where(lane_lo, va_sw, 0.0), jnp.where(lane_lo, 0.0, va)], axis=1)
        + vones_ref[:, 0:512]).astype(BF16))
    kwin_ref[...] = split(_rope_slab(seg(SEG_KWIN), ck_w, sk_w, 16).astype(BF16))


def _proj_call(layer, alpha, x_all, moe, ln2_g, ln2_b, mods, tab, w_in_p, g_kvn, g_qn, w_kv, w_q,
               vones, n_ctx_blocks):
    T = TOK_BLOCK
    has_ln = moe is not None
    if has_ln:
        B, S, D = x_all.shape
    else:
        ctx_in, x_in = x_all
        B, D = x_in.shape[0], x_in.shape[2]
        S = ctx_in.shape[1] + x_in.shape[1]
    L = w_in_p[0].shape[0]
    ns = PROJ_SAMPLES if B % PROJ_SAMPLES == 0 else 1

    def mod_spec(k, lyr, n):
        def imap(i, b):
            row = jnp.where(i < n_ctx_blocks, B, ns * b + n)
            return ((lyr * ADA_ROWS + row) * 6 + k, 0, 0)
        return pl.BlockSpec((1, 1, D), imap)

    tok = lambda w: pl.BlockSpec((ns, T, w), lambda i, b: (b, i, 0))
    headed = tok(MLA_HEADS * LANES)
    const = lambda shp: pl.BlockSpec(shp, lambda i, b: (layer,) + (0,) * (len(shp) - 1),
                                     pipeline_mode=pl.Buffered(1))

    if has_ln:
        prev = pl.BlockSpec((1, 1, D), lambda i, b: (layer - 1, 0, 0))
        tiled = pl.BlockSpec((ns, T, SUBLANES, LANES), lambda i, b: (b, i, 0, 0))
        in_specs = [tok(D), tiled, prev, prev] + [mod_spec(5, layer - 1, n) for n in range(ns)]
        args = [x_all, moe, ln2_g.reshape(L, 1, D), ln2_b.reshape(L, 1, D)] + [mods] * ns
    else:
        assert ctx_in.shape[1] == n_ctx_blocks * T
        in_specs = [pl.BlockSpec((ns, T, D), lambda i, b: (b, jnp.minimum(i, n_ctx_blocks - 1), 0)),
                    pl.BlockSpec((ns, T, D), lambda i, b: (b, jnp.maximum(i - n_ctx_blocks, 0), 0))]
        args = [ctx_in, x_in]
    in_specs += [mod_spec(k, layer, n) for n in range(ns) for k in (0, 1)]
    args += [mods] * (2 * ns)
    in_specs += [
        pl.BlockSpec((T, 8 * LANES), lambda i, b: (i, 0)),
        *[const((1, D, w.shape[2])) for w in w_in_p],
        const((1, 1, MLA_KV_RANK)), const((1, 1, MLA_Q_RANK)),
        const((1, MLA_KV_RANK, 2048)), const((1, MLA_Q_RANK, 1024)),
        pl.BlockSpec((1, 1536), lambda i, b: (0, 0)),
    ]
    args += [tab, *w_in_p, g_kvn.reshape(L, 1, -1), g_qn.reshape(L, 1, -1), w_kv, w_q, vones]

    out_specs, out_shape = [tok(D)], [jax.ShapeDtypeStruct((B, S, D), F32)]
    out_specs += [tok(WIN_HEADS * LANES), tok(LANES), tok(512), headed, headed, headed, tok(2048)]
    out_shape += [jax.ShapeDtypeStruct((B, S, WIN_HEADS * LANES), BF16),
                  jax.ShapeDtypeStruct((B, S, LANES), BF16),
                  jax.ShapeDtypeStruct((B, S, 512), BF16)]
    out_shape += [jax.ShapeDtypeStruct((B, S, MLA_HEADS * LANES), BF16)] * 3
    out_shape += [jax.ShapeDtypeStruct((B, S, 2048), BF16)]

    outs = pl.pallas_call(
        functools.partial(_proj_kernel, has_ln, alpha, n_ctx_blocks, ns),
        grid=(S // T, B // ns),
        in_specs=in_specs, out_specs=out_specs, out_shape=out_shape,
        compiler_params=_compiler_params(("parallel", "parallel")),
        name="proj",
    )(*args)
    return outs[0], outs[1:]


WIN_HEAD_ORDER = (0, 2, 1, 3, 4, 6, 5, 7)
WIN_SAMPLES = 4


def _win_kernel(layer, n_ctx, n_steps, sink_ref, q_ref, kc_ref, vc_ref, kp_ref, km_ref, kn_ref,
                vp_ref, vm_ref, vn_ref, o_ref, kall_ref, vall_ref, p_ref):
    i = pl.program_id(1)
    row = lax.broadcasted_iota(jnp.int32, (BAND, LANES), 0)
    col = lax.broadcasted_iota(jnp.int32, (BAND, LANES), 1)
    lane_lo = col < 64
    ns = q_ref.shape[0]
    for n in range(ns):
        kall_ref[n, 0:n_ctx, :] = kc_ref[n]
        vall_ref[n, 0:n_ctx, :] = vc_ref[n]

    def scores(n, qb, r0, nk):
        q = q_ref[n, qb * BAND:(qb + 1) * BAND, :]
        q8 = jnp.concatenate([q[:, h * LANES:(h + 1) * LANES] for h in WIN_HEAD_ORDER], axis=0)
        return _dot_nt(q8, kall_ref[n, r0:r0 + nk, :])

    def softmax(n, qb, nk, masks, s):
        e_sink = []
        for c, h in enumerate(WIN_HEAD_ORDER):
            sc = s[c * BAND:(c + 1) * BAND, :]
            if masks:
                blocks = []
                for kb in range(nk // BAND):
                    blk = sc[:, kb * BAND:(kb + 1) * BAND]
                    blocks.append(jnp.where(masks[kb], blk, NEG_INF) if kb in masks else blk)
                sc = jnp.concatenate(blocks, axis=1)
            sink = sink_ref[layer, h] * LOG2E
            m = jnp.maximum(jnp.max(sc, axis=1, keepdims=True), sink)
            p_ref[n, qb, c * BAND:(c + 1) * BAND, 0:nk] = jnp.exp2(sc - m).astype(BF16)
            e_sink.append(jnp.exp2(sink - m))
        return e_sink

    def values(n, qb, r0, nk, e_sink):
        pairs = []
        for pi in range(4):
            acc = _dot(p_ref[n, qb, pi * 2 * BAND:(pi + 1) * 2 * BAND, 0:nk],
                       vall_ref[n, r0:r0 + nk, pi * LANES:(pi + 1) * LANES])
            es = jnp.concatenate([e_sink[2 * pi], e_sink[2 * pi + 1]], axis=0)
            den = (acc[:, 64:65] if pi % 2 == 0 else acc[:, 0:1]) + es
            pairs.append(acc / den)
        slabs = [jnp.where(lane_lo, pairs[0][:BAND], pairs[1][:BAND]),
                 jnp.where(lane_lo, pairs[0][BAND:], pairs[1][BAND:]),
                 jnp.where(lane_lo, pairs[2][:BAND], pairs[3][:BAND]),
                 jnp.where(lane_lo, pairs[2][BAND:], pairs[3][BAND:])]
        o_ref[n, qb * BAND:(qb + 1) * BAND, :] = jnp.concatenate(slabs, axis=1).astype(o_ref.dtype)

    def attend_both(args0, args1):
        (r0, nk0, m0), (r1, nk1, m1) = args0, args1
        s = [(scores(n, 0, r0, nk0), scores(n, 1, r1, nk1)) for n in range(ns)]
        e = [(softmax(n, 0, nk0, m0, s[n][0]), softmax(n, 1, nk1, m1, s[n][1])) for n in range(ns)]
        for n in range(ns):
            values(n, 0, r0, nk0, e[n][0])
            values(n, 1, r1, nk1, e[n][1])

    @pl.when(i == 0)
    def _():
        attend_both((0, n_ctx, None), (0, n_ctx, None))

    @pl.when(i > 0)
    def _():
        for n in range(ns):
            r = n_ctx
            for k_ref, v_ref, rows in ((kp_ref, vp_ref, BAND), (km_ref, vm_ref, 2 * BAND),
                                       (kn_ref, vn_ref, BAND), (kc_ref, vc_ref, n_ctx)):
                kall_ref[n, r:r + rows, :] = k_ref[n]
                vall_ref[n, r:r + rows, :] = v_ref[n]
                r += rows
        off_first = jnp.where(i == 1, 2 * LANES, 0)
        off_last = jnp.where(i == n_steps - 1, 2 * LANES, 0)
        cb = n_ctx // BAND
        attend_both((0, n_ctx + 3 * BAND, {cb: col >= row + off_first, cb + 2: col <= row}),
                    (n_ctx + BAND, n_ctx + 3 * BAND, {0: col >= row, 2: col <= row - off_last}))


def _win_call(layer, sink, qwin, kwin, vwin, n_ctx):
    B, S, _ = qwin.shape
    nb = S // BAND
    ncb = n_ctx // BAND
    assert n_ctx == 2 * BAND and nb % 2 == 0
    nk = n_ctx + 3 * BAND
    ns = WIN_SAMPLES if B % WIN_SAMPLES == 0 else 1
    one = lambda w, f: pl.BlockSpec((ns, BAND, w), lambda b, i: (b, jnp.clip(f(i), ncb, nb - 1), 0))
    two = lambda w: pl.BlockSpec((ns, 2 * BAND, w), lambda b, i: (b, i, 0))
    ctx = lambda w: pl.BlockSpec((ns, n_ctx, w), lambda b, i: (b, 0, 0))
    prev, nxt = (lambda i: 2 * i - 1), (lambda i: 2 * i + 2)
    return pl.pallas_call(
        functools.partial(_win_kernel, layer, n_ctx, nb // 2),
        grid=(B // ns, nb // 2),
        in_specs=[pl.BlockSpec(memory_space=pltpu.SMEM), two(WIN_HEADS * LANES),
                  ctx(LANES), ctx(512), one(LANES, prev), two(LANES), one(LANES, nxt),
                  one(512, prev), two(512), one(512, nxt)],
        out_specs=two(512),
        out_shape=jax.ShapeDtypeStruct((B, S, 512), BF16),
        scratch_shapes=[pltpu.VMEM((ns, 2 * n_ctx + 4 * BAND, LANES), BF16),
                        pltpu.VMEM((ns, 2 * n_ctx + 4 * BAND, 512), BF16),
                        pltpu.VMEM((ns, 2, WIN_HEADS * BAND, nk), BF16)],
        compiler_params=_compiler_params(("parallel", "parallel")),
        name="win_attn",
    )(sink, qwin, kwin, vwin, kwin, kwin, kwin, vwin, vwin, vwin)


MLA_KCHUNK = 256
MLA_HEADS_PER_STEP = 4


def _mla_kernel(n_ctx, n_keys, q_ref, k_ref, v_ref, o_ref, s_ref):
    qi = pl.program_id(2)
    tq = q_ref.shape[1]
    lane = lax.broadcasted_iota(jnp.int32, (tq, LANES), 1)

    def run(nk):
        row_max = []
        for hh in range(MLA_HEADS_PER_STEP):
            hl = slice(hh * LANES, (hh + 1) * LANES)
            q = q_ref[0, :, hl]
            mrun = None
            for c in range(nk // MLA_KCHUNK):
                s = _dot_nt(q, k_ref[0, c * MLA_KCHUNK:(c + 1) * MLA_KCHUNK, hl])
                s_ref[hh, c] = s
                mc = jnp.maximum(s[:, :LANES], s[:, LANES:])
                mrun = mc if mrun is None else jnp.maximum(mrun, mc)
            row_max.append(jnp.max(mrun, axis=1, keepdims=True))
        outs = []
        for hh in range(MLA_HEADS_PER_STEP):
            acc = jnp.zeros((tq, LANES), F32)
            for c in range(nk // MLA_KCHUNK):
                p = jnp.exp2(s_ref[hh, c] - row_max[hh]).astype(BF16)
                acc = acc + _dot(p, v_ref[0, c * MLA_KCHUNK:(c + 1) * MLA_KCHUNK,
                                          hh * LANES:(hh + 1) * LANES])
            den = acc[:, 64:65] if hh % 2 == 0 else acc[:, 0:1]
            outs.append(acc / den)
        o_ref[0] = jnp.concatenate(
            [jnp.where(lane < 64, outs[2 * j], outs[2 * j + 1]) for j in range(len(outs) // 2)],
            axis=1).astype(o_ref.dtype)

    @pl.when(qi == 0)
    def _():
        run(n_ctx)

    @pl.when(qi > 0)
    def _():
        run(n_keys)


def _mla_call(qcat, kcat, vext, n_ctx):
    B, S, _ = qcat.shape
    H = MLA_HEADS
    T = TOK_BLOCK
    assert n_ctx == T
    G = MLA_HEADS_PER_STEP
    kv = pl.BlockSpec((1, S, G * LANES), lambda b, hp, i: (b, 0, hp))
    return pl.pallas_call(
        functools.partial(_mla_kernel, n_ctx, S),
        grid=(B, H // G, S // T),
        in_specs=[pl.BlockSpec((1, T, G * LANES), lambda b, hp, i: (b, i, hp)), kv, kv],
        out_specs=pl.BlockSpec((1, T, (G // 2) * LANES), lambda b, hp, i: (b, i, hp)),
        out_shape=jax.ShapeDtypeStruct((B, S, (H // 2) * LANES), BF16),
        scratch_shapes=[pltpu.VMEM((G, S // MLA_KCHUNK, T, MLA_KCHUNK), F32)],
        compiler_params=_compiler_params(("parallel", "parallel", "arbitrary")),
        name="mla_attn",
    )(qcat, kcat, vext)


MERGE_SAMPLES = 4
PROJ_SAMPLES = 1


def _merge_kernel(alpha, ns, x_ref, oa_ref, ob_ref, sg_ref, woa_ref, wob_ref, wout_ref, g1_ref,
                  b1_ref, *refs):
    mod_refs, (wr_ref, x1_ref, h2_ref, lg_ref) = refs[:3 * ns], refs[3 * ns:]
    T, D = x_ref.shape[1], x_ref.shape[2]
    rows = lambda ref: ref[...].reshape(ns * T, ref.shape[2])
    sg = rows(sg_ref)
    t = (sg[:, :D].astype(F32) * _dot(rows(oa_ref), woa_ref[0])
         + sg[:, D:].astype(F32) * _dot(rows(ob_ref), wob_ref[0]))
    y = _dot(t.astype(BF16), wout_ref[0])
    h2s = []
    for n in range(ns):
        m2_ref, m3_ref, m4_ref = mod_refs[3 * n:3 * n + 3]
        x1 = _layer_norm(alpha * x_ref[n] + m2_ref[0] * y[n * T:(n + 1) * T],
                         g1_ref[0], b1_ref[0])
        x1_ref[n] = x1
        h2 = x1 * (1.0 + m4_ref[0]) + m3_ref[0]
        h2_ref[n] = _rows_to_tiles(h2)
        h2s.append(h2.astype(BF16))
    lg = _dot(jnp.concatenate(h2s, axis=0), wr_ref[0])
    for n in range(ns):
        lg_ref[n] = lg[n * T:(n + 1) * T].T[:N_EXPERTS]


def _merge_call(layer, alpha, x_res, oa, ob, sg, w_oa, w_ob, w_out, ln1_g, ln1_b, mods, w_r,
                n_ctx_blocks):
    B, S, D = x_res.shape
    T = TOK_BLOCK
    L = w_oa.shape[0]
    ns = MERGE_SAMPLES if B % MERGE_SAMPLES == 0 else 1

    def mod_spec(k, n):
        def imap(bp, i):
            row = jnp.where(i < n_ctx_blocks, B, ns * bp + n)
            return ((layer * ADA_ROWS + row) * 6 + k, 0, 0)
        return pl.BlockSpec((1, 1, D), imap)

    tok = lambda w: pl.BlockSpec((ns, T, w), lambda bp, i: (bp, i, 0))
    const = lambda shp: pl.BlockSpec(shp, lambda bp, i: (layer,) + (0,) * (len(shp) - 1),
                                     pipeline_mode=pl.Buffered(1))
    mod_specs = [mod_spec(k, n) for n in range(ns) for k in (2, 3, 4)]
    return pl.pallas_call(
        functools.partial(_merge_kernel, alpha, ns),
        grid=(B // ns, S // T),
        in_specs=[tok(D), tok(512), tok(512), tok(2048),
                  const((1, 512, D)), const((1, 512, D)), const((1, D, D)),
                  const((1, 1, D)), const((1, 1, D))] + mod_specs + [const((1, D, LANES))],
        out_specs=[tok(D), pl.BlockSpec((ns, T, SUBLANES, LANES), lambda bp, i: (bp, i, 0, 0)),
                   pl.BlockSpec((ns, N_EXPERTS, T), lambda bp, i: (bp, 0, i))],
        out_shape=[jax.ShapeDtypeStruct((B, S, D), F32),
                   jax.ShapeDtypeStruct((B, S, SUBLANES, LANES), F32),
                   jax.ShapeDtypeStruct((B, N_EXPERTS, S), F32)],
        compiler_params=_compiler_params(("parallel", "parallel")),
        name="merge",
    )(x_res, oa, ob, sg, w_oa, w_ob, w_out, ln1_g.reshape(L, 1, D), ln1_b.reshape(L, 1, D),
      *([mods] * (3 * ns)), w_r)


def _cumsum_lanes(x):
    n = x.shape[1]
    xb = x.astype(BF16)
    r = lax.broadcasted_iota(jnp.int32, (LANES, LANES), 0)
    c = lax.broadcasted_iota(jnp.int32, (LANES, LANES), 1)
    tri = jnp.where(r <= c, 1.0, 0.0).astype(BF16)
    tb = lax.broadcasted_iota(jnp.int32, (n, LANES), 0) // LANES
    kb = lax.broadcasted_iota(jnp.int32, (n, LANES), 1)
    before = jnp.where(tb < kb, 1.0, 0.0).astype(BF16)
    off = _dot(xb, before)
    outs = []
    for k in range(n // LANES):
        outs.append(_dot(xb[:, k * LANES:(k + 1) * LANES], tri) + off[:, k:k + 1])
    return outs


def _select_top(aff, cap):
    bits = pltpu.bitcast(aff, jnp.int32)
    thr = jnp.zeros((aff.shape[0], 1), jnp.int32)
    for bit in range(30, -1, -1):
        cand = thr | (1 << bit)
        cnt = jnp.sum(jnp.where(bits >= cand, 1.0, 0.0), axis=1, keepdims=True)
        thr = jnp.where(cnt >= cap, cand, thr)
    gt = bits > thr
    eq = jnp.where(bits == thr, 1.0, 0.0)
    need = cap - jnp.sum(jnp.where(gt, 1.0, 0.0), axis=1, keepdims=True)
    eq_rank = jnp.concatenate(_cumsum_lanes(eq), axis=1) - eq
    sel = jnp.where(jnp.logical_or(gt, jnp.logical_and(eq > 0.5, eq_rank < need)), 1.0, 0.0)
    return _cumsum_lanes(sel)


ROUTE_PAD = 1e6


def _route_kernel(n_ctx, cap_c, cap_l, lg_ref, aff_ref, idx_ref, cc_ref, cl_ref):
    lg = lg_ref[0]
    m = jnp.max(lg, axis=0, keepdims=True)
    ex = jnp.exp(lg - m)
    aff = ex / jnp.sum(ex, axis=0, keepdims=True)
    aff_ref[0] = aff
    for c_ref, blocks in ((cc_ref, _select_top(aff[:, :n_ctx], cap_c)),
                          (cl_ref, _select_top(aff[:, n_ctx:], cap_l))):
        c_ref[...] = jnp.full(c_ref.shape, ROUTE_PAD, F32)
        for k, blk in enumerate(blocks):
            for ex_i in range(N_EXPERTS):
                c_ref[ex_i, k:k + 1, :] = blk[ex_i:ex_i + 1, :]

    def slots(c_ref, e, n_blocks, width):
        cm = c_ref[e]
        jrow = lax.broadcasted_iota(jnp.int32, (1, width), 1).astype(F32)
        rows = max(n_blocks, SUBLANES)
        cend = cm[0:rows, LANES - 1:LANES]
        blk = jnp.sum(jnp.where(cend <= jrow, 1.0, 0.0), axis=0, keepdims=True)
        kio = lax.broadcasted_iota(jnp.int32, (LANES, width), 0).astype(F32)
        onehot = jnp.where(kio == blk, 1.0, 0.0).astype(BF16)
        cmt = cm.T
        hi = jnp.floor(cmt * (1.0 / 256.0))
        lo = cmt - 256.0 * hi
        cg = 256.0 * _dot(hi.astype(BF16), onehot) + _dot(lo.astype(BF16), onehot)
        inside = jnp.sum(jnp.where(cg <= jrow, 1.0, 0.0), axis=0, keepdims=True)
        return (blk * LANES + inside).astype(jnp.int32)

    def per_expert(e, carry):
        n_lat_blocks = (lg_ref.shape[2] - n_ctx) // LANES
        idx_ref[0, e, :, 0:cap_l] = slots(cl_ref, e, n_lat_blocks, cap_l) + n_ctx
        idx_ref[0, e, :, cap_l:cap_l + cap_c] = slots(cc_ref, e, n_ctx // LANES, LANES)[:, 0:cap_c]
        return carry

    lax.fori_loop(0, N_EXPERTS, per_expert, 0)


def _route_call(lg_t, n_ctx, cap_c, cap_l):
    B, E, S = lg_t.shape
    cap = cap_c + cap_l
    assert (S - n_ctx) // LANES <= LANES and cap_l % LANES == 0 and cap_c <= LANES
    return pl.pallas_call(
        functools.partial(_route_kernel, n_ctx, cap_c, cap_l),
        grid=(B,),
        in_specs=[pl.BlockSpec((1, E, S), lambda b: (b, 0, 0))],
        out_specs=[pl.BlockSpec((1, E, S), lambda b: (b, 0, 0)),
                   pl.BlockSpec((1, E, 1, cap), lambda b: (b, 0, 0, 0))],
        out_shape=[jax.ShapeDtypeStruct((B, E, S), F32),
                   jax.ShapeDtypeStruct((B, E, 1, cap), jnp.int32)],
        scratch_shapes=[pltpu.VMEM((E, LANES, LANES), F32), pltpu.VMEM((E, LANES, LANES), F32)],
        compiler_params=_compiler_params(("parallel",)),
        name="route",
    )(lg_t)


ROW_BATCH = 16
MOE_FF_CHUNK = 512


GATHER_EXPERTS = 4


def _gather_kernel(idx_ref, h_ref, xg_ref, tmp_ref):
    cap = xg_ref.shape[2]
    for ge in range(GATHER_EXPERTS):
        def body(g, carry):
            for r in range(ROW_BATCH):
                j = g * ROW_BATCH + r
                tmp_ref[j] = h_ref[0, idx_ref[0, 0, ge * cap + j]]
            return carry

        lax.fori_loop(0, cap // ROW_BATCH, body, 0)
        xg_ref[0, ge] = _tiles_to_rows(tmp_ref[...]).astype(xg_ref.dtype)


def _gather_call(idx, h2t):
    B, S = h2t.shape[0], h2t.shape[1]
    D = SUBLANES * LANES
    E, cap = idx.shape[1], idx.shape[2]
    G = GATHER_EXPERTS
    assert cap % ROW_BATCH == 0 and E % G == 0
    return pl.pallas_call(
        _gather_kernel,
        grid=(B, E // G),
        in_specs=[pl.BlockSpec((1, 1, G * cap), lambda b, e: (b * (E // G) + e, 0, 0),
                               memory_space=pltpu.SMEM),
                  pl.BlockSpec((1, S, SUBLANES, LANES), lambda b, e: (b, 0, 0, 0))],
        out_specs=pl.BlockSpec((1, G, cap, D), lambda b, e: (b, e, 0, 0)),
        out_shape=jax.ShapeDtypeStruct((B, E, cap, D), BF16),
        scratch_shapes=[pltpu.VMEM((cap, SUBLANES, LANES), F32)],
        compiler_params=_compiler_params(("parallel", "arbitrary")),
        name="moe_gather",
    )(idx.reshape(B * E // G, 1, G * cap), h2t)


def _moe_kernel(idx_ref, aff_ref, xg_ref, wg_ref, wu_ref, wd_ref, out_ref, y_ref):
    s = pl.program_id(1)
    n_exp = pl.num_programs(1) - 1
    cap = y_ref.shape[1]
    n_batches = cap // ROW_BATCH

    def scatter(slot, batches):
        new = None
        for g in batches:
            rows = _rows_to_tiles(y_ref[slot, g * ROW_BATCH:(g + 1) * ROW_BATCH, :])
            toks = [idx_ref[0, 0, g * ROW_BATCH + r] for r in range(ROW_BATCH)]
            old = [out_ref[0, t] for t in toks]
            for r, t in enumerate(toks):
                new = old[r] + rows[r] * aff_ref[0, 0, t]
                out_ref[0, t] = new
        return new

    def ffn(slot, prev_slot):
        ff = wg_ref.shape[3]
        n_dots = 3 * (ff // MOE_FF_CHUNK)
        per = -(-n_batches // n_dots)
        todo = list(range(n_batches)) if prev_slot is not None else []

        def weights(w):
            take, todo[:] = todo[:per], todo[per:]
            w = w.astype(BF16)
            if not take:
                return w
            last = scatter(prev_slot, take)
            bits = pltpu.bitcast(last, jnp.uint32)
            zero = pltpu.bitcast((bits >> 16) >> 16, F32)
            zero = jnp.concatenate([zero, zero], axis=0).astype(BF16)
            return w + jnp.tile(zero, (w.shape[0] // zero.shape[0], w.shape[1] // LANES))

        x = xg_ref[0, 0]
        y = None
        for f0 in range(0, ff, MOE_FF_CHUNK):
            f1 = f0 + MOE_FF_CHUNK
            a = _dot(x, weights(wg_ref[0, 0, :, f0:f1]))
            u = _dot(x, weights(wu_ref[0, 0, :, f0:f1]))
            hmid = (a * jax.nn.sigmoid(a) * u).astype(BF16)
            part = _dot(hmid, weights(wd_ref[0, 0, f0:f1, :]))
            y = part if y is None else y + part
        y_ref[slot] = y

    @pl.when(s == 0)
    def _():
        out_ref[...] = jnp.zeros(out_ref.shape, out_ref.dtype)
        ffn(0, None)

    @pl.when(jnp.logical_and(s > 0, s < n_exp))
    def _():
        slot = s % 2
        ffn(slot, 1 - slot)

    @pl.when(s == n_exp)
    def _():
        scatter((n_exp - 1) % 2, range(n_batches))


def _moe_call(layer, idx, aff, xg, w_gate, w_up, w_down, S):
    B, E, cap, D = xg.shape
    F = w_gate.shape[3]
    cur = lambda s: jnp.minimum(s, E - 1)
    prev = lambda s: jnp.maximum(s - 1, 0)
    wspec = lambda shp: pl.BlockSpec(shp, lambda b, s: (layer, cur(s), 0, 0))
    return pl.pallas_call(
        _moe_kernel,
        grid=(B, E + 1),
        in_specs=[pl.BlockSpec((1, 1, cap), lambda b, s: (b * E + prev(s), 0, 0), memory_space=pltpu.SMEM),
                  pl.BlockSpec((1, 1, S), lambda b, s: (b * E + prev(s), 0, 0), memory_space=pltpu.SMEM),
                  pl.BlockSpec((1, 1, cap, D), lambda b, s: (b, cur(s), 0, 0)),
                  wspec((1, 1, D, F)), wspec((1, 1, D, F)), wspec((1, 1, F, D))],
        out_specs=pl.BlockSpec((1, S, SUBLANES, LANES), lambda b, s: (b, 0, 0, 0),
                               pipeline_mode=pl.Buffered(1)),
        out_shape=jax.ShapeDtypeStruct((B, S, SUBLANES, LANES), F32),
        scratch_shapes=[pltpu.VMEM((2, cap, D), F32)],
        compiler_params=_compiler_params(("parallel", "arbitrary")),
        name="moe_ffn",
    )(idx.reshape(B * E, 1, cap), aff.reshape(B * E, 1, S), xg, w_gate, w_up, w_down)


def _final_kernel(alpha, ns, x_ref, moe_ref, g_ref, b_ref, *refs):
    o_ref = refs[ns]
    for n in range(ns):
        o_ref[n] = _layer_norm(alpha * x_ref[n] + refs[n][0] * _tiles_to_rows(moe_ref[n]),
                               g_ref[0], b_ref[0])


def _final_call(layer, alpha, x1, moe, ln2_g, ln2_b, mods, n_ctx):
    B, S, D = x1.shape
    T = TOK_BLOCK
    L = ln2_g.shape[0]
    ncb = n_ctx // T
    ns = MERGE_SAMPLES if B % MERGE_SAMPLES == 0 else 1
    tok_in = pl.BlockSpec((ns, T, D), lambda b, i: (b, i + ncb, 0))
    const = pl.BlockSpec((1, 1, D), lambda b, i: (layer, 0, 0))
    gate = lambda n: pl.BlockSpec(
        (1, 1, D), lambda b, i: ((layer * ADA_ROWS + ns * b + n) * 6 + 5, 0, 0))
    return pl.pallas_call(
        functools.partial(_final_kernel, alpha, ns),
        grid=(B // ns, (S - n_ctx) // T),
        in_specs=[tok_in, pl.BlockSpec((ns, T, SUBLANES, LANES), lambda b, i: (b, i + ncb, 0, 0)),
                  const, const] + [gate(n) for n in range(ns)],
        out_specs=pl.BlockSpec((ns, T, D), lambda b, i: (b, i, 0)),
        out_shape=jax.ShapeDtypeStruct((B, S - n_ctx, D), F32),
        compiler_params=_compiler_params(("parallel", "parallel")),
        name="final_norm",
    )(x1, moe, ln2_g.reshape(L, 1, D), ln2_b.reshape(L, 1, D), *([mods] * ns))


def _prep_weights(w_in, w_uq, w_ukv, w_router):
    L, D, _ = w_in.shape
    krs = jnp.concatenate([jnp.zeros((L, D, 64), w_in.dtype), w_in[..., 512:544],
                           jnp.zeros((L, D, 32), w_in.dtype)], axis=-1)
    w_in_p = (w_in[..., 0:512].astype(BF16), krs.astype(BF16), w_in[..., 544:].astype(BF16))

    kvr = w_ukv.reshape(L, MLA_KV_RANK, MLA_HEADS, MLA_NOPE + MLA_V)
    kn, vv = kvr[..., :MLA_NOPE], kvr[..., MLA_NOPE:]
    zk = jnp.zeros_like(kn)
    w_k = jnp.concatenate([kn, zk], axis=-1).reshape(L, MLA_KV_RANK, MLA_HEADS * LANES)
    v_even = jnp.concatenate([vv, zk], axis=-1)
    v_odd = jnp.concatenate([zk, vv], axis=-1)
    odd = (jnp.arange(MLA_HEADS) % 2 == 1)[None, None, :, None]
    w_v = jnp.where(odd, v_odd, v_even).reshape(L, MLA_KV_RANK, MLA_HEADS * LANES)
    w_kv = jnp.concatenate([w_k, w_v], axis=-1).astype(BF16)

    qr = w_uq.reshape(L, MLA_Q_RANK, MLA_HEADS, MLA_NOPE + MLA_ROPE)
    w_q = jnp.concatenate([qr, jnp.zeros((L, MLA_Q_RANK, MLA_HEADS, 32), w_uq.dtype)], axis=-1)
    w_q = w_q.reshape(L, MLA_Q_RANK, MLA_HEADS * LANES).astype(BF16)

    w_r = jnp.concatenate(
        [w_router, jnp.zeros((L, D, LANES - N_EXPERTS), w_router.dtype)], axis=-1).astype(BF16)
    return w_in_p, w_kv, w_q, w_r


def _ones_columns():
    lane = jnp.arange(LANES)
    even = (lane == 64).astype(F32)
    odd = (lane == 0).astype(F32)
    win = jnp.concatenate([even, odd, even, odd])
    mla = jnp.concatenate([even, odd] * (MLA_HEADS // 2))
    return jnp.concatenate([win, mla])[None, :]


def _rope_tables(n_ctx, n_lat):
    pos = jnp.arange(n_lat)
    rowp = (pos // GRID_W).astype(F32)
    colp = (pos % GRID_W).astype(F32)

    def pattern(rot_dim):
        nf = rot_dim // 4
        inv = ROPE_THETA ** (-jnp.arange(nf, dtype=F32) / nf)
        ar, ac = rowp[:, None] * inv, colp[:, None] * inv
        cos = jnp.concatenate([jnp.cos(ar), jnp.cos(ar), jnp.cos(ac), jnp.cos(ac)], axis=1)
        sin = jnp.concatenate([-jnp.sin(ar), jnp.sin(ar), -jnp.sin(ac), jnp.sin(ac)], axis=1)
        return cos, sin

    cw, sw = pattern(WIN_HEAD_DIM)
    cw, sw = jnp.tile(cw, (1, 2)), jnp.tile(sw, (1, 2))
    cm, sm = pattern(MLA_ROPE)
    one64, zero64 = jnp.ones((n_lat, 64), F32), jnp.zeros((n_lat, 64), F32)
    cm = jnp.concatenate([one64, cm, one64[:, :32]], axis=1)
    sm = jnp.concatenate([zero64, sm, zero64[:, :32]], axis=1)
    sq_w, sq_m = WIN_SCALE * LOG2E, MLA_SCALE * LOG2E
    lat = jnp.concatenate([cw * sq_w, sw * sq_w, cw, sw, cm * sq_m, sm * sq_m, cm, sm], axis=1)
    ones, zeros = jnp.ones((n_ctx, LANES), F32), jnp.zeros((n_ctx, LANES), F32)
    ctx = jnp.concatenate([ones * sq_w, zeros, ones, zeros, ones * sq_m, zeros, ones, zeros], axis=1)
    return jnp.concatenate([ctx, lat], axis=0)


def kernel(x, c, ctx, c_ctx, w_ada, b_ada, w_in, attn_sink, mla_q_norm, mla_kv_norm, w_uq, w_ukv,
           w_oa, w_ob, w_out, ln1_g, ln1_b, w_router, w_exp_gate, w_exp_up, w_exp_down, ln2_g, ln2_b):
    B, n_lat, D = x.shape
    n_ctx = ctx.shape[1]
    depth = w_in.shape[0]
    S = n_ctx + n_lat
    alpha = (2 * depth) ** 0.25
    assert D == D_MODEL and n_ctx == TOK_BLOCK and n_lat % TOK_BLOCK == 0 and B + 1 <= ADA_ROWS
    cap_c = CAPACITY_FACTOR * n_ctx // N_EXPERTS
    cap_l = CAPACITY_FACTOR * n_lat // N_EXPERTS
    assert cap_l % 256 == 0 and cap_c % 8 == 0
    ncb = n_ctx // TOK_BLOCK

    cond = jnp.concatenate([c, c_ctx[None], jnp.zeros((ADA_ROWS - B - 1, D), F32)], axis=0)
    mods = _ada_call(cond.T, w_ada, b_ada, B + 1).reshape(depth * ADA_ROWS * 6, 1, D)

    w_in_p, w_kv, w_q, w_r = _prep_weights(w_in, w_uq, w_ukv, w_router)
    w_oa_b, w_ob_b, w_out_b = w_oa.astype(BF16), w_ob.astype(BF16), w_out.astype(BF16)
    tab = _rope_tables(n_ctx, n_lat)
    vones = _ones_columns()

    x_all = (ctx, x)
    moe = None
    for l in range(depth):
        x_res, (qwin, kwin, vwin, qcat, kcat, vext, sg) = _proj_call(
            l, alpha, x_all, moe, ln2_g, ln2_b, mods, tab, w_in_p, mla_kv_norm, mla_q_norm,
            w_kv, w_q, vones, ncb)
        oa = _win_call(l, attn_sink, qwin, kwin, vwin, n_ctx)
        ob = _mla_call(qcat, kcat, vext, n_ctx)
        x1, h2, lg_t = _merge_call(l, alpha, x_res, oa, ob, sg, w_oa_b, w_ob_b, w_out_b,
                                   ln1_g, ln1_b, mods, w_r, ncb)
        aff, idx4 = _route_call(lg_t, n_ctx, cap_c, cap_l)
        idx = idx4.reshape(B, N_EXPERTS, cap_c + cap_l)
        xg = _gather_call(idx, h2)
        moe = _moe_call(l, idx, aff, xg, w_exp_gate, w_exp_up, w_exp_down, S)
        x_all = x1
    return _final_call(depth - 1, alpha, x_all, moe, ln2_g, ln2_b, mods, n_ctx)
```

```python
import functools
import math

import jax
import jax.numpy as jnp
from jax import lax
from jax.experimental import pallas as pl
from jax.experimental.pallas import tpu as pltpu

D_MODEL = 1024
GRID_W = 64
WIN_HEADS = 8
WIN_KV_HEADS = 2
WIN_HEAD_DIM = 64
BAND = 128
MLA_HEADS = 8
MLA_Q_RANK = 384
MLA_KV_RANK = 256
MLA_NOPE = 64
MLA_ROPE = 32
MLA_V = 64
N_EXPERTS = 16
CAPACITY_FACTOR = 2
ROPE_THETA = 10000.0
LN_EPS = 1e-5
RMS_EPS = 1e-6
NEG_INF = -1e30
LOG2E = math.log2(math.e)
WIN_SCALE = WIN_HEAD_DIM ** -0.5
MLA_SCALE = (MLA_NOPE + MLA_ROPE) ** -0.5

LANES = 128
TOK_BLOCK = 256
VMEM_LIMIT = 56 * 1024 * 1024

BF16 = jnp.bfloat16
F32 = jnp.float32

SEG_KWIN = (0, 0, 128)
SEG_VWIN = (0, 128, 256)
SEG_CKV = (0, 256, 512)
SEG_KR = (1, 0, 128)
SEG_QWIN = (2, 0, 512)
SEG_CQ = (2, 512, 896)
SEG_GATE = (2, 896, 2944)


def _dot(a, b):
    return jnp.dot(a, b, preferred_element_type=F32)


def _dot_nt(a, b):
    return lax.dot_general(a, b, (((1,), (1,)), ((), ())), preferred_element_type=F32)


def _layer_norm(z, g, b):
    mu = jnp.mean(z, axis=-1, keepdims=True)
    zc = z - mu
    var = jnp.mean(zc * zc, axis=-1, keepdims=True)
    return zc * lax.rsqrt(var + LN_EPS) * g + b


def _rms_norm(x, g):
    return x * lax.rsqrt(jnp.mean(x * x, axis=-1, keepdims=True) + RMS_EPS) * g


def _rope_slab(x, cos, sin, half):
    lane = lax.broadcasted_iota(jnp.int32, x.shape, 1)
    partner = jnp.where((lane & half) == 0,
                        pltpu.roll(x, LANES - half, 1), pltpu.roll(x, half, 1))
    return x * cos + partner * sin


SUBLANES = 8


def _rows_to_tiles(y):
    n = y.shape[0]
    y3 = pltpu.einshape("r(sl)->srl", y, s=SUBLANES)
    y4 = y3.reshape(SUBLANES, n // SUBLANES, SUBLANES, LANES)
    return jnp.transpose(y4, (1, 2, 0, 3)).reshape(n, SUBLANES, LANES)


def _tiles_to_rows(x3):
    n = x3.shape[0]
    x4 = x3.reshape(n // SUBLANES, SUBLANES, SUBLANES, LANES)
    xs = jnp.transpose(x4, (2, 0, 1, 3)).reshape(SUBLANES, n, LANES)
    return pltpu.einshape("srl->r(sl)", xs)


def _compiler_params(sem):
    return pltpu.CompilerParams(dimension_semantics=sem, vmem_limit_bytes=VMEM_LIMIT)


ADA_ROWS = 8
ADA_TN = 512


def _ada_kernel(n_rows, condt_ref, w_ref, b_ref, o_ref):
    ct = condt_ref[...]
    st = ct * jax.nn.sigmoid(ct)
    w = w_ref[0]
    rows = []
    for r in range(n_rows):
        rows.append(jnp.sum(w * st[:, r:r + 1], axis=0, keepdims=True) + b_ref[0])
    rows.append(jnp.zeros((ADA_ROWS - n_rows, w.shape[1]), F32))
    o_ref[0] = jnp.concatenate(rows, axis=0)


def _ada_call(cond_t, w_ada, b_ada, n_rows):
    L, D, N = w_ada.shape
    return pl.pallas_call(
        functools.partial(_ada_kernel, n_rows),
        grid=(L, N // ADA_TN),
        in_specs=[
            pl.BlockSpec((D, ADA_ROWS), lambda l, j: (0, 0)),
            pl.BlockSpec((1, D, ADA_TN), lambda l, j: (l, 0, j)),
            pl.BlockSpec((1, 1, ADA_TN), lambda l, j: (l, 0, j)),
        ],
        out_specs=pl.BlockSpec((1, ADA_ROWS, ADA_TN), lambda l, j: (l, 0, j)),
        out_shape=jax.ShapeDtypeStruct((L, ADA_ROWS, N), F32),
        compiler_params=_compiler_params(("parallel", "parallel")),
        name="adaln_mod",
    )(cond_t, w_ada, b_ada.reshape(L, 1, N))


def _proj_kernel(has_ln, alpha, n_ctx_blocks, ns, *refs):
    refs = list(refs)
    if has_ln:
        x_ref, moe_ref, g2_ref, b2_ref = refs[:4]
        m5_refs, refs = refs[4:4 + ns], refs[4 + ns:]
    else:
        (ctx_ref, x_ref), refs = refs[:2], refs[2:]
    mod_refs, refs = refs[:2 * ns], refs[2 * ns:]
    (tab_ref, wa_ref, wb_ref, wc_ref, gkv_ref, gq_ref, wkv_ref, wq_ref, vones_ref,
     xres_ref, qwin_ref, kwin_ref, vwin_ref, qcat_ref, kcat_ref, vext_ref, sg_ref) = refs
    win_refs = (wa_ref, wb_ref, wc_ref)
    T = x_ref.shape[1]
    hbs = []
    for n in range(ns):
        if has_ln:
            z = alpha * x_ref[n] + m5_refs[n][0] * _tiles_to_rows(moe_ref[n])
            x = _layer_norm(z, g2_ref[0], b2_ref[0])
        else:
            x = jnp.where(pl.program_id(0) < n_ctx_blocks, ctx_ref[n], x_ref[n])
        xres_ref[n] = x
        sh_ref, sc_ref = mod_refs[2 * n:2 * n + 2]
        hbs.append((x * (1.0 + sc_ref[0]) + sh_ref[0]).astype(BF16))
    hb = jnp.concatenate(hbs, axis=0)
    split = lambda v: v.reshape(ns, T, v.shape[1])

    def seg(s):
        return _dot(hb, win_refs[s[0]][0, :, s[1]:s[2]])

    def tab(i):
        t = tab_ref[:, i * LANES:(i + 1) * LANES]
        return jnp.concatenate([t] * ns, axis=0)

    cq_w, sq_w, ck_w, sk_w, cq_m, sq_m, ck_m, sk_m = (tab(i) for i in range(8))

    lane_lo = lax.broadcasted_iota(jnp.int32, (hb.shape[0], LANES), 1) < WIN_HEAD_DIM
    half_swap = lambda t: pltpu.roll(t, WIN_HEAD_DIM, 1)

    ckv = _rms_norm(seg(SEG_CKV), gkv_ref[0]).astype(BF16)
    cq = _rms_norm(seg(SEG_CQ), gq_ref[0]).astype(BF16)

    qw = seg(SEG_QWIN)
    q_slabs = []
    for s2 in range(WIN_HEADS // 2):
        pair = _rope_slab(qw[:, s2 * LANES:(s2 + 1) * LANES], cq_w, sq_w, 16)
        pair_sw = half_swap(pair)
        if 2 * s2 < WIN_HEADS // WIN_KV_HEADS:
            q_slabs += [jnp.where(lane_lo, pair, 0.0), jnp.where(lane_lo, pair_sw, 0.0)]
        else:
            q_slabs += [jnp.where(lane_lo, 0.0, pair_sw), jnp.where(lane_lo, 0.0, pair)]
    qwin_ref[...] = split(jnp.concatenate(q_slabs, axis=1).astype(BF16))

    kv = _dot(ckv, wkv_ref[0])
    kr = _rope_slab(seg(SEG_KR), ck_m, sk_m, 8)
    qm = _dot(cq, wq_ref[0])
    for h in range(MLA_HEADS):
        sl = slice(h * LANES, (h + 1) * LANES)
        kcat_ref[:, :, sl] = split((kv[:, sl] + kr).astype(BF16))
        vext_ref[:, :, sl] = split(
            (kv[:, 1024 + h * LANES:1024 + (h + 1) * LANES]
             + vones_ref[:, 512 + h * LANES:512 + (h + 1) * LANES]).astype(BF16))
        qcat_ref[:, :, sl] = split(_rope_slab(qm[:, sl], cq_m, sq_m, 8).astype(BF16))

    sg_ref[...] = split(jax.nn.sigmoid(seg(SEG_GATE)).astype(BF16))
    va = seg(SEG_VWIN)
    va_sw = half_swap(va)
    vwin_ref[...] = split((jnp.concatenate(
        [jnp.where(lane_lo, va, 0.0), jnp.where(lane_lo, 0.0, va_sw),
         jnp.where(lane_lo, va_sw, 0.0), jnp.where(lane_lo, 0.0, va)], axis=1)
        + vones_ref[:, 0:512]).astype(BF16))
    kwin_ref[...] = split(_rope_slab(seg(SEG_KWIN), ck_w, sk_w, 16).astype(BF16))


def _proj_call(layer, alpha, x_all, moe, ln2_g, ln2_b, mods, tab, w_in_p, g_kvn, g_qn, w_kv, w_q,
               vones, n_ctx_blocks):
    T = TOK_BLOCK
    has_ln = moe is not None
    if has_ln:
        B, S, D = x_all.shape
    else:
        ctx_in, x_in = x_all
        B, D = x_in.shape[0], x_in.shape[2]
        S = ctx_in.shape[1] + x_in.shape[1]
    L = w_in_p[0].shape[0]
    ns = PROJ_SAMPLES if B % PROJ_SAMPLES == 0 else 1

    def mod_spec(k, lyr, n):
        def imap(i, b):
            row = jnp.where(i < n_ctx_blocks, B, ns * b + n)
            return ((lyr * ADA_ROWS + row) * 6 + k, 0, 0)
        return pl.BlockSpec((1, 1, D), imap)

    tok = lambda w: pl.BlockSpec((ns, T, w), lambda i, b: (b, i, 0))
    headed = tok(MLA_HEADS * LANES)
    const = lambda shp: pl.BlockSpec(shp, lambda i, b: (layer,) + (0,) * (len(shp) - 1),
                                     pipeline_mode=pl.Buffered(1))

    if has_ln:
        prev = pl.BlockSpec((1, 1, D), lambda i, b: (layer - 1, 0, 0))
        tiled = pl.BlockSpec((ns, T, SUBLANES, LANES), lambda i, b: (b, i, 0, 0))
        in_specs = [tok(D), tiled, prev, prev] + [mod_spec(5, layer - 1, n) for n in range(ns)]
        args = [x_all, moe, ln2_g.reshape(L, 1, D), ln2_b.reshape(L, 1, D)] + [mods] * ns
    else:
        assert ctx_in.shape[1] == n_ctx_blocks * T
        in_specs = [pl.BlockSpec((ns, T, D), lambda i, b: (b, jnp.minimum(i, n_ctx_blocks - 1), 0)),
                    pl.BlockSpec((ns, T, D), lambda i, b: (b, jnp.maximum(i - n_ctx_blocks, 0), 0))]
        args = [ctx_in, x_in]
    in_specs += [mod_spec(k, layer, n) for n in range(ns) for k in (0, 1)]
    args += [mods] * (2 * ns)
    in_specs += [
        pl.BlockSpec((T, 8 * LANES), lambda i, b: (i, 0)),
        *[const((1, D, w.shape[2])) for w in w_in_p],
        const((1, 1, MLA_KV_RANK)), const((1, 1, MLA_Q_RANK)),
        const((1, MLA_KV_RANK, 2048)), const((1, MLA_Q_RANK, 1024)),
        pl.BlockSpec((1, 1536), lambda i, b: (0, 0)),
    ]
    args += [tab, *w_in_p, g_kvn.reshape(L, 1, -1), g_qn.reshape(L, 1, -1), w_kv, w_q, vones]

    out_specs, out_shape = [tok(D)], [jax.ShapeDtypeStruct((B, S, D), F32)]
    out_specs += [tok(WIN_HEADS * LANES), tok(LANES), tok(512), headed, headed, headed, tok(2048)]
    out_shape += [jax.ShapeDtypeStruct((B, S, WIN_HEADS * LANES), BF16),
                  jax.ShapeDtypeStruct((B, S, LANES), BF16),
                  jax.ShapeDtypeStruct((B, S, 512), BF16)]
    out_shape += [jax.ShapeDtypeStruct((B, S, MLA_HEADS * LANES), BF16)] * 3
    out_shape += [jax.ShapeDtypeStruct((B, S, 2048), BF16)]

    outs = pl.pallas_call(
        functools.partial(_proj_kernel, has_ln, alpha, n_ctx_blocks, ns),
        grid=(S // T, B // ns),
        in_specs=in_specs, out_specs=out_specs, out_shape=out_shape,
        compiler_params=_compiler_params(("parallel", "parallel")),
        name="proj",
    )(*args)
    return outs[0], outs[1:]


WIN_HEAD_ORDER = (0, 2, 1, 3, 4, 6, 5, 7)
WIN_SAMPLES = 4


def _win_kernel(layer, n_ctx, n_steps, sink_ref, q_ref, kc_ref, vc_ref, kp_ref, km_ref, kn_ref,
                vp_ref, vm_ref, vn_ref, o_ref, kall_ref, vall_ref, p_ref):
    i = pl.program_id(1)
    row = lax.broadcasted_iota(jnp.int32, (BAND, LANES), 0)
    col = lax.broadcasted_iota(jnp.int32, (BAND, LANES), 1)
    lane_lo = col < 64
    ns = q_ref.shape[0]
    for n in range(ns):
        kall_ref[n, 0:n_ctx, :] = kc_ref[n]
        vall_ref[n, 0:n_ctx, :] = vc_ref[n]

    def scores(n, qb, r0, nk):
        q = q_ref[n, qb * BAND:(qb + 1) * BAND, :]
        q8 = jnp.concatenate([q[:, h * LANES:(h + 1) * LANES] for h in WIN_HEAD_ORDER], axis=0)
        return _dot_nt(q8, kall_ref[n, r0:r0 + nk, :])

    def softmax(n, qb, nk, masks, s):
        e_sink = []
        for c, h in enumerate(WIN_HEAD_ORDER):
            sc = s[c * BAND:(c + 1) * BAND, :]
            if masks:
                blocks = []
                for kb in range(nk // BAND):
                    blk = sc[:, kb * BAND:(kb + 1) * BAND]
                    blocks.append(jnp.where(masks[kb], blk, NEG_INF) if kb in masks else blk)
                sc = jnp.concatenate(blocks, axis=1)
            sink = sink_ref[layer, h] * LOG2E
            m = jnp.maximum(jnp.max(sc, axis=1, keepdims=True), sink)
            p_ref[n, qb, c * BAND:(c + 1) * BAND, 0:nk] = jnp.exp2(sc - m).astype(BF16)
            e_sink.append(jnp.exp2(sink - m))
        return e_sink

    def values(n, qb, r0, nk, e_sink):
        pairs = []
        for pi in range(4):
            acc = _dot(p_ref[n, qb, pi * 2 * BAND:(pi + 1) * 2 * BAND, 0:nk],
                       vall_ref[n, r0:r0 + nk, pi * LANES:(pi + 1) * LANES])
            es = jnp.concatenate([e_sink[2 * pi], e_sink[2 * pi + 1]], axis=0)
            den = (acc[:, 64:65] if pi % 2 == 0 else acc[:, 0:1]) + es
            pairs.append(acc / den)
        slabs = [jnp.where(lane_lo, pairs[0][:BAND], pairs[1][:BAND]),
                 jnp.where(lane_lo, pairs[0][BAND:], pairs[1][BAND:]),
                 jnp.where(lane_lo, pairs[2][:BAND], pairs[3][:BAND]),
                 jnp.where(lane_lo, pairs[2][BAND:], pairs[3][BAND:])]
        o_ref[n, qb * BAND:(qb + 1) * BAND, :] = jnp.concatenate(slabs, axis=1).astype(o_ref.dtype)

    def attend_both(args0, args1):
        (r0, nk0, m0), (r1, nk1, m1) = args0, args1
        s = [(scores(n, 0, r0, nk0), scores(n, 1, r1, nk1)) for n in range(ns)]
        e = [(softmax(n, 0, nk0, m0, s[n][0]), softmax(n, 1, nk1, m1, s[n][1])) for n in range(ns)]
        for n in range(ns):
            values(n, 0, r0, nk0, e[n][0])
            values(n, 1, r1, nk1, e[n][1])

    @pl.when(i == 0)
    def _():
        attend_both((0, n_ctx, None), (0, n_ctx, None))

    @pl.when(i > 0)
    def _():
        for n in range(ns):
            r = n_ctx
            for k_ref, v_ref, rows in ((kp_ref, vp_ref, BAND), (km_ref, vm_ref, 2 * BAND),
                                       (kn_ref, vn_ref, BAND), (kc_ref, vc_ref, n_ctx)):
                kall_ref[n, r:r + rows, :] = k_ref[n]
                vall_ref[n, r:r + rows, :] = v_ref[n]
                r += rows
        off_first = jnp.where(i == 1, 2 * LANES, 0)
        off_last = jnp.where(i == n_steps - 1, 2 * LANES, 0)
        cb = n_ctx // BAND
        attend_both((0, n_ctx + 3 * BAND, {cb: col >= row + off_first, cb + 2: col <= row}),
                    (n_ctx + BAND, n_ctx + 3 * BAND, {0: col >= row, 2: col <= row - off_last}))


def _win_call(layer, sink, qwin, kwin, vwin, n_ctx):
    B, S, _ = qwin.shape
    nb = S // BAND
    ncb = n_ctx // BAND
    assert n_ctx == 2 * BAND and nb % 2 == 0
    nk = n_ctx + 3 * BAND
    ns = WIN_SAMPLES if B % WIN_SAMPLES == 0 else 1
    one = lambda w, f: pl.BlockSpec((ns, BAND, w), lambda b, i: (b, jnp.clip(f(i), ncb, nb - 1), 0))
    two = lambda w: pl.BlockSpec((ns, 2 * BAND, w), lambda b, i: (b, i, 0))
    ctx = lambda w: pl.BlockSpec((ns, n_ctx, w), lambda b, i: (b, 0, 0))
    prev, nxt = (lambda i: 2 * i - 1), (lambda i: 2 * i + 2)
    return pl.pallas_call(
        functools.partial(_win_kernel, layer, n_ctx, nb // 2),
        grid=(B // ns, nb // 2),
        in_specs=[pl.BlockSpec(memory_space=pltpu.SMEM), two(WIN_HEADS * LANES),
                  ctx(LANES), ctx(512), one(LANES, prev), two(LANES), one(LANES, nxt),
                  one(512, prev), two(512), one(512, nxt)],
        out_specs=two(512),
        out_shape=jax.ShapeDtypeStruct((B, S, 512), BF16),
        scratch_shapes=[pltpu.VMEM((ns, 2 * n_ctx + 4 * BAND, LANES), BF16),
                        pltpu.VMEM((ns, 2 * n_ctx + 4 * BAND, 512), BF16),
                        pltpu.VMEM((ns, 2, WIN_HEADS * BAND, nk), BF16)],
        compiler_params=_compiler_params(("parallel", "parallel")),
        name="win_attn",
    )(sink, qwin, kwin, vwin, kwin, kwin, kwin, vwin, vwin, vwin)


MLA_KCHUNK = 256
MLA_HEADS_PER_STEP = 4


def _mla_kernel(n_ctx, n_keys, q_ref, k_ref, v_ref, o_ref, s_ref):
    qi = pl.program_id(2)
    tq = q_ref.shape[1]
    lane = lax.broadcasted_iota(jnp.int32, (tq, LANES), 1)

    def run(nk):
        row_max = []
        for hh in range(MLA_HEADS_PER_STEP):
            hl = slice(hh * LANES, (hh + 1) * LANES)
            q = q_ref[0, :, hl]
            mrun = None
            for c in range(nk // MLA_KCHUNK):
                s = _dot_nt(q, k_ref[0, c * MLA_KCHUNK:(c + 1) * MLA_KCHUNK, hl])
                s_ref[hh, c] = s
                mc = jnp.maximum(s[:, :LANES], s[:, LANES:])
                mrun = mc if mrun is None else jnp.maximum(mrun, mc)
            row_max.append(jnp.max(mrun, axis=1, keepdims=True))
        outs = []
        for hh in range(MLA_HEADS_PER_STEP):
            acc = jnp.zeros((tq, LANES), F32)
            for c in range(nk // MLA_KCHUNK):
                p = jnp.exp2(s_ref[hh, c] - row_max[hh]).astype(BF16)
                acc = acc + _dot(p, v_ref[0, c * MLA_KCHUNK:(c + 1) * MLA_KCHUNK,
                                          hh * LANES:(hh + 1) * LANES])
            den = acc[:, 64:65] if hh % 2 == 0 else acc[:, 0:1]
            outs.append(acc / den)
        o_ref[0] = jnp.concatenate(
            [jnp.where(lane < 64, outs[2 * j], outs[2 * j + 1]) for j in range(len(outs) // 2)],
            axis=1).astype(o_ref.dtype)

    @pl.when(qi == 0)
    def _():
        run(n_ctx)

    @pl.when(qi > 0)
    def _():
        run(n_keys)


def _mla_call(qcat, kcat, vext, n_ctx):
    B, S, _ = qcat.shape
    H = MLA_HEADS
    T = TOK_BLOCK
    assert n_ctx == T
    G = MLA_HEADS_PER_STEP
    kv = pl.BlockSpec((1, S, G * LANES), lambda b, hp, i: (b, 0, hp))
    return pl.pallas_call(
        functools.partial(_mla_kernel, n_ctx, S),
        grid=(B, H // G, S // T),
        in_specs=[pl.BlockSpec((1, T, G * LANES), lambda b, hp, i: (b, i, hp)), kv, kv],
        out_specs=pl.BlockSpec((1, T, (G // 2) * LANES), lambda b, hp, i: (b, i, hp)),
        out_shape=jax.ShapeDtypeStruct((B, S, (H // 2) * LANES), BF16),
        scratch_shapes=[pltpu.VMEM((G, S // MLA_KCHUNK, T, MLA_KCHUNK), F32)],
        compiler_params=_compiler_params(("parallel", "parallel", "arbitrary")),
        name="mla_attn",
    )(qcat, kcat, vext)


MERGE_SAMPLES = 4
PROJ_SAMPLES = 1


def _merge_kernel(alpha, ns, x_ref, oa_ref, ob_ref, sg_ref, woa_ref, wob_ref, wout_ref, g1_ref,
                  b1_ref, *refs):
    mod_refs, (wr_ref, x1_ref, h2_ref, lg_ref) = refs[:3 * ns], refs[3 * ns:]
    T, D = x_ref.shape[1], x_ref.shape[2]
    rows = lambda ref: ref[...].reshape(ns * T, ref.shape[2])
    sg = rows(sg_ref)
    t = (sg[:, :D].astype(F32) * _dot(rows(oa_ref), woa_ref[0])
         + sg[:, D:].astype(F32) * _dot(rows(ob_ref), wob_ref[0]))
    y = _dot(t.astype(BF16), wout_ref[0])
    h2s = []
    for n in range(ns):
        m2_ref, m3_ref, m4_ref = mod_refs[3 * n:3 * n + 3]
        x1 = _layer_norm(alpha * x_ref[n] + m2_ref[0] * y[n * T:(n + 1) * T],
                         g1_ref[0], b1_ref[0])
        x1_ref[n] = x1
        h2 = x1 * (1.0 + m4_ref[0]) + m3_ref[0]
        h2_ref[n] = _rows_to_tiles(h2)
        h2s.append(h2.astype(BF16))
    lg = _dot(jnp.concatenate(h2s, axis=0), wr_ref[0])
    for n in range(ns):
        lg_ref[n] = lg[n * T:(n + 1) * T].T[:N_EXPERTS]


def _merge_call(layer, alpha, x_res, oa, ob, sg, w_oa, w_ob, w_out, ln1_g, ln1_b, mods, w_r,
                n_ctx_blocks):
    B, S, D = x_res.shape
    T = TOK_BLOCK
    L = w_oa.shape[0]
    ns = MERGE_SAMPLES if B % MERGE_SAMPLES == 0 else 1

    def mod_spec(k, n):
        def imap(bp, i):
            row = jnp.where(i < n_ctx_blocks, B, ns * bp + n)
            return ((layer * ADA_ROWS + row) * 6 + k, 0, 0)
        return pl.BlockSpec((1, 1, D), imap)

    tok = lambda w: pl.BlockSpec((ns, T, w), lambda bp, i: (bp, i, 0))
    const = lambda shp: pl.BlockSpec(shp, lambda bp, i: (layer,) + (0,) * (len(shp) - 1),
                                     pipeline_mode=pl.Buffered(1))
    mod_specs = [mod_spec(k, n) for n in range(ns) for k in (2, 3, 4)]
    return pl.pallas_call(
        functools.partial(_merge_kernel, alpha, ns),
        grid=(B // ns, S // T),
        in_specs=[tok(D), tok(512), tok(512), tok(2048),
                  const((1, 512, D)), const((1, 512, D)), const((1, D, D)),
                  const((1, 1, D)), const((1, 1, D))] + mod_specs + [const((1, D, LANES))],
        out_specs=[tok(D), pl.BlockSpec((ns, T, SUBLANES, LANES), lambda bp, i: (bp, i, 0, 0)),
                   pl.BlockSpec((ns, N_EXPERTS, T), lambda bp, i: (bp, 0, i))],
        out_shape=[jax.ShapeDtypeStruct((B, S, D), F32),
                   jax.ShapeDtypeStruct((B, S, SUBLANES, LANES), F32),
                   jax.ShapeDtypeStruct((B, N_EXPERTS, S), F32)],
        compiler_params=_compiler_params(("parallel", "parallel")),
        name="merge",
    )(x_res, oa, ob, sg, w_oa, w_ob, w_out, ln1_g.reshape(L, 1, D), ln1_b.reshape(L, 1, D),
      *([mods] * (3 * ns)), w_r)


def _cumsum_lanes(x):
    n = x.shape[1]
    xb = x.astype(BF16)
    r = lax.broadcasted_iota(jnp.int32, (LANES, LANES), 0)
    c = lax.broadcasted_iota(jnp.int32, (LANES, LANES), 1)
    tri = jnp.where(r <= c, 1.0, 0.0).astype(BF16)
    tb = lax.broadcasted_iota(jnp.int32, (n, LANES), 0) // LANES
    kb = lax.broadcasted_iota(jnp.int32, (n, LANES), 1)
    before = jnp.where(tb < kb, 1.0, 0.0).astype(BF16)
    off = _dot(xb, before)
    outs = []
    for k in range(n // LANES):
        outs.append(_dot(xb[:, k * LANES:(k + 1) * LANES], tri) + off[:, k:k + 1])
    return outs


def _select_top(aff, cap):
    bits = pltpu.bitcast(aff, jnp.int32)
    thr = jnp.zeros((aff.shape[0], 1), jnp.int32)
    for bit in range(30, -1, -1):
        cand = thr | (1 << bit)
        cnt = jnp.sum(jnp.where(bits >= cand, 1.0, 0.0), axis=1, keepdims=True)
        thr = jnp.where(cnt >= cap, cand, thr)
    gt = bits > thr
    eq = jnp.where(bits == thr, 1.0, 0.0)
    need = cap - jnp.sum(jnp.where(gt, 1.0, 0.0), axis=1, keepdims=True)
    eq_rank = jnp.concatenate(_cumsum_lanes(eq), axis=1) - eq
    sel = jnp.where(jnp.logical_or(gt, jnp.logical_and(eq > 0.5, eq_rank < need)), 1.0, 0.0)
    return _cumsum_lanes(sel)


ROUTE_PAD = 1e6


def _route_kernel(n_ctx, cap_c, cap_l, lg_ref, aff_ref, idx_ref, cc_ref, cl_ref):
    affs = []
    for b in range(lg_ref.shape[0]):
        lg = lg_ref[b]
        m = jnp.max(lg, axis=0, keepdims=True)
        ex = jnp.exp(lg - m)
        affs.append(ex / jnp.sum(ex, axis=0, keepdims=True))
        aff_ref[b] = affs[-1]
    aff = jnp.concatenate(affs, axis=0)
    n_rows = aff.shape[0]
    for c_ref, blocks in ((cc_ref, _select_top(aff[:, :n_ctx], cap_c)),
                          (cl_ref, _select_top(aff[:, n_ctx:], cap_l))):
        c_ref[...] = jnp.full(c_ref.shape, ROUTE_PAD, F32)
        for k, blk in enumerate(blocks):
            for r in range(n_rows):
                c_ref[r, k:k + 1, :] = blk[r:r + 1, :]

    def slots(c_ref, e, n_blocks, width):
        cm = c_ref[e]
        jrow = lax.broadcasted_iota(jnp.int32, (1, width), 1).astype(F32)
        rows = max(n_blocks, SUBLANES)
        cend = cm[0:rows, LANES - 1:LANES]
        blk = jnp.sum(jnp.where(cend <= jrow, 1.0, 0.0), axis=0, keepdims=True)
        kio = lax.broadcasted_iota(jnp.int32, (LANES, width), 0).astype(F32)
        onehot = jnp.where(kio == blk, 1.0, 0.0).astype(BF16)
        cmt = cm.T
        hi = jnp.floor(cmt * (1.0 / 256.0))
        lo = cmt - 256.0 * hi
        cg = 256.0 * _dot(hi.astype(BF16), onehot) + _dot(lo.astype(BF16), onehot)
        inside = jnp.sum(jnp.where(cg <= jrow, 1.0, 0.0), axis=0, keepdims=True)
        return (blk * LANES + inside).astype(jnp.int32)

    def per_row(r, carry):
        n_lat_blocks = (lg_ref.shape[2] - n_ctx) // LANES
        idx_ref[r, :, 0:cap_l] = slots(cl_ref, r, n_lat_blocks, cap_l) + n_ctx
        idx_ref[r, :, cap_l:cap_l + cap_c] = slots(cc_ref, r, n_ctx // LANES, LANES)[:, 0:cap_c]
        return carry

    lax.fori_loop(0, n_rows, per_row, 0, unroll=2)


def _route_call(lg_t, n_ctx, cap_c, cap_l):
    B, E, S = lg_t.shape
    cap = cap_c + cap_l
    assert (S - n_ctx) // LANES <= LANES and cap_l % LANES == 0 and cap_c <= LANES
    whole = lambda shp: pl.BlockSpec(shp, lambda i: (0,) * len(shp))
    return pl.pallas_call(
        functools.partial(_route_kernel, n_ctx, cap_c, cap_l),
        grid=(1,),
        in_specs=[whole((B, E, S))],
        out_specs=[whole((B, E, S)), whole((B * E, 1, cap))],
        out_shape=[jax.ShapeDtypeStruct((B, E, S), F32),
                   jax.ShapeDtypeStruct((B * E, 1, cap), jnp.int32)],
        scratch_shapes=[pltpu.VMEM((B * E, LANES, LANES), F32),
                        pltpu.VMEM((B * E, LANES, LANES), F32)],
        compiler_params=_compiler_params(("arbitrary",)),
        name="route",
    )(lg_t)


ROW_BATCH = 16
MOE_FF_CHUNK = 512


GATHER_EXPERTS = 4


def _gather_kernel(idx_ref, h_ref, xg_ref, tmp_ref):
    cap = xg_ref.shape[2]
    for ge in range(GATHER_EXPERTS):
        def body(g, carry):
            for r in range(ROW_BATCH):
                j = g * ROW_BATCH + r
                tmp_ref[j] = h_ref[0, idx_ref[0, 0, ge * cap + j]]
            return carry

        lax.fori_loop(0, cap // ROW_BATCH, body, 0)
        xg_ref[0, ge] = _tiles_to_rows(tmp_ref[...]).astype(xg_ref.dtype)


def _gather_call(idx, h2t):
    B, S = h2t.shape[0], h2t.shape[1]
    D = SUBLANES * LANES
    E, cap = idx.shape[1], idx.shape[2]
    G = GATHER_EXPERTS
    assert cap % ROW_BATCH == 0 and E % G == 0
    return pl.pallas_call(
        _gather_kernel,
        grid=(B, E // G),
        in_specs=[pl.BlockSpec((1, 1, G * cap), lambda b, e: (b * (E // G) + e, 0, 0),
                               memory_space=pltpu.SMEM),
                  pl.BlockSpec((1, S, SUBLANES, LANES), lambda b, e: (b, 0, 0, 0))],
        out_specs=pl.BlockSpec((1, G, cap, D), lambda b, e: (b, e, 0, 0)),
        out_shape=jax.ShapeDtypeStruct((B, E, cap, D), BF16),
        scratch_shapes=[pltpu.VMEM((cap, SUBLANES, LANES), F32)],
        compiler_params=_compiler_params(("parallel", "arbitrary")),
        name="moe_gather",
    )(idx.reshape(B * E // G, 1, G * cap), h2t)


def _moe_kernel(idx_ref, aff_ref, xg_ref, wg_ref, wu_ref, wd_ref, out_ref, y_ref):
    s = pl.program_id(1)
    n_exp = pl.num_programs(1) - 1
    cap = y_ref.shape[1]
    n_batches = cap // ROW_BATCH

    def scatter(slot, batches):
        new = None
        for g in batches:
            rows = _rows_to_tiles(y_ref[slot, g * ROW_BATCH:(g + 1) * ROW_BATCH, :])
            toks = [idx_ref[0, 0, g * ROW_BATCH + r] for r in range(ROW_BATCH)]
            old = [out_ref[0, t] for t in toks]
            for r, t in enumerate(toks):
                new = old[r] + rows[r] * aff_ref[0, 0, t]
                out_ref[0, t] = new
        return new

    def ffn(slot, prev_slot):
        ff = wg_ref.shape[3]
        n_dots = 3 * (ff // MOE_FF_CHUNK)
        per = -(-n_batches // n_dots)
        todo = list(range(n_batches)) if prev_slot is not None else []

        def weights(w):
            take, todo[:] = todo[:per], todo[per:]
            w = w.astype(BF16)
            if not take:
                return w
            last = scatter(prev_slot, take)
            bits = pltpu.bitcast(last, jnp.uint32)
            zero = pltpu.bitcast((bits >> 16) >> 16, F32)
            zero = jnp.concatenate([zero, zero], axis=0).astype(BF16)
            return w + jnp.tile(zero, (w.shape[0] // zero.shape[0], w.shape[1] // LANES))

        x = xg_ref[0, 0]
        y = None
        for f0 in range(0, ff, MOE_FF_CHUNK):
            f1 = f0 + MOE_FF_CHUNK
            a = _dot(x, weights(wg_ref[0, 0, :, f0:f1]))
            u = _dot(x, weights(wu_ref[0, 0, :, f0:f1]))
            hmid = (a * jax.nn.sigmoid(a) * u).astype(BF16)
            part = _dot(hmid, weights(wd_ref[0, 0, f0:f1, :]))
            y = part if y is None else y + part
        y_ref[slot] = y

    @pl.when(s == 0)
    def _():
        out_ref[...] = jnp.zeros(out_ref.shape, out_ref.dtype)
        ffn(0, None)

    @pl.when(jnp.logical_and(s > 0, s < n_exp))
    def _():
        slot = s % 2
        ffn(slot, 1 - slot)

    @pl.when(s == n_exp)
    def _():
        scatter((n_exp - 1) % 2, range(n_batches))


def _moe_call(layer, idx, aff, xg, w_gate, w_up, w_down, S):
    B, E, cap, D = xg.shape
    F = w_gate.shape[3]
    cur = lambda s: jnp.minimum(s, E - 1)
    prev = lambda s: jnp.maximum(s - 1, 0)
    wspec = lambda shp: pl.BlockSpec(shp, lambda b, s: (layer, cur(s), 0, 0))
    return pl.pallas_call(
        _moe_kernel,
        grid=(B, E + 1),
        in_specs=[pl.BlockSpec((1, 1, cap), lambda b, s: (b * E + prev(s), 0, 0), memory_space=pltpu.SMEM),
                  pl.BlockSpec((1, 1, S), lambda b, s: (b * E + prev(s), 0, 0), memory_space=pltpu.SMEM),
                  pl.BlockSpec((1, 1, cap, D), lambda b, s: (b, cur(s), 0, 0)),
                  wspec((1, 1, D, F)), wspec((1, 1, D, F)), wspec((1, 1, F, D))],
        out_specs=pl.BlockSpec((1, S, SUBLANES, LANES), lambda b, s: (b, 0, 0, 0),
                               pipeline_mode=pl.Buffered(1)),
        out_shape=jax.ShapeDtypeStruct((B, S, SUBLANES, LANES), F32),
        scratch_shapes=[pltpu.VMEM((2, cap, D), F32)],
        compiler_params=_compiler_params(("parallel", "arbitrary")),
        name="moe_ffn",
    )(idx.reshape(B * E, 1, cap), aff.reshape(B * E, 1, S), xg, w_gate, w_up, w_down)


def _final_kernel(alpha, ns, x_ref, moe_ref, g_ref, b_ref, *refs):
    o_ref = refs[ns]
    for n in range(ns):
        o_ref[n] = _layer_norm(alpha * x_ref[n] + refs[n][0] * _tiles_to_rows(moe_ref[n]),
                               g_ref[0], b_ref[0])


def _final_call(layer, alpha, x1, moe, ln2_g, ln2_b, mods, n_ctx):
    B, S, D = x1.shape
    T = TOK_BLOCK
    L = ln2_g.shape[0]
    ncb = n_ctx // T
    ns = MERGE_SAMPLES if B % MERGE_SAMPLES == 0 else 1
    tok_in = pl.BlockSpec((ns, T, D), lambda b, i: (b, i + ncb, 0))
    const = pl.BlockSpec((1, 1, D), lambda b, i: (layer, 0, 0))
    gate = lambda n: pl.BlockSpec(
        (1, 1, D), lambda b, i: ((layer * ADA_ROWS + ns * b + n) * 6 + 5, 0, 0))
    return pl.pallas_call(
        functools.partial(_final_kernel, alpha, ns),
        grid=(B // ns, (S - n_ctx) // T),
        in_specs=[tok_in, pl.BlockSpec((ns, T, SUBLANES, LANES), lambda b, i: (b, i + ncb, 0, 0)),
                  const, const] + [gate(n) for n in range(ns)],
        out_specs=pl.BlockSpec((ns, T, D), lambda b, i: (b, i, 0)),
        out_shape=jax.ShapeDtypeStruct((B, S - n_ctx, D), F32),
        compiler_params=_compiler_params(("parallel", "parallel")),
        name="final_norm",
    )(x1, moe, ln2_g.reshape(L, 1, D), ln2_b.reshape(L, 1, D), *([mods] * ns))


def _prep_weights(w_in, w_uq, w_ukv, w_router):
    L, D, _ = w_in.shape
    krs = jnp.concatenate([jnp.zeros((L, D, 64), w_in.dtype), w_in[..., 512:544],
                           jnp.zeros((L, D, 32), w_in.dtype)], axis=-1)
    w_in_p = (w_in[..., 0:512].astype(BF16), krs.astype(BF16), w_in[..., 544:].astype(BF16))

    kvr = w_ukv.reshape(L, MLA_KV_RANK, MLA_HEADS, MLA_NOPE + MLA_V)
    kn, vv = kvr[..., :MLA_NOPE], kvr[..., MLA_NOPE:]
    zk = jnp.zeros_like(kn)
    w_k = jnp.concatenate([kn, zk], axis=-1).reshape(L, MLA_KV_RANK, MLA_HEADS * LANES)
    v_even = jnp.concatenate([vv, zk], axis=-1)
    v_odd = jnp.concatenate([zk, vv], axis=-1)
    odd = (jnp.arange(MLA_HEADS) % 2 == 1)[None, None, :, None]
    w_v = jnp.where(odd, v_odd, v_even).reshape(L, MLA_KV_RANK, MLA_HEADS * LANES)
    w_kv = jnp.concatenate([w_k, w_v], axis=-1).astype(BF16)

    qr = w_uq.reshape(L, MLA_Q_RANK, MLA_HEADS, MLA_NOPE + MLA_ROPE)
    w_q = jnp.concatenate([qr, jnp.zeros((L, MLA_Q_RANK, MLA_HEADS, 32), w_uq.dtype)], axis=-1)
    w_q = w_q.reshape(L, MLA_Q_RANK, MLA_HEADS * LANES).astype(BF16)

    w_r = jnp.concatenate(
        [w_router, jnp.zeros((L, D, LANES - N_EXPERTS), w_router.dtype)], axis=-1).astype(BF16)
    return w_in_p, w_kv, w_q, w_r


def _ones_columns():
    lane = jnp.arange(LANES)
    even = (lane == 64).astype(F32)
    odd = (lane == 0).astype(F32)
    win = jnp.concatenate([even, odd, even, odd])
    mla = jnp.concatenate([even, odd] * (MLA_HEADS // 2))
    return jnp.concatenate([win, mla])[None, :]


def _rope_tables(n_ctx, n_lat):
    pos = jnp.arange(n_lat)
    rowp = (pos // GRID_W).astype(F32)
    colp = (pos % GRID_W).astype(F32)

    def pattern(rot_dim):
        nf = rot_dim // 4
        inv = ROPE_THETA ** (-jnp.arange(nf, dtype=F32) / nf)
        ar, ac = rowp[:, None] * inv, colp[:, None] * inv
        cos = jnp.concatenate([jnp.cos(ar), jnp.cos(ar), jnp.cos(ac), jnp.cos(ac)], axis=1)
        sin = jnp.concatenate([-jnp.sin(ar), jnp.sin(ar), -jnp.sin(ac), jnp.sin(ac)], axis=1)
        return cos, sin

    cw, sw = pattern(WIN_HEAD_DIM)
    cw, sw = jnp.tile(cw, (1, 2)), jnp.tile(sw, (1, 2))
    cm, sm = pattern(MLA_ROPE)
    one64, zero64 = jnp.ones((n_lat, 64), F32), jnp.zeros((n_lat, 64), F32)
    cm = jnp.concatenate([one64, cm, one64[:, :32]], axis=1)
    sm = jnp.concatenate([zero64, sm, zero64[:, :32]], axis=1)
    sq_w, sq_m = WIN_SCALE * LOG2E, MLA_SCALE * LOG2E
    lat = jnp.concatenate([cw * sq_w, sw * sq_w, cw, sw, cm * sq_m, sm * sq_m, cm, sm], axis=1)
    ones, zeros = jnp.ones((n_ctx, LANES), F32), jnp.zeros((n_ctx, LANES), F32)
    ctx = jnp.concatenate([ones * sq_w, zeros, ones, zeros, ones * sq_m, zeros, ones, zeros], axis=1)
    return jnp.concatenate([ctx, lat], axis=0)


def kernel(x, c, ctx, c_ctx, w_ada, b_ada, w_in, attn_sink, mla_q_norm, mla_kv_norm, w_uq, w_ukv,
           w_oa, w_ob, w_out, ln1_g, ln1_b, w_router, w_exp_gate, w_exp_up, w_exp_down, ln2_g, ln2_b):
    B, n_lat, D = x.shape
    n_ctx = ctx.shape[1]
    depth = w_in.shape[0]
    S = n_ctx + n_lat
    alpha = (2 * depth) ** 0.25
    assert D == D_MODEL and n_ctx == TOK_BLOCK and n_lat % TOK_BLOCK == 0 and B + 1 <= ADA_ROWS
    cap_c = CAPACITY_FACTOR * n_ctx // N_EXPERTS
    cap_l = CAPACITY_FACTOR * n_lat // N_EXPERTS
    assert cap_l % 256 == 0 and cap_c % 8 == 0
    ncb = n_ctx // TOK_BLOCK

    cond = jnp.concatenate([c, c_ctx[None], jnp.zeros((ADA_ROWS - B - 1, D), F32)], axis=0)
    mods = _ada_call(cond.T, w_ada, b_ada, B + 1).reshape(depth * ADA_ROWS * 6, 1, D)

    w_in_p, w_kv, w_q, w_r = _prep_weights(w_in, w_uq, w_ukv, w_router)
    w_oa_b, w_ob_b, w_out_b = w_oa.astype(BF16), w_ob.astype(BF16), w_out.astype(BF16)
    tab = _rope_tables(n_ctx, n_lat)
    vones = _ones_columns()

    x_all = (ctx, x)
    moe = None
    for l in range(depth):
        x_res, (qwin, kwin, vwin, qcat, kcat, vext, sg) = _proj_call(
            l, alpha, x_all, moe, ln2_g, ln2_b, mods, tab, w_in_p, mla_kv_norm, mla_q_norm,
            w_kv, w_q, vones, ncb)
        oa = _win_call(l, attn_sink, qwin, kwin, vwin, n_ctx)
        ob = _mla_call(qcat, kcat, vext, n_ctx)
        x1, h2, lg_t = _merge_call(l, alpha, x_res, oa, ob, sg, w_oa_b, w_ob_b, w_out_b,
                                   ln1_g, ln1_b, mods, w_r, ncb)
        aff, idx_rows = _route_call(lg_t, n_ctx, cap_c, cap_l)
        idx = idx_rows.reshape(B, N_EXPERTS, cap_c + cap_l)
        xg = _gather_call(idx, h2)
        moe = _moe_call(l, idx, aff, xg, w_exp_gate, w_exp_up, w_exp_down, S)
        x_all = x1
    return _final_call(depth - 1, alpha, x_all, moe, ln2_g, ln2_b, mods, n_ctx)
```

```python
import functools
import math

import jax
import jax.numpy as jnp
from jax import lax
from jax.experimental import pallas as pl
from jax.experimental.pallas import tpu as pltpu

D_MODEL = 1024
GRID_W = 64
WIN_HEADS = 8
WIN_KV_HEADS = 2
WIN_HEAD_DIM = 64
BAND = 128
MLA_HEADS = 8
MLA_Q_RANK = 384
MLA_KV_RANK = 256
MLA_NOPE = 64
MLA_ROPE = 32
MLA_V = 64
N_EXPERTS = 16
CAPACITY_FACTOR = 2
ROPE_THETA = 10000.0
LN_EPS = 1e-5
RMS_EPS = 1e-6
NEG_INF = -1e30
LOG2E = math.log2(math.e)
WIN_SCALE = WIN_HEAD_DIM ** -0.5
MLA_SCALE = (MLA_NOPE + MLA_ROPE) ** -0.5

LANES = 128
TOK_BLOCK = 256
VMEM_LIMIT = 56 * 1024 * 1024

BF16 = jnp.bfloat16
F32 = jnp.float32

SEG_KWIN = (0, 0, 128)
SEG_VWIN = (0, 128, 256)
SEG_CKV = (0, 256, 512)
SEG_KR = (1, 0, 128)
SEG_QWIN = (2, 0, 512)
SEG_CQ = (2, 512, 896)
SEG_GATE = (2, 896, 2944)


def _dot(a, b):
    return jnp.dot(a, b, preferred_element_type=F32)


def _dot_nt(a, b):
    return lax.dot_general(a, b, (((1,), (1,)), ((), ())), preferred_element_type=F32)


def _layer_norm(z, g, b):
    mu = jnp.mean(z, axis=-1, keepdims=True)
    zc = z - mu
    var = jnp.mean(zc * zc, axis=-1, keepdims=True)
    return zc * lax.rsqrt(var + LN_EPS) * g + b


def _rms_norm(x, g):
    return x * lax.rsqrt(jnp.mean(x * x, axis=-1, keepdims=True) + RMS_EPS) * g


def _rope_slab(x, cos, sin, half):
    lane = lax.broadcasted_iota(jnp.int32, x.shape, 1)
    partner = jnp.where((lane & half) == 0,
                        pltpu.roll(x, LANES - half, 1), pltpu.roll(x, half, 1))
    return x * cos + partner * sin


SUBLANES = 8


def _rows_to_tiles(y):
    n = y.shape[0]
    y3 = pltpu.einshape("r(sl)->srl", y, s=SUBLANES)
    y4 = y3.reshape(SUBLANES, n // SUBLANES, SUBLANES, LANES)
    return jnp.transpose(y4, (1, 2, 0, 3)).reshape(n, SUBLANES, LANES)


def _tiles_to_rows(x3):
    n = x3.shape[0]
    x4 = x3.reshape(n // SUBLANES, SUBLANES, SUBLANES, LANES)
    xs = jnp.transpose(x4, (2, 0, 1, 3)).reshape(SUBLANES, n, LANES)
    return pltpu.einshape("srl->r(sl)", xs)


def _compiler_params(sem):
    return pltpu.CompilerParams(dimension_semantics=sem, vmem_limit_bytes=VMEM_LIMIT)


ADA_ROWS = 8
ADA_TN = 512


def _ada_kernel(n_rows, condt_ref, w_ref, b_ref, o_ref):
    ct = condt_ref[...]
    st = ct * jax.nn.sigmoid(ct)
    w = w_ref[0]
    rows = []
    for r in range(n_rows):
        rows.append(jnp.sum(w * st[:, r:r + 1], axis=0, keepdims=True) + b_ref[0])
    rows.append(jnp.zeros((ADA_ROWS - n_rows, w.shape[1]), F32))
    o_ref[0] = jnp.concatenate(rows, axis=0)


def _ada_call(cond_t, w_ada, b_ada, n_rows):
    L, D, N = w_ada.shape
    return pl.pallas_call(
        functools.partial(_ada_kernel, n_rows),
        grid=(L, N // ADA_TN),
        in_specs=[
            pl.BlockSpec((D, ADA_ROWS), lambda l, j: (0, 0)),
            pl.BlockSpec((1, D, ADA_TN), lambda l, j: (l, 0, j)),
            pl.BlockSpec((1, 1, ADA_TN), lambda l, j: (l, 0, j)),
        ],
        out_specs=pl.BlockSpec((1, ADA_ROWS, ADA_TN), lambda l, j: (l, 0, j)),
        out_shape=jax.ShapeDtypeStruct((L, ADA_ROWS, N), F32),
        compiler_params=_compiler_params(("parallel", "parallel")),
        name="adaln_mod",
    )(cond_t, w_ada, b_ada.reshape(L, 1, N))


def _proj_kernel(has_ln, alpha, n_ctx_blocks, ns, *refs):
    refs = list(refs)
    if has_ln:
        x_ref, moe_ref, g2_ref, b2_ref = refs[:4]
        m5_refs, refs = refs[4:4 + ns], refs[4 + ns:]
    else:
        (ctx_ref, x_ref), refs = refs[:2], refs[2:]
    mod_refs, refs = refs[:2 * ns], refs[2 * ns:]
    (tab_ref, wa_ref, wb_ref, wc_ref, gkv_ref, gq_ref, wkv_ref, wq_ref, vones_ref,
     xres_ref, qwin_ref, kwin_ref, vwin_ref, qcat_ref, kcat_ref, vext_ref, sg_ref) = refs
    win_refs = (wa_ref, wb_ref, wc_ref)
    T = x_ref.shape[1]
    hbs = []
    for n in range(ns):
        if has_ln:
            z = alpha * x_ref[n] + m5_refs[n][0] * _tiles_to_rows(moe_ref[n])
            x = _layer_norm(z, g2_ref[0], b2_ref[0])
        else:
            x = jnp.where(pl.program_id(0) < n_ctx_blocks, ctx_ref[n], x_ref[n])
        xres_ref[n] = x
        sh_ref, sc_ref = mod_refs[2 * n:2 * n + 2]
        hbs.append((x * (1.0 + sc_ref[0]) + sh_ref[0]).astype(BF16))
    hb = jnp.concatenate(hbs, axis=0)
    split = lambda v: v.reshape(ns, T, v.shape[1])

    def seg(s):
        return _dot(hb, win_refs[s[0]][0, :, s[1]:s[2]])

    def tab(i):
        t = tab_ref[:, i * LANES:(i + 1) * LANES]
        return jnp.concatenate([t] * ns, axis=0)

    cq_w, sq_w, ck_w, sk_w, cq_m, sq_m, ck_m, sk_m = (tab(i) for i in range(8))

    lane_lo = lax.broadcasted_iota(jnp.int32, (hb.shape[0], LANES), 1) < WIN_HEAD_DIM
    half_swap = lambda t: pltpu.roll(t, WIN_HEAD_DIM, 1)

    ckv = _rms_norm(seg(SEG_CKV), gkv_ref[0]).astype(BF16)
    cq = _rms_norm(seg(SEG_CQ), gq_ref[0]).astype(BF16)

    qw = seg(SEG_QWIN)
    q_slabs = []
    for s2 in range(WIN_HEADS // 2):
        pair = _rope_slab(qw[:, s2 * LANES:(s2 + 1) * LANES], cq_w, sq_w, 16)
        pair_sw = half_swap(pair)
        if 2 * s2 < WIN_HEADS // WIN_KV_HEADS:
            q_slabs += [jnp.where(lane_lo, pair, 0.0), jnp.where(lane_lo, pair_sw, 0.0)]
        else:
            q_slabs += [jnp.where(lane_lo, 0.0, pair_sw), jnp.where(lane_lo, 0.0, pair)]
    qwin_ref[...] = split(jnp.concatenate(q_slabs, axis=1).astype(BF16))

    kv = _dot(ckv, wkv_ref[0])
    kr = _rope_slab(seg(SEG_KR), ck_m, sk_m, 8)
    qm = _dot(cq, wq_ref[0])
    for h in range(MLA_HEADS):
        sl = slice(h * LANES, (h + 1) * LANES)
        kcat_ref[:, :, sl] = split((kv[:, sl] + kr).astype(BF16))
        vext_ref[:, :, sl] = split(
            (kv[:, 1024 + h * LANES:1024 + (h + 1) * LANES]
             + vones_ref[:, 512 + h * LANES:512 + (h + 1) * LANES]).astype(BF16))
        qcat_ref[:, :, sl] = split(_rope_slab(qm[:, sl], cq_m, sq_m, 8).astype(BF16))

    sg_ref[...] = split(jax.nn.sigmoid(seg(SEG_GATE)).astype(BF16))
    va = seg(SEG_VWIN)
    va_sw = half_swap(va)
    vwin_ref[...] = split((jnp.concatenate(
        [jnp.where(lane_lo, va, 0.0), jnp.where(lane_lo, 0.0, va_sw),
         jnp.where(lane_lo, va_sw, 0.0), jnp.where(lane_lo, 0.0, va)], axis=1)
        + vones_ref[:, 0:512]).astype(BF16))
    kwin_ref[...] = split(_rope_slab(seg(SEG_KWIN), ck_w, sk_w, 16).astype(BF16))


def _proj_call(layer, alpha, x_all, moe, ln2_g, ln2_b, mods, tab, w_in_p, g_kvn, g_qn, w_kv, w_q,
               vones, n_ctx_blocks):
    T = TOK_BLOCK
    has_ln = moe is not None
    if has_ln:
        B, S, D = x_all.shape
    else:
        ctx_in, x_in = x_all
        B, D = x_in.shape[0], x_in.shape[2]
        S = ctx_in.shape[1] + x_in.shape[1]
    L = w_in_p[0].shape[0]
    ns = PROJ_SAMPLES if B % PROJ_SAMPLES == 0 else 1

    def mod_spec(k, lyr, n):
        def imap(i, b):
            row = jnp.where(i < n_ctx_blocks, B, ns * b + n)
            return ((lyr * ADA_ROWS + row) * 6 + k, 0, 0)
        return pl.BlockSpec((1, 1, D), imap)

    tok = lambda w: pl.BlockSpec((ns, T, w), lambda i, b: (b, i, 0))
    headed = tok(MLA_HEADS * LANES)
    const = lambda shp: pl.BlockSpec(shp, lambda i, b: (layer,) + (0,) * (len(shp) - 1),
                                     pipeline_mode=pl.Buffered(1))

    if has_ln:
        prev = pl.BlockSpec((1, 1, D), lambda i, b: (layer - 1, 0, 0))
        tiled = pl.BlockSpec((ns, T, SUBLANES, LANES), lambda i, b: (b, i, 0, 0))
        in_specs = [tok(D), tiled, prev, prev] + [mod_spec(5, layer - 1, n) for n in range(ns)]
        args = [x_all, moe, ln2_g.reshape(L, 1, D), ln2_b.reshape(L, 1, D)] + [mods] * ns
    else:
        assert ctx_in.shape[1] == n_ctx_blocks * T
        in_specs = [pl.BlockSpec((ns, T, D), lambda i, b: (b, jnp.minimum(i, n_ctx_blocks - 1), 0)),
                    pl.BlockSpec((ns, T, D), lambda i, b: (b, jnp.maximum(i - n_ctx_blocks, 0), 0))]
        args = [ctx_in, x_in]
    in_specs += [mod_spec(k, layer, n) for n in range(ns) for k in (0, 1)]
    args += [mods] * (2 * ns)
    in_specs += [
        pl.BlockSpec((T, 8 * LANES), lambda i, b: (i, 0)),
        *[const((1, D, w.shape[2])) for w in w_in_p],
        const((1, 1, MLA_KV_RANK)), const((1, 1, MLA_Q_RANK)),
        const((1, MLA_KV_RANK, 2048)), const((1, MLA_Q_RANK, 1024)),
        pl.BlockSpec((1, 1536), lambda i, b: (0, 0)),
    ]
    args += [tab, *w_in_p, g_kvn.reshape(L, 1, -1), g_qn.reshape(L, 1, -1), w_kv, w_q, vones]

    out_specs, out_shape = [tok(D)], [jax.ShapeDtypeStruct((B, S, D), F32)]
    out_specs += [tok(WIN_HEADS * LANES), tok(LANES), tok(512), headed, headed, headed, tok(2048)]
    out_shape += [jax.ShapeDtypeStruct((B, S, WIN_HEADS * LANES), BF16),
                  jax.ShapeDtypeStruct((B, S, LANES), BF16),
                  jax.ShapeDtypeStruct((B, S, 512), BF16)]
    out_shape += [jax.ShapeDtypeStruct((B, S, MLA_HEADS * LANES), BF16)] * 3
    out_shape += [jax.ShapeDtypeStruct((B, S, 2048), BF16)]

    outs = pl.pallas_call(
        functools.partial(_proj_kernel, has_ln, alpha, n_ctx_blocks, ns),
        grid=(S // T, B // ns),
        in_specs=in_specs, out_specs=out_specs, out_shape=out_shape,
        compiler_params=_compiler_params(("parallel", "parallel")),
        name="proj",
    )(*args)
    return outs[0], outs[1:]


WIN_HEAD_ORDER = (0, 2, 1, 3, 4, 6, 5, 7)
WIN_SAMPLES = 4


def _win_kernel(layer, n_ctx, n_steps, sink_ref, q_ref, kc_ref, vc_ref, kp_ref, km_ref, kn_ref,
                vp_ref, vm_ref, vn_ref, o_ref, kall_ref, vall_ref, p_ref):
    i = pl.program_id(1)
    row = lax.broadcasted_iota(jnp.int32, (BAND, LANES), 0)
    col = lax.broadcasted_iota(jnp.int32, (BAND, LANES), 1)
    lane_lo = col < 64
    ns = q_ref.shape[0]
    for n in range(ns):
        kall_ref[n, 0:n_ctx, :] = kc_ref[n]
        vall_ref[n, 0:n_ctx, :] = vc_ref[n]

    def scores(n, qb, r0, nk):
        q = q_ref[n, qb * BAND:(qb + 1) * BAND, :]
        q8 = jnp.concatenate([q[:, h * LANES:(h + 1) * LANES] for h in WIN_HEAD_ORDER], axis=0)
        return _dot_nt(q8, kall_ref[n, r0:r0 + nk, :])

    def softmax(n, qb, nk, masks, s):
        e_sink = []
        for c, h in enumerate(WIN_HEAD_ORDER):
            sc = s[c * BAND:(c + 1) * BAND, :]
            if masks:
                blocks = []
                for kb in range(nk // BAND):
                    blk = sc[:, kb * BAND:(kb + 1) * BAND]
                    blocks.append(jnp.where(masks[kb], blk, NEG_INF) if kb in masks else blk)
                sc = jnp.concatenate(blocks, axis=1)
            sink = sink_ref[layer, h] * LOG2E
            m = jnp.maximum(jnp.max(sc, axis=1, keepdims=True), sink)
            p_ref[n, qb, c * BAND:(c + 1) * BAND, 0:nk] = jnp.exp2(sc - m).astype(BF16)
            e_sink.append(jnp.exp2(sink - m))
        return e_sink

    def values(n, qb, r0, nk, e_sink):
        pairs = []
        for pi in range(4):
            acc = _dot(p_ref[n, qb, pi * 2 * BAND:(pi + 1) * 2 * BAND, 0:nk],
                       vall_ref[n, r0:r0 + nk, pi * LANES:(pi + 1) * LANES])
            es = jnp.concatenate([e_sink[2 * pi], e_sink[2 * pi + 1]], axis=0)
            den = (acc[:, 64:65] if pi % 2 == 0 else acc[:, 0:1]) + es
            pairs.append(acc / den)
        slabs = [jnp.where(lane_lo, pairs[0][:BAND], pairs[1][:BAND]),
                 jnp.where(lane_lo, pairs[0][BAND:], pairs[1][BAND:]),
                 jnp.where(lane_lo, pairs[2][:BAND], pairs[3][:BAND]),
                 jnp.where(lane_lo, pairs[2][BAND:], pairs[3][BAND:])]
        o_ref[n, qb * BAND:(qb + 1) * BAND, :] = jnp.concatenate(slabs, axis=1).astype(o_ref.dtype)

    def attend_both(args0, args1):
        (r0, nk0, m0), (r1, nk1, m1) = args0, args1
        s = [(scores(n, 0, r0, nk0), scores(n, 1, r1, nk1)) for n in range(ns)]
        e = [(softmax(n, 0, nk0, m0, s[n][0]), softmax(n, 1, nk1, m1, s[n][1])) for n in range(ns)]
        for n in range(ns):
            values(n, 0, r0, nk0, e[n][0])
            values(n, 1, r1, nk1, e[n][1])

    @pl.when(i == 0)
    def _():
        attend_both((0, n_ctx, None), (0, n_ctx, None))

    @pl.when(i > 0)
    def _():
        for n in range(ns):
            r = n_ctx
            for k_ref, v_ref, rows in ((kp_ref, vp_ref, BAND), (km_ref, vm_ref, 2 * BAND),
                                       (kn_ref, vn_ref, BAND), (kc_ref, vc_ref, n_ctx)):
                kall_ref[n, r:r + rows, :] = k_ref[n]
                vall_ref[n, r:r + rows, :] = v_ref[n]
                r += rows
        off_first = jnp.where(i == 1, 2 * LANES, 0)
        off_last = jnp.where(i == n_steps - 1, 2 * LANES, 0)
        cb = n_ctx // BAND
        attend_both((0, n_ctx + 3 * BAND, {cb: col >= row + off_first, cb + 2: col <= row}),
                    (n_ctx + BAND, n_ctx + 3 * BAND, {0: col >= row, 2: col <= row - off_last}))


def _win_call(layer, sink, qwin, kwin, vwin, n_ctx):
    B, S, _ = qwin.shape
    nb = S // BAND
    ncb = n_ctx // BAND
    assert n_ctx == 2 * BAND and nb % 2 == 0
    nk = n_ctx + 3 * BAND
    ns = WIN_SAMPLES if B % WIN_SAMPLES == 0 else 1
    one = lambda w, f: pl.BlockSpec((ns, BAND, w), lambda b, i: (b, jnp.clip(f(i), ncb, nb - 1), 0))
    two = lambda w: pl.BlockSpec((ns, 2 * BAND, w), lambda b, i: (b, i, 0))
    ctx = lambda w: pl.BlockSpec((ns, n_ctx, w), lambda b, i: (b, 0, 0))
    prev, nxt = (lambda i: 2 * i - 1), (lambda i: 2 * i + 2)
    return pl.pallas_call(
        functools.partial(_win_kernel, layer, n_ctx, nb // 2),
        grid=(B // ns, nb // 2),
        in_specs=[pl.BlockSpec(memory_space=pltpu.SMEM), two(WIN_HEADS * LANES),
                  ctx(LANES), ctx(512), one(LANES, prev), two(LANES), one(LANES, nxt),
                  one(512, prev), two(512), one(512, nxt)],
        out_specs=two(512),
        out_shape=jax.ShapeDtypeStruct((B, S, 512), BF16),
        scratch_shapes=[pltpu.VMEM((ns, 2 * n_ctx + 4 * BAND, LANES), BF16),
                        pltpu.VMEM((ns, 2 * n_ctx + 4 * BAND, 512), BF16),
                        pltpu.VMEM((ns, 2, WIN_HEADS * BAND, nk), BF16)],
        compiler_params=_compiler_params(("parallel", "parallel")),
        name="win_attn",
    )(sink, qwin, kwin, vwin, kwin, kwin, kwin, vwin, vwin, vwin)


MLA_KCHUNK = 256
MLA_HEADS_PER_STEP = 4


def _mla_kernel(n_ctx, n_keys, q_ref, k_ref, v_ref, o_ref, s_ref):
    qi = pl.program_id(2)
    tq = q_ref.shape[1]
    lane = lax.broadcasted_iota(jnp.int32, (tq, LANES), 1)

    def run(nk):
        row_max = []
        for hh in range(MLA_HEADS_PER_STEP):
            hl = slice(hh * LANES, (hh + 1) * LANES)
            q = q_ref[0, :, hl]
            mrun = None
            for c in range(nk // MLA_KCHUNK):
                s = _dot_nt(q, k_ref[0, c * MLA_KCHUNK:(c + 1) * MLA_KCHUNK, hl])
                s_ref[hh, c] = s
                mc = jnp.maximum(s[:, :LANES], s[:, LANES:])
                mrun = mc if mrun is None else jnp.maximum(mrun, mc)
            row_max.append(jnp.max(mrun, axis=1, keepdims=True))
        outs = []
        for hh in range(MLA_HEADS_PER_STEP):
            acc = jnp.zeros((tq, LANES), F32)
            for c in range(nk // MLA_KCHUNK):
                p = jnp.exp2(s_ref[hh, c] - row_max[hh]).astype(BF16)
                acc = acc + _dot(p, v_ref[0, c * MLA_KCHUNK:(c + 1) * MLA_KCHUNK,
                                          hh * LANES:(hh + 1) * LANES])
            den = acc[:, 64:65] if hh % 2 == 0 else acc[:, 0:1]
            outs.append(acc / den)
        o_ref[0] = jnp.concatenate(
            [jnp.where(lane < 64, outs[2 * j], outs[2 * j + 1]) for j in range(len(outs) // 2)],
            axis=1).astype(o_ref.dtype)

    @pl.when(qi == 0)
    def _():
        run(n_ctx)

    @pl.when(qi > 0)
    def _():
        run(n_keys)


def _mla_call(qcat, kcat, vext, n_ctx):
    B, S, _ = qcat.shape
    H = MLA_HEADS
    T = TOK_BLOCK
    assert n_ctx == T
    G = MLA_HEADS_PER_STEP
    kv = pl.BlockSpec((1, S, G * LANES), lambda b, hp, i: (b, 0, hp))
    return pl.pallas_call(
        functools.partial(_mla_kernel, n_ctx, S),
        grid=(B, H // G, S // T),
        in_specs=[pl.BlockSpec((1, T, G * LANES), lambda b, hp, i: (b, i, hp)), kv, kv],
        out_specs=pl.BlockSpec((1, T, (G // 2) * LANES), lambda b, hp, i: (b, i, hp)),
        out_shape=jax.ShapeDtypeStruct((B, S, (H // 2) * LANES), BF16),
        scratch_shapes=[pltpu.VMEM((G, S // MLA_KCHUNK, T, MLA_KCHUNK), F32)],
        compiler_params=_compiler_params(("parallel", "parallel", "arbitrary")),
        name="mla_attn",
    )(qcat, kcat, vext)


MERGE_SAMPLES = 4
PROJ_SAMPLES = 1


def _merge_kernel(alpha, ns, x_ref, oa_ref, ob_ref, sg_ref, woa_ref, wob_ref, wout_ref, g1_ref,
                  b1_ref, *refs):
    mod_refs, (wr_ref, x1_ref, h2_ref, lg_ref) = refs[:3 * ns], refs[3 * ns:]
    T, D = x_ref.shape[1], x_ref.shape[2]
    rows = lambda ref: ref[...].reshape(ns * T, ref.shape[2])
    sg = rows(sg_ref)
    t = (sg[:, :D].astype(F32) * _dot(rows(oa_ref), woa_ref[0])
         + sg[:, D:].astype(F32) * _dot(rows(ob_ref), wob_ref[0]))
    y = _dot(t.astype(BF16), wout_ref[0])
    h2s = []
    for n in range(ns):
        m2_ref, m3_ref, m4_ref = mod_refs[3 * n:3 * n + 3]
        x1 = _layer_norm(alpha * x_ref[n] + m2_ref[0] * y[n * T:(n + 1) * T],
                         g1_ref[0], b1_ref[0])
        x1_ref[n] = x1
        h2 = x1 * (1.0 + m4_ref[0]) + m3_ref[0]
        h2_ref[n] = _rows_to_tiles(h2)
        h2s.append(h2.astype(BF16))
    lg = _dot(jnp.concatenate(h2s, axis=0), wr_ref[0])
    for n in range(ns):
        lg_ref[n] = lg[n * T:(n + 1) * T].T[:N_EXPERTS]


def _merge_call(layer, alpha, x_res, oa, ob, sg, w_oa, w_ob, w_out, ln1_g, ln1_b, mods, w_r,
                n_ctx_blocks):
    B, S, D = x_res.shape
    T = TOK_BLOCK
    L = w_oa.shape[0]
    ns = MERGE_SAMPLES if B % MERGE_SAMPLES == 0 else 1

    def mod_spec(k, n):
        def imap(bp, i):
            row = jnp.where(i < n_ctx_blocks, B, ns * bp + n)
            return ((layer * ADA_ROWS + row) * 6 + k, 0, 0)
        return pl.BlockSpec((1, 1, D), imap)

    tok = lambda w: pl.BlockSpec((ns, T, w), lambda bp, i: (bp, i, 0))
    const = lambda shp: pl.BlockSpec(shp, lambda bp, i: (layer,) + (0,) * (len(shp) - 1),
                                     pipeline_mode=pl.Buffered(1))
    mod_specs = [mod_spec(k, n) for n in range(ns) for k in (2, 3, 4)]
    return pl.pallas_call(
        functools.partial(_merge_kernel, alpha, ns),
        grid=(B // ns, S // T),
        in_specs=[tok(D), tok(512), tok(512), tok(2048),
                  const((1, 512, D)), const((1, 512, D)), const((1, D, D)),
                  const((1, 1, D)), const((1, 1, D))] + mod_specs + [const((1, D, LANES))],
        out_specs=[tok(D), pl.BlockSpec((ns, T, SUBLANES, LANES), lambda bp, i: (bp, i, 0, 0)),
                   pl.BlockSpec((ns, N_EXPERTS, T), lambda bp, i: (bp, 0, i))],
        out_shape=[jax.ShapeDtypeStruct((B, S, D), F32),
                   jax.ShapeDtypeStruct((B, S, SUBLANES, LANES), F32),
                   jax.ShapeDtypeStruct((B, N_EXPERTS, S), F32)],
        compiler_params=_compiler_params(("parallel", "parallel")),
        name="merge",
    )(x_res, oa, ob, sg, w_oa, w_ob, w_out, ln1_g.reshape(L, 1, D), ln1_b.reshape(L, 1, D),
      *([mods] * (3 * ns)), w_r)


def _cumsum_lanes(x):
    n = x.shape[1]
    xb = x.astype(BF16)
    r = lax.broadcasted_iota(jnp.int32, (LANES, LANES), 0)
    c = lax.broadcasted_iota(jnp.int32, (LANES, LANES), 1)
    tri = jnp.where(r <= c, 1.0, 0.0).astype(BF16)
    tb = lax.broadcasted_iota(jnp.int32, (n, LANES), 0) // LANES
    kb = lax.broadcasted_iota(jnp.int32, (n, LANES), 1)
    before = jnp.where(tb < kb, 1.0, 0.0).astype(BF16)
    off = _dot(xb, before)
    outs = []
    for k in range(n // LANES):
        outs.append(_dot(xb[:, k * LANES:(k + 1) * LANES], tri) + off[:, k:k + 1])
    return outs


def _select_top(aff, cap):
    bits = pltpu.bitcast(aff, jnp.int32)
    thr = jnp.zeros((aff.shape[0], 1), jnp.int32)
    for bit in range(30, -1, -1):
        cand = thr | (1 << bit)
        cnt = jnp.sum(jnp.where(bits >= cand, 1.0, 0.0), axis=1, keepdims=True)
        thr = jnp.where(cnt >= cap, cand, thr)
    gt = bits > thr
    eq = jnp.where(bits == thr, 1.0, 0.0)
    need = cap - jnp.sum(jnp.where(gt, 1.0, 0.0), axis=1, keepdims=True)
    eq_rank = jnp.concatenate(_cumsum_lanes(eq), axis=1) - eq
    sel = jnp.where(jnp.logical_or(gt, jnp.logical_and(eq > 0.5, eq_rank < need)), 1.0, 0.0)
    return _cumsum_lanes(sel)


ROUTE_PAD = 1e6


def _route_kernel(n_ctx, cap_c, cap_l, lg_ref, aff_ref, idx_ref, cc_ref, cl_ref):
    affs = []
    for b in range(lg_ref.shape[0]):
        lg = lg_ref[b]
        m = jnp.max(lg, axis=0, keepdims=True)
        ex = jnp.exp(lg - m)
        affs.append(ex / jnp.sum(ex, axis=0, keepdims=True))
        aff_ref[b] = affs[-1]
    aff = jnp.concatenate(affs, axis=0)
    n_rows = aff.shape[0]
    for c_ref, blocks in ((cc_ref, _select_top(aff[:, :n_ctx], cap_c)),
                          (cl_ref, _select_top(aff[:, n_ctx:], cap_l))):
        c_ref[...] = jnp.full(c_ref.shape, ROUTE_PAD, F32)
        for k, blk in enumerate(blocks):
            for r in range(n_rows):
                c_ref[r, k:k + 1, :] = blk[r:r + 1, :]

    def slots(c_ref, e, n_blocks, width):
        cm = c_ref[e]
        jrow = lax.broadcasted_iota(jnp.int32, (1, width), 1).astype(F32)
        rows = max(n_blocks, SUBLANES)
        cend = cm[0:rows, LANES - 1:LANES]
        blk = jnp.sum(jnp.where(cend <= jrow, 1.0, 0.0), axis=0, keepdims=True)
        kio = lax.broadcasted_iota(jnp.int32, (LANES, width), 0).astype(F32)
        onehot = jnp.where(kio == blk, 1.0, 0.0).astype(BF16)
        cmt = cm.T
        hi = jnp.floor(cmt * (1.0 / 256.0))
        lo = cmt - 256.0 * hi
        cg = 256.0 * _dot(hi.astype(BF16), onehot) + _dot(lo.astype(BF16), onehot)
        inside = jnp.sum(jnp.where(cg <= jrow, 1.0, 0.0), axis=0, keepdims=True)
        return (blk * LANES + inside).astype(jnp.int32)

    def per_row(r, carry):
        n_lat_blocks = (lg_ref.shape[2] - n_ctx) // LANES
        idx_ref[r, :, 0:cap_l] = slots(cl_ref, r, n_lat_blocks, cap_l) + n_ctx
        idx_ref[r, :, cap_l:cap_l + cap_c] = slots(cc_ref, r, n_ctx // LANES, LANES)[:, 0:cap_c]
        return carry

    lax.fori_loop(0, n_rows, per_row, 0, unroll=2)


def _route_call(lg_t, n_ctx, cap_c, cap_l):
    B, E, S = lg_t.shape
    cap = cap_c + cap_l
    assert (S - n_ctx) // LANES <= LANES and cap_l % LANES == 0 and cap_c <= LANES
    whole = lambda shp: pl.BlockSpec(shp, lambda i: (0,) * len(shp))
    return pl.pallas_call(
        functools.partial(_route_kernel, n_ctx, cap_c, cap_l),
        grid=(1,),
        in_specs=[whole((B, E, S))],
        out_specs=[whole((B, E, S)), whole((B * E, 1, cap))],
        out_shape=[jax.ShapeDtypeStruct((B, E, S), F32),
                   jax.ShapeDtypeStruct((B * E, 1, cap), jnp.int32)],
        scratch_shapes=[pltpu.VMEM((B * E, LANES, LANES), F32),
                        pltpu.VMEM((B * E, LANES, LANES), F32)],
        compiler_params=_compiler_params(("arbitrary",)),
        name="route",
    )(lg_t)


ROW_BATCH = 16
MOE_FF_CHUNK = 512


GATHER_EXPERTS = 4


def _gather_kernel(idx_ref, h_ref, xg_ref, tmp_ref):
    cap = xg_ref.shape[2]
    for ge in range(GATHER_EXPERTS):
        def body(g, carry):
            for r in range(ROW_BATCH):
                j = g * ROW_BATCH + r
                tmp_ref[j] = h_ref[0, idx_ref[0, 0, ge * cap + j]]
            return carry

        lax.fori_loop(0, cap // ROW_BATCH, body, 0)
        xg_ref[0, ge] = _tiles_to_rows(tmp_ref[...]).astype(xg_ref.dtype)


def _gather_call(idx, h2t):
    B, S = h2t.shape[0], h2t.shape[1]
    D = SUBLANES * LANES
    E, cap = idx.shape[1], idx.shape[2]
    G = GATHER_EXPERTS
    assert cap % ROW_BATCH == 0 and E % G == 0
    return pl.pallas_call(
        _gather_kernel,
        grid=(B, E // G),
        in_specs=[pl.BlockSpec((1, 1, G * cap), lambda b, e: (b * (E // G) + e, 0, 0),
                               memory_space=pltpu.SMEM),
                  pl.BlockSpec((1, S, SUBLANES, LANES), lambda b, e: (b, 0, 0, 0))],
        out_specs=pl.BlockSpec((1, G, cap, D), lambda b, e: (b, e, 0, 0)),
        out_shape=jax.ShapeDtypeStruct((B, E, cap, D), BF16),
        scratch_shapes=[pltpu.VMEM((cap, SUBLANES, LANES), F32)],
        compiler_params=_compiler_params(("parallel", "arbitrary")),
        name="moe_gather",
    )(idx.reshape(B * E // G, 1, G * cap), h2t)


MOE_RING = 4
MOE_AHEAD = 3


def _moe_kernel(layer, idx_ref, aff_ref, xg_ref, wg_hbm, wu_hbm, wd_hbm, out_ref,
                y_ref, wg_buf, wu_buf, wd_buf, sem):
    b = pl.program_id(0)
    s = pl.program_id(1)
    n_exp = wg_hbm.shape[1]
    cap = y_ref.shape[1]
    n_batches = cap // ROW_BATCH
    n_chunks = wg_hbm.shape[3] // MOE_FF_CHUNK
    total_chunks = pl.num_programs(0) * n_exp * n_chunks

    def chunk_copies(g):
        e = (g // n_chunks) % n_exp
        f0 = pl.multiple_of((g % n_chunks) * MOE_FF_CHUNK, MOE_FF_CHUNK)
        slot = g % MOE_RING
        cols = pl.ds(f0, MOE_FF_CHUNK)
        return (pltpu.make_async_copy(wg_hbm.at[layer, e, :, cols], wg_buf.at[slot], sem.at[slot, 0]),
                pltpu.make_async_copy(wu_hbm.at[layer, e, :, cols], wu_buf.at[slot], sem.at[slot, 1]),
                pltpu.make_async_copy(wd_hbm.at[layer, e, cols, :], wd_buf.at[slot], sem.at[slot, 2]))

    def fetch(g):
        @pl.when(g < total_chunks)
        def _():
            for cp in chunk_copies(g):
                cp.start()

    def scatter(slot, batches):
        new = None
        for g in batches:
            rows = _rows_to_tiles(y_ref[slot, g * ROW_BATCH:(g + 1) * ROW_BATCH, :])
            toks = [idx_ref[0, 0, g * ROW_BATCH + r] for r in range(ROW_BATCH)]
            old = [out_ref[0, t] for t in toks]
            for r, t in enumerate(toks):
                new = old[r] + rows[r] * aff_ref[0, 0, t]
                out_ref[0, t] = new
        return new

    def ffn(slot, prev_slot):
        n_dots = 3 * n_chunks
        per = -(-n_batches // n_dots)
        todo = list(range(n_batches)) if prev_slot is not None else []

        def weights(w):
            take, todo[:] = todo[:per], todo[per:]
            w = w.astype(BF16)
            if not take:
                return w
            last = scatter(prev_slot, take)
            bits = pltpu.bitcast(last, jnp.uint32)
            zero = pltpu.bitcast((bits >> 16) >> 16, F32)
            zero = jnp.concatenate([zero, zero], axis=0).astype(BF16)
            return w + jnp.tile(zero, (w.shape[0] // zero.shape[0], w.shape[1] // LANES))

        x = xg_ref[0, 0]
        y = None
        for c in range(n_chunks):
            g = (b * n_exp + s) * n_chunks + c
            fetch(g + MOE_AHEAD)
            for cp in chunk_copies(g):
                cp.wait()
            ring = g % MOE_RING
            a = _dot(x, weights(wg_buf[ring]))
            u = _dot(x, weights(wu_buf[ring]))
            hmid = (a * jax.nn.sigmoid(a) * u).astype(BF16)
            part = _dot(hmid, weights(wd_buf[ring]))
            y = part if y is None else y + part
        y_ref[slot] = y

    @pl.when(s == 0)
    def _():
        @pl.when(b == 0)
        def _():
            for g in range(MOE_AHEAD):
                fetch(jnp.int32(g))

        out_ref[...] = jnp.zeros(out_ref.shape, out_ref.dtype)
        ffn(0, None)

    @pl.when(jnp.logical_and(s > 0, s < n_exp))
    def _():
        slot = s % 2
        ffn(slot, 1 - slot)

    @pl.when(s == n_exp)
    def _():
        scatter((n_exp - 1) % 2, range(n_batches))


def _moe_call(layer, idx, aff, xg, w_gate, w_up, w_down, S):
    B, E, cap, D = xg.shape
    F = w_gate.shape[3]
    cur = lambda s: jnp.minimum(s, E - 1)
    prev = lambda s: jnp.maximum(s - 1, 0)
    assert F % MOE_FF_CHUNK == 0 and MOE_AHEAD < MOE_RING
    in_hbm = pl.BlockSpec(memory_space=pl.ANY)
    return pl.pallas_call(
        functools.partial(_moe_kernel, layer),
        grid=(B, E + 1),
        in_specs=[pl.BlockSpec((1, 1, cap), lambda b, s: (b * E + prev(s), 0, 0), memory_space=pltpu.SMEM),
                  pl.BlockSpec((1, 1, S), lambda b, s: (b * E + prev(s), 0, 0), memory_space=pltpu.SMEM),
                  pl.BlockSpec((1, 1, cap, D), lambda b, s: (b, cur(s), 0, 0)),
                  in_hbm, in_hbm, in_hbm],
        out_specs=pl.BlockSpec((1, S, SUBLANES, LANES), lambda b, s: (b, 0, 0, 0),
                               pipeline_mode=pl.Buffered(1)),
        out_shape=jax.ShapeDtypeStruct((B, S, SUBLANES, LANES), F32),
        scratch_shapes=[pltpu.VMEM((2, cap, D), F32),
                        pltpu.VMEM((MOE_RING, D, MOE_FF_CHUNK), F32),
                        pltpu.VMEM((MOE_RING, D, MOE_FF_CHUNK), F32),
                        pltpu.VMEM((MOE_RING, MOE_FF_CHUNK, D), F32),
                        pltpu.SemaphoreType.DMA((MOE_RING, 3))],
        compiler_params=_compiler_params(("arbitrary", "arbitrary")),
        name="moe_ffn",
    )(idx.reshape(B * E, 1, cap), aff.reshape(B * E, 1, S), xg, w_gate, w_up, w_down)


def _final_kernel(alpha, ns, x_ref, moe_ref, g_ref, b_ref, *refs):
    o_ref = refs[ns]
    for n in range(ns):
        o_ref[n] = _layer_norm(alpha * x_ref[n] + refs[n][0] * _tiles_to_rows(moe_ref[n]),
                               g_ref[0], b_ref[0])


def _final_call(layer, alpha, x1, moe, ln2_g, ln2_b, mods, n_ctx):
    B, S, D = x1.shape
    T = TOK_BLOCK
    L = ln2_g.shape[0]
    ncb = n_ctx // T
    ns = MERGE_SAMPLES if B % MERGE_SAMPLES == 0 else 1
    tok_in = pl.BlockSpec((ns, T, D), lambda b, i: (b, i + ncb, 0))
    const = pl.BlockSpec((1, 1, D), lambda b, i: (layer, 0, 0))
    gate = lambda n: pl.BlockSpec(
        (1, 1, D), lambda b, i: ((layer * ADA_ROWS + ns * b + n) * 6 + 5, 0, 0))
    return pl.pallas_call(
        functools.partial(_final_kernel, alpha, ns),
        grid=(B // ns, (S - n_ctx) // T),
        in_specs=[tok_in, pl.BlockSpec((ns, T, SUBLANES, LANES), lambda b, i: (b, i + ncb, 0, 0)),
                  const, const] + [gate(n) for n in range(ns)],
        out_specs=pl.BlockSpec((ns, T, D), lambda b, i: (b, i, 0)),
        out_shape=jax.ShapeDtypeStruct((B, S - n_ctx, D), F32),
        compiler_params=_compiler_params(("parallel", "parallel")),
        name="final_norm",
    )(x1, moe, ln2_g.reshape(L, 1, D), ln2_b.reshape(L, 1, D), *([mods] * ns))


def _prep_weights(w_in, w_uq, w_ukv, w_router):
    L, D, _ = w_in.shape
    krs = jnp.concatenate([jnp.zeros((L, D, 64), w_in.dtype), w_in[..., 512:544],
                           jnp.zeros((L, D, 32), w_in.dtype)], axis=-1)
    w_in_p = (w_in[..., 0:512].astype(BF16), krs.astype(BF16), w_in[..., 544:].astype(BF16))

    kvr = w_ukv.reshape(L, MLA_KV_RANK, MLA_HEADS, MLA_NOPE + MLA_V)
    kn, vv = kvr[..., :MLA_NOPE], kvr[..., MLA_NOPE:]
    zk = jnp.zeros_like(kn)
    w_k = jnp.concatenate([kn, zk], axis=-1).reshape(L, MLA_KV_RANK, MLA_HEADS * LANES)
    v_even = jnp.concatenate([vv, zk], axis=-1)
    v_odd = jnp.concatenate([zk, vv], axis=-1)
    odd = (jnp.arange(MLA_HEADS) % 2 == 1)[None, None, :, None]
    w_v = jnp.where(odd, v_odd, v_even).reshape(L, MLA_KV_RANK, MLA_HEADS * LANES)
    w_kv = jnp.concatenate([w_k, w_v], axis=-1).astype(BF16)

    qr = w_uq.reshape(L, MLA_Q_RANK, MLA_HEADS, MLA_NOPE + MLA_ROPE)
    w_q = jnp.concatenate([qr, jnp.zeros((L, MLA_Q_RANK, MLA_HEADS, 32), w_uq.dtype)], axis=-1)
    w_q = w_q.reshape(L, MLA_Q_RANK, MLA_HEADS * LANES).astype(BF16)

    w_r = jnp.concatenate(
        [w_router, jnp.zeros((L, D, LANES - N_EXPERTS), w_router.dtype)], axis=-1).astype(BF16)
    return w_in_p, w_kv, w_q, w_r


def _ones_columns():
    lane = jnp.arange(LANES)
    even = (lane == 64).astype(F32)
    odd = (lane == 0).astype(F32)
    win = jnp.concatenate([even, odd, even, odd])
    mla = jnp.concatenate([even, odd] * (MLA_HEADS // 2))
    return jnp.concatenate([win, mla])[None, :]


def _rope_tables(n_ctx, n_lat):
    pos = jnp.arange(n_lat)
    rowp = (pos // GRID_W).astype(F32)
    colp = (pos % GRID_W).astype(F32)

    def pattern(rot_dim):
        nf = rot_dim // 4
        inv = ROPE_THETA ** (-jnp.arange(nf, dtype=F32) / nf)
        ar, ac = rowp[:, None] * inv, colp[:, None] * inv
        cos = jnp.concatenate([jnp.cos(ar), jnp.cos(ar), jnp.cos(ac), jnp.cos(ac)], axis=1)
        sin = jnp.concatenate([-jnp.sin(ar), jnp.sin(ar), -jnp.sin(ac), jnp.sin(ac)], axis=1)
        return cos, sin

    cw, sw = pattern(WIN_HEAD_DIM)
    cw, sw = jnp.tile(cw, (1, 2)), jnp.tile(sw, (1, 2))
    cm, sm = pattern(MLA_ROPE)
    one64, zero64 = jnp.ones((n_lat, 64), F32), jnp.zeros((n_lat, 64), F32)
    cm = jnp.concatenate([one64, cm, one64[:, :32]], axis=1)
    sm = jnp.concatenate([zero64, sm, zero64[:, :32]], axis=1)
    sq_w, sq_m = WIN_SCALE * LOG2E, MLA_SCALE * LOG2E
    lat = jnp.concatenate([cw * sq_w, sw * sq_w, cw, sw, cm * sq_m, sm * sq_m, cm, sm], axis=1)
    ones, zeros = jnp.ones((n_ctx, LANES), F32), jnp.zeros((n_ctx, LANES), F32)
    ctx = jnp.concatenate([ones * sq_w, zeros, ones, zeros, ones * sq_m, zeros, ones, zeros], axis=1)
    return jnp.concatenate([ctx, lat], axis=0)


def kernel(x, c, ctx, c_ctx, w_ada, b_ada, w_in, attn_sink, mla_q_norm, mla_kv_norm, w_uq, w_ukv,
           w_oa, w_ob, w_out, ln1_g, ln1_b, w_router, w_exp_gate, w_exp_up, w_exp_down, ln2_g, ln2_b):
    B, n_lat, D = x.shape
    n_ctx = ctx.shape[1]
    depth = w_in.shape[0]
    S = n_ctx + n_lat
    alpha = (2 * depth) ** 0.25
    assert D == D_MODEL and n_ctx == TOK_BLOCK and n_lat % TOK_BLOCK == 0 and B + 1 <= ADA_ROWS
    cap_c = CAPACITY_FACTOR * n_ctx // N_EXPERTS
    cap_l = CAPACITY_FACTOR * n_lat // N_EXPERTS
    assert cap_l % 256 == 0 and cap_c % 8 == 0
    ncb = n_ctx // TOK_BLOCK

    cond = jnp.concatenate([c, c_ctx[None], jnp.zeros((ADA_ROWS - B - 1, D), F32)], axis=0)
    mods = _ada_call(cond.T, w_ada, b_ada, B + 1).reshape(depth * ADA_ROWS * 6, 1, D)

    w_in_p, w_kv, w_q, w_r = _prep_weights(w_in, w_uq, w_ukv, w_router)
    w_oa_b, w_ob_b, w_out_b = w_oa.astype(BF16), w_ob.astype(BF16), w_out.astype(BF16)
    tab = _rope_tables(n_ctx, n_lat)
    vones = _ones_columns()

    x_all = (ctx, x)
    moe = None
    for l in range(depth):
        x_res, (qwin, kwin, vwin, qcat, kcat, vext, sg) = _proj_call(
            l, alpha, x_all, moe, ln2_g, ln2_b, mods, tab, w_in_p, mla_kv_norm, mla_q_norm,
            w_kv, w_q, vones, ncb)
        oa = _win_call(l, attn_sink, qwin, kwin, vwin, n_ctx)
        ob = _mla_call(qcat, kcat, vext, n_ctx)
        x1, h2, lg_t = _merge_call(l, alpha, x_res, oa, ob, sg, w_oa_b, w_ob_b, w_out_b,
                                   ln1_g, ln1_b, mods, w_r, ncb)
        aff, idx_rows = _route_call(lg_t, n_ctx, cap_c, cap_l)
        idx = idx_rows.reshape(B, N_EXPERTS, cap_c + cap_l)
        xg = _gather_call(idx, h2)
        moe = _moe_call(l, idx, aff, xg, w_exp_gate, w_exp_up, w_exp_down, S)
        x_all = x1
    return _final_call(depth - 1, alpha, x_all, moe, ln2_g, ln2_b, mods, n_ctx)
```

```python
import functools
import math

import jax
import jax.numpy as jnp
from jax import lax
from jax.experimental import pallas as pl
from jax.experimental.pallas import tpu as pltpu

D_MODEL = 1024
GRID_W = 64
WIN_HEADS = 8
WIN_KV_HEADS = 2
WIN_HEAD_DIM = 64
BAND = 128
MLA_HEADS = 8
MLA_Q_RANK = 384
MLA_KV_RANK = 256
MLA_NOPE = 64
MLA_ROPE = 32
MLA_V = 64
N_EXPERTS = 16
CAPACITY_FACTOR = 2
ROPE_THETA = 10000.0
LN_EPS = 1e-5
RMS_EPS = 1e-6
NEG_INF = -1e30
LOG2E = math.log2(math.e)
WIN_SCALE = WIN_HEAD_DIM ** -0.5
MLA_SCALE = (MLA_NOPE + MLA_ROPE) ** -0.5

LANES = 128
TOK_BLOCK = 256
VMEM_LIMIT = 56 * 1024 * 1024

BF16 = jnp.bfloat16
F32 = jnp.float32

SEG_KWIN = (0, 0, 128)
SEG_VWIN = (0, 128, 256)
SEG_CKV = (0, 256, 512)
SEG_KR = (1, 0, 128)
SEG_QWIN = (2, 0, 512)
SEG_CQ = (2, 512, 896)
SEG_GATE = (2, 896, 2944)


def _dot(a, b):
    return jnp.dot(a, b, preferred_element_type=F32)


def _dot_nt(a, b):
    return lax.dot_general(a, b, (((1,), (1,)), ((), ())), preferred_element_type=F32)


def _layer_norm(z, g, b):
    mu = jnp.mean(z, axis=-1, keepdims=True)
    zc = z - mu
    var = jnp.mean(zc * zc, axis=-1, keepdims=True)
    return zc * lax.rsqrt(var + LN_EPS) * g + b


def _rms_norm(x, g):
    return x * lax.rsqrt(jnp.mean(x * x, axis=-1, keepdims=True) + RMS_EPS) * g


def _rope_slab(x, cos, sin, half):
    lane = lax.broadcasted_iota(jnp.int32, x.shape, 1)
    partner = jnp.where((lane & half) == 0,
                        pltpu.roll(x, LANES - half, 1), pltpu.roll(x, half, 1))
    return x * cos + partner * sin


SUBLANES = 8


def _rows_to_tiles(y):
    n = y.shape[0]
    y3 = pltpu.einshape("r(sl)->srl", y, s=SUBLANES)
    y4 = y3.reshape(SUBLANES, n // SUBLANES, SUBLANES, LANES)
    return jnp.transpose(y4, (1, 2, 0, 3)).reshape(n, SUBLANES, LANES)


def _tiles_to_rows(x3):
    n = x3.shape[0]
    x4 = x3.reshape(n // SUBLANES, SUBLANES, SUBLANES, LANES)
    xs = jnp.transpose(x4, (2, 0, 1, 3)).reshape(SUBLANES, n, LANES)
    return pltpu.einshape("srl->r(sl)", xs)


def _compiler_params(sem):
    return pltpu.CompilerParams(dimension_semantics=sem, vmem_limit_bytes=VMEM_LIMIT)


ADA_ROWS = 8
ADA_TN = 512


def _ada_kernel(n_rows, condt_ref, w_ref, b_ref, o_ref):
    ct = condt_ref[...]
    st = ct * jax.nn.sigmoid(ct)
    w = w_ref[0]
    rows = []
    for r in range(n_rows):
        rows.append(jnp.sum(w * st[:, r:r + 1], axis=0, keepdims=True) + b_ref[0])
    rows.append(jnp.zeros((ADA_ROWS - n_rows, w.shape[1]), F32))
    o_ref[0] = jnp.concatenate(rows, axis=0)


def _ada_call(cond_t, w_ada, b_ada, n_rows):
    L, D, N = w_ada.shape
    return pl.pallas_call(
        functools.partial(_ada_kernel, n_rows),
        grid=(L, N // ADA_TN),
        in_specs=[
            pl.BlockSpec((D, ADA_ROWS), lambda l, j: (0, 0)),
            pl.BlockSpec((1, D, ADA_TN), lambda l, j: (l, 0, j)),
            pl.BlockSpec((1, 1, ADA_TN), lambda l, j: (l, 0, j)),
        ],
        out_specs=pl.BlockSpec((1, ADA_ROWS, ADA_TN), lambda l, j: (l, 0, j)),
        out_shape=jax.ShapeDtypeStruct((L, ADA_ROWS, N), F32),
        compiler_params=_compiler_params(("parallel", "parallel")),
        name="adaln_mod",
    )(cond_t, w_ada, b_ada.reshape(L, 1, N))


def _proj_kernel(has_ln, alpha, n_ctx_blocks, ns, *refs):
    refs = list(refs)
    if has_ln:
        x_ref, moe_ref, g2_ref, b2_ref = refs[:4]
        m5_refs, refs = refs[4:4 + ns], refs[4 + ns:]
    else:
        (ctx_ref, x_ref), refs = refs[:2], refs[2:]
    mod_refs, refs = refs[:2 * ns], refs[2 * ns:]
    (tab_ref, wa_ref, wb_ref, wc_ref, gkv_ref, gq_ref, wkv_ref, wq_ref, vones_ref,
     xres_ref, qwin_ref, kwin_ref, vwin_ref, qcat_ref, kcat_ref, vext_ref, sg_ref) = refs
    win_refs = (wa_ref, wb_ref, wc_ref)
    T = x_ref.shape[1]
    hbs = []
    for n in range(ns):
        if has_ln:
            z = alpha * x_ref[n] + m5_refs[n][0] * _tiles_to_rows(moe_ref[n])
            x = _layer_norm(z, g2_ref[0], b2_ref[0])
        else:
            x = jnp.where(pl.program_id(0) < n_ctx_blocks, ctx_ref[n], x_ref[n])
        xres_ref[n] = x
        sh_ref, sc_ref = mod_refs[2 * n:2 * n + 2]
        hbs.append((x * (1.0 + sc_ref[0]) + sh_ref[0]).astype(BF16))
    hb = jnp.concatenate(hbs, axis=0)
    split = lambda v: v.reshape(ns, T, v.shape[1])

    def seg(s):
        return _dot(hb, win_refs[s[0]][0, :, s[1]:s[2]])

    def tab(i):
        t = tab_ref[:, i * LANES:(i + 1) * LANES]
        return jnp.concatenate([t] * ns, axis=0)

    cq_w, sq_w, ck_w, sk_w, cq_m, sq_m, ck_m, sk_m = (tab(i) for i in range(8))

    lane_lo = lax.broadcasted_iota(jnp.int32, (hb.shape[0], LANES), 1) < WIN_HEAD_DIM
    half_swap = lambda t: pltpu.roll(t, WIN_HEAD_DIM, 1)

    ckv = _rms_norm(seg(SEG_CKV), gkv_ref[0]).astype(BF16)
    cq = _rms_norm(seg(SEG_CQ), gq_ref[0]).astype(BF16)

    qw = seg(SEG_QWIN)
    q_slabs = []
    for s2 in range(WIN_HEADS // 2):
        pair = _rope_slab(qw[:, s2 * LANES:(s2 + 1) * LANES], cq_w, sq_w, 16)
        pair_sw = half_swap(pair)
        if 2 * s2 < WIN_HEADS // WIN_KV_HEADS:
            q_slabs += [jnp.where(lane_lo, pair, 0.0), jnp.where(lane_lo, pair_sw, 0.0)]
        else:
            q_slabs += [jnp.where(lane_lo, 0.0, pair_sw), jnp.where(lane_lo, 0.0, pair)]
    qwin_ref[...] = split(jnp.concatenate(q_slabs, axis=1).astype(BF16))

    kv = _dot(ckv, wkv_ref[0])
    kr = _rope_slab(seg(SEG_KR), ck_m, sk_m, 8)
    qm = _dot(cq, wq_ref[0])
    for h in range(MLA_HEADS):
        sl = slice(h * LANES, (h + 1) * LANES)
        kcat_ref[:, :, sl] = split((kv[:, sl] + kr).astype(BF16))
        vext_ref[:, :, sl] = split(
            (kv[:, 1024 + h * LANES:1024 + (h + 1) * LANES]
             + vones_ref[:, 512 + h * LANES:512 + (h + 1) * LANES]).astype(BF16))
        qcat_ref[:, :, sl] = split(_rope_slab(qm[:, sl], cq_m, sq_m, 8).astype(BF16))

    sg_ref[...] = split(jax.nn.sigmoid(seg(SEG_GATE)).astype(BF16))
    va = seg(SEG_VWIN)
    va_sw = half_swap(va)
    vwin_ref[...] = split((jnp.concatenate(
        [jnp.where(lane_lo, va, 0.0), jnp.where(lane_lo, 0.0, va_sw),
         jnp.where(lane_lo, va_sw, 0.0), jnp.where(lane_lo, 0.0, va)], axis=1)
        + vones_ref[:, 0:512]).astype(BF16))
    kwin_ref[...] = split(_rope_slab(seg(SEG_KWIN), ck_w, sk_w, 16).astype(BF16))


def _proj_call(layer, alpha, x_all, moe, ln2_g, ln2_b, mods, tab, w_in_p, g_kvn, g_qn, w_kv, w_q,
               vones, n_ctx_blocks):
    T = TOK_BLOCK
    has_ln = moe is not None
    if has_ln:
        B, S, D = x_all.shape
    else:
        ctx_in, x_in = x_all
        B, D = x_in.shape[0], x_in.shape[2]
        S = ctx_in.shape[1] + x_in.shape[1]
    L = w_in_p[0].shape[0]
    ns = PROJ_SAMPLES if B % PROJ_SAMPLES == 0 else 1

    def mod_spec(k, lyr, n):
        def imap(i, b):
            row = jnp.where(i < n_ctx_blocks, B, ns * b + n)
            return ((lyr * ADA_ROWS + row) * 6 + k, 0, 0)
        return pl.BlockSpec((1, 1, D), imap)

    tok = lambda w: pl.BlockSpec((ns, T, w), lambda i, b: (b, i, 0))
    headed = tok(MLA_HEADS * LANES)
    const = lambda shp: pl.BlockSpec(shp, lambda i, b: (layer,) + (0,) * (len(shp) - 1),
                                     pipeline_mode=pl.Buffered(1))

    if has_ln:
        prev = pl.BlockSpec((1, 1, D), lambda i, b: (layer - 1, 0, 0))
        tiled = pl.BlockSpec((ns, T, SUBLANES, LANES), lambda i, b: (b, i, 0, 0))
        in_specs = [tok(D), tiled, prev, prev] + [mod_spec(5, layer - 1, n) for n in range(ns)]
        args = [x_all, moe, ln2_g.reshape(L, 1, D), ln2_b.reshape(L, 1, D)] + [mods] * ns
    else:
        assert ctx_in.shape[1] == n_ctx_blocks * T
        in_specs = [pl.BlockSpec((ns, T, D), lambda i, b: (b, jnp.minimum(i, n_ctx_blocks - 1), 0)),
                    pl.BlockSpec((ns, T, D), lambda i, b: (b, jnp.maximum(i - n_ctx_blocks, 0), 0))]
        args = [ctx_in, x_in]
    in_specs += [mod_spec(k, layer, n) for n in range(ns) for k in (0, 1)]
    args += [mods] * (2 * ns)
    in_specs += [
        pl.BlockSpec((T, 8 * LANES), lambda i, b: (i, 0)),
        *[const((1, D, w.shape[2])) for w in w_in_p],
        const((1, 1, MLA_KV_RANK)), const((1, 1, MLA_Q_RANK)),
        const((1, MLA_KV_RANK, 2048)), const((1, MLA_Q_RANK, 1024)),
        pl.BlockSpec((1, 1536), lambda i, b: (0, 0)),
    ]
    args += [tab, *w_in_p, g_kvn.reshape(L, 1, -1), g_qn.reshape(L, 1, -1), w_kv, w_q, vones]

    out_specs, out_shape = [tok(D)], [jax.ShapeDtypeStruct((B, S, D), F32)]
    out_specs += [tok(WIN_HEADS * LANES), tok(LANES), tok(512), headed, headed, headed, tok(2048)]
    out_shape += [jax.ShapeDtypeStruct((B, S, WIN_HEADS * LANES), BF16),
                  jax.ShapeDtypeStruct((B, S, LANES), BF16),
                  jax.ShapeDtypeStruct((B, S, 512), BF16)]
    out_shape += [jax.ShapeDtypeStruct((B, S, MLA_HEADS * LANES), BF16)] * 3
    out_shape += [jax.ShapeDtypeStruct((B, S, 2048), BF16)]

    outs = pl.pallas_call(
        functools.partial(_proj_kernel, has_ln, alpha, n_ctx_blocks, ns),
        grid=(S // T, B // ns),
        in_specs=in_specs, out_specs=out_specs, out_shape=out_shape,
        compiler_params=_compiler_params(("parallel", "parallel")),
        name="proj",
    )(*args)
    return outs[0], outs[1:]


WIN_HEAD_ORDER = (0, 2, 1, 3, 4, 6, 5, 7)
WIN_SAMPLES = 4


def _win_kernel(layer, n_ctx, n_steps, sink_ref, q_ref, kc_ref, vc_ref, kp_ref, km_ref, kn_ref,
                vp_ref, vm_ref, vn_ref, o_ref, kall_ref, vall_ref, p_ref):
    i = pl.program_id(1)
    row = lax.broadcasted_iota(jnp.int32, (BAND, LANES), 0)
    col = lax.broadcasted_iota(jnp.int32, (BAND, LANES), 1)
    lane_lo = col < 64
    ns = q_ref.shape[0]
    for n in range(ns):
        kall_ref[n, 0:n_ctx, :] = kc_ref[n]
        vall_ref[n, 0:n_ctx, :] = vc_ref[n]

    def scores(n, qb, r0, nk):
        q = q_ref[n, qb * BAND:(qb + 1) * BAND, :]
        q8 = jnp.concatenate([q[:, h * LANES:(h + 1) * LANES] for h in WIN_HEAD_ORDER], axis=0)
        return _dot_nt(q8, kall_ref[n, r0:r0 + nk, :])

    def softmax(n, qb, nk, masks, s):
        e_sink = []
        for c, h in enumerate(WIN_HEAD_ORDER):
            sc = s[c * BAND:(c + 1) * BAND, :]
            if masks:
                blocks = []
                for kb in range(nk // BAND):
                    blk = sc[:, kb * BAND:(kb + 1) * BAND]
                    blocks.append(jnp.where(masks[kb], blk, NEG_INF) if kb in masks else blk)
                sc = jnp.concatenate(blocks, axis=1)
            sink = sink_ref[layer, h] * LOG2E
            m = jnp.maximum(jnp.max(sc, axis=1, keepdims=True), sink)
            p_ref[n, qb, c * BAND:(c + 1) * BAND, 0:nk] = jnp.exp2(sc - m).astype(BF16)
            e_sink.append(jnp.exp2(sink - m))
        return e_sink

    def values(n, qb, r0, nk, e_sink):
        pairs = []
        for pi in range(4):
            acc = _dot(p_ref[n, qb, pi * 2 * BAND:(pi + 1) * 2 * BAND, 0:nk],
                       vall_ref[n, r0:r0 + nk, pi * LANES:(pi + 1) * LANES])
            es = jnp.concatenate([e_sink[2 * pi], e_sink[2 * pi + 1]], axis=0)
            den = (acc[:, 64:65] if pi % 2 == 0 else acc[:, 0:1]) + es
            pairs.append(acc / den)
        slabs = [jnp.where(lane_lo, pairs[0][:BAND], pairs[1][:BAND]),
                 jnp.where(lane_lo, pairs[0][BAND:], pairs[1][BAND:]),
                 jnp.where(lane_lo, pairs[2][:BAND], pairs[3][:BAND]),
                 jnp.where(lane_lo, pairs[2][BAND:], pairs[3][BAND:])]
        o_ref[n, qb * BAND:(qb + 1) * BAND, :] = jnp.concatenate(slabs, axis=1).astype(o_ref.dtype)

    def attend_both(args0, args1):
        (r0, nk0, m0), (r1, nk1, m1) = args0, args1
        s = [(scores(n, 0, r0, nk0), scores(n, 1, r1, nk1)) for n in range(ns)]
        e = [(softmax(n, 0, nk0, m0, s[n][0]), softmax(n, 1, nk1, m1, s[n][1])) for n in range(ns)]
        for n in range(ns):
            values(n, 0, r0, nk0, e[n][0])
            values(n, 1, r1, nk1, e[n][1])

    @pl.when(i == 0)
    def _():
        attend_both((0, n_ctx, None), (0, n_ctx, None))

    @pl.when(i > 0)
    def _():
        for n in range(ns):
            r = n_ctx
            for k_ref, v_ref, rows in ((kp_ref, vp_ref, BAND), (km_ref, vm_ref, 2 * BAND),
                                       (kn_ref, vn_ref, BAND), (kc_ref, vc_ref, n_ctx)):
                kall_ref[n, r:r + rows, :] = k_ref[n]
                vall_ref[n, r:r + rows, :] = v_ref[n]
                r += rows
        off_first = jnp.where(i == 1, 2 * LANES, 0)
        off_last = jnp.where(i == n_steps - 1, 2 * LANES, 0)
        cb = n_ctx // BAND
        attend_both((0, n_ctx + 3 * BAND, {cb: col >= row + off_first, cb + 2: col <= row}),
                    (n_ctx + BAND, n_ctx + 3 * BAND, {0: col >= row, 2: col <= row - off_last}))


def _win_call(layer, sink, qwin, kwin, vwin, n_ctx):
    B, S, _ = qwin.shape
    nb = S // BAND
    ncb = n_ctx // BAND
    assert n_ctx == 2 * BAND and nb % 2 == 0
    nk = n_ctx + 3 * BAND
    ns = WIN_SAMPLES if B % WIN_SAMPLES == 0 else 1
    one = lambda w, f: pl.BlockSpec((ns, BAND, w), lambda b, i: (b, jnp.clip(f(i), ncb, nb - 1), 0))
    two = lambda w: pl.BlockSpec((ns, 2 * BAND, w), lambda b, i: (b, i, 0))
    ctx = lambda w: pl.BlockSpec((ns, n_ctx, w), lambda b, i: (b, 0, 0))
    prev, nxt = (lambda i: 2 * i - 1), (lambda i: 2 * i + 2)
    return pl.pallas_call(
        functools.partial(_win_kernel, layer, n_ctx, nb // 2),
        grid=(B // ns, nb // 2),
        in_specs=[pl.BlockSpec(memory_space=pltpu.SMEM), two(WIN_HEADS * LANES),
                  ctx(LANES), ctx(512), one(LANES, prev), two(LANES), one(LANES, nxt),
                  one(512, prev), two(512), one(512, nxt)],
        out_specs=two(512),
        out_shape=jax.ShapeDtypeStruct((B, S, 512), BF16),
        scratch_shapes=[pltpu.VMEM((ns, 2 * n_ctx + 4 * BAND, LANES), BF16),
                        pltpu.VMEM((ns, 2 * n_ctx + 4 * BAND, 512), BF16),
                        pltpu.VMEM((ns, 2, WIN_HEADS * BAND, nk), BF16)],
        compiler_params=_compiler_params(("parallel", "parallel")),
        name="win_attn",
    )(sink, qwin, kwin, vwin, kwin, kwin, kwin, vwin, vwin, vwin)


MLA_KCHUNK = 256
MLA_HEADS_PER_STEP = 4


def _mla_kernel(n_ctx, n_keys, q_ref, k_ref, v_ref, o_ref, s_ref):
    qi = pl.program_id(2)
    tq = q_ref.shape[1]
    lane = lax.broadcasted_iota(jnp.int32, (tq, LANES), 1)

    def run(nk):
        row_max = []
        for hh in range(MLA_HEADS_PER_STEP):
            hl = slice(hh * LANES, (hh + 1) * LANES)
            q = q_ref[0, :, hl]
            mrun = None
            for c in range(nk // MLA_KCHUNK):
                s = _dot_nt(q, k_ref[0, c * MLA_KCHUNK:(c + 1) * MLA_KCHUNK, hl])
                s_ref[hh, c] = s
                mc = jnp.maximum(s[:, :LANES], s[:, LANES:])
                mrun = mc if mrun is None else jnp.maximum(mrun, mc)
            row_max.append(jnp.max(mrun, axis=1, keepdims=True))
        outs = []
        for hh in range(MLA_HEADS_PER_STEP):
            acc = jnp.zeros((tq, LANES), F32)
            for c in range(nk // MLA_KCHUNK):
                p = jnp.exp2(s_ref[hh, c] - row_max[hh]).astype(BF16)
                acc = acc + _dot(p, v_ref[0, c * MLA_KCHUNK:(c + 1) * MLA_KCHUNK,
                                          hh * LANES:(hh + 1) * LANES])
            den = acc[:, 64:65] if hh % 2 == 0 else acc[:, 0:1]
            outs.append(acc / den)
        o_ref[0] = jnp.concatenate(
            [jnp.where(lane < 64, outs[2 * j], outs[2 * j + 1]) for j in range(len(outs) // 2)],
            axis=1).astype(o_ref.dtype)

    @pl.when(qi == 0)
    def _():
        run(n_ctx)

    @pl.when(qi > 0)
    def _():
        run(n_keys)


def _mla_call(qcat, kcat, vext, n_ctx):
    B, S, _ = qcat.shape
    H = MLA_HEADS
    T = TOK_BLOCK
    assert n_ctx == T
    G = MLA_HEADS_PER_STEP
    kv = pl.BlockSpec((1, S, G * LANES), lambda b, hp, i: (b, 0, hp))
    return pl.pallas_call(
        functools.partial(_mla_kernel, n_ctx, S),
        grid=(B, H // G, S // T),
        in_specs=[pl.BlockSpec((1, T, G * LANES), lambda b, hp, i: (b, i, hp)), kv, kv],
        out_specs=pl.BlockSpec((1, T, (G // 2) * LANES), lambda b, hp, i: (b, i, hp)),
        out_shape=jax.ShapeDtypeStruct((B, S, (H // 2) * LANES), BF16),
        scratch_shapes=[pltpu.VMEM((G, S // MLA_KCHUNK, T, MLA_KCHUNK), F32)],
        compiler_params=_compiler_params(("parallel", "parallel", "arbitrary")),
        name="mla_attn",
    )(qcat, kcat, vext)


MERGE_SAMPLES = 4
PROJ_SAMPLES = 1


def _merge_kernel(alpha, ns, x_ref, oa_ref, ob_ref, sg_ref, woa_ref, wob_ref, wout_ref, g1_ref,
                  b1_ref, *refs):
    mod_refs, (wr_ref, x1_ref, h2_ref, lg_ref) = refs[:3 * ns], refs[3 * ns:]
    T, D = x_ref.shape[1], x_ref.shape[2]
    rows = lambda ref: ref[...].reshape(ns * T, ref.shape[2])
    sg = rows(sg_ref)
    t = (sg[:, :D].astype(F32) * _dot(rows(oa_ref), woa_ref[0])
         + sg[:, D:].astype(F32) * _dot(rows(ob_ref), wob_ref[0]))
    y = _dot(t.astype(BF16), wout_ref[0])
    h2s = []
    for n in range(ns):
        m2_ref, m3_ref, m4_ref = mod_refs[3 * n:3 * n + 3]
        x1 = _layer_norm(alpha * x_ref[n] + m2_ref[0] * y[n * T:(n + 1) * T],
                         g1_ref[0], b1_ref[0])
        x1_ref[n] = x1
        h2 = x1 * (1.0 + m4_ref[0]) + m3_ref[0]
        h2_ref[n] = _rows_to_tiles(h2)
        h2s.append(h2.astype(BF16))
    lg = _dot(jnp.concatenate(h2s, axis=0), wr_ref[0])
    for n in range(ns):
        lg_ref[n] = lg[n * T:(n + 1) * T].T[:N_EXPERTS]


def _merge_call(layer, alpha, x_res, oa, ob, sg, w_oa, w_ob, w_out, ln1_g, ln1_b, mods, w_r,
                n_ctx_blocks):
    B, S, D = x_res.shape
    T = TOK_BLOCK
    L = w_oa.shape[0]
    ns = MERGE_SAMPLES if B % MERGE_SAMPLES == 0 else 1

    def mod_spec(k, n):
        def imap(bp, i):
            row = jnp.where(i < n_ctx_blocks, B, ns * bp + n)
            return ((layer * ADA_ROWS + row) * 6 + k, 0, 0)
        return pl.BlockSpec((1, 1, D), imap)

    tok = lambda w: pl.BlockSpec((ns, T, w), lambda bp, i: (bp, i, 0))
    const = lambda shp: pl.BlockSpec(shp, lambda bp, i: (layer,) + (0,) * (len(shp) - 1),
                                     pipeline_mode=pl.Buffered(1))
    mod_specs = [mod_spec(k, n) for n in range(ns) for k in (2, 3, 4)]
    return pl.pallas_call(
        functools.partial(_merge_kernel, alpha, ns),
        grid=(B // ns, S // T),
        in_specs=[tok(D), tok(512), tok(512), tok(2048),
                  const((1, 512, D)), const((1, 512, D)), const((1, D, D)),
                  const((1, 1, D)), const((1, 1, D))] + mod_specs + [const((1, D, LANES))],
        out_specs=[tok(D), pl.BlockSpec((ns, T, SUBLANES, LANES), lambda bp, i: (bp, i, 0, 0)),
                   pl.BlockSpec((ns, N_EXPERTS, T), lambda bp, i: (bp, 0, i))],
        out_shape=[jax.ShapeDtypeStruct((B, S, D), F32),
                   jax.ShapeDtypeStruct((B, S, SUBLANES, LANES), F32),
                   jax.ShapeDtypeStruct((B, N_EXPERTS, S), F32)],
        compiler_params=_compiler_params(("parallel", "parallel")),
        name="merge",
    )(x_res, oa, ob, sg, w_oa, w_ob, w_out, ln1_g.reshape(L, 1, D), ln1_b.reshape(L, 1, D),
      *([mods] * (3 * ns)), w_r)


def _cumsum_lanes(x):
    n = x.shape[1]
    xb = x.astype(BF16)
    r = lax.broadcasted_iota(jnp.int32, (LANES, LANES), 0)
    c = lax.broadcasted_iota(jnp.int32, (LANES, LANES), 1)
    tri = jnp.where(r <= c, 1.0, 0.0).astype(BF16)
    tb = lax.broadcasted_iota(jnp.int32, (n, LANES), 0) // LANES
    kb = lax.broadcasted_iota(jnp.int32, (n, LANES), 1)
    before = jnp.where(tb < kb, 1.0, 0.0).astype(BF16)
    off = _dot(xb, before)
    outs = []
    for k in range(n // LANES):
        outs.append(_dot(xb[:, k * LANES:(k + 1) * LANES], tri) + off[:, k:k + 1])
    return outs


def _select_top(aff, cap):
    bits = pltpu.bitcast(aff, jnp.int32)
    thr = jnp.zeros((aff.shape[0], 1), jnp.int32)
    for bit in range(30, -1, -1):
        cand = thr | (1 << bit)
        cnt = jnp.sum(jnp.where(bits >= cand, 1.0, 0.0), axis=1, keepdims=True)
        thr = jnp.where(cnt >= cap, cand, thr)
    gt = bits > thr
    eq = jnp.where(bits == thr, 1.0, 0.0)
    need = cap - jnp.sum(jnp.where(gt, 1.0, 0.0), axis=1, keepdims=True)
    eq_rank = jnp.concatenate(_cumsum_lanes(eq), axis=1) - eq
    sel = jnp.where(jnp.logical_or(gt, jnp.logical_and(eq > 0.5, eq_rank < need)), 1.0, 0.0)
    return _cumsum_lanes(sel)


ROUTE_PAD = 1e6


def _route_kernel(n_ctx, cap_c, cap_l, lg_ref, aff_ref, idx_ref, cc_ref, cl_ref):
    affs = []
    for b in range(lg_ref.shape[0]):
        lg = lg_ref[b]
        m = jnp.max(lg, axis=0, keepdims=True)
        ex = jnp.exp(lg - m)
        affs.append(ex / jnp.sum(ex, axis=0, keepdims=True))
        aff_ref[b] = affs[-1]
    aff = jnp.concatenate(affs, axis=0)
    n_rows = aff.shape[0]
    for c_ref, blocks in ((cc_ref, _select_top(aff[:, :n_ctx], cap_c)),
                          (cl_ref, _select_top(aff[:, n_ctx:], cap_l))):
        c_ref[...] = jnp.full(c_ref.shape, ROUTE_PAD, F32)
        for k, blk in enumerate(blocks):
            for r in range(n_rows):
                c_ref[r, k:k + 1, :] = blk[r:r + 1, :]

    def slots(c_ref, e, n_blocks, width):
        cm = c_ref[e]
        jrow = lax.broadcasted_iota(jnp.int32, (1, width), 1).astype(F32)
        rows = max(n_blocks, SUBLANES)
        cend = cm[0:rows, LANES - 1:LANES]
        blk = jnp.sum(jnp.where(cend <= jrow, 1.0, 0.0), axis=0, keepdims=True)
        kio = lax.broadcasted_iota(jnp.int32, (LANES, width), 0).astype(F32)
        onehot = jnp.where(kio == blk, 1.0, 0.0).astype(BF16)
        cmt = cm.T
        hi = jnp.floor(cmt * (1.0 / 256.0))
        lo = cmt - 256.0 * hi
        cg = 256.0 * _dot(hi.astype(BF16), onehot) + _dot(lo.astype(BF16), onehot)
        inside = jnp.sum(jnp.where(cg <= jrow, 1.0, 0.0), axis=0, keepdims=True)
        return (blk * LANES + inside).astype(jnp.int32)

    def per_row(r, carry):
        n_lat_blocks = (lg_ref.shape[2] - n_ctx) // LANES
        idx_ref[r, :, 0:cap_l] = slots(cl_ref, r, n_lat_blocks, cap_l) + n_ctx
        idx_ref[r, :, cap_l:cap_l + cap_c] = slots(cc_ref, r, n_ctx // LANES, LANES)[:, 0:cap_c]
        return carry

    lax.fori_loop(0, n_rows, per_row, 0, unroll=2)


def _route_call(lg_t, n_ctx, cap_c, cap_l):
    B, E, S = lg_t.shape
    cap = cap_c + cap_l
    assert (S - n_ctx) // LANES <= LANES and cap_l % LANES == 0 and cap_c <= LANES
    whole = lambda shp: pl.BlockSpec(shp, lambda i: (0,) * len(shp))
    return pl.pallas_call(
        functools.partial(_route_kernel, n_ctx, cap_c, cap_l),
        grid=(1,),
        in_specs=[whole((B, E, S))],
        out_specs=[whole((B, E, S)), whole((B * E, 1, cap))],
        out_shape=[jax.ShapeDtypeStruct((B, E, S), F32),
                   jax.ShapeDtypeStruct((B * E, 1, cap), jnp.int32)],
        scratch_shapes=[pltpu.VMEM((B * E, LANES, LANES), F32),
                        pltpu.VMEM((B * E, LANES, LANES), F32)],
        compiler_params=_compiler_params(("arbitrary",)),
        name="route",
    )(lg_t)


ROW_BATCH = 16
MOE_FF_CHUNK = 512


GATHER_EXPERTS = 8


def _gather_kernel(idx_ref, h_ref, xg_ref, tmp_ref):
    cap = xg_ref.shape[2]
    for ge in range(GATHER_EXPERTS):
        def body(g, carry):
            for r in range(ROW_BATCH):
                j = g * ROW_BATCH + r
                tmp_ref[j] = h_ref[0, idx_ref[0, 0, ge * cap + j]]
            return carry

        lax.fori_loop(0, cap // ROW_BATCH, body, 0)
        xg_ref[0, ge] = _tiles_to_rows(tmp_ref[...]).astype(xg_ref.dtype)


def _gather_call(idx, h2t):
    B, S = h2t.shape[0], h2t.shape[1]
    D = SUBLANES * LANES
    E, cap = idx.shape[1], idx.shape[2]
    G = GATHER_EXPERTS
    assert cap % ROW_BATCH == 0 and E % G == 0
    return pl.pallas_call(
        _gather_kernel,
        grid=(B, E // G),
        in_specs=[pl.BlockSpec((1, 1, G * cap), lambda b, e: (b * (E // G) + e, 0, 0),
                               memory_space=pltpu.SMEM),
                  pl.BlockSpec((1, S, SUBLANES, LANES), lambda b, e: (b, 0, 0, 0))],
        out_specs=pl.BlockSpec((1, G, cap, D), lambda b, e: (b, e, 0, 0)),
        out_shape=jax.ShapeDtypeStruct((B, E, cap, D), BF16),
        scratch_shapes=[pltpu.VMEM((cap, SUBLANES, LANES), F32)],
        compiler_params=_compiler_params(("parallel", "arbitrary")),
        name="moe_gather",
    )(idx.reshape(B * E // G, 1, G * cap), h2t)


def _moe_kernel(idx_ref, aff_ref, xg_ref, wg_ref, wu_ref, wd_ref, out_ref, y_ref):
    s = pl.program_id(1)
    n_exp = pl.num_programs(1) - 1
    cap = y_ref.shape[1]
    n_batches = cap // ROW_BATCH

    def scatter(slot, batches):
        new = None
        for g in batches:
            rows = _rows_to_tiles(y_ref[slot, g * ROW_BATCH:(g + 1) * ROW_BATCH, :])
            toks = [idx_ref[0, 0, g * ROW_BATCH + r] for r in range(ROW_BATCH)]
            old = [out_ref[0, t] for t in toks]
            for r, t in enumerate(toks):
                new = old[r] + rows[r] * aff_ref[0, 0, t]
                out_ref[0, t] = new
        return new

    def ffn(slot, prev_slot):
        ff = wg_ref.shape[3]
        n_dots = 3 * (ff // MOE_FF_CHUNK)
        per = -(-n_batches // n_dots)
        todo = list(range(n_batches)) if prev_slot is not None else []

        def weights(w):
            take, todo[:] = todo[:per], todo[per:]
            w = w.astype(BF16)
            if not take:
                return w
            last = scatter(prev_slot, take)
            bits = pltpu.bitcast(last, jnp.uint32)
            zero = pltpu.bitcast((bits >> 16) >> 16, F32)
            zero = jnp.concatenate([zero, zero], axis=0).astype(BF16)
            return w + jnp.tile(zero, (w.shape[0] // zero.shape[0], w.shape[1] // LANES))

        x = xg_ref[0, 0]
        y = None
        for f0 in range(0, ff, MOE_FF_CHUNK):
            f1 = f0 + MOE_FF_CHUNK
            a = _dot(x, weights(wg_ref[0, 0, :, f0:f1]))
            u = _dot(x, weights(wu_ref[0, 0, :, f0:f1]))
            hmid = (a * jax.nn.sigmoid(a) * u).astype(BF16)
            part = _dot(hmid, weights(wd_ref[0, 0, f0:f1, :]))
            y = part if y is None else y + part
        y_ref[slot] = y

    @pl.when(s == 0)
    def _():
        out_ref[...] = jnp.zeros(out_ref.shape, out_ref.dtype)
        ffn(0, None)

    @pl.when(jnp.logical_and(s > 0, s < n_exp))
    def _():
        slot = s % 2
        ffn(slot, 1 - slot)

    @pl.when(s == n_exp)
    def _():
        scatter((n_exp - 1) % 2, range(n_batches))


def _moe_call(layer, idx, aff, xg, w_gate, w_up, w_down, S):
    B, E, cap, D = xg.shape
    F = w_gate.shape[3]
    cur = lambda s: jnp.minimum(s, E - 1)
    prev = lambda s: jnp.maximum(s - 1, 0)
    wspec = lambda shp: pl.BlockSpec(shp, lambda b, s: (layer, cur(s), 0, 0))
    return pl.pallas_call(
        _moe_kernel,
        grid=(B, E + 1),
        in_specs=[pl.BlockSpec((1, 1, cap), lambda b, s: (b * E + prev(s), 0, 0), memory_space=pltpu.SMEM),
                  pl.BlockSpec((1, 1, S), lambda b, s: (b * E + prev(s), 0, 0), memory_space=pltpu.SMEM),
                  pl.BlockSpec((1, 1, cap, D), lambda b, s: (b, cur(s), 0, 0)),
                  wspec((1, 1, D, F)), wspec((1, 1, D, F)), wspec((1, 1, F, D))],
        out_specs=pl.BlockSpec((1, S, SUBLANES, LANES), lambda b, s: (b, 0, 0, 0),
                               pipeline_mode=pl.Buffered(1)),
        out_shape=jax.ShapeDtypeStruct((B, S, SUBLANES, LANES), F32),
        scratch_shapes=[pltpu.VMEM((2, cap, D), F32)],
        compiler_params=_compiler_params(("parallel", "arbitrary")),
        name="moe_ffn",
    )(idx.reshape(B * E, 1, cap), aff.reshape(B * E, 1, S), xg, w_gate, w_up, w_down)


def _final_kernel(alpha, ns, x_ref, moe_ref, g_ref, b_ref, *refs):
    o_ref = refs[ns]
    for n in range(ns):
        o_ref[n] = _layer_norm(alpha * x_ref[n] + refs[n][0] * _tiles_to_rows(moe_ref[n]),
                               g_ref[0], b_ref[0])


def _final_call(layer, alpha, x1, moe, ln2_g, ln2_b, mods, n_ctx):
    B, S, D = x1.shape
    T = TOK_BLOCK
    L = ln2_g.shape[0]
    ncb = n_ctx // T
    ns = MERGE_SAMPLES if B % MERGE_SAMPLES == 0 else 1
    tok_in = pl.BlockSpec((ns, T, D), lambda b, i: (b, i + ncb, 0))
    const = pl.BlockSpec((1, 1, D), lambda b, i: (layer, 0, 0))
    gate = lambda n: pl.BlockSpec(
        (1, 1, D), lambda b, i: ((layer * ADA_ROWS + ns * b + n) * 6 + 5, 0, 0))
    return pl.pallas_call(
        functools.partial(_final_kernel, alpha, ns),
        grid=(B // ns, (S - n_ctx) // T),
        in_specs=[tok_in, pl.BlockSpec((ns, T, SUBLANES, LANES), lambda b, i: (b, i + ncb, 0, 0)),
                  const, const] + [gate(n) for n in range(ns)],
        out_specs=pl.BlockSpec((ns, T, D), lambda b, i: (b, i, 0)),
        out_shape=jax.ShapeDtypeStruct((B, S - n_ctx, D), F32),
        compiler_params=_compiler_params(("parallel", "parallel")),
        name="final_norm",
    )(x1, moe, ln2_g.reshape(L, 1, D), ln2_b.reshape(L, 1, D), *([mods] * ns))


def _prep_weights(w_in, w_uq, w_ukv, w_router):
    L, D, _ = w_in.shape
    krs = jnp.concatenate([jnp.zeros((L, D, 64), w_in.dtype), w_in[..., 512:544],
                           jnp.zeros((L, D, 32), w_in.dtype)], axis=-1)
    w_in_p = (w_in[..., 0:512].astype(BF16), krs.astype(BF16), w_in[..., 544:].astype(BF16))

    kvr = w_ukv.reshape(L, MLA_KV_RANK, MLA_HEADS, MLA_NOPE + MLA_V)
    kn, vv = kvr[..., :MLA_NOPE], kvr[..., MLA_NOPE:]
    zk = jnp.zeros_like(kn)
    w_k = jnp.concatenate([kn, zk], axis=-1).reshape(L, MLA_KV_RANK, MLA_HEADS * LANES)
    v_even = jnp.concatenate([vv, zk], axis=-1)
    v_odd = jnp.concatenate([zk, vv], axis=-1)
    odd = (jnp.arange(MLA_HEADS) % 2 == 1)[None, None, :, None]
    w_v = jnp.where(odd, v_odd, v_even).reshape(L, MLA_KV_RANK, MLA_HEADS * LANES)
    w_kv = jnp.concatenate([w_k, w_v], axis=-1).astype(BF16)

    qr = w_uq.reshape(L, MLA_Q_RANK, MLA_HEADS, MLA_NOPE + MLA_ROPE)
    w_q = jnp.concatenate([qr, jnp.zeros((L, MLA_Q_RANK, MLA_HEADS, 32), w_uq.dtype)], axis=-1)
    w_q = w_q.reshape(L, MLA_Q_RANK, MLA_HEADS * LANES).astype(BF16)

    w_r = jnp.concatenate(
        [w_router, jnp.zeros((L, D, LANES - N_EXPERTS), w_router.dtype)], axis=-1).astype(BF16)
    return w_in_p, w_kv, w_q, w_r


def _ones_columns():
    lane = jnp.arange(LANES)
    even = (lane == 64).astype(F32)
    odd = (lane == 0).astype(F32)
    win = jnp.concatenate([even, odd, even, odd])
    mla = jnp.concatenate([even, odd] * (MLA_HEADS // 2))
    return jnp.concatenate([win, mla])[None, :]


def _rope_tables(n_ctx, n_lat):
    pos = jnp.arange(n_lat)
    rowp = (pos // GRID_W).astype(F32)
    colp = (pos % GRID_W).astype(F32)

    def pattern(rot_dim):
        nf = rot_dim // 4
        inv = ROPE_THETA ** (-jnp.arange(nf, dtype=F32) / nf)
        ar, ac = rowp[:, None] * inv, colp[:, None] * inv
        cos = jnp.concatenate([jnp.cos(ar), jnp.cos(ar), jnp.cos(ac), jnp.cos(ac)], axis=1)
        sin = jnp.concatenate([-jnp.sin(ar), jnp.sin(ar), -jnp.sin(ac), jnp.sin(ac)], axis=1)
        return cos, sin

    cw, sw = pattern(WIN_HEAD_DIM)
    cw, sw = jnp.tile(cw, (1, 2)), jnp.tile(sw, (1, 2))
    cm, sm = pattern(MLA_ROPE)
    one64, zero64 = jnp.ones((n_lat, 64), F32), jnp.zeros((n_lat, 64), F32)
    cm = jnp.concatenate([one64, cm, one64[:, :32]], axis=1)
    sm = jnp.concatenate([zero64, sm, zero64[:, :32]], axis=1)
    sq_w, sq_m = WIN_SCALE * LOG2E, MLA_SCALE * LOG2E
    lat = jnp.concatenate([cw * sq_w, sw * sq_w, cw, sw, cm * sq_m, sm * sq_m, cm, sm], axis=1)
    ones, zeros = jnp.ones((n_ctx, LANES), F32), jnp.zeros((n_ctx, LANES), F32)
    ctx = jnp.concatenate([ones * sq_w, zeros, ones, zeros, ones * sq_m, zeros, ones, zeros], axis=1)
    return jnp.concatenate([ctx, lat], axis=0)


def kernel(x, c, ctx, c_ctx, w_ada, b_ada, w_in, attn_sink, mla_q_norm, mla_kv_norm, w_uq, w_ukv,
           w_oa, w_ob, w_out, ln1_g, ln1_b, w_router, w_exp_gate, w_exp_up, w_exp_down, ln2_g, ln2_b):
    B, n_lat, D = x.shape
    n_ctx = ctx.shape[1]
    depth = w_in.shape[0]
    S = n_ctx + n_lat
    alpha = (2 * depth) ** 0.25
    assert D == D_MODEL and n_ctx == TOK_BLOCK and n_lat % TOK_BLOCK == 0 and B + 1 <= ADA_ROWS
    cap_c = CAPACITY_FACTOR * n_ctx // N_EXPERTS
    cap_l = CAPACITY_FACTOR * n_lat // N_EXPERTS
    assert cap_l % 256 == 0 and cap_c % 8 == 0
    ncb = n_ctx // TOK_BLOCK

    cond = jnp.concatenate([c, c_ctx[None], jnp.zeros((ADA_ROWS - B - 1, D), F32)], axis=0)
    mods = _ada_call(cond.T, w_ada, b_ada, B + 1).reshape(depth * ADA_ROWS * 6, 1, D)

    w_in_p, w_kv, w_q, w_r = _prep_weights(w_in, w_uq, w_ukv, w_router)
    w_oa_b, w_ob_b, w_out_b = w_oa.astype(BF16), w_ob.astype(BF16), w_out.astype(BF16)
    tab = _rope_tables(n_ctx, n_lat)
    vones = _ones_columns()

    x_all = (ctx, x)
    moe = None
    for l in range(depth):
        x_res, (qwin, kwin, vwin, qcat, kcat, vext, sg) = _proj_call(
            l, alpha, x_all, moe, ln2_g, ln2_b, mods, tab, w_in_p, mla_kv_norm, mla_q_norm,
            w_kv, w_q, vones, ncb)
        oa = _win_call(l, attn_sink, qwin, kwin, vwin, n_ctx)
        ob = _mla_call(qcat, kcat, vext, n_ctx)
        x1, h2, lg_t = _merge_call(l, alpha, x_res, oa, ob, sg, w_oa_b, w_ob_b, w_out_b,
                                   ln1_g, ln1_b, mods, w_r, ncb)
        aff, idx_rows = _route_call(lg_t, n_ctx, cap_c, cap_l)
        idx = idx_rows.reshape(B, N_EXPERTS, cap_c + cap_l)
        xg = _gather_call(idx, h2)
        moe = _moe_call(l, idx, aff, xg, w_exp_gate, w_exp_up, w_exp_down, S)
        x_all = x1
    return _final_call(depth - 1, alpha, x_all, moe, ln2_g, ln2_b, mods, n_ctx)
```
